```python
import math
import jax, jax.numpy as jnp
from jax import lax
import numpy as np

D_MODEL = 1024
BATCH = 32
SEQ = 256
DEPTH = 1
DEC_BATCH = 8
DEC_SEQ = 4096
PAST_LEN = 256

GRID_W = 64
HEAD_DIM = 64
N_HEADS_A = 8
N_KV_A = 2
GROUP_A = N_HEADS_A // N_KV_A
N_HEADS_B = 8
WIDTH_A = N_HEADS_A * HEAD_DIM
WIDTH_B = N_HEADS_B * HEAD_DIM
MIX_WIDTH = WIDTH_A + WIDTH_B
KV_WIDTH_A = N_KV_A * HEAD_DIM
IN_COLS = WIDTH_A + 2 * KV_WIDTH_A + 3 * WIDTH_B
ROPE_THETA = 10000.0
Q_BLOCK = 128
NA_KH = 8
NA_KW = 16
NA_QC = 16
NA_BAND = NA_QC + NA_KW
N_EXPERTS = 64
N_GROUPS = 8
TOPK_GROUPS = 4
TOP_K = 8
D_EXPERT = 256
D_SHARED = 256
ROUTED_SCALE = 2.5
MOE_CHUNK = 128
EPS = 1e-6

kernel_name = 'hybrid_gqa_natten_moe_diffusion_step'


def rmsnorm(x, g):
    x32 = x.astype(jnp.float32)
    y = x32 * lax.rsqrt(jnp.mean(x32 * x32, axis=-1, keepdims=True) + EPS)
    return (y * g.astype(jnp.float32)).astype(x.dtype)


def adaln_params(cvec, w_mod, b_mod):
    mod = jnp.einsum('bd,de->be', jax.nn.silu(cvec), w_mod) + b_mod
    return jnp.split(mod[:, None, :], 6, axis=-1)


def modulate(x, g, shift, scale):
    return rmsnorm(x, g) * (1 + scale) + shift


def axial_rope_tables(n_tokens):
    t = jnp.arange(n_tokens)
    row = (t // GRID_W).astype(jnp.float32)
    col = (t % GRID_W).astype(jnp.float32)
    nf = HEAD_DIM // 4
    freqs = ROPE_THETA ** (-jnp.arange(nf, dtype=jnp.float32) / nf)
    ang = jnp.stack([row[:, None] * freqs, col[:, None] * freqs], axis=1)
    return jnp.cos(ang), jnp.sin(ang)


def rope_2d(x, cos, sin):
    xs = x.reshape(x.shape[:-1] + (2, 2, HEAD_DIM // 4))
    x1, x2 = xs[..., 0, :], xs[..., 1, :]
    c, s = cos.astype(x.dtype), sin.astype(x.dtype)
    o1 = x1 * c - x2 * s
    o2 = x1 * s + x2 * c
    return jnp.stack([o1, o2], axis=-2).reshape(x.shape)


def project_heads(h, w_in, qn_a, kn_a, qn_b, kn_b):
    b, s, _ = h.shape
    proj = jnp.einsum('bsd,dc->bsc', h, w_in)
    cuts = (WIDTH_A, WIDTH_A + KV_WIDTH_A, WIDTH_A + 2 * KV_WIDTH_A,
            WIDTH_A + 2 * KV_WIDTH_A + WIDTH_B, WIDTH_A + 2 * KV_WIDTH_A + 2 * WIDTH_B)
    qa, ka, va, qb, kb, vb = jnp.split(proj, cuts, axis=-1)
    qa = rmsnorm(qa.reshape(b, s, N_KV_A, GROUP_A, HEAD_DIM), qn_a).transpose(0, 2, 3, 1, 4)
    ka = rmsnorm(ka.reshape(b, s, N_KV_A, HEAD_DIM), kn_a).transpose(0, 2, 1, 3)
    va = va.reshape(b, s, N_KV_A, HEAD_DIM).transpose(0, 2, 1, 3)
    qb = rmsnorm(qb.reshape(b, s, N_HEADS_B, HEAD_DIM), qn_b).transpose(0, 2, 1, 3)
    kb = rmsnorm(kb.reshape(b, s, N_HEADS_B, HEAD_DIM), kn_b).transpose(0, 2, 1, 3)
    vb = vb.reshape(b, s, N_HEADS_B, HEAD_DIM).transpose(0, 2, 1, 3)
    return qa, ka, va, qb, kb, vb


def blocked_attention(q, k, v):
    b, hk, g, sq, hd = q.shape
    nb = sq // Q_BLOCK
    qb = jnp.moveaxis(q.reshape(b, hk, g, nb, Q_BLOCK, hd), 3, 0)
    scale = hd ** -0.5

    def one(qi):
        s = jnp.einsum('bhgqd,bhkd->bhgqk', qi, k).astype(jnp.float32) * scale
        p = jax.nn.softmax(s, axis=-1).astype(v.dtype)
        return jnp.einsum('bhgqk,bhkd->bhgqd', p, v)

    o = lax.map(one, qb)
    return jnp.moveaxis(o, 0, 3).reshape(b, hk, g, sq, hd)


def neighbourhood_tables(kh):
    ncb = GRID_W // NA_QC
    j = np.arange(ncb)
    band_start = np.clip(j * NA_QC - NA_KW // 2, 0, GRID_W - NA_BAND)
    col_idx = band_start[:, None] + np.arange(NA_BAND)
    qcol = j[:, None] * NA_QC + np.arange(NA_QC)
    win_start = np.clip(qcol - NA_KW // 2, 0, GRID_W - NA_KW)
    kc = col_idx[:, None, :]
    ws = win_start[:, :, None]
    col_mask = (kc >= ws) & (kc < ws + NA_KW)
    dc_idx = np.clip(kc - qcol[:, :, None] + NA_KW - 1, 0, 2 * NA_KW - 2)
    mask = np.broadcast_to(col_mask[:, :, None, :], (ncb, NA_QC, kh, NA_BAND)).reshape(ncb, NA_QC, kh * NA_BAND)
    return col_idx, dc_idx, mask


def neighbourhood_attention(q, k, v, k_ctx, v_ctx, rpb):
    b, h, s, hd = q.shape
    rows = s // GRID_W
    kh = min(NA_KH, rows)
    ncb = GRID_W // NA_QC
    nk_loc = kh * NA_BAND
    col_idx, dc_idx, mask = neighbourhood_tables(kh)
    mask = jnp.asarray(mask)
    qg = q.reshape(b, h, rows, GRID_W, hd)
    kg = k.reshape(b, h, rows, GRID_W, hd)
    vg = v.reshape(b, h, rows, GRID_W, hd)
    scale = hd ** -0.5

    def one_row(r):
        rs = jnp.clip(r - kh // 2, 0, rows - kh)
        k_rows = lax.dynamic_slice_in_dim(kg, rs, kh, axis=2)
        v_rows = lax.dynamic_slice_in_dim(vg, rs, kh, axis=2)
        k_blk = k_rows[:, :, :, col_idx].transpose(0, 1, 3, 2, 4, 5).reshape(b, h, ncb, nk_loc, hd)
        v_blk = v_rows[:, :, :, col_idx].transpose(0, 1, 3, 2, 4, 5).reshape(b, h, ncb, nk_loc, hd)
        q_r = lax.dynamic_index_in_dim(qg, r, axis=2, keepdims=False).reshape(b, h, ncb, NA_QC, hd)
        dr_idx = rs + jnp.arange(kh) - r + (NA_KH - 1)
        bias = rpb[:, dr_idx[:, None, None, None], dc_idx[None]]
        bias = bias.transpose(0, 2, 3, 1, 4).reshape(h, ncb, NA_QC, nk_loc).astype(jnp.float32)
        s_loc = jnp.einsum('bhnqd,bhnkd->bhnqk', q_r, k_blk).astype(jnp.float32) * scale + bias[None]
        s_loc = jnp.where(mask, s_loc, -jnp.inf)
        s_ctx = jnp.einsum('bhnqd,bhkd->bhnqk', q_r, k_ctx).astype(jnp.float32) * scale
        p = jax.nn.softmax(jnp.concatenate([s_loc, s_ctx], axis=-1), axis=-1).astype(v.dtype)
        o = (jnp.einsum('bhnqk,bhnkd->bhnqd', p[..., :nk_loc], v_blk)
             + jnp.einsum('bhnqk,bhkd->bhnqd', p[..., nk_loc:], v_ctx))
        return o.reshape(b, h, GRID_W, hd)

    o = lax.map(one_row, jnp.arange(rows))
    return o.transpose(1, 2, 0, 3, 4).reshape(b, h, s, hd)


def merge_groups(o_a, o_b, on_a, on_b, w_out):
    b, s = o_a.shape[0], o_a.shape[3]
    fa = o_a.transpose(0, 3, 1, 2, 4).reshape(b, s, WIDTH_A)
    fb = o_b.transpose(0, 2, 1, 3).reshape(b, s, WIDTH_B)
    cat = jnp.concatenate([rmsnorm(fa, on_a), rmsnorm(fb, on_b)], axis=-1)
    return jnp.einsum('bsc,cd->bsd', cat, w_out)


def moe_ffn(h, w_router, router_bias, w_gate_e, w_up_e, w_down_e, w_gate_s, w_up_s, w_down_s):
    b, s, d = h.shape
    tokens = h.reshape(-1, MOE_CHUNK, d)

    def chunk(xt):
        scores = jax.nn.sigmoid(jnp.einsum('td,de->te', xt, w_router).astype(jnp.float32))
        sel = scores + router_bias.astype(jnp.float32)
        grp = sel.reshape(MOE_CHUNK, N_GROUPS, N_EXPERTS // N_GROUPS)
        grp_score = lax.top_k(grp, 2)[0].sum(-1)
        _, gidx = lax.top_k(grp_score, TOPK_GROUPS)
        gmask = jax.nn.one_hot(gidx, N_GROUPS, dtype=jnp.float32).sum(-2) > 0
        emask = jnp.repeat(gmask, N_EXPERTS // N_GROUPS, axis=-1)
        _, eidx = lax.top_k(jnp.where(emask, sel, -jnp.inf), TOP_K)
        w = jnp.take_along_axis(scores, eidx, axis=-1)
        w = w / jnp.sum(w, axis=-1, keepdims=True) * ROUTED_SCALE
        gate = jnp.einsum('tk,tke->te', w, jax.nn.one_hot(eidx, N_EXPERTS, dtype=jnp.float32)).astype(xt.dtype)
        g = jnp.einsum('td,edf->tef', xt, w_gate_e)
        u = jnp.einsum('td,edf->tef', xt, w_up_e)
        routed = jnp.einsum('tef,efd->td', jax.nn.silu(g) * u * gate[:, :, None], w_down_e)
        shared = jnp.einsum('tf,fd->td', jax.nn.silu(xt @ w_gate_s) * (xt @ w_up_s), w_down_s)
        return routed + shared

    return lax.map(chunk, tokens).reshape(b, s, d)


def setup_inputs(seed: int = 0) -> dict:
    key = jax.random.key(seed)
    ks = jax.random.split(key, 32)
    f32 = jnp.float32

    def nrm(k, shape, scale=1.0):
        return jax.random.normal(k, shape, f32) * scale

    return {
        'x_prompt': nrm(ks[0], (BATCH, SEQ, D_MODEL)),
        'x_sample': nrm(ks[1], (DEC_BATCH, DEC_SEQ, D_MODEL)),
        'cache_k_a': nrm(ks[2], (DEC_BATCH, DEPTH, N_KV_A, PAST_LEN, HEAD_DIM)),
        'cache_v_a': nrm(ks[3], (DEC_BATCH, DEPTH, N_KV_A, PAST_LEN, HEAD_DIM)),
        'cache_k_b': nrm(ks[4], (DEC_BATCH, DEPTH, N_HEADS_B, PAST_LEN, HEAD_DIM)),
        'cache_v_b': nrm(ks[5], (DEC_BATCH, DEPTH, N_HEADS_B, PAST_LEN, HEAD_DIM)),
        'c': nrm(ks[6], (DEC_BATCH, D_MODEL)),
        'c_ctx': nrm(ks[7], (D_MODEL,)),
        'w_mod': nrm(ks[8], (DEPTH, D_MODEL, 6 * D_MODEL), 0.5 * D_MODEL ** -0.5),
        'b_mod': nrm(ks[9], (DEPTH, 6 * D_MODEL), 0.01),
        'norm1': 1.0 + nrm(ks[10], (DEPTH, D_MODEL), 0.1),
        'norm2': 1.0 + nrm(ks[11], (DEPTH, D_MODEL), 0.1),
        'w_in': nrm(ks[12], (DEPTH, D_MODEL, IN_COLS), D_MODEL ** -0.5),
        'qn_a': 1.0 + nrm(ks[13], (DEPTH, HEAD_DIM), 0.1),
        'kn_a': 1.0 + nrm(ks[14], (DEPTH, HEAD_DIM), 0.1),
        'qn_b': 1.0 + nrm(ks[15], (DEPTH, HEAD_DIM), 0.1),
        'kn_b': 1.0 + nrm(ks[16], (DEPTH, HEAD_DIM), 0.1),
        'rpb': nrm(ks[17], (DEPTH, N_HEADS_B, 2 * NA_KH - 1, 2 * NA_KW - 1), 0.1),
        'on_a': 1.0 + nrm(ks[18], (DEPTH, WIDTH_A), 0.1),
        'on_b': 1.0 + nrm(ks[19], (DEPTH, WIDTH_B), 0.1),
        'w_out': nrm(ks[20], (DEPTH, MIX_WIDTH, D_MODEL), MIX_WIDTH ** -0.5),
        'w_router': nrm(ks[21], (DEPTH, D_MODEL, N_EXPERTS), D_MODEL ** -0.5),
        'router_bias': nrm(ks[22], (DEPTH, N_EXPERTS), 0.01),
        'w_gate_e': nrm(ks[23], (DEPTH, N_EXPERTS, D_MODEL, D_EXPERT), D_MODEL ** -0.5),
        'w_up_e': nrm(ks[24], (DEPTH, N_EXPERTS, D_MODEL, D_EXPERT), D_MODEL ** -0.5),
        'w_down_e': nrm(ks[25], (DEPTH, N_EXPERTS, D_EXPERT, D_MODEL), D_EXPERT ** -0.5),
        'w_gate_s': nrm(ks[26], (DEPTH, D_MODEL, D_SHARED), D_MODEL ** -0.5),
        'w_up_s': nrm(ks[27], (DEPTH, D_MODEL, D_SHARED), D_MODEL ** -0.5),
        'w_down_s': nrm(ks[28], (DEPTH, D_SHARED, D_MODEL), D_SHARED ** -0.5),
    }


def reference(x_prompt, x_sample, cache_k_a, cache_v_a, cache_k_b, cache_v_b, c, c_ctx,
              w_mod, b_mod, norm1, norm2, w_in, qn_a, kn_a, qn_b, kn_b, rpb, on_a, on_b, w_out,
              w_router, router_bias, w_gate_e, w_up_e, w_down_e, w_gate_s, w_up_s, w_down_s):
    y_p = x_prompt
    y_s = x_sample
    cos, sin = axial_rope_tables(x_sample.shape[1])
    ka_list, va_list, kb_list, vb_list = [], [], [], []
    for l in range(DEPTH):
        moe_args = (w_router[l], router_bias[l], w_gate_e[l], w_up_e[l], w_down_e[l],
                    w_gate_s[l], w_up_s[l], w_down_s[l])
        sh1, sc1, g1, sh2, sc2, g2 = adaln_params(c_ctx[None, :], w_mod[l], b_mod[l])
        h = modulate(y_p, norm1[l], sh1, sc1)
        qa, ka, va, qb, kb, vb = project_heads(h, w_in[l], qn_a[l], kn_a[l], qn_b[l], kn_b[l])
        o_a = blocked_attention(qa, ka, va)
        o_b = blocked_attention(qb[:, :, None], kb, vb)[:, :, 0]
        y_p = y_p + g1 * merge_groups(o_a, o_b, on_a[l], on_b[l], w_out[l])
        y_p = y_p + g2 * moe_ffn(modulate(y_p, norm2[l], sh2, sc2), *moe_args)
        ka_list.append(ka)
        va_list.append(va)
        kb_list.append(kb)
        vb_list.append(vb)
        sh1, sc1, g1, sh2, sc2, g2 = adaln_params(c, w_mod[l], b_mod[l])
        h = modulate(y_s, norm1[l], sh1, sc1)
        qa, ka, va, qb, kb, vb = project_heads(h, w_in[l], qn_a[l], kn_a[l], qn_b[l], kn_b[l])
        qa = rope_2d(qa, cos, sin)
        ka = rope_2d(ka, cos, sin)
        k_all = jnp.concatenate([ka, cache_k_a[:, l].astype(ka.dtype)], axis=2)
        v_all = jnp.concatenate([va, cache_v_a[:, l].astype(va.dtype)], axis=2)
        o_a = blocked_attention(qa, k_all, v_all)
        o_b = neighbourhood_attention(qb, kb, vb, cache_k_b[:, l].astype(kb.dtype),
                                      cache_v_b[:, l].astype(vb.dtype), rpb[l])
        y_s = y_s + g1 * merge_groups(o_a, o_b, on_a[l], on_b[l], w_out[l])
        y_s = y_s + g2 * moe_ffn(modulate(y_s, norm2[l], sh2, sc2), *moe_args)
    state_k_a = jnp.stack(ka_list, axis=1)
    state_v_a = jnp.stack(va_list, axis=1)
    state_k_b = jnp.stack(kb_list, axis=1)
    state_v_b = jnp.stack(vb_list, axis=1)
    return (y_p, y_s, state_k_a, state_v_a, state_k_b, state_v_b)
```

```python
import functools

import numpy as np
import jax
import jax.numpy as jnp
from jax import lax
from jax.experimental import pallas as pl
from jax.experimental.pallas import tpu as pltpu

F32 = jnp.float32
BF16 = jnp.bfloat16

D_MODEL = 1024
HEAD_DIM = 64
N_HEADS_A = 8
N_KV_A = 2
GROUP_A = N_HEADS_A // N_KV_A
N_HEADS_B = 8
WIDTH_A = N_HEADS_A * HEAD_DIM
WIDTH_B = N_HEADS_B * HEAD_DIM
KV_WIDTH_A = N_KV_A * HEAD_DIM
IN_COLS = WIDTH_A + 2 * KV_WIDTH_A + 3 * WIDTH_B
GRID_W = 64
ROPE_THETA = 10000.0
NA_KH = 8
NA_KW = 16
N_EXPERTS = 64
N_GROUPS = 8
GROUP_SIZE = N_EXPERTS // N_GROUPS
TOPK_GROUPS = 4
TOP_K = 8
D_EXPERT = 256
D_SHARED = 256
ROUTED_SCALE = 2.5
EPS = 1e-6

LANES = 128
MXU_DIM = 256
MASKED = -1e30

COL_QA = 0
COL_KA = WIDTH_A
COL_VA = COL_KA + KV_WIDTH_A
COL_QB = COL_VA + KV_WIDTH_A
COL_KB = COL_QB + WIDTH_B
COL_VB = COL_KB + WIDTH_B

NA_QROWS = 8
NA_KROWS = 2 * NA_KH
NA_TQ = NA_QROWS * GRID_W
NA_TK = NA_KROWS * GRID_W
NA_KBLK = 256

VMEM_LIMIT = 56 * 1024 * 1024


def _cparams(sem):
    return pltpu.CompilerParams(dimension_semantics=sem, vmem_limit_bytes=VMEM_LIMIT)


def _dot(a, b):
    return jnp.dot(a, b, preferred_element_type=F32)


def _dot_nt(a, b):
    return lax.dot_general(a, b, (((1,), (1,)), ((), ())), preferred_element_type=F32)


def _sigmoid(x):
    return 1.0 / (1.0 + jnp.exp(-x))


def _rms(x):
    return x * lax.rsqrt(jnp.mean(x * x, axis=-1, keepdims=True) + EPS)


def _mod_kernel(c_ref, w_ref, b_ref, o_ref):
    c = c_ref[...]
    s = c * _sigmoid(c)
    o_ref[...] = jnp.dot(s, w_ref[...], preferred_element_type=F32,
                         precision=lax.Precision.HIGHEST) + b_ref[...]


def _adaln(cvec, w_mod, b_mod):
    rows, d = cvec.shape
    n = w_mod.shape[1]
    tn = 512
    return pl.pallas_call(
        _mod_kernel,
        out_shape=jax.ShapeDtypeStruct((rows, n), F32),
        grid=(n // tn,),
        in_specs=[pl.BlockSpec((rows, d), lambda j: (0, 0)),
                  pl.BlockSpec((d, tn), lambda j: (0, j)),
                  pl.BlockSpec((1, tn), lambda j: (0, j))],
        out_specs=pl.BlockSpec((rows, tn), lambda j: (0, j)),
        compiler_params=_cparams(("arbitrary",)),
        name="adaln_mod",
    )(cvec, w_mod, b_mod.reshape(1, n))


_PROJ_CHUNKS = (
    [(COL_QA + i * LANES, LANES, True, True) for i in range(WIDTH_A // LANES)]
    + [(COL_KA, LANES, True, True), (COL_VA, LANES, False, False)]
    + [(COL_QB + i * LANES, LANES, True, False) for i in range(WIDTH_B // LANES)]
    + [(COL_KB + i * LANES, LANES, True, False) for i in range(WIDTH_B // LANES)]
    + [(COL_VB + i * LANES, LANES, False, False) for i in range(WIDTH_B // LANES)]
)


def _proj_kernel(*refs, rope, states):
    x_ref, sh_ref, sc_ref, n1_ref, w_ref, gain_ref, seg_ref = refs[:7]
    pos = 7
    if rope:
        cos_ref, sin_ref = refs[pos:pos + 2]
        pos += 2
    out_ref = refs[pos]
    pos += 1
    if states:
        ka_ref, va_ref, kb_ref, vb_ref = refs[pos:pos + 4]

    x = x_ref[...]
    h = _rms(x) * n1_ref[...]
    h = h * (1.0 + sc_ref[0]) + sh_ref[0]
    p = _dot(h.astype(BF16), w_ref[...])
    seg = seg_ref[...]
    if rope:
        cos = cos_ref[...]
        sin = sin_ref[...]
        lane = lax.broadcasted_iota(jnp.int32, cos.shape, 1)
        first_half = (lane % (HEAD_DIM // 2)) < (HEAD_DIM // 4)

    for c0, w, normed, roped in _PROJ_CHUNKS:
        pc = p[:, c0:c0 + w]
        if normed:
            sq = pc * pc
            hi = sq.astype(BF16)
            lo = (sq - hi.astype(F32)).astype(BF16)
            ss = _dot(hi, seg) + _dot(lo, seg)
            pc = pc * lax.rsqrt(ss * (1.0 / HEAD_DIM) + EPS) * gain_ref[:, c0:c0 + w]
        if states:
            if c0 == COL_KA:
                for hh in range(N_KV_A):
                    ka_ref[0, 0, hh] = pc[:, hh * HEAD_DIM:(hh + 1) * HEAD_DIM]
            elif c0 == COL_VA:
                for hh in range(N_KV_A):
                    va_ref[0, 0, hh] = pc[:, hh * HEAD_DIM:(hh + 1) * HEAD_DIM]
            elif COL_KB <= c0 < COL_VB:
                base = (c0 - COL_KB) // HEAD_DIM
                for hh in range(LANES // HEAD_DIM):
                    kb_ref[0, 0, base + hh] = pc[:, hh * HEAD_DIM:(hh + 1) * HEAD_DIM]
            elif c0 >= COL_VB:
                base = (c0 - COL_VB) // HEAD_DIM
                for hh in range(LANES // HEAD_DIM):
                    vb_ref[0, 0, base + hh] = pc[:, hh * HEAD_DIM:(hh + 1) * HEAD_DIM]
        if rope and roped:
            partner = jnp.where(first_half,
                                pltpu.roll(pc, LANES - HEAD_DIM // 4, 1),
                                pltpu.roll(pc, HEAD_DIM // 4, 1))
            pc = pc * cos + partner * sin
        if c0 < COL_KA or COL_QB <= c0 < COL_KB:
            pc = pc * (HEAD_DIM ** -0.5)
        out_ref[:, c0:c0 + w] = pc.astype(BF16)


def _project(x2d, shift, scale, norm1, w_in_bf, gain, seg, rope_tabs, *, tm, seq, states):
    t, d = x2d.shape
    nb = shift.shape[0]
    tiles_per_batch = seq // tm
    rope = rope_tabs is not None

    def mod_map(i):
        return ((i // tiles_per_batch) if nb > 1 else 0, 0, 0)

    in_specs = [pl.BlockSpec((tm, d), lambda i: (i, 0)),
                pl.BlockSpec((1, 1, d), mod_map),
                pl.BlockSpec((1, 1, d), mod_map),
                pl.BlockSpec((1, d), lambda i: (0, 0)),
                pl.BlockSpec((d, IN_COLS), lambda i: (0, 0)),
                pl.BlockSpec((1, IN_COLS), lambda i: (0, 0)),
                pl.BlockSpec((LANES, LANES), lambda i: (0, 0))]
    args = [x2d, shift, scale, norm1, w_in_bf, gain, seg]
    if rope:
        in_specs += [pl.BlockSpec((tm, LANES), lambda i: (i % tiles_per_batch, 0))] * 2
        args += list(rope_tabs)
    out_shape = [jax.ShapeDtypeStruct((t, IN_COLS), BF16)]
    out_specs = [pl.BlockSpec((tm, IN_COLS), lambda i: (i, 0))]
    if states:
        assert tm == seq
        b = t // seq
        for nh in (N_KV_A, N_KV_A, N_HEADS_B, N_HEADS_B):
            out_shape.append(jax.ShapeDtypeStruct((b, 1, nh, seq, HEAD_DIM), F32))
            out_specs.append(pl.BlockSpec((1, 1, nh, seq, HEAD_DIM), lambda i: (i, 0, 0, 0, 0)))
    return pl.pallas_call(
        functools.partial(_proj_kernel, rope=rope, states=states),
        out_shape=out_shape,
        grid=(t // tm,),
        in_specs=in_specs,
        out_specs=out_specs,
        compiler_params=_cparams(("arbitrary",)),
        name="proj_states" if states else "proj_rope",
    )(*args)


def _lane_half(shape):
    return lax.broadcasted_iota(jnp.int32, shape, 1) // HEAD_DIM


def _keep_half(x, half):
    return jnp.where(_lane_half(x.shape) == half, x, jnp.zeros_like(x))


def _transpose_bf16(x):
    return x.astype(F32).T.astype(BF16)


def _attend(q, keys, values_t, biases):
    scores = []
    for k, b in zip(keys, biases):
        s = _dot_nt(k, q)
        if b is not None:
            s = s + b
        scores.append(s)
    m = functools.reduce(jnp.maximum, [jnp.max(s, axis=0, keepdims=True) for s in scores])
    denom = None
    out = None
    for s, vt in zip(scores, values_t):
        p = jnp.exp(s - m)
        ps = jnp.sum(p, axis=0, keepdims=True)
        po = _dot(vt, p.astype(BF16))
        denom = ps if denom is None else denom + ps
        out = po if out is None else out + po
    return out / denom


def _swap_halves(q_bf16):
    return pltpu.roll(q_bf16.astype(F32), HEAD_DIM, 1).astype(BF16)


def _gqa_heads(q_of_pair, keys_by_group, values_t):
    outs = []
    for h in range(N_HEADS_A):
        g = h // GROUP_A
        q = q_of_pair(h // 2)
        if h % 2 != g:
            q = _swap_halves(q)
        o = _attend(q, keys_by_group[g], values_t, [None] * len(values_t))
        outs.append(o[g * HEAD_DIM:(g + 1) * HEAD_DIM])
    return jnp.concatenate(outs, axis=0)


def _ctx_attn_kernel(p_ref, oa_ref, ob_ref):
    ka = p_ref[:, COL_KA:COL_KA + LANES]
    va_t = [_transpose_bf16(p_ref[:, COL_VA:COL_VA + LANES])]
    keys_by_group = [[_keep_half(ka, g)] for g in range(N_KV_A)]
    oa = _gqa_heads(lambda i: p_ref[:, COL_QA + i * LANES:COL_QA + (i + 1) * LANES],
                    keys_by_group, va_t)
    oa_ref[...] = oa.T

    outs = []
    for i in range(N_HEADS_B // 2):
        q = p_ref[:, COL_QB + i * LANES:COL_QB + (i + 1) * LANES]
        k = p_ref[:, COL_KB + i * LANES:COL_KB + (i + 1) * LANES]
        vt = [_transpose_bf16(p_ref[:, COL_VB + i * LANES:COL_VB + (i + 1) * LANES])]
        for half in range(2):
            o = _attend(q, [_keep_half(k, half)], vt, [None])
            outs.append(o[half * HEAD_DIM:(half + 1) * HEAD_DIM])
    ob_ref[...] = jnp.concatenate(outs, axis=0).T


def _context_attention(proj, *, seq):
    t = proj.shape[0]
    return pl.pallas_call(
        _ctx_attn_kernel,
        out_shape=[jax.ShapeDtypeStruct((t, WIDTH_A), F32), jax.ShapeDtypeStruct((t, WIDTH_B), F32)],
        grid=(t // seq,),
        in_specs=[pl.BlockSpec((seq, IN_COLS), lambda i: (i, 0))],
        out_specs=[pl.BlockSpec((seq, WIDTH_A), lambda i: (i, 0)),
                   pl.BlockSpec((seq, WIDTH_B), lambda i: (i, 0))],
        compiler_params=_cparams(("arbitrary",)),
        name="context_attention",
    )(proj)


def _gqa_latent_kernel(q_ref, k_ref, v_ref, ck_ref, cv_ref, o_ref, kg_ref, ckg_ref, vt_ref, cvt_ref):
    @pl.when(pl.program_id(1) == 0)
    def _():
        k = k_ref[...]
        ck = ck_ref[0]
        for g in range(N_KV_A):
            kg_ref[g] = _keep_half(k, g)
            ckg_ref[g] = _keep_half(ck, g)
        vt_ref[...] = _transpose_bf16(v_ref[...])
        cvt_ref[...] = _transpose_bf16(cv_ref[0])

    keys_by_group = [[kg_ref[g], ckg_ref[g]] for g in range(N_KV_A)]
    values_t = [vt_ref[...], cvt_ref[...]]
    oa = _gqa_heads(lambda i: q_ref[:, i * LANES:(i + 1) * LANES], keys_by_group, values_t)
    o_ref[...] = oa.T


def _latent_gqa(proj, ctx_k, ctx_v, *, seq, tq):
    t = proj.shape[0]
    b = t // seq
    nq = seq // tq
    past = ctx_k.shape[1]
    return pl.pallas_call(
        _gqa_latent_kernel,
        out_shape=jax.ShapeDtypeStruct((t, WIDTH_A), F32),
        grid=(b, nq),
        in_specs=[pl.BlockSpec((tq, WIDTH_A), lambda bi, qi: (bi * nq + qi, 0)),
                  pl.BlockSpec((seq, LANES), lambda bi, qi: (bi, COL_KA // LANES)),
                  pl.BlockSpec((seq, LANES), lambda bi, qi: (bi, COL_VA // LANES)),
                  pl.BlockSpec((1, past, LANES), lambda bi, qi: (bi, 0, 0)),
                  pl.BlockSpec((1, past, LANES), lambda bi, qi: (bi, 0, 0))],
        out_specs=pl.BlockSpec((tq, WIDTH_A), lambda bi, qi: (bi * nq + qi, 0)),
        scratch_shapes=[pltpu.VMEM((N_KV_A, seq, LANES), BF16),
                        pltpu.VMEM((N_KV_A, past, LANES), BF16),
                        pltpu.VMEM((LANES, seq), BF16),
                        pltpu.VMEM((LANES, past), BF16)],
        compiler_params=_cparams(("arbitrary", "arbitrary")),
        name="latent_gqa",
    )(proj, proj, proj, ctx_k, ctx_v)


def _na_kernel(q_ref, k0, k1, k2, k3, v0, v1, v2, v3, ck_ref, cv_ref, bias_ref, o_ref):
    q = q_ref[...]
    ks = [r[...] for r in (k0, k1, k2, k3)] + [ck_ref[0]]
    values_t = [_transpose_bf16(r[...]) for r in (v0, v1, v2, v3)] + [_transpose_bf16(cv_ref[0])]
    outs = []
    for half in range(2):
        keys = [_keep_half(k, half) for k in ks]
        biases = [bias_ref[0, half, j * NA_KBLK:(j + 1) * NA_KBLK, :] for j in range(4)] + [None]
        o = _attend(q, keys, values_t, biases)
        outs.append(o[half * HEAD_DIM:(half + 1) * HEAD_DIM])
    o_ref[...] = jnp.concatenate(outs, axis=0).T


def _na_first_key_block(i, rows):
    per_qblock = NA_QROWS * GRID_W // NA_KBLK
    lead = (NA_KH // 2) * GRID_W // NA_KBLK
    return jnp.clip(per_qblock * i - lead, 0, (rows - NA_KROWS) * GRID_W // NA_KBLK)


def _latent_neighbourhood(proj, ctx_k, ctx_v, bias_t, *, seq):
    t = proj.shape[0]
    b = t // seq
    rows = seq // GRID_W
    nblk = rows // NA_QROWS
    kblk_per_batch = seq // NA_KBLK
    past = ctx_k.shape[1]
    grid = (N_HEADS_B // 2, nblk, b)

    def kv_spec(col0, j):
        return pl.BlockSpec(
            (NA_KBLK, LANES),
            lambda hp, i, bi: (bi * kblk_per_batch + _na_first_key_block(i, rows) + j, col0 // LANES + hp))

    def variant(i):
        return jnp.where(i == 0, 0, jnp.where(i == nblk - 1, 2, 1))

    in_specs = ([pl.BlockSpec((NA_TQ, LANES), lambda hp, i, bi: (bi * nblk + i, COL_QB // LANES + hp))]
                + [kv_spec(COL_KB, j) for j in range(4)]
                + [kv_spec(COL_VB, j) for j in range(4)]
                + [pl.BlockSpec((1, past, LANES), lambda hp, i, bi: (bi, 0, hp)),
                   pl.BlockSpec((1, past, LANES), lambda hp, i, bi: (bi, 0, hp)),
                   pl.BlockSpec((1, 2, NA_TK, NA_TQ), lambda hp, i, bi: (variant(i), hp, 0, 0))])
    return pl.pallas_call(
        _na_kernel,
        out_shape=jax.ShapeDtypeStruct((t, WIDTH_B), F32),
        grid=grid,
        in_specs=in_specs,
        out_specs=pl.BlockSpec((NA_TQ, LANES), lambda hp, i, bi: (bi * nblk + i, hp)),
        compiler_params=_cparams(("arbitrary", "arbitrary", "arbitrary")),
        name="latent_neighbourhood",
    )(proj, *([proj] * 8), ctx_k, ctx_v, bias_t)


def _neighbourhood_bias(rpb, rows):
    nblk = rows // NA_QROWS
    dr_l, dc_l, ok_l = [], [], []
    for i in (0, 1, nblk - 1):
        r0 = i * NA_QROWS
        ks = int(np.clip(r0 - NA_KH // 2, 0, rows - NA_KROWS))
        qr = r0 + np.arange(NA_QROWS)[:, None]
        qc = np.arange(GRID_W)[None, :]
        qr = np.broadcast_to(qr, (NA_QROWS, GRID_W)).reshape(-1)
        qc = np.broadcast_to(qc, (NA_QROWS, GRID_W)).reshape(-1)
        kr = np.broadcast_to(ks + np.arange(NA_KROWS)[:, None], (NA_KROWS, GRID_W)).reshape(-1)
        kc = np.broadcast_to(np.arange(GRID_W)[None, :], (NA_KROWS, GRID_W)).reshape(-1)
        rs = np.clip(qr - NA_KH // 2, 0, rows - NA_KH)
        ws = np.clip(qc - NA_KW // 2, 0, GRID_W - NA_KW)
        row_ok = (kr[:, None] >= rs[None, :]) & (kr[:, None] < rs[None, :] + NA_KH)
        col_ok = (kc[:, None] >= ws[None, :]) & (kc[:, None] < ws[None, :] + NA_KW)
        dr = np.clip(kr[:, None] - qr[None, :] + NA_KH - 1, 0, 2 * NA_KH - 2)
        dc = np.clip(kc[:, None] - qc[None, :] + NA_KW - 1, 0, 2 * NA_KW - 2)
        dr_l.append(dr)
        dc_l.append(dc)
        ok_l.append(row_ok & col_ok)
    dr = jnp.asarray(np.stack(dr_l), jnp.int32)
    dc = jnp.asarray(np.stack(dc_l), jnp.int32)
    ok = jnp.asarray(np.stack(ok_l))
    gathered = rpb[:, dr, dc]
    return jnp.where(ok[None], gathered, MASKED).transpose(1, 0, 2, 3).astype(F32)


def _merge_kernel(x_ref, oa_ref, ob_ref, ona_ref, onb_ref, wo_ref, g1_ref, sh2_ref, sc2_ref, n2_ref,
                  y_ref, h_ref):
    na = (_rms(oa_ref[...]) * ona_ref[...]).astype(BF16)
    nb = (_rms(ob_ref[...]) * onb_ref[...]).astype(BF16)
    mix = _dot(na, wo_ref[0:WIDTH_A, :]) + _dot(nb, wo_ref[WIDTH_A:WIDTH_A + WIDTH_B, :])
    y = x_ref[...] + g1_ref[0] * mix
    y_ref[...] = y
    h = _rms(y) * n2_ref[...]
    h_ref[...] = (h * (1.0 + sc2_ref[0]) + sh2_ref[0]).astype(BF16)


def _merge(x2d, oa, ob, on_a, on_b, w_out_bf, gate1, shift2, scale2, norm2, *, tm, seq):
    t, d = x2d.shape
    nb = gate1.shape[0]
    tiles_per_batch = seq // tm

    def mod_map(i):
        return ((i // tiles_per_batch) if nb > 1 else 0, 0, 0)

    return pl.pallas_call(
        _merge_kernel,
        out_shape=[jax.ShapeDtypeStruct((t, d), F32), jax.ShapeDtypeStruct((t, d), BF16)],
        grid=(t // tm,),
        in_specs=[pl.BlockSpec((tm, d), lambda i: (i, 0)),
                  pl.BlockSpec((tm, WIDTH_A), lambda i: (i, 0)),
                  pl.BlockSpec((tm, WIDTH_B), lambda i: (i, 0)),
                  pl.BlockSpec((1, WIDTH_A), lambda i: (0, 0)),
                  pl.BlockSpec((1, WIDTH_B), lambda i: (0, 0)),
                  pl.BlockSpec((WIDTH_A + WIDTH_B, d), lambda i: (0, 0)),
                  pl.BlockSpec((1, 1, d), mod_map),
                  pl.BlockSpec((1, 1, d), mod_map),
                  pl.BlockSpec((1, 1, d), mod_map),
                  pl.BlockSpec((1, d), lambda i: (0, 0))],
        out_specs=[pl.BlockSpec((tm, d), lambda i: (i, 0)),
                   pl.BlockSpec((tm, d), lambda i: (i, 0))],
        compiler_params=_cparams(("arbitrary",)),
        name="merge_out_proj",
    )(x2d, oa, ob, on_a, on_b, w_out_bf, gate1, shift2, scale2, norm2)


def _first_index_of_max(x, iota):
    mx = jnp.max(x, axis=0, keepdims=True)
    idx = jnp.min(jnp.where(x == mx, iota, float(x.shape[0])), axis=0, keepdims=True)
    return mx, iota == idx


def _router_gates(h_bf, wr_hi, wr_lo, rbias):
    logits = _dot_nt(wr_hi, h_bf) + _dot_nt(wr_lo, h_bf)
    scores = _sigmoid(logits)
    sel = scores + rbias
    tm = sel.shape[1]
    iota_g = lax.broadcasted_iota(jnp.int32, (GROUP_SIZE, tm), 0).astype(F32)
    group_scores = []
    for g in range(N_GROUPS):
        grp = sel[g * GROUP_SIZE:(g + 1) * GROUP_SIZE]
        m1, first = _first_index_of_max(grp, iota_g)
        m2 = jnp.max(jnp.where(first, -jnp.inf, grp), axis=0, keepdims=True)
        group_scores.append(m1 + m2)
    gs = jnp.concatenate(group_scores, axis=0)
    iota_n = lax.broadcasted_iota(jnp.int32, (N_GROUPS, tm), 0).astype(F32)
    group_on = jnp.zeros((N_GROUPS, tm), F32)
    for _ in range(TOPK_GROUPS):
        _, pick = _first_index_of_max(gs, iota_n)
        group_on = jnp.where(pick, 1.0, group_on)
        gs = jnp.where(pick, -jnp.inf, gs)
    expert_on = jnp.concatenate(
        [jnp.broadcast_to(group_on[g:g + 1], (GROUP_SIZE, tm)) for g in range(N_GROUPS)], axis=0)
    cand = jnp.where(expert_on > 0.0, sel, -jnp.inf)
    iota_e = lax.broadcasted_iota(jnp.int32, (N_EXPERTS, tm), 0).astype(F32)
    w = jnp.zeros((N_EXPERTS, tm), F32)
    for _ in range(TOP_K):
        _, pick = _first_index_of_max(cand, iota_e)
        w = jnp.where(pick, scores, w)
        cand = jnp.where(pick, -jnp.inf, cand)
    return w / jnp.sum(w, axis=0, keepdims=True) * ROUTED_SCALE


def _moe_kernel(h_ref, y_ref, g2_ref, wrh_ref, wrl_ref, rb_ref, wgu_ref, wd_ref, wgs_ref, wus_ref, wds_ref,
                o_ref, gate_ref, acc_ref, *, experts_per_step):
    j = pl.program_id(1)
    h = h_ref[...]

    @pl.when(j == 0)
    def _():
        gates_t = _router_gates(h, wrh_ref[...], wrl_ref[...], rb_ref[...])
        pad = jnp.zeros((LANES - N_EXPERTS, gates_t.shape[1]), F32)
        gate_ref[...] = jnp.concatenate([gates_t, pad], axis=0).T
        gs = _dot(h, wgs_ref[...])
        us = _dot(h, wus_ref[...])
        act = (gs * _sigmoid(gs)) * us
        acc_ref[...] = _dot(act.astype(BF16), wds_ref[...])

    gate = gate_ref[...]
    lane = lax.broadcasted_iota(jnp.int32, gate.shape, 1)
    total = None
    for jj in range(experts_per_step):
        e = j * experts_per_step + jj
        ge = jnp.sum(jnp.where(lane == e, gate, 0.0), axis=1, keepdims=True)
        gu = _dot(h, wgu_ref[jj])
        g = gu[:, :D_EXPERT]
        u = gu[:, D_EXPERT:]
        act = ((g * _sigmoid(g)) * u) * ge
        part = _dot(act.astype(BF16), wd_ref[jj])
        total = part if total is None else total + part
    acc_ref[...] += total

    @pl.when(j == pl.num_programs(1) - 1)
    def _():
        o_ref[...] = y_ref[...] + g2_ref[0] * acc_ref[...]


def _moe(h2, y1, gate2, wr_hi, wr_lo, rbias, wgu, wd, wgs, wus, wds, *, tm, seq, experts_per_step):
    t, d = h2.shape
    nb = gate2.shape[0]
    tiles_per_batch = seq // tm
    n_e = wgu.shape[0]

    def mod_map(i, j):
        return ((i // tiles_per_batch) if nb > 1 else 0, 0, 0)

    eps_ = experts_per_step
    return pl.pallas_call(
        functools.partial(_moe_kernel, experts_per_step=eps_),
        out_shape=jax.ShapeDtypeStruct((t, d), F32),
        grid=(t // tm, n_e // eps_),
        in_specs=[pl.BlockSpec((tm, d), lambda i, j: (i, 0)),
                  pl.BlockSpec((tm, d), lambda i, j: (i, 0)),
                  pl.BlockSpec((1, 1, d), mod_map),
                  pl.BlockSpec((n_e, d), lambda i, j: (0, 0)),
                  pl.BlockSpec((n_e, d), lambda i, j: (0, 0)),
                  pl.BlockSpec((n_e, 1), lambda i, j: (0, 0)),
                  pl.BlockSpec((eps_, d, 2 * D_EXPERT), lambda i, j: (j, 0, 0)),
                  pl.BlockSpec((eps_, D_EXPERT, d), lambda i, j: (j, 0, 0)),
                  pl.BlockSpec((d, D_SHARED), lambda i, j: (0, 0)),
                  pl.BlockSpec((d, D_SHARED), lambda i, j: (0, 0)),
                  pl.BlockSpec((D_SHARED, d), lambda i, j: (0, 0))],
        out_specs=pl.BlockSpec((tm, d), lambda i, j: (i, 0)),
        scratch_shapes=[pltpu.VMEM((tm, LANES), F32), pltpu.VMEM((tm, d), F32)],
        compiler_params=_cparams(("arbitrary", "arbitrary")),
        name="moe_dense",
    )(h2, y1, gate2, wr_hi, wr_lo, rbias, wgu, wd, wgs, wus, wds)


def _rope_tables(n_tokens):
    t = jnp.arange(n_tokens)
    row = (t // GRID_W).astype(F32)
    col = (t % GRID_W).astype(F32)
    nf = HEAD_DIM // 4
    freqs = ROPE_THETA ** (-jnp.arange(nf, dtype=F32) / nf)
    ang_r = row[:, None] * freqs
    ang_c = col[:, None] * freqs
    cos = jnp.concatenate([jnp.cos(ang_r)] * 2 + [jnp.cos(ang_c)] * 2, axis=1)
    sin = jnp.concatenate([-jnp.sin(ang_r), jnp.sin(ang_r), -jnp.sin(ang_c), jnp.sin(ang_c)], axis=1)
    reps = LANES // HEAD_DIM
    return jnp.tile(cos, (1, reps)), jnp.tile(sin, (1, reps))


def _head_gains(qn_a, kn_a, qn_b, kn_b):
    ones = jnp.ones((HEAD_DIM,), F32)
    parts = ([qn_a] * N_HEADS_A + [kn_a] * N_KV_A + [ones] * N_KV_A
             + [qn_b] * N_HEADS_B + [kn_b] * N_HEADS_B + [ones] * N_HEADS_B)
    return jnp.concatenate(parts).reshape(1, IN_COLS).astype(F32)


def _same_head_indicator():
    i = np.arange(LANES)
    return jnp.asarray((i[:, None] // HEAD_DIM) == (i[None, :] // HEAD_DIM), BF16)


def _token_major(cache):
    b, h, s, hd = cache.shape
    return cache.transpose(0, 2, 1, 3).reshape(b, s, h * hd).astype(BF16)


def kernel(x_prompt, x_sample, cache_k_a, cache_v_a, cache_k_b, cache_v_b, c, c_ctx, w_mod, b_mod, norm1, norm2, w_in, qn_a, kn_a, qn_b, kn_b, rpb, on_a, on_b, w_out, w_router, router_bias, w_gate_e, w_up_e, w_down_e, w_gate_s, w_up_s, w_down_s):
    depth = w_mod.shape[0]
    assert depth == 1
    l = 0
    bp, sp, d = x_prompt.shape
    bs, ss, _ = x_sample.shape

    cvec = jnp.concatenate([c_ctx[None, :], c], axis=0)
    rows = -(-cvec.shape[0] // 8) * 8
    cvec = jnp.pad(cvec, ((0, rows - cvec.shape[0]), (0, 0)))
    mod = _adaln(cvec, w_mod[l], b_mod[l])
    mod_p = [m.reshape(1, 1, d) for m in jnp.split(mod[0:1], 6, axis=-1)]
    mod_s = [m.reshape(bs, 1, d) for m in jnp.split(mod[1:1 + bs], 6, axis=-1)]

    w_in_bf = w_in[l].astype(BF16)
    w_out_bf = w_out[l].astype(BF16)
    gain = _head_gains(qn_a[l], kn_a[l], qn_b[l], kn_b[l])
    seg = _same_head_indicator()
    n1 = norm1[l].reshape(1, d)
    n2 = norm2[l].reshape(1, d)
    ona = on_a[l].reshape(1, WIDTH_A)
    onb = on_b[l].reshape(1, WIDTH_B)
    wr_t = w_router[l].T
    wr_hi = wr_t.astype(BF16)
    wr_lo = (wr_t - wr_hi.astype(F32)).astype(BF16)
    rbias = router_bias[l].reshape(N_EXPERTS, 1).astype(F32)
    wgu = jnp.concatenate([w_gate_e[l], w_up_e[l]], axis=-1).astype(BF16)
    wd = w_down_e[l].astype(BF16)
    wgs = w_gate_s[l].astype(BF16)
    wus = w_up_s[l].astype(BF16)
    wds = w_down_s[l].astype(BF16)
    moe_w = (wr_hi, wr_lo, rbias, wgu, wd, wgs, wus, wds)

    xp = x_prompt.reshape(bp * sp, d)
    proj_p, st_ka, st_va, st_kb, st_vb = _project(
        xp, mod_p[0], mod_p[1], n1, w_in_bf, gain, seg, None, tm=sp, seq=sp, states=True)
    oa_p, ob_p = _context_attention(proj_p, seq=sp)
    y1_p, h2_p = _merge(xp, oa_p, ob_p, ona, onb, w_out_bf, mod_p[2], mod_p[3], mod_p[4], n2, tm=512, seq=sp)
    y_p = _moe(h2_p, y1_p, mod_p[5], *moe_w, tm=1024, seq=sp, experts_per_step=4)

    xs = x_sample.reshape(bs * ss, d)
    proj_s, = _project(xs, mod_s[0], mod_s[1], n1, w_in_bf, gain, seg, _rope_tables(ss),
                       tm=512, seq=ss, states=False)
    oa_s = _latent_gqa(proj_s, _token_major(cache_k_a[:, l]), _token_major(cache_v_a[:, l]), seq=ss, tq=256)
    bias_t = _neighbourhood_bias(rpb[l], ss // GRID_W)
    ob_s = _latent_neighbourhood(proj_s, _token_major(cache_k_b[:, l]), _token_major(cache_v_b[:, l]),
                                 bias_t, seq=ss)
    y1_s, h2_s = _merge(xs, oa_s, ob_s, ona, onb, w_out_bf, mod_s[2], mod_s[3], mod_s[4], n2, tm=512, seq=ss)
    y_s = _moe(h2_s, y1_s, mod_s[5], *moe_w, tm=1024, seq=ss, experts_per_step=4)

    return (y_p.reshape(bp, sp, d), y_s.reshape(bs, ss, d), st_ka, st_va, st_kb, st_vb)
```

```python
import functools

import numpy as np
import jax
import jax.numpy as jnp
from jax import lax
from jax.experimental import pallas as pl
from jax.experimental.pallas import tpu as pltpu

F32 = jnp.float32
BF16 = jnp.bfloat16

D_MODEL = 1024
HEAD_DIM = 64
N_HEADS_A = 8
N_KV_A = 2
GROUP_A = N_HEADS_A // N_KV_A
N_HEADS_B = 8
WIDTH_A = N_HEADS_A * HEAD_DIM
WIDTH_B = N_HEADS_B * HEAD_DIM
KV_WIDTH_A = N_KV_A * HEAD_DIM
IN_COLS = WIDTH_A + 2 * KV_WIDTH_A + 3 * WIDTH_B
GRID_W = 64
ROPE_THETA = 10000.0
NA_KH = 8
NA_KW = 16
N_EXPERTS = 64
N_GROUPS = 8
GROUP_SIZE = N_EXPERTS // N_GROUPS
TOPK_GROUPS = 4
TOP_K = 8
D_EXPERT = 256
D_SHARED = 256
ROUTED_SCALE = 2.5
EPS = 1e-6

LANES = 128
MXU_DIM = 256
MASKED = -1e30

COL_QA = 0
COL_KA = WIDTH_A
COL_VA = COL_KA + KV_WIDTH_A
COL_QB = COL_VA + KV_WIDTH_A
COL_KB = COL_QB + WIDTH_B
COL_VB = COL_KB + WIDTH_B

NA_QROWS = 8
NA_KROWS = 2 * NA_KH
NA_TQ = NA_QROWS * GRID_W
NA_TK = NA_KROWS * GRID_W
NA_KBLK = 256

VMEM_LIMIT = 56 * 1024 * 1024


def _cparams(sem):
    return pltpu.CompilerParams(dimension_semantics=sem, vmem_limit_bytes=VMEM_LIMIT)


def _dot(a, b):
    return jnp.dot(a, b, preferred_element_type=F32)


def _dot_nt(a, b):
    return lax.dot_general(a, b, (((1,), (1,)), ((), ())), preferred_element_type=F32)


def _sigmoid(x):
    return 1.0 / (1.0 + jnp.exp(-x))


def _rms(x):
    return x * lax.rsqrt(jnp.mean(x * x, axis=-1, keepdims=True) + EPS)


def _mod_kernel(c_ref, w_ref, b_ref, o_ref):
    c = c_ref[...]
    s = c * _sigmoid(c)
    o_ref[...] = jnp.dot(s, w_ref[...], preferred_element_type=F32,
                         precision=lax.Precision.HIGHEST) + b_ref[...]


def _adaln(cvec, w_mod, b_mod):
    rows, d = cvec.shape
    n = w_mod.shape[1]
    tn = 512
    return pl.pallas_call(
        _mod_kernel,
        out_shape=jax.ShapeDtypeStruct((rows, n), F32),
        grid=(n // tn,),
        in_specs=[pl.BlockSpec((rows, d), lambda j: (0, 0)),
                  pl.BlockSpec((d, tn), lambda j: (0, j)),
                  pl.BlockSpec((1, tn), lambda j: (0, j))],
        out_specs=pl.BlockSpec((rows, tn), lambda j: (0, j)),
        compiler_params=_cparams(("arbitrary",)),
        name="adaln_mod",
    )(cvec, w_mod, b_mod.reshape(1, n))


_PROJ_CHUNKS = (
    [(COL_QA + i * LANES, LANES, True, True) for i in range(WIDTH_A // LANES)]
    + [(COL_KA, LANES, True, True), (COL_VA, LANES, False, False)]
    + [(COL_QB + i * LANES, LANES, True, False) for i in range(WIDTH_B // LANES)]
    + [(COL_KB + i * LANES, LANES, True, False) for i in range(WIDTH_B // LANES)]
    + [(COL_VB + i * LANES, LANES, False, False) for i in range(WIDTH_B // LANES)]
)


def _proj_kernel(*refs, rope, states):
    x_ref, sh_ref, sc_ref, n1_ref, w_ref, gain_ref, seg_ref = refs[:7]
    pos = 7
    if rope:
        cos_ref, sin_ref = refs[pos:pos + 2]
        pos += 2
    out_ref = refs[pos]
    pos += 1
    if states:
        ka_ref, va_ref, kb_ref, vb_ref = refs[pos:pos + 4]

    x = x_ref[...]
    h = _rms(x) * n1_ref[...]
    h = h * (1.0 + sc_ref[0]) + sh_ref[0]
    p = _dot(h.astype(BF16), w_ref[...])
    seg = seg_ref[...]
    if rope:
        cos = cos_ref[...]
        sin = sin_ref[...]
        lane = lax.broadcasted_iota(jnp.int32, cos.shape, 1)
        first_half = (lane % (HEAD_DIM // 2)) < (HEAD_DIM // 4)

    for c0, w, normed, roped in _PROJ_CHUNKS:
        pc = p[:, c0:c0 + w]
        if normed:
            sq = pc * pc
            hi = sq.astype(BF16)
            lo = (sq - hi.astype(F32)).astype(BF16)
            ss = _dot(hi, seg) + _dot(lo, seg)
            pc = pc * lax.rsqrt(ss * (1.0 / HEAD_DIM) + EPS) * gain_ref[:, c0:c0 + w]
        if states:
            if c0 == COL_KA:
                for hh in range(N_KV_A):
                    ka_ref[0, 0, hh] = pc[:, hh * HEAD_DIM:(hh + 1) * HEAD_DIM]
            elif c0 == COL_VA:
                for hh in range(N_KV_A):
                    va_ref[0, 0, hh] = pc[:, hh * HEAD_DIM:(hh + 1) * HEAD_DIM]
            elif COL_KB <= c0 < COL_VB:
                base = (c0 - COL_KB) // HEAD_DIM
                for hh in range(LANES // HEAD_DIM):
                    kb_ref[0, 0, base + hh] = pc[:, hh * HEAD_DIM:(hh + 1) * HEAD_DIM]
            elif c0 >= COL_VB:
                base = (c0 - COL_VB) // HEAD_DIM
                for hh in range(LANES // HEAD_DIM):
                    vb_ref[0, 0, base + hh] = pc[:, hh * HEAD_DIM:(hh + 1) * HEAD_DIM]
        if rope and roped:
            partner = jnp.where(first_half,
                                pltpu.roll(pc, LANES - HEAD_DIM // 4, 1),
                                pltpu.roll(pc, HEAD_DIM // 4, 1))
            pc = pc * cos + partner * sin
        if c0 < COL_KA or COL_QB <= c0 < COL_KB:
            pc = pc * (HEAD_DIM ** -0.5)
        out_ref[:, c0:c0 + w] = pc.astype(BF16)


def _project(x2d, shift, scale, norm1, w_in_bf, gain, seg, rope_tabs, *, tm, seq, states):
    t, d = x2d.shape
    nb = shift.shape[0]
    tiles_per_batch = seq // tm
    rope = rope_tabs is not None

    def mod_map(i):
        return ((i // tiles_per_batch) if nb > 1 else 0, 0, 0)

    in_specs = [pl.BlockSpec((tm, d), lambda i: (i, 0)),
                pl.BlockSpec((1, 1, d), mod_map),
                pl.BlockSpec((1, 1, d), mod_map),
                pl.BlockSpec((1, d), lambda i: (0, 0)),
                pl.BlockSpec((d, IN_COLS), lambda i: (0, 0)),
                pl.BlockSpec((1, IN_COLS), lambda i: (0, 0)),
                pl.BlockSpec((LANES, LANES), lambda i: (0, 0))]
    args = [x2d, shift, scale, norm1, w_in_bf, gain, seg]
    if rope:
        in_specs += [pl.BlockSpec((tm, LANES), lambda i: (i % tiles_per_batch, 0))] * 2
        args += list(rope_tabs)
    out_shape = [jax.ShapeDtypeStruct((t, IN_COLS), BF16)]
    out_specs = [pl.BlockSpec((tm, IN_COLS), lambda i: (i, 0))]
    if states:
        assert tm == seq
        b = t // seq
        for nh in (N_KV_A, N_KV_A, N_HEADS_B, N_HEADS_B):
            out_shape.append(jax.ShapeDtypeStruct((b, 1, nh, seq, HEAD_DIM), F32))
            out_specs.append(pl.BlockSpec((1, 1, nh, seq, HEAD_DIM), lambda i: (i, 0, 0, 0, 0)))
    return pl.pallas_call(
        functools.partial(_proj_kernel, rope=rope, states=states),
        out_shape=out_shape,
        grid=(t // tm,),
        in_specs=in_specs,
        out_specs=out_specs,
        compiler_params=_cparams(("arbitrary",)),
        name="proj_states" if states else "proj_rope",
    )(*args)


def _lane_half(shape):
    return lax.broadcasted_iota(jnp.int32, shape, 1) // HEAD_DIM


def _keep_half(x, half):
    return jnp.where(_lane_half(x.shape) == half, x, jnp.zeros_like(x))


def _transpose_bf16(x):
    return x.astype(F32).T.astype(BF16)


def _attend(q, keys, values_t, biases):
    scores = []
    for k, b in zip(keys, biases):
        s = _dot_nt(k, q)
        if b is not None:
            s = s + b
        scores.append(s)
    m = functools.reduce(jnp.maximum, [jnp.max(s, axis=0, keepdims=True) for s in scores])
    denom = None
    out = None
    for s, vt in zip(scores, values_t):
        p = jnp.exp(s - m)
        ps = jnp.sum(p, axis=0, keepdims=True)
        po = _dot(vt, p.astype(BF16))
        denom = ps if denom is None else denom + ps
        out = po if out is None else out + po
    return out / denom


def _swap_halves(q_bf16):
    return pltpu.roll(q_bf16.astype(F32), HEAD_DIM, 1).astype(BF16)


def _gqa_heads(q_of_pair, keys_by_group, values_t):
    outs = []
    for h in range(N_HEADS_A):
        g = h // GROUP_A
        q = q_of_pair(h // 2)
        if h % 2 != g:
            q = _swap_halves(q)
        o = _attend(q, keys_by_group[g], values_t, [None] * len(values_t))
        outs.append(o[g * HEAD_DIM:(g + 1) * HEAD_DIM])
    return jnp.concatenate(outs, axis=0)


def _ctx_attn_kernel(p_ref, oa_ref, ob_ref):
    ka = p_ref[:, COL_KA:COL_KA + LANES]
    va_t = [_transpose_bf16(p_ref[:, COL_VA:COL_VA + LANES])]
    keys_by_group = [[_keep_half(ka, g)] for g in range(N_KV_A)]
    oa = _gqa_heads(lambda i: p_ref[:, COL_QA + i * LANES:COL_QA + (i + 1) * LANES],
                    keys_by_group, va_t)
    oa_ref[...] = oa.T

    outs = []
    for i in range(N_HEADS_B // 2):
        q = p_ref[:, COL_QB + i * LANES:COL_QB + (i + 1) * LANES]
        k = p_ref[:, COL_KB + i * LANES:COL_KB + (i + 1) * LANES]
        vt = [_transpose_bf16(p_ref[:, COL_VB + i * LANES:COL_VB + (i + 1) * LANES])]
        for half in range(2):
            o = _attend(q, [_keep_half(k, half)], vt, [None])
            outs.append(o[half * HEAD_DIM:(half + 1) * HEAD_DIM])
    ob_ref[...] = jnp.concatenate(outs, axis=0).T


def _context_attention(proj, *, seq):
    t = proj.shape[0]
    return pl.pallas_call(
        _ctx_attn_kernel,
        out_shape=[jax.ShapeDtypeStruct((t, WIDTH_A), F32), jax.ShapeDtypeStruct((t, WIDTH_B), F32)],
        grid=(t // seq,),
        in_specs=[pl.BlockSpec((seq, IN_COLS), lambda i: (i, 0))],
        out_specs=[pl.BlockSpec((seq, WIDTH_A), lambda i: (i, 0)),
                   pl.BlockSpec((seq, WIDTH_B), lambda i: (i, 0))],
        compiler_params=_cparams(("arbitrary",)),
        name="context_attention",
    )(proj)


def _gqa_latent_kernel(q_ref, k_ref, v_ref, ck_ref, cv_ref, o_ref, kg_ref, ckg_ref, vt_ref, cvt_ref):
    @pl.when(pl.program_id(1) == 0)
    def _():
        k = k_ref[...]
        ck = ck_ref[0]
        for g in range(N_KV_A):
            kg_ref[g] = _keep_half(k, g)
            ckg_ref[g] = _keep_half(ck, g)
        vt_ref[...] = _transpose_bf16(v_ref[...])
        cvt_ref[...] = _transpose_bf16(cv_ref[0])

    keys_by_group = [[kg_ref[g], ckg_ref[g]] for g in range(N_KV_A)]
    values_t = [vt_ref[...], cvt_ref[...]]
    oa = _gqa_heads(lambda i: q_ref[:, i * LANES:(i + 1) * LANES], keys_by_group, values_t)
    o_ref[...] = oa.T


def _latent_gqa(proj, ctx_k, ctx_v, *, seq, tq):
    t = proj.shape[0]
    b = t // seq
    nq = seq // tq
    past = ctx_k.shape[1]
    return pl.pallas_call(
        _gqa_latent_kernel,
        out_shape=jax.ShapeDtypeStruct((t, WIDTH_A), F32),
        grid=(b, nq),
        in_specs=[pl.BlockSpec((tq, WIDTH_A), lambda bi, qi: (bi * nq + qi, 0)),
                  pl.BlockSpec((seq, LANES), lambda bi, qi: (bi, COL_KA // LANES)),
                  pl.BlockSpec((seq, LANES), lambda bi, qi: (bi, COL_VA // LANES)),
                  pl.BlockSpec((1, past, LANES), lambda bi, qi: (bi, 0, 0)),
                  pl.BlockSpec((1, past, LANES), lambda bi, qi: (bi, 0, 0))],
        out_specs=pl.BlockSpec((tq, WIDTH_A), lambda bi, qi: (bi * nq + qi, 0)),
        scratch_shapes=[pltpu.VMEM((N_KV_A, seq, LANES), BF16),
                        pltpu.VMEM((N_KV_A, past, LANES), BF16),
                        pltpu.VMEM((LANES, seq), BF16),
                        pltpu.VMEM((LANES, past), BF16)],
        compiler_params=_cparams(("arbitrary", "arbitrary")),
        name="latent_gqa",
    )(proj, proj, proj, ctx_k, ctx_v)


def _na_kernel(q_ref, k0, k1, k2, k3, v0, v1, v2, v3, ck_ref, cv_ref, bias_ref, o_ref):
    q = q_ref[...]
    ks = [r[...] for r in (k0, k1, k2, k3)] + [ck_ref[0]]
    values_t = [_transpose_bf16(r[...]) for r in (v0, v1, v2, v3)] + [_transpose_bf16(cv_ref[0])]
    outs = []
    for half in range(2):
        keys = [_keep_half(k, half) for k in ks]
        biases = [bias_ref[0, half, j * NA_KBLK:(j + 1) * NA_KBLK, :] for j in range(4)] + [None]
        o = _attend(q, keys, values_t, biases)
        outs.append(o[half * HEAD_DIM:(half + 1) * HEAD_DIM])
    o_ref[...] = jnp.concatenate(outs, axis=0).T


def _na_first_key_block(i, rows):
    per_qblock = NA_QROWS * GRID_W // NA_KBLK
    lead = (NA_KH // 2) * GRID_W // NA_KBLK
    return jnp.clip(per_qblock * i - lead, 0, (rows - NA_KROWS) * GRID_W // NA_KBLK)


def _latent_neighbourhood(proj, ctx_k, ctx_v, bias_t, *, seq):
    t = proj.shape[0]
    b = t // seq
    rows = seq // GRID_W
    nblk = rows // NA_QROWS
    kblk_per_batch = seq // NA_KBLK
    past = ctx_k.shape[1]
    grid = (N_HEADS_B // 2, nblk, b)

    def kv_spec(col0, j):
        return pl.BlockSpec(
            (NA_KBLK, LANES),
            lambda hp, i, bi: (bi * kblk_per_batch + _na_first_key_block(i, rows) + j, col0 // LANES + hp))

    def variant(i):
        return jnp.where(i == 0, 0, jnp.where(i == nblk - 1, 2, 1))

    in_specs = ([pl.BlockSpec((NA_TQ, LANES), lambda hp, i, bi: (bi * nblk + i, COL_QB // LANES + hp))]
                + [kv_spec(COL_KB, j) for j in range(4)]
                + [kv_spec(COL_VB, j) for j in range(4)]
                + [pl.BlockSpec((1, past, LANES), lambda hp, i, bi: (bi, 0, hp)),
                   pl.BlockSpec((1, past, LANES), lambda hp, i, bi: (bi, 0, hp)),
                   pl.BlockSpec((1, 2, NA_TK, NA_TQ), lambda hp, i, bi: (variant(i), hp, 0, 0))])
    return pl.pallas_call(
        _na_kernel,
        out_shape=jax.ShapeDtypeStruct((t, WIDTH_B), F32),
        grid=grid,
        in_specs=in_specs,
        out_specs=pl.BlockSpec((NA_TQ, LANES), lambda hp, i, bi: (bi * nblk + i, hp)),
        compiler_params=_cparams(("arbitrary", "arbitrary", "arbitrary")),
        name="latent_neighbourhood",
    )(proj, *([proj] * 8), ctx_k, ctx_v, bias_t)


def _neighbourhood_bias(rpb, rows):
    nblk = rows // NA_QROWS
    n_dr = 2 * NA_KH - 1
    n_dc = 2 * NA_KW - 1
    kc = np.arange(GRID_W)[:, None]
    qc = np.arange(GRID_W)[None, :]
    ws = np.clip(qc - NA_KW // 2, 0, GRID_W - NA_KW)
    col_ok = (kc >= ws) & (kc < ws + NA_KW)
    dc = np.clip(kc - qc + NA_KW - 1, 0, n_dc - 1)
    dc_onehot = (dc[None] == np.arange(n_dc)[:, None, None]).astype(np.float32)
    tiles = jnp.einsum('hab,bkq->hakq', rpb.astype(F32), jnp.asarray(dc_onehot),
                       precision=lax.Precision.HIGHEST)
    tiles = jnp.where(jnp.asarray(col_ok)[None, None], tiles, MASKED)
    masked_tile = jnp.full((rpb.shape[0], 1, GRID_W, GRID_W), MASKED, F32)
    tiles = jnp.concatenate([tiles, masked_tile], axis=1)
    pick = np.zeros((3, NA_KROWS, NA_QROWS, n_dr + 1), np.float32)
    for v, i in enumerate((0, 1, nblk - 1)):
        r0 = i * NA_QROWS
        ks = int(np.clip(r0 - NA_KH // 2, 0, rows - NA_KROWS))
        for kl in range(NA_KROWS):
            for ql in range(NA_QROWS):
                kr, qr = ks + kl, r0 + ql
                rs = int(np.clip(qr - NA_KH // 2, 0, rows - NA_KH))
                ok = rs <= kr < rs + NA_KH
                pick[v, kl, ql, (kr - qr + NA_KH - 1) if ok else n_dr] = 1.0
    bias = jnp.einsum('vkqa,hacd->vhkcqd', jnp.asarray(pick), tiles, precision=lax.Precision.HIGHEST)
    return bias.reshape(3, rpb.shape[0], NA_TK, NA_TQ)


def _merge_kernel(x_ref, oa_ref, ob_ref, ona_ref, onb_ref, wo_ref, g1_ref, sh2_ref, sc2_ref, n2_ref,
                  y_ref, h_ref):
    na = (_rms(oa_ref[...]) * ona_ref[...]).astype(BF16)
    nb = (_rms(ob_ref[...]) * onb_ref[...]).astype(BF16)
    mix = _dot(na, wo_ref[0:WIDTH_A, :]) + _dot(nb, wo_ref[WIDTH_A:WIDTH_A + WIDTH_B, :])
    y = x_ref[...] + g1_ref[0] * mix
    y_ref[...] = y
    h = _rms(y) * n2_ref[...]
    h_ref[...] = (h * (1.0 + sc2_ref[0]) + sh2_ref[0]).astype(BF16)


def _merge(x2d, oa, ob, on_a, on_b, w_out_bf, gate1, shift2, scale2, norm2, *, tm, seq):
    t, d = x2d.shape
    nb = gate1.shape[0]
    tiles_per_batch = seq // tm

    def mod_map(i):
        return ((i // tiles_per_batch) if nb > 1 else 0, 0, 0)

    return pl.pallas_call(
        _merge_kernel,
        out_shape=[jax.ShapeDtypeStruct((t, d), F32), jax.ShapeDtypeStruct((t, d), BF16)],
        grid=(t // tm,),
        in_specs=[pl.BlockSpec((tm, d), lambda i: (i, 0)),
                  pl.BlockSpec((tm, WIDTH_A), lambda i: (i, 0)),
                  pl.BlockSpec((tm, WIDTH_B), lambda i: (i, 0)),
                  pl.BlockSpec((1, WIDTH_A), lambda i: (0, 0)),
                  pl.BlockSpec((1, WIDTH_B), lambda i: (0, 0)),
                  pl.BlockSpec((WIDTH_A + WIDTH_B, d), lambda i: (0, 0)),
                  pl.BlockSpec((1, 1, d), mod_map),
                  pl.BlockSpec((1, 1, d), mod_map),
                  pl.BlockSpec((1, 1, d), mod_map),
                  pl.BlockSpec((1, d), lambda i: (0, 0))],
        out_specs=[pl.BlockSpec((tm, d), lambda i: (i, 0)),
                   pl.BlockSpec((tm, d), lambda i: (i, 0))],
        compiler_params=_cparams(("arbitrary",)),
        name="merge_out_proj",
    )(x2d, oa, ob, on_a, on_b, w_out_bf, gate1, shift2, scale2, norm2)


def _first_index_of_max(x, iota):
    mx = jnp.max(x, axis=0, keepdims=True)
    idx = jnp.min(jnp.where(x == mx, iota, float(x.shape[0])), axis=0, keepdims=True)
    return mx, iota == idx


def _router_gates(h_bf, wr_hi, wr_lo, rbias):
    logits = _dot_nt(wr_hi, h_bf) + _dot_nt(wr_lo, h_bf)
    scores = _sigmoid(logits)
    sel = scores + rbias
    tm = sel.shape[1]
    iota_g = lax.broadcasted_iota(jnp.int32, (GROUP_SIZE, tm), 0).astype(F32)
    group_scores = []
    for g in range(N_GROUPS):
        grp = sel[g * GROUP_SIZE:(g + 1) * GROUP_SIZE]
        m1, first = _first_index_of_max(grp, iota_g)
        m2 = jnp.max(jnp.where(first, -jnp.inf, grp), axis=0, keepdims=True)
        group_scores.append(m1 + m2)
    gs = jnp.concatenate(group_scores, axis=0)
    iota_n = lax.broadcasted_iota(jnp.int32, (N_GROUPS, tm), 0).astype(F32)
    group_on = jnp.zeros((N_GROUPS, tm), F32)
    for _ in range(TOPK_GROUPS):
        _, pick = _first_index_of_max(gs, iota_n)
        group_on = jnp.where(pick, 1.0, group_on)
        gs = jnp.where(pick, -jnp.inf, gs)
    expert_on = jnp.concatenate(
        [jnp.broadcast_to(group_on[g:g + 1], (GROUP_SIZE, tm)) for g in range(N_GROUPS)], axis=0)
    cand = jnp.where(expert_on > 0.0, sel, -jnp.inf)
    iota_e = lax.broadcasted_iota(jnp.int32, (N_EXPERTS, tm), 0).astype(F32)
    w = jnp.zeros((N_EXPERTS, tm), F32)
    for _ in range(TOP_K):
        _, pick = _first_index_of_max(cand, iota_e)
        w = jnp.where(pick, scores, w)
        cand = jnp.where(pick, -jnp.inf, cand)
    return w / jnp.sum(w, axis=0, keepdims=True) * ROUTED_SCALE


def _moe_kernel(h_ref, y_ref, g2_ref, wrh_ref, wrl_ref, rb_ref, wgu_ref, wd_ref, wgs_ref, wus_ref, wds_ref,
                o_ref, gate_ref, acc_ref, *, experts_per_step):
    j = pl.program_id(1)
    h = h_ref[...]

    @pl.when(j == 0)
    def _():
        gates_t = _router_gates(h, wrh_ref[...], wrl_ref[...], rb_ref[...])
        pad = jnp.zeros((LANES - N_EXPERTS, gates_t.shape[1]), F32)
        gate_ref[...] = jnp.concatenate([gates_t, pad], axis=0).T
        gs = _dot(h, wgs_ref[...])
        us = _dot(h, wus_ref[...])
        act = (gs * _sigmoid(gs)) * us
        acc_ref[...] = _dot(act.astype(BF16), wds_ref[...])

    gate = gate_ref[...]
    lane = lax.broadcasted_iota(jnp.int32, gate.shape, 1)
    total = None
    for jj in range(experts_per_step):
        e = j * experts_per_step + jj
        ge = jnp.sum(jnp.where(lane == e, gate, 0.0), axis=1, keepdims=True)
        gu = _dot(h, wgu_ref[jj])
        g = gu[:, :D_EXPERT]
        u = gu[:, D_EXPERT:]
        act = ((g * _sigmoid(g)) * u) * ge
        part = _dot(act.astype(BF16), wd_ref[jj])
        total = part if total is None else total + part
    acc_ref[...] += total

    @pl.when(j == pl.num_programs(1) - 1)
    def _():
        o_ref[...] = y_ref[...] + g2_ref[0] * acc_ref[...]


def _moe(h2, y1, gate2, wr_hi, wr_lo, rbias, wgu, wd, wgs, wus, wds, *, tm, seq, experts_per_step):
    t, d = h2.shape
    nb = gate2.shape[0]
    tiles_per_batch = seq // tm
    n_e = wgu.shape[0]

    def mod_map(i, j):
        return ((i // tiles_per_batch) if nb > 1 else 0, 0, 0)

    eps_ = experts_per_step
    return pl.pallas_call(
        functools.partial(_moe_kernel, experts_per_step=eps_),
        out_shape=jax.ShapeDtypeStruct((t, d), F32),
        grid=(t // tm, n_e // eps_),
        in_specs=[pl.BlockSpec((tm, d), lambda i, j: (i, 0)),
                  pl.BlockSpec((tm, d), lambda i, j: (i, 0)),
                  pl.BlockSpec((1, 1, d), mod_map),
                  pl.BlockSpec((n_e, d), lambda i, j: (0, 0)),
                  pl.BlockSpec((n_e, d), lambda i, j: (0, 0)),
                  pl.BlockSpec((n_e, 1), lambda i, j: (0, 0)),
                  pl.BlockSpec((eps_, d, 2 * D_EXPERT), lambda i, j: (j, 0, 0)),
                  pl.BlockSpec((eps_, D_EXPERT, d), lambda i, j: (j, 0, 0)),
                  pl.BlockSpec((d, D_SHARED), lambda i, j: (0, 0)),
                  pl.BlockSpec((d, D_SHARED), lambda i, j: (0, 0)),
                  pl.BlockSpec((D_SHARED, d), lambda i, j: (0, 0))],
        out_specs=pl.BlockSpec((tm, d), lambda i, j: (i, 0)),
        scratch_shapes=[pltpu.VMEM((tm, LANES), F32), pltpu.VMEM((tm, d), F32)],
        compiler_params=_cparams(("arbitrary", "arbitrary")),
        name="moe_dense",
    )(h2, y1, gate2, wr_hi, wr_lo, rbias, wgu, wd, wgs, wus, wds)


def _rope_tables(n_tokens):
    t = jnp.arange(n_tokens)
    row = (t // GRID_W).astype(F32)
    col = (t % GRID_W).astype(F32)
    nf = HEAD_DIM // 4
    freqs = ROPE_THETA ** (-jnp.arange(nf, dtype=F32) / nf)
    ang_r = row[:, None] * freqs
    ang_c = col[:, None] * freqs
    cos = jnp.concatenate([jnp.cos(ang_r)] * 2 + [jnp.cos(ang_c)] * 2, axis=1)
    sin = jnp.concatenate([-jnp.sin(ang_r), jnp.sin(ang_r), -jnp.sin(ang_c), jnp.sin(ang_c)], axis=1)
    reps = LANES // HEAD_DIM
    return jnp.tile(cos, (1, reps)), jnp.tile(sin, (1, reps))


def _head_gains(qn_a, kn_a, qn_b, kn_b):
    ones = jnp.ones((HEAD_DIM,), F32)
    parts = ([qn_a] * N_HEADS_A + [kn_a] * N_KV_A + [ones] * N_KV_A
             + [qn_b] * N_HEADS_B + [kn_b] * N_HEADS_B + [ones] * N_HEADS_B)
    return jnp.concatenate(parts).reshape(1, IN_COLS).astype(F32)


def _same_head_indicator():
    i = np.arange(LANES)
    return jnp.asarray((i[:, None] // HEAD_DIM) == (i[None, :] // HEAD_DIM), BF16)


def _token_major(cache):
    b, h, s, hd = cache.shape
    return cache.transpose(0, 2, 1, 3).reshape(b, s, h * hd).astype(BF16)


def kernel(x_prompt, x_sample, cache_k_a, cache_v_a, cache_k_b, cache_v_b, c, c_ctx, w_mod, b_mod, norm1, norm2, w_in, qn_a, kn_a, qn_b, kn_b, rpb, on_a, on_b, w_out, w_router, router_bias, w_gate_e, w_up_e, w_down_e, w_gate_s, w_up_s, w_down_s):
    depth = w_mod.shape[0]
    assert depth == 1
    l = 0
    bp, sp, d = x_prompt.shape
    bs, ss, _ = x_sample.shape

    cvec = jnp.concatenate([c_ctx[None, :], c], axis=0)
    rows = -(-cvec.shape[0] // 8) * 8
    cvec = jnp.pad(cvec, ((0, rows - cvec.shape[0]), (0, 0)))
    mod = _adaln(cvec, w_mod[l], b_mod[l])
    mod_p = [m.reshape(1, 1, d) for m in jnp.split(mod[0:1], 6, axis=-1)]
    mod_s = [m.reshape(bs, 1, d) for m in jnp.split(mod[1:1 + bs], 6, axis=-1)]

    w_in_bf = w_in[l].astype(BF16)
    w_out_bf = w_out[l].astype(BF16)
    gain = _head_gains(qn_a[l], kn_a[l], qn_b[l], kn_b[l])
    seg = _same_head_indicator()
    n1 = norm1[l].reshape(1, d)
    n2 = norm2[l].reshape(1, d)
    ona = on_a[l].reshape(1, WIDTH_A)
    onb = on_b[l].reshape(1, WIDTH_B)
    wr_t = w_router[l].T
    wr_hi = wr_t.astype(BF16)
    wr_lo = (wr_t - wr_hi.astype(F32)).astype(BF16)
    rbias = router_bias[l].reshape(N_EXPERTS, 1).astype(F32)
    wgu = jnp.concatenate([w_gate_e[l], w_up_e[l]], axis=-1).astype(BF16)
    wd = w_down_e[l].astype(BF16)
    wgs = w_gate_s[l].astype(BF16)
    wus = w_up_s[l].astype(BF16)
    wds = w_down_s[l].astype(BF16)
    moe_w = (wr_hi, wr_lo, rbias, wgu, wd, wgs, wus, wds)

    xp = x_prompt.reshape(bp * sp, d)
    proj_p, st_ka, st_va, st_kb, st_vb = _project(
        xp, mod_p[0], mod_p[1], n1, w_in_bf, gain, seg, None, tm=sp, seq=sp, states=True)
    oa_p, ob_p = _context_attention(proj_p, seq=sp)
    y1_p, h2_p = _merge(xp, oa_p, ob_p, ona, onb, w_out_bf, mod_p[2], mod_p[3], mod_p[4], n2, tm=512, seq=sp)
    y_p = _moe(h2_p, y1_p, mod_p[5], *moe_w, tm=1024, seq=sp, experts_per_step=4)

    xs = x_sample.reshape(bs * ss, d)
    proj_s, = _project(xs, mod_s[0], mod_s[1], n1, w_in_bf, gain, seg, _rope_tables(ss),
                       tm=512, seq=ss, states=False)
    oa_s = _latent_gqa(proj_s, _token_major(cache_k_a[:, l]), _token_major(cache_v_a[:, l]), seq=ss, tq=256)
    bias_t = _neighbourhood_bias(rpb[l], ss // GRID_W)
    ob_s = _latent_neighbourhood(proj_s, _token_major(cache_k_b[:, l]), _token_major(cache_v_b[:, l]),
                                 bias_t, seq=ss)
    y1_s, h2_s = _merge(xs, oa_s, ob_s, ona, onb, w_out_bf, mod_s[2], mod_s[3], mod_s[4], n2, tm=512, seq=ss)
    y_s = _moe(h2_s, y1_s, mod_s[5], *moe_w, tm=1024, seq=ss, experts_per_step=4)

    return (y_p.reshape(bp, sp, d), y_s.reshape(bs, ss, d), st_ka, st_va, st_kb, st_vb)
```

```python
import functools

import numpy as np
import jax
import jax.numpy as jnp
from jax import lax
from jax.experimental import pallas as pl
from jax.experimental.pallas import tpu as pltpu

F32 = jnp.float32
BF16 = jnp.bfloat16

D_MODEL = 1024
HEAD_DIM = 64
N_HEADS_A = 8
N_KV_A = 2
GROUP_A = N_HEADS_A // N_KV_A
N_HEADS_B = 8
WIDTH_A = N_HEADS_A * HEAD_DIM
WIDTH_B = N_HEADS_B * HEAD_DIM
KV_WIDTH_A = N_KV_A * HEAD_DIM
IN_COLS = WIDTH_A + 2 * KV_WIDTH_A + 3 * WIDTH_B
GRID_W = 64
ROPE_THETA = 10000.0
NA_KH = 8
NA_KW = 16
N_EXPERTS = 64
N_GROUPS = 8
GROUP_SIZE = N_EXPERTS // N_GROUPS
TOPK_GROUPS = 4
TOP_K = 8
D_EXPERT = 256
D_SHARED = 256
ROUTED_SCALE = 2.5
EPS = 1e-6

LANES = 128
MXU_DIM = 256
MASKED = -1e30

COL_QA = 0
COL_KA = WIDTH_A
COL_VA = COL_KA + KV_WIDTH_A
COL_QB = COL_VA + KV_WIDTH_A
COL_KB = COL_QB + WIDTH_B
COL_VB = COL_KB + WIDTH_B

NA_QROWS = 8
NA_KROWS = 2 * NA_KH
NA_TQ = NA_QROWS * GRID_W
NA_TK = NA_KROWS * GRID_W
NA_KBLK = 256

VMEM_LIMIT = 56 * 1024 * 1024


def _cparams(sem):
    return pltpu.CompilerParams(dimension_semantics=sem, vmem_limit_bytes=VMEM_LIMIT)


def _dot(a, b):
    return jnp.dot(a, b, preferred_element_type=F32)


def _dot_nt(a, b):
    return lax.dot_general(a, b, (((1,), (1,)), ((), ())), preferred_element_type=F32)


def _sigmoid(x):
    return 1.0 / (1.0 + jnp.exp(-x))


def _rms(x):
    return x * lax.rsqrt(jnp.mean(x * x, axis=-1, keepdims=True) + EPS)


def _mod_kernel(c_ref, w_ref, b_ref, o_ref):
    c = c_ref[...]
    s = c * _sigmoid(c)
    o_ref[...] = jnp.dot(s, w_ref[...], preferred_element_type=F32,
                         precision=lax.Precision.HIGHEST) + b_ref[...]


def _adaln(cvec, w_mod, b_mod):
    rows, d = cvec.shape
    n = w_mod.shape[1]
    tn = 512
    return pl.pallas_call(
        _mod_kernel,
        out_shape=jax.ShapeDtypeStruct((rows, n), F32),
        grid=(n // tn,),
        in_specs=[pl.BlockSpec((rows, d), lambda j: (0, 0)),
                  pl.BlockSpec((d, tn), lambda j: (0, j)),
                  pl.BlockSpec((1, tn), lambda j: (0, j))],
        out_specs=pl.BlockSpec((rows, tn), lambda j: (0, j)),
        compiler_params=_cparams(("arbitrary",)),
        name="adaln_mod",
    )(cvec, w_mod, b_mod.reshape(1, n))


_PROJ_CHUNKS = (
    [(COL_QA + i * LANES, LANES, True, True) for i in range(WIDTH_A // LANES)]
    + [(COL_KA, LANES, True, True), (COL_VA, LANES, False, False)]
    + [(COL_QB + i * LANES, LANES, True, False) for i in range(WIDTH_B // LANES)]
    + [(COL_KB + i * LANES, LANES, True, False) for i in range(WIDTH_B // LANES)]
    + [(COL_VB + i * LANES, LANES, False, False) for i in range(WIDTH_B // LANES)]
)


def _proj_kernel(*refs, rope, states):
    x_ref, sh_ref, sc_ref, n1_ref, w_ref, gain_ref, seg_ref = refs[:7]
    pos = 7
    if rope:
        cos_ref, sin_ref = refs[pos:pos + 2]
        pos += 2
    out_ref = refs[pos]
    pos += 1
    if states:
        ka_ref, va_ref, kb_ref, vb_ref = refs[pos:pos + 4]

    x = x_ref[...]
    h = _rms(x) * n1_ref[...]
    h = h * (1.0 + sc_ref[0]) + sh_ref[0]
    p = _dot(h.astype(BF16), w_ref[...])
    seg = seg_ref[...]
    if rope:
        cos = cos_ref[...]
        sin = sin_ref[...]
        lane = lax.broadcasted_iota(jnp.int32, cos.shape, 1)
        first_half = (lane % (HEAD_DIM // 2)) < (HEAD_DIM // 4)

    for c0, w, normed, roped in _PROJ_CHUNKS:
        pc = p[:, c0:c0 + w]
        if normed:
            sq = pc * pc
            hi = sq.astype(BF16)
            lo = (sq - hi.astype(F32)).astype(BF16)
            ss = _dot(hi, seg) + _dot(lo, seg)
            pc = pc * lax.rsqrt(ss * (1.0 / HEAD_DIM) + EPS) * gain_ref[:, c0:c0 + w]
        if states:
            if c0 == COL_KA:
                for hh in range(N_KV_A):
                    ka_ref[0, 0, hh] = pc[:, hh * HEAD_DIM:(hh + 1) * HEAD_DIM]
            elif c0 == COL_VA:
                for hh in range(N_KV_A):
                    va_ref[0, 0, hh] = pc[:, hh * HEAD_DIM:(hh + 1) * HEAD_DIM]
            elif COL_KB <= c0 < COL_VB:
                base = (c0 - COL_KB) // HEAD_DIM
                for hh in range(LANES // HEAD_DIM):
                    kb_ref[0, 0, base + hh] = pc[:, hh * HEAD_DIM:(hh + 1) * HEAD_DIM]
            elif c0 >= COL_VB:
                base = (c0 - COL_VB) // HEAD_DIM
                for hh in range(LANES // HEAD_DIM):
                    vb_ref[0, 0, base + hh] = pc[:, hh * HEAD_DIM:(hh + 1) * HEAD_DIM]
        if rope and roped:
            partner = jnp.where(first_half,
                                pltpu.roll(pc, LANES - HEAD_DIM // 4, 1),
                                pltpu.roll(pc, HEAD_DIM // 4, 1))
            pc = pc * cos + partner * sin
        if c0 < COL_KA or COL_QB <= c0 < COL_KB:
            pc = pc * (HEAD_DIM ** -0.5)
        out_ref[:, c0:c0 + w] = pc.astype(BF16)


def _project(x2d, shift, scale, norm1, w_in_bf, gain, seg, rope_tabs, *, tm, seq, states):
    t, d = x2d.shape
    nb = shift.shape[0]
    tiles_per_batch = seq // tm
    rope = rope_tabs is not None

    def mod_map(i):
        return ((i // tiles_per_batch) if nb > 1 else 0, 0, 0)

    in_specs = [pl.BlockSpec((tm, d), lambda i: (i, 0)),
                pl.BlockSpec((1, 1, d), mod_map),
                pl.BlockSpec((1, 1, d), mod_map),
                pl.BlockSpec((1, d), lambda i: (0, 0)),
                pl.BlockSpec((d, IN_COLS), lambda i: (0, 0)),
                pl.BlockSpec((1, IN_COLS), lambda i: (0, 0)),
                pl.BlockSpec((LANES, LANES), lambda i: (0, 0))]
    args = [x2d, shift, scale, norm1, w_in_bf, gain, seg]
    if rope:
        in_specs += [pl.BlockSpec((tm, LANES), lambda i: (i % tiles_per_batch, 0))] * 2
        args += list(rope_tabs)
    out_shape = [jax.ShapeDtypeStruct((t, IN_COLS), BF16)]
    out_specs = [pl.BlockSpec((tm, IN_COLS), lambda i: (i, 0))]
    if states:
        assert tm == seq
        b = t // seq
        for nh in (N_KV_A, N_KV_A, N_HEADS_B, N_HEADS_B):
            out_shape.append(jax.ShapeDtypeStruct((b, 1, nh, seq, HEAD_DIM), F32))
            out_specs.append(pl.BlockSpec((1, 1, nh, seq, HEAD_DIM), lambda i: (i, 0, 0, 0, 0)))
    return pl.pallas_call(
        functools.partial(_proj_kernel, rope=rope, states=states),
        out_shape=out_shape,
        grid=(t // tm,),
        in_specs=in_specs,
        out_specs=out_specs,
        compiler_params=_cparams(("arbitrary",)),
        name="proj_states" if states else "proj_rope",
    )(*args)


def _lane_half(shape):
    return lax.broadcasted_iota(jnp.int32, shape, 1) // HEAD_DIM


def _keep_half(x, half):
    return jnp.where(_lane_half(x.shape) == half, x, jnp.zeros_like(x))


def _transpose_bf16(x):
    return x.astype(F32).T.astype(BF16)


def _attend(q, keys, values_t, biases):
    scores = []
    for k, b in zip(keys, biases):
        s = _dot_nt(k, q)
        if b is not None:
            s = s + b
        scores.append(s)
    m = functools.reduce(jnp.maximum, [jnp.max(s, axis=0, keepdims=True) for s in scores])
    denom = None
    out = None
    for s, vt in zip(scores, values_t):
        p = jnp.exp(s - m)
        ps = jnp.sum(p, axis=0, keepdims=True)
        po = _dot(vt, p.astype(BF16))
        denom = ps if denom is None else denom + ps
        out = po if out is None else out + po
    return out / denom


def _swap_halves(q_bf16):
    return pltpu.roll(q_bf16.astype(F32), HEAD_DIM, 1).astype(BF16)


def _gqa_heads(q_of_pair, keys_by_group, values_t):
    outs = []
    for h in range(N_HEADS_A):
        g = h // GROUP_A
        q = q_of_pair(h // 2)
        if h % 2 != g:
            q = _swap_halves(q)
        o = _attend(q, keys_by_group[g], values_t, [None] * len(values_t))
        outs.append(o[g * HEAD_DIM:(g + 1) * HEAD_DIM])
    return jnp.concatenate(outs, axis=0)


def _ctx_attn_kernel(p_ref, oa_ref, ob_ref):
    ka = p_ref[:, COL_KA:COL_KA + LANES]
    va_t = [_transpose_bf16(p_ref[:, COL_VA:COL_VA + LANES])]
    keys_by_group = [[_keep_half(ka, g)] for g in range(N_KV_A)]
    oa = _gqa_heads(lambda i: p_ref[:, COL_QA + i * LANES:COL_QA + (i + 1) * LANES],
                    keys_by_group, va_t)
    oa_ref[...] = oa.T

    outs = []
    for i in range(N_HEADS_B // 2):
        q = p_ref[:, COL_QB + i * LANES:COL_QB + (i + 1) * LANES]
        k = p_ref[:, COL_KB + i * LANES:COL_KB + (i + 1) * LANES]
        vt = [_transpose_bf16(p_ref[:, COL_VB + i * LANES:COL_VB + (i + 1) * LANES])]
        for half in range(2):
            o = _attend(q, [_keep_half(k, half)], vt, [None])
            outs.append(o[half * HEAD_DIM:(half + 1) * HEAD_DIM])
    ob_ref[...] = jnp.concatenate(outs, axis=0).T


def _context_attention(proj, *, seq):
    t = proj.shape[0]
    return pl.pallas_call(
        _ctx_attn_kernel,
        out_shape=[jax.ShapeDtypeStruct((t, WIDTH_A), F32), jax.ShapeDtypeStruct((t, WIDTH_B), F32)],
        grid=(t // seq,),
        in_specs=[pl.BlockSpec((seq, IN_COLS), lambda i: (i, 0))],
        out_specs=[pl.BlockSpec((seq, WIDTH_A), lambda i: (i, 0)),
                   pl.BlockSpec((seq, WIDTH_B), lambda i: (i, 0))],
        compiler_params=_cparams(("arbitrary",)),
        name="context_attention",
    )(proj)


def _gqa_latent_kernel(q_ref, k_ref, v_ref, ck_ref, cv_ref, o_ref, kg_ref, ckg_ref, vt_ref, cvt_ref):
    @pl.when(pl.program_id(1) == 0)
    def _():
        k = k_ref[...]
        ck = ck_ref[0]
        for g in range(N_KV_A):
            kg_ref[g] = _keep_half(k, g)
            ckg_ref[g] = _keep_half(ck, g)
        vt_ref[...] = _transpose_bf16(v_ref[...])
        cvt_ref[...] = _transpose_bf16(cv_ref[0])

    tq = q_ref.shape[0]
    scores = []
    for g in range(N_KV_A):
        qs = []
        for j in range(GROUP_A):
            h = g * GROUP_A + j
            q = q_ref[:, (h // 2) * LANES:(h // 2 + 1) * LANES]
            qs.append(q if h % 2 == g else _swap_halves(q))
        qg = jnp.concatenate(qs, axis=0)
        scores.append((_dot_nt(kg_ref[g], qg), _dot_nt(ckg_ref[g], qg)))
    probs = []
    for s_lat, s_ctx in scores:
        m = jnp.maximum(jnp.max(s_lat, axis=0, keepdims=True), jnp.max(s_ctx, axis=0, keepdims=True))
        p_lat = jnp.exp(s_lat - m)
        p_ctx = jnp.exp(s_ctx - m)
        denom = jnp.sum(p_lat, axis=0, keepdims=True) + jnp.sum(p_ctx, axis=0, keepdims=True)
        probs.append((p_lat.astype(BF16), p_ctx.astype(BF16), denom))
    outs = []
    for g, (p_lat, p_ctx, denom) in enumerate(probs):
        o = _dot(vt_ref[...], p_lat) + _dot(cvt_ref[...], p_ctx)
        o = (o / denom)[g * HEAD_DIM:(g + 1) * HEAD_DIM]
        outs += [o[:, j * tq:(j + 1) * tq] for j in range(GROUP_A)]
    o_ref[...] = jnp.concatenate(outs, axis=0).T


def _latent_gqa(proj, ctx_k, ctx_v, *, seq, tq):
    t = proj.shape[0]
    b = t // seq
    nq = seq // tq
    past = ctx_k.shape[1]
    return pl.pallas_call(
        _gqa_latent_kernel,
        out_shape=jax.ShapeDtypeStruct((t, WIDTH_A), F32),
        grid=(b, nq),
        in_specs=[pl.BlockSpec((tq, WIDTH_A), lambda bi, qi: (bi * nq + qi, 0)),
                  pl.BlockSpec((seq, LANES), lambda bi, qi: (bi, COL_KA // LANES)),
                  pl.BlockSpec((seq, LANES), lambda bi, qi: (bi, COL_VA // LANES)),
                  pl.BlockSpec((1, past, LANES), lambda bi, qi: (bi, 0, 0)),
                  pl.BlockSpec((1, past, LANES), lambda bi, qi: (bi, 0, 0))],
        out_specs=pl.BlockSpec((tq, WIDTH_A), lambda bi, qi: (bi * nq + qi, 0)),
        scratch_shapes=[pltpu.VMEM((N_KV_A, seq, LANES), BF16),
                        pltpu.VMEM((N_KV_A, past, LANES), BF16),
                        pltpu.VMEM((LANES, seq), BF16),
                        pltpu.VMEM((LANES, past), BF16)],
        compiler_params=_cparams(("arbitrary", "arbitrary")),
        name="latent_gqa",
    )(proj, proj, proj, ctx_k, ctx_v)


def _na_kernel(q_ref, k0, k1, k2, k3, v0, v1, v2, v3, ck_ref, cv_ref, bias_ref, o_ref):
    q = q_ref[...]
    ks = [r[...] for r in (k0, k1, k2, k3)] + [ck_ref[0]]
    values_t = [_transpose_bf16(r[...]) for r in (v0, v1, v2, v3)] + [_transpose_bf16(cv_ref[0])]
    outs = []
    for half in range(2):
        keys = [_keep_half(k, half) for k in ks]
        biases = [bias_ref[0, half, j * NA_KBLK:(j + 1) * NA_KBLK, :] for j in range(4)] + [None]
        o = _attend(q, keys, values_t, biases)
        outs.append(o[half * HEAD_DIM:(half + 1) * HEAD_DIM])
    o_ref[...] = jnp.concatenate(outs, axis=0).T


def _na_first_key_block(i, rows):
    per_qblock = NA_QROWS * GRID_W // NA_KBLK
    lead = (NA_KH // 2) * GRID_W // NA_KBLK
    return jnp.clip(per_qblock * i - lead, 0, (rows - NA_KROWS) * GRID_W // NA_KBLK)


def _latent_neighbourhood(proj, ctx_k, ctx_v, bias_t, *, seq):
    t = proj.shape[0]
    b = t // seq
    rows = seq // GRID_W
    nblk = rows // NA_QROWS
    kblk_per_batch = seq // NA_KBLK
    past = ctx_k.shape[1]
    grid = (N_HEADS_B // 2, nblk, b)

    def kv_spec(col0, j):
        return pl.BlockSpec(
            (NA_KBLK, LANES),
            lambda hp, i, bi: (bi * kblk_per_batch + _na_first_key_block(i, rows) + j, col0 // LANES + hp))

    def variant(i):
        return jnp.where(i == 0, 0, jnp.where(i == nblk - 1, 2, 1))

    in_specs = ([pl.BlockSpec((NA_TQ, LANES), lambda hp, i, bi: (bi * nblk + i, COL_QB // LANES + hp))]
                + [kv_spec(COL_KB, j) for j in range(4)]
                + [kv_spec(COL_VB, j) for j in range(4)]
                + [pl.BlockSpec((1, past, LANES), lambda hp, i, bi: (bi, 0, hp)),
                   pl.BlockSpec((1, past, LANES), lambda hp, i, bi: (bi, 0, hp)),
                   pl.BlockSpec((1, 2, NA_TK, NA_TQ), lambda hp, i, bi: (variant(i), hp, 0, 0))])
    return pl.pallas_call(
        _na_kernel,
        out_shape=jax.ShapeDtypeStruct((t, WIDTH_B), F32),
        grid=grid,
        in_specs=in_specs,
        out_specs=pl.BlockSpec((NA_TQ, LANES), lambda hp, i, bi: (bi * nblk + i, hp)),
        compiler_params=_cparams(("arbitrary", "arbitrary", "arbitrary")),
        name="latent_neighbourhood",
    )(proj, *([proj] * 8), ctx_k, ctx_v, bias_t)


def _neighbourhood_bias(rpb, rows):
    nblk = rows // NA_QROWS
    n_dr = 2 * NA_KH - 1
    n_dc = 2 * NA_KW - 1
    kc = np.arange(GRID_W)[:, None]
    qc = np.arange(GRID_W)[None, :]
    ws = np.clip(qc - NA_KW // 2, 0, GRID_W - NA_KW)
    col_ok = (kc >= ws) & (kc < ws + NA_KW)
    dc = np.clip(kc - qc + NA_KW - 1, 0, n_dc - 1)
    dc_onehot = (dc[None] == np.arange(n_dc)[:, None, None]).astype(np.float32)
    tiles = jnp.einsum('hab,bkq->hakq', rpb.astype(F32), jnp.asarray(dc_onehot),
                       precision=lax.Precision.HIGHEST)
    tiles = jnp.where(jnp.asarray(col_ok)[None, None], tiles, MASKED)
    masked_tile = jnp.full((rpb.shape[0], 1, GRID_W, GRID_W), MASKED, F32)
    tiles = jnp.concatenate([tiles, masked_tile], axis=1)
    pick = np.zeros((3, NA_KROWS, NA_QROWS, n_dr + 1), np.float32)
    for v, i in enumerate((0, 1, nblk - 1)):
        r0 = i * NA_QROWS
        ks = int(np.clip(r0 - NA_KH // 2, 0, rows - NA_KROWS))
        for kl in range(NA_KROWS):
            for ql in range(NA_QROWS):
                kr, qr = ks + kl, r0 + ql
                rs = int(np.clip(qr - NA_KH // 2, 0, rows - NA_KH))
                ok = rs <= kr < rs + NA_KH
                pick[v, kl, ql, (kr - qr + NA_KH - 1) if ok else n_dr] = 1.0
    bias = jnp.einsum('vkqa,hacd->vhkcqd', jnp.asarray(pick), tiles, precision=lax.Precision.HIGHEST)
    return bias.reshape(3, rpb.shape[0], NA_TK, NA_TQ)


def _merge_kernel(xp_ref, oap_ref, obp_ref, xs_ref, oas_ref, obs_ref, ona_ref, onb_ref, wo_ref,
                  g1_ref, sh2_ref, sc2_ref, n2_ref, y_ref, h_ref, *, ctx_tiles):
    def one_stream(x_ref, oa_ref, ob_ref):
        na = (_rms(oa_ref[...]) * ona_ref[...]).astype(BF16)
        nb = (_rms(ob_ref[...]) * onb_ref[...]).astype(BF16)
        mix = _dot(na, wo_ref[0:WIDTH_A, :]) + _dot(nb, wo_ref[WIDTH_A:WIDTH_A + WIDTH_B, :])
        y = x_ref[...] + g1_ref[0] * mix
        y_ref[...] = y
        h = _rms(y) * n2_ref[...]
        h_ref[...] = h * (1.0 + sc2_ref[0]) + sh2_ref[0]

    i = pl.program_id(0)
    pl.when(i < ctx_tiles)(lambda: one_stream(xp_ref, oap_ref, obp_ref))
    pl.when(i >= ctx_tiles)(lambda: one_stream(xs_ref, oas_ref, obs_ref))


def _merge(ctx, lat, on_a, on_b, w_out_bf, gate1, shift2, scale2, norm2, *, tm, lat_seq):
    t_c, d = ctx[0].shape
    t_l = lat[0].shape[0]
    ctx_tiles = t_c // tm
    lat_tiles_per_batch = lat_seq // tm

    def ctx_map(i):
        return (jnp.minimum(i, ctx_tiles - 1), 0)

    def lat_map(i):
        return (jnp.maximum(i - ctx_tiles, 0), 0)

    def mod_map(i):
        return (jnp.where(i < ctx_tiles, 0, 1 + (i - ctx_tiles) // lat_tiles_per_batch), 0, 0)

    def stream_specs(index_map):
        return [pl.BlockSpec((tm, d), index_map),
                pl.BlockSpec((tm, WIDTH_A), index_map),
                pl.BlockSpec((tm, WIDTH_B), index_map)]

    return pl.pallas_call(
        functools.partial(_merge_kernel, ctx_tiles=ctx_tiles),
        out_shape=[jax.ShapeDtypeStruct((t_c + t_l, d), F32)] * 2,
        grid=((t_c + t_l) // tm,),
        in_specs=stream_specs(ctx_map) + stream_specs(lat_map) + [
            pl.BlockSpec((1, WIDTH_A), lambda i: (0, 0)),
            pl.BlockSpec((1, WIDTH_B), lambda i: (0, 0)),
            pl.BlockSpec((WIDTH_A + WIDTH_B, d), lambda i: (0, 0)),
            pl.BlockSpec((1, 1, d), mod_map),
            pl.BlockSpec((1, 1, d), mod_map),
            pl.BlockSpec((1, 1, d), mod_map),
            pl.BlockSpec((1, d), lambda i: (0, 0))],
        out_specs=[pl.BlockSpec((tm, d), lambda i: (i, 0))] * 2,
        compiler_params=_cparams(("arbitrary",)),
        name="merge_out_proj",
    )(*ctx, *lat, on_a, on_b, w_out_bf, gate1, shift2, scale2, norm2)


def _first_index_of_max(x, iota):
    mx = jnp.max(x, axis=0, keepdims=True)
    idx = jnp.min(jnp.where(x == mx, iota, float(x.shape[0])), axis=0, keepdims=True)
    return mx, iota == idx


def _router_gates(h, wr_hi, wr_lo, rbias):
    h_hi = h.astype(BF16)
    h_lo = (h - h_hi.astype(F32)).astype(BF16)
    logits = _dot_nt(wr_hi, h_hi) + (_dot_nt(wr_lo, h_hi) + _dot_nt(wr_hi, h_lo))
    scores = _sigmoid(logits)
    sel = scores + rbias
    tm = sel.shape[1]
    iota_g = lax.broadcasted_iota(jnp.int32, (GROUP_SIZE, tm), 0).astype(F32)
    group_scores = []
    for g in range(N_GROUPS):
        grp = sel[g * GROUP_SIZE:(g + 1) * GROUP_SIZE]
        m1, first = _first_index_of_max(grp, iota_g)
        m2 = jnp.max(jnp.where(first, -jnp.inf, grp), axis=0, keepdims=True)
        group_scores.append(m1 + m2)
    gs = jnp.concatenate(group_scores, axis=0)
    iota_n = lax.broadcasted_iota(jnp.int32, (N_GROUPS, tm), 0).astype(F32)
    group_on = jnp.zeros((N_GROUPS, tm), F32)
    for _ in range(TOPK_GROUPS):
        _, pick = _first_index_of_max(gs, iota_n)
        group_on = jnp.where(pick, 1.0, group_on)
        gs = jnp.where(pick, -jnp.inf, gs)
    expert_on = jnp.concatenate(
        [jnp.broadcast_to(group_on[g:g + 1], (GROUP_SIZE, tm)) for g in range(N_GROUPS)], axis=0)
    cand = jnp.where(expert_on > 0.0, sel, -jnp.inf)
    iota_e = lax.broadcasted_iota(jnp.int32, (N_EXPERTS, tm), 0).astype(F32)
    w = jnp.zeros((N_EXPERTS, tm), F32)
    chosen = jnp.zeros((N_EXPERTS, tm), F32)
    for _ in range(TOP_K):
        _, pick = _first_index_of_max(cand, iota_e)
        w = jnp.where(pick, scores, w)
        chosen = jnp.where(pick, 1.0, chosen)
        cand = jnp.where(pick, -jnp.inf, cand)
    return w / jnp.sum(w, axis=0, keepdims=True) * ROUTED_SCALE, chosen


MOE_TS = 512
MOE_ROUTE_TM = 512
MOE_ROW_TM = 256


def _route_kernel(h_ref, wrh_ref, wrl_ref, rb_ref, tri_ref, gates_ref, rank_ref, count_ref):
    @pl.when(pl.program_id(0) == 0)
    def _():
        count_ref[...] = jnp.zeros_like(count_ref)

    gates, chosen = _router_gates(h_ref[...], wrh_ref[...], wrl_ref[...], rb_ref[...])
    gates_ref[...] = gates
    before = _dot(chosen.astype(BF16), tri_ref[...])
    seen = count_ref[...]
    rank_ref[...] = jnp.where(chosen > 0.0, before + seen[:, 0:1], -1.0)
    count_ref[...] = seen + jnp.sum(chosen, axis=1, keepdims=True)


def _route(h_all, wr_hi, wr_lo, rbias):
    t, d = h_all.shape
    tm = MOE_ROUTE_TM
    tri = jnp.asarray(np.triu(np.ones((tm, tm), np.float32), k=1), BF16)
    return pl.pallas_call(
        _route_kernel,
        out_shape=[jax.ShapeDtypeStruct((N_EXPERTS, t), F32),
                   jax.ShapeDtypeStruct((N_EXPERTS, t), F32),
                   jax.ShapeDtypeStruct((N_EXPERTS, LANES), F32)],
        grid=(t // tm,),
        in_specs=[pl.BlockSpec((tm, d), lambda i: (i, 0)),
                  pl.BlockSpec((N_EXPERTS, d), lambda i: (0, 0)),
                  pl.BlockSpec((N_EXPERTS, d), lambda i: (0, 0)),
                  pl.BlockSpec((N_EXPERTS, 1), lambda i: (0, 0)),
                  pl.BlockSpec((tm, tm), lambda i: (0, 0))],
        out_specs=[pl.BlockSpec((N_EXPERTS, tm), lambda i: (0, i)),
                   pl.BlockSpec((N_EXPERTS, tm), lambda i: (0, i)),
                   pl.BlockSpec((N_EXPERTS, LANES), lambda i: (0, 0))],
        compiler_params=_cparams(("arbitrary",)),
        name="moe_route",
    )(h_all, wr_hi, wr_lo, rbias, tri)


def _slots_kernel(gates_ref, rank_ref, off_ref, pos_ref, gtok_ref):
    gates = gates_ref[...]
    rank = rank_ref[...]
    tm = gates.shape[1]
    slot = off_ref[...] + rank
    left = jnp.where(rank >= 0.0, 1.0, 0.0)
    iota_e = lax.broadcasted_iota(jnp.int32, (N_EXPERTS, tm), 0).astype(F32)
    pos_rows, gate_rows = [], []
    for _ in range(TOP_K):
        _, pick = _first_index_of_max(left, iota_e)
        pos_rows.append(jnp.sum(jnp.where(pick, slot, 0.0), axis=0, keepdims=True))
        gate_rows.append(jnp.sum(jnp.where(pick, gates, 0.0), axis=0, keepdims=True))
        left = jnp.where(pick, 0.0, left)
    pos_ref[...] = jnp.concatenate(pos_rows, axis=0).astype(jnp.int32)
    pad = jnp.zeros((LANES - TOP_K, tm), F32)
    gtok_ref[...] = jnp.concatenate(gate_rows + [pad], axis=0).T


def _slots(gates_t, rank_t, off):
    t = gates_t.shape[1]
    tm = MOE_ROUTE_TM
    return pl.pallas_call(
        _slots_kernel,
        out_shape=[jax.ShapeDtypeStruct((TOP_K, t), jnp.int32), jax.ShapeDtypeStruct((t, LANES), F32)],
        grid=(t // tm,),
        in_specs=[pl.BlockSpec((N_EXPERTS, tm), lambda i: (0, i)),
                  pl.BlockSpec((N_EXPERTS, tm), lambda i: (0, i)),
                  pl.BlockSpec((N_EXPERTS, 1), lambda i: (0, 0))],
        out_specs=[pl.BlockSpec((TOP_K, tm), lambda i: (0, i)),
                   pl.BlockSpec((tm, LANES), lambda i: (i, 0))],
        compiler_params=_cparams(("arbitrary",)),
        name="moe_slots",
    )(gates_t, rank_t, off)


def _row_copy(src_ref, src_row, dst_ref, dst_row, sem):
    return pltpu.make_async_copy(src_ref.at[pl.ds(src_row, 1)], dst_ref.at[pl.ds(dst_row, 1)], sem)


def _dispatch_kernel(fill_lo_ref, fill_hi_ref, h_ref, pos_ref, xs_ref, sem):
    tm = h_ref.shape[0]

    def start(t, c):
        for k in range(TOP_K):
            _row_copy(h_ref, t, xs_ref, pos_ref[k, t], sem).start(priority=k % 2)
        return c

    lax.fori_loop(0, tm, start, 0, unroll=8)

    def wait(t, c):
        for k in range(TOP_K):
            _row_copy(h_ref, 0, xs_ref, 0, sem).wait()
        return c

    lax.fori_loop(0, tm, wait, 0, unroll=8)

    @pl.when(pl.program_id(0) == 0)
    def _():
        def fill(e, c):
            lo = fill_lo_ref[e]
            hi = fill_hi_ref[e]

            def one(r, c2):
                _row_copy(h_ref, 0, xs_ref, r, sem).start()
                return c2

            def one_wait(r, c2):
                _row_copy(h_ref, 0, xs_ref, 0, sem).wait()
                return c2

            lax.fori_loop(lo, hi, one, 0)
            lax.fori_loop(lo, hi, one_wait, 0)
            return c

        lax.fori_loop(0, N_EXPERTS, fill, 0)


def _dispatch(h_all, pos, fill_lo, fill_hi, n_slots):
    t, d = h_all.shape
    tm = MOE_ROW_TM
    return pl.pallas_call(
        _dispatch_kernel,
        out_shape=jax.ShapeDtypeStruct((n_slots, d), F32),
        grid_spec=pltpu.PrefetchScalarGridSpec(
            num_scalar_prefetch=2,
            grid=(t // tm,),
            in_specs=[pl.BlockSpec((tm, d), lambda i, lo, hi: (i, 0)),
                      pl.BlockSpec((TOP_K, tm), lambda i, lo, hi: (0, i), memory_space=pltpu.SMEM)],
            out_specs=pl.BlockSpec(memory_space=pl.ANY),
            scratch_shapes=[pltpu.SemaphoreType.DMA(())]),
        compiler_params=_cparams(("arbitrary",)),
        name="moe_dispatch",
    )(fill_lo, fill_hi, h_all, pos)


def _experts_kernel(te_ref, xs_ref, wg_ref, wu_ref, wd_ref, ys_ref, wg_bf, wu_bf, wd_bf):
    i = pl.program_id(0)

    @pl.when((i == 0) | (te_ref[i] != te_ref[jnp.maximum(i, 1) - 1]))
    def _():
        wg_bf[...] = wg_ref[0].astype(BF16)
        wu_bf[...] = wu_ref[0].astype(BF16)
        wd_bf[...] = wd_ref[0].astype(BF16)

    x = xs_ref[...].astype(BF16)
    g = _dot(x, wg_bf[...])
    u = _dot(x, wu_bf[...])
    act = (g * _sigmoid(g)) * u
    ys_ref[...] = _dot(act.astype(BF16), wd_bf[...])


def _experts(xs, tile_expert, w_gate, w_up, w_down):
    n_slots, d = xs.shape
    ts = MOE_TS
    return pl.pallas_call(
        _experts_kernel,
        out_shape=jax.ShapeDtypeStruct((n_slots, d), F32),
        grid_spec=pltpu.PrefetchScalarGridSpec(
            num_scalar_prefetch=1,
            grid=(n_slots // ts,),
            in_specs=[pl.BlockSpec((ts, d), lambda i, te: (i, 0)),
                      pl.BlockSpec((1, d, D_EXPERT), lambda i, te: (te[i], 0, 0)),
                      pl.BlockSpec((1, d, D_EXPERT), lambda i, te: (te[i], 0, 0)),
                      pl.BlockSpec((1, D_EXPERT, d), lambda i, te: (te[i], 0, 0))],
            out_specs=pl.BlockSpec((ts, d), lambda i, te: (i, 0)),
            scratch_shapes=[pltpu.VMEM((d, D_EXPERT), BF16), pltpu.VMEM((d, D_EXPERT), BF16),
                            pltpu.VMEM((D_EXPERT, d), BF16)]),
        compiler_params=_cparams(("arbitrary",)),
        name="moe_experts",
    )(tile_expert, xs, w_gate, w_up, w_down)


def _combine_kernel(y_ref, h_ref, g2_ref, gtok_ref, pos_ref, wgs_ref, wus_ref, wds_ref, ys_ref, o_ref,
                    rows_ref, sem):
    tm = y_ref.shape[0]

    def start(t, c):
        for k in range(TOP_K):
            _row_copy(ys_ref, pos_ref[k, t], rows_ref.at[k], t, sem).start(priority=k % 2)
        return c

    lax.fori_loop(0, tm, start, 0, unroll=8)

    h = h_ref[...].astype(BF16)
    gs = _dot(h, wgs_ref[...])
    us = _dot(h, wus_ref[...])
    acc = _dot(((gs * _sigmoid(gs)) * us).astype(BF16), wds_ref[...])

    def wait(t, c):
        for k in range(TOP_K):
            _row_copy(ys_ref, 0, rows_ref.at[k], 0, sem).wait()
        return c

    lax.fori_loop(0, tm, wait, 0, unroll=8)

    gtok = gtok_ref[...]
    for k in range(TOP_K):
        acc = acc + gtok[:, k:k + 1] * rows_ref[k]
    o_ref[...] = y_ref[...] + g2_ref[0] * acc


def _combine(y_all, h_all, gate2, gtok, pos, ys, wgs, wus, wds, *, first_token, tokens, seq):
    d = y_all.shape[1]
    tm = MOE_ROW_TM
    tile0 = first_token // tm
    nb = gate2.shape[0]
    tiles_per_batch = seq // tm

    def mod_map(i):
        return ((i // tiles_per_batch) if nb > 1 else 0, 0, 0)

    return pl.pallas_call(
        _combine_kernel,
        out_shape=jax.ShapeDtypeStruct((tokens, d), F32),
        grid=(tokens // tm,),
        in_specs=[pl.BlockSpec((tm, d), lambda i: (tile0 + i, 0)),
                  pl.BlockSpec((tm, d), lambda i: (tile0 + i, 0)),
                  pl.BlockSpec((1, 1, d), mod_map),
                  pl.BlockSpec((tm, LANES), lambda i: (tile0 + i, 0)),
                  pl.BlockSpec((TOP_K, tm), lambda i: (0, tile0 + i), memory_space=pltpu.SMEM),
                  pl.BlockSpec((d, D_SHARED), lambda i: (0, 0)),
                  pl.BlockSpec((d, D_SHARED), lambda i: (0, 0)),
                  pl.BlockSpec((D_SHARED, d), lambda i: (0, 0)),
                  pl.BlockSpec(memory_space=pl.ANY)],
        out_specs=pl.BlockSpec((tm, d), lambda i: (i, 0)),
        scratch_shapes=[pltpu.VMEM((TOP_K, tm, d), F32), pltpu.SemaphoreType.DMA(())],
        compiler_params=_cparams(("arbitrary",)),
        name="moe_combine",
    )(y_all, h_all, gate2, gtok, pos, wgs, wus, wds, ys)


def _expert_layout(counts, n_tiles):
    cnt = counts.astype(jnp.int32)
    tiles = (cnt + (MOE_TS - 1)) // MOE_TS
    last_tile = jnp.cumsum(tiles)
    off = (last_tile - tiles) * MOE_TS
    pad_end = (off + tiles * MOE_TS).at[N_EXPERTS - 1].set(n_tiles * MOE_TS)
    tile_expert = jnp.minimum(
        jnp.searchsorted(last_tile, jnp.arange(n_tiles, dtype=jnp.int32), side='right'),
        N_EXPERTS - 1).astype(jnp.int32)
    return off, off + cnt, pad_end, tile_expert


def _rope_tables(n_tokens):
    t = jnp.arange(n_tokens)
    row = (t // GRID_W).astype(F32)
    col = (t % GRID_W).astype(F32)
    nf = HEAD_DIM // 4
    freqs = ROPE_THETA ** (-jnp.arange(nf, dtype=F32) / nf)
    ang_r = row[:, None] * freqs
    ang_c = col[:, None] * freqs
    cos = jnp.concatenate([jnp.cos(ang_r)] * 2 + [jnp.cos(ang_c)] * 2, axis=1)
    sin = jnp.concatenate([-jnp.sin(ang_r), jnp.sin(ang_r), -jnp.sin(ang_c), jnp.sin(ang_c)], axis=1)
    reps = LANES // HEAD_DIM
    return jnp.tile(cos, (1, reps)), jnp.tile(sin, (1, reps))


def _head_gains(qn_a, kn_a, qn_b, kn_b):
    ones = jnp.ones((HEAD_DIM,), F32)
    parts = ([qn_a] * N_HEADS_A + [kn_a] * N_KV_A + [ones] * N_KV_A
             + [qn_b] * N_HEADS_B + [kn_b] * N_HEADS_B + [ones] * N_HEADS_B)
    return jnp.concatenate(parts).reshape(1, IN_COLS).astype(F32)


def _same_head_indicator():
    i = np.arange(LANES)
    return jnp.asarray((i[:, None] // HEAD_DIM) == (i[None, :] // HEAD_DIM), BF16)


def _token_major(cache):
    b, h, s, hd = cache.shape
    return cache.transpose(0, 2, 1, 3).reshape(b, s, h * hd).astype(BF16)


def kernel(x_prompt, x_sample, cache_k_a, cache_v_a, cache_k_b, cache_v_b, c, c_ctx, w_mod, b_mod, norm1, norm2, w_in, qn_a, kn_a, qn_b, kn_b, rpb, on_a, on_b, w_out, w_router, router_bias, w_gate_e, w_up_e, w_down_e, w_gate_s, w_up_s, w_down_s):
    depth = w_mod.shape[0]
    assert depth == 1
    l = 0
    bp, sp, d = x_prompt.shape
    bs, ss, _ = x_sample.shape

    cvec = jnp.concatenate([c_ctx[None, :], c], axis=0)
    rows = -(-cvec.shape[0] // 8) * 8
    cvec = jnp.pad(cvec, ((0, rows - cvec.shape[0]), (0, 0)))
    mod = _adaln(cvec, w_mod[l], b_mod[l])
    mod_p = [m.reshape(1, 1, d) for m in jnp.split(mod[0:1], 6, axis=-1)]
    mod_s = [m.reshape(bs, 1, d) for m in jnp.split(mod[1:1 + bs], 6, axis=-1)]
    mod_all = [m.reshape(1 + bs, 1, d) for m in jnp.split(mod[0:1 + bs], 6, axis=-1)]

    w_in_bf = w_in[l].astype(BF16)
    w_out_bf = w_out[l].astype(BF16)
    gain = _head_gains(qn_a[l], kn_a[l], qn_b[l], kn_b[l])
    seg = _same_head_indicator()
    n1 = norm1[l].reshape(1, d)
    n2 = norm2[l].reshape(1, d)
    ona = on_a[l].reshape(1, WIDTH_A)
    onb = on_b[l].reshape(1, WIDTH_B)
    wr_t = w_router[l].T
    wr_hi = wr_t.astype(BF16)
    wr_lo = (wr_t - wr_hi.astype(F32)).astype(BF16)
    rbias = router_bias[l].reshape(N_EXPERTS, 1).astype(F32)
    wgs = w_gate_s[l].astype(BF16)
    wus = w_up_s[l].astype(BF16)
    wds = w_down_s[l].astype(BF16)
    t_p = bp * sp
    t_s = bs * ss
    t_all = t_p + t_s

    xp = x_prompt.reshape(t_p, d)
    proj_p, st_ka, st_va, st_kb, st_vb = _project(
        xp, mod_p[0], mod_p[1], n1, w_in_bf, gain, seg, None, tm=sp, seq=sp, states=True)
    oa_p, ob_p = _context_attention(proj_p, seq=sp)

    xs = x_sample.reshape(t_s, d)
    proj_s, = _project(xs, mod_s[0], mod_s[1], n1, w_in_bf, gain, seg, _rope_tables(ss),
                       tm=512, seq=ss, states=False)
    oa_s = _latent_gqa(proj_s, _token_major(cache_k_a[:, l]), _token_major(cache_v_a[:, l]), seq=ss, tq=128)
    bias_t = _neighbourhood_bias(rpb[l], ss // GRID_W)
    ob_s = _latent_neighbourhood(proj_s, _token_major(cache_k_b[:, l]), _token_major(cache_v_b[:, l]),
                                 bias_t, seq=ss)

    y1_all, h_all = _merge((xp, oa_p, ob_p), (xs, oa_s, ob_s), ona, onb, w_out_bf,
                           mod_all[2], mod_all[3], mod_all[4], n2, tm=512, lat_seq=ss)
    gates_t, rank_t, counts = _route(h_all, wr_hi, wr_lo, rbias)
    n_tiles = t_all * TOP_K // MOE_TS + N_EXPERTS
    off, fill_lo, fill_hi, tile_expert = _expert_layout(counts[:, 0], n_tiles)
    pos, gtok = _slots(gates_t, rank_t, off.astype(F32).reshape(N_EXPERTS, 1))
    x_slots = _dispatch(h_all, pos, fill_lo, fill_hi, n_tiles * MOE_TS)
    y_slots = _experts(x_slots, tile_expert, w_gate_e[l], w_up_e[l], w_down_e[l])
    y_p = _combine(y1_all, h_all, mod_p[5], gtok, pos, y_slots, wgs, wus, wds,
                   first_token=0, tokens=t_p, seq=sp)
    y_s = _combine(y1_all, h_all, mod_s[5], gtok, pos, y_slots, wgs, wus, wds,
                   first_token=t_p, tokens=t_s, seq=ss)

    return (y_p.reshape(bp, sp, d), y_s.reshape(bs, ss, d), st_ka, st_va, st_kb, st_vb)
```

```python
import functools

import numpy as np
import jax
import jax.numpy as jnp
from jax import lax
from jax.experimental import pallas as pl
from jax.experimental.pallas import tpu as pltpu

F32 = jnp.float32
BF16 = jnp.bfloat16

D_MODEL = 1024
HEAD_DIM = 64
N_HEADS_A = 8
N_KV_A = 2
GROUP_A = N_HEADS_A // N_KV_A
N_HEADS_B = 8
WIDTH_A = N_HEADS_A * HEAD_DIM
WIDTH_B = N_HEADS_B * HEAD_DIM
KV_WIDTH_A = N_KV_A * HEAD_DIM
IN_COLS = WIDTH_A + 2 * KV_WIDTH_A + 3 * WIDTH_B
GRID_W = 64
ROPE_THETA = 10000.0
NA_KH = 8
NA_KW = 16
N_EXPERTS = 64
N_GROUPS = 8
GROUP_SIZE = N_EXPERTS // N_GROUPS
TOPK_GROUPS = 4
TOP_K = 8
D_EXPERT = 256
D_SHARED = 256
ROUTED_SCALE = 2.5
EPS = 1e-6

LANES = 128
MXU_DIM = 256
MASKED = -1e30

COL_QA = 0
COL_KA = WIDTH_A
COL_VA = COL_KA + KV_WIDTH_A
COL_QB = COL_VA + KV_WIDTH_A
COL_KB = COL_QB + WIDTH_B
COL_VB = COL_KB + WIDTH_B

NA_QROWS = 8
NA_KROWS = 2 * NA_KH
NA_TQ = NA_QROWS * GRID_W
NA_TK = NA_KROWS * GRID_W
NA_KBLK = 256

VMEM_LIMIT = 56 * 1024 * 1024


def _cparams(sem):
    return pltpu.CompilerParams(dimension_semantics=sem, vmem_limit_bytes=VMEM_LIMIT)


def _dot(a, b):
    return jnp.dot(a, b, preferred_element_type=F32)


def _dot_nt(a, b):
    return lax.dot_general(a, b, (((1,), (1,)), ((), ())), preferred_element_type=F32)


def _sigmoid(x):
    return 1.0 / (1.0 + jnp.exp(-x))


def _rms(x):
    return x * lax.rsqrt(jnp.mean(x * x, axis=-1, keepdims=True) + EPS)


def _mod_kernel(c_ref, w_ref, b_ref, o_ref):
    c = c_ref[...]
    s = c * _sigmoid(c)
    o_ref[...] = jnp.dot(s, w_ref[...], preferred_element_type=F32,
                         precision=lax.Precision.HIGHEST) + b_ref[...]


def _adaln(cvec, w_mod, b_mod):
    rows, d = cvec.shape
    n = w_mod.shape[1]
    tn = 512
    return pl.pallas_call(
        _mod_kernel,
        out_shape=jax.ShapeDtypeStruct((rows, n), F32),
        grid=(n // tn,),
        in_specs=[pl.BlockSpec((rows, d), lambda j: (0, 0)),
                  pl.BlockSpec((d, tn), lambda j: (0, j)),
                  pl.BlockSpec((1, tn), lambda j: (0, j))],
        out_specs=pl.BlockSpec((rows, tn), lambda j: (0, j)),
        compiler_params=_cparams(("arbitrary",)),
        name="adaln_mod",
    )(cvec, w_mod, b_mod.reshape(1, n))


_PROJ_CHUNKS = (
    [(COL_QA + i * LANES, LANES, True, True) for i in range(WIDTH_A // LANES)]
    + [(COL_KA, LANES, True, True), (COL_VA, LANES, False, False)]
    + [(COL_QB + i * LANES, LANES, True, False) for i in range(WIDTH_B // LANES)]
    + [(COL_KB + i * LANES, LANES, True, False) for i in range(WIDTH_B // LANES)]
    + [(COL_VB + i * LANES, LANES, False, False) for i in range(WIDTH_B // LANES)]
)


def _proj_kernel(*refs, rope, states):
    x_ref, sh_ref, sc_ref, n1_ref, w_ref, gain_ref, seg_ref = refs[:7]
    pos = 7
    if rope:
        cos_ref, sin_ref = refs[pos:pos + 2]
        pos += 2
    out_ref = refs[pos]
    pos += 1
    if states:
        ka_ref, va_ref, kb_ref, vb_ref = refs[pos:pos + 4]

    x = x_ref[...]
    h = _rms(x) * n1_ref[...]
    h = h * (1.0 + sc_ref[0]) + sh_ref[0]
    p = _dot(h.astype(BF16), w_ref[...])
    seg = seg_ref[...]
    if rope:
        cos = cos_ref[...]
        sin = sin_ref[...]
        lane = lax.broadcasted_iota(jnp.int32, cos.shape, 1)
        first_half = (lane % (HEAD_DIM // 2)) < (HEAD_DIM // 4)

    for c0, w, normed, roped in _PROJ_CHUNKS:
        pc = p[:, c0:c0 + w]
        if normed:
            sq = pc * pc
            hi = sq.astype(BF16)
            lo = (sq - hi.astype(F32)).astype(BF16)
            ss = _dot(hi, seg) + _dot(lo, seg)
            pc = pc * lax.rsqrt(ss * (1.0 / HEAD_DIM) + EPS) * gain_ref[:, c0:c0 + w]
        if states:
            if c0 == COL_KA:
                for hh in range(N_KV_A):
                    ka_ref[0, 0, hh] = pc[:, hh * HEAD_DIM:(hh + 1) * HEAD_DIM]
            elif c0 == COL_VA:
                for hh in range(N_KV_A):
                    va_ref[0, 0, hh] = pc[:, hh * HEAD_DIM:(hh + 1) * HEAD_DIM]
            elif COL_KB <= c0 < COL_VB:
                base = (c0 - COL_KB) // HEAD_DIM
                for hh in range(LANES // HEAD_DIM):
                    kb_ref[0, 0, base + hh] = pc[:, hh * HEAD_DIM:(hh + 1) * HEAD_DIM]
            elif c0 >= COL_VB:
                base = (c0 - COL_VB) // HEAD_DIM
                for hh in range(LANES // HEAD_DIM):
                    vb_ref[0, 0, base + hh] = pc[:, hh * HEAD_DIM:(hh + 1) * HEAD_DIM]
        if rope and roped:
            partner = jnp.where(first_half,
                                pltpu.roll(pc, LANES - HEAD_DIM // 4, 1),
                                pltpu.roll(pc, HEAD_DIM // 4, 1))
            pc = pc * cos + partner * sin
        if c0 < COL_KA or COL_QB <= c0 < COL_KB:
            pc = pc * (HEAD_DIM ** -0.5)
        out_ref[:, c0:c0 + w] = pc.astype(BF16)


def _project(x2d, shift, scale, norm1, w_in_bf, gain, seg, rope_tabs, *, tm, seq, states):
    t, d = x2d.shape
    nb = shift.shape[0]
    tiles_per_batch = seq // tm
    rope = rope_tabs is not None

    def mod_map(i):
        return ((i // tiles_per_batch) if nb > 1 else 0, 0, 0)

    in_specs = [pl.BlockSpec((tm, d), lambda i: (i, 0)),
                pl.BlockSpec((1, 1, d), mod_map),
                pl.BlockSpec((1, 1, d), mod_map),
                pl.BlockSpec((1, d), lambda i: (0, 0)),
                pl.BlockSpec((d, IN_COLS), lambda i: (0, 0)),
                pl.BlockSpec((1, IN_COLS), lambda i: (0, 0)),
                pl.BlockSpec((LANES, LANES), lambda i: (0, 0))]
    args = [x2d, shift, scale, norm1, w_in_bf, gain, seg]
    if rope:
        in_specs += [pl.BlockSpec((tm, LANES), lambda i: (i % tiles_per_batch, 0))] * 2
        args += list(rope_tabs)
    out_shape = [jax.ShapeDtypeStruct((t, IN_COLS), BF16)]
    out_specs = [pl.BlockSpec((tm, IN_COLS), lambda i: (i, 0))]
    if states:
        assert tm == seq
        b = t // seq
        for nh in (N_KV_A, N_KV_A, N_HEADS_B, N_HEADS_B):
            out_shape.append(jax.ShapeDtypeStruct((b, 1, nh, seq, HEAD_DIM), F32))
            out_specs.append(pl.BlockSpec((1, 1, nh, seq, HEAD_DIM), lambda i: (i, 0, 0, 0, 0)))
    return pl.pallas_call(
        functools.partial(_proj_kernel, rope=rope, states=states),
        out_shape=out_shape,
        grid=(t // tm,),
        in_specs=in_specs,
        out_specs=out_specs,
        compiler_params=_cparams(("arbitrary",)),
        name="proj_states" if states else "proj_rope",
    )(*args)


def _lane_half(shape):
    return lax.broadcasted_iota(jnp.int32, shape, 1) // HEAD_DIM


def _keep_half(x, half):
    return jnp.where(_lane_half(x.shape) == half, x, jnp.zeros_like(x))


def _transpose_bf16(x):
    return x.astype(F32).T.astype(BF16)


def _attend(q, keys, values_t, biases):
    scores = []
    for k, b in zip(keys, biases):
        s = _dot_nt(k, q)
        if b is not None:
            s = s + b
        scores.append(s)
    m = functools.reduce(jnp.maximum, [jnp.max(s, axis=0, keepdims=True) for s in scores])
    denom = None
    out = None
    for s, vt in zip(scores, values_t):
        p = jnp.exp(s - m)
        ps = jnp.sum(p, axis=0, keepdims=True)
        po = _dot(vt, p.astype(BF16))
        denom = ps if denom is None else denom + ps
        out = po if out is None else out + po
    return out / denom


def _swap_halves(q_bf16):
    return pltpu.roll(q_bf16.astype(F32), HEAD_DIM, 1).astype(BF16)


def _gqa_heads(q_of_pair, keys_by_group, values_t):
    outs = []
    for h in range(N_HEADS_A):
        g = h // GROUP_A
        q = q_of_pair(h // 2)
        if h % 2 != g:
            q = _swap_halves(q)
        o = _attend(q, keys_by_group[g], values_t, [None] * len(values_t))
        outs.append(o[g * HEAD_DIM:(g + 1) * HEAD_DIM])
    return jnp.concatenate(outs, axis=0)


def _ctx_attn_kernel(p_ref, oa_ref, ob_ref):
    ka = p_ref[:, COL_KA:COL_KA + LANES]
    va_t = [_transpose_bf16(p_ref[:, COL_VA:COL_VA + LANES])]
    keys_by_group = [[_keep_half(ka, g)] for g in range(N_KV_A)]
    oa = _gqa_heads(lambda i: p_ref[:, COL_QA + i * LANES:COL_QA + (i + 1) * LANES],
                    keys_by_group, va_t)
    oa_ref[...] = oa.T

    outs = []
    for i in range(N_HEADS_B // 2):
        q = p_ref[:, COL_QB + i * LANES:COL_QB + (i + 1) * LANES]
        k = p_ref[:, COL_KB + i * LANES:COL_KB + (i + 1) * LANES]
        vt = [_transpose_bf16(p_ref[:, COL_VB + i * LANES:COL_VB + (i + 1) * LANES])]
        for half in range(2):
            o = _attend(q, [_keep_half(k, half)], vt, [None])
            outs.append(o[half * HEAD_DIM:(half + 1) * HEAD_DIM])
    ob_ref[...] = jnp.concatenate(outs, axis=0).T


def _context_attention(proj, *, seq):
    t = proj.shape[0]
    return pl.pallas_call(
        _ctx_attn_kernel,
        out_shape=[jax.ShapeDtypeStruct((t, WIDTH_A), F32), jax.ShapeDtypeStruct((t, WIDTH_B), F32)],
        grid=(t // seq,),
        in_specs=[pl.BlockSpec((seq, IN_COLS), lambda i: (i, 0))],
        out_specs=[pl.BlockSpec((seq, WIDTH_A), lambda i: (i, 0)),
                   pl.BlockSpec((seq, WIDTH_B), lambda i: (i, 0))],
        compiler_params=_cparams(("arbitrary",)),
        name="context_attention",
    )(proj)


def _gqa_latent_kernel(q_ref, k_ref, v_ref, ck_ref, cv_ref, o_ref, kg_ref, ckg_ref, vt_ref, cvt_ref):
    @pl.when(pl.program_id(1) == 0)
    def _():
        k = k_ref[...]
        ck = ck_ref[0]
        for g in range(N_KV_A):
            kg_ref[g] = _keep_half(k, g)
            ckg_ref[g] = _keep_half(ck, g)
        vt_ref[...] = _transpose_bf16(v_ref[...])
        cvt_ref[...] = _transpose_bf16(cv_ref[0])

    tq = q_ref.shape[0]
    scores = []
    for g in range(N_KV_A):
        qs = []
        for j in range(GROUP_A):
            h = g * GROUP_A + j
            q = q_ref[:, (h // 2) * LANES:(h // 2 + 1) * LANES]
            qs.append(q if h % 2 == g else _swap_halves(q))
        qg = jnp.concatenate(qs, axis=0)
        scores.append((_dot_nt(kg_ref[g], qg), _dot_nt(ckg_ref[g], qg)))
    probs = []
    for s_lat, s_ctx in scores:
        m = jnp.maximum(jnp.max(s_lat, axis=0, keepdims=True), jnp.max(s_ctx, axis=0, keepdims=True))
        p_lat = jnp.exp(s_lat - m)
        p_ctx = jnp.exp(s_ctx - m)
        denom = jnp.sum(p_lat, axis=0, keepdims=True) + jnp.sum(p_ctx, axis=0, keepdims=True)
        probs.append((p_lat.astype(BF16), p_ctx.astype(BF16), denom))
    outs = []
    for g, (p_lat, p_ctx, denom) in enumerate(probs):
        o = _dot(vt_ref[...], p_lat) + _dot(cvt_ref[...], p_ctx)
        o = (o / denom)[g * HEAD_DIM:(g + 1) * HEAD_DIM]
        outs += [o[:, j * tq:(j + 1) * tq] for j in range(GROUP_A)]
    o_ref[...] = jnp.concatenate(outs, axis=0).T


def _latent_gqa(proj, ctx_k, ctx_v, *, seq, tq):
    t = proj.shape[0]
    b = t // seq
    nq = seq // tq
    past = ctx_k.shape[1]
    return pl.pallas_call(
        _gqa_latent_kernel,
        out_shape=jax.ShapeDtypeStruct((t, WIDTH_A), F32),
        grid=(b, nq),
        in_specs=[pl.BlockSpec((tq, WIDTH_A), lambda bi, qi: (bi * nq + qi, 0)),
                  pl.BlockSpec((seq, LANES), lambda bi, qi: (bi, COL_KA // LANES)),
                  pl.BlockSpec((seq, LANES), lambda bi, qi: (bi, COL_VA // LANES)),
                  pl.BlockSpec((1, past, LANES), lambda bi, qi: (bi, 0, 0)),
                  pl.BlockSpec((1, past, LANES), lambda bi, qi: (bi, 0, 0))],
        out_specs=pl.BlockSpec((tq, WIDTH_A), lambda bi, qi: (bi * nq + qi, 0)),
        scratch_shapes=[pltpu.VMEM((N_KV_A, seq, LANES), BF16),
                        pltpu.VMEM((N_KV_A, past, LANES), BF16),
                        pltpu.VMEM((LANES, seq), BF16),
                        pltpu.VMEM((LANES, past), BF16)],
        compiler_params=_cparams(("arbitrary", "arbitrary")),
        name="latent_gqa",
    )(proj, proj, proj, ctx_k, ctx_v)


def _na_kernel(q_ref, k0, k1, k2, k3, v0, v1, v2, v3, ck_ref, cv_ref, bias_ref, o_ref):
    q = q_ref[...]
    ks = [r[...] for r in (k0, k1, k2, k3)] + [ck_ref[0]]
    values_t = [_transpose_bf16(r[...]) for r in (v0, v1, v2, v3)] + [_transpose_bf16(cv_ref[0])]
    outs = []
    for half in range(2):
        keys = [_keep_half(k, half) for k in ks]
        biases = [bias_ref[0, half, j * NA_KBLK:(j + 1) * NA_KBLK, :] for j in range(4)] + [None]
        o = _attend(q, keys, values_t, biases)
        outs.append(o[half * HEAD_DIM:(half + 1) * HEAD_DIM])
    o_ref[...] = jnp.concatenate(outs, axis=0).T


def _na_first_key_block(i, rows):
    per_qblock = NA_QROWS * GRID_W // NA_KBLK
    lead = (NA_KH // 2) * GRID_W // NA_KBLK
    return jnp.clip(per_qblock * i - lead, 0, (rows - NA_KROWS) * GRID_W // NA_KBLK)


def _latent_neighbourhood(proj, ctx_k, ctx_v, bias_t, *, seq):
    t = proj.shape[0]
    b = t // seq
    rows = seq // GRID_W
    nblk = rows // NA_QROWS
    kblk_per_batch = seq // NA_KBLK
    past = ctx_k.shape[1]
    grid = (N_HEADS_B // 2, nblk, b)

    def kv_spec(col0, j):
        return pl.BlockSpec(
            (NA_KBLK, LANES),
            lambda hp, i, bi: (bi * kblk_per_batch + _na_first_key_block(i, rows) + j, col0 // LANES + hp))

    def variant(i):
        return jnp.where(i == 0, 0, jnp.where(i == nblk - 1, 2, 1))

    in_specs = ([pl.BlockSpec((NA_TQ, LANES), lambda hp, i, bi: (bi * nblk + i, COL_QB // LANES + hp))]
                + [kv_spec(COL_KB, j) for j in range(4)]
                + [kv_spec(COL_VB, j) for j in range(4)]
                + [pl.BlockSpec((1, past, LANES), lambda hp, i, bi: (bi, 0, hp)),
                   pl.BlockSpec((1, past, LANES), lambda hp, i, bi: (bi, 0, hp)),
                   pl.BlockSpec((1, 2, NA_TK, NA_TQ), lambda hp, i, bi: (variant(i), hp, 0, 0))])
    return pl.pallas_call(
        _na_kernel,
        out_shape=jax.ShapeDtypeStruct((t, WIDTH_B), F32),
        grid=grid,
        in_specs=in_specs,
        out_specs=pl.BlockSpec((NA_TQ, LANES), lambda hp, i, bi: (bi * nblk + i, hp)),
        compiler_params=_cparams(("arbitrary", "arbitrary", "arbitrary")),
        name="latent_neighbourhood",
    )(proj, *([proj] * 8), ctx_k, ctx_v, bias_t)


def _neighbourhood_bias(rpb, rows):
    nblk = rows // NA_QROWS
    n_dr = 2 * NA_KH - 1
    n_dc = 2 * NA_KW - 1
    kc = np.arange(GRID_W)[:, None]
    qc = np.arange(GRID_W)[None, :]
    ws = np.clip(qc - NA_KW // 2, 0, GRID_W - NA_KW)
    col_ok = (kc >= ws) & (kc < ws + NA_KW)
    dc = np.clip(kc - qc + NA_KW - 1, 0, n_dc - 1)
    dc_onehot = (dc[None] == np.arange(n_dc)[:, None, None]).astype(np.float32)
    tiles = jnp.einsum('hab,bkq->hakq', rpb.astype(F32), jnp.asarray(dc_onehot),
                       precision=lax.Precision.HIGHEST)
    tiles = jnp.where(jnp.asarray(col_ok)[None, None], tiles, MASKED)
    masked_tile = jnp.full((rpb.shape[0], 1, GRID_W, GRID_W), MASKED, F32)
    tiles = jnp.concatenate([tiles, masked_tile], axis=1)
    pick = np.zeros((3, NA_KROWS, NA_QROWS, n_dr + 1), np.float32)
    for v, i in enumerate((0, 1, nblk - 1)):
        r0 = i * NA_QROWS
        ks = int(np.clip(r0 - NA_KH // 2, 0, rows - NA_KROWS))
        for kl in range(NA_KROWS):
            for ql in range(NA_QROWS):
                kr, qr = ks + kl, r0 + ql
                rs = int(np.clip(qr - NA_KH // 2, 0, rows - NA_KH))
                ok = rs <= kr < rs + NA_KH
                pick[v, kl, ql, (kr - qr + NA_KH - 1) if ok else n_dr] = 1.0
    bias = jnp.einsum('vkqa,hacd->vhkcqd', jnp.asarray(pick), tiles, precision=lax.Precision.HIGHEST)
    return bias.reshape(3, rpb.shape[0], NA_TK, NA_TQ)


def _merge_kernel(xp_ref, oap_ref, obp_ref, xs_ref, oas_ref, obs_ref, ona_ref, onb_ref, wo_ref,
                  g1_ref, sh2_ref, sc2_ref, n2_ref, y_ref, h_ref, *, ctx_tiles):
    def one_stream(x_ref, oa_ref, ob_ref):
        na = (_rms(oa_ref[...]) * ona_ref[...]).astype(BF16)
        nb = (_rms(ob_ref[...]) * onb_ref[...]).astype(BF16)
        mix = _dot(na, wo_ref[0:WIDTH_A, :]) + _dot(nb, wo_ref[WIDTH_A:WIDTH_A + WIDTH_B, :])
        y = x_ref[...] + g1_ref[0] * mix
        y_ref[...] = y
        h = _rms(y) * n2_ref[...]
        h_ref[...] = h * (1.0 + sc2_ref[0]) + sh2_ref[0]

    i = pl.program_id(0)
    pl.when(i < ctx_tiles)(lambda: one_stream(xp_ref, oap_ref, obp_ref))
    pl.when(i >= ctx_tiles)(lambda: one_stream(xs_ref, oas_ref, obs_ref))


def _merge(ctx, lat, on_a, on_b, w_out_bf, gate1, shift2, scale2, norm2, *, tm, lat_seq):
    t_c, d = ctx[0].shape
    t_l = lat[0].shape[0]
    ctx_tiles = t_c // tm
    lat_tiles_per_batch = lat_seq // tm

    def ctx_map(i):
        return (jnp.minimum(i, ctx_tiles - 1), 0)

    def lat_map(i):
        return (jnp.maximum(i - ctx_tiles, 0), 0)

    def mod_map(i):
        return (jnp.where(i < ctx_tiles, 0, 1 + (i - ctx_tiles) // lat_tiles_per_batch), 0, 0)

    def stream_specs(index_map):
        return [pl.BlockSpec((tm, d), index_map),
                pl.BlockSpec((tm, WIDTH_A), index_map),
                pl.BlockSpec((tm, WIDTH_B), index_map)]

    return pl.pallas_call(
        functools.partial(_merge_kernel, ctx_tiles=ctx_tiles),
        out_shape=[jax.ShapeDtypeStruct((t_c + t_l, d), F32)] * 2,
        grid=((t_c + t_l) // tm,),
        in_specs=stream_specs(ctx_map) + stream_specs(lat_map) + [
            pl.BlockSpec((1, WIDTH_A), lambda i: (0, 0)),
            pl.BlockSpec((1, WIDTH_B), lambda i: (0, 0)),
            pl.BlockSpec((WIDTH_A + WIDTH_B, d), lambda i: (0, 0)),
            pl.BlockSpec((1, 1, d), mod_map),
            pl.BlockSpec((1, 1, d), mod_map),
            pl.BlockSpec((1, 1, d), mod_map),
            pl.BlockSpec((1, d), lambda i: (0, 0))],
        out_specs=[pl.BlockSpec((tm, d), lambda i: (i, 0))] * 2,
        compiler_params=_cparams(("arbitrary",)),
        name="merge_out_proj",
    )(*ctx, *lat, on_a, on_b, w_out_bf, gate1, shift2, scale2, norm2)


def _first_index_of_max(x, iota):
    mx = jnp.max(x, axis=0, keepdims=True)
    idx = jnp.min(jnp.where(x == mx, iota, float(x.shape[0])), axis=0, keepdims=True)
    return mx, iota == idx


def _router_gates(h, wr_hi, wr_lo, rbias):
    h_hi = h.astype(BF16)
    h_lo = (h - h_hi.astype(F32)).astype(BF16)
    logits = _dot_nt(wr_hi, h_hi) + (_dot_nt(wr_lo, h_hi) + _dot_nt(wr_hi, h_lo))
    scores = _sigmoid(logits)
    sel = scores + rbias
    tm = sel.shape[1]
    iota_g = lax.broadcasted_iota(jnp.int32, (GROUP_SIZE, tm), 0).astype(F32)
    group_scores = []
    for g in range(N_GROUPS):
        grp = sel[g * GROUP_SIZE:(g + 1) * GROUP_SIZE]
        m1, first = _first_index_of_max(grp, iota_g)
        m2 = jnp.max(jnp.where(first, -jnp.inf, grp), axis=0, keepdims=True)
        group_scores.append(m1 + m2)
    gs = jnp.concatenate(group_scores, axis=0)
    iota_n = lax.broadcasted_iota(jnp.int32, (N_GROUPS, tm), 0).astype(F32)
    group_on = jnp.zeros((N_GROUPS, tm), F32)
    for _ in range(TOPK_GROUPS):
        _, pick = _first_index_of_max(gs, iota_n)
        group_on = jnp.where(pick, 1.0, group_on)
        gs = jnp.where(pick, -jnp.inf, gs)
    expert_on = jnp.concatenate(
        [jnp.broadcast_to(group_on[g:g + 1], (GROUP_SIZE, tm)) for g in range(N_GROUPS)], axis=0)
    cand = jnp.where(expert_on > 0.0, sel, -jnp.inf)
    iota_e = lax.broadcasted_iota(jnp.int32, (N_EXPERTS, tm), 0).astype(F32)
    w = jnp.zeros((N_EXPERTS, tm), F32)
    chosen = jnp.zeros((N_EXPERTS, tm), F32)
    for _ in range(TOP_K):
        _, pick = _first_index_of_max(cand, iota_e)
        w = jnp.where(pick, scores, w)
        chosen = jnp.where(pick, 1.0, chosen)
        cand = jnp.where(pick, -jnp.inf, cand)
    return w / jnp.sum(w, axis=0, keepdims=True) * ROUTED_SCALE, chosen


MOE_TS = 512
MOE_ROUTE_TM = 512
MOE_ROW_TM = 512


def _route_kernel(h_ref, wrh_ref, wrl_ref, rb_ref, tri_ref, gates_ref, rank_ref, count_ref):
    @pl.when(pl.program_id(0) == 0)
    def _():
        count_ref[...] = jnp.zeros_like(count_ref)

    gates, chosen = _router_gates(h_ref[...], wrh_ref[...], wrl_ref[...], rb_ref[...])
    gates_ref[...] = gates
    before = _dot(chosen.astype(BF16), tri_ref[...])
    seen = count_ref[...]
    rank_ref[...] = jnp.where(chosen > 0.0, before + seen[:, 0:1], -1.0)
    count_ref[...] = seen + jnp.sum(chosen, axis=1, keepdims=True)


def _route(h_all, wr_hi, wr_lo, rbias):
    t, d = h_all.shape
    tm = MOE_ROUTE_TM
    tri = jnp.asarray(np.triu(np.ones((tm, tm), np.float32), k=1), BF16)
    return pl.pallas_call(
        _route_kernel,
        out_shape=[jax.ShapeDtypeStruct((N_EXPERTS, t), F32),
                   jax.ShapeDtypeStruct((N_EXPERTS, t), F32),
                   jax.ShapeDtypeStruct((N_EXPERTS, LANES), F32)],
        grid=(t // tm,),
        in_specs=[pl.BlockSpec((tm, d), lambda i: (i, 0)),
                  pl.BlockSpec((N_EXPERTS, d), lambda i: (0, 0)),
                  pl.BlockSpec((N_EXPERTS, d), lambda i: (0, 0)),
                  pl.BlockSpec((N_EXPERTS, 1), lambda i: (0, 0)),
                  pl.BlockSpec((tm, tm), lambda i: (0, 0))],
        out_specs=[pl.BlockSpec((N_EXPERTS, tm), lambda i: (0, i)),
                   pl.BlockSpec((N_EXPERTS, tm), lambda i: (0, i)),
                   pl.BlockSpec((N_EXPERTS, LANES), lambda i: (0, 0))],
        compiler_params=_cparams(("arbitrary",)),
        name="moe_route",
    )(h_all, wr_hi, wr_lo, rbias, tri)


def _slots_kernel(gates_ref, rank_ref, off_ref, pos_ref, gtok_ref):
    gates = gates_ref[...]
    rank = rank_ref[...]
    tm = gates.shape[1]
    slot = off_ref[...] + rank
    left = jnp.where(rank >= 0.0, 1.0, 0.0)
    iota_e = lax.broadcasted_iota(jnp.int32, (N_EXPERTS, tm), 0).astype(F32)
    pos_rows, gate_rows = [], []
    for _ in range(TOP_K):
        _, pick = _first_index_of_max(left, iota_e)
        pos_rows.append(jnp.sum(jnp.where(pick, slot, 0.0), axis=0, keepdims=True))
        gate_rows.append(jnp.sum(jnp.where(pick, gates, 0.0), axis=0, keepdims=True))
        left = jnp.where(pick, 0.0, left)
    pos_ref[...] = jnp.concatenate(pos_rows, axis=0).astype(jnp.int32)
    pad = jnp.zeros((LANES - TOP_K, tm), F32)
    gtok_ref[...] = jnp.concatenate(gate_rows + [pad], axis=0).T


def _slots(gates_t, rank_t, off):
    t = gates_t.shape[1]
    tm = MOE_ROUTE_TM
    return pl.pallas_call(
        _slots_kernel,
        out_shape=[jax.ShapeDtypeStruct((TOP_K, t), jnp.int32), jax.ShapeDtypeStruct((t, LANES), F32)],
        grid=(t // tm,),
        in_specs=[pl.BlockSpec((N_EXPERTS, tm), lambda i: (0, i)),
                  pl.BlockSpec((N_EXPERTS, tm), lambda i: (0, i)),
                  pl.BlockSpec((N_EXPERTS, 1), lambda i: (0, 0))],
        out_specs=[pl.BlockSpec((TOP_K, tm), lambda i: (0, i)),
                   pl.BlockSpec((tm, LANES), lambda i: (i, 0))],
        compiler_params=_cparams(("arbitrary",)),
        name="moe_slots",
    )(gates_t, rank_t, off)


def _row_copy(src_ref, src_row, dst_ref, dst_row, sem):
    return pltpu.make_async_copy(src_ref.at[pl.ds(src_row, 1)], dst_ref.at[pl.ds(dst_row, 1)], sem)


def _dispatch_kernel(fill_lo_ref, fill_hi_ref, h_ref, pos_ref, xs_ref, sem):
    tm = h_ref.shape[0]

    def start(t, c):
        for k in range(TOP_K):
            _row_copy(h_ref, t, xs_ref, pos_ref[k, t], sem).start(priority=k % 2)
        return c

    lax.fori_loop(0, tm, start, 0, unroll=8)

    def wait(t, c):
        for k in range(TOP_K):
            _row_copy(h_ref, 0, xs_ref, 0, sem).wait()
        return c

    lax.fori_loop(0, tm, wait, 0, unroll=8)

    @pl.when(pl.program_id(0) == 0)
    def _():
        def fill(e, c):
            lo = fill_lo_ref[e]
            hi = fill_hi_ref[e]

            def one(r, c2):
                _row_copy(h_ref, 0, xs_ref, r, sem).start()
                return c2

            def one_wait(r, c2):
                _row_copy(h_ref, 0, xs_ref, 0, sem).wait()
                return c2

            lax.fori_loop(lo, hi, one, 0)
            lax.fori_loop(lo, hi, one_wait, 0)
            return c

        lax.fori_loop(0, N_EXPERTS, fill, 0)


def _dispatch(h_all, pos, fill_lo, fill_hi, n_slots):
    t, d = h_all.shape
    tm = MOE_ROW_TM
    return pl.pallas_call(
        _dispatch_kernel,
        out_shape=jax.ShapeDtypeStruct((n_slots, d), F32),
        grid_spec=pltpu.PrefetchScalarGridSpec(
            num_scalar_prefetch=2,
            grid=(t // tm,),
            in_specs=[pl.BlockSpec((tm, d), lambda i, lo, hi: (i, 0)),
                      pl.BlockSpec((TOP_K, tm), lambda i, lo, hi: (0, i), memory_space=pltpu.SMEM)],
            out_specs=pl.BlockSpec(memory_space=pl.ANY),
            scratch_shapes=[pltpu.SemaphoreType.DMA(())]),
        compiler_params=_cparams(("arbitrary",)),
        name="moe_dispatch",
    )(fill_lo, fill_hi, h_all, pos)


def _experts_kernel(te_ref, xs_ref, wg_ref, wu_ref, wd_ref, ys_ref, wg_bf, wu_bf, wd_bf):
    i = pl.program_id(0)

    @pl.when((i == 0) | (te_ref[i] != te_ref[jnp.maximum(i, 1) - 1]))
    def _():
        wg_bf[...] = wg_ref[0].astype(BF16)
        wu_bf[...] = wu_ref[0].astype(BF16)
        wd_bf[...] = wd_ref[0].astype(BF16)

    x = xs_ref[...].astype(BF16)
    g = _dot(x, wg_bf[...])
    u = _dot(x, wu_bf[...])
    act = (g * _sigmoid(g)) * u
    ys_ref[...] = _dot(act.astype(BF16), wd_bf[...])


def _experts(xs, tile_expert, w_gate, w_up, w_down):
    n_slots, d = xs.shape
    ts = MOE_TS
    return pl.pallas_call(
        _experts_kernel,
        out_shape=jax.ShapeDtypeStruct((n_slots, d), F32),
        grid_spec=pltpu.PrefetchScalarGridSpec(
            num_scalar_prefetch=1,
            grid=(n_slots // ts,),
            in_specs=[pl.BlockSpec((ts, d), lambda i, te: (i, 0)),
                      pl.BlockSpec((1, d, D_EXPERT), lambda i, te: (te[i], 0, 0)),
                      pl.BlockSpec((1, d, D_EXPERT), lambda i, te: (te[i], 0, 0)),
                      pl.BlockSpec((1, D_EXPERT, d), lambda i, te: (te[i], 0, 0))],
            out_specs=pl.BlockSpec((ts, d), lambda i, te: (i, 0)),
            scratch_shapes=[pltpu.VMEM((d, D_EXPERT), BF16), pltpu.VMEM((d, D_EXPERT), BF16),
                            pltpu.VMEM((D_EXPERT, d), BF16)]),
        compiler_params=_cparams(("arbitrary",)),
        name="moe_experts",
    )(tile_expert, xs, w_gate, w_up, w_down)


def _combine_kernel(y_ref, h_ref, g2_ref, gtok_ref, pos_ref, wgs_ref, wus_ref, wds_ref, ys_ref, o_ref,
                    rows_ref, sem):
    tm = y_ref.shape[0]

    def start(t, c):
        for k in range(TOP_K):
            _row_copy(ys_ref, pos_ref[k, t], rows_ref.at[k], t, sem).start(priority=k % 2)
        return c

    lax.fori_loop(0, tm, start, 0, unroll=8)

    h = h_ref[...].astype(BF16)
    gs = _dot(h, wgs_ref[...])
    us = _dot(h, wus_ref[...])
    acc = _dot(((gs * _sigmoid(gs)) * us).astype(BF16), wds_ref[...])

    def wait(t, c):
        for k in range(TOP_K):
            _row_copy(ys_ref, 0, rows_ref.at[k], 0, sem).wait()
        return c

    lax.fori_loop(0, tm, wait, 0, unroll=8)

    gtok = gtok_ref[...]
    for k in range(TOP_K):
        acc = acc + gtok[:, k:k + 1] * rows_ref[k]
    o_ref[...] = y_ref[...] + g2_ref[0] * acc


def _combine(y_all, h_all, gate2, gtok, pos, ys, wgs, wus, wds, *, first_token, tokens, seq):
    d = y_all.shape[1]
    tm = MOE_ROW_TM
    tile0 = first_token // tm
    nb = gate2.shape[0]
    tiles_per_batch = seq // tm

    def mod_map(i):
        return ((i // tiles_per_batch) if nb > 1 else 0, 0, 0)

    return pl.pallas_call(
        _combine_kernel,
        out_shape=jax.ShapeDtypeStruct((tokens, d), F32),
        grid=(tokens // tm,),
        in_specs=[pl.BlockSpec((tm, d), lambda i: (tile0 + i, 0)),
                  pl.BlockSpec((tm, d), lambda i: (tile0 + i, 0)),
                  pl.BlockSpec((1, 1, d), mod_map),
                  pl.BlockSpec((tm, LANES), lambda i: (tile0 + i, 0)),
                  pl.BlockSpec((TOP_K, tm), lambda i: (0, tile0 + i), memory_space=pltpu.SMEM),
                  pl.BlockSpec((d, D_SHARED), lambda i: (0, 0)),
                  pl.BlockSpec((d, D_SHARED), lambda i: (0, 0)),
                  pl.BlockSpec((D_SHARED, d), lambda i: (0, 0)),
                  pl.BlockSpec(memory_space=pl.ANY)],
        out_specs=pl.BlockSpec((tm, d), lambda i: (i, 0)),
        scratch_shapes=[pltpu.VMEM((TOP_K, tm, d), F32), pltpu.SemaphoreType.DMA(())],
        compiler_params=_cparams(("arbitrary",)),
        name="moe_combine",
    )(y_all, h_all, gate2, gtok, pos, wgs, wus, wds, ys)


def _expert_layout(counts, n_tiles):
    cnt = counts.astype(jnp.int32)
    tiles = (cnt + (MOE_TS - 1)) // MOE_TS
    last_tile = jnp.cumsum(tiles)
    off = (last_tile - tiles) * MOE_TS
    pad_end = (off + tiles * MOE_TS).at[N_EXPERTS - 1].set(n_tiles * MOE_TS)
    tile_ids = jnp.arange(n_tiles, dtype=jnp.int32)
    tile_expert = jnp.minimum(
        jnp.sum((last_tile[None, :] <= tile_ids[:, None]).astype(jnp.int32), axis=1), N_EXPERTS - 1)
    return off, off + cnt, pad_end, tile_expert


def _rope_tables(n_tokens):
    t = jnp.arange(n_tokens)
    row = (t // GRID_W).astype(F32)
    col = (t % GRID_W).astype(F32)
    nf = HEAD_DIM // 4
    freqs = ROPE_THETA ** (-jnp.arange(nf, dtype=F32) / nf)
    ang_r = row[:, None] * freqs
    ang_c = col[:, None] * freqs
    cos = jnp.concatenate([jnp.cos(ang_r)] * 2 + [jnp.cos(ang_c)] * 2, axis=1)
    sin = jnp.concatenate([-jnp.sin(ang_r), jnp.sin(ang_r), -jnp.sin(ang_c), jnp.sin(ang_c)], axis=1)
    reps = LANES // HEAD_DIM
    return jnp.tile(cos, (1, reps)), jnp.tile(sin, (1, reps))


def _head_gains(qn_a, kn_a, qn_b, kn_b):
    ones = jnp.ones((HEAD_DIM,), F32)
    parts = ([qn_a] * N_HEADS_A + [kn_a] * N_KV_A + [ones] * N_KV_A
             + [qn_b] * N_HEADS_B + [kn_b] * N_HEADS_B + [ones] * N_HEADS_B)
    return jnp.concatenate(parts).reshape(1, IN_COLS).astype(F32)


def _same_head_indicator():
    i = np.arange(LANES)
    return jnp.asarray((i[:, None] // HEAD_DIM) == (i[None, :] // HEAD_DIM), BF16)


def _token_major(cache):
    b, h, s, hd = cache.shape
    return cache.transpose(0, 2, 1, 3).reshape(b, s, h * hd).astype(BF16)


def kernel(x_prompt, x_sample, cache_k_a, cache_v_a, cache_k_b, cache_v_b, c, c_ctx, w_mod, b_mod, norm1, norm2, w_in, qn_a, kn_a, qn_b, kn_b, rpb, on_a, on_b, w_out, w_router, router_bias, w_gate_e, w_up_e, w_down_e, w_gate_s, w_up_s, w_down_s):
    depth = w_mod.shape[0]
    assert depth == 1
    l = 0
    bp, sp, d = x_prompt.shape
    bs, ss, _ = x_sample.shape

    cvec = jnp.concatenate([c_ctx[None, :], c], axis=0)
    rows = -(-cvec.shape[0] // 8) * 8
    cvec = jnp.pad(cvec, ((0, rows - cvec.shape[0]), (0, 0)))
    mod = _adaln(cvec, w_mod[l], b_mod[l])
    mod_p = [m.reshape(1, 1, d) for m in jnp.split(mod[0:1], 6, axis=-1)]
    mod_s = [m.reshape(bs, 1, d) for m in jnp.split(mod[1:1 + bs], 6, axis=-1)]
    mod_all = [m.reshape(1 + bs, 1, d) for m in jnp.split(mod[0:1 + bs], 6, axis=-1)]

    w_in_bf = w_in[l].astype(BF16)
    w_out_bf = w_out[l].astype(BF16)
    gain = _head_gains(qn_a[l], kn_a[l], qn_b[l], kn_b[l])
    seg = _same_head_indicator()
    n1 = norm1[l].reshape(1, d)
    n2 = norm2[l].reshape(1, d)
    ona = on_a[l].reshape(1, WIDTH_A)
    onb = on_b[l].reshape(1, WIDTH_B)
    wr_t = w_router[l].T
    wr_hi = wr_t.astype(BF16)
    wr_lo = (wr_t - wr_hi.astype(F32)).astype(BF16)
    rbias = router_bias[l].reshape(N_EXPERTS, 1).astype(F32)
    wgs = w_gate_s[l].astype(BF16)
    wus = w_up_s[l].astype(BF16)
    wds = w_down_s[l].astype(BF16)
    t_p = bp * sp
    t_s = bs * ss
    t_all = t_p + t_s

    xp = x_prompt.reshape(t_p, d)
    proj_p, st_ka, st_va, st_kb, st_vb = _project(
        xp, mod_p[0], mod_p[1], n1, w_in_bf, gain, seg, None, tm=sp, seq=sp, states=True)
    oa_p, ob_p = _context_attention(proj_p, seq=sp)

    xs = x_sample.reshape(t_s, d)
    proj_s, = _project(xs, mod_s[0], mod_s[1], n1, w_in_bf, gain, seg, _rope_tables(ss),
                       tm=512, seq=ss, states=False)
    oa_s = _latent_gqa(proj_s, _token_major(cache_k_a[:, l]), _token_major(cache_v_a[:, l]), seq=ss, tq=128)
    bias_t = _neighbourhood_bias(rpb[l], ss // GRID_W)
    ob_s = _latent_neighbourhood(proj_s, _token_major(cache_k_b[:, l]), _token_major(cache_v_b[:, l]),
                                 bias_t, seq=ss)

    y1_all, h_all = _merge((xp, oa_p, ob_p), (xs, oa_s, ob_s), ona, onb, w_out_bf,
                           mod_all[2], mod_all[3], mod_all[4], n2, tm=512, lat_seq=ss)
    gates_t, rank_t, counts = _route(h_all, wr_hi, wr_lo, rbias)
    n_tiles = t_all * TOP_K // MOE_TS + N_EXPERTS
    off, fill_lo, fill_hi, tile_expert = _expert_layout(counts[:, 0], n_tiles)
    pos, gtok = _slots(gates_t, rank_t, off.astype(F32).reshape(N_EXPERTS, 1))
    x_slots = _dispatch(h_all, pos, fill_lo, fill_hi, n_tiles * MOE_TS)
    y_slots = _experts(x_slots, tile_expert, w_gate_e[l], w_up_e[l], w_down_e[l])
    y_p = _combine(y1_all, h_all, mod_p[5], gtok, pos, y_slots, wgs, wus, wds,
                   first_token=0, tokens=t_p, seq=sp)
    y_s = _combine(y1_all, h_all, mod_s[5], gtok, pos, y_slots, wgs, wus, wds,
                   first_token=t_p, tokens=t_s, seq=ss)

    return (y_p.reshape(bp, sp, d), y_s.reshape(bs, ss, d), st_ka, st_va, st_kb, st_vb)
```

```python
import functools

import numpy as np
import jax
import jax.numpy as jnp
from jax import lax
from jax.experimental import pallas as pl
from jax.experimental.pallas import tpu as pltpu
from jax.experimental.pallas import tpu_sc as plsc

F32 = jnp.float32
BF16 = jnp.bfloat16

D_MODEL = 1024
HEAD_DIM = 64
N_HEADS_A = 8
N_KV_A = 2
GROUP_A = N_HEADS_A // N_KV_A
N_HEADS_B = 8
WIDTH_A = N_HEADS_A * HEAD_DIM
WIDTH_B = N_HEADS_B * HEAD_DIM
KV_WIDTH_A = N_KV_A * HEAD_DIM
IN_COLS = WIDTH_A + 2 * KV_WIDTH_A + 3 * WIDTH_B
GRID_W = 64
ROPE_THETA = 10000.0
NA_KH = 8
NA_KW = 16
N_EXPERTS = 64
N_GROUPS = 8
GROUP_SIZE = N_EXPERTS // N_GROUPS
TOPK_GROUPS = 4
TOP_K = 8
D_EXPERT = 256
D_SHARED = 256
ROUTED_SCALE = 2.5
EPS = 1e-6

LANES = 128
MXU_DIM = 256
MASKED = -1e30

COL_QA = 0
COL_KA = WIDTH_A
COL_VA = COL_KA + KV_WIDTH_A
COL_QB = COL_VA + KV_WIDTH_A
COL_KB = COL_QB + WIDTH_B
COL_VB = COL_KB + WIDTH_B

NA_QROWS = 8
NA_KROWS = 2 * NA_KH
NA_TQ = NA_QROWS * GRID_W
NA_TK = NA_KROWS * GRID_W
NA_KBLK = 256

VMEM_LIMIT = 56 * 1024 * 1024


def _cparams(sem):
    return pltpu.CompilerParams(dimension_semantics=sem, vmem_limit_bytes=VMEM_LIMIT)


def _dot(a, b):
    return jnp.dot(a, b, preferred_element_type=F32)


def _dot_nt(a, b):
    return lax.dot_general(a, b, (((1,), (1,)), ((), ())), preferred_element_type=F32)


def _sigmoid(x):
    return 1.0 / (1.0 + jnp.exp(-x))


def _rms(x):
    return x * lax.rsqrt(jnp.mean(x * x, axis=-1, keepdims=True) + EPS)


def _mod_kernel(c_ref, w_ref, b_ref, o_ref):
    c = c_ref[...]
    s = c * _sigmoid(c)
    o_ref[...] = jnp.dot(s, w_ref[...], preferred_element_type=F32,
                         precision=lax.Precision.HIGHEST) + b_ref[...]


def _adaln(cvec, w_mod, b_mod):
    rows, d = cvec.shape
    n = w_mod.shape[1]
    tn = 512
    return pl.pallas_call(
        _mod_kernel,
        out_shape=jax.ShapeDtypeStruct((rows, n), F32),
        grid=(n // tn,),
        in_specs=[pl.BlockSpec((rows, d), lambda j: (0, 0)),
                  pl.BlockSpec((d, tn), lambda j: (0, j)),
                  pl.BlockSpec((1, tn), lambda j: (0, j))],
        out_specs=pl.BlockSpec((rows, tn), lambda j: (0, j)),
        compiler_params=_cparams(("arbitrary",)),
        name="adaln_mod",
    )(cvec, w_mod, b_mod.reshape(1, n))


_PROJ_CHUNKS = (
    [(COL_QA + i * LANES, LANES, True, True) for i in range(WIDTH_A // LANES)]
    + [(COL_KA, LANES, True, True), (COL_VA, LANES, False, False)]
    + [(COL_QB + i * LANES, LANES, True, False) for i in range(WIDTH_B // LANES)]
    + [(COL_KB + i * LANES, LANES, True, False) for i in range(WIDTH_B // LANES)]
    + [(COL_VB + i * LANES, LANES, False, False) for i in range(WIDTH_B // LANES)]
)


def _proj_kernel(*refs, rope, states):
    x_ref, sh_ref, sc_ref, n1_ref, w_ref, gain_ref, seg_ref = refs[:7]
    pos = 7
    if rope:
        cos_ref, sin_ref = refs[pos:pos + 2]
        pos += 2
    out_ref = refs[pos]
    pos += 1
    if states:
        ka_ref, va_ref, kb_ref, vb_ref = refs[pos:pos + 4]

    x = x_ref[...]
    h = _rms(x) * n1_ref[...]
    h = h * (1.0 + sc_ref[0]) + sh_ref[0]
    p = _dot(h.astype(BF16), w_ref[...])
    seg = seg_ref[...]
    if rope:
        cos = cos_ref[...]
        sin = sin_ref[...]
        lane = lax.broadcasted_iota(jnp.int32, cos.shape, 1)
        first_half = (lane % (HEAD_DIM // 2)) < (HEAD_DIM // 4)

    for c0, w, normed, roped in _PROJ_CHUNKS:
        pc = p[:, c0:c0 + w]
        if normed:
            sq = pc * pc
            hi = sq.astype(BF16)
            lo = (sq - hi.astype(F32)).astype(BF16)
            ss = _dot(hi, seg) + _dot(lo, seg)
            pc = pc * lax.rsqrt(ss * (1.0 / HEAD_DIM) + EPS) * gain_ref[:, c0:c0 + w]
        if states:
            if c0 == COL_KA:
                for hh in range(N_KV_A):
                    ka_ref[0, 0, hh] = pc[:, hh * HEAD_DIM:(hh + 1) * HEAD_DIM]
            elif c0 == COL_VA:
                for hh in range(N_KV_A):
                    va_ref[0, 0, hh] = pc[:, hh * HEAD_DIM:(hh + 1) * HEAD_DIM]
            elif COL_KB <= c0 < COL_VB:
                base = (c0 - COL_KB) // HEAD_DIM
                for hh in range(LANES // HEAD_DIM):
                    kb_ref[0, 0, base + hh] = pc[:, hh * HEAD_DIM:(hh + 1) * HEAD_DIM]
            elif c0 >= COL_VB:
                base = (c0 - COL_VB) // HEAD_DIM
                for hh in range(LANES // HEAD_DIM):
                    vb_ref[0, 0, base + hh] = pc[:, hh * HEAD_DIM:(hh + 1) * HEAD_DIM]
        if rope and roped:
            partner = jnp.where(first_half,
                                pltpu.roll(pc, LANES - HEAD_DIM // 4, 1),
                                pltpu.roll(pc, HEAD_DIM // 4, 1))
            pc = pc * cos + partner * sin
        if c0 < COL_KA or COL_QB <= c0 < COL_KB:
            pc = pc * (HEAD_DIM ** -0.5)
        out_ref[:, c0:c0 + w] = pc.astype(BF16)


def _project(x2d, shift, scale, norm1, w_in_bf, gain, seg, rope_tabs, *, tm, seq, states):
    t, d = x2d.shape
    nb = shift.shape[0]
    tiles_per_batch = seq // tm
    rope = rope_tabs is not None

    def mod_map(i):
        return ((i // tiles_per_batch) if nb > 1 else 0, 0, 0)

    in_specs = [pl.BlockSpec((tm, d), lambda i: (i, 0)),
                pl.BlockSpec((1, 1, d), mod_map),
                pl.BlockSpec((1, 1, d), mod_map),
                pl.BlockSpec((1, d), lambda i: (0, 0)),
                pl.BlockSpec((d, IN_COLS), lambda i: (0, 0)),
                pl.BlockSpec((1, IN_COLS), lambda i: (0, 0)),
                pl.BlockSpec((LANES, LANES), lambda i: (0, 0))]
    args = [x2d, shift, scale, norm1, w_in_bf, gain, seg]
    if rope:
        in_specs += [pl.BlockSpec((tm, LANES), lambda i: (i % tiles_per_batch, 0))] * 2
        args += list(rope_tabs)
    out_shape = [jax.ShapeDtypeStruct((t, IN_COLS), BF16)]
    out_specs = [pl.BlockSpec((tm, IN_COLS), lambda i: (i, 0))]
    if states:
        assert tm == seq
        b = t // seq
        for nh in (N_KV_A, N_KV_A, N_HEADS_B, N_HEADS_B):
            out_shape.append(jax.ShapeDtypeStruct((b, 1, nh, seq, HEAD_DIM), F32))
            out_specs.append(pl.BlockSpec((1, 1, nh, seq, HEAD_DIM), lambda i: (i, 0, 0, 0, 0)))
    return pl.pallas_call(
        functools.partial(_proj_kernel, rope=rope, states=states),
        out_shape=out_shape,
        grid=(t // tm,),
        in_specs=in_specs,
        out_specs=out_specs,
        compiler_params=_cparams(("arbitrary",)),
        name="proj_states" if states else "proj_rope",
    )(*args)


def _lane_half(shape):
    return lax.broadcasted_iota(jnp.int32, shape, 1) // HEAD_DIM


def _keep_half(x, half):
    return jnp.where(_lane_half(x.shape) == half, x, jnp.zeros_like(x))


def _transpose_bf16(x):
    return x.astype(F32).T.astype(BF16)


def _attend(q, keys, values_t, biases):
    scores = []
    for k, b in zip(keys, biases):
        s = _dot_nt(k, q)
        if b is not None:
            s = s + b
        scores.append(s)
    m = functools.reduce(jnp.maximum, [jnp.max(s, axis=0, keepdims=True) for s in scores])
    denom = None
    out = None
    for s, vt in zip(scores, values_t):
        p = jnp.exp(s - m)
        ps = jnp.sum(p, axis=0, keepdims=True)
        po = _dot(vt, p.astype(BF16))
        denom = ps if denom is None else denom + ps
        out = po if out is None else out + po
    return out / denom


def _swap_halves(q_bf16):
    return pltpu.roll(q_bf16.astype(F32), HEAD_DIM, 1).astype(BF16)


def _gqa_heads(q_of_pair, keys_by_group, values_t):
    outs = []
    for h in range(N_HEADS_A):
        g = h // GROUP_A
        q = q_of_pair(h // 2)
        if h % 2 != g:
            q = _swap_halves(q)
        o = _attend(q, keys_by_group[g], values_t, [None] * len(values_t))
        outs.append(o[g * HEAD_DIM:(g + 1) * HEAD_DIM])
    return jnp.concatenate(outs, axis=0)


def _ctx_attn_kernel(p_ref, oa_ref, ob_ref):
    ka = p_ref[:, COL_KA:COL_KA + LANES]
    va_t = [_transpose_bf16(p_ref[:, COL_VA:COL_VA + LANES])]
    keys_by_group = [[_keep_half(ka, g)] for g in range(N_KV_A)]
    oa = _gqa_heads(lambda i: p_ref[:, COL_QA + i * LANES:COL_QA + (i + 1) * LANES],
                    keys_by_group, va_t)
    oa_ref[...] = oa.T

    outs = []
    for i in range(N_HEADS_B // 2):
        q = p_ref[:, COL_QB + i * LANES:COL_QB + (i + 1) * LANES]
        k = p_ref[:, COL_KB + i * LANES:COL_KB + (i + 1) * LANES]
        vt = [_transpose_bf16(p_ref[:, COL_VB + i * LANES:COL_VB + (i + 1) * LANES])]
        for half in range(2):
            o = _attend(q, [_keep_half(k, half)], vt, [None])
            outs.append(o[half * HEAD_DIM:(half + 1) * HEAD_DIM])
    ob_ref[...] = jnp.concatenate(outs, axis=0).T


def _context_attention(proj, *, seq):
    t = proj.shape[0]
    return pl.pallas_call(
        _ctx_attn_kernel,
        out_shape=[jax.ShapeDtypeStruct((t, WIDTH_A), F32), jax.ShapeDtypeStruct((t, WIDTH_B), F32)],
        grid=(t // seq,),
        in_specs=[pl.BlockSpec((seq, IN_COLS), lambda i: (i, 0))],
        out_specs=[pl.BlockSpec((seq, WIDTH_A), lambda i: (i, 0)),
                   pl.BlockSpec((seq, WIDTH_B), lambda i: (i, 0))],
        compiler_params=_cparams(("arbitrary",)),
        name="context_attention",
    )(proj)


def _gqa_latent_kernel(q_ref, k_ref, v_ref, ck_ref, cv_ref, o_ref, kg_ref, ckg_ref, vt_ref, cvt_ref):
    @pl.when(pl.program_id(1) == 0)
    def _():
        k = k_ref[...]
        ck = ck_ref[0]
        for g in range(N_KV_A):
            kg_ref[g] = _keep_half(k, g)
            ckg_ref[g] = _keep_half(ck, g)
        vt_ref[...] = _transpose_bf16(v_ref[...])
        cvt_ref[...] = _transpose_bf16(cv_ref[0])

    tq = q_ref.shape[0]
    scores = []
    for g in range(N_KV_A):
        qs = []
        for j in range(GROUP_A):
            h = g * GROUP_A + j
            q = q_ref[:, (h // 2) * LANES:(h // 2 + 1) * LANES]
            qs.append(q if h % 2 == g else _swap_halves(q))
        qg = jnp.concatenate(qs, axis=0)
        scores.append((_dot_nt(kg_ref[g], qg), _dot_nt(ckg_ref[g], qg)))
    probs = []
    for s_lat, s_ctx in scores:
        m = jnp.maximum(jnp.max(s_lat, axis=0, keepdims=True), jnp.max(s_ctx, axis=0, keepdims=True))
        p_lat = jnp.exp(s_lat - m)
        p_ctx = jnp.exp(s_ctx - m)
        denom = jnp.sum(p_lat, axis=0, keepdims=True) + jnp.sum(p_ctx, axis=0, keepdims=True)
        probs.append((p_lat.astype(BF16), p_ctx.astype(BF16), denom))
    outs = []
    for g, (p_lat, p_ctx, denom) in enumerate(probs):
        o = _dot(vt_ref[...], p_lat) + _dot(cvt_ref[...], p_ctx)
        o = (o / denom)[g * HEAD_DIM:(g + 1) * HEAD_DIM]
        outs += [o[:, j * tq:(j + 1) * tq] for j in range(GROUP_A)]
    o_ref[...] = jnp.concatenate(outs, axis=0).T


def _latent_gqa(proj, ctx_k, ctx_v, *, seq, tq):
    t = proj.shape[0]
    b = t // seq
    nq = seq // tq
    past = ctx_k.shape[1]
    return pl.pallas_call(
        _gqa_latent_kernel,
        out_shape=jax.ShapeDtypeStruct((t, WIDTH_A), F32),
        grid=(b, nq),
        in_specs=[pl.BlockSpec((tq, WIDTH_A), lambda bi, qi: (bi * nq + qi, 0)),
                  pl.BlockSpec((seq, LANES), lambda bi, qi: (bi, COL_KA // LANES)),
                  pl.BlockSpec((seq, LANES), lambda bi, qi: (bi, COL_VA // LANES)),
                  pl.BlockSpec((1, past, LANES), lambda bi, qi: (bi, 0, 0)),
                  pl.BlockSpec((1, past, LANES), lambda bi, qi: (bi, 0, 0))],
        out_specs=pl.BlockSpec((tq, WIDTH_A), lambda bi, qi: (bi * nq + qi, 0)),
        scratch_shapes=[pltpu.VMEM((N_KV_A, seq, LANES), BF16),
                        pltpu.VMEM((N_KV_A, past, LANES), BF16),
                        pltpu.VMEM((LANES, seq), BF16),
                        pltpu.VMEM((LANES, past), BF16)],
        compiler_params=_cparams(("arbitrary", "arbitrary")),
        name="latent_gqa",
    )(proj, proj, proj, ctx_k, ctx_v)


def _na_kernel(q_ref, k0, k1, k2, k3, v0, v1, v2, v3, ck_ref, cv_ref, bias_ref, o_ref):
    q = q_ref[...]
    ks = [r[...] for r in (k0, k1, k2, k3)] + [ck_ref[0]]
    values_t = [_transpose_bf16(r[...]) for r in (v0, v1, v2, v3)] + [_transpose_bf16(cv_ref[0])]
    outs = []
    for half in range(2):
        keys = [_keep_half(k, half) for k in ks]
        biases = [bias_ref[0, half, j * NA_KBLK:(j + 1) * NA_KBLK, :] for j in range(4)] + [None]
        o = _attend(q, keys, values_t, biases)
        outs.append(o[half * HEAD_DIM:(half + 1) * HEAD_DIM])
    o_ref[...] = jnp.concatenate(outs, axis=0).T


def _na_first_key_block(i, rows):
    per_qblock = NA_QROWS * GRID_W // NA_KBLK
    lead = (NA_KH // 2) * GRID_W // NA_KBLK
    return jnp.clip(per_qblock * i - lead, 0, (rows - NA_KROWS) * GRID_W // NA_KBLK)


def _latent_neighbourhood(proj, ctx_k, ctx_v, bias_t, *, seq):
    t = proj.shape[0]
    b = t // seq
    rows = seq // GRID_W
    nblk = rows // NA_QROWS
    kblk_per_batch = seq // NA_KBLK
    past = ctx_k.shape[1]
    grid = (N_HEADS_B // 2, nblk, b)

    def kv_spec(col0, j):
        return pl.BlockSpec(
            (NA_KBLK, LANES),
            lambda hp, i, bi: (bi * kblk_per_batch + _na_first_key_block(i, rows) + j, col0 // LANES + hp))

    def variant(i):
        return jnp.where(i == 0, 0, jnp.where(i == nblk - 1, 2, 1))

    in_specs = ([pl.BlockSpec((NA_TQ, LANES), lambda hp, i, bi: (bi * nblk + i, COL_QB // LANES + hp))]
                + [kv_spec(COL_KB, j) for j in range(4)]
                + [kv_spec(COL_VB, j) for j in range(4)]
                + [pl.BlockSpec((1, past, LANES), lambda hp, i, bi: (bi, 0, hp)),
                   pl.BlockSpec((1, past, LANES), lambda hp, i, bi: (bi, 0, hp)),
                   pl.BlockSpec((1, 2, NA_TK, NA_TQ), lambda hp, i, bi: (variant(i), hp, 0, 0))])
    return pl.pallas_call(
        _na_kernel,
        out_shape=jax.ShapeDtypeStruct((t, WIDTH_B), F32),
        grid=grid,
        in_specs=in_specs,
        out_specs=pl.BlockSpec((NA_TQ, LANES), lambda hp, i, bi: (bi * nblk + i, hp)),
        compiler_params=_cparams(("arbitrary", "arbitrary", "arbitrary")),
        name="latent_neighbourhood",
    )(proj, *([proj] * 8), ctx_k, ctx_v, bias_t)


def _neighbourhood_bias(rpb, rows):
    nblk = rows // NA_QROWS
    n_dr = 2 * NA_KH - 1
    n_dc = 2 * NA_KW - 1
    kc = np.arange(GRID_W)[:, None]
    qc = np.arange(GRID_W)[None, :]
    ws = np.clip(qc - NA_KW // 2, 0, GRID_W - NA_KW)
    col_ok = (kc >= ws) & (kc < ws + NA_KW)
    dc = np.clip(kc - qc + NA_KW - 1, 0, n_dc - 1)
    dc_onehot = (dc[None] == np.arange(n_dc)[:, None, None]).astype(np.float32)
    tiles = jnp.einsum('hab,bkq->hakq', rpb.astype(F32), jnp.asarray(dc_onehot),
                       precision=lax.Precision.HIGHEST)
    tiles = jnp.where(jnp.asarray(col_ok)[None, None], tiles, MASKED)
    masked_tile = jnp.full((rpb.shape[0], 1, GRID_W, GRID_W), MASKED, F32)
    tiles = jnp.concatenate([tiles, masked_tile], axis=1)
    pick = np.zeros((3, NA_KROWS, NA_QROWS, n_dr + 1), np.float32)
    for v, i in enumerate((0, 1, nblk - 1)):
        r0 = i * NA_QROWS
        ks = int(np.clip(r0 - NA_KH // 2, 0, rows - NA_KROWS))
        for kl in range(NA_KROWS):
            for ql in range(NA_QROWS):
                kr, qr = ks + kl, r0 + ql
                rs = int(np.clip(qr - NA_KH // 2, 0, rows - NA_KH))
                ok = rs <= kr < rs + NA_KH
                pick[v, kl, ql, (kr - qr + NA_KH - 1) if ok else n_dr] = 1.0
    bias = jnp.einsum('vkqa,hacd->vhkcqd', jnp.asarray(pick), tiles, precision=lax.Precision.HIGHEST)
    return bias.reshape(3, rpb.shape[0], NA_TK, NA_TQ)


def _merge_kernel(xp_ref, oap_ref, obp_ref, xs_ref, oas_ref, obs_ref, ona_ref, onb_ref, wo_ref,
                  g1_ref, sh2_ref, sc2_ref, n2_ref, y_ref, h_ref, *, ctx_tiles):
    def one_stream(x_ref, oa_ref, ob_ref):
        na = (_rms(oa_ref[...]) * ona_ref[...]).astype(BF16)
        nb = (_rms(ob_ref[...]) * onb_ref[...]).astype(BF16)
        mix = _dot(na, wo_ref[0:WIDTH_A, :]) + _dot(nb, wo_ref[WIDTH_A:WIDTH_A + WIDTH_B, :])
        y = x_ref[...] + g1_ref[0] * mix
        y_ref[...] = y
        h = _rms(y) * n2_ref[...]
        h_ref[...] = h * (1.0 + sc2_ref[0]) + sh2_ref[0]

    i = pl.program_id(0)
    pl.when(i < ctx_tiles)(lambda: one_stream(xp_ref, oap_ref, obp_ref))
    pl.when(i >= ctx_tiles)(lambda: one_stream(xs_ref, oas_ref, obs_ref))


def _merge(ctx, lat, on_a, on_b, w_out_bf, gate1, shift2, scale2, norm2, *, tm, lat_seq):
    t_c, d = ctx[0].shape
    t_l = lat[0].shape[0]
    ctx_tiles = t_c // tm
    lat_tiles_per_batch = lat_seq // tm

    def ctx_map(i):
        return (jnp.minimum(i, ctx_tiles - 1), 0)

    def lat_map(i):
        return (jnp.maximum(i - ctx_tiles, 0), 0)

    def mod_map(i):
        return (jnp.where(i < ctx_tiles, 0, 1 + (i - ctx_tiles) // lat_tiles_per_batch), 0, 0)

    def stream_specs(index_map):
        return [pl.BlockSpec((tm, d), index_map),
                pl.BlockSpec((tm, WIDTH_A), index_map),
                pl.BlockSpec((tm, WIDTH_B), index_map)]

    return pl.pallas_call(
        functools.partial(_merge_kernel, ctx_tiles=ctx_tiles),
        out_shape=[jax.ShapeDtypeStruct((t_c + t_l, d), F32)] * 2,
        grid=((t_c + t_l) // tm,),
        in_specs=stream_specs(ctx_map) + stream_specs(lat_map) + [
            pl.BlockSpec((1, WIDTH_A), lambda i: (0, 0)),
            pl.BlockSpec((1, WIDTH_B), lambda i: (0, 0)),
            pl.BlockSpec((WIDTH_A + WIDTH_B, d), lambda i: (0, 0)),
            pl.BlockSpec((1, 1, d), mod_map),
            pl.BlockSpec((1, 1, d), mod_map),
            pl.BlockSpec((1, 1, d), mod_map),
            pl.BlockSpec((1, d), lambda i: (0, 0))],
        out_specs=[pl.BlockSpec((tm, d), lambda i: (i, 0))] * 2,
        compiler_params=_cparams(("arbitrary",)),
        name="merge_out_proj",
    )(*ctx, *lat, on_a, on_b, w_out_bf, gate1, shift2, scale2, norm2)


def _first_index_of_max(x, iota):
    mx = jnp.max(x, axis=0, keepdims=True)
    idx = jnp.min(jnp.where(x == mx, iota, float(x.shape[0])), axis=0, keepdims=True)
    return mx, iota == idx


def _router_gates(h, wr_hi, wr_lo, rbias):
    h_hi = h.astype(BF16)
    h_lo = (h - h_hi.astype(F32)).astype(BF16)
    logits = _dot_nt(wr_hi, h_hi) + (_dot_nt(wr_lo, h_hi) + _dot_nt(wr_hi, h_lo))
    scores = _sigmoid(logits)
    sel = scores + rbias
    tm = sel.shape[1]
    iota_g = lax.broadcasted_iota(jnp.int32, (GROUP_SIZE, tm), 0).astype(F32)
    group_scores = []
    for g in range(N_GROUPS):
        grp = sel[g * GROUP_SIZE:(g + 1) * GROUP_SIZE]
        m1, first = _first_index_of_max(grp, iota_g)
        m2 = jnp.max(jnp.where(first, -jnp.inf, grp), axis=0, keepdims=True)
        group_scores.append(m1 + m2)
    gs = jnp.concatenate(group_scores, axis=0)
    iota_n = lax.broadcasted_iota(jnp.int32, (N_GROUPS, tm), 0).astype(F32)
    group_on = jnp.zeros((N_GROUPS, tm), F32)
    for _ in range(TOPK_GROUPS):
        _, pick = _first_index_of_max(gs, iota_n)
        group_on = jnp.where(pick, 1.0, group_on)
        gs = jnp.where(pick, -jnp.inf, gs)
    expert_on = jnp.concatenate(
        [jnp.broadcast_to(group_on[g:g + 1], (GROUP_SIZE, tm)) for g in range(N_GROUPS)], axis=0)
    cand = jnp.where(expert_on > 0.0, sel, -jnp.inf)
    iota_e = lax.broadcasted_iota(jnp.int32, (N_EXPERTS, tm), 0).astype(F32)
    w = jnp.zeros((N_EXPERTS, tm), F32)
    chosen = jnp.zeros((N_EXPERTS, tm), F32)
    for _ in range(TOP_K):
        _, pick = _first_index_of_max(cand, iota_e)
        w = jnp.where(pick, scores, w)
        chosen = jnp.where(pick, 1.0, chosen)
        cand = jnp.where(pick, -jnp.inf, cand)
    return w / jnp.sum(w, axis=0, keepdims=True) * ROUTED_SCALE, chosen


MOE_TS = 512
MOE_ROUTE_TM = 512
MOE_ROW_TM = 512


def _route_kernel(h_ref, wrh_ref, wrl_ref, rb_ref, tri_ref, gates_ref, rank_ref, count_ref):
    @pl.when(pl.program_id(0) == 0)
    def _():
        count_ref[...] = jnp.zeros_like(count_ref)

    gates, chosen = _router_gates(h_ref[...], wrh_ref[...], wrl_ref[...], rb_ref[...])
    gates_ref[...] = gates
    before = _dot(chosen.astype(BF16), tri_ref[...])
    seen = count_ref[...]
    rank_ref[...] = jnp.where(chosen > 0.0, before + seen[:, 0:1], -1.0)
    count_ref[...] = seen + jnp.sum(chosen, axis=1, keepdims=True)


def _route(h_all, wr_hi, wr_lo, rbias):
    t, d = h_all.shape
    tm = MOE_ROUTE_TM
    tri = jnp.asarray(np.triu(np.ones((tm, tm), np.float32), k=1), BF16)
    return pl.pallas_call(
        _route_kernel,
        out_shape=[jax.ShapeDtypeStruct((N_EXPERTS, t), F32),
                   jax.ShapeDtypeStruct((N_EXPERTS, t), F32),
                   jax.ShapeDtypeStruct((N_EXPERTS, LANES), F32)],
        grid=(t // tm,),
        in_specs=[pl.BlockSpec((tm, d), lambda i: (i, 0)),
                  pl.BlockSpec((N_EXPERTS, d), lambda i: (0, 0)),
                  pl.BlockSpec((N_EXPERTS, d), lambda i: (0, 0)),
                  pl.BlockSpec((N_EXPERTS, 1), lambda i: (0, 0)),
                  pl.BlockSpec((tm, tm), lambda i: (0, 0))],
        out_specs=[pl.BlockSpec((N_EXPERTS, tm), lambda i: (0, i)),
                   pl.BlockSpec((N_EXPERTS, tm), lambda i: (0, i)),
                   pl.BlockSpec((N_EXPERTS, LANES), lambda i: (0, 0))],
        compiler_params=_cparams(("arbitrary",)),
        name="moe_route",
    )(h_all, wr_hi, wr_lo, rbias, tri)


def _slots_kernel(gates_ref, rank_ref, off_ref, pos_ref, gtok_ref):
    gates = gates_ref[...]
    rank = rank_ref[...]
    tm = gates.shape[1]
    slot = off_ref[...] + rank
    left = jnp.where(rank >= 0.0, 1.0, 0.0)
    iota_e = lax.broadcasted_iota(jnp.int32, (N_EXPERTS, tm), 0).astype(F32)
    pos_rows, gate_rows = [], []
    for _ in range(TOP_K):
        _, pick = _first_index_of_max(left, iota_e)
        pos_rows.append(jnp.sum(jnp.where(pick, slot, 0.0), axis=0, keepdims=True))
        gate_rows.append(jnp.sum(jnp.where(pick, gates, 0.0), axis=0, keepdims=True))
        left = jnp.where(pick, 0.0, left)
    pos_ref[...] = jnp.concatenate(pos_rows, axis=0).astype(jnp.int32)
    pad = jnp.zeros((LANES - TOP_K, tm), F32)
    gtok_ref[...] = jnp.concatenate(gate_rows + [pad], axis=0).T


def _slots(gates_t, rank_t, off):
    t = gates_t.shape[1]
    tm = MOE_ROUTE_TM
    return pl.pallas_call(
        _slots_kernel,
        out_shape=[jax.ShapeDtypeStruct((TOP_K, t), jnp.int32), jax.ShapeDtypeStruct((t, LANES), F32)],
        grid=(t // tm,),
        in_specs=[pl.BlockSpec((N_EXPERTS, tm), lambda i: (0, i)),
                  pl.BlockSpec((N_EXPERTS, tm), lambda i: (0, i)),
                  pl.BlockSpec((N_EXPERTS, 1), lambda i: (0, 0))],
        out_specs=[pl.BlockSpec((TOP_K, tm), lambda i: (0, i)),
                   pl.BlockSpec((tm, LANES), lambda i: (i, 0))],
        compiler_params=_cparams(("arbitrary",)),
        name="moe_slots",
    )(gates_t, rank_t, off)


def _row_copy(src_ref, src_row, dst_ref, dst_row, sem):
    return pltpu.make_async_copy(src_ref.at[pl.ds(src_row, 1)], dst_ref.at[pl.ds(dst_row, 1)], sem)


SC_CORES = 2
SC_SUBCORES = 16
SC_ROWS = 64


def _dispatch(h_all, pos_flat, n_slots):
    t, d = h_all.shape
    workers = SC_CORES * SC_SUBCORES
    per_worker = t // workers
    assert per_worker * workers == t and per_worker % SC_ROWS == 0
    mesh = plsc.VectorSubcoreMesh(core_axis_name="core", subcore_axis_name="subcore")

    @functools.partial(
        pl.kernel, mesh=mesh,
        out_type=jax.ShapeDtypeStruct((n_slots, d), F32),
        scratch_types=[pltpu.VMEM((SC_ROWS,), jnp.int32),
                       pltpu.VMEM((SC_ROWS, d), F32),
                       pltpu.SemaphoreType.DMA],
    )
    def scatter_rows(h_hbm, pos_hbm, out_hbm, idx_v, rows_v, sem):
        worker = lax.axis_index("subcore") * SC_CORES + lax.axis_index("core")
        base = worker * per_worker

        @pl.loop(0, per_worker // SC_ROWS)
        def _(j):
            first = base + j * SC_ROWS
            pltpu.sync_copy(h_hbm.at[pl.ds(first, SC_ROWS)], rows_v)
            for k in range(TOP_K):
                pltpu.sync_copy(pos_hbm.at[pl.ds(k * t + first, SC_ROWS)], idx_v)
                pltpu.async_copy(rows_v, out_hbm.at[idx_v], sem).wait()

    return scatter_rows(h_all, pos_flat)


def _fill_padding_kernel(fill_lo_ref, fill_hi_ref, h_ref, xs_in_ref, xs_ref, sem):
    del xs_in_ref

    def fill(e, c):
        lo = fill_lo_ref[e]
        hi = fill_hi_ref[e]

        def one(r, c2):
            _row_copy(h_ref, 0, xs_ref, r, sem).start()
            return c2

        def one_wait(r, c2):
            _row_copy(h_ref, 0, xs_ref, 0, sem).wait()
            return c2

        lax.fori_loop(lo, hi, one, 0)
        lax.fori_loop(lo, hi, one_wait, 0)
        return c

    lax.fori_loop(0, N_EXPERTS, fill, 0)


def _fill_padding(x_slots, h_all, fill_lo, fill_hi):
    d = h_all.shape[1]
    return pl.pallas_call(
        _fill_padding_kernel,
        out_shape=jax.ShapeDtypeStruct(x_slots.shape, F32),
        grid_spec=pltpu.PrefetchScalarGridSpec(
            num_scalar_prefetch=2,
            grid=(1,),
            in_specs=[pl.BlockSpec((8, d), lambda i, lo, hi: (0, 0)),
                      pl.BlockSpec(memory_space=pl.ANY)],
            out_specs=pl.BlockSpec(memory_space=pl.ANY),
            scratch_shapes=[pltpu.SemaphoreType.DMA(())]),
        input_output_aliases={3: 0},
        compiler_params=_cparams(("arbitrary",)),
        name="moe_fill_padding",
    )(fill_lo, fill_hi, h_all, x_slots)


def _experts_kernel(te_ref, xs_ref, wg_ref, wu_ref, wd_ref, ys_ref, wg_bf, wu_bf, wd_bf):
    i = pl.program_id(0)

    @pl.when((i == 0) | (te_ref[i] != te_ref[jnp.maximum(i, 1) - 1]))
    def _():
        wg_bf[...] = wg_ref[0].astype(BF16)
        wu_bf[...] = wu_ref[0].astype(BF16)
        wd_bf[...] = wd_ref[0].astype(BF16)

    x = xs_ref[...].astype(BF16)
    g = _dot(x, wg_bf[...])
    u = _dot(x, wu_bf[...])
    act = (g * _sigmoid(g)) * u
    ys_ref[...] = _dot(act.astype(BF16), wd_bf[...])


def _experts(xs, tile_expert, w_gate, w_up, w_down):
    n_slots, d = xs.shape
    ts = MOE_TS
    return pl.pallas_call(
        _experts_kernel,
        out_shape=jax.ShapeDtypeStruct((n_slots, d), F32),
        grid_spec=pltpu.PrefetchScalarGridSpec(
            num_scalar_prefetch=1,
            grid=(n_slots // ts,),
            in_specs=[pl.BlockSpec((ts, d), lambda i, te: (i, 0)),
                      pl.BlockSpec((1, d, D_EXPERT), lambda i, te: (te[i], 0, 0)),
                      pl.BlockSpec((1, d, D_EXPERT), lambda i, te: (te[i], 0, 0)),
                      pl.BlockSpec((1, D_EXPERT, d), lambda i, te: (te[i], 0, 0))],
            out_specs=pl.BlockSpec((ts, d), lambda i, te: (i, 0)),
            scratch_shapes=[pltpu.VMEM((d, D_EXPERT), BF16), pltpu.VMEM((d, D_EXPERT), BF16),
                            pltpu.VMEM((D_EXPERT, d), BF16)]),
        compiler_params=_cparams(("arbitrary",)),
        name="moe_experts",
    )(tile_expert, xs, w_gate, w_up, w_down)


def _combine_kernel(y_ref, h_ref, g2_ref, gtok_ref, pos_ref, wgs_ref, wus_ref, wds_ref, ys_ref, o_ref,
                    rows_ref, sem):
    tm = y_ref.shape[0]

    def start(t, c):
        for k in range(TOP_K):
            _row_copy(ys_ref, pos_ref[k, t], rows_ref.at[k], t, sem).start(priority=k % 2)
        return c

    lax.fori_loop(0, tm, start, 0, unroll=8)

    h = h_ref[...].astype(BF16)
    gs = _dot(h, wgs_ref[...])
    us = _dot(h, wus_ref[...])
    acc = _dot(((gs * _sigmoid(gs)) * us).astype(BF16), wds_ref[...])

    def wait(t, c):
        for k in range(TOP_K):
            _row_copy(ys_ref, 0, rows_ref.at[k], 0, sem).wait()
        return c

    lax.fori_loop(0, tm, wait, 0, unroll=8)

    gtok = gtok_ref[...]
    for k in range(TOP_K):
        acc = acc + gtok[:, k:k + 1] * rows_ref[k]
    o_ref[...] = y_ref[...] + g2_ref[0] * acc


def _combine(y_all, h_all, gate2, gtok, pos, ys, wgs, wus, wds, *, first_token, tokens, seq):
    d = y_all.shape[1]
    tm = MOE_ROW_TM
    tile0 = first_token // tm
    nb = gate2.shape[0]
    tiles_per_batch = seq // tm

    def mod_map(i):
        return ((i // tiles_per_batch) if nb > 1 else 0, 0, 0)

    return pl.pallas_call(
        _combine_kernel,
        out_shape=jax.ShapeDtypeStruct((tokens, d), F32),
        grid=(tokens // tm,),
        in_specs=[pl.BlockSpec((tm, d), lambda i: (tile0 + i, 0)),
                  pl.BlockSpec((tm, d), lambda i: (tile0 + i, 0)),
                  pl.BlockSpec((1, 1, d), mod_map),
                  pl.BlockSpec((tm, LANES), lambda i: (tile0 + i, 0)),
                  pl.BlockSpec((TOP_K, tm), lambda i: (0, tile0 + i), memory_space=pltpu.SMEM),
                  pl.BlockSpec((d, D_SHARED), lambda i: (0, 0)),
                  pl.BlockSpec((d, D_SHARED), lambda i: (0, 0)),
                  pl.BlockSpec((D_SHARED, d), lambda i: (0, 0)),
                  pl.BlockSpec(memory_space=pl.ANY)],
        out_specs=pl.BlockSpec((tm, d), lambda i: (i, 0)),
        scratch_shapes=[pltpu.VMEM((TOP_K, tm, d), F32), pltpu.SemaphoreType.DMA(())],
        compiler_params=_cparams(("arbitrary",)),
        name="moe_combine",
    )(y_all, h_all, gate2, gtok, pos, wgs, wus, wds, ys)


def _expert_layout(counts, n_tiles):
    cnt = counts.astype(jnp.int32)
    tiles = (cnt + (MOE_TS - 1)) // MOE_TS
    last_tile = jnp.cumsum(tiles)
    off = (last_tile - tiles) * MOE_TS
    pad_end = (off + tiles * MOE_TS).at[N_EXPERTS - 1].set(n_tiles * MOE_TS)
    tile_ids = jnp.arange(n_tiles, dtype=jnp.int32)
    tile_expert = jnp.minimum(
        jnp.sum((last_tile[None, :] <= tile_ids[:, None]).astype(jnp.int32), axis=1), N_EXPERTS - 1)
    return off, off + cnt, pad_end, tile_expert


def _rope_tables(n_tokens):
    t = jnp.arange(n_tokens)
    row = (t // GRID_W).astype(F32)
    col = (t % GRID_W).astype(F32)
    nf = HEAD_DIM // 4
    freqs = ROPE_THETA ** (-jnp.arange(nf, dtype=F32) / nf)
    ang_r = row[:, None] * freqs
    ang_c = col[:, None] * freqs
    cos = jnp.concatenate([jnp.cos(ang_r)] * 2 + [jnp.cos(ang_c)] * 2, axis=1)
    sin = jnp.concatenate([-jnp.sin(ang_r), jnp.sin(ang_r), -jnp.sin(ang_c), jnp.sin(ang_c)], axis=1)
    reps = LANES // HEAD_DIM
    return jnp.tile(cos, (1, reps)), jnp.tile(sin, (1, reps))


def _head_gains(qn_a, kn_a, qn_b, kn_b):
    ones = jnp.ones((HEAD_DIM,), F32)
    parts = ([qn_a] * N_HEADS_A + [kn_a] * N_KV_A + [ones] * N_KV_A
             + [qn_b] * N_HEADS_B + [kn_b] * N_HEADS_B + [ones] * N_HEADS_B)
    return jnp.concatenate(parts).reshape(1, IN_COLS).astype(F32)


def _same_head_indicator():
    i = np.arange(LANES)
    return jnp.asarray((i[:, None] // HEAD_DIM) == (i[None, :] // HEAD_DIM), BF16)


def _token_major(cache):
    b, h, s, hd = cache.shape
    return cache.transpose(0, 2, 1, 3).reshape(b, s, h * hd).astype(BF16)


def kernel(x_prompt, x_sample, cache_k_a, cache_v_a, cache_k_b, cache_v_b, c, c_ctx, w_mod, b_mod, norm1, norm2, w_in, qn_a, kn_a, qn_b, kn_b, rpb, on_a, on_b, w_out, w_router, router_bias, w_gate_e, w_up_e, w_down_e, w_gate_s, w_up_s, w_down_s):
    depth = w_mod.shape[0]
    assert depth == 1
    l = 0
    bp, sp, d = x_prompt.shape
    bs, ss, _ = x_sample.shape

    cvec = jnp.concatenate([c_ctx[None, :], c], axis=0)
    rows = -(-cvec.shape[0] // 8) * 8
    cvec = jnp.pad(cvec, ((0, rows - cvec.shape[0]), (0, 0)))
    mod = _adaln(cvec, w_mod[l], b_mod[l])
    mod_p = [m.reshape(1, 1, d) for m in jnp.split(mod[0:1], 6, axis=-1)]
    mod_s = [m.reshape(bs, 1, d) for m in jnp.split(mod[1:1 + bs], 6, axis=-1)]
    mod_all = [m.reshape(1 + bs, 1, d) for m in jnp.split(mod[0:1 + bs], 6, axis=-1)]

    w_in_bf = w_in[l].astype(BF16)
    w_out_bf = w_out[l].astype(BF16)
    gain = _head_gains(qn_a[l], kn_a[l], qn_b[l], kn_b[l])
    seg = _same_head_indicator()
    n1 = norm1[l].reshape(1, d)
    n2 = norm2[l].reshape(1, d)
    ona = on_a[l].reshape(1, WIDTH_A)
    onb = on_b[l].reshape(1, WIDTH_B)
    wr_t = w_router[l].T
    wr_hi = wr_t.astype(BF16)
    wr_lo = (wr_t - wr_hi.astype(F32)).astype(BF16)
    rbias = router_bias[l].reshape(N_EXPERTS, 1).astype(F32)
    wgs = w_gate_s[l].astype(BF16)
    wus = w_up_s[l].astype(BF16)
    wds = w_down_s[l].astype(BF16)
    t_p = bp * sp
    t_s = bs * ss
    t_all = t_p + t_s

    xp = x_prompt.reshape(t_p, d)
    proj_p, st_ka, st_va, st_kb, st_vb = _project(
        xp, mod_p[0], mod_p[1], n1, w_in_bf, gain, seg, None, tm=sp, seq=sp, states=True)
    oa_p, ob_p = _context_attention(proj_p, seq=sp)

    xs = x_sample.reshape(t_s, d)
    proj_s, = _project(xs, mod_s[0], mod_s[1], n1, w_in_bf, gain, seg, _rope_tables(ss),
                       tm=512, seq=ss, states=False)
    oa_s = _latent_gqa(proj_s, _token_major(cache_k_a[:, l]), _token_major(cache_v_a[:, l]), seq=ss, tq=128)
    bias_t = _neighbourhood_bias(rpb[l], ss // GRID_W)
    ob_s = _latent_neighbourhood(proj_s, _token_major(cache_k_b[:, l]), _token_major(cache_v_b[:, l]),
                                 bias_t, seq=ss)

    y1_all, h_all = _merge((xp, oa_p, ob_p), (xs, oa_s, ob_s), ona, onb, w_out_bf,
                           mod_all[2], mod_all[3], mod_all[4], n2, tm=512, lat_seq=ss)
    gates_t, rank_t, counts = _route(h_all, wr_hi, wr_lo, rbias)
    n_tiles = t_all * TOP_K // MOE_TS + N_EXPERTS
    off, fill_lo, fill_hi, tile_expert = _expert_layout(counts[:, 0], n_tiles)
    pos, gtok = _slots(gates_t, rank_t, off.astype(F32).reshape(N_EXPERTS, 1))
    x_slots = _dispatch(h_all, pos.reshape(TOP_K * t_all), n_tiles * MOE_TS)
    x_slots = _fill_padding(x_slots, h_all, fill_lo, fill_hi)
    y_slots = _experts(x_slots, tile_expert, w_gate_e[l], w_up_e[l], w_down_e[l])
    y_p = _combine(y1_all, h_all, mod_p[5], gtok, pos, y_slots, wgs, wus, wds,
                   first_token=0, tokens=t_p, seq=sp)
    y_s = _combine(y1_all, h_all, mod_s[5], gtok, pos, y_slots, wgs, wus, wds,
                   first_token=t_p, tokens=t_s, seq=ss)

    return (y_p.reshape(bp, sp, d), y_s.reshape(bs, ss, d), st_ka, st_va, st_kb, st_vb)
```

```python
import functools

import numpy as np
import jax
import jax.numpy as jnp
from jax import lax
from jax.experimental import pallas as pl
from jax.experimental.pallas import tpu as pltpu
from jax.experimental.pallas import tpu_sc as plsc

F32 = jnp.float32
BF16 = jnp.bfloat16

D_MODEL = 1024
HEAD_DIM = 64
N_HEADS_A = 8
N_KV_A = 2
GROUP_A = N_HEADS_A // N_KV_A
N_HEADS_B = 8
WIDTH_A = N_HEADS_A * HEAD_DIM
WIDTH_B = N_HEADS_B * HEAD_DIM
KV_WIDTH_A = N_KV_A * HEAD_DIM
IN_COLS = WIDTH_A + 2 * KV_WIDTH_A + 3 * WIDTH_B
GRID_W = 64
ROPE_THETA = 10000.0
NA_KH = 8
NA_KW = 16
N_EXPERTS = 64
N_GROUPS = 8
GROUP_SIZE = N_EXPERTS // N_GROUPS
TOPK_GROUPS = 4
TOP_K = 8
D_EXPERT = 256
D_SHARED = 256
ROUTED_SCALE = 2.5
EPS = 1e-6

LANES = 128
MXU_DIM = 256
MASKED = -1e30

COL_QA = 0
COL_KA = WIDTH_A
COL_VA = COL_KA + KV_WIDTH_A
COL_QB = COL_VA + KV_WIDTH_A
COL_KB = COL_QB + WIDTH_B
COL_VB = COL_KB + WIDTH_B

NA_QROWS = 8
NA_KROWS = 2 * NA_KH
NA_TQ = NA_QROWS * GRID_W
NA_TK = NA_KROWS * GRID_W
NA_KBLK = 256

VMEM_LIMIT = 56 * 1024 * 1024


def _cparams(sem):
    return pltpu.CompilerParams(dimension_semantics=sem, vmem_limit_bytes=VMEM_LIMIT)


def _dot(a, b):
    return jnp.dot(a, b, preferred_element_type=F32)


def _dot_nt(a, b):
    return lax.dot_general(a, b, (((1,), (1,)), ((), ())), preferred_element_type=F32)


def _sigmoid(x):
    return 1.0 / (1.0 + jnp.exp(-x))


def _rms(x):
    return x * lax.rsqrt(jnp.mean(x * x, axis=-1, keepdims=True) + EPS)


def _pack_rows(x):
    n = x.shape[1] // 2
    hi = lax.bitcast_convert_type(x[:, :n].astype(BF16).astype(F32), jnp.int32)
    lo = lax.bitcast_convert_type(x[:, n:].astype(BF16).astype(F32), jnp.int32)
    return hi | lax.shift_right_logical(lo, 16)


def _unpack_rows(w):
    left = lax.bitcast_convert_type(w & jnp.int32(-65536), F32)
    right = lax.bitcast_convert_type(lax.shift_left(w, 16), F32)
    return left, right


def _mod_kernel(c_ref, w_ref, b_ref, o_ref):
    c = c_ref[...]
    s = c * _sigmoid(c)
    o_ref[...] = jnp.dot(s, w_ref[...], preferred_element_type=F32,
                         precision=lax.Precision.HIGHEST) + b_ref[...]


def _adaln(cvec, w_mod, b_mod):
    rows, d = cvec.shape
    n = w_mod.shape[1]
    tn = 512
    return pl.pallas_call(
        _mod_kernel,
        out_shape=jax.ShapeDtypeStruct((rows, n), F32),
        grid=(n // tn,),
        in_specs=[pl.BlockSpec((rows, d), lambda j: (0, 0)),
                  pl.BlockSpec((d, tn), lambda j: (0, j)),
                  pl.BlockSpec((1, tn), lambda j: (0, j))],
        out_specs=pl.BlockSpec((rows, tn), lambda j: (0, j)),
        compiler_params=_cparams(("arbitrary",)),
        name="adaln_mod",
    )(cvec, w_mod, b_mod.reshape(1, n))


_PROJ_CHUNKS = (
    [(COL_QA + i * LANES, LANES, True, True) for i in range(WIDTH_A // LANES)]
    + [(COL_KA, LANES, True, True), (COL_VA, LANES, False, False)]
    + [(COL_QB + i * LANES, LANES, True, False) for i in range(WIDTH_B // LANES)]
    + [(COL_KB + i * LANES, LANES, True, False) for i in range(WIDTH_B // LANES)]
    + [(COL_VB + i * LANES, LANES, False, False) for i in range(WIDTH_B // LANES)]
)


def _proj_kernel(*refs, rope, states):
    x_ref, sh_ref, sc_ref, n1_ref, w_ref, gain_ref, seg_ref = refs[:7]
    pos = 7
    if rope:
        cos_ref, sin_ref = refs[pos:pos + 2]
        pos += 2
    out_ref = refs[pos]
    pos += 1
    if states:
        ka_ref, va_ref, kb_ref, vb_ref = refs[pos:pos + 4]

    x = x_ref[...]
    h = _rms(x) * n1_ref[...]
    h = h * (1.0 + sc_ref[0]) + sh_ref[0]
    p = _dot(h.astype(BF16), w_ref[...])
    seg = seg_ref[...]
    if rope:
        cos = cos_ref[...]
        sin = sin_ref[...]
        lane = lax.broadcasted_iota(jnp.int32, cos.shape, 1)
        first_half = (lane % (HEAD_DIM // 2)) < (HEAD_DIM // 4)

    for c0, w, normed, roped in _PROJ_CHUNKS:
        pc = p[:, c0:c0 + w]
        if normed:
            sq = pc * pc
            hi = sq.astype(BF16)
            lo = (sq - hi.astype(F32)).astype(BF16)
            ss = _dot(hi, seg) + _dot(lo, seg)
            pc = pc * lax.rsqrt(ss * (1.0 / HEAD_DIM) + EPS) * gain_ref[:, c0:c0 + w]
        if states:
            if c0 == COL_KA:
                for hh in range(N_KV_A):
                    ka_ref[0, 0, hh] = pc[:, hh * HEAD_DIM:(hh + 1) * HEAD_DIM]
            elif c0 == COL_VA:
                for hh in range(N_KV_A):
                    va_ref[0, 0, hh] = pc[:, hh * HEAD_DIM:(hh + 1) * HEAD_DIM]
            elif COL_KB <= c0 < COL_VB:
                base = (c0 - COL_KB) // HEAD_DIM
                for hh in range(LANES // HEAD_DIM):
                    kb_ref[0, 0, base + hh] = pc[:, hh * HEAD_DIM:(hh + 1) * HEAD_DIM]
            elif c0 >= COL_VB:
                base = (c0 - COL_VB) // HEAD_DIM
                for hh in range(LANES // HEAD_DIM):
                    vb_ref[0, 0, base + hh] = pc[:, hh * HEAD_DIM:(hh + 1) * HEAD_DIM]
        if rope and roped:
            partner = jnp.where(first_half,
                                pltpu.roll(pc, LANES - HEAD_DIM // 4, 1),
                                pltpu.roll(pc, HEAD_DIM // 4, 1))
            pc = pc * cos + partner * sin
        if c0 < COL_KA or COL_QB <= c0 < COL_KB:
            pc = pc * (HEAD_DIM ** -0.5)
        out_ref[:, c0:c0 + w] = pc.astype(BF16)


def _project(x2d, shift, scale, norm1, w_in_bf, gain, seg, rope_tabs, *, tm, seq, states):
    t, d = x2d.shape
    nb = shift.shape[0]
    tiles_per_batch = seq // tm
    rope = rope_tabs is not None

    def mod_map(i):
        return ((i // tiles_per_batch) if nb > 1 else 0, 0, 0)

    in_specs = [pl.BlockSpec((tm, d), lambda i: (i, 0)),
                pl.BlockSpec((1, 1, d), mod_map),
                pl.BlockSpec((1, 1, d), mod_map),
                pl.BlockSpec((1, d), lambda i: (0, 0)),
                pl.BlockSpec((d, IN_COLS), lambda i: (0, 0)),
                pl.BlockSpec((1, IN_COLS), lambda i: (0, 0)),
                pl.BlockSpec((LANES, LANES), lambda i: (0, 0))]
    args = [x2d, shift, scale, norm1, w_in_bf, gain, seg]
    if rope:
        in_specs += [pl.BlockSpec((tm, LANES), lambda i: (i % tiles_per_batch, 0))] * 2
        args += list(rope_tabs)
    out_shape = [jax.ShapeDtypeStruct((t, IN_COLS), BF16)]
    out_specs = [pl.BlockSpec((tm, IN_COLS), lambda i: (i, 0))]
    if states:
        assert tm == seq
        b = t // seq
        for nh in (N_KV_A, N_KV_A, N_HEADS_B, N_HEADS_B):
            out_shape.append(jax.ShapeDtypeStruct((b, 1, nh, seq, HEAD_DIM), F32))
            out_specs.append(pl.BlockSpec((1, 1, nh, seq, HEAD_DIM), lambda i: (i, 0, 0, 0, 0)))
    return pl.pallas_call(
        functools.partial(_proj_kernel, rope=rope, states=states),
        out_shape=out_shape,
        grid=(t // tm,),
        in_specs=in_specs,
        out_specs=out_specs,
        compiler_params=_cparams(("arbitrary",)),
        name="proj_states" if states else "proj_rope",
    )(*args)


def _lane_half(shape):
    return lax.broadcasted_iota(jnp.int32, shape, 1) // HEAD_DIM


def _keep_half(x, half):
    return jnp.where(_lane_half(x.shape) == half, x, jnp.zeros_like(x))


def _transpose_bf16(x):
    return x.astype(F32).T.astype(BF16)


def _attend(q, keys, values_t, biases):
    scores = []
    for k, b in zip(keys, biases):
        s = _dot_nt(k, q)
        if b is not None:
            s = s + b
        scores.append(s)
    m = functools.reduce(jnp.maximum, [jnp.max(s, axis=0, keepdims=True) for s in scores])
    denom = None
    out = None
    for s, vt in zip(scores, values_t):
        p = jnp.exp(s - m)
        ps = jnp.sum(p, axis=0, keepdims=True)
        po = _dot(vt, p.astype(BF16))
        denom = ps if denom is None else denom + ps
        out = po if out is None else out + po
    return out / denom


def _swap_halves(q_bf16):
    return pltpu.roll(q_bf16.astype(F32), HEAD_DIM, 1).astype(BF16)


def _gqa_heads(q_of_pair, keys_by_group, values_t):
    outs = []
    for h in range(N_HEADS_A):
        g = h // GROUP_A
        q = q_of_pair(h // 2)
        if h % 2 != g:
            q = _swap_halves(q)
        o = _attend(q, keys_by_group[g], values_t, [None] * len(values_t))
        outs.append(o[g * HEAD_DIM:(g + 1) * HEAD_DIM])
    return jnp.concatenate(outs, axis=0)


def _ctx_attn_kernel(p_ref, oa_ref, ob_ref):
    ka = p_ref[:, COL_KA:COL_KA + LANES]
    va_t = [_transpose_bf16(p_ref[:, COL_VA:COL_VA + LANES])]
    keys_by_group = [[_keep_half(ka, g)] for g in range(N_KV_A)]
    oa = _gqa_heads(lambda i: p_ref[:, COL_QA + i * LANES:COL_QA + (i + 1) * LANES],
                    keys_by_group, va_t)
    oa_ref[...] = oa.T

    outs = []
    for i in range(N_HEADS_B // 2):
        q = p_ref[:, COL_QB + i * LANES:COL_QB + (i + 1) * LANES]
        k = p_ref[:, COL_KB + i * LANES:COL_KB + (i + 1) * LANES]
        vt = [_transpose_bf16(p_ref[:, COL_VB + i * LANES:COL_VB + (i + 1) * LANES])]
        for half in range(2):
            o = _attend(q, [_keep_half(k, half)], vt, [None])
            outs.append(o[half * HEAD_DIM:(half + 1) * HEAD_DIM])
    ob_ref[...] = jnp.concatenate(outs, axis=0).T


def _context_attention(proj, *, seq):
    t = proj.shape[0]
    return pl.pallas_call(
        _ctx_attn_kernel,
        out_shape=[jax.ShapeDtypeStruct((t, WIDTH_A), F32), jax.ShapeDtypeStruct((t, WIDTH_B), F32)],
        grid=(t // seq,),
        in_specs=[pl.BlockSpec((seq, IN_COLS), lambda i: (i, 0))],
        out_specs=[pl.BlockSpec((seq, WIDTH_A), lambda i: (i, 0)),
                   pl.BlockSpec((seq, WIDTH_B), lambda i: (i, 0))],
        compiler_params=_cparams(("arbitrary",)),
        name="context_attention",
    )(proj)


def _gqa_latent_kernel(q_ref, k_ref, v_ref, ck_ref, cv_ref, o_ref, kg_ref, ckg_ref, vt_ref, cvt_ref):
    @pl.when(pl.program_id(1) == 0)
    def _():
        k = k_ref[...]
        ck = ck_ref[0]
        for g in range(N_KV_A):
            kg_ref[g] = _keep_half(k, g)
            ckg_ref[g] = _keep_half(ck, g)
        vt_ref[...] = _transpose_bf16(v_ref[...])
        cvt_ref[...] = _transpose_bf16(cv_ref[0])

    tq = q_ref.shape[0]
    scores = []
    for g in range(N_KV_A):
        qs = []
        for j in range(GROUP_A):
            h = g * GROUP_A + j
            q = q_ref[:, (h // 2) * LANES:(h // 2 + 1) * LANES]
            qs.append(q if h % 2 == g else _swap_halves(q))
        qg = jnp.concatenate(qs, axis=0)
        scores.append((_dot_nt(kg_ref[g], qg), _dot_nt(ckg_ref[g], qg)))
    probs = []
    for s_lat, s_ctx in scores:
        m = jnp.maximum(jnp.max(s_lat, axis=0, keepdims=True), jnp.max(s_ctx, axis=0, keepdims=True))
        p_lat = jnp.exp(s_lat - m)
        p_ctx = jnp.exp(s_ctx - m)
        denom = jnp.sum(p_lat, axis=0, keepdims=True) + jnp.sum(p_ctx, axis=0, keepdims=True)
        probs.append((p_lat.astype(BF16), p_ctx.astype(BF16), denom))
    outs = []
    for g, (p_lat, p_ctx, denom) in enumerate(probs):
        o = _dot(vt_ref[...], p_lat) + _dot(cvt_ref[...], p_ctx)
        o = (o / denom)[g * HEAD_DIM:(g + 1) * HEAD_DIM]
        outs += [o[:, j * tq:(j + 1) * tq] for j in range(GROUP_A)]
    o_ref[...] = jnp.concatenate(outs, axis=0).T


def _latent_gqa(proj, ctx_k, ctx_v, *, seq, tq):
    t = proj.shape[0]
    b = t // seq
    nq = seq // tq
    past = ctx_k.shape[1]
    return pl.pallas_call(
        _gqa_latent_kernel,
        out_shape=jax.ShapeDtypeStruct((t, WIDTH_A), F32),
        grid=(b, nq),
        in_specs=[pl.BlockSpec((tq, WIDTH_A), lambda bi, qi: (bi * nq + qi, 0)),
                  pl.BlockSpec((seq, LANES), lambda bi, qi: (bi, COL_KA // LANES)),
                  pl.BlockSpec((seq, LANES), lambda bi, qi: (bi, COL_VA // LANES)),
                  pl.BlockSpec((1, past, LANES), lambda bi, qi: (bi, 0, 0)),
                  pl.BlockSpec((1, past, LANES), lambda bi, qi: (bi, 0, 0))],
        out_specs=pl.BlockSpec((tq, WIDTH_A), lambda bi, qi: (bi * nq + qi, 0)),
        scratch_shapes=[pltpu.VMEM((N_KV_A, seq, LANES), BF16),
                        pltpu.VMEM((N_KV_A, past, LANES), BF16),
                        pltpu.VMEM((LANES, seq), BF16),
                        pltpu.VMEM((LANES, past), BF16)],
        compiler_params=_cparams(("arbitrary", "arbitrary")),
        name="latent_gqa",
    )(proj, proj, proj, ctx_k, ctx_v)


def _na_kernel(q_ref, k0, k1, k2, k3, v0, v1, v2, v3, ck_ref, cv_ref, bias_ref, o_ref):
    q = q_ref[...]
    ks = [r[...] for r in (k0, k1, k2, k3)] + [ck_ref[0]]
    values_t = [_transpose_bf16(r[...]) for r in (v0, v1, v2, v3)] + [_transpose_bf16(cv_ref[0])]
    outs = []
    for half in range(2):
        keys = [_keep_half(k, half) for k in ks]
        biases = [bias_ref[0, half, j * NA_KBLK:(j + 1) * NA_KBLK, :] for j in range(4)] + [None]
        o = _attend(q, keys, values_t, biases)
        outs.append(o[half * HEAD_DIM:(half + 1) * HEAD_DIM])
    o_ref[...] = jnp.concatenate(outs, axis=0).T


def _na_first_key_block(i, rows):
    per_qblock = NA_QROWS * GRID_W // NA_KBLK
    lead = (NA_KH // 2) * GRID_W // NA_KBLK
    return jnp.clip(per_qblock * i - lead, 0, (rows - NA_KROWS) * GRID_W // NA_KBLK)


def _latent_neighbourhood(proj, ctx_k, ctx_v, bias_t, *, seq):
    t = proj.shape[0]
    b = t // seq
    rows = seq // GRID_W
    nblk = rows // NA_QROWS
    kblk_per_batch = seq // NA_KBLK
    past = ctx_k.shape[1]
    grid = (N_HEADS_B // 2, nblk, b)

    def kv_spec(col0, j):
        return pl.BlockSpec(
            (NA_KBLK, LANES),
            lambda hp, i, bi: (bi * kblk_per_batch + _na_first_key_block(i, rows) + j, col0 // LANES + hp))

    def variant(i):
        return jnp.where(i == 0, 0, jnp.where(i == nblk - 1, 2, 1))

    in_specs = ([pl.BlockSpec((NA_TQ, LANES), lambda hp, i, bi: (bi * nblk + i, COL_QB // LANES + hp))]
                + [kv_spec(COL_KB, j) for j in range(4)]
                + [kv_spec(COL_VB, j) for j in range(4)]
                + [pl.BlockSpec((1, past, LANES), lambda hp, i, bi: (bi, 0, hp)),
                   pl.BlockSpec((1, past, LANES), lambda hp, i, bi: (bi, 0, hp)),
                   pl.BlockSpec((1, 2, NA_TK, NA_TQ), lambda hp, i, bi: (variant(i), hp, 0, 0))])
    return pl.pallas_call(
        _na_kernel,
        out_shape=jax.ShapeDtypeStruct((t, WIDTH_B), F32),
        grid=grid,
        in_specs=in_specs,
        out_specs=pl.BlockSpec((NA_TQ, LANES), lambda hp, i, bi: (bi * nblk + i, hp)),
        compiler_params=_cparams(("arbitrary", "arbitrary", "arbitrary")),
        name="latent_neighbourhood",
    )(proj, *([proj] * 8), ctx_k, ctx_v, bias_t)


def _neighbourhood_bias(rpb, rows):
    nblk = rows // NA_QROWS
    n_dr = 2 * NA_KH - 1
    n_dc = 2 * NA_KW - 1
    kc = np.arange(GRID_W)[:, None]
    qc = np.arange(GRID_W)[None, :]
    ws = np.clip(qc - NA_KW // 2, 0, GRID_W - NA_KW)
    col_ok = (kc >= ws) & (kc < ws + NA_KW)
    dc = np.clip(kc - qc + NA_KW - 1, 0, n_dc - 1)
    dc_onehot = (dc[None] == np.arange(n_dc)[:, None, None]).astype(np.float32)
    tiles = jnp.einsum('hab,bkq->hakq', rpb.astype(F32), jnp.asarray(dc_onehot),
                       precision=lax.Precision.HIGHEST)
    tiles = jnp.where(jnp.asarray(col_ok)[None, None], tiles, MASKED)
    masked_tile = jnp.full((rpb.shape[0], 1, GRID_W, GRID_W), MASKED, F32)
    tiles = jnp.concatenate([tiles, masked_tile], axis=1)
    pick = np.zeros((3, NA_KROWS, NA_QROWS, n_dr + 1), np.float32)
    for v, i in enumerate((0, 1, nblk - 1)):
        r0 = i * NA_QROWS
        ks = int(np.clip(r0 - NA_KH // 2, 0, rows - NA_KROWS))
        for kl in range(NA_KROWS):
            for ql in range(NA_QROWS):
                kr, qr = ks + kl, r0 + ql
                rs = int(np.clip(qr - NA_KH // 2, 0, rows - NA_KH))
                ok = rs <= kr < rs + NA_KH
                pick[v, kl, ql, (kr - qr + NA_KH - 1) if ok else n_dr] = 1.0
    bias = jnp.einsum('vkqa,hacd->vhkcqd', jnp.asarray(pick), tiles, precision=lax.Precision.HIGHEST)
    return bias.reshape(3, rpb.shape[0], NA_TK, NA_TQ)


def _merge_kernel(xp_ref, oap_ref, obp_ref, xs_ref, oas_ref, obs_ref, ona_ref, onb_ref, wo_ref,
                  g1_ref, sh2_ref, sc2_ref, n2_ref, y_ref, h_ref, hp_ref, *, ctx_tiles):
    def one_stream(x_ref, oa_ref, ob_ref):
        na = (_rms(oa_ref[...]) * ona_ref[...]).astype(BF16)
        nb = (_rms(ob_ref[...]) * onb_ref[...]).astype(BF16)
        mix = _dot(na, wo_ref[0:WIDTH_A, :]) + _dot(nb, wo_ref[WIDTH_A:WIDTH_A + WIDTH_B, :])
        y = x_ref[...] + g1_ref[0] * mix
        y_ref[...] = y
        h = _rms(y) * n2_ref[...]
        h = h * (1.0 + sc2_ref[0]) + sh2_ref[0]
        h_ref[...] = h
        hp_ref[...] = _pack_rows(h)

    i = pl.program_id(0)
    pl.when(i < ctx_tiles)(lambda: one_stream(xp_ref, oap_ref, obp_ref))
    pl.when(i >= ctx_tiles)(lambda: one_stream(xs_ref, oas_ref, obs_ref))


def _merge(ctx, lat, on_a, on_b, w_out_bf, gate1, shift2, scale2, norm2, *, tm, lat_seq):
    t_c, d = ctx[0].shape
    t_l = lat[0].shape[0]
    ctx_tiles = t_c // tm
    lat_tiles_per_batch = lat_seq // tm

    def ctx_map(i):
        return (jnp.minimum(i, ctx_tiles - 1), 0)

    def lat_map(i):
        return (jnp.maximum(i - ctx_tiles, 0), 0)

    def mod_map(i):
        return (jnp.where(i < ctx_tiles, 0, 1 + (i - ctx_tiles) // lat_tiles_per_batch), 0, 0)

    def stream_specs(index_map):
        return [pl.BlockSpec((tm, d), index_map),
                pl.BlockSpec((tm, WIDTH_A), index_map),
                pl.BlockSpec((tm, WIDTH_B), index_map)]

    return pl.pallas_call(
        functools.partial(_merge_kernel, ctx_tiles=ctx_tiles),
        out_shape=[jax.ShapeDtypeStruct((t_c + t_l, d), F32)] * 2
        + [jax.ShapeDtypeStruct((t_c + t_l, d // 2), jnp.int32)],
        grid=((t_c + t_l) // tm,),
        in_specs=stream_specs(ctx_map) + stream_specs(lat_map) + [
            pl.BlockSpec((1, WIDTH_A), lambda i: (0, 0)),
            pl.BlockSpec((1, WIDTH_B), lambda i: (0, 0)),
            pl.BlockSpec((WIDTH_A + WIDTH_B, d), lambda i: (0, 0)),
            pl.BlockSpec((1, 1, d), mod_map),
            pl.BlockSpec((1, 1, d), mod_map),
            pl.BlockSpec((1, 1, d), mod_map),
            pl.BlockSpec((1, d), lambda i: (0, 0))],
        out_specs=[pl.BlockSpec((tm, d), lambda i: (i, 0))] * 2 + [pl.BlockSpec((tm, d // 2), lambda i: (i, 0))],
        compiler_params=_cparams(("arbitrary",)),
        name="merge_out_proj",
    )(*ctx, *lat, on_a, on_b, w_out_bf, gate1, shift2, scale2, norm2)


def _first_index_of_max(x, iota):
    mx = jnp.max(x, axis=0, keepdims=True)
    idx = jnp.min(jnp.where(x == mx, iota, float(x.shape[0])), axis=0, keepdims=True)
    return mx, iota == idx


def _router_gates(h, wr_hi, wr_lo, rbias):
    h_hi = h.astype(BF16)
    h_lo = (h - h_hi.astype(F32)).astype(BF16)
    logits = _dot_nt(wr_hi, h_hi) + (_dot_nt(wr_lo, h_hi) + _dot_nt(wr_hi, h_lo))
    scores = _sigmoid(logits)
    sel = scores + rbias
    tm = sel.shape[1]
    iota_g = lax.broadcasted_iota(jnp.int32, (GROUP_SIZE, tm), 0).astype(F32)
    group_scores = []
    for g in range(N_GROUPS):
        grp = sel[g * GROUP_SIZE:(g + 1) * GROUP_SIZE]
        m1, first = _first_index_of_max(grp, iota_g)
        m2 = jnp.max(jnp.where(first, -jnp.inf, grp), axis=0, keepdims=True)
        group_scores.append(m1 + m2)
    gs = jnp.concatenate(group_scores, axis=0)
    iota_n = lax.broadcasted_iota(jnp.int32, (N_GROUPS, tm), 0).astype(F32)
    group_on = jnp.zeros((N_GROUPS, tm), F32)
    for _ in range(TOPK_GROUPS):
        _, pick = _first_index_of_max(gs, iota_n)
        group_on = jnp.where(pick, 1.0, group_on)
        gs = jnp.where(pick, -jnp.inf, gs)
    expert_on = jnp.concatenate(
        [jnp.broadcast_to(group_on[g:g + 1], (GROUP_SIZE, tm)) for g in range(N_GROUPS)], axis=0)
    cand = jnp.where(expert_on > 0.0, sel, -jnp.inf)
    iota_e = lax.broadcasted_iota(jnp.int32, (N_EXPERTS, tm), 0).astype(F32)
    w = jnp.zeros((N_EXPERTS, tm), F32)
    chosen = jnp.zeros((N_EXPERTS, tm), F32)
    for _ in range(TOP_K):
        _, pick = _first_index_of_max(cand, iota_e)
        w = jnp.where(pick, scores, w)
        chosen = jnp.where(pick, 1.0, chosen)
        cand = jnp.where(pick, -jnp.inf, cand)
    return w / jnp.sum(w, axis=0, keepdims=True) * ROUTED_SCALE, chosen


MOE_TS = 512
MOE_ROUTE_TM = 512
MOE_ROW_TM = 512


def _route_kernel(h_ref, wrh_ref, wrl_ref, rb_ref, tri_ref, gates_ref, rank_ref, count_ref):
    @pl.when(pl.program_id(0) == 0)
    def _():
        count_ref[...] = jnp.zeros_like(count_ref)

    gates, chosen = _router_gates(h_ref[...], wrh_ref[...], wrl_ref[...], rb_ref[...])
    gates_ref[...] = gates
    before = _dot(chosen.astype(BF16), tri_ref[...])
    seen = count_ref[...]
    rank_ref[...] = jnp.where(chosen > 0.0, before + seen[:, 0:1], -1.0)
    count_ref[...] = seen + jnp.sum(chosen, axis=1, keepdims=True)


def _route(h_all, wr_hi, wr_lo, rbias):
    t, d = h_all.shape
    tm = MOE_ROUTE_TM
    tri = jnp.asarray(np.triu(np.ones((tm, tm), np.float32), k=1), BF16)
    return pl.pallas_call(
        _route_kernel,
        out_shape=[jax.ShapeDtypeStruct((N_EXPERTS, t), F32),
                   jax.ShapeDtypeStruct((N_EXPERTS, t), F32),
                   jax.ShapeDtypeStruct((N_EXPERTS, LANES), F32)],
        grid=(t // tm,),
        in_specs=[pl.BlockSpec((tm, d), lambda i: (i, 0)),
                  pl.BlockSpec((N_EXPERTS, d), lambda i: (0, 0)),
                  pl.BlockSpec((N_EXPERTS, d), lambda i: (0, 0)),
                  pl.BlockSpec((N_EXPERTS, 1), lambda i: (0, 0)),
                  pl.BlockSpec((tm, tm), lambda i: (0, 0))],
        out_specs=[pl.BlockSpec((N_EXPERTS, tm), lambda i: (0, i)),
                   pl.BlockSpec((N_EXPERTS, tm), lambda i: (0, i)),
                   pl.BlockSpec((N_EXPERTS, LANES), lambda i: (0, 0))],
        compiler_params=_cparams(("arbitrary",)),
        name="moe_route",
    )(h_all, wr_hi, wr_lo, rbias, tri)


def _slots_kernel(gates_ref, rank_ref, off_ref, pos_ref, gtok_ref):
    gates = gates_ref[...]
    rank = rank_ref[...]
    tm = gates.shape[1]
    slot = off_ref[...] + rank
    left = jnp.where(rank >= 0.0, 1.0, 0.0)
    iota_e = lax.broadcasted_iota(jnp.int32, (N_EXPERTS, tm), 0).astype(F32)
    pos_rows, gate_rows = [], []
    for _ in range(TOP_K):
        _, pick = _first_index_of_max(left, iota_e)
        pos_rows.append(jnp.sum(jnp.where(pick, slot, 0.0), axis=0, keepdims=True))
        gate_rows.append(jnp.sum(jnp.where(pick, gates, 0.0), axis=0, keepdims=True))
        left = jnp.where(pick, 0.0, left)
    pos_ref[...] = jnp.concatenate(pos_rows, axis=0).astype(jnp.int32)
    pad = jnp.zeros((LANES - TOP_K, tm), F32)
    gtok_ref[...] = jnp.concatenate(gate_rows + [pad], axis=0).T


def _slots(gates_t, rank_t, off):
    t = gates_t.shape[1]
    tm = MOE_ROUTE_TM
    return pl.pallas_call(
        _slots_kernel,
        out_shape=[jax.ShapeDtypeStruct((TOP_K, t), jnp.int32), jax.ShapeDtypeStruct((t, LANES), F32)],
        grid=(t // tm,),
        in_specs=[pl.BlockSpec((N_EXPERTS, tm), lambda i: (0, i)),
                  pl.BlockSpec((N_EXPERTS, tm), lambda i: (0, i)),
                  pl.BlockSpec((N_EXPERTS, 1), lambda i: (0, 0))],
        out_specs=[pl.BlockSpec((TOP_K, tm), lambda i: (0, i)),
                   pl.BlockSpec((tm, LANES), lambda i: (i, 0))],
        compiler_params=_cparams(("arbitrary",)),
        name="moe_slots",
    )(gates_t, rank_t, off)


def _row_copy(src_ref, src_row, dst_ref, dst_row, sem):
    return pltpu.make_async_copy(src_ref.at[pl.ds(src_row, 1)], dst_ref.at[pl.ds(dst_row, 1)], sem)


SC_CORES = 2
SC_SUBCORES = 16
SC_ROWS = 64


def _dispatch(hp_all, slot_of):
    t, width = hp_all.shape
    n_slots = slot_of.shape[0]
    n_pad = n_slots - TOP_K * t
    workers = SC_CORES * SC_SUBCORES
    per_worker = t // workers
    pad_per_worker = n_pad // workers
    assert per_worker * workers == t and per_worker % SC_ROWS == 0
    assert pad_per_worker * workers == n_pad and pad_per_worker % SC_ROWS == 0
    mesh = plsc.VectorSubcoreMesh(core_axis_name="core", subcore_axis_name="subcore")

    @functools.partial(
        pl.kernel, mesh=mesh,
        out_type=jax.ShapeDtypeStruct((n_slots, width), jnp.int32),
        scratch_types=[pltpu.VMEM((SC_ROWS,), jnp.int32),
                       pltpu.VMEM((SC_ROWS, width), jnp.int32),
                       pltpu.SemaphoreType.DMA],
    )
    def scatter_rows(h_hbm, slot_hbm, out_hbm, idx_v, rows_v, sem):
        worker = lax.axis_index("subcore") * SC_CORES + lax.axis_index("core")
        base = worker * per_worker

        @pl.loop(0, per_worker // SC_ROWS)
        def _(j):
            first = base + j * SC_ROWS
            pltpu.sync_copy(h_hbm.at[pl.ds(first, SC_ROWS)], rows_v)
            for k in range(TOP_K):
                pltpu.sync_copy(slot_hbm.at[pl.ds(k * t + first, SC_ROWS)], idx_v)
                pltpu.async_copy(rows_v, out_hbm.at[idx_v], sem).wait()

        pltpu.sync_copy(h_hbm.at[pl.ds(0, SC_ROWS)], rows_v)
        pad_base = TOP_K * t + worker * pad_per_worker

        @pl.loop(0, pad_per_worker // SC_ROWS)
        def _(j):
            pltpu.sync_copy(slot_hbm.at[pl.ds(pad_base + j * SC_ROWS, SC_ROWS)], idx_v)
            pltpu.async_copy(rows_v, out_hbm.at[idx_v], sem).wait()

    return scatter_rows(hp_all, slot_of)


def _experts_kernel(te_ref, xs_ref, wg_ref, wu_ref, wd_ref, ys_ref, wg_bf, wu_bf, wd_bf):
    i = pl.program_id(0)

    @pl.when((i == 0) | (te_ref[i] != te_ref[jnp.maximum(i, 1) - 1]))
    def _():
        wg_bf[...] = wg_ref[0].astype(BF16)
        wu_bf[...] = wu_ref[0].astype(BF16)
        wd_bf[...] = wd_ref[0].astype(BF16)

    left, right = _unpack_rows(xs_ref[...])
    x = jnp.concatenate([left, right], axis=1).astype(BF16)
    g = _dot(x, wg_bf[...])
    u = _dot(x, wu_bf[...])
    act = (g * _sigmoid(g)) * u
    ys_ref[...] = _pack_rows(_dot(act.astype(BF16), wd_bf[...]))


def _experts(xs, tile_expert, w_gate, w_up, w_down):
    n_slots, width = xs.shape
    d = 2 * width
    ts = MOE_TS
    return pl.pallas_call(
        _experts_kernel,
        out_shape=jax.ShapeDtypeStruct((n_slots, width), jnp.int32),
        grid_spec=pltpu.PrefetchScalarGridSpec(
            num_scalar_prefetch=1,
            grid=(n_slots // ts,),
            in_specs=[pl.BlockSpec((ts, width), lambda i, te: (i, 0)),
                      pl.BlockSpec((1, d, D_EXPERT), lambda i, te: (te[i], 0, 0)),
                      pl.BlockSpec((1, d, D_EXPERT), lambda i, te: (te[i], 0, 0)),
                      pl.BlockSpec((1, D_EXPERT, d), lambda i, te: (te[i], 0, 0))],
            out_specs=pl.BlockSpec((ts, width), lambda i, te: (i, 0)),
            scratch_shapes=[pltpu.VMEM((d, D_EXPERT), BF16), pltpu.VMEM((d, D_EXPERT), BF16),
                            pltpu.VMEM((D_EXPERT, d), BF16)]),
        compiler_params=_cparams(("arbitrary",)),
        name="moe_experts",
    )(tile_expert, xs, w_gate, w_up, w_down)


def _combine_kernel(y_ref, h_ref, g2_ref, gtok_ref, pos_ref, wgs_ref, wus_ref, wds_ref, ys_ref, o_ref,
                    rows_ref, sem):
    tm = y_ref.shape[0]

    def start(t, c):
        for k in range(TOP_K):
            _row_copy(ys_ref, pos_ref[k, t], rows_ref.at[k], t, sem).start(priority=k % 2)
        return c

    lax.fori_loop(0, tm, start, 0, unroll=8)

    h_left, h_right = _unpack_rows(h_ref[...])
    h = jnp.concatenate([h_left, h_right], axis=1).astype(BF16)
    gs = _dot(h, wgs_ref[...])
    us = _dot(h, wus_ref[...])
    shared = _dot(((gs * _sigmoid(gs)) * us).astype(BF16), wds_ref[...])

    def wait(t, c):
        for k in range(TOP_K):
            _row_copy(ys_ref, 0, rows_ref.at[k], 0, sem).wait()
        return c

    lax.fori_loop(0, tm, wait, 0, unroll=8)

    gtok = gtok_ref[...]
    acc_left = acc_right = None
    for k in range(TOP_K):
        left, right = _unpack_rows(rows_ref[k])
        gate = gtok[:, k:k + 1]
        acc_left = gate * left if acc_left is None else acc_left + gate * left
        acc_right = gate * right if acc_right is None else acc_right + gate * right
    routed = jnp.concatenate([acc_left, acc_right], axis=1)
    o_ref[...] = y_ref[...] + g2_ref[0] * (routed + shared)


def _combine(y_all, hp_all, gate2, gtok, pos, ys, wgs, wus, wds, *, first_token, tokens, seq):
    d = y_all.shape[1]
    width = hp_all.shape[1]
    tm = MOE_ROW_TM
    tile0 = first_token // tm
    nb = gate2.shape[0]
    tiles_per_batch = seq // tm

    def mod_map(i):
        return ((i // tiles_per_batch) if nb > 1 else 0, 0, 0)

    return pl.pallas_call(
        _combine_kernel,
        out_shape=jax.ShapeDtypeStruct((tokens, d), F32),
        grid=(tokens // tm,),
        in_specs=[pl.BlockSpec((tm, d), lambda i: (tile0 + i, 0)),
                  pl.BlockSpec((tm, width), lambda i: (tile0 + i, 0)),
                  pl.BlockSpec((1, 1, d), mod_map),
                  pl.BlockSpec((tm, LANES), lambda i: (tile0 + i, 0)),
                  pl.BlockSpec((TOP_K, tm), lambda i: (0, tile0 + i), memory_space=pltpu.SMEM),
                  pl.BlockSpec((d, D_SHARED), lambda i: (0, 0)),
                  pl.BlockSpec((d, D_SHARED), lambda i: (0, 0)),
                  pl.BlockSpec((D_SHARED, d), lambda i: (0, 0)),
                  pl.BlockSpec(memory_space=pl.ANY)],
        out_specs=pl.BlockSpec((tm, d), lambda i: (i, 0)),
        scratch_shapes=[pltpu.VMEM((TOP_K, tm, width), jnp.int32), pltpu.SemaphoreType.DMA(())],
        compiler_params=_cparams(("arbitrary",)),
        name="moe_combine",
    )(y_all, hp_all, gate2, gtok, pos, wgs, wus, wds, ys)


def _expert_layout(counts, n_tiles):
    cnt = counts.astype(jnp.int32)
    tiles = (cnt + (MOE_TS - 1)) // MOE_TS
    last_tile = jnp.cumsum(tiles)
    off = (last_tile - tiles) * MOE_TS
    pad_lo = off + cnt
    pad_hi = (off + tiles * MOE_TS).at[N_EXPERTS - 1].set(n_tiles * MOE_TS)
    pad_cnt = pad_hi - pad_lo
    pad_last = jnp.cumsum(pad_cnt)
    shift = pad_lo - (pad_last - pad_cnt)
    j = jnp.arange(N_EXPERTS * MOE_TS, dtype=jnp.int32)
    past = (pad_last[None, :-1] <= j[:, None]).astype(jnp.int32)
    pad_slots = j + shift[0] + jnp.sum(past * (shift[1:] - shift[:-1])[None, :], axis=1)
    tile_ids = jnp.arange(n_tiles, dtype=jnp.int32)
    tile_expert = jnp.minimum(
        jnp.sum((last_tile[None, :] <= tile_ids[:, None]).astype(jnp.int32), axis=1), N_EXPERTS - 1)
    return off, pad_slots, tile_expert


def _rope_tables(n_tokens):
    t = jnp.arange(n_tokens)
    row = (t // GRID_W).astype(F32)
    col = (t % GRID_W).astype(F32)
    nf = HEAD_DIM // 4
    freqs = ROPE_THETA ** (-jnp.arange(nf, dtype=F32) / nf)
    ang_r = row[:, None] * freqs
    ang_c = col[:, None] * freqs
    cos = jnp.concatenate([jnp.cos(ang_r)] * 2 + [jnp.cos(ang_c)] * 2, axis=1)
    sin = jnp.concatenate([-jnp.sin(ang_r), jnp.sin(ang_r), -jnp.sin(ang_c), jnp.sin(ang_c)], axis=1)
    reps = LANES // HEAD_DIM
    return jnp.tile(cos, (1, reps)), jnp.tile(sin, (1, reps))


def _head_gains(qn_a, kn_a, qn_b, kn_b):
    ones = jnp.ones((HEAD_DIM,), F32)
    parts = ([qn_a] * N_HEADS_A + [kn_a] * N_KV_A + [ones] * N_KV_A
             + [qn_b] * N_HEADS_B + [kn_b] * N_HEADS_B + [ones] * N_HEADS_B)
    return jnp.concatenate(parts).reshape(1, IN_COLS).astype(F32)


def _same_head_indicator():
    i = np.arange(LANES)
    return jnp.asarray((i[:, None] // HEAD_DIM) == (i[None, :] // HEAD_DIM), BF16)


def _token_major(cache):
    b, h, s, hd = cache.shape
    return cache.transpose(0, 2, 1, 3).reshape(b, s, h * hd).astype(BF16)


def kernel(x_prompt, x_sample, cache_k_a, cache_v_a, cache_k_b, cache_v_b, c, c_ctx, w_mod, b_mod, norm1, norm2, w_in, qn_a, kn_a, qn_b, kn_b, rpb, on_a, on_b, w_out, w_router, router_bias, w_gate_e, w_up_e, w_down_e, w_gate_s, w_up_s, w_down_s):
    depth = w_mod.shape[0]
    assert depth == 1
    l = 0
    bp, sp, d = x_prompt.shape
    bs, ss, _ = x_sample.shape

    cvec = jnp.concatenate([c_ctx[None, :], c], axis=0)
    rows = -(-cvec.shape[0] // 8) * 8
    cvec = jnp.pad(cvec, ((0, rows - cvec.shape[0]), (0, 0)))
    mod = _adaln(cvec, w_mod[l], b_mod[l])
    mod_p = [m.reshape(1, 1, d) for m in jnp.split(mod[0:1], 6, axis=-1)]
    mod_s = [m.reshape(bs, 1, d) for m in jnp.split(mod[1:1 + bs], 6, axis=-1)]
    mod_all = [m.reshape(1 + bs, 1, d) for m in jnp.split(mod[0:1 + bs], 6, axis=-1)]

    w_in_bf = w_in[l].astype(BF16)
    w_out_bf = w_out[l].astype(BF16)
    gain = _head_gains(qn_a[l], kn_a[l], qn_b[l], kn_b[l])
    seg = _same_head_indicator()
    n1 = norm1[l].reshape(1, d)
    n2 = norm2[l].reshape(1, d)
    ona = on_a[l].reshape(1, WIDTH_A)
    onb = on_b[l].reshape(1, WIDTH_B)
    wr_t = w_router[l].T
    wr_hi = wr_t.astype(BF16)
    wr_lo = (wr_t - wr_hi.astype(F32)).astype(BF16)
    rbias = router_bias[l].reshape(N_EXPERTS, 1).astype(F32)
    wgs = w_gate_s[l].astype(BF16)
    wus = w_up_s[l].astype(BF16)
    wds = w_down_s[l].astype(BF16)
    t_p = bp * sp
    t_s = bs * ss
    t_all = t_p + t_s

    xp = x_prompt.reshape(t_p, d)
    proj_p, st_ka, st_va, st_kb, st_vb = _project(
        xp, mod_p[0], mod_p[1], n1, w_in_bf, gain, seg, None, tm=sp, seq=sp, states=True)
    oa_p, ob_p = _context_attention(proj_p, seq=sp)

    xs = x_sample.reshape(t_s, d)
    proj_s, = _project(xs, mod_s[0], mod_s[1], n1, w_in_bf, gain, seg, _rope_tables(ss),
                       tm=512, seq=ss, states=False)
    oa_s = _latent_gqa(proj_s, _token_major(cache_k_a[:, l]), _token_major(cache_v_a[:, l]), seq=ss, tq=128)
    bias_t = _neighbourhood_bias(rpb[l], ss // GRID_W)
    ob_s = _latent_neighbourhood(proj_s, _token_major(cache_k_b[:, l]), _token_major(cache_v_b[:, l]),
                                 bias_t, seq=ss)

    y1_all, h_all, hp_all = _merge((xp, oa_p, ob_p), (xs, oa_s, ob_s), ona, onb, w_out_bf,
                                   mod_all[2], mod_all[3], mod_all[4], n2, tm=512, lat_seq=ss)
    gates_t, rank_t, counts = _route(h_all, wr_hi, wr_lo, rbias)
    n_tiles = t_all * TOP_K // MOE_TS + N_EXPERTS
    off, pad_slots, tile_expert = _expert_layout(counts[:, 0], n_tiles)
    pos, gtok = _slots(gates_t, rank_t, off.astype(F32).reshape(N_EXPERTS, 1))
    x_slots = _dispatch(hp_all, jnp.concatenate([pos.reshape(TOP_K * t_all), pad_slots]))
    y_slots = _experts(x_slots, tile_expert, w_gate_e[l], w_up_e[l], w_down_e[l])
    y_p = _combine(y1_all, hp_all, mod_p[5], gtok, pos, y_slots, wgs, wus, wds,
                   first_token=0, tokens=t_p, seq=sp)
    y_s = _combine(y1_all, hp_all, mod_s[5], gtok, pos, y_slots, wgs, wus, wds,
                   first_token=t_p, tokens=t_s, seq=ss)

    return (y_p.reshape(bp, sp, d), y_s.reshape(bs, ss, d), st_ka, st_va, st_kb, st_vb)
```

```python
import functools

import numpy as np
import jax
import jax.numpy as jnp
from jax import lax
from jax.experimental import pallas as pl
from jax.experimental.pallas import tpu as pltpu
from jax.experimental.pallas import tpu_sc as plsc

F32 = jnp.float32
BF16 = jnp.bfloat16

D_MODEL = 1024
HEAD_DIM = 64
N_HEADS_A = 8
N_KV_A = 2
GROUP_A = N_HEADS_A // N_KV_A
N_HEADS_B = 8
WIDTH_A = N_HEADS_A * HEAD_DIM
WIDTH_B = N_HEADS_B * HEAD_DIM
KV_WIDTH_A = N_KV_A * HEAD_DIM
IN_COLS = WIDTH_A + 2 * KV_WIDTH_A + 3 * WIDTH_B
GRID_W = 64
ROPE_THETA = 10000.0
NA_KH = 8
NA_KW = 16
N_EXPERTS = 64
N_GROUPS = 8
GROUP_SIZE = N_EXPERTS // N_GROUPS
TOPK_GROUPS = 4
TOP_K = 8
D_EXPERT = 256
D_SHARED = 256
ROUTED_SCALE = 2.5
EPS = 1e-6

LANES = 128
MXU_DIM = 256
MASKED = -1e30

COL_QA = 0
COL_KA = WIDTH_A
COL_VA = COL_KA + KV_WIDTH_A
COL_QB = COL_VA + KV_WIDTH_A
COL_KB = COL_QB + WIDTH_B
COL_VB = COL_KB + WIDTH_B

NA_QROWS = 8
NA_KROWS = 2 * NA_KH
NA_TQ = NA_QROWS * GRID_W
NA_TK = NA_KROWS * GRID_W
NA_KBLK = 256

VMEM_LIMIT = 56 * 1024 * 1024


def _cparams(sem):
    return pltpu.CompilerParams(dimension_semantics=sem, vmem_limit_bytes=VMEM_LIMIT)


def _dot(a, b):
    return jnp.dot(a, b, preferred_element_type=F32)


def _dot_nt(a, b):
    return lax.dot_general(a, b, (((1,), (1,)), ((), ())), preferred_element_type=F32)


def _sigmoid(x):
    return 1.0 / (1.0 + jnp.exp(-x))


def _rms(x):
    return x * lax.rsqrt(jnp.mean(x * x, axis=-1, keepdims=True) + EPS)


def _pack_rows(x):
    n = x.shape[1] // 2
    hi = lax.bitcast_convert_type(x[:, :n].astype(BF16).astype(F32), jnp.int32)
    lo = lax.bitcast_convert_type(x[:, n:].astype(BF16).astype(F32), jnp.int32)
    return hi | lax.shift_right_logical(lo, 16)


def _unpack_rows(w):
    left = lax.bitcast_convert_type(w & jnp.int32(-65536), F32)
    right = lax.bitcast_convert_type(lax.shift_left(w, 16), F32)
    return left, right


def _mod_kernel(c_ref, w_ref, b_ref, o_ref):
    c = c_ref[...]
    s = c * _sigmoid(c)
    o_ref[...] = jnp.dot(s, w_ref[...], preferred_element_type=F32,
                         precision=lax.Precision.HIGHEST) + b_ref[...]


def _adaln(cvec, w_mod, b_mod):
    rows, d = cvec.shape
    n = w_mod.shape[1]
    tn = 512
    return pl.pallas_call(
        _mod_kernel,
        out_shape=jax.ShapeDtypeStruct((rows, n), F32),
        grid=(n // tn,),
        in_specs=[pl.BlockSpec((rows, d), lambda j: (0, 0)),
                  pl.BlockSpec((d, tn), lambda j: (0, j)),
                  pl.BlockSpec((1, tn), lambda j: (0, j))],
        out_specs=pl.BlockSpec((rows, tn), lambda j: (0, j)),
        compiler_params=_cparams(("arbitrary",)),
        name="adaln_mod",
    )(cvec, w_mod, b_mod.reshape(1, n))


_PROJ_CHUNKS = (
    [(COL_QA + i * LANES, LANES, True, True) for i in range(WIDTH_A // LANES)]
    + [(COL_KA, LANES, True, True), (COL_VA, LANES, False, False)]
    + [(COL_QB + i * LANES, LANES, True, False) for i in range(WIDTH_B // LANES)]
    + [(COL_KB + i * LANES, LANES, True, False) for i in range(WIDTH_B // LANES)]
    + [(COL_VB + i * LANES, LANES, False, False) for i in range(WIDTH_B // LANES)]
)


def _proj_kernel(*refs, rope, states):
    x_ref, sh_ref, sc_ref, n1_ref, w_ref, gain_ref, seg_ref = refs[:7]
    pos = 7
    if rope:
        cos_ref, sin_ref = refs[pos:pos + 2]
        pos += 2
    out_ref = refs[pos]
    pos += 1
    if states:
        ka_ref, va_ref, kb_ref, vb_ref = refs[pos:pos + 4]

    x = x_ref[...]
    h = _rms(x) * n1_ref[...]
    h = h * (1.0 + sc_ref[0]) + sh_ref[0]
    p = _dot(h.astype(BF16), w_ref[...])
    seg = seg_ref[...]
    if rope:
        cos = cos_ref[...]
        sin = sin_ref[...]
        lane = lax.broadcasted_iota(jnp.int32, cos.shape, 1)
        first_half = (lane % (HEAD_DIM // 2)) < (HEAD_DIM // 4)

    for c0, w, normed, roped in _PROJ_CHUNKS:
        pc = p[:, c0:c0 + w]
        if normed:
            sq = pc * pc
            hi = sq.astype(BF16)
            lo = (sq - hi.astype(F32)).astype(BF16)
            ss = _dot(hi, seg) + _dot(lo, seg)
            pc = pc * lax.rsqrt(ss * (1.0 / HEAD_DIM) + EPS) * gain_ref[:, c0:c0 + w]
        if states:
            if c0 == COL_KA:
                for hh in range(N_KV_A):
                    ka_ref[0, 0, hh] = pc[:, hh * HEAD_DIM:(hh + 1) * HEAD_DIM]
            elif c0 == COL_VA:
                for hh in range(N_KV_A):
                    va_ref[0, 0, hh] = pc[:, hh * HEAD_DIM:(hh + 1) * HEAD_DIM]
            elif COL_KB <= c0 < COL_VB:
                base = (c0 - COL_KB) // HEAD_DIM
                for hh in range(LANES // HEAD_DIM):
                    kb_ref[0, 0, base + hh] = pc[:, hh * HEAD_DIM:(hh + 1) * HEAD_DIM]
            elif c0 >= COL_VB:
                base = (c0 - COL_VB) // HEAD_DIM
                for hh in range(LANES // HEAD_DIM):
                    vb_ref[0, 0, base + hh] = pc[:, hh * HEAD_DIM:(hh + 1) * HEAD_DIM]
        if rope and roped:
            partner = jnp.where(first_half,
                                pltpu.roll(pc, LANES - HEAD_DIM // 4, 1),
                                pltpu.roll(pc, HEAD_DIM // 4, 1))
            pc = pc * cos + partner * sin
        if c0 < COL_KA or COL_QB <= c0 < COL_KB:
            pc = pc * (HEAD_DIM ** -0.5)
        out_ref[:, c0:c0 + w] = pc.astype(BF16)


def _project(x2d, shift, scale, norm1, w_in_bf, gain, seg, rope_tabs, *, tm, seq, states):
    t, d = x2d.shape
    nb = shift.shape[0]
    tiles_per_batch = seq // tm
    rope = rope_tabs is not None

    def mod_map(i):
        return ((i // tiles_per_batch) if nb > 1 else 0, 0, 0)

    in_specs = [pl.BlockSpec((tm, d), lambda i: (i, 0)),
                pl.BlockSpec((1, 1, d), mod_map),
                pl.BlockSpec((1, 1, d), mod_map),
                pl.BlockSpec((1, d), lambda i: (0, 0)),
                pl.BlockSpec((d, IN_COLS), lambda i: (0, 0)),
                pl.BlockSpec((1, IN_COLS), lambda i: (0, 0)),
                pl.BlockSpec((LANES, LANES), lambda i: (0, 0))]
    args = [x2d, shift, scale, norm1, w_in_bf, gain, seg]
    if rope:
        in_specs += [pl.BlockSpec((tm, LANES), lambda i: (i % tiles_per_batch, 0))] * 2
        args += list(rope_tabs)
    out_shape = [jax.ShapeDtypeStruct((t, IN_COLS), BF16)]
    out_specs = [pl.BlockSpec((tm, IN_COLS), lambda i: (i, 0))]
    if states:
        assert tm == seq
        b = t // seq
        for nh in (N_KV_A, N_KV_A, N_HEADS_B, N_HEADS_B):
            out_shape.append(jax.ShapeDtypeStruct((b, 1, nh, seq, HEAD_DIM), F32))
            out_specs.append(pl.BlockSpec((1, 1, nh, seq, HEAD_DIM), lambda i: (i, 0, 0, 0, 0)))
    return pl.pallas_call(
        functools.partial(_proj_kernel, rope=rope, states=states),
        out_shape=out_shape,
        grid=(t // tm,),
        in_specs=in_specs,
        out_specs=out_specs,
        compiler_params=_cparams(("arbitrary",)),
        name="proj_states" if states else "proj_rope",
    )(*args)


def _lane_half(shape):
    return lax.broadcasted_iota(jnp.int32, shape, 1) // HEAD_DIM


def _keep_half(x, half):
    return jnp.where(_lane_half(x.shape) == half, x, jnp.zeros_like(x))


def _transpose_bf16(x):
    return x.astype(F32).T.astype(BF16)


def _attend(q, keys, values_t, biases):
    scores = []
    for k, b in zip(keys, biases):
        s = _dot_nt(k, q)
        if b is not None:
            s = s + b
        scores.append(s)
    m = functools.reduce(jnp.maximum, [jnp.max(s, axis=0, keepdims=True) for s in scores])
    denom = None
    out = None
    for s, vt in zip(scores, values_t):
        p = jnp.exp(s - m)
        ps = jnp.sum(p, axis=0, keepdims=True)
        po = _dot(vt, p.astype(BF16))
        denom = ps if denom is None else denom + ps
        out = po if out is None else out + po
    return out / denom


def _swap_halves(q_bf16):
    return pltpu.roll(q_bf16.astype(F32), HEAD_DIM, 1).astype(BF16)


def _gqa_heads(q_of_pair, keys_by_group, values_t):
    outs = []
    for h in range(N_HEADS_A):
        g = h // GROUP_A
        q = q_of_pair(h // 2)
        if h % 2 != g:
            q = _swap_halves(q)
        o = _attend(q, keys_by_group[g], values_t, [None] * len(values_t))
        outs.append(o[g * HEAD_DIM:(g + 1) * HEAD_DIM])
    return jnp.concatenate(outs, axis=0)


def _ctx_attn_kernel(p_ref, oa_ref, ob_ref):
    ka = p_ref[:, COL_KA:COL_KA + LANES]
    va_t = [_transpose_bf16(p_ref[:, COL_VA:COL_VA + LANES])]
    keys_by_group = [[_keep_half(ka, g)] for g in range(N_KV_A)]
    oa = _gqa_heads(lambda i: p_ref[:, COL_QA + i * LANES:COL_QA + (i + 1) * LANES],
                    keys_by_group, va_t)
    oa_ref[...] = oa.T

    outs = []
    for i in range(N_HEADS_B // 2):
        q = p_ref[:, COL_QB + i * LANES:COL_QB + (i + 1) * LANES]
        k = p_ref[:, COL_KB + i * LANES:COL_KB + (i + 1) * LANES]
        vt = [_transpose_bf16(p_ref[:, COL_VB + i * LANES:COL_VB + (i + 1) * LANES])]
        for half in range(2):
            o = _attend(q, [_keep_half(k, half)], vt, [None])
            outs.append(o[half * HEAD_DIM:(half + 1) * HEAD_DIM])
    ob_ref[...] = jnp.concatenate(outs, axis=0).T


def _context_attention(proj, *, seq):
    t = proj.shape[0]
    return pl.pallas_call(
        _ctx_attn_kernel,
        out_shape=[jax.ShapeDtypeStruct((t, WIDTH_A), F32), jax.ShapeDtypeStruct((t, WIDTH_B), F32)],
        grid=(t // seq,),
        in_specs=[pl.BlockSpec((seq, IN_COLS), lambda i: (i, 0))],
        out_specs=[pl.BlockSpec((seq, WIDTH_A), lambda i: (i, 0)),
                   pl.BlockSpec((seq, WIDTH_B), lambda i: (i, 0))],
        compiler_params=_cparams(("arbitrary",)),
        name="context_attention",
    )(proj)


def _gqa_latent_kernel(q_ref, k_ref, v_ref, ck_ref, cv_ref, o_ref, kg_ref, ckg_ref, vt_ref, cvt_ref):
    @pl.when(pl.program_id(1) == 0)
    def _():
        k = k_ref[...]
        ck = ck_ref[0]
        for g in range(N_KV_A):
            kg_ref[g] = _keep_half(k, g)
            ckg_ref[g] = _keep_half(ck, g)
        vt_ref[...] = _transpose_bf16(v_ref[...])
        cvt_ref[...] = _transpose_bf16(cv_ref[0])

    tq = q_ref.shape[0]
    scores = []
    for g in range(N_KV_A):
        qs = []
        for j in range(GROUP_A):
            h = g * GROUP_A + j
            q = q_ref[:, (h // 2) * LANES:(h // 2 + 1) * LANES]
            qs.append(q if h % 2 == g else _swap_halves(q))
        qg = jnp.concatenate(qs, axis=0)
        scores.append((_dot_nt(kg_ref[g], qg), _dot_nt(ckg_ref[g], qg)))
    probs = []
    for s_lat, s_ctx in scores:
        m = jnp.maximum(jnp.max(s_lat, axis=0, keepdims=True), jnp.max(s_ctx, axis=0, keepdims=True))
        p_lat = jnp.exp(s_lat - m)
        p_ctx = jnp.exp(s_ctx - m)
        denom = jnp.sum(p_lat, axis=0, keepdims=True) + jnp.sum(p_ctx, axis=0, keepdims=True)
        probs.append((p_lat.astype(BF16), p_ctx.astype(BF16), denom))
    outs = []
    for g, (p_lat, p_ctx, denom) in enumerate(probs):
        o = _dot(vt_ref[...], p_lat) + _dot(cvt_ref[...], p_ctx)
        o = (o / denom)[g * HEAD_DIM:(g + 1) * HEAD_DIM]
        outs += [o[:, j * tq:(j + 1) * tq] for j in range(GROUP_A)]
    o_ref[...] = jnp.concatenate(outs, axis=0).T


def _latent_gqa(proj, ctx_k, ctx_v, *, seq, tq):
    t = proj.shape[0]
    b = t // seq
    nq = seq // tq
    past = ctx_k.shape[1]
    return pl.pallas_call(
        _gqa_latent_kernel,
        out_shape=jax.ShapeDtypeStruct((t, WIDTH_A), F32),
        grid=(b, nq),
        in_specs=[pl.BlockSpec((tq, WIDTH_A), lambda bi, qi: (bi * nq + qi, 0)),
                  pl.BlockSpec((seq, LANES), lambda bi, qi: (bi, COL_KA // LANES)),
                  pl.BlockSpec((seq, LANES), lambda bi, qi: (bi, COL_VA // LANES)),
                  pl.BlockSpec((1, past, LANES), lambda bi, qi: (bi, 0, 0)),
                  pl.BlockSpec((1, past, LANES), lambda bi, qi: (bi, 0, 0))],
        out_specs=pl.BlockSpec((tq, WIDTH_A), lambda bi, qi: (bi * nq + qi, 0)),
        scratch_shapes=[pltpu.VMEM((N_KV_A, seq, LANES), BF16),
                        pltpu.VMEM((N_KV_A, past, LANES), BF16),
                        pltpu.VMEM((LANES, seq), BF16),
                        pltpu.VMEM((LANES, past), BF16)],
        compiler_params=_cparams(("arbitrary", "arbitrary")),
        name="latent_gqa",
    )(proj, proj, proj, ctx_k, ctx_v)


def _na_kernel(q_ref, k0, k1, k2, k3, v0, v1, v2, v3, ck_ref, cv_ref, bias_ref, o_ref):
    q = q_ref[...]
    ks = [r[...] for r in (k0, k1, k2, k3)] + [ck_ref[0]]
    values_t = [_transpose_bf16(r[...]) for r in (v0, v1, v2, v3)] + [_transpose_bf16(cv_ref[0])]
    outs = []
    for half in range(2):
        keys = [_keep_half(k, half) for k in ks]
        biases = [bias_ref[0, half, j * NA_KBLK:(j + 1) * NA_KBLK, :] for j in range(4)] + [None]
        o = _attend(q, keys, values_t, biases)
        outs.append(o[half * HEAD_DIM:(half + 1) * HEAD_DIM])
    o_ref[...] = jnp.concatenate(outs, axis=0).T


def _na_first_key_block(i, rows):
    per_qblock = NA_QROWS * GRID_W // NA_KBLK
    lead = (NA_KH // 2) * GRID_W // NA_KBLK
    return jnp.clip(per_qblock * i - lead, 0, (rows - NA_KROWS) * GRID_W // NA_KBLK)


def _latent_neighbourhood(proj, ctx_k, ctx_v, bias_t, *, seq):
    t = proj.shape[0]
    b = t // seq
    rows = seq // GRID_W
    nblk = rows // NA_QROWS
    kblk_per_batch = seq // NA_KBLK
    past = ctx_k.shape[1]
    grid = (N_HEADS_B // 2, nblk, b)

    def kv_spec(col0, j):
        return pl.BlockSpec(
            (NA_KBLK, LANES),
            lambda hp, i, bi: (bi * kblk_per_batch + _na_first_key_block(i, rows) + j, col0 // LANES + hp))

    def variant(i):
        return jnp.where(i == 0, 0, jnp.where(i == nblk - 1, 2, 1))

    in_specs = ([pl.BlockSpec((NA_TQ, LANES), lambda hp, i, bi: (bi * nblk + i, COL_QB // LANES + hp))]
                + [kv_spec(COL_KB, j) for j in range(4)]
                + [kv_spec(COL_VB, j) for j in range(4)]
                + [pl.BlockSpec((1, past, LANES), lambda hp, i, bi: (bi, 0, hp)),
                   pl.BlockSpec((1, past, LANES), lambda hp, i, bi: (bi, 0, hp)),
                   pl.BlockSpec((1, 2, NA_TK, NA_TQ), lambda hp, i, bi: (variant(i), hp, 0, 0))])
    return pl.pallas_call(
        _na_kernel,
        out_shape=jax.ShapeDtypeStruct((t, WIDTH_B), F32),
        grid=grid,
        in_specs=in_specs,
        out_specs=pl.BlockSpec((NA_TQ, LANES), lambda hp, i, bi: (bi * nblk + i, hp)),
        compiler_params=_cparams(("arbitrary", "arbitrary", "arbitrary")),
        name="latent_neighbourhood",
    )(proj, *([proj] * 8), ctx_k, ctx_v, bias_t)


def _neighbourhood_bias(rpb, rows):
    nblk = rows // NA_QROWS
    n_dr = 2 * NA_KH - 1
    n_dc = 2 * NA_KW - 1
    kc = np.arange(GRID_W)[:, None]
    qc = np.arange(GRID_W)[None, :]
    ws = np.clip(qc - NA_KW // 2, 0, GRID_W - NA_KW)
    col_ok = (kc >= ws) & (kc < ws + NA_KW)
    dc = np.clip(kc - qc + NA_KW - 1, 0, n_dc - 1)
    dc_onehot = (dc[None] == np.arange(n_dc)[:, None, None]).astype(np.float32)
    tiles = jnp.einsum('hab,bkq->hakq', rpb.astype(F32), jnp.asarray(dc_onehot),
                       precision=lax.Precision.HIGHEST)
    tiles = jnp.where(jnp.asarray(col_ok)[None, None], tiles, MASKED)
    masked_tile = jnp.full((rpb.shape[0], 1, GRID_W, GRID_W), MASKED, F32)
    tiles = jnp.concatenate([tiles, masked_tile], axis=1)
    pick = np.zeros((3, NA_KROWS, NA_QROWS, n_dr + 1), np.float32)
    for v, i in enumerate((0, 1, nblk - 1)):
        r0 = i * NA_QROWS
        ks = int(np.clip(r0 - NA_KH // 2, 0, rows - NA_KROWS))
        for kl in range(NA_KROWS):
            for ql in range(NA_QROWS):
                kr, qr = ks + kl, r0 + ql
                rs = int(np.clip(qr - NA_KH // 2, 0, rows - NA_KH))
                ok = rs <= kr < rs + NA_KH
                pick[v, kl, ql, (kr - qr + NA_KH - 1) if ok else n_dr] = 1.0
    bias = jnp.einsum('vkqa,hacd->vhkcqd', jnp.asarray(pick), tiles, precision=lax.Precision.HIGHEST)
    return bias.reshape(3, rpb.shape[0], NA_TK, NA_TQ)


def _merge_kernel(xp_ref, oap_ref, obp_ref, xs_ref, oas_ref, obs_ref, ona_ref, onb_ref, wo_ref,
                  g1_ref, sh2_ref, sc2_ref, n2_ref, y_ref, h_ref, hp_ref, *, ctx_tiles):
    def one_stream(x_ref, oa_ref, ob_ref):
        na = (_rms(oa_ref[...]) * ona_ref[...]).astype(BF16)
        nb = (_rms(ob_ref[...]) * onb_ref[...]).astype(BF16)
        mix = _dot(na, wo_ref[0:WIDTH_A, :]) + _dot(nb, wo_ref[WIDTH_A:WIDTH_A + WIDTH_B, :])
        y = x_ref[...] + g1_ref[0] * mix
        y_ref[...] = y
        h = _rms(y) * n2_ref[...]
        h = h * (1.0 + sc2_ref[0]) + sh2_ref[0]
        h_ref[...] = h
        hp_ref[...] = _pack_rows(h)

    i = pl.program_id(0)
    pl.when(i < ctx_tiles)(lambda: one_stream(xp_ref, oap_ref, obp_ref))
    pl.when(i >= ctx_tiles)(lambda: one_stream(xs_ref, oas_ref, obs_ref))


def _merge(ctx, lat, on_a, on_b, w_out_bf, gate1, shift2, scale2, norm2, *, tm, lat_seq):
    t_c, d = ctx[0].shape
    t_l = lat[0].shape[0]
    ctx_tiles = t_c // tm
    lat_tiles_per_batch = lat_seq // tm

    def ctx_map(i):
        return (jnp.minimum(i, ctx_tiles - 1), 0)

    def lat_map(i):
        return (jnp.maximum(i - ctx_tiles, 0), 0)

    def mod_map(i):
        return (jnp.where(i < ctx_tiles, 0, 1 + (i - ctx_tiles) // lat_tiles_per_batch), 0, 0)

    def stream_specs(index_map):
        return [pl.BlockSpec((tm, d), index_map),
                pl.BlockSpec((tm, WIDTH_A), index_map),
                pl.BlockSpec((tm, WIDTH_B), index_map)]

    return pl.pallas_call(
        functools.partial(_merge_kernel, ctx_tiles=ctx_tiles),
        out_shape=[jax.ShapeDtypeStruct((t_c + t_l, d), F32)] * 2
        + [jax.ShapeDtypeStruct((t_c + t_l, d // 2), jnp.int32)],
        grid=((t_c + t_l) // tm,),
        in_specs=stream_specs(ctx_map) + stream_specs(lat_map) + [
            pl.BlockSpec((1, WIDTH_A), lambda i: (0, 0)),
            pl.BlockSpec((1, WIDTH_B), lambda i: (0, 0)),
            pl.BlockSpec((WIDTH_A + WIDTH_B, d), lambda i: (0, 0)),
            pl.BlockSpec((1, 1, d), mod_map),
            pl.BlockSpec((1, 1, d), mod_map),
            pl.BlockSpec((1, 1, d), mod_map),
            pl.BlockSpec((1, d), lambda i: (0, 0))],
        out_specs=[pl.BlockSpec((tm, d), lambda i: (i, 0))] * 2 + [pl.BlockSpec((tm, d // 2), lambda i: (i, 0))],
        compiler_params=_cparams(("arbitrary",)),
        name="merge_out_proj",
    )(*ctx, *lat, on_a, on_b, w_out_bf, gate1, shift2, scale2, norm2)


def _first_index_of_max(x, iota):
    mx = jnp.max(x, axis=0, keepdims=True)
    idx = jnp.min(jnp.where(x == mx, iota, float(x.shape[0])), axis=0, keepdims=True)
    return mx, iota == idx


def _router_gates(h, wr_hi, wr_lo, rbias):
    h_hi = h.astype(BF16)
    h_lo = (h - h_hi.astype(F32)).astype(BF16)
    logits = _dot_nt(wr_hi, h_hi) + (_dot_nt(wr_lo, h_hi) + _dot_nt(wr_hi, h_lo))
    scores = _sigmoid(logits)
    sel = scores + rbias
    tm = sel.shape[1]
    iota_g = lax.broadcasted_iota(jnp.int32, (GROUP_SIZE, tm), 0).astype(F32)
    group_scores = []
    for g in range(N_GROUPS):
        grp = sel[g * GROUP_SIZE:(g + 1) * GROUP_SIZE]
        m1, first = _first_index_of_max(grp, iota_g)
        m2 = jnp.max(jnp.where(first, -jnp.inf, grp), axis=0, keepdims=True)
        group_scores.append(m1 + m2)
    gs = jnp.concatenate(group_scores, axis=0)
    iota_n = lax.broadcasted_iota(jnp.int32, (N_GROUPS, tm), 0).astype(F32)
    group_on = jnp.zeros((N_GROUPS, tm), F32)
    for _ in range(TOPK_GROUPS):
        _, pick = _first_index_of_max(gs, iota_n)
        group_on = jnp.where(pick, 1.0, group_on)
        gs = jnp.where(pick, -jnp.inf, gs)
    expert_on = jnp.concatenate(
        [jnp.broadcast_to(group_on[g:g + 1], (GROUP_SIZE, tm)) for g in range(N_GROUPS)], axis=0)
    cand = jnp.where(expert_on > 0.0, sel, -jnp.inf)
    iota_e = lax.broadcasted_iota(jnp.int32, (N_EXPERTS, tm), 0).astype(F32)
    w = jnp.zeros((N_EXPERTS, tm), F32)
    chosen = jnp.zeros((N_EXPERTS, tm), F32)
    for _ in range(TOP_K):
        _, pick = _first_index_of_max(cand, iota_e)
        w = jnp.where(pick, scores, w)
        chosen = jnp.where(pick, 1.0, chosen)
        cand = jnp.where(pick, -jnp.inf, cand)
    return w / jnp.sum(w, axis=0, keepdims=True) * ROUTED_SCALE, chosen


MOE_TS = 512
MOE_ROUTE_TM = 512
MOE_ROW_TM = 512


def _route_kernel(h_ref, wrh_ref, wrl_ref, rb_ref, tri_ref, gates_ref, rank_ref, count_ref):
    @pl.when(pl.program_id(0) == 0)
    def _():
        count_ref[...] = jnp.zeros_like(count_ref)

    gates, chosen = _router_gates(h_ref[...], wrh_ref[...], wrl_ref[...], rb_ref[...])
    gates_ref[...] = gates
    before = _dot(chosen.astype(BF16), tri_ref[...])
    seen = count_ref[...]
    rank_ref[...] = jnp.where(chosen > 0.0, before + seen[:, 0:1], -1.0)
    count_ref[...] = seen + jnp.sum(chosen, axis=1, keepdims=True)


def _route(h_all, wr_hi, wr_lo, rbias):
    t, d = h_all.shape
    tm = MOE_ROUTE_TM
    tri = jnp.asarray(np.triu(np.ones((tm, tm), np.float32), k=1), BF16)
    return pl.pallas_call(
        _route_kernel,
        out_shape=[jax.ShapeDtypeStruct((N_EXPERTS, t), F32),
                   jax.ShapeDtypeStruct((N_EXPERTS, t), F32),
                   jax.ShapeDtypeStruct((N_EXPERTS, LANES), F32)],
        grid=(t // tm,),
        in_specs=[pl.BlockSpec((tm, d), lambda i: (i, 0)),
                  pl.BlockSpec((N_EXPERTS, d), lambda i: (0, 0)),
                  pl.BlockSpec((N_EXPERTS, d), lambda i: (0, 0)),
                  pl.BlockSpec((N_EXPERTS, 1), lambda i: (0, 0)),
                  pl.BlockSpec((tm, tm), lambda i: (0, 0))],
        out_specs=[pl.BlockSpec((N_EXPERTS, tm), lambda i: (0, i)),
                   pl.BlockSpec((N_EXPERTS, tm), lambda i: (0, i)),
                   pl.BlockSpec((N_EXPERTS, LANES), lambda i: (0, 0))],
        compiler_params=_cparams(("arbitrary",)),
        name="moe_route",
    )(h_all, wr_hi, wr_lo, rbias, tri)


def _slots_kernel(gates_ref, rank_ref, off_ref, pos_ref, gtok_ref):
    gates = gates_ref[...]
    rank = rank_ref[...]
    tm = gates.shape[1]
    slot = off_ref[...] + rank
    left = jnp.where(rank >= 0.0, 1.0, 0.0)
    iota_e = lax.broadcasted_iota(jnp.int32, (N_EXPERTS, tm), 0).astype(F32)
    pos_rows, gate_rows = [], []
    for _ in range(TOP_K):
        _, pick = _first_index_of_max(left, iota_e)
        pos_rows.append(jnp.sum(jnp.where(pick, slot, 0.0), axis=0, keepdims=True))
        gate_rows.append(jnp.sum(jnp.where(pick, gates, 0.0), axis=0, keepdims=True))
        left = jnp.where(pick, 0.0, left)
    pos_ref[...] = jnp.concatenate(pos_rows, axis=0).astype(jnp.int32)
    pad = jnp.zeros((LANES - TOP_K, tm), F32)
    gtok_ref[...] = jnp.concatenate(gate_rows + [pad], axis=0).T


def _slots(gates_t, rank_t, off):
    t = gates_t.shape[1]
    tm = MOE_ROUTE_TM
    return pl.pallas_call(
        _slots_kernel,
        out_shape=[jax.ShapeDtypeStruct((TOP_K, t), jnp.int32), jax.ShapeDtypeStruct((t, LANES), F32)],
        grid=(t // tm,),
        in_specs=[pl.BlockSpec((N_EXPERTS, tm), lambda i: (0, i)),
                  pl.BlockSpec((N_EXPERTS, tm), lambda i: (0, i)),
                  pl.BlockSpec((N_EXPERTS, 1), lambda i: (0, 0))],
        out_specs=[pl.BlockSpec((TOP_K, tm), lambda i: (0, i)),
                   pl.BlockSpec((tm, LANES), lambda i: (i, 0))],
        compiler_params=_cparams(("arbitrary",)),
        name="moe_slots",
    )(gates_t, rank_t, off)


SC_CORES = 2
SC_SUBCORES = 16
SC_ROWS = 64


def _dispatch(hp_all, slot_of):
    t, width = hp_all.shape
    n_slots = slot_of.shape[0]
    n_pad = n_slots - TOP_K * t
    workers = SC_CORES * SC_SUBCORES
    per_worker = t // workers
    pad_per_worker = n_pad // workers
    assert per_worker * workers == t and per_worker % SC_ROWS == 0
    assert pad_per_worker * workers == n_pad and pad_per_worker % SC_ROWS == 0
    mesh = plsc.VectorSubcoreMesh(core_axis_name="core", subcore_axis_name="subcore")

    @functools.partial(
        pl.kernel, mesh=mesh,
        out_type=jax.ShapeDtypeStruct((n_slots, width), jnp.int32),
        scratch_types=[pltpu.VMEM((SC_ROWS,), jnp.int32),
                       pltpu.VMEM((SC_ROWS, width), jnp.int32),
                       pltpu.SemaphoreType.DMA],
    )
    def scatter_rows(h_hbm, slot_hbm, out_hbm, idx_v, rows_v, sem):
        worker = lax.axis_index("subcore") * SC_CORES + lax.axis_index("core")
        base = worker * per_worker

        @pl.loop(0, per_worker // SC_ROWS)
        def _(j):
            first = base + j * SC_ROWS
            pltpu.sync_copy(h_hbm.at[pl.ds(first, SC_ROWS)], rows_v)
            for k in range(TOP_K):
                pltpu.sync_copy(slot_hbm.at[pl.ds(k * t + first, SC_ROWS)], idx_v)
                pltpu.async_copy(rows_v, out_hbm.at[idx_v], sem).wait()

        pltpu.sync_copy(h_hbm.at[pl.ds(0, SC_ROWS)], rows_v)
        pad_base = TOP_K * t + worker * pad_per_worker

        @pl.loop(0, pad_per_worker // SC_ROWS)
        def _(j):
            pltpu.sync_copy(slot_hbm.at[pl.ds(pad_base + j * SC_ROWS, SC_ROWS)], idx_v)
            pltpu.async_copy(rows_v, out_hbm.at[idx_v], sem).wait()

    return scatter_rows(hp_all, slot_of)


def _experts_kernel(te_ref, xs_ref, wg_ref, wu_ref, wd_ref, ys_ref, wg_bf, wu_bf, wd_bf):
    i = pl.program_id(0)

    @pl.when((i == 0) | (te_ref[i] != te_ref[jnp.maximum(i, 1) - 1]))
    def _():
        wg_bf[...] = wg_ref[0].astype(BF16)
        wu_bf[...] = wu_ref[0].astype(BF16)
        wd_bf[...] = wd_ref[0].astype(BF16)

    left, right = _unpack_rows(xs_ref[...])
    x = jnp.concatenate([left, right], axis=1).astype(BF16)
    g = _dot(x, wg_bf[...])
    u = _dot(x, wu_bf[...])
    act = (g * _sigmoid(g)) * u
    ys_ref[...] = _pack_rows(_dot(act.astype(BF16), wd_bf[...]))


def _experts(xs, tile_expert, w_gate, w_up, w_down):
    n_slots, width = xs.shape
    d = 2 * width
    ts = MOE_TS
    return pl.pallas_call(
        _experts_kernel,
        out_shape=jax.ShapeDtypeStruct((n_slots, width), jnp.int32),
        grid_spec=pltpu.PrefetchScalarGridSpec(
            num_scalar_prefetch=1,
            grid=(n_slots // ts,),
            in_specs=[pl.BlockSpec((ts, width), lambda i, te: (i, 0)),
                      pl.BlockSpec((1, d, D_EXPERT), lambda i, te: (te[i], 0, 0)),
                      pl.BlockSpec((1, d, D_EXPERT), lambda i, te: (te[i], 0, 0)),
                      pl.BlockSpec((1, D_EXPERT, d), lambda i, te: (te[i], 0, 0))],
            out_specs=pl.BlockSpec((ts, width), lambda i, te: (i, 0)),
            scratch_shapes=[pltpu.VMEM((d, D_EXPERT), BF16), pltpu.VMEM((d, D_EXPERT), BF16),
                            pltpu.VMEM((D_EXPERT, d), BF16)]),
        compiler_params=_cparams(("arbitrary",)),
        name="moe_experts",
    )(tile_expert, xs, w_gate, w_up, w_down)


def _gather_slots(y_slots, slot_of, t):
    width = y_slots.shape[1]
    workers = SC_CORES * SC_SUBCORES
    per_worker = t // workers
    n_blocks = (per_worker // SC_ROWS) * TOP_K
    assert per_worker * workers == t and per_worker % SC_ROWS == 0 and n_blocks % 2 == 0
    mesh = plsc.VectorSubcoreMesh(core_axis_name="core", subcore_axis_name="subcore")

    @functools.partial(
        pl.kernel, mesh=mesh,
        out_type=jax.ShapeDtypeStruct((TOP_K * t, width), jnp.int32),
        scratch_types=[pltpu.VMEM((SC_ROWS,), jnp.int32), pltpu.VMEM((SC_ROWS,), jnp.int32),
                       pltpu.VMEM((SC_ROWS, width), jnp.int32), pltpu.VMEM((SC_ROWS, width), jnp.int32),
                       pltpu.SemaphoreType.DMA, pltpu.SemaphoreType.DMA],
    )
    def gather_rows(ys_hbm, slot_hbm, out_hbm, idx0, idx1, rows0, rows1, sem0, sem1):
        worker = lax.axis_index("subcore") * SC_CORES + lax.axis_index("core")
        base = worker * per_worker

        def first_row(n):
            return (n % TOP_K) * t + base + (n // TOP_K) * SC_ROWS

        def start(n, idx_v, rows_v, sem):
            pltpu.sync_copy(slot_hbm.at[pl.ds(first_row(n), SC_ROWS)], idx_v)
            pltpu.async_copy(ys_hbm.at[idx_v], rows_v, sem)

        def finish(n, idx_v, rows_v, sem):
            pltpu.make_async_copy(ys_hbm.at[idx_v], rows_v, sem).wait()
            pltpu.sync_copy(rows_v, out_hbm.at[pl.ds(first_row(n), SC_ROWS)])

        start(0, idx0, rows0, sem0)

        @pl.loop(0, n_blocks, step=2)
        def _(n):
            start(n + 1, idx1, rows1, sem1)
            finish(n, idx0, rows0, sem0)

            @pl.when(n + 2 < n_blocks)
            def _():
                start(n + 2, idx0, rows0, sem0)

            finish(n + 1, idx1, rows1, sem1)

    return gather_rows(y_slots, slot_of)


def _combine_kernel(y_ref, h_ref, g2_ref, gtok_ref, rows_ref, wgs_ref, wus_ref, wds_ref, o_ref):
    h_left, h_right = _unpack_rows(h_ref[...])
    h = jnp.concatenate([h_left, h_right], axis=1).astype(BF16)
    gs = _dot(h, wgs_ref[...])
    us = _dot(h, wus_ref[...])
    shared = _dot(((gs * _sigmoid(gs)) * us).astype(BF16), wds_ref[...])

    gtok = gtok_ref[...]
    acc_left = acc_right = None
    for k in range(TOP_K):
        left, right = _unpack_rows(rows_ref[k])
        gate = gtok[:, k:k + 1]
        acc_left = gate * left if acc_left is None else acc_left + gate * left
        acc_right = gate * right if acc_right is None else acc_right + gate * right
    routed = jnp.concatenate([acc_left, acc_right], axis=1)
    o_ref[...] = y_ref[...] + g2_ref[0] * (routed + shared)


def _combine(y_all, hp_all, gate2, gtok, rows, wgs, wus, wds, *, first_token, tokens, seq):
    d = y_all.shape[1]
    width = hp_all.shape[1]
    tm = MOE_ROW_TM
    tile0 = first_token // tm
    nb = gate2.shape[0]
    tiles_per_batch = seq // tm

    def mod_map(i):
        return ((i // tiles_per_batch) if nb > 1 else 0, 0, 0)

    return pl.pallas_call(
        _combine_kernel,
        out_shape=jax.ShapeDtypeStruct((tokens, d), F32),
        grid=(tokens // tm,),
        in_specs=[pl.BlockSpec((tm, d), lambda i: (tile0 + i, 0)),
                  pl.BlockSpec((tm, width), lambda i: (tile0 + i, 0)),
                  pl.BlockSpec((1, 1, d), mod_map),
                  pl.BlockSpec((tm, LANES), lambda i: (tile0 + i, 0)),
                  pl.BlockSpec((TOP_K, tm, width), lambda i: (0, tile0 + i, 0)),
                  pl.BlockSpec((d, D_SHARED), lambda i: (0, 0)),
                  pl.BlockSpec((d, D_SHARED), lambda i: (0, 0)),
                  pl.BlockSpec((D_SHARED, d), lambda i: (0, 0))],
        out_specs=pl.BlockSpec((tm, d), lambda i: (i, 0)),
        compiler_params=_cparams(("arbitrary",)),
        name="moe_combine",
    )(y_all, hp_all, gate2, gtok, rows, wgs, wus, wds)


def _expert_layout(counts, n_tiles):
    cnt = counts.astype(jnp.int32)
    tiles = (cnt + (MOE_TS - 1)) // MOE_TS
    last_tile = jnp.cumsum(tiles)
    off = (last_tile - tiles) * MOE_TS
    pad_lo = off + cnt
    pad_hi = (off + tiles * MOE_TS).at[N_EXPERTS - 1].set(n_tiles * MOE_TS)
    pad_cnt = pad_hi - pad_lo
    pad_last = jnp.cumsum(pad_cnt)
    shift = pad_lo - (pad_last - pad_cnt)
    j = jnp.arange(N_EXPERTS * MOE_TS, dtype=jnp.int32)
    past = (pad_last[None, :-1] <= j[:, None]).astype(jnp.int32)
    pad_slots = j + shift[0] + jnp.sum(past * (shift[1:] - shift[:-1])[None, :], axis=1)
    tile_ids = jnp.arange(n_tiles, dtype=jnp.int32)
    tile_expert = jnp.minimum(
        jnp.sum((last_tile[None, :] <= tile_ids[:, None]).astype(jnp.int32), axis=1), N_EXPERTS - 1)
    return off, pad_slots, tile_expert


def _rope_tables(n_tokens):
    t = jnp.arange(n_tokens)
    row = (t // GRID_W).astype(F32)
    col = (t % GRID_W).astype(F32)
    nf = HEAD_DIM // 4
    freqs = ROPE_THETA ** (-jnp.arange(nf, dtype=F32) / nf)
    ang_r = row[:, None] * freqs
    ang_c = col[:, None] * freqs
    cos = jnp.concatenate([jnp.cos(ang_r)] * 2 + [jnp.cos(ang_c)] * 2, axis=1)
    sin = jnp.concatenate([-jnp.sin(ang_r), jnp.sin(ang_r), -jnp.sin(ang_c), jnp.sin(ang_c)], axis=1)
    reps = LANES // HEAD_DIM
    return jnp.tile(cos, (1, reps)), jnp.tile(sin, (1, reps))


def _head_gains(qn_a, kn_a, qn_b, kn_b):
    ones = jnp.ones((HEAD_DIM,), F32)
    parts = ([qn_a] * N_HEADS_A + [kn_a] * N_KV_A + [ones] * N_KV_A
             + [qn_b] * N_HEADS_B + [kn_b] * N_HEADS_B + [ones] * N_HEADS_B)
    return jnp.concatenate(parts).reshape(1, IN_COLS).astype(F32)


def _same_head_indicator():
    i = np.arange(LANES)
    return jnp.asarray((i[:, None] // HEAD_DIM) == (i[None, :] // HEAD_DIM), BF16)


def _token_major(cache):
    b, h, s, hd = cache.shape
    return cache.transpose(0, 2, 1, 3).reshape(b, s, h * hd).astype(BF16)


def kernel(x_prompt, x_sample, cache_k_a, cache_v_a, cache_k_b, cache_v_b, c, c_ctx, w_mod, b_mod, norm1, norm2, w_in, qn_a, kn_a, qn_b, kn_b, rpb, on_a, on_b, w_out, w_router, router_bias, w_gate_e, w_up_e, w_down_e, w_gate_s, w_up_s, w_down_s):
    depth = w_mod.shape[0]
    assert depth == 1
    l = 0
    bp, sp, d = x_prompt.shape
    bs, ss, _ = x_sample.shape

    cvec = jnp.concatenate([c_ctx[None, :], c], axis=0)
    rows = -(-cvec.shape[0] // 8) * 8
    cvec = jnp.pad(cvec, ((0, rows - cvec.shape[0]), (0, 0)))
    mod = _adaln(cvec, w_mod[l], b_mod[l])
    mod_p = [m.reshape(1, 1, d) for m in jnp.split(mod[0:1], 6, axis=-1)]
    mod_s = [m.reshape(bs, 1, d) for m in jnp.split(mod[1:1 + bs], 6, axis=-1)]
    mod_all = [m.reshape(1 + bs, 1, d) for m in jnp.split(mod[0:1 + bs], 6, axis=-1)]

    w_in_bf = w_in[l].astype(BF16)
    w_out_bf = w_out[l].astype(BF16)
    gain = _head_gains(qn_a[l], kn_a[l], qn_b[l], kn_b[l])
    seg = _same_head_indicator()
    n1 = norm1[l].reshape(1, d)
    n2 = norm2[l].reshape(1, d)
    ona = on_a[l].reshape(1, WIDTH_A)
    onb = on_b[l].reshape(1, WIDTH_B)
    wr_t = w_router[l].T
    wr_hi = wr_t.astype(BF16)
    wr_lo = (wr_t - wr_hi.astype(F32)).astype(BF16)
    rbias = router_bias[l].reshape(N_EXPERTS, 1).astype(F32)
    wgs = w_gate_s[l].astype(BF16)
    wus = w_up_s[l].astype(BF16)
    wds = w_down_s[l].astype(BF16)
    t_p = bp * sp
    t_s = bs * ss
    t_all = t_p + t_s

    xp = x_prompt.reshape(t_p, d)
    proj_p, st_ka, st_va, st_kb, st_vb = _project(
        xp, mod_p[0], mod_p[1], n1, w_in_bf, gain, seg, None, tm=sp, seq=sp, states=True)
    oa_p, ob_p = _context_attention(proj_p, seq=sp)

    xs = x_sample.reshape(t_s, d)
    proj_s, = _project(xs, mod_s[0], mod_s[1], n1, w_in_bf, gain, seg, _rope_tables(ss),
                       tm=512, seq=ss, states=False)
    oa_s = _latent_gqa(proj_s, _token_major(cache_k_a[:, l]), _token_major(cache_v_a[:, l]), seq=ss, tq=128)
    bias_t = _neighbourhood_bias(rpb[l], ss // GRID_W)
    ob_s = _latent_neighbourhood(proj_s, _token_major(cache_k_b[:, l]), _token_major(cache_v_b[:, l]),
                                 bias_t, seq=ss)

    y1_all, h_all, hp_all = _merge((xp, oa_p, ob_p), (xs, oa_s, ob_s), ona, onb, w_out_bf,
                                   mod_all[2], mod_all[3], mod_all[4], n2, tm=512, lat_seq=ss)
    gates_t, rank_t, counts = _route(h_all, wr_hi, wr_lo, rbias)
    n_tiles = t_all * TOP_K // MOE_TS + N_EXPERTS
    off, pad_slots, tile_expert = _expert_layout(counts[:, 0], n_tiles)
    pos, gtok = _slots(gates_t, rank_t, off.astype(F32).reshape(N_EXPERTS, 1))
    slot_of = pos.reshape(TOP_K * t_all)
    x_slots = _dispatch(hp_all, jnp.concatenate([slot_of, pad_slots]))
    y_slots = _experts(x_slots, tile_expert, w_gate_e[l], w_up_e[l], w_down_e[l])
    rows = _gather_slots(y_slots, slot_of, t_all).reshape(TOP_K, t_all, d // 2)
    y_p = _combine(y1_all, hp_all, mod_p[5], gtok, rows, wgs, wus, wds,
                   first_token=0, tokens=t_p, seq=sp)
    y_s = _combine(y1_all, hp_all, mod_s[5], gtok, rows, wgs, wus, wds,
                   first_token=t_p, tokens=t_s, seq=ss)

    return (y_p.reshape(bp, sp, d), y_s.reshape(bs, ss, d), st_ka, st_va, st_kb, st_vb)
```

```python
import functools

import numpy as np
import jax
import jax.numpy as jnp
from jax import lax
from jax.experimental import pallas as pl
from jax.experimental.pallas import tpu as pltpu
from jax.experimental.pallas import tpu_sc as plsc

F32 = jnp.float32
BF16 = jnp.bfloat16

D_MODEL = 1024
HEAD_DIM = 64
N_HEADS_A = 8
N_KV_A = 2
GROUP_A = N_HEADS_A // N_KV_A
N_HEADS_B = 8
WIDTH_A = N_HEADS_A * HEAD_DIM
WIDTH_B = N_HEADS_B * HEAD_DIM
KV_WIDTH_A = N_KV_A * HEAD_DIM
IN_COLS = WIDTH_A + 2 * KV_WIDTH_A + 3 * WIDTH_B
GRID_W = 64
ROPE_THETA = 10000.0
NA_KH = 8
NA_KW = 16
N_EXPERTS = 64
N_GROUPS = 8
GROUP_SIZE = N_EXPERTS // N_GROUPS
TOPK_GROUPS = 4
TOP_K = 8
D_EXPERT = 256
D_SHARED = 256
ROUTED_SCALE = 2.5
EPS = 1e-6

LANES = 128
MXU_DIM = 256
MASKED = -1e30

COL_QA = 0
COL_KA = WIDTH_A
COL_VA = COL_KA + KV_WIDTH_A
COL_QB = COL_VA + KV_WIDTH_A
COL_KB = COL_QB + WIDTH_B
COL_VB = COL_KB + WIDTH_B

NA_QROWS = 8
NA_KROWS = 2 * NA_KH
NA_TQ = NA_QROWS * GRID_W
NA_TK = NA_KROWS * GRID_W
NA_KBLK = 256

VMEM_LIMIT = 56 * 1024 * 1024


def _cparams(sem):
    return pltpu.CompilerParams(dimension_semantics=sem, vmem_limit_bytes=VMEM_LIMIT)


def _dot(a, b):
    return jnp.dot(a, b, preferred_element_type=F32)


def _dot_nt(a, b):
    return lax.dot_general(a, b, (((1,), (1,)), ((), ())), preferred_element_type=F32)


def _sigmoid(x):
    return 1.0 / (1.0 + jnp.exp(-x))


def _rms(x):
    return x * lax.rsqrt(jnp.mean(x * x, axis=-1, keepdims=True) + EPS)


def _pack_rows(x):
    n = x.shape[1] // 2
    hi = lax.bitcast_convert_type(x[:, :n].astype(BF16).astype(F32), jnp.int32)
    lo = lax.bitcast_convert_type(x[:, n:].astype(BF16).astype(F32), jnp.int32)
    return hi | lax.shift_right_logical(lo, 16)


def _unpack_rows(w):
    left = lax.bitcast_convert_type(w & jnp.int32(-65536), F32)
    right = lax.bitcast_convert_type(lax.shift_left(w, 16), F32)
    return left, right


def _mod_kernel(c_ref, w_ref, b_ref, o_ref):
    c = c_ref[...]
    s = c * _sigmoid(c)
    o_ref[...] = jnp.dot(s, w_ref[...], preferred_element_type=F32,
                         precision=lax.Precision.HIGHEST) + b_ref[...]


def _adaln(cvec, w_mod, b_mod):
    rows, d = cvec.shape
    n = w_mod.shape[1]
    tn = 512
    return pl.pallas_call(
        _mod_kernel,
        out_shape=jax.ShapeDtypeStruct((rows, n), F32),
        grid=(n // tn,),
        in_specs=[pl.BlockSpec((rows, d), lambda j: (0, 0)),
                  pl.BlockSpec((d, tn), lambda j: (0, j)),
                  pl.BlockSpec((1, tn), lambda j: (0, j))],
        out_specs=pl.BlockSpec((rows, tn), lambda j: (0, j)),
        compiler_params=_cparams(("arbitrary",)),
        name="adaln_mod",
    )(cvec, w_mod, b_mod.reshape(1, n))


_PROJ_CHUNKS = (
    [(COL_QA + i * LANES, LANES, True, True) for i in range(WIDTH_A // LANES)]
    + [(COL_KA, LANES, True, True), (COL_VA, LANES, False, False)]
    + [(COL_QB + i * LANES, LANES, True, False) for i in range(WIDTH_B // LANES)]
    + [(COL_KB + i * LANES, LANES, True, False) for i in range(WIDTH_B // LANES)]
    + [(COL_VB + i * LANES, LANES, False, False) for i in range(WIDTH_B // LANES)]
)


def _proj_kernel(*refs, rope, states):
    x_ref, sh_ref, sc_ref, n1_ref, w_ref, gain_ref, seg_ref = refs[:7]
    pos = 7
    if rope:
        cos_ref, sin_ref = refs[pos:pos + 2]
        pos += 2
    out_ref = refs[pos]
    pos += 1
    if states:
        ka_ref, va_ref, kb_ref, vb_ref = refs[pos:pos + 4]

    x = x_ref[...]
    h = _rms(x) * n1_ref[...]
    h = h * (1.0 + sc_ref[0]) + sh_ref[0]
    p = _dot(h.astype(BF16), w_ref[...])
    seg = seg_ref[...]
    if rope:
        cos = cos_ref[...]
        sin = sin_ref[...]
        lane = lax.broadcasted_iota(jnp.int32, cos.shape, 1)
        first_half = (lane % (HEAD_DIM // 2)) < (HEAD_DIM // 4)

    for c0, w, normed, roped in _PROJ_CHUNKS:
        pc = p[:, c0:c0 + w]
        if normed:
            sq = pc * pc
            hi = sq.astype(BF16)
            lo = (sq - hi.astype(F32)).astype(BF16)
            ss = _dot(hi, seg) + _dot(lo, seg)
            pc = pc * lax.rsqrt(ss * (1.0 / HEAD_DIM) + EPS) * gain_ref[:, c0:c0 + w]
        if states:
            if c0 == COL_KA:
                for hh in range(N_KV_A):
                    ka_ref[0, 0, hh] = pc[:, hh * HEAD_DIM:(hh + 1) * HEAD_DIM]
            elif c0 == COL_VA:
                for hh in range(N_KV_A):
                    va_ref[0, 0, hh] = pc[:, hh * HEAD_DIM:(hh + 1) * HEAD_DIM]
            elif COL_KB <= c0 < COL_VB:
                base = (c0 - COL_KB) // HEAD_DIM
                for hh in range(LANES // HEAD_DIM):
                    kb_ref[0, 0, base + hh] = pc[:, hh * HEAD_DIM:(hh + 1) * HEAD_DIM]
            elif c0 >= COL_VB:
                base = (c0 - COL_VB) // HEAD_DIM
                for hh in range(LANES // HEAD_DIM):
                    vb_ref[0, 0, base + hh] = pc[:, hh * HEAD_DIM:(hh + 1) * HEAD_DIM]
        if rope and roped:
            partner = jnp.where(first_half,
                                pltpu.roll(pc, LANES - HEAD_DIM // 4, 1),
                                pltpu.roll(pc, HEAD_DIM // 4, 1))
            pc = pc * cos + partner * sin
        if c0 < COL_KA or COL_QB <= c0 < COL_KB:
            pc = pc * (HEAD_DIM ** -0.5)
        out_ref[:, c0:c0 + w] = pc.astype(BF16)


def _project(x2d, shift, scale, norm1, w_in_bf, gain, seg, rope_tabs, *, tm, seq, states):
    t, d = x2d.shape
    nb = shift.shape[0]
    tiles_per_batch = seq // tm
    rope = rope_tabs is not None

    def mod_map(i):
        return ((i // tiles_per_batch) if nb > 1 else 0, 0, 0)

    in_specs = [pl.BlockSpec((tm, d), lambda i: (i, 0)),
                pl.BlockSpec((1, 1, d), mod_map),
                pl.BlockSpec((1, 1, d), mod_map),
                pl.BlockSpec((1, d), lambda i: (0, 0)),
                pl.BlockSpec((d, IN_COLS), lambda i: (0, 0)),
                pl.BlockSpec((1, IN_COLS), lambda i: (0, 0)),
                pl.BlockSpec((LANES, LANES), lambda i: (0, 0))]
    args = [x2d, shift, scale, norm1, w_in_bf, gain, seg]
    if rope:
        in_specs += [pl.BlockSpec((tm, LANES), lambda i: (i % tiles_per_batch, 0))] * 2
        args += list(rope_tabs)
    out_shape = [jax.ShapeDtypeStruct((t, IN_COLS), BF16)]
    out_specs = [pl.BlockSpec((tm, IN_COLS), lambda i: (i, 0))]
    if states:
        assert tm == seq
        b = t // seq
        for nh in (N_KV_A, N_KV_A, N_HEADS_B, N_HEADS_B):
            out_shape.append(jax.ShapeDtypeStruct((b, 1, nh, seq, HEAD_DIM), F32))
            out_specs.append(pl.BlockSpec((1, 1, nh, seq, HEAD_DIM), lambda i: (i, 0, 0, 0, 0)))
    return pl.pallas_call(
        functools.partial(_proj_kernel, rope=rope, states=states),
        out_shape=out_shape,
        grid=(t // tm,),
        in_specs=in_specs,
        out_specs=out_specs,
        compiler_params=_cparams(("arbitrary",)),
        name="proj_states" if states else "proj_rope",
    )(*args)


def _lane_half(shape):
    return lax.broadcasted_iota(jnp.int32, shape, 1) // HEAD_DIM


def _keep_half(x, half):
    return jnp.where(_lane_half(x.shape) == half, x, jnp.zeros_like(x))


def _transpose_bf16(x):
    return x.astype(F32).T.astype(BF16)


def _attend(q, keys, values_t, biases):
    return _softmax_av(_scores(q, keys, biases), values_t)


def _scores(q, keys, biases):
    scores = []
    for k, b in zip(keys, biases):
        s = _dot_nt(k, q)
        if b is not None:
            s = s + b
        scores.append(s)
    return scores


def _softmax_av(scores, values_t):
    m = functools.reduce(jnp.maximum, [jnp.max(s, axis=0, keepdims=True) for s in scores])
    denom = None
    out = None
    for s, vt in zip(scores, values_t):
        p = jnp.exp(s - m)
        ps = jnp.sum(p, axis=0, keepdims=True)
        po = _dot(vt, p.astype(BF16))
        denom = ps if denom is None else denom + ps
        out = po if out is None else out + po
    return out / denom


def _swap_halves(q_bf16):
    return pltpu.roll(q_bf16.astype(F32), HEAD_DIM, 1).astype(BF16)


def _gqa_heads(q_of_pair, keys_by_group, values_t):
    outs = []
    for h in range(N_HEADS_A):
        g = h // GROUP_A
        q = q_of_pair(h // 2)
        if h % 2 != g:
            q = _swap_halves(q)
        o = _attend(q, keys_by_group[g], values_t, [None] * len(values_t))
        outs.append(o[g * HEAD_DIM:(g + 1) * HEAD_DIM])
    return jnp.concatenate(outs, axis=0)


def _ctx_attn_kernel(p_ref, oa_ref, ob_ref):
    ka = p_ref[:, COL_KA:COL_KA + LANES]
    va_t = [_transpose_bf16(p_ref[:, COL_VA:COL_VA + LANES])]
    keys_by_group = [[_keep_half(ka, g)] for g in range(N_KV_A)]
    oa = _gqa_heads(lambda i: p_ref[:, COL_QA + i * LANES:COL_QA + (i + 1) * LANES],
                    keys_by_group, va_t)
    oa_ref[...] = oa.T

    outs = []
    for i in range(N_HEADS_B // 2):
        q = p_ref[:, COL_QB + i * LANES:COL_QB + (i + 1) * LANES]
        k = p_ref[:, COL_KB + i * LANES:COL_KB + (i + 1) * LANES]
        vt = [_transpose_bf16(p_ref[:, COL_VB + i * LANES:COL_VB + (i + 1) * LANES])]
        for half in range(2):
            o = _attend(q, [_keep_half(k, half)], vt, [None])
            outs.append(o[half * HEAD_DIM:(half + 1) * HEAD_DIM])
    ob_ref[...] = jnp.concatenate(outs, axis=0).T


def _context_attention(proj, *, seq):
    t = proj.shape[0]
    return pl.pallas_call(
        _ctx_attn_kernel,
        out_shape=[jax.ShapeDtypeStruct((t, WIDTH_A), F32), jax.ShapeDtypeStruct((t, WIDTH_B), F32)],
        grid=(t // seq,),
        in_specs=[pl.BlockSpec((seq, IN_COLS), lambda i: (i, 0))],
        out_specs=[pl.BlockSpec((seq, WIDTH_A), lambda i: (i, 0)),
                   pl.BlockSpec((seq, WIDTH_B), lambda i: (i, 0))],
        compiler_params=_cparams(("arbitrary",)),
        name="context_attention",
    )(proj)


def _gqa_latent_kernel(q_ref, k_ref, v_ref, ck_ref, cv_ref, o_ref, kg_ref, ckg_ref, vt_ref, cvt_ref):
    @pl.when(pl.program_id(1) == 0)
    def _():
        k = k_ref[...]
        ck = ck_ref[0]
        for g in range(N_KV_A):
            kg_ref[g] = _keep_half(k, g)
            ckg_ref[g] = _keep_half(ck, g)
        vt_ref[...] = _transpose_bf16(v_ref[...])
        cvt_ref[...] = _transpose_bf16(cv_ref[0])

    tq = q_ref.shape[0]
    scores = []
    for g in range(N_KV_A):
        qs = []
        for j in range(GROUP_A):
            h = g * GROUP_A + j
            q = q_ref[:, (h // 2) * LANES:(h // 2 + 1) * LANES]
            qs.append(q if h % 2 == g else _swap_halves(q))
        qg = jnp.concatenate(qs, axis=0)
        scores.append((_dot_nt(kg_ref[g], qg), _dot_nt(ckg_ref[g], qg)))
    probs = []
    for s_lat, s_ctx in scores:
        m = jnp.maximum(jnp.max(s_lat, axis=0, keepdims=True), jnp.max(s_ctx, axis=0, keepdims=True))
        p_lat = jnp.exp(s_lat - m)
        p_ctx = jnp.exp(s_ctx - m)
        denom = jnp.sum(p_lat, axis=0, keepdims=True) + jnp.sum(p_ctx, axis=0, keepdims=True)
        probs.append((p_lat.astype(BF16), p_ctx.astype(BF16), denom))
    outs = []
    for g, (p_lat, p_ctx, denom) in enumerate(probs):
        o = _dot(vt_ref[...], p_lat) + _dot(cvt_ref[...], p_ctx)
        o = (o / denom)[g * HEAD_DIM:(g + 1) * HEAD_DIM]
        outs += [o[:, j * tq:(j + 1) * tq] for j in range(GROUP_A)]
    o_ref[...] = jnp.concatenate(outs, axis=0).T


def _latent_gqa(proj, ctx_k, ctx_v, *, seq, tq):
    t = proj.shape[0]
    b = t // seq
    nq = seq // tq
    past = ctx_k.shape[1]
    return pl.pallas_call(
        _gqa_latent_kernel,
        out_shape=jax.ShapeDtypeStruct((t, WIDTH_A), F32),
        grid=(b, nq),
        in_specs=[pl.BlockSpec((tq, WIDTH_A), lambda bi, qi: (bi * nq + qi, 0)),
                  pl.BlockSpec((seq, LANES), lambda bi, qi: (bi, COL_KA // LANES)),
                  pl.BlockSpec((seq, LANES), lambda bi, qi: (bi, COL_VA // LANES)),
                  pl.BlockSpec((1, past, LANES), lambda bi, qi: (bi, 0, 0)),
                  pl.BlockSpec((1, past, LANES), lambda bi, qi: (bi, 0, 0))],
        out_specs=pl.BlockSpec((tq, WIDTH_A), lambda bi, qi: (bi * nq + qi, 0)),
        scratch_shapes=[pltpu.VMEM((N_KV_A, seq, LANES), BF16),
                        pltpu.VMEM((N_KV_A, past, LANES), BF16),
                        pltpu.VMEM((LANES, seq), BF16),
                        pltpu.VMEM((LANES, past), BF16)],
        compiler_params=_cparams(("arbitrary", "arbitrary")),
        name="latent_gqa",
    )(proj, proj, proj, ctx_k, ctx_v)


def _na_kernel(q_ref, k0, k1, k2, k3, v0, v1, v2, v3, ck_ref, cv_ref, bias_ref, o_ref):
    q = q_ref[...]
    ks = [r[...] for r in (k0, k1, k2, k3)] + [ck_ref[0]]
    values_t = [_transpose_bf16(r[...]) for r in (v0, v1, v2, v3)] + [_transpose_bf16(cv_ref[0])]
    scores = []
    for half in range(2):
        keys = [_keep_half(k, half) for k in ks]
        biases = [bias_ref[0, half, j * NA_KBLK:(j + 1) * NA_KBLK, :] for j in range(4)] + [None]
        scores.append(_scores(q, keys, biases))
    outs = [_softmax_av(scores[half], values_t)[half * HEAD_DIM:(half + 1) * HEAD_DIM] for half in range(2)]
    o_ref[...] = jnp.concatenate(outs, axis=0).T


def _na_first_key_block(i, rows):
    per_qblock = NA_QROWS * GRID_W // NA_KBLK
    lead = (NA_KH // 2) * GRID_W // NA_KBLK
    return jnp.clip(per_qblock * i - lead, 0, (rows - NA_KROWS) * GRID_W // NA_KBLK)


def _latent_neighbourhood(proj, ctx_k, ctx_v, bias_t, *, seq):
    t = proj.shape[0]
    b = t // seq
    rows = seq // GRID_W
    nblk = rows // NA_QROWS
    kblk_per_batch = seq // NA_KBLK
    past = ctx_k.shape[1]
    grid = (N_HEADS_B // 2, nblk, b)

    def kv_spec(col0, j):
        return pl.BlockSpec(
            (NA_KBLK, LANES),
            lambda hp, i, bi: (bi * kblk_per_batch + _na_first_key_block(i, rows) + j, col0 // LANES + hp))

    def variant(i):
        return jnp.where(i == 0, 0, jnp.where(i == nblk - 1, 2, 1))

    in_specs = ([pl.BlockSpec((NA_TQ, LANES), lambda hp, i, bi: (bi * nblk + i, COL_QB // LANES + hp))]
                + [kv_spec(COL_KB, j) for j in range(4)]
                + [kv_spec(COL_VB, j) for j in range(4)]
                + [pl.BlockSpec((1, past, LANES), lambda hp, i, bi: (bi, 0, hp)),
                   pl.BlockSpec((1, past, LANES), lambda hp, i, bi: (bi, 0, hp)),
                   pl.BlockSpec((1, 2, NA_TK, NA_TQ), lambda hp, i, bi: (variant(i), hp, 0, 0))])
    return pl.pallas_call(
        _na_kernel,
        out_shape=jax.ShapeDtypeStruct((t, WIDTH_B), F32),
        grid=grid,
        in_specs=in_specs,
        out_specs=pl.BlockSpec((NA_TQ, LANES), lambda hp, i, bi: (bi * nblk + i, hp)),
        compiler_params=_cparams(("arbitrary", "arbitrary", "arbitrary")),
        name="latent_neighbourhood",
    )(proj, *([proj] * 8), ctx_k, ctx_v, bias_t)


def _neighbourhood_bias(rpb, rows):
    nblk = rows // NA_QROWS
    n_dr = 2 * NA_KH - 1
    n_dc = 2 * NA_KW - 1
    kc = np.arange(GRID_W)[:, None]
    qc = np.arange(GRID_W)[None, :]
    ws = np.clip(qc - NA_KW // 2, 0, GRID_W - NA_KW)
    col_ok = (kc >= ws) & (kc < ws + NA_KW)
    dc = np.clip(kc - qc + NA_KW - 1, 0, n_dc - 1)
    dc_onehot = (dc[None] == np.arange(n_dc)[:, None, None]).astype(np.float32)
    tiles = jnp.einsum('hab,bkq->hakq', rpb.astype(F32), jnp.asarray(dc_onehot),
                       precision=lax.Precision.HIGHEST)
    tiles = jnp.where(jnp.asarray(col_ok)[None, None], tiles, MASKED)
    masked_tile = jnp.full((rpb.shape[0], 1, GRID_W, GRID_W), MASKED, F32)
    tiles = jnp.concatenate([tiles, masked_tile], axis=1)
    pick = np.zeros((3, NA_KROWS, NA_QROWS, n_dr + 1), np.float32)
    for v, i in enumerate((0, 1, nblk - 1)):
        r0 = i * NA_QROWS
        ks = int(np.clip(r0 - NA_KH // 2, 0, rows - NA_KROWS))
        for kl in range(NA_KROWS):
            for ql in range(NA_QROWS):
                kr, qr = ks + kl, r0 + ql
                rs = int(np.clip(qr - NA_KH // 2, 0, rows - NA_KH))
                ok = rs <= kr < rs + NA_KH
                pick[v, kl, ql, (kr - qr + NA_KH - 1) if ok else n_dr] = 1.0
    bias = jnp.einsum('vkqa,hacd->vhkcqd', jnp.asarray(pick), tiles, precision=lax.Precision.HIGHEST)
    return bias.reshape(3, rpb.shape[0], NA_TK, NA_TQ)


def _merge_kernel(xp_ref, oap_ref, obp_ref, xs_ref, oas_ref, obs_ref, ona_ref, onb_ref, wo_ref,
                  g1_ref, sh2_ref, sc2_ref, n2_ref, y_ref, h_ref, hp_ref, *, ctx_tiles):
    def one_stream(x_ref, oa_ref, ob_ref):
        na = (_rms(oa_ref[...]) * ona_ref[...]).astype(BF16)
        nb = (_rms(ob_ref[...]) * onb_ref[...]).astype(BF16)
        mix = _dot(na, wo_ref[0:WIDTH_A, :]) + _dot(nb, wo_ref[WIDTH_A:WIDTH_A + WIDTH_B, :])
        y = x_ref[...] + g1_ref[0] * mix
        y_ref[...] = y
        h = _rms(y) * n2_ref[...]
        h = h * (1.0 + sc2_ref[0]) + sh2_ref[0]
        h_ref[...] = h
        hp_ref[...] = _pack_rows(h)

    i = pl.program_id(0)
    pl.when(i < ctx_tiles)(lambda: one_stream(xp_ref, oap_ref, obp_ref))
    pl.when(i >= ctx_tiles)(lambda: one_stream(xs_ref, oas_ref, obs_ref))


def _merge(ctx, lat, on_a, on_b, w_out_bf, gate1, shift2, scale2, norm2, *, tm, lat_seq):
    t_c, d = ctx[0].shape
    t_l = lat[0].shape[0]
    ctx_tiles = t_c // tm
    lat_tiles_per_batch = lat_seq // tm

    def ctx_map(i):
        return (jnp.minimum(i, ctx_tiles - 1), 0)

    def lat_map(i):
        return (jnp.maximum(i - ctx_tiles, 0), 0)

    def mod_map(i):
        return (jnp.where(i < ctx_tiles, 0, 1 + (i - ctx_tiles) // lat_tiles_per_batch), 0, 0)

    def stream_specs(index_map):
        return [pl.BlockSpec((tm, d), index_map),
                pl.BlockSpec((tm, WIDTH_A), index_map),
                pl.BlockSpec((tm, WIDTH_B), index_map)]

    return pl.pallas_call(
        functools.partial(_merge_kernel, ctx_tiles=ctx_tiles),
        out_shape=[jax.ShapeDtypeStruct((t_c + t_l, d), F32)] * 2
        + [jax.ShapeDtypeStruct((t_c + t_l, d // 2), jnp.int32)],
        grid=((t_c + t_l) // tm,),
        in_specs=stream_specs(ctx_map) + stream_specs(lat_map) + [
            pl.BlockSpec((1, WIDTH_A), lambda i: (0, 0)),
            pl.BlockSpec((1, WIDTH_B), lambda i: (0, 0)),
            pl.BlockSpec((WIDTH_A + WIDTH_B, d), lambda i: (0, 0)),
            pl.BlockSpec((1, 1, d), mod_map),
            pl.BlockSpec((1, 1, d), mod_map),
            pl.BlockSpec((1, 1, d), mod_map),
            pl.BlockSpec((1, d), lambda i: (0, 0))],
        out_specs=[pl.BlockSpec((tm, d), lambda i: (i, 0))] * 2 + [pl.BlockSpec((tm, d // 2), lambda i: (i, 0))],
        compiler_params=_cparams(("arbitrary",)),
        name="merge_out_proj",
    )(*ctx, *lat, on_a, on_b, w_out_bf, gate1, shift2, scale2, norm2)


def _first_index_of_max(x, iota):
    mx = jnp.max(x, axis=0, keepdims=True)
    idx = jnp.min(jnp.where(x == mx, iota, float(x.shape[0])), axis=0, keepdims=True)
    return mx, iota == idx


def _router_gates(h, wr_hi, wr_lo, rbias):
    h_hi = h.astype(BF16)
    h_lo = (h - h_hi.astype(F32)).astype(BF16)
    logits = _dot_nt(wr_hi, h_hi) + (_dot_nt(wr_lo, h_hi) + _dot_nt(wr_hi, h_lo))
    scores = _sigmoid(logits)
    sel = scores + rbias
    tm = sel.shape[1]
    iota_g = lax.broadcasted_iota(jnp.int32, (GROUP_SIZE, tm), 0).astype(F32)
    group_scores = []
    for g in range(N_GROUPS):
        grp = sel[g * GROUP_SIZE:(g + 1) * GROUP_SIZE]
        m1, first = _first_index_of_max(grp, iota_g)
        m2 = jnp.max(jnp.where(first, -jnp.inf, grp), axis=0, keepdims=True)
        group_scores.append(m1 + m2)
    gs = jnp.concatenate(group_scores, axis=0)
    iota_n = lax.broadcasted_iota(jnp.int32, (N_GROUPS, tm), 0).astype(F32)
    group_on = jnp.zeros((N_GROUPS, tm), F32)
    for _ in range(TOPK_GROUPS):
        _, pick = _first_index_of_max(gs, iota_n)
        group_on = jnp.where(pick, 1.0, group_on)
        gs = jnp.where(pick, -jnp.inf, gs)
    expert_on = jnp.concatenate(
        [jnp.broadcast_to(group_on[g:g + 1], (GROUP_SIZE, tm)) for g in range(N_GROUPS)], axis=0)
    cand = jnp.where(expert_on > 0.0, sel, -jnp.inf)
    iota_e = lax.broadcasted_iota(jnp.int32, (N_EXPERTS, tm), 0).astype(F32)
    w = jnp.zeros((N_EXPERTS, tm), F32)
    chosen = jnp.zeros((N_EXPERTS, tm), F32)
    for _ in range(TOP_K):
        _, pick = _first_index_of_max(cand, iota_e)
        w = jnp.where(pick, scores, w)
        chosen = jnp.where(pick, 1.0, chosen)
        cand = jnp.where(pick, -jnp.inf, cand)
    return w / jnp.sum(w, axis=0, keepdims=True) * ROUTED_SCALE, chosen


MOE_TS = 1024
MOE_ROUTE_TM = 512
MOE_ROW_TM = 512


def _route_kernel(h_ref, wrh_ref, wrl_ref, rb_ref, tri_ref, gates_ref, rank_ref, count_ref):
    @pl.when(pl.program_id(0) == 0)
    def _():
        count_ref[...] = jnp.zeros_like(count_ref)

    gates, chosen = _router_gates(h_ref[...], wrh_ref[...], wrl_ref[...], rb_ref[...])
    gates_ref[...] = gates
    before = _dot(chosen.astype(BF16), tri_ref[...])
    seen = count_ref[...]
    rank_ref[...] = jnp.where(chosen > 0.0, before + seen[:, 0:1], -1.0)
    count_ref[...] = seen + jnp.sum(chosen, axis=1, keepdims=True)


def _route(h_all, wr_hi, wr_lo, rbias):
    t, d = h_all.shape
    tm = MOE_ROUTE_TM
    tri = jnp.asarray(np.triu(np.ones((tm, tm), np.float32), k=1), BF16)
    return pl.pallas_call(
        _route_kernel,
        out_shape=[jax.ShapeDtypeStruct((N_EXPERTS, t), F32),
                   jax.ShapeDtypeStruct((N_EXPERTS, t), F32),
                   jax.ShapeDtypeStruct((N_EXPERTS, LANES), F32)],
        grid=(t // tm,),
        in_specs=[pl.BlockSpec((tm, d), lambda i: (i, 0)),
                  pl.BlockSpec((N_EXPERTS, d), lambda i: (0, 0)),
                  pl.BlockSpec((N_EXPERTS, d), lambda i: (0, 0)),
                  pl.BlockSpec((N_EXPERTS, 1), lambda i: (0, 0)),
                  pl.BlockSpec((tm, tm), lambda i: (0, 0))],
        out_specs=[pl.BlockSpec((N_EXPERTS, tm), lambda i: (0, i)),
                   pl.BlockSpec((N_EXPERTS, tm), lambda i: (0, i)),
                   pl.BlockSpec((N_EXPERTS, LANES), lambda i: (0, 0))],
        compiler_params=_cparams(("arbitrary",)),
        name="moe_route",
    )(h_all, wr_hi, wr_lo, rbias, tri)


def _slots_kernel(gates_ref, rank_ref, off_ref, pos_ref, gtok_ref):
    gates = gates_ref[...]
    rank = rank_ref[...]
    tm = gates.shape[1]
    slot = off_ref[...] + rank
    left = jnp.where(rank >= 0.0, 1.0, 0.0)
    iota_e = lax.broadcasted_iota(jnp.int32, (N_EXPERTS, tm), 0).astype(F32)
    pos_rows, gate_rows = [], []
    for _ in range(TOP_K):
        _, pick = _first_index_of_max(left, iota_e)
        pos_rows.append(jnp.sum(jnp.where(pick, slot, 0.0), axis=0, keepdims=True))
        gate_rows.append(jnp.sum(jnp.where(pick, gates, 0.0), axis=0, keepdims=True))
        left = jnp.where(pick, 0.0, left)
    pos_ref[...] = jnp.concatenate(pos_rows, axis=0).astype(jnp.int32)
    pad = jnp.zeros((LANES - TOP_K, tm), F32)
    gtok_ref[...] = jnp.concatenate(gate_rows + [pad], axis=0).T


def _slots(gates_t, rank_t, off):
    t = gates_t.shape[1]
    tm = MOE_ROUTE_TM
    return pl.pallas_call(
        _slots_kernel,
        out_shape=[jax.ShapeDtypeStruct((TOP_K, t), jnp.int32), jax.ShapeDtypeStruct((t, LANES), F32)],
        grid=(t // tm,),
        in_specs=[pl.BlockSpec((N_EXPERTS, tm), lambda i: (0, i)),
                  pl.BlockSpec((N_EXPERTS, tm), lambda i: (0, i)),
                  pl.BlockSpec((N_EXPERTS, 1), lambda i: (0, 0))],
        out_specs=[pl.BlockSpec((TOP_K, tm), lambda i: (0, i)),
                   pl.BlockSpec((tm, LANES), lambda i: (i, 0))],
        compiler_params=_cparams(("arbitrary",)),
        name="moe_slots",
    )(gates_t, rank_t, off)


SC_CORES = 2
SC_SUBCORES = 16
SC_ROWS = 64


def _dispatch(hp_all, slot_of):
    t, width = hp_all.shape
    n_slots = slot_of.shape[0]
    n_pad = n_slots - TOP_K * t
    workers = SC_CORES * SC_SUBCORES
    per_worker = t // workers
    pad_per_worker = n_pad // workers
    assert per_worker * workers == t and per_worker % SC_ROWS == 0
    assert pad_per_worker * workers == n_pad and pad_per_worker % SC_ROWS == 0
    mesh = plsc.VectorSubcoreMesh(core_axis_name="core", subcore_axis_name="subcore")

    @functools.partial(
        pl.kernel, mesh=mesh,
        out_type=jax.ShapeDtypeStruct((n_slots, width), jnp.int32),
        scratch_types=[pltpu.VMEM((SC_ROWS,), jnp.int32),
                       pltpu.VMEM((SC_ROWS, width), jnp.int32),
                       pltpu.SemaphoreType.DMA],
    )
    def scatter_rows(h_hbm, slot_hbm, out_hbm, idx_v, rows_v, sem):
        worker = lax.axis_index("subcore") * SC_CORES + lax.axis_index("core")
        base = worker * per_worker

        @pl.loop(0, per_worker // SC_ROWS)
        def _(j):
            first = base + j * SC_ROWS
            pltpu.sync_copy(h_hbm.at[pl.ds(first, SC_ROWS)], rows_v)
            for k in range(TOP_K):
                pltpu.sync_copy(slot_hbm.at[pl.ds(k * t + first, SC_ROWS)], idx_v)
                pltpu.async_copy(rows_v, out_hbm.at[idx_v], sem).wait()

        pltpu.sync_copy(h_hbm.at[pl.ds(0, SC_ROWS)], rows_v)
        pad_base = TOP_K * t + worker * pad_per_worker

        @pl.loop(0, pad_per_worker // SC_ROWS)
        def _(j):
            pltpu.sync_copy(slot_hbm.at[pl.ds(pad_base + j * SC_ROWS, SC_ROWS)], idx_v)
            pltpu.async_copy(rows_v, out_hbm.at[idx_v], sem).wait()

    return scatter_rows(hp_all, slot_of)


def _experts_kernel(te_ref, xs_ref, wg_ref, wu_ref, wd_ref, ys_ref, wg_bf, wu_bf, wd_bf):
    i = pl.program_id(0)

    @pl.when((i == 0) | (te_ref[i] != te_ref[jnp.maximum(i, 1) - 1]))
    def _():
        wg_bf[...] = wg_ref[0].astype(BF16)
        wu_bf[...] = wu_ref[0].astype(BF16)
        wd_bf[...] = wd_ref[0].astype(BF16)

    left, right = _unpack_rows(xs_ref[...])
    x = jnp.concatenate([left, right], axis=1).astype(BF16)
    g = _dot(x, wg_bf[...])
    u = _dot(x, wu_bf[...])
    act = (g * _sigmoid(g)) * u
    ys_ref[...] = _pack_rows(_dot(act.astype(BF16), wd_bf[...]))


def _experts(xs, tile_expert, w_gate, w_up, w_down):
    n_slots, width = xs.shape
    d = 2 * width
    ts = MOE_TS
    return pl.pallas_call(
        _experts_kernel,
        out_shape=jax.ShapeDtypeStruct((n_slots, width), jnp.int32),
        grid_spec=pltpu.PrefetchScalarGridSpec(
            num_scalar_prefetch=1,
            grid=(n_slots // ts,),
            in_specs=[pl.BlockSpec((ts, width), lambda i, te: (i, 0)),
                      pl.BlockSpec((1, d, D_EXPERT), lambda i, te: (te[i], 0, 0)),
                      pl.BlockSpec((1, d, D_EXPERT), lambda i, te: (te[i], 0, 0)),
                      pl.BlockSpec((1, D_EXPERT, d), lambda i, te: (te[i], 0, 0))],
            out_specs=pl.BlockSpec((ts, width), lambda i, te: (i, 0)),
            scratch_shapes=[pltpu.VMEM((d, D_EXPERT), BF16), pltpu.VMEM((d, D_EXPERT), BF16),
                            pltpu.VMEM((D_EXPERT, d), BF16)]),
        compiler_params=_cparams(("arbitrary",)),
        name="moe_experts",
    )(tile_expert, xs, w_gate, w_up, w_down)


def _gather_slots(y_slots, slot_of, t):
    width = y_slots.shape[1]
    workers = SC_CORES * SC_SUBCORES
    per_worker = t // workers
    n_blocks = (per_worker // SC_ROWS) * TOP_K
    assert per_worker * workers == t and per_worker % SC_ROWS == 0 and n_blocks % 2 == 0
    mesh = plsc.VectorSubcoreMesh(core_axis_name="core", subcore_axis_name="subcore")

    @functools.partial(
        pl.kernel, mesh=mesh,
        out_type=jax.ShapeDtypeStruct((TOP_K * t, width), jnp.int32),
        scratch_types=[pltpu.VMEM((SC_ROWS,), jnp.int32), pltpu.VMEM((SC_ROWS,), jnp.int32),
                       pltpu.VMEM((SC_ROWS, width), jnp.int32), pltpu.VMEM((SC_ROWS, width), jnp.int32),
                       pltpu.SemaphoreType.DMA, pltpu.SemaphoreType.DMA],
    )
    def gather_rows(ys_hbm, slot_hbm, out_hbm, idx0, idx1, rows0, rows1, sem0, sem1):
        worker = lax.axis_index("subcore") * SC_CORES + lax.axis_index("core")
        base = worker * per_worker

        def first_row(n):
            return (n % TOP_K) * t + base + (n // TOP_K) * SC_ROWS

        def start(n, idx_v, rows_v, sem):
            pltpu.sync_copy(slot_hbm.at[pl.ds(first_row(n), SC_ROWS)], idx_v)
            pltpu.async_copy(ys_hbm.at[idx_v], rows_v, sem)

        def finish(n, idx_v, rows_v, sem):
            pltpu.make_async_copy(ys_hbm.at[idx_v], rows_v, sem).wait()
            pltpu.sync_copy(rows_v, out_hbm.at[pl.ds(first_row(n), SC_ROWS)])

        start(0, idx0, rows0, sem0)

        @pl.loop(0, n_blocks, step=2)
        def _(n):
            start(n + 1, idx1, rows1, sem1)
            finish(n, idx0, rows0, sem0)

            @pl.when(n + 2 < n_blocks)
            def _():
                start(n + 2, idx0, rows0, sem0)

            finish(n + 1, idx1, rows1, sem1)

    return gather_rows(y_slots, slot_of)


def _combine_kernel(y_ref, h_ref, g2_ref, gtok_ref, rows_ref, wgs_ref, wus_ref, wds_ref, o_ref):
    h_left, h_right = _unpack_rows(h_ref[...])
    h = jnp.concatenate([h_left, h_right], axis=1).astype(BF16)
    gs = _dot(h, wgs_ref[...])
    us = _dot(h, wus_ref[...])
    shared = _dot(((gs * _sigmoid(gs)) * us).astype(BF16), wds_ref[...])

    gtok = gtok_ref[...]
    acc_left = acc_right = None
    for k in range(TOP_K):
        left, right = _unpack_rows(rows_ref[k])
        gate = gtok[:, k:k + 1]
        acc_left = gate * left if acc_left is None else acc_left + gate * left
        acc_right = gate * right if acc_right is None else acc_right + gate * right
    routed = jnp.concatenate([acc_left, acc_right], axis=1)
    o_ref[...] = y_ref[...] + g2_ref[0] * (routed + shared)


def _combine(y_all, hp_all, gate2, gtok, rows, wgs, wus, wds, *, first_token, tokens, seq):
    d = y_all.shape[1]
    width = hp_all.shape[1]
    tm = MOE_ROW_TM
    tile0 = first_token // tm
    nb = gate2.shape[0]
    tiles_per_batch = seq // tm

    def mod_map(i):
        return ((i // tiles_per_batch) if nb > 1 else 0, 0, 0)

    return pl.pallas_call(
        _combine_kernel,
        out_shape=jax.ShapeDtypeStruct((tokens, d), F32),
        grid=(tokens // tm,),
        in_specs=[pl.BlockSpec((tm, d), lambda i: (tile0 + i, 0)),
                  pl.BlockSpec((tm, width), lambda i: (tile0 + i, 0)),
                  pl.BlockSpec((1, 1, d), mod_map),
                  pl.BlockSpec((tm, LANES), lambda i: (tile0 + i, 0)),
                  pl.BlockSpec((TOP_K, tm, width), lambda i: (0, tile0 + i, 0)),
                  pl.BlockSpec((d, D_SHARED), lambda i: (0, 0)),
                  pl.BlockSpec((d, D_SHARED), lambda i: (0, 0)),
                  pl.BlockSpec((D_SHARED, d), lambda i: (0, 0))],
        out_specs=pl.BlockSpec((tm, d), lambda i: (i, 0)),
        compiler_params=_cparams(("arbitrary",)),
        name="moe_combine",
    )(y_all, hp_all, gate2, gtok, rows, wgs, wus, wds)


def _expert_layout(counts, n_tiles):
    cnt = counts.astype(jnp.int32)
    tiles = (cnt + (MOE_TS - 1)) // MOE_TS
    last_tile = jnp.cumsum(tiles)
    off = (last_tile - tiles) * MOE_TS
    pad_lo = off + cnt
    pad_hi = (off + tiles * MOE_TS).at[N_EXPERTS - 1].set(n_tiles * MOE_TS)
    pad_cnt = pad_hi - pad_lo
    pad_last = jnp.cumsum(pad_cnt)
    shift = pad_lo - (pad_last - pad_cnt)
    j = jnp.arange(N_EXPERTS * MOE_TS, dtype=jnp.int32)
    past = (pad_last[None, :-1] <= j[:, None]).astype(jnp.int32)
    pad_slots = j + shift[0] + jnp.sum(past * (shift[1:] - shift[:-1])[None, :], axis=1)
    tile_ids = jnp.arange(n_tiles, dtype=jnp.int32)
    tile_expert = jnp.minimum(
        jnp.sum((last_tile[None, :] <= tile_ids[:, None]).astype(jnp.int32), axis=1), N_EXPERTS - 1)
    return off, pad_slots, tile_expert


def _rope_tables(n_tokens):
    t = jnp.arange(n_tokens)
    row = (t // GRID_W).astype(F32)
    col = (t % GRID_W).astype(F32)
    nf = HEAD_DIM // 4
    freqs = ROPE_THETA ** (-jnp.arange(nf, dtype=F32) / nf)
    ang_r = row[:, None] * freqs
    ang_c = col[:, None] * freqs
    cos = jnp.concatenate([jnp.cos(ang_r)] * 2 + [jnp.cos(ang_c)] * 2, axis=1)
    sin = jnp.concatenate([-jnp.sin(ang_r), jnp.sin(ang_r), -jnp.sin(ang_c), jnp.sin(ang_c)], axis=1)
    reps = LANES // HEAD_DIM
    return jnp.tile(cos, (1, reps)), jnp.tile(sin, (1, reps))


def _head_gains(qn_a, kn_a, qn_b, kn_b):
    ones = jnp.ones((HEAD_DIM,), F32)
    parts = ([qn_a] * N_HEADS_A + [kn_a] * N_KV_A + [ones] * N_KV_A
             + [qn_b] * N_HEADS_B + [kn_b] * N_HEADS_B + [ones] * N_HEADS_B)
    return jnp.concatenate(parts).reshape(1, IN_COLS).astype(F32)


def _same_head_indicator():
    i = np.arange(LANES)
    return jnp.asarray((i[:, None] // HEAD_DIM) == (i[None, :] // HEAD_DIM), BF16)


def _token_major(cache):
    b, h, s, hd = cache.shape
    return cache.transpose(0, 2, 1, 3).reshape(b, s, h * hd).astype(BF16)


def kernel(x_prompt, x_sample, cache_k_a, cache_v_a, cache_k_b, cache_v_b, c, c_ctx, w_mod, b_mod, norm1, norm2, w_in, qn_a, kn_a, qn_b, kn_b, rpb, on_a, on_b, w_out, w_router, router_bias, w_gate_e, w_up_e, w_down_e, w_gate_s, w_up_s, w_down_s):
    depth = w_mod.shape[0]
    assert depth == 1
    l = 0
    bp, sp, d = x_prompt.shape
    bs, ss, _ = x_sample.shape

    cvec = jnp.concatenate([c_ctx[None, :], c], axis=0)
    rows = -(-cvec.shape[0] // 8) * 8
    cvec = jnp.pad(cvec, ((0, rows - cvec.shape[0]), (0, 0)))
    mod = _adaln(cvec, w_mod[l], b_mod[l])
    mod_p = [m.reshape(1, 1, d) for m in jnp.split(mod[0:1], 6, axis=-1)]
    mod_s = [m.reshape(bs, 1, d) for m in jnp.split(mod[1:1 + bs], 6, axis=-1)]
    mod_all = [m.reshape(1 + bs, 1, d) for m in jnp.split(mod[0:1 + bs], 6, axis=-1)]

    w_in_bf = w_in[l].astype(BF16)
    w_out_bf = w_out[l].astype(BF16)
    gain = _head_gains(qn_a[l], kn_a[l], qn_b[l], kn_b[l])
    seg = _same_head_indicator()
    n1 = norm1[l].reshape(1, d)
    n2 = norm2[l].reshape(1, d)
    ona = on_a[l].reshape(1, WIDTH_A)
    onb = on_b[l].reshape(1, WIDTH_B)
    wr_t = w_router[l].T
    wr_hi = wr_t.astype(BF16)
    wr_lo = (wr_t - wr_hi.astype(F32)).astype(BF16)
    rbias = router_bias[l].reshape(N_EXPERTS, 1).astype(F32)
    wgs = w_gate_s[l].astype(BF16)
    wus = w_up_s[l].astype(BF16)
    wds = w_down_s[l].astype(BF16)
    t_p = bp * sp
    t_s = bs * ss
    t_all = t_p + t_s

    xp = x_prompt.reshape(t_p, d)
    proj_p, st_ka, st_va, st_kb, st_vb = _project(
        xp, mod_p[0], mod_p[1], n1, w_in_bf, gain, seg, None, tm=sp, seq=sp, states=True)
    oa_p, ob_p = _context_attention(proj_p, seq=sp)

    xs = x_sample.reshape(t_s, d)
    proj_s, = _project(xs, mod_s[0], mod_s[1], n1, w_in_bf, gain, seg, _rope_tables(ss),
                       tm=512, seq=ss, states=False)
    oa_s = _latent_gqa(proj_s, _token_major(cache_k_a[:, l]), _token_major(cache_v_a[:, l]), seq=ss, tq=128)
    bias_t = _neighbourhood_bias(rpb[l], ss // GRID_W)
    ob_s = _latent_neighbourhood(proj_s, _token_major(cache_k_b[:, l]), _token_major(cache_v_b[:, l]),
                                 bias_t, seq=ss)

    y1_all, h_all, hp_all = _merge((xp, oa_p, ob_p), (xs, oa_s, ob_s), ona, onb, w_out_bf,
                                   mod_all[2], mod_all[3], mod_all[4], n2, tm=512, lat_seq=ss)
    gates_t, rank_t, counts = _route(h_all, wr_hi, wr_lo, rbias)
    n_tiles = t_all * TOP_K // MOE_TS + N_EXPERTS
    off, pad_slots, tile_expert = _expert_layout(counts[:, 0], n_tiles)
    pos, gtok = _slots(gates_t, rank_t, off.astype(F32).reshape(N_EXPERTS, 1))
    slot_of = pos.reshape(TOP_K * t_all)
    x_slots = _dispatch(hp_all, jnp.concatenate([slot_of, pad_slots]))
    y_slots = _experts(x_slots, tile_expert, w_gate_e[l], w_up_e[l], w_down_e[l])
    rows = _gather_slots(y_slots, slot_of, t_all).reshape(TOP_K, t_all, d // 2)
    y_p = _combine(y1_all, hp_all, mod_p[5], gtok, rows, wgs, wus, wds,
                   first_token=0, tokens=t_p, seq=sp)
    y_s = _combine(y1_all, hp_all, mod_s[5], gtok, rows, wgs, wus, wds,
                   first_token=t_p, tokens=t_s, seq=ss)

    return (y_p.reshape(bp, sp, d), y_s.reshape(bs, ss, d), st_ka, st_va, st_kb, st_vb)
```

```python
import functools

import numpy as np
import jax
import jax.numpy as jnp
from jax import lax
from jax.experimental import pallas as pl
from jax.experimental.pallas import tpu as pltpu
from jax.experimental.pallas import tpu_sc as plsc

F32 = jnp.float32
BF16 = jnp.bfloat16

D_MODEL = 1024
HEAD_DIM = 64
N_HEADS_A = 8
N_KV_A = 2
GROUP_A = N_HEADS_A // N_KV_A
N_HEADS_B = 8
WIDTH_A = N_HEADS_A * HEAD_DIM
WIDTH_B = N_HEADS_B * HEAD_DIM
KV_WIDTH_A = N_KV_A * HEAD_DIM
IN_COLS = WIDTH_A + 2 * KV_WIDTH_A + 3 * WIDTH_B
GRID_W = 64
ROPE_THETA = 10000.0
NA_KH = 8
NA_KW = 16
N_EXPERTS = 64
N_GROUPS = 8
GROUP_SIZE = N_EXPERTS // N_GROUPS
TOPK_GROUPS = 4
TOP_K = 8
D_EXPERT = 256
D_SHARED = 256
ROUTED_SCALE = 2.5
EPS = 1e-6

LANES = 128
MXU_DIM = 256
MASKED = -1e30

COL_QA = 0
COL_KA = WIDTH_A
COL_VA = COL_KA + KV_WIDTH_A
COL_QB = COL_VA + KV_WIDTH_A
COL_KB = COL_QB + WIDTH_B
COL_VB = COL_KB + WIDTH_B

NA_QROWS = 8
NA_KROWS = 2 * NA_KH
NA_TQ = NA_QROWS * GRID_W
NA_TK = NA_KROWS * GRID_W
NA_KBLK = 256

VMEM_LIMIT = 56 * 1024 * 1024


def _cparams(sem):
    return pltpu.CompilerParams(dimension_semantics=sem, vmem_limit_bytes=VMEM_LIMIT)


def _dot(a, b):
    return jnp.dot(a, b, preferred_element_type=F32)


def _dot_nt(a, b):
    return lax.dot_general(a, b, (((1,), (1,)), ((), ())), preferred_element_type=F32)


def _sigmoid(x):
    return 1.0 / (1.0 + jnp.exp(-x))


def _rms(x):
    return x * lax.rsqrt(jnp.mean(x * x, axis=-1, keepdims=True) + EPS)


def _pack_rows(x):
    n = x.shape[1] // 2
    hi = lax.bitcast_convert_type(x[:, :n].astype(BF16).astype(F32), jnp.int32)
    lo = lax.bitcast_convert_type(x[:, n:].astype(BF16).astype(F32), jnp.int32)
    return hi | lax.shift_right_logical(lo, 16)


def _unpack_rows(w):
    left = lax.bitcast_convert_type(w & jnp.int32(-65536), F32)
    right = lax.bitcast_convert_type(lax.shift_left(w, 16), F32)
    return left, right


def _mod_kernel(c_ref, w_ref, b_ref, o_ref):
    c = c_ref[...]
    s = c * _sigmoid(c)
    o_ref[...] = jnp.dot(s, w_ref[...], preferred_element_type=F32,
                         precision=lax.Precision.HIGHEST) + b_ref[...]


def _adaln(cvec, w_mod, b_mod):
    rows, d = cvec.shape
    n = w_mod.shape[1]
    tn = 512
    return pl.pallas_call(
        _mod_kernel,
        out_shape=jax.ShapeDtypeStruct((rows, n), F32),
        grid=(n // tn,),
        in_specs=[pl.BlockSpec((rows, d), lambda j: (0, 0)),
                  pl.BlockSpec((d, tn), lambda j: (0, j)),
                  pl.BlockSpec((1, tn), lambda j: (0, j))],
        out_specs=pl.BlockSpec((rows, tn), lambda j: (0, j)),
        compiler_params=_cparams(("arbitrary",)),
        name="adaln_mod",
    )(cvec, w_mod, b_mod.reshape(1, n))


_PROJ_CHUNKS = (
    [(COL_QA + i * LANES, LANES, True, True) for i in range(WIDTH_A // LANES)]
    + [(COL_KA, LANES, True, True), (COL_VA, LANES, False, False)]
    + [(COL_QB + i * LANES, LANES, True, False) for i in range(WIDTH_B // LANES)]
    + [(COL_KB + i * LANES, LANES, True, False) for i in range(WIDTH_B // LANES)]
    + [(COL_VB + i * LANES, LANES, False, False) for i in range(WIDTH_B // LANES)]
)


def _proj_kernel(*refs, rope, states):
    x_ref, sh_ref, sc_ref, n1_ref, w_ref, gain_ref, seg_ref = refs[:7]
    pos = 7
    if rope:
        cos_ref, sin_ref = refs[pos:pos + 2]
        pos += 2
    out_ref = refs[pos]
    pos += 1
    if states:
        ka_ref, va_ref, kb_ref, vb_ref = refs[pos:pos + 4]

    x = x_ref[...]
    h = _rms(x) * n1_ref[...]
    h = h * (1.0 + sc_ref[0]) + sh_ref[0]
    p = _dot(h.astype(BF16), w_ref[...])
    seg = seg_ref[...]
    if rope:
        cos = cos_ref[...]
        sin = sin_ref[...]
        lane = lax.broadcasted_iota(jnp.int32, cos.shape, 1)
        first_half = (lane % (HEAD_DIM // 2)) < (HEAD_DIM // 4)

    for c0, w, normed, roped in _PROJ_CHUNKS:
        pc = p[:, c0:c0 + w]
        if normed:
            sq = pc * pc
            hi = sq.astype(BF16)
            lo = (sq - hi.astype(F32)).astype(BF16)
            ss = _dot(hi, seg) + _dot(lo, seg)
            pc = pc * lax.rsqrt(ss * (1.0 / HEAD_DIM) + EPS) * gain_ref[:, c0:c0 + w]
        if states:
            if c0 == COL_KA:
                for hh in range(N_KV_A):
                    ka_ref[0, 0, hh] = pc[:, hh * HEAD_DIM:(hh + 1) * HEAD_DIM]
            elif c0 == COL_VA:
                for hh in range(N_KV_A):
                    va_ref[0, 0, hh] = pc[:, hh * HEAD_DIM:(hh + 1) * HEAD_DIM]
            elif COL_KB <= c0 < COL_VB:
                base = (c0 - COL_KB) // HEAD_DIM
                for hh in range(LANES // HEAD_DIM):
                    kb_ref[0, 0, base + hh] = pc[:, hh * HEAD_DIM:(hh + 1) * HEAD_DIM]
            elif c0 >= COL_VB:
                base = (c0 - COL_VB) // HEAD_DIM
                for hh in range(LANES // HEAD_DIM):
                    vb_ref[0, 0, base + hh] = pc[:, hh * HEAD_DIM:(hh + 1) * HEAD_DIM]
        if rope and roped:
            partner = jnp.where(first_half,
                                pltpu.roll(pc, LANES - HEAD_DIM // 4, 1),
                                pltpu.roll(pc, HEAD_DIM // 4, 1))
            pc = pc * cos + partner * sin
        if c0 < COL_KA or COL_QB <= c0 < COL_KB:
            pc = pc * (HEAD_DIM ** -0.5)
        out_ref[:, c0:c0 + w] = pc.astype(BF16)


def _project(x2d, shift, scale, norm1, w_in_bf, gain, seg, rope_tabs, *, tm, seq, states):
    t, d = x2d.shape
    nb = shift.shape[0]
    tiles_per_batch = seq // tm
    rope = rope_tabs is not None

    def mod_map(i):
        return ((i // tiles_per_batch) if nb > 1 else 0, 0, 0)

    in_specs = [pl.BlockSpec((tm, d), lambda i: (i, 0)),
                pl.BlockSpec((1, 1, d), mod_map),
                pl.BlockSpec((1, 1, d), mod_map),
                pl.BlockSpec((1, d), lambda i: (0, 0)),
                pl.BlockSpec((d, IN_COLS), lambda i: (0, 0)),
                pl.BlockSpec((1, IN_COLS), lambda i: (0, 0)),
                pl.BlockSpec((LANES, LANES), lambda i: (0, 0))]
    args = [x2d, shift, scale, norm1, w_in_bf, gain, seg]
    if rope:
        in_specs += [pl.BlockSpec((tm, LANES), lambda i: (i % tiles_per_batch, 0))] * 2
        args += list(rope_tabs)
    out_shape = [jax.ShapeDtypeStruct((t, IN_COLS), BF16)]
    out_specs = [pl.BlockSpec((tm, IN_COLS), lambda i: (i, 0))]
    if states:
        assert tm == seq
        b = t // seq
        for nh in (N_KV_A, N_KV_A, N_HEADS_B, N_HEADS_B):
            out_shape.append(jax.ShapeDtypeStruct((b, 1, nh, seq, HEAD_DIM), F32))
            out_specs.append(pl.BlockSpec((1, 1, nh, seq, HEAD_DIM), lambda i: (i, 0, 0, 0, 0)))
    return pl.pallas_call(
        functools.partial(_proj_kernel, rope=rope, states=states),
        out_shape=out_shape,
        grid=(t // tm,),
        in_specs=in_specs,
        out_specs=out_specs,
        compiler_params=_cparams(("arbitrary",)),
        name="proj_states" if states else "proj_rope",
    )(*args)


def _lane_half(shape):
    return lax.broadcasted_iota(jnp.int32, shape, 1) // HEAD_DIM


def _keep_half(x, half):
    return jnp.where(_lane_half(x.shape) == half, x, jnp.zeros_like(x))


def _transpose_bf16(x):
    return x.astype(F32).T.astype(BF16)


def _attend(q, keys, values_t, biases):
    return _softmax_av(_scores(q, keys, biases), values_t)


def _scores(q, keys, biases):
    scores = []
    for k, b in zip(keys, biases):
        s = _dot_nt(k, q)
        if b is not None:
            s = s + b
        scores.append(s)
    return scores


def _softmax_av(scores, values_t):
    m = functools.reduce(jnp.maximum, [jnp.max(s, axis=0, keepdims=True) for s in scores])
    denom = None
    out = None
    for s, vt in zip(scores, values_t):
        p = jnp.exp(s - m)
        ps = jnp.sum(p, axis=0, keepdims=True)
        po = _dot(vt, p.astype(BF16))
        denom = ps if denom is None else denom + ps
        out = po if out is None else out + po
    return out / denom


def _swap_halves(q_bf16):
    return pltpu.roll(q_bf16.astype(F32), HEAD_DIM, 1).astype(BF16)


def _gqa_heads(q_of_pair, keys_by_group, values_t):
    outs = []
    for h in range(N_HEADS_A):
        g = h // GROUP_A
        q = q_of_pair(h // 2)
        if h % 2 != g:
            q = _swap_halves(q)
        o = _attend(q, keys_by_group[g], values_t, [None] * len(values_t))
        outs.append(o[g * HEAD_DIM:(g + 1) * HEAD_DIM])
    return jnp.concatenate(outs, axis=0)


def _ctx_attn_kernel(p_ref, oa_ref, ob_ref):
    ka = p_ref[:, COL_KA:COL_KA + LANES]
    va_t = [_transpose_bf16(p_ref[:, COL_VA:COL_VA + LANES])]
    keys_by_group = [[_keep_half(ka, g)] for g in range(N_KV_A)]
    oa = _gqa_heads(lambda i: p_ref[:, COL_QA + i * LANES:COL_QA + (i + 1) * LANES],
                    keys_by_group, va_t)
    oa_ref[...] = oa.T

    outs = []
    for i in range(N_HEADS_B // 2):
        q = p_ref[:, COL_QB + i * LANES:COL_QB + (i + 1) * LANES]
        k = p_ref[:, COL_KB + i * LANES:COL_KB + (i + 1) * LANES]
        vt = [_transpose_bf16(p_ref[:, COL_VB + i * LANES:COL_VB + (i + 1) * LANES])]
        for half in range(2):
            o = _attend(q, [_keep_half(k, half)], vt, [None])
            outs.append(o[half * HEAD_DIM:(half + 1) * HEAD_DIM])
    ob_ref[...] = jnp.concatenate(outs, axis=0).T


def _context_attention(proj, *, seq):
    t = proj.shape[0]
    return pl.pallas_call(
        _ctx_attn_kernel,
        out_shape=[jax.ShapeDtypeStruct((t, WIDTH_A), F32), jax.ShapeDtypeStruct((t, WIDTH_B), F32)],
        grid=(t // seq,),
        in_specs=[pl.BlockSpec((seq, IN_COLS), lambda i: (i, 0))],
        out_specs=[pl.BlockSpec((seq, WIDTH_A), lambda i: (i, 0)),
                   pl.BlockSpec((seq, WIDTH_B), lambda i: (i, 0))],
        compiler_params=_cparams(("arbitrary",)),
        name="context_attention",
    )(proj)


ATTN_SAFE_SHIFT = 40.0
ONES_ROWS = 16


def _ones_lane(g):
    return (1 - g) * HEAD_DIM


def _gqa_latent_kernel(q_ref, k_ref, v_ref, ck_ref, cv_ref, o_ref, kg_ref, ckg_ref, vt_ref, cvt_ref, kmax_ref):
    lane_k = lax.broadcasted_iota(jnp.int32, (1, LANES), 1)

    @pl.when(pl.program_id(1) == 0)
    def _():
        k = k_ref[...]
        ck = ck_ref[0]
        vt = v_ref[...].astype(F32).T
        cvt = cv_ref[0].astype(F32).T
        for g in range(N_KV_A):
            kf = _keep_half(k, g).astype(F32)
            ckf = _keep_half(ck, g).astype(F32)
            ksq = jnp.maximum(jnp.max(jnp.sum(kf * kf, axis=1, keepdims=True), axis=0, keepdims=True),
                              jnp.max(jnp.sum(ckf * ckf, axis=1, keepdims=True), axis=0, keepdims=True))
            kmax_ref[g] = jnp.broadcast_to(jnp.sqrt(ksq), kmax_ref.shape[1:])
            kg_ref[g] = jnp.where(lane_k == _ones_lane(g), 1.0, kf).astype(BF16)
            ckg_ref[g] = jnp.where(lane_k == _ones_lane(g), 1.0, ckf).astype(BF16)
            rows = slice(g * HEAD_DIM, (g + 1) * HEAD_DIM)
            vt_ref[g] = jnp.concatenate([vt[rows], jnp.ones((ONES_ROWS, vt.shape[1]), F32)], axis=0).astype(BF16)
            cvt_ref[g] = jnp.concatenate([cvt[rows], jnp.ones((ONES_ROWS, cvt.shape[1]), F32)], axis=0).astype(BF16)

    tq = q_ref.shape[0]
    lane_q = lax.broadcasted_iota(jnp.int32, (GROUP_A * tq, LANES), 1)
    queries, shifts = [], []
    for g in range(N_KV_A):
        qs = []
        for j in range(GROUP_A):
            h = g * GROUP_A + j
            q = q_ref[:, (h // 2) * LANES:(h // 2 + 1) * LANES].astype(F32)
            qs.append(q if h % 2 == g else pltpu.roll(q, HEAD_DIM, 1))
        qf = jnp.where(lane_q // HEAD_DIM == g, jnp.concatenate(qs, axis=0), 0.0)
        bound = jnp.sqrt(jnp.sum(qf * qf, axis=1, keepdims=True)) * kmax_ref[g][0:1, 0:1]
        shifts.append((bound * (1.0 + 2.0 ** -6)).astype(BF16).astype(F32))
        queries.append(qf)
    safe = jnp.maximum(jnp.max(shifts[0]), jnp.max(shifts[1])) <= ATTN_SAFE_SHIFT

    def attend(probabilities):
        outs = []
        for g, (p_lat, p_ctx) in enumerate(probabilities):
            o = _dot(vt_ref[g], p_lat) + _dot(cvt_ref[g], p_ctx)
            o = o[:HEAD_DIM] / o[HEAD_DIM:HEAD_DIM + 1]
            outs += [o[:, j * tq:(j + 1) * tq] for j in range(GROUP_A)]
        o_ref[...] = jnp.concatenate(outs, axis=0).T

    def with_bound():
        probabilities = []
        for g in range(N_KV_A):
            qa = jnp.where(lane_q == _ones_lane(g), -shifts[g], queries[g]).astype(BF16)
            probabilities.append((jnp.exp(_dot_nt(kg_ref[g], qa)).astype(BF16),
                                  jnp.exp(_dot_nt(ckg_ref[g], qa)).astype(BF16)))
        attend(probabilities)

    def with_row_max():
        scores = []
        for g in range(N_KV_A):
            qa = queries[g].astype(BF16)
            scores.append((_dot_nt(kg_ref[g], qa), _dot_nt(ckg_ref[g], qa)))
        probabilities = []
        for s_lat, s_ctx in scores:
            m = jnp.maximum(jnp.max(s_lat, axis=0, keepdims=True), jnp.max(s_ctx, axis=0, keepdims=True))
            probabilities.append((jnp.exp(s_lat - m).astype(BF16), jnp.exp(s_ctx - m).astype(BF16)))
        attend(probabilities)

    pl.when(safe)(with_bound)
    pl.when(jnp.logical_not(safe))(with_row_max)


def _latent_gqa(proj, ctx_k, ctx_v, *, seq, tq):
    t = proj.shape[0]
    b = t // seq
    nq = seq // tq
    past = ctx_k.shape[1]
    return pl.pallas_call(
        _gqa_latent_kernel,
        out_shape=jax.ShapeDtypeStruct((t, WIDTH_A), F32),
        grid=(b, nq),
        in_specs=[pl.BlockSpec((tq, WIDTH_A), lambda bi, qi: (bi * nq + qi, 0)),
                  pl.BlockSpec((seq, LANES), lambda bi, qi: (bi, COL_KA // LANES)),
                  pl.BlockSpec((seq, LANES), lambda bi, qi: (bi, COL_VA // LANES)),
                  pl.BlockSpec((1, past, LANES), lambda bi, qi: (bi, 0, 0)),
                  pl.BlockSpec((1, past, LANES), lambda bi, qi: (bi, 0, 0))],
        out_specs=pl.BlockSpec((tq, WIDTH_A), lambda bi, qi: (bi * nq + qi, 0)),
        scratch_shapes=[pltpu.VMEM((N_KV_A, seq, LANES), BF16),
                        pltpu.VMEM((N_KV_A, past, LANES), BF16),
                        pltpu.VMEM((N_KV_A, HEAD_DIM + ONES_ROWS, seq), BF16),
                        pltpu.VMEM((N_KV_A, HEAD_DIM + ONES_ROWS, past), BF16),
                        pltpu.VMEM((N_KV_A, 8, LANES), F32)],
        compiler_params=_cparams(("arbitrary", "arbitrary")),
        name="latent_gqa",
    )(proj, proj, proj, ctx_k, ctx_v)


def _na_kernel(q_ref, k0, k1, k2, k3, v0, v1, v2, v3, ck_ref, cv_ref, bias_ref, o_ref):
    q = q_ref[...]
    ks = [r[...] for r in (k0, k1, k2, k3)] + [ck_ref[0]]
    values_t = [_transpose_bf16(r[...]) for r in (v0, v1, v2, v3)] + [_transpose_bf16(cv_ref[0])]
    scores = []
    for half in range(2):
        keys = [_keep_half(k, half) for k in ks]
        biases = [bias_ref[0, half, j * NA_KBLK:(j + 1) * NA_KBLK, :] for j in range(4)] + [None]
        scores.append(_scores(q, keys, biases))
    outs = [_softmax_av(scores[half], values_t)[half * HEAD_DIM:(half + 1) * HEAD_DIM] for half in range(2)]
    o_ref[...] = jnp.concatenate(outs, axis=0).T


def _na_first_key_block(i, rows):
    per_qblock = NA_QROWS * GRID_W // NA_KBLK
    lead = (NA_KH // 2) * GRID_W // NA_KBLK
    return jnp.clip(per_qblock * i - lead, 0, (rows - NA_KROWS) * GRID_W // NA_KBLK)


def _latent_neighbourhood(proj, ctx_k, ctx_v, bias_t, *, seq):
    t = proj.shape[0]
    b = t // seq
    rows = seq // GRID_W
    nblk = rows // NA_QROWS
    kblk_per_batch = seq // NA_KBLK
    past = ctx_k.shape[1]
    grid = (N_HEADS_B // 2, nblk, b)

    def kv_spec(col0, j):
        return pl.BlockSpec(
            (NA_KBLK, LANES),
            lambda hp, i, bi: (bi * kblk_per_batch + _na_first_key_block(i, rows) + j, col0 // LANES + hp))

    def variant(i):
        return jnp.where(i == 0, 0, jnp.where(i == nblk - 1, 2, 1))

    in_specs = ([pl.BlockSpec((NA_TQ, LANES), lambda hp, i, bi: (bi * nblk + i, COL_QB // LANES + hp))]
                + [kv_spec(COL_KB, j) for j in range(4)]
                + [kv_spec(COL_VB, j) for j in range(4)]
                + [pl.BlockSpec((1, past, LANES), lambda hp, i, bi: (bi, 0, hp)),
                   pl.BlockSpec((1, past, LANES), lambda hp, i, bi: (bi, 0, hp)),
                   pl.BlockSpec((1, 2, NA_TK, NA_TQ), lambda hp, i, bi: (variant(i), hp, 0, 0))])
    return pl.pallas_call(
        _na_kernel,
        out_shape=jax.ShapeDtypeStruct((t, WIDTH_B), F32),
        grid=grid,
        in_specs=in_specs,
        out_specs=pl.BlockSpec((NA_TQ, LANES), lambda hp, i, bi: (bi * nblk + i, hp)),
        compiler_params=_cparams(("arbitrary", "arbitrary", "arbitrary")),
        name="latent_neighbourhood",
    )(proj, *([proj] * 8), ctx_k, ctx_v, bias_t)


def _neighbourhood_bias(rpb, rows):
    nblk = rows // NA_QROWS
    n_dr = 2 * NA_KH - 1
    n_dc = 2 * NA_KW - 1
    kc = np.arange(GRID_W)[:, None]
    qc = np.arange(GRID_W)[None, :]
    ws = np.clip(qc - NA_KW // 2, 0, GRID_W - NA_KW)
    col_ok = (kc >= ws) & (kc < ws + NA_KW)
    dc = np.clip(kc - qc + NA_KW - 1, 0, n_dc - 1)
    dc_onehot = (dc[None] == np.arange(n_dc)[:, None, None]).astype(np.float32)
    tiles = jnp.einsum('hab,bkq->hakq', rpb.astype(F32), jnp.asarray(dc_onehot),
                       precision=lax.Precision.HIGHEST)
    tiles = jnp.where(jnp.asarray(col_ok)[None, None], tiles, MASKED)
    masked_tile = jnp.full((rpb.shape[0], 1, GRID_W, GRID_W), MASKED, F32)
    tiles = jnp.concatenate([tiles, masked_tile], axis=1)
    pick = np.zeros((3, NA_KROWS, NA_QROWS, n_dr + 1), np.float32)
    for v, i in enumerate((0, 1, nblk - 1)):
        r0 = i * NA_QROWS
        ks = int(np.clip(r0 - NA_KH // 2, 0, rows - NA_KROWS))
        for kl in range(NA_KROWS):
            for ql in range(NA_QROWS):
                kr, qr = ks + kl, r0 + ql
                rs = int(np.clip(qr - NA_KH // 2, 0, rows - NA_KH))
                ok = rs <= kr < rs + NA_KH
                pick[v, kl, ql, (kr - qr + NA_KH - 1) if ok else n_dr] = 1.0
    bias = jnp.einsum('vkqa,hacd->vhkcqd', jnp.asarray(pick), tiles, precision=lax.Precision.HIGHEST)
    return bias.reshape(3, rpb.shape[0], NA_TK, NA_TQ)


def _merge_kernel(xp_ref, oap_ref, obp_ref, xs_ref, oas_ref, obs_ref, ona_ref, onb_ref, wo_ref,
                  g1_ref, sh2_ref, sc2_ref, n2_ref, y_ref, h_ref, hp_ref, *, ctx_tiles):
    def one_stream(x_ref, oa_ref, ob_ref):
        na = (_rms(oa_ref[...]) * ona_ref[...]).astype(BF16)
        nb = (_rms(ob_ref[...]) * onb_ref[...]).astype(BF16)
        mix = _dot(na, wo_ref[0:WIDTH_A, :]) + _dot(nb, wo_ref[WIDTH_A:WIDTH_A + WIDTH_B, :])
        y = x_ref[...] + g1_ref[0] * mix
        y_ref[...] = y
        h = _rms(y) * n2_ref[...]
        h = h * (1.0 + sc2_ref[0]) + sh2_ref[0]
        h_ref[...] = h
        hp_ref[...] = _pack_rows(h)

    i = pl.program_id(0)
    pl.when(i < ctx_tiles)(lambda: one_stream(xp_ref, oap_ref, obp_ref))
    pl.when(i >= ctx_tiles)(lambda: one_stream(xs_ref, oas_ref, obs_ref))


def _merge(ctx, lat, on_a, on_b, w_out_bf, gate1, shift2, scale2, norm2, *, tm, lat_seq):
    t_c, d = ctx[0].shape
    t_l = lat[0].shape[0]
    ctx_tiles = t_c // tm
    lat_tiles_per_batch = lat_seq // tm

    def ctx_map(i):
        return (jnp.minimum(i, ctx_tiles - 1), 0)

    def lat_map(i):
        return (jnp.maximum(i - ctx_tiles, 0), 0)

    def mod_map(i):
        return (jnp.where(i < ctx_tiles, 0, 1 + (i - ctx_tiles) // lat_tiles_per_batch), 0, 0)

    def stream_specs(index_map):
        return [pl.BlockSpec((tm, d), index_map),
                pl.BlockSpec((tm, WIDTH_A), index_map),
                pl.BlockSpec((tm, WIDTH_B), index_map)]

    return pl.pallas_call(
        functools.partial(_merge_kernel, ctx_tiles=ctx_tiles),
        out_shape=[jax.ShapeDtypeStruct((t_c + t_l, d), F32)] * 2
        + [jax.ShapeDtypeStruct((t_c + t_l, d // 2), jnp.int32)],
        grid=((t_c + t_l) // tm,),
        in_specs=stream_specs(ctx_map) + stream_specs(lat_map) + [
            pl.BlockSpec((1, WIDTH_A), lambda i: (0, 0)),
            pl.BlockSpec((1, WIDTH_B), lambda i: (0, 0)),
            pl.BlockSpec((WIDTH_A + WIDTH_B, d), lambda i: (0, 0)),
            pl.BlockSpec((1, 1, d), mod_map),
            pl.BlockSpec((1, 1, d), mod_map),
            pl.BlockSpec((1, 1, d), mod_map),
            pl.BlockSpec((1, d), lambda i: (0, 0))],
        out_specs=[pl.BlockSpec((tm, d), lambda i: (i, 0))] * 2 + [pl.BlockSpec((tm, d // 2), lambda i: (i, 0))],
        compiler_params=_cparams(("arbitrary",)),
        name="merge_out_proj",
    )(*ctx, *lat, on_a, on_b, w_out_bf, gate1, shift2, scale2, norm2)


def _first_index_of_max(x, iota):
    mx = jnp.max(x, axis=0, keepdims=True)
    idx = jnp.min(jnp.where(x == mx, iota, float(x.shape[0])), axis=0, keepdims=True)
    return mx, iota == idx


def _router_gates(h, wr_hi, wr_lo, rbias):
    h_hi = h.astype(BF16)
    h_lo = (h - h_hi.astype(F32)).astype(BF16)
    logits = _dot_nt(wr_hi, h_hi) + (_dot_nt(wr_lo, h_hi) + _dot_nt(wr_hi, h_lo))
    scores = _sigmoid(logits)
    sel = scores + rbias
    tm = sel.shape[1]
    iota_g = lax.broadcasted_iota(jnp.int32, (GROUP_SIZE, tm), 0).astype(F32)
    group_scores = []
    for g in range(N_GROUPS):
        grp = sel[g * GROUP_SIZE:(g + 1) * GROUP_SIZE]
        m1, first = _first_index_of_max(grp, iota_g)
        m2 = jnp.max(jnp.where(first, -jnp.inf, grp), axis=0, keepdims=True)
        group_scores.append(m1 + m2)
    gs = jnp.concatenate(group_scores, axis=0)
    iota_n = lax.broadcasted_iota(jnp.int32, (N_GROUPS, tm), 0).astype(F32)
    group_on = jnp.zeros((N_GROUPS, tm), F32)
    for _ in range(TOPK_GROUPS):
        _, pick = _first_index_of_max(gs, iota_n)
        group_on = jnp.where(pick, 1.0, group_on)
        gs = jnp.where(pick, -jnp.inf, gs)
    expert_on = jnp.concatenate(
        [jnp.broadcast_to(group_on[g:g + 1], (GROUP_SIZE, tm)) for g in range(N_GROUPS)], axis=0)
    cand = jnp.where(expert_on > 0.0, sel, -jnp.inf)
    iota_e = lax.broadcasted_iota(jnp.int32, (N_EXPERTS, tm), 0).astype(F32)
    w = jnp.zeros((N_EXPERTS, tm), F32)
    chosen = jnp.zeros((N_EXPERTS, tm), F32)
    for _ in range(TOP_K):
        _, pick = _first_index_of_max(cand, iota_e)
        w = jnp.where(pick, scores, w)
        chosen = jnp.where(pick, 1.0, chosen)
        cand = jnp.where(pick, -jnp.inf, cand)
    return w / jnp.sum(w, axis=0, keepdims=True) * ROUTED_SCALE, chosen


MOE_TS = 1024
MOE_ROUTE_TM = 512
MOE_ROW_TM = 512


def _route_kernel(h_ref, wrh_ref, wrl_ref, rb_ref, tri_ref, gates_ref, rank_ref, count_ref):
    @pl.when(pl.program_id(0) == 0)
    def _():
        count_ref[...] = jnp.zeros_like(count_ref)

    gates, chosen = _router_gates(h_ref[...], wrh_ref[...], wrl_ref[...], rb_ref[...])
    gates_ref[...] = gates
    before = _dot(chosen.astype(BF16), tri_ref[...])
    seen = count_ref[...]
    rank_ref[...] = jnp.where(chosen > 0.0, before + seen[:, 0:1], -1.0)
    count_ref[...] = seen + jnp.sum(chosen, axis=1, keepdims=True)


def _route(h_all, wr_hi, wr_lo, rbias):
    t, d = h_all.shape
    tm = MOE_ROUTE_TM
    tri = jnp.asarray(np.triu(np.ones((tm, tm), np.float32), k=1), BF16)
    return pl.pallas_call(
        _route_kernel,
        out_shape=[jax.ShapeDtypeStruct((N_EXPERTS, t), F32),
                   jax.ShapeDtypeStruct((N_EXPERTS, t), F32),
                   jax.ShapeDtypeStruct((N_EXPERTS, LANES), F32)],
        grid=(t // tm,),
        in_specs=[pl.BlockSpec((tm, d), lambda i: (i, 0)),
                  pl.BlockSpec((N_EXPERTS, d), lambda i: (0, 0)),
                  pl.BlockSpec((N_EXPERTS, d), lambda i: (0, 0)),
                  pl.BlockSpec((N_EXPERTS, 1), lambda i: (0, 0)),
                  pl.BlockSpec((tm, tm), lambda i: (0, 0))],
        out_specs=[pl.BlockSpec((N_EXPERTS, tm), lambda i: (0, i)),
                   pl.BlockSpec((N_EXPERTS, tm), lambda i: (0, i)),
                   pl.BlockSpec((N_EXPERTS, LANES), lambda i: (0, 0))],
        compiler_params=_cparams(("arbitrary",)),
        name="moe_route",
    )(h_all, wr_hi, wr_lo, rbias, tri)


def _slots_kernel(gates_ref, rank_ref, off_ref, pos_ref, gtok_ref):
    gates = gates_ref[...]
    rank = rank_ref[...]
    tm = gates.shape[1]
    slot = off_ref[...] + rank
    left = jnp.where(rank >= 0.0, 1.0, 0.0)
    iota_e = lax.broadcasted_iota(jnp.int32, (N_EXPERTS, tm), 0).astype(F32)
    pos_rows, gate_rows = [], []
    for _ in range(TOP_K):
        _, pick = _first_index_of_max(left, iota_e)
        pos_rows.append(jnp.sum(jnp.where(pick, slot, 0.0), axis=0, keepdims=True))
        gate_rows.append(jnp.sum(jnp.where(pick, gates, 0.0), axis=0, keepdims=True))
        left = jnp.where(pick, 0.0, left)
    pos_ref[...] = jnp.concatenate(pos_rows, axis=0).astype(jnp.int32)
    pad = jnp.zeros((LANES - TOP_K, tm), F32)
    gtok_ref[...] = jnp.concatenate(gate_rows + [pad], axis=0).T


def _slots(gates_t, rank_t, off):
    t = gates_t.shape[1]
    tm = MOE_ROUTE_TM
    return pl.pallas_call(
        _slots_kernel,
        out_shape=[jax.ShapeDtypeStruct((TOP_K, t), jnp.int32), jax.ShapeDtypeStruct((t, LANES), F32)],
        grid=(t // tm,),
        in_specs=[pl.BlockSpec((N_EXPERTS, tm), lambda i: (0, i)),
                  pl.BlockSpec((N_EXPERTS, tm), lambda i: (0, i)),
                  pl.BlockSpec((N_EXPERTS, 1), lambda i: (0, 0))],
        out_specs=[pl.BlockSpec((TOP_K, tm), lambda i: (0, i)),
                   pl.BlockSpec((tm, LANES), lambda i: (i, 0))],
        compiler_params=_cparams(("arbitrary",)),
        name="moe_slots",
    )(gates_t, rank_t, off)


SC_CORES = 2
SC_SUBCORES = 16
SC_ROWS = 64


def _dispatch(hp_all, slot_of):
    t, width = hp_all.shape
    n_slots = slot_of.shape[0]
    n_pad = n_slots - TOP_K * t
    workers = SC_CORES * SC_SUBCORES
    per_worker = t // workers
    pad_per_worker = n_pad // workers
    assert per_worker * workers == t and per_worker % SC_ROWS == 0
    assert pad_per_worker * workers == n_pad and pad_per_worker % SC_ROWS == 0
    mesh = plsc.VectorSubcoreMesh(core_axis_name="core", subcore_axis_name="subcore")

    @functools.partial(
        pl.kernel, mesh=mesh,
        out_type=jax.ShapeDtypeStruct((n_slots, width), jnp.int32),
        scratch_types=[pltpu.VMEM((SC_ROWS,), jnp.int32),
                       pltpu.VMEM((SC_ROWS, width), jnp.int32),
                       pltpu.SemaphoreType.DMA],
    )
    def scatter_rows(h_hbm, slot_hbm, out_hbm, idx_v, rows_v, sem):
        worker = lax.axis_index("subcore") * SC_CORES + lax.axis_index("core")
        base = worker * per_worker

        @pl.loop(0, per_worker // SC_ROWS)
        def _(j):
            first = base + j * SC_ROWS
            pltpu.sync_copy(h_hbm.at[pl.ds(first, SC_ROWS)], rows_v)
            for k in range(TOP_K):
                pltpu.sync_copy(slot_hbm.at[pl.ds(k * t + first, SC_ROWS)], idx_v)
                pltpu.async_copy(rows_v, out_hbm.at[idx_v], sem).wait()

        pltpu.sync_copy(h_hbm.at[pl.ds(0, SC_ROWS)], rows_v)
        pad_base = TOP_K * t + worker * pad_per_worker

        @pl.loop(0, pad_per_worker // SC_ROWS)
        def _(j):
            pltpu.sync_copy(slot_hbm.at[pl.ds(pad_base + j * SC_ROWS, SC_ROWS)], idx_v)
            pltpu.async_copy(rows_v, out_hbm.at[idx_v], sem).wait()

    return scatter_rows(hp_all, slot_of)


def _experts_kernel(te_ref, xs_ref, wg_ref, wu_ref, wd_ref, ys_ref, wg_bf, wu_bf, wd_bf):
    i = pl.program_id(0)

    @pl.when((i == 0) | (te_ref[i] != te_ref[jnp.maximum(i, 1) - 1]))
    def _():
        wg_bf[...] = wg_ref[0].astype(BF16)
        wu_bf[...] = wu_ref[0].astype(BF16)
        wd_bf[...] = wd_ref[0].astype(BF16)

    left, right = _unpack_rows(xs_ref[...])
    x = jnp.concatenate([left, right], axis=1).astype(BF16)
    g = _dot(x, wg_bf[...])
    u = _dot(x, wu_bf[...])
    act = (g * _sigmoid(g)) * u
    ys_ref[...] = _pack_rows(_dot(act.astype(BF16), wd_bf[...]))


def _experts(xs, tile_expert, w_gate, w_up, w_down):
    n_slots, width = xs.shape
    d = 2 * width
    ts = MOE_TS
    return pl.pallas_call(
        _experts_kernel,
        out_shape=jax.ShapeDtypeStruct((n_slots, width), jnp.int32),
        grid_spec=pltpu.PrefetchScalarGridSpec(
            num_scalar_prefetch=1,
            grid=(n_slots // ts,),
            in_specs=[pl.BlockSpec((ts, width), lambda i, te: (i, 0)),
                      pl.BlockSpec((1, d, D_EXPERT), lambda i, te: (te[i], 0, 0)),
                      pl.BlockSpec((1, d, D_EXPERT), lambda i, te: (te[i], 0, 0)),
                      pl.BlockSpec((1, D_EXPERT, d), lambda i, te: (te[i], 0, 0))],
            out_specs=pl.BlockSpec((ts, width), lambda i, te: (i, 0)),
            scratch_shapes=[pltpu.VMEM((d, D_EXPERT), BF16), pltpu.VMEM((d, D_EXPERT), BF16),
                            pltpu.VMEM((D_EXPERT, d), BF16)]),
        compiler_params=_cparams(("arbitrary",)),
        name="moe_experts",
    )(tile_expert, xs, w_gate, w_up, w_down)


def _gather_slots(y_slots, slot_of, t):
    width = y_slots.shape[1]
    workers = SC_CORES * SC_SUBCORES
    per_worker = t // workers
    n_blocks = (per_worker // SC_ROWS) * TOP_K
    assert per_worker * workers == t and per_worker % SC_ROWS == 0 and n_blocks % 2 == 0
    mesh = plsc.VectorSubcoreMesh(core_axis_name="core", subcore_axis_name="subcore")

    @functools.partial(
        pl.kernel, mesh=mesh,
        out_type=jax.ShapeDtypeStruct((TOP_K * t, width), jnp.int32),
        scratch_types=[pltpu.VMEM((SC_ROWS,), jnp.int32), pltpu.VMEM((SC_ROWS,), jnp.int32),
                       pltpu.VMEM((SC_ROWS, width), jnp.int32), pltpu.VMEM((SC_ROWS, width), jnp.int32),
                       pltpu.SemaphoreType.DMA, pltpu.SemaphoreType.DMA],
    )
    def gather_rows(ys_hbm, slot_hbm, out_hbm, idx0, idx1, rows0, rows1, sem0, sem1):
        worker = lax.axis_index("subcore") * SC_CORES + lax.axis_index("core")
        base = worker * per_worker

        def first_row(n):
            return (n % TOP_K) * t + base + (n // TOP_K) * SC_ROWS

        def start(n, idx_v, rows_v, sem):
            pltpu.sync_copy(slot_hbm.at[pl.ds(first_row(n), SC_ROWS)], idx_v)
            pltpu.async_copy(ys_hbm.at[idx_v], rows_v, sem)

        def finish(n, idx_v, rows_v, sem):
            pltpu.make_async_copy(ys_hbm.at[idx_v], rows_v, sem).wait()
            pltpu.sync_copy(rows_v, out_hbm.at[pl.ds(first_row(n), SC_ROWS)])

        start(0, idx0, rows0, sem0)

        @pl.loop(0, n_blocks, step=2)
        def _(n):
            start(n + 1, idx1, rows1, sem1)
            finish(n, idx0, rows0, sem0)

            @pl.when(n + 2 < n_blocks)
            def _():
                start(n + 2, idx0, rows0, sem0)

            finish(n + 1, idx1, rows1, sem1)

    return gather_rows(y_slots, slot_of)


def _combine_kernel(y_ref, h_ref, g2_ref, gtok_ref, rows_ref, wgs_ref, wus_ref, wds_ref, o_ref):
    h_left, h_right = _unpack_rows(h_ref[...])
    h = jnp.concatenate([h_left, h_right], axis=1).astype(BF16)
    gs = _dot(h, wgs_ref[...])
    us = _dot(h, wus_ref[...])
    shared = _dot(((gs * _sigmoid(gs)) * us).astype(BF16), wds_ref[...])

    gtok = gtok_ref[...]
    acc_left = acc_right = None
    for k in range(TOP_K):
        left, right = _unpack_rows(rows_ref[k])
        gate = gtok[:, k:k + 1]
        acc_left = gate * left if acc_left is None else acc_left + gate * left
        acc_right = gate * right if acc_right is None else acc_right + gate * right
    routed = jnp.concatenate([acc_left, acc_right], axis=1)
    o_ref[...] = y_ref[...] + g2_ref[0] * (routed + shared)


def _combine(y_all, hp_all, gate2, gtok, rows, wgs, wus, wds, *, first_token, tokens, seq):
    d = y_all.shape[1]
    width = hp_all.shape[1]
    tm = MOE_ROW_TM
    tile0 = first_token // tm
    nb = gate2.shape[0]
    tiles_per_batch = seq // tm

    def mod_map(i):
        return ((i // tiles_per_batch) if nb > 1 else 0, 0, 0)

    return pl.pallas_call(
        _combine_kernel,
        out_shape=jax.ShapeDtypeStruct((tokens, d), F32),
        grid=(tokens // tm,),
        in_specs=[pl.BlockSpec((tm, d), lambda i: (tile0 + i, 0)),
                  pl.BlockSpec((tm, width), lambda i: (tile0 + i, 0)),
                  pl.BlockSpec((1, 1, d), mod_map),
                  pl.BlockSpec((tm, LANES), lambda i: (tile0 + i, 0)),
                  pl.BlockSpec((TOP_K, tm, width), lambda i: (0, tile0 + i, 0)),
                  pl.BlockSpec((d, D_SHARED), lambda i: (0, 0)),
                  pl.BlockSpec((d, D_SHARED), lambda i: (0, 0)),
                  pl.BlockSpec((D_SHARED, d), lambda i: (0, 0))],
        out_specs=pl.BlockSpec((tm, d), lambda i: (i, 0)),
        compiler_params=_cparams(("arbitrary",)),
        name="moe_combine",
    )(y_all, hp_all, gate2, gtok, rows, wgs, wus, wds)


def _expert_layout(counts, n_tiles):
    cnt = counts.astype(jnp.int32)
    tiles = (cnt + (MOE_TS - 1)) // MOE_TS
    last_tile = jnp.cumsum(tiles)
    off = (last_tile - tiles) * MOE_TS
    pad_lo = off + cnt
    pad_hi = (off + tiles * MOE_TS).at[N_EXPERTS - 1].set(n_tiles * MOE_TS)
    pad_cnt = pad_hi - pad_lo
    pad_last = jnp.cumsum(pad_cnt)
    shift = pad_lo - (pad_last - pad_cnt)
    j = jnp.arange(N_EXPERTS * MOE_TS, dtype=jnp.int32)
    past = (pad_last[None, :-1] <= j[:, None]).astype(jnp.int32)
    pad_slots = j + shift[0] + jnp.sum(past * (shift[1:] - shift[:-1])[None, :], axis=1)
    tile_ids = jnp.arange(n_tiles, dtype=jnp.int32)
    tile_expert = jnp.minimum(
        jnp.sum((last_tile[None, :] <= tile_ids[:, None]).astype(jnp.int32), axis=1), N_EXPERTS - 1)
    return off, pad_slots, tile_expert


def _rope_tables(n_tokens):
    t = jnp.arange(n_tokens)
    row = (t // GRID_W).astype(F32)
    col = (t % GRID_W).astype(F32)
    nf = HEAD_DIM // 4
    freqs = ROPE_THETA ** (-jnp.arange(nf, dtype=F32) / nf)
    ang_r = row[:, None] * freqs
    ang_c = col[:, None] * freqs
    cos = jnp.concatenate([jnp.cos(ang_r)] * 2 + [jnp.cos(ang_c)] * 2, axis=1)
    sin = jnp.concatenate([-jnp.sin(ang_r), jnp.sin(ang_r), -jnp.sin(ang_c), jnp.sin(ang_c)], axis=1)
    reps = LANES // HEAD_DIM
    return jnp.tile(cos, (1, reps)), jnp.tile(sin, (1, reps))


def _head_gains(qn_a, kn_a, qn_b, kn_b):
    ones = jnp.ones((HEAD_DIM,), F32)
    parts = ([qn_a] * N_HEADS_A + [kn_a] * N_KV_A + [ones] * N_KV_A
             + [qn_b] * N_HEADS_B + [kn_b] * N_HEADS_B + [ones] * N_HEADS_B)
    return jnp.concatenate(parts).reshape(1, IN_COLS).astype(F32)


def _same_head_indicator():
    i = np.arange(LANES)
    return jnp.asarray((i[:, None] // HEAD_DIM) == (i[None, :] // HEAD_DIM), BF16)


def _token_major(cache):
    b, h, s, hd = cache.shape
    return cache.transpose(0, 2, 1, 3).reshape(b, s, h * hd).astype(BF16)


def kernel(x_prompt, x_sample, cache_k_a, cache_v_a, cache_k_b, cache_v_b, c, c_ctx, w_mod, b_mod, norm1, norm2, w_in, qn_a, kn_a, qn_b, kn_b, rpb, on_a, on_b, w_out, w_router, router_bias, w_gate_e, w_up_e, w_down_e, w_gate_s, w_up_s, w_down_s):
    depth = w_mod.shape[0]
    assert depth == 1
    l = 0
    bp, sp, d = x_prompt.shape
    bs, ss, _ = x_sample.shape

    cvec = jnp.concatenate([c_ctx[None, :], c], axis=0)
    rows = -(-cvec.shape[0] // 8) * 8
    cvec = jnp.pad(cvec, ((0, rows - cvec.shape[0]), (0, 0)))
    mod = _adaln(cvec, w_mod[l], b_mod[l])
    mod_p = [m.reshape(1, 1, d) for m in jnp.split(mod[0:1], 6, axis=-1)]
    mod_s = [m.reshape(bs, 1, d) for m in jnp.split(mod[1:1 + bs], 6, axis=-1)]
    mod_all = [m.reshape(1 + bs, 1, d) for m in jnp.split(mod[0:1 + bs], 6, axis=-1)]

    w_in_bf = w_in[l].astype(BF16)
    w_out_bf = w_out[l].astype(BF16)
    gain = _head_gains(qn_a[l], kn_a[l], qn_b[l], kn_b[l])
    seg = _same_head_indicator()
    n1 = norm1[l].reshape(1, d)
    n2 = norm2[l].reshape(1, d)
    ona = on_a[l].reshape(1, WIDTH_A)
    onb = on_b[l].reshape(1, WIDTH_B)
    wr_t = w_router[l].T
    wr_hi = wr_t.astype(BF16)
    wr_lo = (wr_t - wr_hi.astype(F32)).astype(BF16)
    rbias = router_bias[l].reshape(N_EXPERTS, 1).astype(F32)
    wgs = w_gate_s[l].astype(BF16)
    wus = w_up_s[l].astype(BF16)
    wds = w_down_s[l].astype(BF16)
    t_p = bp * sp
    t_s = bs * ss
    t_all = t_p + t_s

    xp = x_prompt.reshape(t_p, d)
    proj_p, st_ka, st_va, st_kb, st_vb = _project(
        xp, mod_p[0], mod_p[1], n1, w_in_bf, gain, seg, None, tm=sp, seq=sp, states=True)
    oa_p, ob_p = _context_attention(proj_p, seq=sp)

    xs = x_sample.reshape(t_s, d)
    proj_s, = _project(xs, mod_s[0], mod_s[1], n1, w_in_bf, gain, seg, _rope_tables(ss),
                       tm=512, seq=ss, states=False)
    oa_s = _latent_gqa(proj_s, _token_major(cache_k_a[:, l]), _token_major(cache_v_a[:, l]), seq=ss, tq=128)
    bias_t = _neighbourhood_bias(rpb[l], ss // GRID_W)
    ob_s = _latent_neighbourhood(proj_s, _token_major(cache_k_b[:, l]), _token_major(cache_v_b[:, l]),
                                 bias_t, seq=ss)

    y1_all, h_all, hp_all = _merge((xp, oa_p, ob_p), (xs, oa_s, ob_s), ona, onb, w_out_bf,
                                   mod_all[2], mod_all[3], mod_all[4], n2, tm=512, lat_seq=ss)
    gates_t, rank_t, counts = _route(h_all, wr_hi, wr_lo, rbias)
    n_tiles = t_all * TOP_K // MOE_TS + N_EXPERTS
    off, pad_slots, tile_expert = _expert_layout(counts[:, 0], n_tiles)
    pos, gtok = _slots(gates_t, rank_t, off.astype(F32).reshape(N_EXPERTS, 1))
    slot_of = pos.reshape(TOP_K * t_all)
    x_slots = _dispatch(hp_all, jnp.concatenate([slot_of, pad_slots]))
    y_slots = _experts(x_slots, tile_expert, w_gate_e[l], w_up_e[l], w_down_e[l])
    rows = _gather_slots(y_slots, slot_of, t_all).reshape(TOP_K, t_all, d // 2)
    y_p = _combine(y1_all, hp_all, mod_p[5], gtok, rows, wgs, wus, wds,
                   first_token=0, tokens=t_p, seq=sp)
    y_s = _combine(y1_all, hp_all, mod_s[5], gtok, rows, wgs, wus, wds,
                   first_token=t_p, tokens=t_s, seq=ss)

    return (y_p.reshape(bp, sp, d), y_s.reshape(bs, ss, d), st_ka, st_va, st_kb, st_vb)
```

```python
import functools

import numpy as np
import jax
import jax.numpy as jnp
from jax import lax
from jax.experimental import pallas as pl
from jax.experimental.pallas import tpu as pltpu
from jax.experimental.pallas import tpu_sc as plsc

F32 = jnp.float32
BF16 = jnp.bfloat16

D_MODEL = 1024
HEAD_DIM = 64
N_HEADS_A = 8
N_KV_A = 2
GROUP_A = N_HEADS_A // N_KV_A
N_HEADS_B = 8
WIDTH_A = N_HEADS_A * HEAD_DIM
WIDTH_B = N_HEADS_B * HEAD_DIM
KV_WIDTH_A = N_KV_A * HEAD_DIM
IN_COLS = WIDTH_A + 2 * KV_WIDTH_A + 3 * WIDTH_B
GRID_W = 64
ROPE_THETA = 10000.0
NA_KH = 8
NA_KW = 16
N_EXPERTS = 64
N_GROUPS = 8
GROUP_SIZE = N_EXPERTS // N_GROUPS
TOPK_GROUPS = 4
TOP_K = 8
D_EXPERT = 256
D_SHARED = 256
ROUTED_SCALE = 2.5
EPS = 1e-6

LANES = 128
MXU_DIM = 256
MASKED = -1e30

COL_QA = 0
COL_KA = WIDTH_A
COL_VA = COL_KA + KV_WIDTH_A
COL_QB = COL_VA + KV_WIDTH_A
COL_KB = COL_QB + WIDTH_B
COL_VB = COL_KB + WIDTH_B

NA_QROWS = 8
NA_KROWS = 2 * NA_KH
NA_TQ = NA_QROWS * GRID_W
NA_TK = NA_KROWS * GRID_W
NA_KBLK = 256

VMEM_LIMIT = 56 * 1024 * 1024


def _cparams(sem):
    return pltpu.CompilerParams(dimension_semantics=sem, vmem_limit_bytes=VMEM_LIMIT)


def _dot(a, b):
    return jnp.dot(a, b, preferred_element_type=F32)


def _dot_nt(a, b):
    return lax.dot_general(a, b, (((1,), (1,)), ((), ())), preferred_element_type=F32)


def _sigmoid(x):
    return 1.0 / (1.0 + jnp.exp(-x))


def _rms(x):
    return x * lax.rsqrt(jnp.mean(x * x, axis=-1, keepdims=True) + EPS)


def _pack_rows(x):
    n = x.shape[1] // 2
    hi = lax.bitcast_convert_type(x[:, :n].astype(BF16).astype(F32), jnp.int32)
    lo = lax.bitcast_convert_type(x[:, n:].astype(BF16).astype(F32), jnp.int32)
    return hi | lax.shift_right_logical(lo, 16)


def _unpack_rows(w):
    left = lax.bitcast_convert_type(w & jnp.int32(-65536), F32)
    right = lax.bitcast_convert_type(lax.shift_left(w, 16), F32)
    return left, right


def _mod_kernel(c_ref, w_ref, b_ref, o_ref):
    c = c_ref[...]
    s = c * _sigmoid(c)
    o_ref[...] = jnp.dot(s, w_ref[...], preferred_element_type=F32,
                         precision=lax.Precision.HIGHEST) + b_ref[...]


def _adaln(cvec, w_mod, b_mod):
    rows, d = cvec.shape
    n = w_mod.shape[1]
    tn = 512
    return pl.pallas_call(
        _mod_kernel,
        out_shape=jax.ShapeDtypeStruct((rows, n), F32),
        grid=(n // tn,),
        in_specs=[pl.BlockSpec((rows, d), lambda j: (0, 0)),
                  pl.BlockSpec((d, tn), lambda j: (0, j)),
                  pl.BlockSpec((1, tn), lambda j: (0, j))],
        out_specs=pl.BlockSpec((rows, tn), lambda j: (0, j)),
        compiler_params=_cparams(("arbitrary",)),
        name="adaln_mod",
    )(cvec, w_mod, b_mod.reshape(1, n))


_PROJ_CHUNKS = (
    [(COL_QA + i * LANES, LANES, True, True) for i in range(WIDTH_A // LANES)]
    + [(COL_KA, LANES, True, True), (COL_VA, LANES, False, False)]
    + [(COL_QB + i * LANES, LANES, True, False) for i in range(WIDTH_B // LANES)]
    + [(COL_KB + i * LANES, LANES, True, False) for i in range(WIDTH_B // LANES)]
    + [(COL_VB + i * LANES, LANES, False, False) for i in range(WIDTH_B // LANES)]
)


def _proj_kernel(*refs, rope, states):
    x_ref, sh_ref, sc_ref, n1_ref, w_ref, gain_ref, seg_ref = refs[:7]
    pos = 7
    if rope:
        cos_ref, sin_ref = refs[pos:pos + 2]
        pos += 2
    out_ref = refs[pos]
    pos += 1
    if states:
        ka_ref, va_ref, kb_ref, vb_ref = refs[pos:pos + 4]

    x = x_ref[...]
    h = _rms(x) * n1_ref[...]
    h = h * (1.0 + sc_ref[0]) + sh_ref[0]
    p = _dot(h.astype(BF16), w_ref[...])
    seg = seg_ref[...]
    if rope:
        cos = cos_ref[...]
        sin = sin_ref[...]
        lane = lax.broadcasted_iota(jnp.int32, cos.shape, 1)
        first_half = (lane % (HEAD_DIM // 2)) < (HEAD_DIM // 4)

    for c0, w, normed, roped in _PROJ_CHUNKS:
        pc = p[:, c0:c0 + w]
        if normed:
            sq = pc * pc
            hi = sq.astype(BF16)
            lo = (sq - hi.astype(F32)).astype(BF16)
            ss = _dot(hi, seg) + _dot(lo, seg)
            pc = pc * lax.rsqrt(ss * (1.0 / HEAD_DIM) + EPS) * gain_ref[:, c0:c0 + w]
        if states:
            if c0 == COL_KA:
                for hh in range(N_KV_A):
                    ka_ref[0, 0, hh] = pc[:, hh * HEAD_DIM:(hh + 1) * HEAD_DIM]
            elif c0 == COL_VA:
                for hh in range(N_KV_A):
                    va_ref[0, 0, hh] = pc[:, hh * HEAD_DIM:(hh + 1) * HEAD_DIM]
            elif COL_KB <= c0 < COL_VB:
                base = (c0 - COL_KB) // HEAD_DIM
                for hh in range(LANES // HEAD_DIM):
                    kb_ref[0, 0, base + hh] = pc[:, hh * HEAD_DIM:(hh + 1) * HEAD_DIM]
            elif c0 >= COL_VB:
                base = (c0 - COL_VB) // HEAD_DIM
                for hh in range(LANES // HEAD_DIM):
                    vb_ref[0, 0, base + hh] = pc[:, hh * HEAD_DIM:(hh + 1) * HEAD_DIM]
        if rope and roped:
            partner = jnp.where(first_half,
                                pltpu.roll(pc, LANES - HEAD_DIM // 4, 1),
                                pltpu.roll(pc, HEAD_DIM // 4, 1))
            pc = pc * cos + partner * sin
        if c0 < COL_KA or COL_QB <= c0 < COL_KB:
            pc = pc * (HEAD_DIM ** -0.5)
        out_ref[:, c0:c0 + w] = pc.astype(BF16)


def _project(x2d, shift, scale, norm1, w_in_bf, gain, seg, rope_tabs, *, tm, seq, states):
    t, d = x2d.shape
    nb = shift.shape[0]
    tiles_per_batch = seq // tm
    rope = rope_tabs is not None

    def mod_map(i):
        return ((i // tiles_per_batch) if nb > 1 else 0, 0, 0)

    in_specs = [pl.BlockSpec((tm, d), lambda i: (i, 0)),
                pl.BlockSpec((1, 1, d), mod_map),
                pl.BlockSpec((1, 1, d), mod_map),
                pl.BlockSpec((1, d), lambda i: (0, 0)),
                pl.BlockSpec((d, IN_COLS), lambda i: (0, 0)),
                pl.BlockSpec((1, IN_COLS), lambda i: (0, 0)),
                pl.BlockSpec((LANES, LANES), lambda i: (0, 0))]
    args = [x2d, shift, scale, norm1, w_in_bf, gain, seg]
    if rope:
        in_specs += [pl.BlockSpec((tm, LANES), lambda i: (i % tiles_per_batch, 0))] * 2
        args += list(rope_tabs)
    out_shape = [jax.ShapeDtypeStruct((t, IN_COLS), BF16)]
    out_specs = [pl.BlockSpec((tm, IN_COLS), lambda i: (i, 0))]
    if states:
        assert tm == seq
        b = t // seq
        for nh in (N_KV_A, N_KV_A, N_HEADS_B, N_HEADS_B):
            out_shape.append(jax.ShapeDtypeStruct((b, 1, nh, seq, HEAD_DIM), F32))
            out_specs.append(pl.BlockSpec((1, 1, nh, seq, HEAD_DIM), lambda i: (i, 0, 0, 0, 0)))
    return pl.pallas_call(
        functools.partial(_proj_kernel, rope=rope, states=states),
        out_shape=out_shape,
        grid=(t // tm,),
        in_specs=in_specs,
        out_specs=out_specs,
        compiler_params=_cparams(("arbitrary",)),
        name="proj_states" if states else "proj_rope",
    )(*args)


def _lane_half(shape):
    return lax.broadcasted_iota(jnp.int32, shape, 1) // HEAD_DIM


def _keep_half(x, half):
    return jnp.where(_lane_half(x.shape) == half, x, jnp.zeros_like(x))


def _transpose_bf16(x):
    return x.astype(F32).T.astype(BF16)


def _attend(q, keys, values_t, biases):
    return _softmax_av(_scores(q, keys, biases), values_t)


def _scores(q, keys, biases):
    scores = []
    for k, b in zip(keys, biases):
        s = _dot_nt(k, q)
        if b is not None:
            s = s + b
        scores.append(s)
    return scores


def _softmax_av(scores, values_t):
    m = functools.reduce(jnp.maximum, [jnp.max(s, axis=0, keepdims=True) for s in scores])
    denom = None
    out = None
    for s, vt in zip(scores, values_t):
        p = jnp.exp(s - m)
        ps = jnp.sum(p, axis=0, keepdims=True)
        po = _dot(vt, p.astype(BF16))
        denom = ps if denom is None else denom + ps
        out = po if out is None else out + po
    return out / denom


def _swap_halves(q_bf16):
    return pltpu.roll(q_bf16.astype(F32), HEAD_DIM, 1).astype(BF16)


def _gqa_heads(q_of_pair, keys_by_group, values_t):
    outs = []
    for h in range(N_HEADS_A):
        g = h // GROUP_A
        q = q_of_pair(h // 2)
        if h % 2 != g:
            q = _swap_halves(q)
        o = _attend(q, keys_by_group[g], values_t, [None] * len(values_t))
        outs.append(o[g * HEAD_DIM:(g + 1) * HEAD_DIM])
    return jnp.concatenate(outs, axis=0)


def _ctx_attn_kernel(p_ref, oa_ref, ob_ref):
    ka = p_ref[:, COL_KA:COL_KA + LANES]
    va_t = [_transpose_bf16(p_ref[:, COL_VA:COL_VA + LANES])]
    keys_by_group = [[_keep_half(ka, g)] for g in range(N_KV_A)]
    oa = _gqa_heads(lambda i: p_ref[:, COL_QA + i * LANES:COL_QA + (i + 1) * LANES],
                    keys_by_group, va_t)
    oa_ref[...] = oa.T

    outs = []
    for i in range(N_HEADS_B // 2):
        q = p_ref[:, COL_QB + i * LANES:COL_QB + (i + 1) * LANES]
        k = p_ref[:, COL_KB + i * LANES:COL_KB + (i + 1) * LANES]
        vt = [_transpose_bf16(p_ref[:, COL_VB + i * LANES:COL_VB + (i + 1) * LANES])]
        for half in range(2):
            o = _attend(q, [_keep_half(k, half)], vt, [None])
            outs.append(o[half * HEAD_DIM:(half + 1) * HEAD_DIM])
    ob_ref[...] = jnp.concatenate(outs, axis=0).T


def _context_attention(proj, *, seq):
    t = proj.shape[0]
    return pl.pallas_call(
        _ctx_attn_kernel,
        out_shape=[jax.ShapeDtypeStruct((t, WIDTH_A), F32), jax.ShapeDtypeStruct((t, WIDTH_B), F32)],
        grid=(t // seq,),
        in_specs=[pl.BlockSpec((seq, IN_COLS), lambda i: (i, 0))],
        out_specs=[pl.BlockSpec((seq, WIDTH_A), lambda i: (i, 0)),
                   pl.BlockSpec((seq, WIDTH_B), lambda i: (i, 0))],
        compiler_params=_cparams(("arbitrary",)),
        name="context_attention",
    )(proj)


ATTN_SAFE_SHIFT = 40.0
ONES_ROWS = 16


def _round_up_bf16(x):
    return (x * (1.0 + 2.0 ** -6)).astype(BF16).astype(F32)


def _query_norm_bound(gain):
    return jnp.max(jnp.abs(gain)).reshape(1, 1).astype(F32)


def _ones_lane(g):
    return (1 - g) * HEAD_DIM


def _gqa_latent_kernel(q_ref, k_ref, v_ref, ck_ref, cv_ref, qmax_ref, o_ref,
                       kg_ref, ckg_ref, vt_ref, cvt_ref, shift_ref):
    lane_k = lax.broadcasted_iota(jnp.int32, (1, LANES), 1)

    @pl.when(pl.program_id(1) == 0)
    def _():
        k = k_ref[...]
        ck = ck_ref[0]
        vt = v_ref[...].astype(F32).T
        cvt = cv_ref[0].astype(F32).T
        for g in range(N_KV_A):
            kf = _keep_half(k, g).astype(F32)
            ckf = _keep_half(ck, g).astype(F32)
            ksq = jnp.maximum(jnp.max(jnp.sum(kf * kf, axis=1, keepdims=True), axis=0, keepdims=True),
                              jnp.max(jnp.sum(ckf * ckf, axis=1, keepdims=True), axis=0, keepdims=True))
            shift_ref[g] = jnp.broadcast_to(_round_up_bf16(qmax_ref[...] * jnp.sqrt(ksq)), shift_ref.shape[1:])
            kg_ref[g] = jnp.where(lane_k == _ones_lane(g), 1.0, kf).astype(BF16)
            ckg_ref[g] = jnp.where(lane_k == _ones_lane(g), 1.0, ckf).astype(BF16)
            rows = slice(g * HEAD_DIM, (g + 1) * HEAD_DIM)
            vt_ref[g] = jnp.concatenate([vt[rows], jnp.ones((ONES_ROWS, vt.shape[1]), F32)], axis=0).astype(BF16)
            cvt_ref[g] = jnp.concatenate([cvt[rows], jnp.ones((ONES_ROWS, cvt.shape[1]), F32)], axis=0).astype(BF16)

    tq = q_ref.shape[0]
    lane_q = lax.broadcasted_iota(jnp.int32, (GROUP_A * tq, LANES), 1)
    queries, shifts = [], []
    for g in range(N_KV_A):
        qs = []
        for j in range(GROUP_A):
            h = g * GROUP_A + j
            q = q_ref[:, (h // 2) * LANES:(h // 2 + 1) * LANES].astype(F32)
            qs.append(q if h % 2 == g else pltpu.roll(q, HEAD_DIM, 1))
        queries.append(jnp.where(lane_q // HEAD_DIM == g, jnp.concatenate(qs, axis=0), 0.0))
        shifts.append(shift_ref[g][0:1, 0:1])
    safe = jnp.max(jnp.maximum(shift_ref[0], shift_ref[1])) <= ATTN_SAFE_SHIFT

    def attend(probabilities):
        outs = []
        for g, (p_lat, p_ctx) in enumerate(probabilities):
            o = _dot(vt_ref[g], p_lat) + _dot(cvt_ref[g], p_ctx)
            o = o[:HEAD_DIM] / o[HEAD_DIM:HEAD_DIM + 1]
            outs += [o[:, j * tq:(j + 1) * tq] for j in range(GROUP_A)]
        o_ref[...] = jnp.concatenate(outs, axis=0).T

    def with_bound():
        probabilities = []
        for g in range(N_KV_A):
            qa = jnp.where(lane_q == _ones_lane(g), -shifts[g], queries[g]).astype(BF16)
            probabilities.append((jnp.exp(_dot_nt(kg_ref[g], qa)).astype(BF16),
                                  jnp.exp(_dot_nt(ckg_ref[g], qa)).astype(BF16)))
        attend(probabilities)

    def with_row_max():
        scores = []
        for g in range(N_KV_A):
            qa = queries[g].astype(BF16)
            scores.append((_dot_nt(kg_ref[g], qa), _dot_nt(ckg_ref[g], qa)))
        probabilities = []
        for s_lat, s_ctx in scores:
            m = jnp.maximum(jnp.max(s_lat, axis=0, keepdims=True), jnp.max(s_ctx, axis=0, keepdims=True))
            probabilities.append((jnp.exp(s_lat - m).astype(BF16), jnp.exp(s_ctx - m).astype(BF16)))
        attend(probabilities)

    pl.when(safe)(with_bound)
    pl.when(jnp.logical_not(safe))(with_row_max)


def _latent_gqa(proj, ctx_k, ctx_v, qmax, *, seq, tq):
    t = proj.shape[0]
    b = t // seq
    nq = seq // tq
    past = ctx_k.shape[1]
    return pl.pallas_call(
        _gqa_latent_kernel,
        out_shape=jax.ShapeDtypeStruct((t, WIDTH_A), F32),
        grid=(b, nq),
        in_specs=[pl.BlockSpec((tq, WIDTH_A), lambda bi, qi: (bi * nq + qi, 0)),
                  pl.BlockSpec((seq, LANES), lambda bi, qi: (bi, COL_KA // LANES)),
                  pl.BlockSpec((seq, LANES), lambda bi, qi: (bi, COL_VA // LANES)),
                  pl.BlockSpec((1, past, LANES), lambda bi, qi: (bi, 0, 0)),
                  pl.BlockSpec((1, past, LANES), lambda bi, qi: (bi, 0, 0)),
                  pl.BlockSpec((1, 1), lambda bi, qi: (0, 0))],
        out_specs=pl.BlockSpec((tq, WIDTH_A), lambda bi, qi: (bi * nq + qi, 0)),
        scratch_shapes=[pltpu.VMEM((N_KV_A, seq, LANES), BF16),
                        pltpu.VMEM((N_KV_A, past, LANES), BF16),
                        pltpu.VMEM((N_KV_A, HEAD_DIM + ONES_ROWS, seq), BF16),
                        pltpu.VMEM((N_KV_A, HEAD_DIM + ONES_ROWS, past), BF16),
                        pltpu.VMEM((N_KV_A, 8, LANES), F32)],
        compiler_params=_cparams(("arbitrary", "arbitrary")),
        name="latent_gqa",
    )(proj, proj, proj, ctx_k, ctx_v, qmax)


def _na_kernel(q_ref, k0, k1, k2, k3, v0, v1, v2, v3, ck_ref, cv_ref, bias_ref, bound_ref, o_ref):
    qf = q_ref[...].astype(F32)
    lane_q = lax.broadcasted_iota(jnp.int32, qf.shape, 1)
    lane_k = lax.broadcasted_iota(jnp.int32, (1, LANES), 1)
    ks = [r[...] for r in (k0, k1, k2, k3)] + [ck_ref[0]]
    vts = [r[...].astype(F32).T for r in (v0, v1, v2, v3)] + [cv_ref[0].astype(F32).T]

    heads = []
    for half in range(2):
        kfs = [_keep_half(k, half).astype(F32) for k in ks]
        ksq = functools.reduce(jnp.maximum, [jnp.max(jnp.sum(kf * kf, axis=1, keepdims=True), axis=0, keepdims=True)
                                             for kf in kfs])
        consts = bound_ref[0, half:half + 1, :]
        shift = _round_up_bf16(consts[:, 0:1] * jnp.sqrt(ksq) + consts[:, 1:2])
        keys = [jnp.where(lane_k == _ones_lane(half), 1.0, kf).astype(BF16) for kf in kfs]
        rows = slice(half * HEAD_DIM, (half + 1) * HEAD_DIM)
        values_t = [jnp.concatenate([vt[rows], jnp.ones((ONES_ROWS, vt.shape[1]), F32)], axis=0).astype(BF16)
                    for vt in vts]
        biases = [bias_ref[0, half, j * NA_KBLK:(j + 1) * NA_KBLK, :] for j in range(4)] + [None]
        q_half = jnp.where(lane_q // HEAD_DIM == half, qf, 0.0)
        heads.append((shift, keys, values_t, biases, q_half))
    safe = jnp.max(jnp.maximum(heads[0][0], heads[1][0])) <= ATTN_SAFE_SHIFT

    def attend(probabilities):
        outs = []
        for (_, _, values_t, _, _), ps in zip(heads, probabilities):
            o = functools.reduce(lambda a, b: a + b, [_dot(vt, p) for vt, p in zip(values_t, ps)])
            outs.append(o[:HEAD_DIM] / o[HEAD_DIM:HEAD_DIM + 1])
        o_ref[...] = jnp.concatenate(outs, axis=0).T

    def with_bound():
        probabilities = []
        for half, (shift, keys, _, biases, q_half) in enumerate(heads):
            qa = jnp.where(lane_q == _ones_lane(half), -shift, q_half).astype(BF16)
            probabilities.append([jnp.exp(s).astype(BF16) for s in _scores(qa, keys, biases)])
        attend(probabilities)

    def with_row_max():
        all_scores = [_scores(q_half.astype(BF16), keys, biases) for _, keys, _, biases, q_half in heads]
        probabilities = []
        for scores in all_scores:
            m = functools.reduce(jnp.maximum, [jnp.max(s, axis=0, keepdims=True) for s in scores])
            probabilities.append([jnp.exp(s - m).astype(BF16) for s in scores])
        attend(probabilities)

    pl.when(safe)(with_bound)
    pl.when(jnp.logical_not(safe))(with_row_max)


def _na_first_key_block(i, rows):
    per_qblock = NA_QROWS * GRID_W // NA_KBLK
    lead = (NA_KH // 2) * GRID_W // NA_KBLK
    return jnp.clip(per_qblock * i - lead, 0, (rows - NA_KROWS) * GRID_W // NA_KBLK)


def _latent_neighbourhood(proj, ctx_k, ctx_v, bias_t, bounds, *, seq):
    t = proj.shape[0]
    b = t // seq
    rows = seq // GRID_W
    nblk = rows // NA_QROWS
    kblk_per_batch = seq // NA_KBLK
    past = ctx_k.shape[1]
    grid = (N_HEADS_B // 2, nblk, b)

    def kv_spec(col0, j):
        return pl.BlockSpec(
            (NA_KBLK, LANES),
            lambda hp, i, bi: (bi * kblk_per_batch + _na_first_key_block(i, rows) + j, col0 // LANES + hp))

    def variant(i):
        return jnp.where(i == 0, 0, jnp.where(i == nblk - 1, 2, 1))

    in_specs = ([pl.BlockSpec((NA_TQ, LANES), lambda hp, i, bi: (bi * nblk + i, COL_QB // LANES + hp))]
                + [kv_spec(COL_KB, j) for j in range(4)]
                + [kv_spec(COL_VB, j) for j in range(4)]
                + [pl.BlockSpec((1, past, LANES), lambda hp, i, bi: (bi, 0, hp)),
                   pl.BlockSpec((1, past, LANES), lambda hp, i, bi: (bi, 0, hp)),
                   pl.BlockSpec((1, 2, NA_TK, NA_TQ), lambda hp, i, bi: (variant(i), hp, 0, 0)),
                   pl.BlockSpec((1, 2, LANES), lambda hp, i, bi: (hp, 0, 0))])
    return pl.pallas_call(
        _na_kernel,
        out_shape=jax.ShapeDtypeStruct((t, WIDTH_B), F32),
        grid=grid,
        in_specs=in_specs,
        out_specs=pl.BlockSpec((NA_TQ, LANES), lambda hp, i, bi: (bi * nblk + i, hp)),
        compiler_params=_cparams(("arbitrary", "arbitrary", "arbitrary")),
        name="latent_neighbourhood",
    )(proj, *([proj] * 8), ctx_k, ctx_v, bias_t, bounds)


def _neighbourhood_bounds(qn_b, rpb):
    n_heads = rpb.shape[0]
    qmax = jnp.broadcast_to(_query_norm_bound(qn_b), (n_heads, 1))
    bmax = jnp.maximum(jnp.max(rpb.reshape(n_heads, -1), axis=1, keepdims=True), 0.0).astype(F32)
    table = jnp.concatenate([qmax, bmax, jnp.zeros((n_heads, LANES - 2), F32)], axis=1)
    return table.reshape(n_heads // 2, 2, LANES)


def _neighbourhood_bias(rpb, rows):
    nblk = rows // NA_QROWS
    n_dr = 2 * NA_KH - 1
    n_dc = 2 * NA_KW - 1
    kc = np.arange(GRID_W)[:, None]
    qc = np.arange(GRID_W)[None, :]
    ws = np.clip(qc - NA_KW // 2, 0, GRID_W - NA_KW)
    col_ok = (kc >= ws) & (kc < ws + NA_KW)
    dc = np.clip(kc - qc + NA_KW - 1, 0, n_dc - 1)
    dc_onehot = (dc[None] == np.arange(n_dc)[:, None, None]).astype(np.float32)
    tiles = jnp.einsum('hab,bkq->hakq', rpb.astype(F32), jnp.asarray(dc_onehot),
                       precision=lax.Precision.HIGHEST)
    tiles = jnp.where(jnp.asarray(col_ok)[None, None], tiles, MASKED)
    masked_tile = jnp.full((rpb.shape[0], 1, GRID_W, GRID_W), MASKED, F32)
    tiles = jnp.concatenate([tiles, masked_tile], axis=1)
    pick = np.zeros((3, NA_KROWS, NA_QROWS, n_dr + 1), np.float32)
    for v, i in enumerate((0, 1, nblk - 1)):
        r0 = i * NA_QROWS
        ks = int(np.clip(r0 - NA_KH // 2, 0, rows - NA_KROWS))
        for kl in range(NA_KROWS):
            for ql in range(NA_QROWS):
                kr, qr = ks + kl, r0 + ql
                rs = int(np.clip(qr - NA_KH // 2, 0, rows - NA_KH))
                ok = rs <= kr < rs + NA_KH
                pick[v, kl, ql, (kr - qr + NA_KH - 1) if ok else n_dr] = 1.0
    bias = jnp.einsum('vkqa,hacd->vhkcqd', jnp.asarray(pick), tiles, precision=lax.Precision.HIGHEST)
    return bias.reshape(3, rpb.shape[0], NA_TK, NA_TQ)


def _merge_kernel(xp_ref, oap_ref, obp_ref, xs_ref, oas_ref, obs_ref, ona_ref, onb_ref, wo_ref,
                  g1_ref, sh2_ref, sc2_ref, n2_ref, y_ref, h_ref, hp_ref, *, ctx_tiles):
    def one_stream(x_ref, oa_ref, ob_ref):
        na = (_rms(oa_ref[...]) * ona_ref[...]).astype(BF16)
        nb = (_rms(ob_ref[...]) * onb_ref[...]).astype(BF16)
        mix = _dot(na, wo_ref[0:WIDTH_A, :]) + _dot(nb, wo_ref[WIDTH_A:WIDTH_A + WIDTH_B, :])
        y = x_ref[...] + g1_ref[0] * mix
        y_ref[...] = y
        h = _rms(y) * n2_ref[...]
        h = h * (1.0 + sc2_ref[0]) + sh2_ref[0]
        h_ref[...] = h
        hp_ref[...] = _pack_rows(h)

    i = pl.program_id(0)
    pl.when(i < ctx_tiles)(lambda: one_stream(xp_ref, oap_ref, obp_ref))
    pl.when(i >= ctx_tiles)(lambda: one_stream(xs_ref, oas_ref, obs_ref))


def _merge(ctx, lat, on_a, on_b, w_out_bf, gate1, shift2, scale2, norm2, *, tm, lat_seq):
    t_c, d = ctx[0].shape
    t_l = lat[0].shape[0]
    ctx_tiles = t_c // tm
    lat_tiles_per_batch = lat_seq // tm

    def ctx_map(i):
        return (jnp.minimum(i, ctx_tiles - 1), 0)

    def lat_map(i):
        return (jnp.maximum(i - ctx_tiles, 0), 0)

    def mod_map(i):
        return (jnp.where(i < ctx_tiles, 0, 1 + (i - ctx_tiles) // lat_tiles_per_batch), 0, 0)

    def stream_specs(index_map):
        return [pl.BlockSpec((tm, d), index_map),
                pl.BlockSpec((tm, WIDTH_A), index_map),
                pl.BlockSpec((tm, WIDTH_B), index_map)]

    return pl.pallas_call(
        functools.partial(_merge_kernel, ctx_tiles=ctx_tiles),
        out_shape=[jax.ShapeDtypeStruct((t_c + t_l, d), F32)] * 2
        + [jax.ShapeDtypeStruct((t_c + t_l, d // 2), jnp.int32)],
        grid=((t_c + t_l) // tm,),
        in_specs=stream_specs(ctx_map) + stream_specs(lat_map) + [
            pl.BlockSpec((1, WIDTH_A), lambda i: (0, 0)),
            pl.BlockSpec((1, WIDTH_B), lambda i: (0, 0)),
            pl.BlockSpec((WIDTH_A + WIDTH_B, d), lambda i: (0, 0)),
            pl.BlockSpec((1, 1, d), mod_map),
            pl.BlockSpec((1, 1, d), mod_map),
            pl.BlockSpec((1, 1, d), mod_map),
            pl.BlockSpec((1, d), lambda i: (0, 0))],
        out_specs=[pl.BlockSpec((tm, d), lambda i: (i, 0))] * 2 + [pl.BlockSpec((tm, d // 2), lambda i: (i, 0))],
        compiler_params=_cparams(("arbitrary",)),
        name="merge_out_proj",
    )(*ctx, *lat, on_a, on_b, w_out_bf, gate1, shift2, scale2, norm2)


def _first_index_of_max(x, iota):
    mx = jnp.max(x, axis=0, keepdims=True)
    idx = jnp.min(jnp.where(x == mx, iota, float(x.shape[0])), axis=0, keepdims=True)
    return mx, iota == idx


def _router_gates(h, wr_hi, wr_lo, rbias):
    h_hi = h.astype(BF16)
    h_lo = (h - h_hi.astype(F32)).astype(BF16)
    logits = _dot_nt(wr_hi, h_hi) + (_dot_nt(wr_lo, h_hi) + _dot_nt(wr_hi, h_lo))
    scores = _sigmoid(logits)
    sel = scores + rbias
    tm = sel.shape[1]
    iota_g = lax.broadcasted_iota(jnp.int32, (GROUP_SIZE, tm), 0).astype(F32)
    group_scores = []
    for g in range(N_GROUPS):
        grp = sel[g * GROUP_SIZE:(g + 1) * GROUP_SIZE]
        m1, first = _first_index_of_max(grp, iota_g)
        m2 = jnp.max(jnp.where(first, -jnp.inf, grp), axis=0, keepdims=True)
        group_scores.append(m1 + m2)
    gs = jnp.concatenate(group_scores, axis=0)
    iota_n = lax.broadcasted_iota(jnp.int32, (N_GROUPS, tm), 0).astype(F32)
    group_on = jnp.zeros((N_GROUPS, tm), F32)
    for _ in range(TOPK_GROUPS):
        _, pick = _first_index_of_max(gs, iota_n)
        group_on = jnp.where(pick, 1.0, group_on)
        gs = jnp.where(pick, -jnp.inf, gs)
    expert_on = jnp.concatenate(
        [jnp.broadcast_to(group_on[g:g + 1], (GROUP_SIZE, tm)) for g in range(N_GROUPS)], axis=0)
    cand = jnp.where(expert_on > 0.0, sel, -jnp.inf)
    iota_e = lax.broadcasted_iota(jnp.int32, (N_EXPERTS, tm), 0).astype(F32)
    w = jnp.zeros((N_EXPERTS, tm), F32)
    chosen = jnp.zeros((N_EXPERTS, tm), F32)
    for _ in range(TOP_K):
        _, pick = _first_index_of_max(cand, iota_e)
        w = jnp.where(pick, scores, w)
        chosen = jnp.where(pick, 1.0, chosen)
        cand = jnp.where(pick, -jnp.inf, cand)
    return w / jnp.sum(w, axis=0, keepdims=True) * ROUTED_SCALE, chosen


MOE_TS = 1024
MOE_ROUTE_TM = 512
MOE_ROW_TM = 512


def _route_kernel(h_ref, wrh_ref, wrl_ref, rb_ref, tri_ref, gates_ref, rank_ref, count_ref):
    @pl.when(pl.program_id(0) == 0)
    def _():
        count_ref[...] = jnp.zeros_like(count_ref)

    gates, chosen = _router_gates(h_ref[...], wrh_ref[...], wrl_ref[...], rb_ref[...])
    gates_ref[...] = gates
    before = _dot(chosen.astype(BF16), tri_ref[...])
    seen = count_ref[...]
    rank_ref[...] = jnp.where(chosen > 0.0, before + seen[:, 0:1], -1.0)
    count_ref[...] = seen + jnp.sum(chosen, axis=1, keepdims=True)


def _route(h_all, wr_hi, wr_lo, rbias):
    t, d = h_all.shape
    tm = MOE_ROUTE_TM
    tri = jnp.asarray(np.triu(np.ones((tm, tm), np.float32), k=1), BF16)
    return pl.pallas_call(
        _route_kernel,
        out_shape=[jax.ShapeDtypeStruct((N_EXPERTS, t), F32),
                   jax.ShapeDtypeStruct((N_EXPERTS, t), F32),
                   jax.ShapeDtypeStruct((N_EXPERTS, LANES), F32)],
        grid=(t // tm,),
        in_specs=[pl.BlockSpec((tm, d), lambda i: (i, 0)),
                  pl.BlockSpec((N_EXPERTS, d), lambda i: (0, 0)),
                  pl.BlockSpec((N_EXPERTS, d), lambda i: (0, 0)),
                  pl.BlockSpec((N_EXPERTS, 1), lambda i: (0, 0)),
                  pl.BlockSpec((tm, tm), lambda i: (0, 0))],
        out_specs=[pl.BlockSpec((N_EXPERTS, tm), lambda i: (0, i)),
                   pl.BlockSpec((N_EXPERTS, tm), lambda i: (0, i)),
                   pl.BlockSpec((N_EXPERTS, LANES), lambda i: (0, 0))],
        compiler_params=_cparams(("arbitrary",)),
        name="moe_route",
    )(h_all, wr_hi, wr_lo, rbias, tri)


def _slots_kernel(gates_ref, rank_ref, off_ref, pos_ref, gtok_ref):
    gates = gates_ref[...]
    rank = rank_ref[...]
    tm = gates.shape[1]
    slot = off_ref[...] + rank
    left = jnp.where(rank >= 0.0, 1.0, 0.0)
    iota_e = lax.broadcasted_iota(jnp.int32, (N_EXPERTS, tm), 0).astype(F32)
    pos_rows, gate_rows = [], []
    for _ in range(TOP_K):
        _, pick = _first_index_of_max(left, iota_e)
        pos_rows.append(jnp.sum(jnp.where(pick, slot, 0.0), axis=0, keepdims=True))
        gate_rows.append(jnp.sum(jnp.where(pick, gates, 0.0), axis=0, keepdims=True))
        left = jnp.where(pick, 0.0, left)
    pos_ref[...] = jnp.concatenate(pos_rows, axis=0).astype(jnp.int32)
    pad = jnp.zeros((LANES - TOP_K, tm), F32)
    gtok_ref[...] = jnp.concatenate(gate_rows + [pad], axis=0).T


def _slots(gates_t, rank_t, off):
    t = gates_t.shape[1]
    tm = MOE_ROUTE_TM
    return pl.pallas_call(
        _slots_kernel,
        out_shape=[jax.ShapeDtypeStruct((TOP_K, t), jnp.int32), jax.ShapeDtypeStruct((t, LANES), F32)],
        grid=(t // tm,),
        in_specs=[pl.BlockSpec((N_EXPERTS, tm), lambda i: (0, i)),
                  pl.BlockSpec((N_EXPERTS, tm), lambda i: (0, i)),
                  pl.BlockSpec((N_EXPERTS, 1), lambda i: (0, 0))],
        out_specs=[pl.BlockSpec((TOP_K, tm), lambda i: (0, i)),
                   pl.BlockSpec((tm, LANES), lambda i: (i, 0))],
        compiler_params=_cparams(("arbitrary",)),
        name="moe_slots",
    )(gates_t, rank_t, off)


SC_CORES = 2
SC_SUBCORES = 16
SC_ROWS = 64


def _dispatch(hp_all, slot_of):
    t, width = hp_all.shape
    n_slots = slot_of.shape[0]
    n_pad = n_slots - TOP_K * t
    workers = SC_CORES * SC_SUBCORES
    per_worker = t // workers
    pad_per_worker = n_pad // workers
    assert per_worker * workers == t and per_worker % SC_ROWS == 0
    assert pad_per_worker * workers == n_pad and pad_per_worker % SC_ROWS == 0
    mesh = plsc.VectorSubcoreMesh(core_axis_name="core", subcore_axis_name="subcore")

    @functools.partial(
        pl.kernel, mesh=mesh,
        out_type=jax.ShapeDtypeStruct((n_slots, width), jnp.int32),
        scratch_types=[pltpu.VMEM((SC_ROWS,), jnp.int32),
                       pltpu.VMEM((SC_ROWS, width), jnp.int32),
                       pltpu.SemaphoreType.DMA],
    )
    def scatter_rows(h_hbm, slot_hbm, out_hbm, idx_v, rows_v, sem):
        worker = lax.axis_index("subcore") * SC_CORES + lax.axis_index("core")
        base = worker * per_worker

        @pl.loop(0, per_worker // SC_ROWS)
        def _(j):
            first = base + j * SC_ROWS
            pltpu.sync_copy(h_hbm.at[pl.ds(first, SC_ROWS)], rows_v)
            for k in range(TOP_K):
                pltpu.sync_copy(slot_hbm.at[pl.ds(k * t + first, SC_ROWS)], idx_v)
                pltpu.async_copy(rows_v, out_hbm.at[idx_v], sem).wait()

        pltpu.sync_copy(h_hbm.at[pl.ds(0, SC_ROWS)], rows_v)
        pad_base = TOP_K * t + worker * pad_per_worker

        @pl.loop(0, pad_per_worker // SC_ROWS)
        def _(j):
            pltpu.sync_copy(slot_hbm.at[pl.ds(pad_base + j * SC_ROWS, SC_ROWS)], idx_v)
            pltpu.async_copy(rows_v, out_hbm.at[idx_v], sem).wait()

    return scatter_rows(hp_all, slot_of)


def _experts_kernel(te_ref, xs_ref, wg_ref, wu_ref, wd_ref, ys_ref, wg_bf, wu_bf, wd_bf):
    i = pl.program_id(0)

    @pl.when((i == 0) | (te_ref[i] != te_ref[jnp.maximum(i, 1) - 1]))
    def _():
        wg_bf[...] = wg_ref[0].astype(BF16)
        wu_bf[...] = wu_ref[0].astype(BF16)
        wd_bf[...] = wd_ref[0].astype(BF16)

    left, right = _unpack_rows(xs_ref[...])
    x = jnp.concatenate([left, right], axis=1).astype(BF16)
    g = _dot(x, wg_bf[...])
    u = _dot(x, wu_bf[...])
    act = (g * _sigmoid(g)) * u
    ys_ref[...] = _pack_rows(_dot(act.astype(BF16), wd_bf[...]))


def _experts(xs, tile_expert, w_gate, w_up, w_down):
    n_slots, width = xs.shape
    d = 2 * width
    ts = MOE_TS
    return pl.pallas_call(
        _experts_kernel,
        out_shape=jax.ShapeDtypeStruct((n_slots, width), jnp.int32),
        grid_spec=pltpu.PrefetchScalarGridSpec(
            num_scalar_prefetch=1,
            grid=(n_slots // ts,),
            in_specs=[pl.BlockSpec((ts, width), lambda i, te: (i, 0)),
                      pl.BlockSpec((1, d, D_EXPERT), lambda i, te: (te[i], 0, 0)),
                      pl.BlockSpec((1, d, D_EXPERT), lambda i, te: (te[i], 0, 0)),
                      pl.BlockSpec((1, D_EXPERT, d), lambda i, te: (te[i], 0, 0))],
            out_specs=pl.BlockSpec((ts, width), lambda i, te: (i, 0)),
            scratch_shapes=[pltpu.VMEM((d, D_EXPERT), BF16), pltpu.VMEM((d, D_EXPERT), BF16),
                            pltpu.VMEM((D_EXPERT, d), BF16)]),
        compiler_params=_cparams(("arbitrary",)),
        name="moe_experts",
    )(tile_expert, xs, w_gate, w_up, w_down)


def _gather_slots(y_slots, slot_of, t):
    width = y_slots.shape[1]
    workers = SC_CORES * SC_SUBCORES
    per_worker = t // workers
    n_blocks = (per_worker // SC_ROWS) * TOP_K
    assert per_worker * workers == t and per_worker % SC_ROWS == 0 and n_blocks % 2 == 0
    mesh = plsc.VectorSubcoreMesh(core_axis_name="core", subcore_axis_name="subcore")

    @functools.partial(
        pl.kernel, mesh=mesh,
        out_type=jax.ShapeDtypeStruct((TOP_K * t, width), jnp.int32),
        scratch_types=[pltpu.VMEM((SC_ROWS,), jnp.int32), pltpu.VMEM((SC_ROWS,), jnp.int32),
                       pltpu.VMEM((SC_ROWS, width), jnp.int32), pltpu.VMEM((SC_ROWS, width), jnp.int32),
                       pltpu.SemaphoreType.DMA, pltpu.SemaphoreType.DMA],
    )
    def gather_rows(ys_hbm, slot_hbm, out_hbm, idx0, idx1, rows0, rows1, sem0, sem1):
        worker = lax.axis_index("subcore") * SC_CORES + lax.axis_index("core")
        base = worker * per_worker

        def first_row(n):
            return (n % TOP_K) * t + base + (n // TOP_K) * SC_ROWS

        def start(n, idx_v, rows_v, sem):
            pltpu.sync_copy(slot_hbm.at[pl.ds(first_row(n), SC_ROWS)], idx_v)
            pltpu.async_copy(ys_hbm.at[idx_v], rows_v, sem)

        def finish(n, idx_v, rows_v, sem):
            pltpu.make_async_copy(ys_hbm.at[idx_v], rows_v, sem).wait()
            pltpu.sync_copy(rows_v, out_hbm.at[pl.ds(first_row(n), SC_ROWS)])

        start(0, idx0, rows0, sem0)

        @pl.loop(0, n_blocks, step=2)
        def _(n):
            start(n + 1, idx1, rows1, sem1)
            finish(n, idx0, rows0, sem0)

            @pl.when(n + 2 < n_blocks)
            def _():
                start(n + 2, idx0, rows0, sem0)

            finish(n + 1, idx1, rows1, sem1)

    return gather_rows(y_slots, slot_of)


def _combine_kernel(y_ref, h_ref, g2_ref, gtok_ref, rows_ref, wgs_ref, wus_ref, wds_ref, o_ref):
    h_left, h_right = _unpack_rows(h_ref[...])
    h = jnp.concatenate([h_left, h_right], axis=1).astype(BF16)
    gs = _dot(h, wgs_ref[...])
    us = _dot(h, wus_ref[...])
    shared = _dot(((gs * _sigmoid(gs)) * us).astype(BF16), wds_ref[...])

    gtok = gtok_ref[...]
    acc_left = acc_right = None
    for k in range(TOP_K):
        left, right = _unpack_rows(rows_ref[k])
        gate = gtok[:, k:k + 1]
        acc_left = gate * left if acc_left is None else acc_left + gate * left
        acc_right = gate * right if acc_right is None else acc_right + gate * right
    routed = jnp.concatenate([acc_left, acc_right], axis=1)
    o_ref[...] = y_ref[...] + g2_ref[0] * (routed + shared)


def _combine(y_all, hp_all, gate2, gtok, rows, wgs, wus, wds, *, first_token, tokens, seq):
    d = y_all.shape[1]
    width = hp_all.shape[1]
    tm = MOE_ROW_TM
    tile0 = first_token // tm
    nb = gate2.shape[0]
    tiles_per_batch = seq // tm

    def mod_map(i):
        return ((i // tiles_per_batch) if nb > 1 else 0, 0, 0)

    return pl.pallas_call(
        _combine_kernel,
        out_shape=jax.ShapeDtypeStruct((tokens, d), F32),
        grid=(tokens // tm,),
        in_specs=[pl.BlockSpec((tm, d), lambda i: (tile0 + i, 0)),
                  pl.BlockSpec((tm, width), lambda i: (tile0 + i, 0)),
                  pl.BlockSpec((1, 1, d), mod_map),
                  pl.BlockSpec((tm, LANES), lambda i: (tile0 + i, 0)),
                  pl.BlockSpec((TOP_K, tm, width), lambda i: (0, tile0 + i, 0)),
                  pl.BlockSpec((d, D_SHARED), lambda i: (0, 0)),
                  pl.BlockSpec((d, D_SHARED), lambda i: (0, 0)),
                  pl.BlockSpec((D_SHARED, d), lambda i: (0, 0))],
        out_specs=pl.BlockSpec((tm, d), lambda i: (i, 0)),
        compiler_params=_cparams(("arbitrary",)),
        name="moe_combine",
    )(y_all, hp_all, gate2, gtok, rows, wgs, wus, wds)


def _expert_layout(counts, n_tiles):
    cnt = counts.astype(jnp.int32)
    tiles = (cnt + (MOE_TS - 1)) // MOE_TS
    last_tile = jnp.cumsum(tiles)
    off = (last_tile - tiles) * MOE_TS
    pad_lo = off + cnt
    pad_hi = (off + tiles * MOE_TS).at[N_EXPERTS - 1].set(n_tiles * MOE_TS)
    pad_cnt = pad_hi - pad_lo
    pad_last = jnp.cumsum(pad_cnt)
    shift = pad_lo - (pad_last - pad_cnt)
    j = jnp.arange(N_EXPERTS * MOE_TS, dtype=jnp.int32)
    past = (pad_last[None, :-1] <= j[:, None]).astype(jnp.int32)
    pad_slots = j + shift[0] + jnp.sum(past * (shift[1:] - shift[:-1])[None, :], axis=1)
    tile_ids = jnp.arange(n_tiles, dtype=jnp.int32)
    tile_expert = jnp.minimum(
        jnp.sum((last_tile[None, :] <= tile_ids[:, None]).astype(jnp.int32), axis=1), N_EXPERTS - 1)
    return off, pad_slots, tile_expert


def _rope_tables(n_tokens):
    t = jnp.arange(n_tokens)
    row = (t // GRID_W).astype(F32)
    col = (t % GRID_W).astype(F32)
    nf = HEAD_DIM // 4
    freqs = ROPE_THETA ** (-jnp.arange(nf, dtype=F32) / nf)
    ang_r = row[:, None] * freqs
    ang_c = col[:, None] * freqs
    cos = jnp.concatenate([jnp.cos(ang_r)] * 2 + [jnp.cos(ang_c)] * 2, axis=1)
    sin = jnp.concatenate([-jnp.sin(ang_r), jnp.sin(ang_r), -jnp.sin(ang_c), jnp.sin(ang_c)], axis=1)
    reps = LANES // HEAD_DIM
    return jnp.tile(cos, (1, reps)), jnp.tile(sin, (1, reps))


def _head_gains(qn_a, kn_a, qn_b, kn_b):
    ones = jnp.ones((HEAD_DIM,), F32)
    parts = ([qn_a] * N_HEADS_A + [kn_a] * N_KV_A + [ones] * N_KV_A
             + [qn_b] * N_HEADS_B + [kn_b] * N_HEADS_B + [ones] * N_HEADS_B)
    return jnp.concatenate(parts).reshape(1, IN_COLS).astype(F32)


def _same_head_indicator():
    i = np.arange(LANES)
    return jnp.asarray((i[:, None] // HEAD_DIM) == (i[None, :] // HEAD_DIM), BF16)


def _token_major(cache):
    b, h, s, hd = cache.shape
    return cache.transpose(0, 2, 1, 3).reshape(b, s, h * hd).astype(BF16)


def kernel(x_prompt, x_sample, cache_k_a, cache_v_a, cache_k_b, cache_v_b, c, c_ctx, w_mod, b_mod, norm1, norm2, w_in, qn_a, kn_a, qn_b, kn_b, rpb, on_a, on_b, w_out, w_router, router_bias, w_gate_e, w_up_e, w_down_e, w_gate_s, w_up_s, w_down_s):
    depth = w_mod.shape[0]
    assert depth == 1
    l = 0
    bp, sp, d = x_prompt.shape
    bs, ss, _ = x_sample.shape

    cvec = jnp.concatenate([c_ctx[None, :], c], axis=0)
    rows = -(-cvec.shape[0] // 8) * 8
    cvec = jnp.pad(cvec, ((0, rows - cvec.shape[0]), (0, 0)))
    mod = _adaln(cvec, w_mod[l], b_mod[l])
    mod_p = [m.reshape(1, 1, d) for m in jnp.split(mod[0:1], 6, axis=-1)]
    mod_s = [m.reshape(bs, 1, d) for m in jnp.split(mod[1:1 + bs], 6, axis=-1)]
    mod_all = [m.reshape(1 + bs, 1, d) for m in jnp.split(mod[0:1 + bs], 6, axis=-1)]

    w_in_bf = w_in[l].astype(BF16)
    w_out_bf = w_out[l].astype(BF16)
    gain = _head_gains(qn_a[l], kn_a[l], qn_b[l], kn_b[l])
    seg = _same_head_indicator()
    n1 = norm1[l].reshape(1, d)
    n2 = norm2[l].reshape(1, d)
    ona = on_a[l].reshape(1, WIDTH_A)
    onb = on_b[l].reshape(1, WIDTH_B)
    wr_t = w_router[l].T
    wr_hi = wr_t.astype(BF16)
    wr_lo = (wr_t - wr_hi.astype(F32)).astype(BF16)
    rbias = router_bias[l].reshape(N_EXPERTS, 1).astype(F32)
    wgs = w_gate_s[l].astype(BF16)
    wus = w_up_s[l].astype(BF16)
    wds = w_down_s[l].astype(BF16)
    t_p = bp * sp
    t_s = bs * ss
    t_all = t_p + t_s

    xp = x_prompt.reshape(t_p, d)
    proj_p, st_ka, st_va, st_kb, st_vb = _project(
        xp, mod_p[0], mod_p[1], n1, w_in_bf, gain, seg, None, tm=sp, seq=sp, states=True)
    oa_p, ob_p = _context_attention(proj_p, seq=sp)

    xs = x_sample.reshape(t_s, d)
    proj_s, = _project(xs, mod_s[0], mod_s[1], n1, w_in_bf, gain, seg, _rope_tables(ss),
                       tm=512, seq=ss, states=False)
    oa_s = _latent_gqa(proj_s, _token_major(cache_k_a[:, l]), _token_major(cache_v_a[:, l]),
                       _query_norm_bound(qn_a[l]), seq=ss, tq=128)
    bias_t = _neighbourhood_bias(rpb[l], ss // GRID_W)
    ob_s = _latent_neighbourhood(proj_s, _token_major(cache_k_b[:, l]), _token_major(cache_v_b[:, l]),
                                 bias_t, _neighbourhood_bounds(qn_b[l], rpb[l]), seq=ss)

    y1_all, h_all, hp_all = _merge((xp, oa_p, ob_p), (xs, oa_s, ob_s), ona, onb, w_out_bf,
                                   mod_all[2], mod_all[3], mod_all[4], n2, tm=512, lat_seq=ss)
    gates_t, rank_t, counts = _route(h_all, wr_hi, wr_lo, rbias)
    n_tiles = t_all * TOP_K // MOE_TS + N_EXPERTS
    off, pad_slots, tile_expert = _expert_layout(counts[:, 0], n_tiles)
    pos, gtok = _slots(gates_t, rank_t, off.astype(F32).reshape(N_EXPERTS, 1))
    slot_of = pos.reshape(TOP_K * t_all)
    x_slots = _dispatch(hp_all, jnp.concatenate([slot_of, pad_slots]))
    y_slots = _experts(x_slots, tile_expert, w_gate_e[l], w_up_e[l], w_down_e[l])
    rows = _gather_slots(y_slots, slot_of, t_all).reshape(TOP_K, t_all, d // 2)
    y_p = _combine(y1_all, hp_all, mod_p[5], gtok, rows, wgs, wus, wds,
                   first_token=0, tokens=t_p, seq=sp)
    y_s = _combine(y1_all, hp_all, mod_s[5], gtok, rows, wgs, wus, wds,
                   first_token=t_p, tokens=t_s, seq=ss)

    return (y_p.reshape(bp, sp, d), y_s.reshape(bs, ss, d), st_ka, st_va, st_kb, st_vb)
```

```python
import functools

import numpy as np
import jax
import jax.numpy as jnp
from jax import lax
from jax.experimental import pallas as pl
from jax.experimental.pallas import tpu as pltpu
from jax.experimental.pallas import tpu_sc as plsc

F32 = jnp.float32
BF16 = jnp.bfloat16

D_MODEL = 1024
HEAD_DIM = 64
N_HEADS_A = 8
N_KV_A = 2
GROUP_A = N_HEADS_A // N_KV_A
N_HEADS_B = 8
WIDTH_A = N_HEADS_A * HEAD_DIM
WIDTH_B = N_HEADS_B * HEAD_DIM
KV_WIDTH_A = N_KV_A * HEAD_DIM
IN_COLS = WIDTH_A + 2 * KV_WIDTH_A + 3 * WIDTH_B
GRID_W = 64
ROPE_THETA = 10000.0
NA_KH = 8
NA_KW = 16
N_EXPERTS = 64
N_GROUPS = 8
GROUP_SIZE = N_EXPERTS // N_GROUPS
TOPK_GROUPS = 4
TOP_K = 8
D_EXPERT = 256
D_SHARED = 256
ROUTED_SCALE = 2.5
EPS = 1e-6

LANES = 128
MXU_DIM = 256
MASKED = -1e30

COL_QA = 0
COL_KA = WIDTH_A
COL_VA = COL_KA + KV_WIDTH_A
COL_QB = COL_VA + KV_WIDTH_A
COL_KB = COL_QB + WIDTH_B
COL_VB = COL_KB + WIDTH_B

NA_QROWS = 8
NA_KROWS = 2 * NA_KH
NA_TQ = NA_QROWS * GRID_W
NA_TK = NA_KROWS * GRID_W
NA_KBLK = 256

VMEM_LIMIT = 56 * 1024 * 1024


def _cparams(sem):
    return pltpu.CompilerParams(dimension_semantics=sem, vmem_limit_bytes=VMEM_LIMIT)


def _dot(a, b):
    return jnp.dot(a, b, preferred_element_type=F32)


def _dot_nt(a, b):
    return lax.dot_general(a, b, (((1,), (1,)), ((), ())), preferred_element_type=F32)


def _sigmoid(x):
    return 1.0 / (1.0 + jnp.exp(-x))


def _rms(x):
    return x * lax.rsqrt(jnp.mean(x * x, axis=-1, keepdims=True) + EPS)


def _pack_rows(x):
    n = x.shape[1] // 2
    hi = lax.bitcast_convert_type(x[:, :n].astype(BF16).astype(F32), jnp.int32)
    lo = lax.bitcast_convert_type(x[:, n:].astype(BF16).astype(F32), jnp.int32)
    return hi | lax.shift_right_logical(lo, 16)


def _unpack_rows(w):
    left = lax.bitcast_convert_type(w & jnp.int32(-65536), F32)
    right = lax.bitcast_convert_type(lax.shift_left(w, 16), F32)
    return left, right


def _mod_kernel(c_ref, w_ref, b_ref, o_ref):
    c = c_ref[...]
    s = c * _sigmoid(c)
    o_ref[...] = jnp.dot(s, w_ref[...], preferred_element_type=F32,
                         precision=lax.Precision.HIGHEST) + b_ref[...]


def _adaln(cvec, w_mod, b_mod):
    rows, d = cvec.shape
    n = w_mod.shape[1]
    tn = 512
    return pl.pallas_call(
        _mod_kernel,
        out_shape=jax.ShapeDtypeStruct((rows, n), F32),
        grid=(n // tn,),
        in_specs=[pl.BlockSpec((rows, d), lambda j: (0, 0)),
                  pl.BlockSpec((d, tn), lambda j: (0, j)),
                  pl.BlockSpec((1, tn), lambda j: (0, j))],
        out_specs=pl.BlockSpec((rows, tn), lambda j: (0, j)),
        compiler_params=_cparams(("arbitrary",)),
        name="adaln_mod",
    )(cvec, w_mod, b_mod.reshape(1, n))


_PROJ_CHUNKS = (
    [(COL_QA + i * LANES, LANES, True, True) for i in range(WIDTH_A // LANES)]
    + [(COL_KA, LANES, True, True), (COL_VA, LANES, False, False)]
    + [(COL_QB + i * LANES, LANES, True, False) for i in range(WIDTH_B // LANES)]
    + [(COL_KB + i * LANES, LANES, True, False) for i in range(WIDTH_B // LANES)]
    + [(COL_VB + i * LANES, LANES, False, False) for i in range(WIDTH_B // LANES)]
)


def _proj_kernel(*refs, rope, states):
    x_ref, sh_ref, sc_ref, n1_ref, w_ref, gain_ref, seg_ref = refs[:7]
    pos = 7
    if rope:
        cos_ref, sin_ref = refs[pos:pos + 2]
        pos += 2
    out_ref = refs[pos]
    pos += 1
    if states:
        ka_ref, va_ref, kb_ref, vb_ref = refs[pos:pos + 4]

    x = x_ref[...]
    h = _rms(x) * n1_ref[...]
    h = h * (1.0 + sc_ref[0]) + sh_ref[0]
    p = _dot(h.astype(BF16), w_ref[...])
    seg = seg_ref[...]
    if rope:
        cos = cos_ref[...]
        sin = sin_ref[...]
        lane = lax.broadcasted_iota(jnp.int32, cos.shape, 1)
        first_half = (lane % (HEAD_DIM // 2)) < (HEAD_DIM // 4)

    for c0, w, normed, roped in _PROJ_CHUNKS:
        pc = p[:, c0:c0 + w]
        if normed:
            sq = pc * pc
            hi = sq.astype(BF16)
            lo = (sq - hi.astype(F32)).astype(BF16)
            ss = _dot(hi, seg) + _dot(lo, seg)
            pc = pc * lax.rsqrt(ss * (1.0 / HEAD_DIM) + EPS) * gain_ref[:, c0:c0 + w]
        if states:
            if c0 == COL_KA:
                for hh in range(N_KV_A):
                    ka_ref[0, 0, hh] = pc[:, hh * HEAD_DIM:(hh + 1) * HEAD_DIM]
            elif c0 == COL_VA:
                for hh in range(N_KV_A):
                    va_ref[0, 0, hh] = pc[:, hh * HEAD_DIM:(hh + 1) * HEAD_DIM]
            elif COL_KB <= c0 < COL_VB:
                base = (c0 - COL_KB) // HEAD_DIM
                for hh in range(LANES // HEAD_DIM):
                    kb_ref[0, 0, base + hh] = pc[:, hh * HEAD_DIM:(hh + 1) * HEAD_DIM]
            elif c0 >= COL_VB:
                base = (c0 - COL_VB) // HEAD_DIM
                for hh in range(LANES // HEAD_DIM):
                    vb_ref[0, 0, base + hh] = pc[:, hh * HEAD_DIM:(hh + 1) * HEAD_DIM]
        if rope and roped:
            partner = jnp.where(first_half,
                                pltpu.roll(pc, LANES - HEAD_DIM // 4, 1),
                                pltpu.roll(pc, HEAD_DIM // 4, 1))
            pc = pc * cos + partner * sin
        if c0 < COL_KA or COL_QB <= c0 < COL_KB:
            pc = pc * (HEAD_DIM ** -0.5)
        out_ref[:, c0:c0 + w] = pc.astype(BF16)


def _project(x2d, shift, scale, norm1, w_in_bf, gain, seg, rope_tabs, *, tm, seq, states):
    t, d = x2d.shape
    nb = shift.shape[0]
    tiles_per_batch = seq // tm
    rope = rope_tabs is not None

    def mod_map(i):
        return ((i // tiles_per_batch) if nb > 1 else 0, 0, 0)

    in_specs = [pl.BlockSpec((tm, d), lambda i: (i, 0)),
                pl.BlockSpec((1, 1, d), mod_map),
                pl.BlockSpec((1, 1, d), mod_map),
                pl.BlockSpec((1, d), lambda i: (0, 0)),
                pl.BlockSpec((d, IN_COLS), lambda i: (0, 0)),
                pl.BlockSpec((1, IN_COLS), lambda i: (0, 0)),
                pl.BlockSpec((LANES, LANES), lambda i: (0, 0))]
    args = [x2d, shift, scale, norm1, w_in_bf, gain, seg]
    if rope:
        in_specs += [pl.BlockSpec((tm, LANES), lambda i: (i % tiles_per_batch, 0))] * 2
        args += list(rope_tabs)
    out_shape = [jax.ShapeDtypeStruct((t, IN_COLS), BF16)]
    out_specs = [pl.BlockSpec((tm, IN_COLS), lambda i: (i, 0))]
    if states:
        assert tm == seq
        b = t // seq
        for nh in (N_KV_A, N_KV_A, N_HEADS_B, N_HEADS_B):
            out_shape.append(jax.ShapeDtypeStruct((b, 1, nh, seq, HEAD_DIM), F32))
            out_specs.append(pl.BlockSpec((1, 1, nh, seq, HEAD_DIM), lambda i: (i, 0, 0, 0, 0)))
    return pl.pallas_call(
        functools.partial(_proj_kernel, rope=rope, states=states),
        out_shape=out_shape,
        grid=(t // tm,),
        in_specs=in_specs,
        out_specs=out_specs,
        compiler_params=_cparams(("arbitrary",)),
        name="proj_states" if states else "proj_rope",
    )(*args)


def _lane_half(shape):
    return lax.broadcasted_iota(jnp.int32, shape, 1) // HEAD_DIM


def _keep_half(x, half):
    return jnp.where(_lane_half(x.shape) == half, x, jnp.zeros_like(x))


def _transpose_bf16(x):
    return x.astype(F32).T.astype(BF16)


def _attend(q, keys, values_t, biases):
    return _softmax_av(_scores(q, keys, biases), values_t)


def _scores(q, keys, biases):
    scores = []
    for k, b in zip(keys, biases):
        s = _dot_nt(k, q)
        if b is not None:
            s = s + b
        scores.append(s)
    return scores


def _softmax_av(scores, values_t):
    m = functools.reduce(jnp.maximum, [jnp.max(s, axis=0, keepdims=True) for s in scores])
    denom = None
    out = None
    for s, vt in zip(scores, values_t):
        p = jnp.exp(s - m)
        ps = jnp.sum(p, axis=0, keepdims=True)
        po = _dot(vt, p.astype(BF16))
        denom = ps if denom is None else denom + ps
        out = po if out is None else out + po
    return out / denom


def _swap_halves(q_bf16):
    return pltpu.roll(q_bf16.astype(F32), HEAD_DIM, 1).astype(BF16)


def _gqa_heads(q_of_pair, keys_by_group, values_t):
    outs = []
    for h in range(N_HEADS_A):
        g = h // GROUP_A
        q = q_of_pair(h // 2)
        if h % 2 != g:
            q = _swap_halves(q)
        o = _attend(q, keys_by_group[g], values_t, [None] * len(values_t))
        outs.append(o[g * HEAD_DIM:(g + 1) * HEAD_DIM])
    return jnp.concatenate(outs, axis=0)


def _ctx_attn_kernel(p_ref, oa_ref, ob_ref):
    ka = p_ref[:, COL_KA:COL_KA + LANES]
    va_t = [_transpose_bf16(p_ref[:, COL_VA:COL_VA + LANES])]
    keys_by_group = [[_keep_half(ka, g)] for g in range(N_KV_A)]
    oa = _gqa_heads(lambda i: p_ref[:, COL_QA + i * LANES:COL_QA + (i + 1) * LANES],
                    keys_by_group, va_t)
    oa_ref[...] = oa.T

    outs = []
    for i in range(N_HEADS_B // 2):
        q = p_ref[:, COL_QB + i * LANES:COL_QB + (i + 1) * LANES]
        k = p_ref[:, COL_KB + i * LANES:COL_KB + (i + 1) * LANES]
        vt = [_transpose_bf16(p_ref[:, COL_VB + i * LANES:COL_VB + (i + 1) * LANES])]
        for half in range(2):
            o = _attend(q, [_keep_half(k, half)], vt, [None])
            outs.append(o[half * HEAD_DIM:(half + 1) * HEAD_DIM])
    ob_ref[...] = jnp.concatenate(outs, axis=0).T


def _context_attention(proj, *, seq):
    t = proj.shape[0]
    return pl.pallas_call(
        _ctx_attn_kernel,
        out_shape=[jax.ShapeDtypeStruct((t, WIDTH_A), F32), jax.ShapeDtypeStruct((t, WIDTH_B), F32)],
        grid=(t // seq,),
        in_specs=[pl.BlockSpec((seq, IN_COLS), lambda i: (i, 0))],
        out_specs=[pl.BlockSpec((seq, WIDTH_A), lambda i: (i, 0)),
                   pl.BlockSpec((seq, WIDTH_B), lambda i: (i, 0))],
        compiler_params=_cparams(("arbitrary",)),
        name="context_attention",
    )(proj)


ATTN_SAFE_SHIFT = 40.0
ONES_ROWS = 16


def _round_up_bf16(x):
    return (x * (1.0 + 2.0 ** -6)).astype(BF16).astype(F32)


def _query_norm_bound(gain):
    return jnp.max(jnp.abs(gain)).reshape(1, 1).astype(F32)


def _ones_lane(g):
    return (1 - g) * HEAD_DIM


def _gqa_latent_kernel(q_ref, k_ref, v_ref, ck_ref, cv_ref, qmax_ref, o_ref,
                       kg_ref, ckg_ref, vt_ref, cvt_ref, shift_ref):
    lane_k = lax.broadcasted_iota(jnp.int32, (1, LANES), 1)

    @pl.when(pl.program_id(1) == 0)
    def _():
        k = k_ref[...]
        ck = ck_ref[0]
        vt = v_ref[...].astype(F32).T
        cvt = cv_ref[0].astype(F32).T
        for g in range(N_KV_A):
            kf = _keep_half(k, g).astype(F32)
            ckf = _keep_half(ck, g).astype(F32)
            ksq = jnp.maximum(jnp.max(jnp.sum(kf * kf, axis=1, keepdims=True), axis=0, keepdims=True),
                              jnp.max(jnp.sum(ckf * ckf, axis=1, keepdims=True), axis=0, keepdims=True))
            shift_ref[g] = jnp.broadcast_to(_round_up_bf16(qmax_ref[...] * jnp.sqrt(ksq)), shift_ref.shape[1:])
            kg_ref[g] = jnp.where(lane_k == _ones_lane(g), 1.0, kf).astype(BF16)
            ckg_ref[g] = jnp.where(lane_k == _ones_lane(g), 1.0, ckf).astype(BF16)
            rows = slice(g * HEAD_DIM, (g + 1) * HEAD_DIM)
            vt_ref[g] = jnp.concatenate([vt[rows], jnp.ones((ONES_ROWS, vt.shape[1]), F32)], axis=0).astype(BF16)
            cvt_ref[g] = jnp.concatenate([cvt[rows], jnp.ones((ONES_ROWS, cvt.shape[1]), F32)], axis=0).astype(BF16)

    tq = q_ref.shape[0]
    lane_q = lax.broadcasted_iota(jnp.int32, (GROUP_A * tq, LANES), 1)
    queries, shifts = [], []
    for g in range(N_KV_A):
        qs = []
        for j in range(GROUP_A):
            h = g * GROUP_A + j
            q = q_ref[:, (h // 2) * LANES:(h // 2 + 1) * LANES].astype(F32)
            qs.append(q if h % 2 == g else pltpu.roll(q, HEAD_DIM, 1))
        queries.append(jnp.where(lane_q // HEAD_DIM == g, jnp.concatenate(qs, axis=0), 0.0))
        shifts.append(shift_ref[g][0:1, 0:1])
    safe = jnp.max(jnp.maximum(shift_ref[0], shift_ref[1])) <= ATTN_SAFE_SHIFT

    def attend(probabilities):
        outs = []
        for g, (p_lat, p_ctx) in enumerate(probabilities):
            o = _dot(vt_ref[g], p_lat) + _dot(cvt_ref[g], p_ctx)
            o = o[:HEAD_DIM] / o[HEAD_DIM:HEAD_DIM + 1]
            outs += [o[:, j * tq:(j + 1) * tq] for j in range(GROUP_A)]
        o_ref[...] = jnp.concatenate(outs, axis=0).T

    def with_bound():
        probabilities = []
        for g in range(N_KV_A):
            qa = jnp.where(lane_q == _ones_lane(g), -shifts[g], queries[g]).astype(BF16)
            probabilities.append((jnp.exp(_dot_nt(kg_ref[g], qa)).astype(BF16),
                                  jnp.exp(_dot_nt(ckg_ref[g], qa)).astype(BF16)))
        attend(probabilities)

    def with_row_max():
        scores = []
        for g in range(N_KV_A):
            qa = queries[g].astype(BF16)
            scores.append((_dot_nt(kg_ref[g], qa), _dot_nt(ckg_ref[g], qa)))
        probabilities = []
        for s_lat, s_ctx in scores:
            m = jnp.maximum(jnp.max(s_lat, axis=0, keepdims=True), jnp.max(s_ctx, axis=0, keepdims=True))
            probabilities.append((jnp.exp(s_lat - m).astype(BF16), jnp.exp(s_ctx - m).astype(BF16)))
        attend(probabilities)

    pl.when(safe)(with_bound)
    pl.when(jnp.logical_not(safe))(with_row_max)


def _latent_gqa(proj, ctx_k, ctx_v, qmax, *, seq, tq):
    t = proj.shape[0]
    b = t // seq
    nq = seq // tq
    past = ctx_k.shape[1]
    return pl.pallas_call(
        _gqa_latent_kernel,
        out_shape=jax.ShapeDtypeStruct((t, WIDTH_A), F32),
        grid=(b, nq),
        in_specs=[pl.BlockSpec((tq, WIDTH_A), lambda bi, qi: (bi * nq + qi, 0)),
                  pl.BlockSpec((seq, LANES), lambda bi, qi: (bi, COL_KA // LANES)),
                  pl.BlockSpec((seq, LANES), lambda bi, qi: (bi, COL_VA // LANES)),
                  pl.BlockSpec((1, past, LANES), lambda bi, qi: (bi, 0, 0)),
                  pl.BlockSpec((1, past, LANES), lambda bi, qi: (bi, 0, 0)),
                  pl.BlockSpec((1, 1), lambda bi, qi: (0, 0))],
        out_specs=pl.BlockSpec((tq, WIDTH_A), lambda bi, qi: (bi * nq + qi, 0)),
        scratch_shapes=[pltpu.VMEM((N_KV_A, seq, LANES), BF16),
                        pltpu.VMEM((N_KV_A, past, LANES), BF16),
                        pltpu.VMEM((N_KV_A, HEAD_DIM + ONES_ROWS, seq), BF16),
                        pltpu.VMEM((N_KV_A, HEAD_DIM + ONES_ROWS, past), BF16),
                        pltpu.VMEM((N_KV_A, 8, LANES), F32)],
        compiler_params=_cparams(("arbitrary", "arbitrary")),
        name="latent_gqa",
    )(proj, proj, proj, ctx_k, ctx_v, qmax)


def _na_kernel(q_ref, k0, k1, k2, k3, v0, v1, v2, v3, ck_ref, cv_ref, bias_ref, bound_ref, o_ref):
    q = q_ref[...]
    lane_q = lax.broadcasted_iota(jnp.int32, q.shape, 1)
    lane_k = lax.broadcasted_iota(jnp.int32, (1, LANES), 1)
    ks = [r[...] for r in (k0, k1, k2, k3)] + [ck_ref[0]]
    values_t = [jnp.concatenate([r.astype(F32).T, jnp.ones((ONES_ROWS, r.shape[0]), F32)], axis=0).astype(BF16)
                for r in [ref[...] for ref in (v0, v1, v2, v3)] + [cv_ref[0]]]

    heads = []
    for half in range(2):
        ckf = _keep_half(ks[-1], half).astype(F32)
        ctx_norm = jnp.sqrt(jnp.max(jnp.sum(ckf * ckf, axis=1, keepdims=True), axis=0, keepdims=True))
        consts = bound_ref[0, half:half + 1, :]
        kmax = jnp.maximum(ctx_norm, consts[:, 2:3])
        shift = _round_up_bf16(consts[:, 0:1] * kmax + consts[:, 1:2])
        one = jnp.ones((), BF16)
        keys = [jnp.where(lane_k == _ones_lane(half), one, _keep_half(k, half)) for k in ks]
        biases = [bias_ref[0, half, j * NA_KBLK:(j + 1) * NA_KBLK, :] for j in range(4)] + [None]
        heads.append((shift, keys, biases))
    safe = jnp.max(jnp.maximum(heads[0][0], heads[1][0])) <= ATTN_SAFE_SHIFT

    def attend(probabilities):
        outs = []
        for half, ps in enumerate(probabilities):
            o = functools.reduce(lambda a, b: a + b, [_dot(vt, p) for vt, p in zip(values_t, ps)])
            outs.append(o[half * HEAD_DIM:(half + 1) * HEAD_DIM] / o[2 * HEAD_DIM:2 * HEAD_DIM + 1])
        o_ref[...] = jnp.concatenate(outs, axis=0).T

    def with_bound():
        probabilities = []
        for half, (shift, keys, biases) in enumerate(heads):
            qa = jnp.where(lane_q == _ones_lane(half), (-shift).astype(BF16), _keep_half(q, half))
            probabilities.append([jnp.exp(s).astype(BF16) for s in _scores(qa, keys, biases)])
        attend(probabilities)

    def with_row_max():
        all_scores = [_scores(_keep_half(q, half), keys, biases) for half, (_, keys, biases) in enumerate(heads)]
        probabilities = []
        for scores in all_scores:
            m = functools.reduce(jnp.maximum, [jnp.max(s, axis=0, keepdims=True) for s in scores])
            probabilities.append([jnp.exp(s - m).astype(BF16) for s in scores])
        attend(probabilities)

    pl.when(safe)(with_bound)
    pl.when(jnp.logical_not(safe))(with_row_max)


def _na_first_key_block(i, rows):
    per_qblock = NA_QROWS * GRID_W // NA_KBLK
    lead = (NA_KH // 2) * GRID_W // NA_KBLK
    return jnp.clip(per_qblock * i - lead, 0, (rows - NA_KROWS) * GRID_W // NA_KBLK)


def _latent_neighbourhood(proj, ctx_k, ctx_v, bias_t, bounds, *, seq):
    t = proj.shape[0]
    b = t // seq
    rows = seq // GRID_W
    nblk = rows // NA_QROWS
    kblk_per_batch = seq // NA_KBLK
    past = ctx_k.shape[1]
    grid = (N_HEADS_B // 2, nblk, b)

    def kv_spec(col0, j):
        return pl.BlockSpec(
            (NA_KBLK, LANES),
            lambda hp, i, bi: (bi * kblk_per_batch + _na_first_key_block(i, rows) + j, col0 // LANES + hp))

    def variant(i):
        return jnp.where(i == 0, 0, jnp.where(i == nblk - 1, 2, 1))

    in_specs = ([pl.BlockSpec((NA_TQ, LANES), lambda hp, i, bi: (bi * nblk + i, COL_QB // LANES + hp))]
                + [kv_spec(COL_KB, j) for j in range(4)]
                + [kv_spec(COL_VB, j) for j in range(4)]
                + [pl.BlockSpec((1, past, LANES), lambda hp, i, bi: (bi, 0, hp)),
                   pl.BlockSpec((1, past, LANES), lambda hp, i, bi: (bi, 0, hp)),
                   pl.BlockSpec((1, 2, NA_TK, NA_TQ), lambda hp, i, bi: (variant(i), hp, 0, 0)),
                   pl.BlockSpec((1, 2, LANES), lambda hp, i, bi: (hp, 0, 0))])
    return pl.pallas_call(
        _na_kernel,
        out_shape=jax.ShapeDtypeStruct((t, WIDTH_B), F32),
        grid=grid,
        in_specs=in_specs,
        out_specs=pl.BlockSpec((NA_TQ, LANES), lambda hp, i, bi: (bi * nblk + i, hp)),
        compiler_params=_cparams(("arbitrary", "arbitrary", "arbitrary")),
        name="latent_neighbourhood",
    )(proj, *([proj] * 8), ctx_k, ctx_v, bias_t, bounds)


def _neighbourhood_bounds(qn_b, kn_b, rpb):
    n_heads = rpb.shape[0]
    qmax = jnp.broadcast_to(_query_norm_bound(qn_b), (n_heads, 1))
    kmax = jnp.broadcast_to(_query_norm_bound(kn_b) * (HEAD_DIM ** 0.5), (n_heads, 1))
    bmax = jnp.maximum(jnp.max(rpb.reshape(n_heads, -1), axis=1, keepdims=True), 0.0).astype(F32)
    table = jnp.concatenate([qmax, bmax, kmax, jnp.zeros((n_heads, LANES - 3), F32)], axis=1)
    return table.reshape(n_heads // 2, 2, LANES)


def _neighbourhood_bias(rpb, rows):
    nblk = rows // NA_QROWS
    n_dr = 2 * NA_KH - 1
    n_dc = 2 * NA_KW - 1
    kc = np.arange(GRID_W)[:, None]
    qc = np.arange(GRID_W)[None, :]
    ws = np.clip(qc - NA_KW // 2, 0, GRID_W - NA_KW)
    col_ok = (kc >= ws) & (kc < ws + NA_KW)
    dc = np.clip(kc - qc + NA_KW - 1, 0, n_dc - 1)
    dc_onehot = (dc[None] == np.arange(n_dc)[:, None, None]).astype(np.float32)
    tiles = jnp.einsum('hab,bkq->hakq', rpb.astype(F32), jnp.asarray(dc_onehot),
                       precision=lax.Precision.HIGHEST)
    tiles = jnp.where(jnp.asarray(col_ok)[None, None], tiles, MASKED)
    masked_tile = jnp.full((rpb.shape[0], 1, GRID_W, GRID_W), MASKED, F32)
    tiles = jnp.concatenate([tiles, masked_tile], axis=1)
    pick = np.zeros((3, NA_KROWS, NA_QROWS, n_dr + 1), np.float32)
    for v, i in enumerate((0, 1, nblk - 1)):
        r0 = i * NA_QROWS
        ks = int(np.clip(r0 - NA_KH // 2, 0, rows - NA_KROWS))
        for kl in range(NA_KROWS):
            for ql in range(NA_QROWS):
                kr, qr = ks + kl, r0 + ql
                rs = int(np.clip(qr - NA_KH // 2, 0, rows - NA_KH))
                ok = rs <= kr < rs + NA_KH
                pick[v, kl, ql, (kr - qr + NA_KH - 1) if ok else n_dr] = 1.0
    bias = jnp.einsum('vkqa,hacd->vhkcqd', jnp.asarray(pick), tiles, precision=lax.Precision.HIGHEST)
    return bias.reshape(3, rpb.shape[0], NA_TK, NA_TQ)


def _merge_kernel(xp_ref, oap_ref, obp_ref, xs_ref, oas_ref, obs_ref, ona_ref, onb_ref, wo_ref,
                  g1_ref, sh2_ref, sc2_ref, n2_ref, y_ref, h_ref, hp_ref, *, ctx_tiles):
    def one_stream(x_ref, oa_ref, ob_ref):
        na = (_rms(oa_ref[...]) * ona_ref[...]).astype(BF16)
        nb = (_rms(ob_ref[...]) * onb_ref[...]).astype(BF16)
        mix = _dot(na, wo_ref[0:WIDTH_A, :]) + _dot(nb, wo_ref[WIDTH_A:WIDTH_A + WIDTH_B, :])
        y = x_ref[...] + g1_ref[0] * mix
        y_ref[...] = y
        h = _rms(y) * n2_ref[...]
        h = h * (1.0 + sc2_ref[0]) + sh2_ref[0]
        h_ref[...] = h
        hp_ref[...] = _pack_rows(h)

    i = pl.program_id(0)
    pl.when(i < ctx_tiles)(lambda: one_stream(xp_ref, oap_ref, obp_ref))
    pl.when(i >= ctx_tiles)(lambda: one_stream(xs_ref, oas_ref, obs_ref))


def _merge(ctx, lat, on_a, on_b, w_out_bf, gate1, shift2, scale2, norm2, *, tm, lat_seq):
    t_c, d = ctx[0].shape
    t_l = lat[0].shape[0]
    ctx_tiles = t_c // tm
    lat_tiles_per_batch = lat_seq // tm

    def ctx_map(i):
        return (jnp.minimum(i, ctx_tiles - 1), 0)

    def lat_map(i):
        return (jnp.maximum(i - ctx_tiles, 0), 0)

    def mod_map(i):
        return (jnp.where(i < ctx_tiles, 0, 1 + (i - ctx_tiles) // lat_tiles_per_batch), 0, 0)

    def stream_specs(index_map):
        return [pl.BlockSpec((tm, d), index_map),
                pl.BlockSpec((tm, WIDTH_A), index_map),
                pl.BlockSpec((tm, WIDTH_B), index_map)]

    return pl.pallas_call(
        functools.partial(_merge_kernel, ctx_tiles=ctx_tiles),
        out_shape=[jax.ShapeDtypeStruct((t_c + t_l, d), F32)] * 2
        + [jax.ShapeDtypeStruct((t_c + t_l, d // 2), jnp.int32)],
        grid=((t_c + t_l) // tm,),
        in_specs=stream_specs(ctx_map) + stream_specs(lat_map) + [
            pl.BlockSpec((1, WIDTH_A), lambda i: (0, 0)),
            pl.BlockSpec((1, WIDTH_B), lambda i: (0, 0)),
            pl.BlockSpec((WIDTH_A + WIDTH_B, d), lambda i: (0, 0)),
            pl.BlockSpec((1, 1, d), mod_map),
            pl.BlockSpec((1, 1, d), mod_map),
            pl.BlockSpec((1, 1, d), mod_map),
            pl.BlockSpec((1, d), lambda i: (0, 0))],
        out_specs=[pl.BlockSpec((tm, d), lambda i: (i, 0))] * 2 + [pl.BlockSpec((tm, d // 2), lambda i: (i, 0))],
        compiler_params=_cparams(("arbitrary",)),
        name="merge_out_proj",
    )(*ctx, *lat, on_a, on_b, w_out_bf, gate1, shift2, scale2, norm2)


def _first_index_of_max(x, iota):
    mx = jnp.max(x, axis=0, keepdims=True)
    idx = jnp.min(jnp.where(x == mx, iota, float(x.shape[0])), axis=0, keepdims=True)
    return mx, iota == idx


def _router_gates(h, wr_hi, wr_lo, rbias):
    h_hi = h.astype(BF16)
    h_lo = (h - h_hi.astype(F32)).astype(BF16)
    logits = _dot_nt(wr_hi, h_hi) + (_dot_nt(wr_lo, h_hi) + _dot_nt(wr_hi, h_lo))
    scores = _sigmoid(logits)
    sel = scores + rbias
    tm = sel.shape[1]
    iota_g = lax.broadcasted_iota(jnp.int32, (GROUP_SIZE, tm), 0).astype(F32)
    group_scores = []
    for g in range(N_GROUPS):
        grp = sel[g * GROUP_SIZE:(g + 1) * GROUP_SIZE]
        m1, first = _first_index_of_max(grp, iota_g)
        m2 = jnp.max(jnp.where(first, -jnp.inf, grp), axis=0, keepdims=True)
        group_scores.append(m1 + m2)
    gs = jnp.concatenate(group_scores, axis=0)
    iota_n = lax.broadcasted_iota(jnp.int32, (N_GROUPS, tm), 0).astype(F32)
    group_on = jnp.zeros((N_GROUPS, tm), F32)
    for _ in range(TOPK_GROUPS):
        _, pick = _first_index_of_max(gs, iota_n)
        group_on = jnp.where(pick, 1.0, group_on)
        gs = jnp.where(pick, -jnp.inf, gs)
    expert_on = jnp.concatenate(
        [jnp.broadcast_to(group_on[g:g + 1], (GROUP_SIZE, tm)) for g in range(N_GROUPS)], axis=0)
    cand = jnp.where(expert_on > 0.0, sel, -jnp.inf)
    iota_e = lax.broadcasted_iota(jnp.int32, (N_EXPERTS, tm), 0).astype(F32)
    w = jnp.zeros((N_EXPERTS, tm), F32)
    chosen = jnp.zeros((N_EXPERTS, tm), F32)
    for _ in range(TOP_K):
        _, pick = _first_index_of_max(cand, iota_e)
        w = jnp.where(pick, scores, w)
        chosen = jnp.where(pick, 1.0, chosen)
        cand = jnp.where(pick, -jnp.inf, cand)
    return w / jnp.sum(w, axis=0, keepdims=True) * ROUTED_SCALE, chosen


MOE_TS = 1024
MOE_ROUTE_TM = 512
MOE_ROW_TM = 512


def _route_kernel(h_ref, wrh_ref, wrl_ref, rb_ref, tri_ref, gates_ref, rank_ref, count_ref):
    @pl.when(pl.program_id(0) == 0)
    def _():
        count_ref[...] = jnp.zeros_like(count_ref)

    gates, chosen = _router_gates(h_ref[...], wrh_ref[...], wrl_ref[...], rb_ref[...])
    gates_ref[...] = gates
    before = _dot(chosen.astype(BF16), tri_ref[...])
    seen = count_ref[...]
    rank_ref[...] = jnp.where(chosen > 0.0, before + seen[:, 0:1], -1.0)
    count_ref[...] = seen + jnp.sum(chosen, axis=1, keepdims=True)


def _route(h_all, wr_hi, wr_lo, rbias):
    t, d = h_all.shape
    tm = MOE_ROUTE_TM
    tri = jnp.asarray(np.triu(np.ones((tm, tm), np.float32), k=1), BF16)
    return pl.pallas_call(
        _route_kernel,
        out_shape=[jax.ShapeDtypeStruct((N_EXPERTS, t), F32),
                   jax.ShapeDtypeStruct((N_EXPERTS, t), F32),
                   jax.ShapeDtypeStruct((N_EXPERTS, LANES), F32)],
        grid=(t // tm,),
        in_specs=[pl.BlockSpec((tm, d), lambda i: (i, 0)),
                  pl.BlockSpec((N_EXPERTS, d), lambda i: (0, 0)),
                  pl.BlockSpec((N_EXPERTS, d), lambda i: (0, 0)),
                  pl.BlockSpec((N_EXPERTS, 1), lambda i: (0, 0)),
                  pl.BlockSpec((tm, tm), lambda i: (0, 0))],
        out_specs=[pl.BlockSpec((N_EXPERTS, tm), lambda i: (0, i)),
                   pl.BlockSpec((N_EXPERTS, tm), lambda i: (0, i)),
                   pl.BlockSpec((N_EXPERTS, LANES), lambda i: (0, 0))],
        compiler_params=_cparams(("arbitrary",)),
        name="moe_route",
    )(h_all, wr_hi, wr_lo, rbias, tri)


def _slots_kernel(gates_ref, rank_ref, off_ref, pos_ref, gtok_ref):
    gates = gates_ref[...]
    rank = rank_ref[...]
    tm = gates.shape[1]
    slot = off_ref[...] + rank
    left = jnp.where(rank >= 0.0, 1.0, 0.0)
    iota_e = lax.broadcasted_iota(jnp.int32, (N_EXPERTS, tm), 0).astype(F32)
    pos_rows, gate_rows = [], []
    for _ in range(TOP_K):
        _, pick = _first_index_of_max(left, iota_e)
        pos_rows.append(jnp.sum(jnp.where(pick, slot, 0.0), axis=0, keepdims=True))
        gate_rows.append(jnp.sum(jnp.where(pick, gates, 0.0), axis=0, keepdims=True))
        left = jnp.where(pick, 0.0, left)
    pos_ref[...] = jnp.concatenate(pos_rows, axis=0).astype(jnp.int32)
    pad = jnp.zeros((LANES - TOP_K, tm), F32)
    gtok_ref[...] = jnp.concatenate(gate_rows + [pad], axis=0).T


def _slots(gates_t, rank_t, off):
    t = gates_t.shape[1]
    tm = MOE_ROUTE_TM
    return pl.pallas_call(
        _slots_kernel,
        out_shape=[jax.ShapeDtypeStruct((TOP_K, t), jnp.int32), jax.ShapeDtypeStruct((t, LANES), F32)],
        grid=(t // tm,),
        in_specs=[pl.BlockSpec((N_EXPERTS, tm), lambda i: (0, i)),
                  pl.BlockSpec((N_EXPERTS, tm), lambda i: (0, i)),
                  pl.BlockSpec((N_EXPERTS, 1), lambda i: (0, 0))],
        out_specs=[pl.BlockSpec((TOP_K, tm), lambda i: (0, i)),
                   pl.BlockSpec((tm, LANES), lambda i: (i, 0))],
        compiler_params=_cparams(("arbitrary",)),
        name="moe_slots",
    )(gates_t, rank_t, off)


SC_CORES = 2
SC_SUBCORES = 16
SC_ROWS = 64


def _dispatch(hp_all, slot_of):
    t, width = hp_all.shape
    n_slots = slot_of.shape[0]
    n_pad = n_slots - TOP_K * t
    workers = SC_CORES * SC_SUBCORES
    per_worker = t // workers
    pad_per_worker = n_pad // workers
    assert per_worker * workers == t and per_worker % SC_ROWS == 0
    assert pad_per_worker * workers == n_pad and pad_per_worker % SC_ROWS == 0
    mesh = plsc.VectorSubcoreMesh(core_axis_name="core", subcore_axis_name="subcore")

    @functools.partial(
        pl.kernel, mesh=mesh,
        out_type=jax.ShapeDtypeStruct((n_slots, width), jnp.int32),
        scratch_types=[pltpu.VMEM((SC_ROWS,), jnp.int32),
                       pltpu.VMEM((SC_ROWS, width), jnp.int32),
                       pltpu.SemaphoreType.DMA],
    )
    def scatter_rows(h_hbm, slot_hbm, out_hbm, idx_v, rows_v, sem):
        worker = lax.axis_index("subcore") * SC_CORES + lax.axis_index("core")
        base = worker * per_worker

        @pl.loop(0, per_worker // SC_ROWS)
        def _(j):
            first = base + j * SC_ROWS
            pltpu.sync_copy(h_hbm.at[pl.ds(first, SC_ROWS)], rows_v)
            for k in range(TOP_K):
                pltpu.sync_copy(slot_hbm.at[pl.ds(k * t + first, SC_ROWS)], idx_v)
                pltpu.async_copy(rows_v, out_hbm.at[idx_v], sem).wait()

        pltpu.sync_copy(h_hbm.at[pl.ds(0, SC_ROWS)], rows_v)
        pad_base = TOP_K * t + worker * pad_per_worker

        @pl.loop(0, pad_per_worker // SC_ROWS)
        def _(j):
            pltpu.sync_copy(slot_hbm.at[pl.ds(pad_base + j * SC_ROWS, SC_ROWS)], idx_v)
            pltpu.async_copy(rows_v, out_hbm.at[idx_v], sem).wait()

    return scatter_rows(hp_all, slot_of)


def _experts_kernel(te_ref, xs_ref, wg_ref, wu_ref, wd_ref, ys_ref, wg_bf, wu_bf, wd_bf):
    i = pl.program_id(0)

    @pl.when((i == 0) | (te_ref[i] != te_ref[jnp.maximum(i, 1) - 1]))
    def _():
        wg_bf[...] = wg_ref[0].astype(BF16)
        wu_bf[...] = wu_ref[0].astype(BF16)
        wd_bf[...] = wd_ref[0].astype(BF16)

    left, right = _unpack_rows(xs_ref[...])
    x = jnp.concatenate([left, right], axis=1).astype(BF16)
    g = _dot(x, wg_bf[...])
    u = _dot(x, wu_bf[...])
    act = (g * _sigmoid(g)) * u
    ys_ref[...] = _pack_rows(_dot(act.astype(BF16), wd_bf[...]))


def _experts(xs, tile_expert, w_gate, w_up, w_down):
    n_slots, width = xs.shape
    d = 2 * width
    ts = MOE_TS
    return pl.pallas_call(
        _experts_kernel,
        out_shape=jax.ShapeDtypeStruct((n_slots, width), jnp.int32),
        grid_spec=pltpu.PrefetchScalarGridSpec(
            num_scalar_prefetch=1,
            grid=(n_slots // ts,),
            in_specs=[pl.BlockSpec((ts, width), lambda i, te: (i, 0)),
                      pl.BlockSpec((1, d, D_EXPERT), lambda i, te: (te[i], 0, 0)),
                      pl.BlockSpec((1, d, D_EXPERT), lambda i, te: (te[i], 0, 0)),
                      pl.BlockSpec((1, D_EXPERT, d), lambda i, te: (te[i], 0, 0))],
            out_specs=pl.BlockSpec((ts, width), lambda i, te: (i, 0)),
            scratch_shapes=[pltpu.VMEM((d, D_EXPERT), BF16), pltpu.VMEM((d, D_EXPERT), BF16),
                            pltpu.VMEM((D_EXPERT, d), BF16)]),
        compiler_params=_cparams(("arbitrary",)),
        name="moe_experts",
    )(tile_expert, xs, w_gate, w_up, w_down)


def _gather_slots(y_slots, slot_of, t):
    width = y_slots.shape[1]
    workers = SC_CORES * SC_SUBCORES
    per_worker = t // workers
    n_blocks = (per_worker // SC_ROWS) * TOP_K
    assert per_worker * workers == t and per_worker % SC_ROWS == 0 and n_blocks % 2 == 0
    mesh = plsc.VectorSubcoreMesh(core_axis_name="core", subcore_axis_name="subcore")

    @functools.partial(
        pl.kernel, mesh=mesh,
        out_type=jax.ShapeDtypeStruct((TOP_K * t, width), jnp.int32),
        scratch_types=[pltpu.VMEM((SC_ROWS,), jnp.int32), pltpu.VMEM((SC_ROWS,), jnp.int32),
                       pltpu.VMEM((SC_ROWS, width), jnp.int32), pltpu.VMEM((SC_ROWS, width), jnp.int32),
                       pltpu.SemaphoreType.DMA, pltpu.SemaphoreType.DMA],
    )
    def gather_rows(ys_hbm, slot_hbm, out_hbm, idx0, idx1, rows0, rows1, sem0, sem1):
        worker = lax.axis_index("subcore") * SC_CORES + lax.axis_index("core")
        base = worker * per_worker

        def first_row(n):
            return (n % TOP_K) * t + base + (n // TOP_K) * SC_ROWS

        def start(n, idx_v, rows_v, sem):
            pltpu.sync_copy(slot_hbm.at[pl.ds(first_row(n), SC_ROWS)], idx_v)
            pltpu.async_copy(ys_hbm.at[idx_v], rows_v, sem)

        def finish(n, idx_v, rows_v, sem):
            pltpu.make_async_copy(ys_hbm.at[idx_v], rows_v, sem).wait()
            pltpu.sync_copy(rows_v, out_hbm.at[pl.ds(first_row(n), SC_ROWS)])

        start(0, idx0, rows0, sem0)

        @pl.loop(0, n_blocks, step=2)
        def _(n):
            start(n + 1, idx1, rows1, sem1)
            finish(n, idx0, rows0, sem0)

            @pl.when(n + 2 < n_blocks)
            def _():
                start(n + 2, idx0, rows0, sem0)

            finish(n + 1, idx1, rows1, sem1)

    return gather_rows(y_slots, slot_of)


def _combine_kernel(y_ref, h_ref, g2_ref, gtok_ref, rows_ref, wgs_ref, wus_ref, wds_ref, o_ref):
    h_left, h_right = _unpack_rows(h_ref[...])
    h = jnp.concatenate([h_left, h_right], axis=1).astype(BF16)
    gs = _dot(h, wgs_ref[...])
    us = _dot(h, wus_ref[...])
    shared = _dot(((gs * _sigmoid(gs)) * us).astype(BF16), wds_ref[...])

    gtok = gtok_ref[...]
    acc_left = acc_right = None
    for k in range(TOP_K):
        left, right = _unpack_rows(rows_ref[k])
        gate = gtok[:, k:k + 1]
        acc_left = gate * left if acc_left is None else acc_left + gate * left
        acc_right = gate * right if acc_right is None else acc_right + gate * right
    routed = jnp.concatenate([acc_left, acc_right], axis=1)
    o_ref[...] = y_ref[...] + g2_ref[0] * (routed + shared)


def _combine(y_all, hp_all, gate2, gtok, rows, wgs, wus, wds, *, first_token, tokens, seq):
    d = y_all.shape[1]
    width = hp_all.shape[1]
    tm = MOE_ROW_TM
    tile0 = first_token // tm
    nb = gate2.shape[0]
    tiles_per_batch = seq // tm

    def mod_map(i):
        return ((i // tiles_per_batch) if nb > 1 else 0, 0, 0)

    return pl.pallas_call(
        _combine_kernel,
        out_shape=jax.ShapeDtypeStruct((tokens, d), F32),
        grid=(tokens // tm,),
        in_specs=[pl.BlockSpec((tm, d), lambda i: (tile0 + i, 0)),
                  pl.BlockSpec((tm, width), lambda i: (tile0 + i, 0)),
                  pl.BlockSpec((1, 1, d), mod_map),
                  pl.BlockSpec((tm, LANES), lambda i: (tile0 + i, 0)),
                  pl.BlockSpec((TOP_K, tm, width), lambda i: (0, tile0 + i, 0)),
                  pl.BlockSpec((d, D_SHARED), lambda i: (0, 0)),
                  pl.BlockSpec((d, D_SHARED), lambda i: (0, 0)),
                  pl.BlockSpec((D_SHARED, d), lambda i: (0, 0))],
        out_specs=pl.BlockSpec((tm, d), lambda i: (i, 0)),
        compiler_params=_cparams(("arbitrary",)),
        name="moe_combine",
    )(y_all, hp_all, gate2, gtok, rows, wgs, wus, wds)


def _expert_layout(counts, n_tiles):
    cnt = counts.astype(jnp.int32)
    tiles = (cnt + (MOE_TS - 1)) // MOE_TS
    last_tile = jnp.cumsum(tiles)
    off = (last_tile - tiles) * MOE_TS
    pad_lo = off + cnt
    pad_hi = (off + tiles * MOE_TS).at[N_EXPERTS - 1].set(n_tiles * MOE_TS)
    pad_cnt = pad_hi - pad_lo
    pad_last = jnp.cumsum(pad_cnt)
    shift = pad_lo - (pad_last - pad_cnt)
    j = jnp.arange(N_EXPERTS * MOE_TS, dtype=jnp.int32)
    past = (pad_last[None, :-1] <= j[:, None]).astype(jnp.int32)
    pad_slots = j + shift[0] + jnp.sum(past * (shift[1:] - shift[:-1])[None, :], axis=1)
    tile_ids = jnp.arange(n_tiles, dtype=jnp.int32)
    tile_expert = jnp.minimum(
        jnp.sum((last_tile[None, :] <= tile_ids[:, None]).astype(jnp.int32), axis=1), N_EXPERTS - 1)
    return off, pad_slots, tile_expert


def _rope_tables(n_tokens):
    t = jnp.arange(n_tokens)
    row = (t // GRID_W).astype(F32)
    col = (t % GRID_W).astype(F32)
    nf = HEAD_DIM // 4
    freqs = ROPE_THETA ** (-jnp.arange(nf, dtype=F32) / nf)
    ang_r = row[:, None] * freqs
    ang_c = col[:, None] * freqs
    cos = jnp.concatenate([jnp.cos(ang_r)] * 2 + [jnp.cos(ang_c)] * 2, axis=1)
    sin = jnp.concatenate([-jnp.sin(ang_r), jnp.sin(ang_r), -jnp.sin(ang_c), jnp.sin(ang_c)], axis=1)
    reps = LANES // HEAD_DIM
    return jnp.tile(cos, (1, reps)), jnp.tile(sin, (1, reps))


def _head_gains(qn_a, kn_a, qn_b, kn_b):
    ones = jnp.ones((HEAD_DIM,), F32)
    parts = ([qn_a] * N_HEADS_A + [kn_a] * N_KV_A + [ones] * N_KV_A
             + [qn_b] * N_HEADS_B + [kn_b] * N_HEADS_B + [ones] * N_HEADS_B)
    return jnp.concatenate(parts).reshape(1, IN_COLS).astype(F32)


def _same_head_indicator():
    i = np.arange(LANES)
    return jnp.asarray((i[:, None] // HEAD_DIM) == (i[None, :] // HEAD_DIM), BF16)


def _token_major(cache):
    b, h, s, hd = cache.shape
    return cache.transpose(0, 2, 1, 3).reshape(b, s, h * hd).astype(BF16)


def kernel(x_prompt, x_sample, cache_k_a, cache_v_a, cache_k_b, cache_v_b, c, c_ctx, w_mod, b_mod, norm1, norm2, w_in, qn_a, kn_a, qn_b, kn_b, rpb, on_a, on_b, w_out, w_router, router_bias, w_gate_e, w_up_e, w_down_e, w_gate_s, w_up_s, w_down_s):
    depth = w_mod.shape[0]
    assert depth == 1
    l = 0
    bp, sp, d = x_prompt.shape
    bs, ss, _ = x_sample.shape

    cvec = jnp.concatenate([c_ctx[None, :], c], axis=0)
    rows = -(-cvec.shape[0] // 8) * 8
    cvec = jnp.pad(cvec, ((0, rows - cvec.shape[0]), (0, 0)))
    mod = _adaln(cvec, w_mod[l], b_mod[l])
    mod_p = [m.reshape(1, 1, d) for m in jnp.split(mod[0:1], 6, axis=-1)]
    mod_s = [m.reshape(bs, 1, d) for m in jnp.split(mod[1:1 + bs], 6, axis=-1)]
    mod_all = [m.reshape(1 + bs, 1, d) for m in jnp.split(mod[0:1 + bs], 6, axis=-1)]

    w_in_bf = w_in[l].astype(BF16)
    w_out_bf = w_out[l].astype(BF16)
    gain = _head_gains(qn_a[l], kn_a[l], qn_b[l], kn_b[l])
    seg = _same_head_indicator()
    n1 = norm1[l].reshape(1, d)
    n2 = norm2[l].reshape(1, d)
    ona = on_a[l].reshape(1, WIDTH_A)
    onb = on_b[l].reshape(1, WIDTH_B)
    wr_t = w_router[l].T
    wr_hi = wr_t.astype(BF16)
    wr_lo = (wr_t - wr_hi.astype(F32)).astype(BF16)
    rbias = router_bias[l].reshape(N_EXPERTS, 1).astype(F32)
    wgs = w_gate_s[l].astype(BF16)
    wus = w_up_s[l].astype(BF16)
    wds = w_down_s[l].astype(BF16)
    t_p = bp * sp
    t_s = bs * ss
    t_all = t_p + t_s

    xp = x_prompt.reshape(t_p, d)
    proj_p, st_ka, st_va, st_kb, st_vb = _project(
        xp, mod_p[0], mod_p[1], n1, w_in_bf, gain, seg, None, tm=sp, seq=sp, states=True)
    oa_p, ob_p = _context_attention(proj_p, seq=sp)

    xs = x_sample.reshape(t_s, d)
    proj_s, = _project(xs, mod_s[0], mod_s[1], n1, w_in_bf, gain, seg, _rope_tables(ss),
                       tm=512, seq=ss, states=False)
    oa_s = _latent_gqa(proj_s, _token_major(cache_k_a[:, l]), _token_major(cache_v_a[:, l]),
                       _query_norm_bound(qn_a[l]), seq=ss, tq=128)
    bias_t = _neighbourhood_bias(rpb[l], ss // GRID_W)
    ob_s = _latent_neighbourhood(proj_s, _token_major(cache_k_b[:, l]), _token_major(cache_v_b[:, l]),
                                 bias_t, _neighbourhood_bounds(qn_b[l], kn_b[l], rpb[l]), seq=ss)

    y1_all, h_all, hp_all = _merge((xp, oa_p, ob_p), (xs, oa_s, ob_s), ona, onb, w_out_bf,
                                   mod_all[2], mod_all[3], mod_all[4], n2, tm=512, lat_seq=ss)
    gates_t, rank_t, counts = _route(h_all, wr_hi, wr_lo, rbias)
    n_tiles = t_all * TOP_K // MOE_TS + N_EXPERTS
    off, pad_slots, tile_expert = _expert_layout(counts[:, 0], n_tiles)
    pos, gtok = _slots(gates_t, rank_t, off.astype(F32).reshape(N_EXPERTS, 1))
    slot_of = pos.reshape(TOP_K * t_all)
    x_slots = _dispatch(hp_all, jnp.concatenate([slot_of, pad_slots]))
    y_slots = _experts(x_slots, tile_expert, w_gate_e[l], w_up_e[l], w_down_e[l])
    rows = _gather_slots(y_slots, slot_of, t_all).reshape(TOP_K, t_all, d // 2)
    y_p = _combine(y1_all, hp_all, mod_p[5], gtok, rows, wgs, wus, wds,
                   first_token=0, tokens=t_p, seq=sp)
    y_s = _combine(y1_all, hp_all, mod_s[5], gtok, rows, wgs, wus, wds,
                   first_token=t_p, tokens=t_s, seq=ss)

    return (y_p.reshape(bp, sp, d), y_s.reshape(bs, ss, d), st_ka, st_va, st_kb, st_vb)
```

```python
import functools

import numpy as np
import jax
import jax.numpy as jnp
from jax import lax
from jax.experimental import pallas as pl
from jax.experimental.pallas import tpu as pltpu
from jax.experimental.pallas import tpu_sc as plsc

F32 = jnp.float32
BF16 = jnp.bfloat16

D_MODEL = 1024
HEAD_DIM = 64
N_HEADS_A = 8
N_KV_A = 2
GROUP_A = N_HEADS_A // N_KV_A
N_HEADS_B = 8
WIDTH_A = N_HEADS_A * HEAD_DIM
WIDTH_B = N_HEADS_B * HEAD_DIM
KV_WIDTH_A = N_KV_A * HEAD_DIM
IN_COLS = WIDTH_A + 2 * KV_WIDTH_A + 3 * WIDTH_B
GRID_W = 64
ROPE_THETA = 10000.0
NA_KH = 8
NA_KW = 16
N_EXPERTS = 64
N_GROUPS = 8
GROUP_SIZE = N_EXPERTS // N_GROUPS
TOPK_GROUPS = 4
TOP_K = 8
D_EXPERT = 256
D_SHARED = 256
ROUTED_SCALE = 2.5
EPS = 1e-6

LANES = 128
MXU_DIM = 256
MASKED = -1e30

COL_QA = 0
COL_KA = WIDTH_A
COL_VA = COL_KA + KV_WIDTH_A
COL_QB = COL_VA + KV_WIDTH_A
COL_KB = COL_QB + WIDTH_B
COL_VB = COL_KB + WIDTH_B

NA_QROWS = 8
NA_KROWS = 2 * NA_KH
NA_TQ = NA_QROWS * GRID_W
NA_TK = NA_KROWS * GRID_W
NA_KBLK = 256

VMEM_LIMIT = 56 * 1024 * 1024


def _cparams(sem):
    return pltpu.CompilerParams(dimension_semantics=sem, vmem_limit_bytes=VMEM_LIMIT)


def _dot(a, b):
    return jnp.dot(a, b, preferred_element_type=F32)


def _dot_nt(a, b):
    return lax.dot_general(a, b, (((1,), (1,)), ((), ())), preferred_element_type=F32)


def _sigmoid(x):
    return 1.0 / (1.0 + jnp.exp(-x))


def _rms(x):
    return x * lax.rsqrt(jnp.mean(x * x, axis=-1, keepdims=True) + EPS)


def _pack_rows(x):
    n = x.shape[1] // 2
    hi = lax.bitcast_convert_type(x[:, :n].astype(BF16).astype(F32), jnp.int32)
    lo = lax.bitcast_convert_type(x[:, n:].astype(BF16).astype(F32), jnp.int32)
    return hi | lax.shift_right_logical(lo, 16)


def _unpack_rows(w):
    left = lax.bitcast_convert_type(w & jnp.int32(-65536), F32)
    right = lax.bitcast_convert_type(lax.shift_left(w, 16), F32)
    return left, right


def _mod_kernel(c_ref, w_ref, b_ref, o_ref):
    c = c_ref[...]
    s = c * _sigmoid(c)
    o_ref[...] = jnp.dot(s, w_ref[...], preferred_element_type=F32,
                         precision=lax.Precision.HIGHEST) + b_ref[...]


def _adaln(cvec, w_mod, b_mod):
    rows, d = cvec.shape
    n = w_mod.shape[1]
    tn = 512
    return pl.pallas_call(
        _mod_kernel,
        out_shape=jax.ShapeDtypeStruct((rows, n), F32),
        grid=(n // tn,),
        in_specs=[pl.BlockSpec((rows, d), lambda j: (0, 0)),
                  pl.BlockSpec((d, tn), lambda j: (0, j)),
                  pl.BlockSpec((1, tn), lambda j: (0, j))],
        out_specs=pl.BlockSpec((rows, tn), lambda j: (0, j)),
        compiler_params=_cparams(("arbitrary",)),
        name="adaln_mod",
    )(cvec, w_mod, b_mod.reshape(1, n))


_PROJ_CHUNKS = (
    [(COL_QA + i * LANES, LANES, True, True) for i in range(WIDTH_A // LANES)]
    + [(COL_KA, LANES, True, True), (COL_VA, LANES, False, False)]
    + [(COL_QB + i * LANES, LANES, True, False) for i in range(WIDTH_B // LANES)]
    + [(COL_KB + i * LANES, LANES, True, False) for i in range(WIDTH_B // LANES)]
    + [(COL_VB + i * LANES, LANES, False, False) for i in range(WIDTH_B // LANES)]
)


def _proj_kernel(*refs, rope, states):
    x_ref, sh_ref, sc_ref, n1_ref, w_ref, gain_ref, seg_ref = refs[:7]
    pos = 7
    if rope:
        cos_ref, sin_ref = refs[pos:pos + 2]
        pos += 2
    out_ref = refs[pos]
    pos += 1
    if states:
        ka_ref, va_ref, kb_ref, vb_ref = refs[pos:pos + 4]

    x = x_ref[...]
    h = _rms(x) * n1_ref[...]
    h = h * (1.0 + sc_ref[0]) + sh_ref[0]
    p = _dot(h.astype(BF16), w_ref[...])
    seg = seg_ref[...]
    if rope:
        cos = cos_ref[...]
        sin = sin_ref[...]
        lane = lax.broadcasted_iota(jnp.int32, cos.shape, 1)
        first_half = (lane % (HEAD_DIM // 2)) < (HEAD_DIM // 4)

    for c0, w, normed, roped in _PROJ_CHUNKS:
        pc = p[:, c0:c0 + w]
        if normed:
            sq = pc * pc
            hi = sq.astype(BF16)
            lo = (sq - hi.astype(F32)).astype(BF16)
            ss = _dot(hi, seg) + _dot(lo, seg)
            pc = pc * lax.rsqrt(ss * (1.0 / HEAD_DIM) + EPS) * gain_ref[:, c0:c0 + w]
        if states:
            if c0 == COL_KA:
                for hh in range(N_KV_A):
                    ka_ref[0, 0, hh] = pc[:, hh * HEAD_DIM:(hh + 1) * HEAD_DIM]
            elif c0 == COL_VA:
                for hh in range(N_KV_A):
                    va_ref[0, 0, hh] = pc[:, hh * HEAD_DIM:(hh + 1) * HEAD_DIM]
            elif COL_KB <= c0 < COL_VB:
                base = (c0 - COL_KB) // HEAD_DIM
                for hh in range(LANES // HEAD_DIM):
                    kb_ref[0, 0, base + hh] = pc[:, hh * HEAD_DIM:(hh + 1) * HEAD_DIM]
            elif c0 >= COL_VB:
                base = (c0 - COL_VB) // HEAD_DIM
                for hh in range(LANES // HEAD_DIM):
                    vb_ref[0, 0, base + hh] = pc[:, hh * HEAD_DIM:(hh + 1) * HEAD_DIM]
        if rope and roped:
            partner = jnp.where(first_half,
                                pltpu.roll(pc, LANES - HEAD_DIM // 4, 1),
                                pltpu.roll(pc, HEAD_DIM // 4, 1))
            pc = pc * cos + partner * sin
        if c0 < COL_KA or COL_QB <= c0 < COL_KB:
            pc = pc * (HEAD_DIM ** -0.5)
        out_ref[:, c0:c0 + w] = pc.astype(BF16)


def _project(x2d, shift, scale, norm1, w_in_bf, gain, seg, rope_tabs, *, tm, seq, states):
    t, d = x2d.shape
    nb = shift.shape[0]
    tiles_per_batch = seq // tm
    rope = rope_tabs is not None

    def mod_map(i):
        return ((i // tiles_per_batch) if nb > 1 else 0, 0, 0)

    in_specs = [pl.BlockSpec((tm, d), lambda i: (i, 0)),
                pl.BlockSpec((1, 1, d), mod_map),
                pl.BlockSpec((1, 1, d), mod_map),
                pl.BlockSpec((1, d), lambda i: (0, 0)),
                pl.BlockSpec((d, IN_COLS), lambda i: (0, 0)),
                pl.BlockSpec((1, IN_COLS), lambda i: (0, 0)),
                pl.BlockSpec((LANES, LANES), lambda i: (0, 0))]
    args = [x2d, shift, scale, norm1, w_in_bf, gain, seg]
    if rope:
        in_specs += [pl.BlockSpec((tm, LANES), lambda i: (i % tiles_per_batch, 0))] * 2
        args += list(rope_tabs)
    out_shape = [jax.ShapeDtypeStruct((t, IN_COLS), BF16)]
    out_specs = [pl.BlockSpec((tm, IN_COLS), lambda i: (i, 0))]
    if states:
        assert tm == seq
        b = t // seq
        for nh in (N_KV_A, N_KV_A, N_HEADS_B, N_HEADS_B):
            out_shape.append(jax.ShapeDtypeStruct((b, 1, nh, seq, HEAD_DIM), F32))
            out_specs.append(pl.BlockSpec((1, 1, nh, seq, HEAD_DIM), lambda i: (i, 0, 0, 0, 0)))
    return pl.pallas_call(
        functools.partial(_proj_kernel, rope=rope, states=states),
        out_shape=out_shape,
        grid=(t // tm,),
        in_specs=in_specs,
        out_specs=out_specs,
        compiler_params=_cparams(("arbitrary",)),
        name="proj_states" if states else "proj_rope",
    )(*args)


def _lane_half(shape):
    return lax.broadcasted_iota(jnp.int32, shape, 1) // HEAD_DIM


def _keep_half(x, half):
    return jnp.where(_lane_half(x.shape) == half, x, jnp.zeros_like(x))


def _transpose_bf16(x):
    return x.astype(F32).T.astype(BF16)


def _attend(q, keys, values_t, biases):
    return _softmax_av(_scores(q, keys, biases), values_t)


def _scores(q, keys, biases):
    scores = []
    for k, b in zip(keys, biases):
        s = _dot_nt(k, q)
        if b is not None:
            s = s + b
        scores.append(s)
    return scores


def _softmax_av(scores, values_t):
    m = functools.reduce(jnp.maximum, [jnp.max(s, axis=0, keepdims=True) for s in scores])
    denom = None
    out = None
    for s, vt in zip(scores, values_t):
        p = jnp.exp(s - m)
        ps = jnp.sum(p, axis=0, keepdims=True)
        po = _dot(vt, p.astype(BF16))
        denom = ps if denom is None else denom + ps
        out = po if out is None else out + po
    return out / denom


def _swap_halves(q_bf16):
    return pltpu.roll(q_bf16.astype(F32), HEAD_DIM, 1).astype(BF16)


def _gqa_heads(q_of_pair, keys_by_group, values_t):
    outs = []
    for h in range(N_HEADS_A):
        g = h // GROUP_A
        q = q_of_pair(h // 2)
        if h % 2 != g:
            q = _swap_halves(q)
        o = _attend(q, keys_by_group[g], values_t, [None] * len(values_t))
        outs.append(o[g * HEAD_DIM:(g + 1) * HEAD_DIM])
    return jnp.concatenate(outs, axis=0)


def _ctx_attn_kernel(p_ref, oa_ref, ob_ref):
    ka = p_ref[:, COL_KA:COL_KA + LANES]
    va_t = [_transpose_bf16(p_ref[:, COL_VA:COL_VA + LANES])]
    keys_by_group = [[_keep_half(ka, g)] for g in range(N_KV_A)]
    oa = _gqa_heads(lambda i: p_ref[:, COL_QA + i * LANES:COL_QA + (i + 1) * LANES],
                    keys_by_group, va_t)
    oa_ref[...] = oa.T

    outs = []
    for i in range(N_HEADS_B // 2):
        q = p_ref[:, COL_QB + i * LANES:COL_QB + (i + 1) * LANES]
        k = p_ref[:, COL_KB + i * LANES:COL_KB + (i + 1) * LANES]
        vt = [_transpose_bf16(p_ref[:, COL_VB + i * LANES:COL_VB + (i + 1) * LANES])]
        for half in range(2):
            o = _attend(q, [_keep_half(k, half)], vt, [None])
            outs.append(o[half * HEAD_DIM:(half + 1) * HEAD_DIM])
    ob_ref[...] = jnp.concatenate(outs, axis=0).T


def _context_attention(proj, *, seq):
    t = proj.shape[0]
    return pl.pallas_call(
        _ctx_attn_kernel,
        out_shape=[jax.ShapeDtypeStruct((t, WIDTH_A), F32), jax.ShapeDtypeStruct((t, WIDTH_B), F32)],
        grid=(t // seq,),
        in_specs=[pl.BlockSpec((seq, IN_COLS), lambda i: (i, 0))],
        out_specs=[pl.BlockSpec((seq, WIDTH_A), lambda i: (i, 0)),
                   pl.BlockSpec((seq, WIDTH_B), lambda i: (i, 0))],
        compiler_params=_cparams(("arbitrary",)),
        name="context_attention",
    )(proj)


ATTN_SAFE_SHIFT = 40.0
ONES_ROWS = 16


def _round_up_bf16(x):
    return (x * (1.0 + 2.0 ** -6)).astype(BF16).astype(F32)


def _query_norm_bound(gain):
    return jnp.max(jnp.abs(gain)).reshape(1, 1).astype(F32)


def _ones_lane(g):
    return (1 - g) * HEAD_DIM


def _gqa_latent_kernel(q_ref, k_ref, v_ref, ck_ref, cv_ref, qmax_ref, o_ref,
                       kg_ref, ckg_ref, vt_ref, cvt_ref, shift_ref):
    lane_k = lax.broadcasted_iota(jnp.int32, (1, LANES), 1)

    @pl.when(pl.program_id(1) == 0)
    def _():
        k = k_ref[...]
        ck = ck_ref[0]
        vt = v_ref[...].astype(F32).T
        cvt = cv_ref[0].astype(F32).T
        for g in range(N_KV_A):
            kf = _keep_half(k, g).astype(F32)
            ckf = _keep_half(ck, g).astype(F32)
            ksq = jnp.maximum(jnp.max(jnp.sum(kf * kf, axis=1, keepdims=True), axis=0, keepdims=True),
                              jnp.max(jnp.sum(ckf * ckf, axis=1, keepdims=True), axis=0, keepdims=True))
            shift_ref[g] = jnp.broadcast_to(_round_up_bf16(qmax_ref[...] * jnp.sqrt(ksq)), shift_ref.shape[1:])
            kg_ref[g] = jnp.where(lane_k == _ones_lane(g), 1.0, kf).astype(BF16)
            ckg_ref[g] = jnp.where(lane_k == _ones_lane(g), 1.0, ckf).astype(BF16)
            rows = slice(g * HEAD_DIM, (g + 1) * HEAD_DIM)
            vt_ref[g] = jnp.concatenate([vt[rows], jnp.ones((ONES_ROWS, vt.shape[1]), F32)], axis=0).astype(BF16)
            cvt_ref[g] = jnp.concatenate([cvt[rows], jnp.ones((ONES_ROWS, cvt.shape[1]), F32)], axis=0).astype(BF16)

    tq = q_ref.shape[0]
    lane_q = lax.broadcasted_iota(jnp.int32, (GROUP_A * tq, LANES), 1)
    queries, shifts = [], []
    for g in range(N_KV_A):
        qs = []
        for j in range(GROUP_A):
            h = g * GROUP_A + j
            q = q_ref[:, (h // 2) * LANES:(h // 2 + 1) * LANES].astype(F32)
            qs.append(q if h % 2 == g else pltpu.roll(q, HEAD_DIM, 1))
        queries.append(jnp.where(lane_q // HEAD_DIM == g, jnp.concatenate(qs, axis=0), 0.0))
        shifts.append(shift_ref[g][0:1, 0:1])
    safe = jnp.max(jnp.maximum(shift_ref[0], shift_ref[1])) <= ATTN_SAFE_SHIFT

    def attend(probabilities):
        outs = []
        for g, (p_lat, p_ctx) in enumerate(probabilities):
            o = _dot(vt_ref[g], p_lat) + _dot(cvt_ref[g], p_ctx)
            o = o[:HEAD_DIM] / o[HEAD_DIM:HEAD_DIM + 1]
            outs += [o[:, j * tq:(j + 1) * tq] for j in range(GROUP_A)]
        o_ref[...] = jnp.concatenate(outs, axis=0).T

    def with_bound():
        probabilities = []
        for g in range(N_KV_A):
            qa = jnp.where(lane_q == _ones_lane(g), -shifts[g], queries[g]).astype(BF16)
            probabilities.append((jnp.exp(_dot_nt(kg_ref[g], qa)).astype(BF16),
                                  jnp.exp(_dot_nt(ckg_ref[g], qa)).astype(BF16)))
        attend(probabilities)

    def with_row_max():
        scores = []
        for g in range(N_KV_A):
            qa = queries[g].astype(BF16)
            scores.append((_dot_nt(kg_ref[g], qa), _dot_nt(ckg_ref[g], qa)))
        probabilities = []
        for s_lat, s_ctx in scores:
            m = jnp.maximum(jnp.max(s_lat, axis=0, keepdims=True), jnp.max(s_ctx, axis=0, keepdims=True))
            probabilities.append((jnp.exp(s_lat - m).astype(BF16), jnp.exp(s_ctx - m).astype(BF16)))
        attend(probabilities)

    pl.when(safe)(with_bound)
    pl.when(jnp.logical_not(safe))(with_row_max)


def _latent_gqa(proj, ctx_k, ctx_v, qmax, *, seq, tq):
    t = proj.shape[0]
    b = t // seq
    nq = seq // tq
    past = ctx_k.shape[1]
    return pl.pallas_call(
        _gqa_latent_kernel,
        out_shape=jax.ShapeDtypeStruct((t, WIDTH_A), F32),
        grid=(b, nq),
        in_specs=[pl.BlockSpec((tq, WIDTH_A), lambda bi, qi: (bi * nq + qi, 0)),
                  pl.BlockSpec((seq, LANES), lambda bi, qi: (bi, COL_KA // LANES)),
                  pl.BlockSpec((seq, LANES), lambda bi, qi: (bi, COL_VA // LANES)),
                  pl.BlockSpec((1, past, LANES), lambda bi, qi: (bi, 0, 0)),
                  pl.BlockSpec((1, past, LANES), lambda bi, qi: (bi, 0, 0)),
                  pl.BlockSpec((1, 1), lambda bi, qi: (0, 0))],
        out_specs=pl.BlockSpec((tq, WIDTH_A), lambda bi, qi: (bi * nq + qi, 0)),
        scratch_shapes=[pltpu.VMEM((N_KV_A, seq, LANES), BF16),
                        pltpu.VMEM((N_KV_A, past, LANES), BF16),
                        pltpu.VMEM((N_KV_A, HEAD_DIM + ONES_ROWS, seq), BF16),
                        pltpu.VMEM((N_KV_A, HEAD_DIM + ONES_ROWS, past), BF16),
                        pltpu.VMEM((N_KV_A, 8, LANES), F32)],
        compiler_params=_cparams(("arbitrary", "arbitrary")),
        name="latent_gqa",
    )(proj, proj, proj, ctx_k, ctx_v, qmax)


def _na_kernel(q_ref, k_ref, v_ref, ck_ref, cv_ref, bias_ref, bound_ref, o_ref, keys_ref, vt_ref, shift_ref,
               *, rows):
    i = pl.program_id(2)
    n_kblk = k_ref.shape[0] // NA_KBLK
    lane_k = lax.broadcasted_iota(jnp.int32, (1, LANES), 1)
    one = jnp.ones((), BF16)

    @pl.when(i == 0)
    def _():
        k = k_ref[...]
        ck = ck_ref[0]
        for half in range(2):
            kh = jnp.where(lane_k == _ones_lane(half), one, _keep_half(k, half))
            keys_ref[half, 0:n_kblk] = kh.reshape(n_kblk, NA_KBLK, LANES)
            ckh = _keep_half(ck, half)
            keys_ref[half, n_kblk] = jnp.where(lane_k == _ones_lane(half), one, ckh)
            ckf = ckh.astype(F32)
            ctx_norm = jnp.sqrt(jnp.max(jnp.sum(ckf * ckf, axis=1, keepdims=True), axis=0, keepdims=True))
            consts = bound_ref[0, half:half + 1, :]
            kmax = jnp.maximum(ctx_norm, consts[:, 2:3])
            shift_ref[half] = jnp.broadcast_to(_round_up_bf16(consts[:, 0:1] * kmax + consts[:, 1:2]),
                                               shift_ref.shape[1:])
        ones_rows = jnp.ones((ONES_ROWS, NA_KBLK), F32)
        vt = v_ref[...].astype(F32).T
        for j in range(n_kblk):
            vt_ref[j] = jnp.concatenate([vt[:, j * NA_KBLK:(j + 1) * NA_KBLK], ones_rows], axis=0).astype(BF16)
        vt_ref[n_kblk] = jnp.concatenate([cv_ref[0].astype(F32).T, ones_rows], axis=0).astype(BF16)

    q = q_ref[...]
    lane_q = lax.broadcasted_iota(jnp.int32, q.shape, 1)
    first = _na_first_key_block(i, rows)
    n_qblk = rows // NA_QROWS
    variant = jnp.where(i == 0, 0, jnp.where(i == n_qblk - 1, 2, 1))
    blocks = [first + j for j in range(NA_TK // NA_KBLK)] + [n_kblk]
    values_t = [vt_ref[blk] for blk in blocks]
    heads = []
    for half in range(2):
        keys = [keys_ref[half, blk] for blk in blocks]
        biases = [bias_ref[variant, half, j * NA_KBLK:(j + 1) * NA_KBLK, :] for j in range(NA_TK // NA_KBLK)] + [None]
        heads.append((shift_ref[half][0:1, 0:1], keys, biases))
    safe = jnp.max(jnp.maximum(shift_ref[0], shift_ref[1])) <= ATTN_SAFE_SHIFT

    def attend(probabilities):
        outs = []
        for half, ps in enumerate(probabilities):
            o = functools.reduce(lambda a, b: a + b, [_dot(vt, p) for vt, p in zip(values_t, ps)])
            outs.append(o[half * HEAD_DIM:(half + 1) * HEAD_DIM] / o[2 * HEAD_DIM:2 * HEAD_DIM + 1])
        o_ref[...] = jnp.concatenate(outs, axis=0).T

    def with_bound():
        probabilities = []
        for half, (shift, keys, biases) in enumerate(heads):
            qa = jnp.where(lane_q == _ones_lane(half), (-shift).astype(BF16), _keep_half(q, half))
            probabilities.append([jnp.exp(s).astype(BF16) for s in _scores(qa, keys, biases)])
        attend(probabilities)

    def with_row_max():
        all_scores = [_scores(_keep_half(q, half), keys, biases) for half, (_, keys, biases) in enumerate(heads)]
        probabilities = []
        for scores in all_scores:
            m = functools.reduce(jnp.maximum, [jnp.max(s, axis=0, keepdims=True) for s in scores])
            probabilities.append([jnp.exp(s - m).astype(BF16) for s in scores])
        attend(probabilities)

    pl.when(safe)(with_bound)
    pl.when(jnp.logical_not(safe))(with_row_max)


def _na_first_key_block(i, rows):
    per_qblock = NA_QROWS * GRID_W // NA_KBLK
    lead = (NA_KH // 2) * GRID_W // NA_KBLK
    return jnp.clip(per_qblock * i - lead, 0, (rows - NA_KROWS) * GRID_W // NA_KBLK)


def _latent_neighbourhood(proj, ctx_k, ctx_v, bias_t, bounds, *, seq):
    t = proj.shape[0]
    b = t // seq
    rows = seq // GRID_W
    nblk = rows // NA_QROWS
    n_kblk = seq // NA_KBLK
    past = ctx_k.shape[1]
    assert past == NA_KBLK
    grid = (N_HEADS_B // 2, b, nblk)
    in_specs = [pl.BlockSpec((NA_TQ, LANES), lambda hp, bi, i: (bi * nblk + i, COL_QB // LANES + hp)),
                pl.BlockSpec((seq, LANES), lambda hp, bi, i: (bi, COL_KB // LANES + hp)),
                pl.BlockSpec((seq, LANES), lambda hp, bi, i: (bi, COL_VB // LANES + hp)),
                pl.BlockSpec((1, past, LANES), lambda hp, bi, i: (bi, 0, hp)),
                pl.BlockSpec((1, past, LANES), lambda hp, bi, i: (bi, 0, hp)),
                pl.BlockSpec((3, 2, NA_TK, NA_TQ), lambda hp, bi, i: (0, hp, 0, 0)),
                pl.BlockSpec((1, 2, LANES), lambda hp, bi, i: (hp, 0, 0))]
    return pl.pallas_call(
        functools.partial(_na_kernel, rows=rows),
        out_shape=jax.ShapeDtypeStruct((t, WIDTH_B), F32),
        grid=grid,
        in_specs=in_specs,
        out_specs=pl.BlockSpec((NA_TQ, LANES), lambda hp, bi, i: (bi * nblk + i, hp)),
        scratch_shapes=[pltpu.VMEM((2, n_kblk + 1, NA_KBLK, LANES), BF16),
                        pltpu.VMEM((n_kblk + 1, 2 * HEAD_DIM + ONES_ROWS, NA_KBLK), BF16),
                        pltpu.VMEM((2, 8, LANES), F32)],
        compiler_params=_cparams(("arbitrary", "arbitrary", "arbitrary")),
        name="latent_neighbourhood",
    )(proj, proj, proj, ctx_k, ctx_v, bias_t, bounds)


def _neighbourhood_bounds(qn_b, kn_b, rpb):
    n_heads = rpb.shape[0]
    qmax = jnp.broadcast_to(_query_norm_bound(qn_b), (n_heads, 1))
    kmax = jnp.broadcast_to(_query_norm_bound(kn_b) * (HEAD_DIM ** 0.5), (n_heads, 1))
    bmax = jnp.maximum(jnp.max(rpb.reshape(n_heads, -1), axis=1, keepdims=True), 0.0).astype(F32)
    table = jnp.concatenate([qmax, bmax, kmax, jnp.zeros((n_heads, LANES - 3), F32)], axis=1)
    return table.reshape(n_heads // 2, 2, LANES)


def _neighbourhood_bias(rpb, rows):
    nblk = rows // NA_QROWS
    n_dr = 2 * NA_KH - 1
    n_dc = 2 * NA_KW - 1
    kc = np.arange(GRID_W)[:, None]
    qc = np.arange(GRID_W)[None, :]
    ws = np.clip(qc - NA_KW // 2, 0, GRID_W - NA_KW)
    col_ok = (kc >= ws) & (kc < ws + NA_KW)
    dc = np.clip(kc - qc + NA_KW - 1, 0, n_dc - 1)
    dc_onehot = (dc[None] == np.arange(n_dc)[:, None, None]).astype(np.float32)
    tiles = jnp.einsum('hab,bkq->hakq', rpb.astype(F32), jnp.asarray(dc_onehot),
                       precision=lax.Precision.HIGHEST)
    tiles = jnp.where(jnp.asarray(col_ok)[None, None], tiles, MASKED)
    masked_tile = jnp.full((rpb.shape[0], 1, GRID_W, GRID_W), MASKED, F32)
    tiles = jnp.concatenate([tiles, masked_tile], axis=1)
    pick = np.zeros((3, NA_KROWS, NA_QROWS, n_dr + 1), np.float32)
    for v, i in enumerate((0, 1, nblk - 1)):
        r0 = i * NA_QROWS
        ks = int(np.clip(r0 - NA_KH // 2, 0, rows - NA_KROWS))
        for kl in range(NA_KROWS):
            for ql in range(NA_QROWS):
                kr, qr = ks + kl, r0 + ql
                rs = int(np.clip(qr - NA_KH // 2, 0, rows - NA_KH))
                ok = rs <= kr < rs + NA_KH
                pick[v, kl, ql, (kr - qr + NA_KH - 1) if ok else n_dr] = 1.0
    bias = jnp.einsum('vkqa,hacd->vhkcqd', jnp.asarray(pick), tiles, precision=lax.Precision.HIGHEST)
    return bias.reshape(3, rpb.shape[0], NA_TK, NA_TQ)


def _merge_kernel(xp_ref, oap_ref, obp_ref, xs_ref, oas_ref, obs_ref, ona_ref, onb_ref, wo_ref,
                  g1_ref, sh2_ref, sc2_ref, n2_ref, y_ref, h_ref, hp_ref, *, ctx_tiles):
    def one_stream(x_ref, oa_ref, ob_ref):
        na = (_rms(oa_ref[...]) * ona_ref[...]).astype(BF16)
        nb = (_rms(ob_ref[...]) * onb_ref[...]).astype(BF16)
        mix = _dot(na, wo_ref[0:WIDTH_A, :]) + _dot(nb, wo_ref[WIDTH_A:WIDTH_A + WIDTH_B, :])
        y = x_ref[...] + g1_ref[0] * mix
        y_ref[...] = y
        h = _rms(y) * n2_ref[...]
        h = h * (1.0 + sc2_ref[0]) + sh2_ref[0]
        h_ref[...] = h
        hp_ref[...] = _pack_rows(h)

    i = pl.program_id(0)
    pl.when(i < ctx_tiles)(lambda: one_stream(xp_ref, oap_ref, obp_ref))
    pl.when(i >= ctx_tiles)(lambda: one_stream(xs_ref, oas_ref, obs_ref))


def _merge(ctx, lat, on_a, on_b, w_out_bf, gate1, shift2, scale2, norm2, *, tm, lat_seq):
    t_c, d = ctx[0].shape
    t_l = lat[0].shape[0]
    ctx_tiles = t_c // tm
    lat_tiles_per_batch = lat_seq // tm

    def ctx_map(i):
        return (jnp.minimum(i, ctx_tiles - 1), 0)

    def lat_map(i):
        return (jnp.maximum(i - ctx_tiles, 0), 0)

    def mod_map(i):
        return (jnp.where(i < ctx_tiles, 0, 1 + (i - ctx_tiles) // lat_tiles_per_batch), 0, 0)

    def stream_specs(index_map):
        return [pl.BlockSpec((tm, d), index_map),
                pl.BlockSpec((tm, WIDTH_A), index_map),
                pl.BlockSpec((tm, WIDTH_B), index_map)]

    return pl.pallas_call(
        functools.partial(_merge_kernel, ctx_tiles=ctx_tiles),
        out_shape=[jax.ShapeDtypeStruct((t_c + t_l, d), F32)] * 2
        + [jax.ShapeDtypeStruct((t_c + t_l, d // 2), jnp.int32)],
        grid=((t_c + t_l) // tm,),
        in_specs=stream_specs(ctx_map) + stream_specs(lat_map) + [
            pl.BlockSpec((1, WIDTH_A), lambda i: (0, 0)),
            pl.BlockSpec((1, WIDTH_B), lambda i: (0, 0)),
            pl.BlockSpec((WIDTH_A + WIDTH_B, d), lambda i: (0, 0)),
            pl.BlockSpec((1, 1, d), mod_map),
            pl.BlockSpec((1, 1, d), mod_map),
            pl.BlockSpec((1, 1, d), mod_map),
            pl.BlockSpec((1, d), lambda i: (0, 0))],
        out_specs=[pl.BlockSpec((tm, d), lambda i: (i, 0))] * 2 + [pl.BlockSpec((tm, d // 2), lambda i: (i, 0))],
        compiler_params=_cparams(("arbitrary",)),
        name="merge_out_proj",
    )(*ctx, *lat, on_a, on_b, w_out_bf, gate1, shift2, scale2, norm2)


def _first_index_of_max(x, iota):
    mx = jnp.max(x, axis=0, keepdims=True)
    idx = jnp.min(jnp.where(x == mx, iota, float(x.shape[0])), axis=0, keepdims=True)
    return mx, iota == idx


def _router_gates(h, wr_hi, wr_lo, rbias):
    h_hi = h.astype(BF16)
    h_lo = (h - h_hi.astype(F32)).astype(BF16)
    logits = _dot_nt(wr_hi, h_hi) + (_dot_nt(wr_lo, h_hi) + _dot_nt(wr_hi, h_lo))
    scores = _sigmoid(logits)
    sel = scores + rbias
    tm = sel.shape[1]
    iota_g = lax.broadcasted_iota(jnp.int32, (GROUP_SIZE, tm), 0).astype(F32)
    group_scores = []
    for g in range(N_GROUPS):
        grp = sel[g * GROUP_SIZE:(g + 1) * GROUP_SIZE]
        m1, first = _first_index_of_max(grp, iota_g)
        m2 = jnp.max(jnp.where(first, -jnp.inf, grp), axis=0, keepdims=True)
        group_scores.append(m1 + m2)
    gs = jnp.concatenate(group_scores, axis=0)
    iota_n = lax.broadcasted_iota(jnp.int32, (N_GROUPS, tm), 0).astype(F32)
    group_on = jnp.zeros((N_GROUPS, tm), F32)
    for _ in range(TOPK_GROUPS):
        _, pick = _first_index_of_max(gs, iota_n)
        group_on = jnp.where(pick, 1.0, group_on)
        gs = jnp.where(pick, -jnp.inf, gs)
    expert_on = jnp.concatenate(
        [jnp.broadcast_to(group_on[g:g + 1], (GROUP_SIZE, tm)) for g in range(N_GROUPS)], axis=0)
    cand = jnp.where(expert_on > 0.0, sel, -jnp.inf)
    iota_e = lax.broadcasted_iota(jnp.int32, (N_EXPERTS, tm), 0).astype(F32)
    w = jnp.zeros((N_EXPERTS, tm), F32)
    chosen = jnp.zeros((N_EXPERTS, tm), F32)
    for _ in range(TOP_K):
        _, pick = _first_index_of_max(cand, iota_e)
        w = jnp.where(pick, scores, w)
        chosen = jnp.where(pick, 1.0, chosen)
        cand = jnp.where(pick, -jnp.inf, cand)
    return w / jnp.sum(w, axis=0, keepdims=True) * ROUTED_SCALE, chosen


MOE_TS = 1024
MOE_ROUTE_TM = 512
MOE_ROW_TM = 512


def _route_kernel(h_ref, wrh_ref, wrl_ref, rb_ref, tri_ref, gates_ref, rank_ref, count_ref):
    @pl.when(pl.program_id(0) == 0)
    def _():
        count_ref[...] = jnp.zeros_like(count_ref)

    gates, chosen = _router_gates(h_ref[...], wrh_ref[...], wrl_ref[...], rb_ref[...])
    gates_ref[...] = gates
    before = _dot(chosen.astype(BF16), tri_ref[...])
    seen = count_ref[...]
    rank_ref[...] = jnp.where(chosen > 0.0, before + seen[:, 0:1], -1.0)
    count_ref[...] = seen + jnp.sum(chosen, axis=1, keepdims=True)


def _route(h_all, wr_hi, wr_lo, rbias):
    t, d = h_all.shape
    tm = MOE_ROUTE_TM
    tri = jnp.asarray(np.triu(np.ones((tm, tm), np.float32), k=1), BF16)
    return pl.pallas_call(
        _route_kernel,
        out_shape=[jax.ShapeDtypeStruct((N_EXPERTS, t), F32),
                   jax.ShapeDtypeStruct((N_EXPERTS, t), F32),
                   jax.ShapeDtypeStruct((N_EXPERTS, LANES), F32)],
        grid=(t // tm,),
        in_specs=[pl.BlockSpec((tm, d), lambda i: (i, 0)),
                  pl.BlockSpec((N_EXPERTS, d), lambda i: (0, 0)),
                  pl.BlockSpec((N_EXPERTS, d), lambda i: (0, 0)),
                  pl.BlockSpec((N_EXPERTS, 1), lambda i: (0, 0)),
                  pl.BlockSpec((tm, tm), lambda i: (0, 0))],
        out_specs=[pl.BlockSpec((N_EXPERTS, tm), lambda i: (0, i)),
                   pl.BlockSpec((N_EXPERTS, tm), lambda i: (0, i)),
                   pl.BlockSpec((N_EXPERTS, LANES), lambda i: (0, 0))],
        compiler_params=_cparams(("arbitrary",)),
        name="moe_route",
    )(h_all, wr_hi, wr_lo, rbias, tri)


def _slots_kernel(gates_ref, rank_ref, off_ref, pos_ref, gtok_ref):
    gates = gates_ref[...]
    rank = rank_ref[...]
    tm = gates.shape[1]
    slot = off_ref[...] + rank
    left = jnp.where(rank >= 0.0, 1.0, 0.0)
    iota_e = lax.broadcasted_iota(jnp.int32, (N_EXPERTS, tm), 0).astype(F32)
    pos_rows, gate_rows = [], []
    for _ in range(TOP_K):
        _, pick = _first_index_of_max(left, iota_e)
        pos_rows.append(jnp.sum(jnp.where(pick, slot, 0.0), axis=0, keepdims=True))
        gate_rows.append(jnp.sum(jnp.where(pick, gates, 0.0), axis=0, keepdims=True))
        left = jnp.where(pick, 0.0, left)
    pos_ref[...] = jnp.concatenate(pos_rows, axis=0).astype(jnp.int32)
    pad = jnp.zeros((LANES - TOP_K, tm), F32)
    gtok_ref[...] = jnp.concatenate(gate_rows + [pad], axis=0).T


def _slots(gates_t, rank_t, off):
    t = gates_t.shape[1]
    tm = MOE_ROUTE_TM
    return pl.pallas_call(
        _slots_kernel,
        out_shape=[jax.ShapeDtypeStruct((TOP_K, t), jnp.int32), jax.ShapeDtypeStruct((t, LANES), F32)],
        grid=(t // tm,),
        in_specs=[pl.BlockSpec((N_EXPERTS, tm), lambda i: (0, i)),
                  pl.BlockSpec((N_EXPERTS, tm), lambda i: (0, i)),
                  pl.BlockSpec((N_EXPERTS, 1), lambda i: (0, 0))],
        out_specs=[pl.BlockSpec((TOP_K, tm), lambda i: (0, i)),
                   pl.BlockSpec((tm, LANES), lambda i: (i, 0))],
        compiler_params=_cparams(("arbitrary",)),
        name="moe_slots",
    )(gates_t, rank_t, off)


SC_CORES = 2
SC_SUBCORES = 16
SC_ROWS = 64


def _dispatch(hp_all, slot_of):
    t, width = hp_all.shape
    n_slots = slot_of.shape[0]
    n_pad = n_slots - TOP_K * t
    workers = SC_CORES * SC_SUBCORES
    per_worker = t // workers
    pad_per_worker = n_pad // workers
    assert per_worker * workers == t and per_worker % SC_ROWS == 0
    assert pad_per_worker * workers == n_pad and pad_per_worker % SC_ROWS == 0
    mesh = plsc.VectorSubcoreMesh(core_axis_name="core", subcore_axis_name="subcore")

    @functools.partial(
        pl.kernel, mesh=mesh,
        out_type=jax.ShapeDtypeStruct((n_slots, width), jnp.int32),
        scratch_types=[pltpu.VMEM((SC_ROWS,), jnp.int32),
                       pltpu.VMEM((SC_ROWS, width), jnp.int32),
                       pltpu.SemaphoreType.DMA],
    )
    def scatter_rows(h_hbm, slot_hbm, out_hbm, idx_v, rows_v, sem):
        worker = lax.axis_index("subcore") * SC_CORES + lax.axis_index("core")
        base = worker * per_worker

        @pl.loop(0, per_worker // SC_ROWS)
        def _(j):
            first = base + j * SC_ROWS
            pltpu.sync_copy(h_hbm.at[pl.ds(first, SC_ROWS)], rows_v)
            for k in range(TOP_K):
                pltpu.sync_copy(slot_hbm.at[pl.ds(k * t + first, SC_ROWS)], idx_v)
                pltpu.async_copy(rows_v, out_hbm.at[idx_v], sem).wait()

        pltpu.sync_copy(h_hbm.at[pl.ds(0, SC_ROWS)], rows_v)
        pad_base = TOP_K * t + worker * pad_per_worker

        @pl.loop(0, pad_per_worker // SC_ROWS)
        def _(j):
            pltpu.sync_copy(slot_hbm.at[pl.ds(pad_base + j * SC_ROWS, SC_ROWS)], idx_v)
            pltpu.async_copy(rows_v, out_hbm.at[idx_v], sem).wait()

    return scatter_rows(hp_all, slot_of)


def _experts_kernel(te_ref, xs_ref, wg_ref, wu_ref, wd_ref, ys_ref, wg_bf, wu_bf, wd_bf):
    i = pl.program_id(0)

    @pl.when((i == 0) | (te_ref[i] != te_ref[jnp.maximum(i, 1) - 1]))
    def _():
        wg_bf[...] = wg_ref[0].astype(BF16)
        wu_bf[...] = wu_ref[0].astype(BF16)
        wd_bf[...] = wd_ref[0].astype(BF16)

    left, right = _unpack_rows(xs_ref[...])
    x = jnp.concatenate([left, right], axis=1).astype(BF16)
    g = _dot(x, wg_bf[...])
    u = _dot(x, wu_bf[...])
    act = (g * _sigmoid(g)) * u
    ys_ref[...] = _pack_rows(_dot(act.astype(BF16), wd_bf[...]))


def _experts(xs, tile_expert, w_gate, w_up, w_down):
    n_slots, width = xs.shape
    d = 2 * width
    ts = MOE_TS
    return pl.pallas_call(
        _experts_kernel,
        out_shape=jax.ShapeDtypeStruct((n_slots, width), jnp.int32),
        grid_spec=pltpu.PrefetchScalarGridSpec(
            num_scalar_prefetch=1,
            grid=(n_slots // ts,),
            in_specs=[pl.BlockSpec((ts, width), lambda i, te: (i, 0)),
                      pl.BlockSpec((1, d, D_EXPERT), lambda i, te: (te[i], 0, 0)),
                      pl.BlockSpec((1, d, D_EXPERT), lambda i, te: (te[i], 0, 0)),
                      pl.BlockSpec((1, D_EXPERT, d), lambda i, te: (te[i], 0, 0))],
            out_specs=pl.BlockSpec((ts, width), lambda i, te: (i, 0)),
            scratch_shapes=[pltpu.VMEM((d, D_EXPERT), BF16), pltpu.VMEM((d, D_EXPERT), BF16),
                            pltpu.VMEM((D_EXPERT, d), BF16)]),
        compiler_params=_cparams(("arbitrary",)),
        name="moe_experts",
    )(tile_expert, xs, w_gate, w_up, w_down)


def _gather_slots(y_slots, slot_of, t):
    width = y_slots.shape[1]
    workers = SC_CORES * SC_SUBCORES
    per_worker = t // workers
    n_blocks = (per_worker // SC_ROWS) * TOP_K
    assert per_worker * workers == t and per_worker % SC_ROWS == 0 and n_blocks % 2 == 0
    mesh = plsc.VectorSubcoreMesh(core_axis_name="core", subcore_axis_name="subcore")

    @functools.partial(
        pl.kernel, mesh=mesh,
        out_type=jax.ShapeDtypeStruct((TOP_K * t, width), jnp.int32),
        scratch_types=[pltpu.VMEM((SC_ROWS,), jnp.int32), pltpu.VMEM((SC_ROWS,), jnp.int32),
                       pltpu.VMEM((SC_ROWS, width), jnp.int32), pltpu.VMEM((SC_ROWS, width), jnp.int32),
                       pltpu.SemaphoreType.DMA, pltpu.SemaphoreType.DMA],
    )
    def gather_rows(ys_hbm, slot_hbm, out_hbm, idx0, idx1, rows0, rows1, sem0, sem1):
        worker = lax.axis_index("subcore") * SC_CORES + lax.axis_index("core")
        base = worker * per_worker

        def first_row(n):
            return (n % TOP_K) * t + base + (n // TOP_K) * SC_ROWS

        def start(n, idx_v, rows_v, sem):
            pltpu.sync_copy(slot_hbm.at[pl.ds(first_row(n), SC_ROWS)], idx_v)
            pltpu.async_copy(ys_hbm.at[idx_v], rows_v, sem)

        def finish(n, idx_v, rows_v, sem):
            pltpu.make_async_copy(ys_hbm.at[idx_v], rows_v, sem).wait()
            pltpu.sync_copy(rows_v, out_hbm.at[pl.ds(first_row(n), SC_ROWS)])

        start(0, idx0, rows0, sem0)

        @pl.loop(0, n_blocks, step=2)
        def _(n):
            start(n + 1, idx1, rows1, sem1)
            finish(n, idx0, rows0, sem0)

            @pl.when(n + 2 < n_blocks)
            def _():
                start(n + 2, idx0, rows0, sem0)

            finish(n + 1, idx1, rows1, sem1)

    return gather_rows(y_slots, slot_of)


def _combine_kernel(y_ref, h_ref, g2_ref, gtok_ref, rows_ref, wgs_ref, wus_ref, wds_ref, o_ref):
    h_left, h_right = _unpack_rows(h_ref[...])
    h = jnp.concatenate([h_left, h_right], axis=1).astype(BF16)
    gs = _dot(h, wgs_ref[...])
    us = _dot(h, wus_ref[...])
    shared = _dot(((gs * _sigmoid(gs)) * us).astype(BF16), wds_ref[...])

    gtok = gtok_ref[...]
    acc_left = acc_right = None
    for k in range(TOP_K):
        left, right = _unpack_rows(rows_ref[k])
        gate = gtok[:, k:k + 1]
        acc_left = gate * left if acc_left is None else acc_left + gate * left
        acc_right = gate * right if acc_right is None else acc_right + gate * right
    routed = jnp.concatenate([acc_left, acc_right], axis=1)
    o_ref[...] = y_ref[...] + g2_ref[0] * (routed + shared)


def _combine(y_all, hp_all, gate2, gtok, rows, wgs, wus, wds, *, first_token, tokens, seq):
    d = y_all.shape[1]
    width = hp_all.shape[1]
    tm = MOE_ROW_TM
    tile0 = first_token // tm
    nb = gate2.shape[0]
    tiles_per_batch = seq // tm

    def mod_map(i):
        return ((i // tiles_per_batch) if nb > 1 else 0, 0, 0)

    return pl.pallas_call(
        _combine_kernel,
        out_shape=jax.ShapeDtypeStruct((tokens, d), F32),
        grid=(tokens // tm,),
        in_specs=[pl.BlockSpec((tm, d), lambda i: (tile0 + i, 0)),
                  pl.BlockSpec((tm, width), lambda i: (tile0 + i, 0)),
                  pl.BlockSpec((1, 1, d), mod_map),
                  pl.BlockSpec((tm, LANES), lambda i: (tile0 + i, 0)),
                  pl.BlockSpec((TOP_K, tm, width), lambda i: (0, tile0 + i, 0)),
                  pl.BlockSpec((d, D_SHARED), lambda i: (0, 0)),
                  pl.BlockSpec((d, D_SHARED), lambda i: (0, 0)),
                  pl.BlockSpec((D_SHARED, d), lambda i: (0, 0))],
        out_specs=pl.BlockSpec((tm, d), lambda i: (i, 0)),
        compiler_params=_cparams(("arbitrary",)),
        name="moe_combine",
    )(y_all, hp_all, gate2, gtok, rows, wgs, wus, wds)


def _expert_layout(counts, n_tiles):
    cnt = counts.astype(jnp.int32)
    tiles = (cnt + (MOE_TS - 1)) // MOE_TS
    last_tile = jnp.cumsum(tiles)
    off = (last_tile - tiles) * MOE_TS
    pad_lo = off + cnt
    pad_hi = (off + tiles * MOE_TS).at[N_EXPERTS - 1].set(n_tiles * MOE_TS)
    pad_cnt = pad_hi - pad_lo
    pad_last = jnp.cumsum(pad_cnt)
    shift = pad_lo - (pad_last - pad_cnt)
    j = jnp.arange(N_EXPERTS * MOE_TS, dtype=jnp.int32)
    past = (pad_last[None, :-1] <= j[:, None]).astype(jnp.int32)
    pad_slots = j + shift[0] + jnp.sum(past * (shift[1:] - shift[:-1])[None, :], axis=1)
    tile_ids = jnp.arange(n_tiles, dtype=jnp.int32)
    tile_expert = jnp.minimum(
        jnp.sum((last_tile[None, :] <= tile_ids[:, None]).astype(jnp.int32), axis=1), N_EXPERTS - 1)
    return off, pad_slots, tile_expert


def _rope_tables(n_tokens):
    t = jnp.arange(n_tokens)
    row = (t // GRID_W).astype(F32)
    col = (t % GRID_W).astype(F32)
    nf = HEAD_DIM // 4
    freqs = ROPE_THETA ** (-jnp.arange(nf, dtype=F32) / nf)
    ang_r = row[:, None] * freqs
    ang_c = col[:, None] * freqs
    cos = jnp.concatenate([jnp.cos(ang_r)] * 2 + [jnp.cos(ang_c)] * 2, axis=1)
    sin = jnp.concatenate([-jnp.sin(ang_r), jnp.sin(ang_r), -jnp.sin(ang_c), jnp.sin(ang_c)], axis=1)
    reps = LANES // HEAD_DIM
    return jnp.tile(cos, (1, reps)), jnp.tile(sin, (1, reps))


def _head_gains(qn_a, kn_a, qn_b, kn_b):
    ones = jnp.ones((HEAD_DIM,), F32)
    parts = ([qn_a] * N_HEADS_A + [kn_a] * N_KV_A + [ones] * N_KV_A
             + [qn_b] * N_HEADS_B + [kn_b] * N_HEADS_B + [ones] * N_HEADS_B)
    return jnp.concatenate(parts).reshape(1, IN_COLS).astype(F32)


def _same_head_indicator():
    i = np.arange(LANES)
    return jnp.asarray((i[:, None] // HEAD_DIM) == (i[None, :] // HEAD_DIM), BF16)


def _token_major(cache):
    b, h, s, hd = cache.shape
    return cache.transpose(0, 2, 1, 3).reshape(b, s, h * hd).astype(BF16)


def kernel(x_prompt, x_sample, cache_k_a, cache_v_a, cache_k_b, cache_v_b, c, c_ctx, w_mod, b_mod, norm1, norm2, w_in, qn_a, kn_a, qn_b, kn_b, rpb, on_a, on_b, w_out, w_router, router_bias, w_gate_e, w_up_e, w_down_e, w_gate_s, w_up_s, w_down_s):
    depth = w_mod.shape[0]
    assert depth == 1
    l = 0
    bp, sp, d = x_prompt.shape
    bs, ss, _ = x_sample.shape

    cvec = jnp.concatenate([c_ctx[None, :], c], axis=0)
    rows = -(-cvec.shape[0] // 8) * 8
    cvec = jnp.pad(cvec, ((0, rows - cvec.shape[0]), (0, 0)))
    mod = _adaln(cvec, w_mod[l], b_mod[l])
    mod_p = [m.reshape(1, 1, d) for m in jnp.split(mod[0:1], 6, axis=-1)]
    mod_s = [m.reshape(bs, 1, d) for m in jnp.split(mod[1:1 + bs], 6, axis=-1)]
    mod_all = [m.reshape(1 + bs, 1, d) for m in jnp.split(mod[0:1 + bs], 6, axis=-1)]

    w_in_bf = w_in[l].astype(BF16)
    w_out_bf = w_out[l].astype(BF16)
    gain = _head_gains(qn_a[l], kn_a[l], qn_b[l], kn_b[l])
    seg = _same_head_indicator()
    n1 = norm1[l].reshape(1, d)
    n2 = norm2[l].reshape(1, d)
    ona = on_a[l].reshape(1, WIDTH_A)
    onb = on_b[l].reshape(1, WIDTH_B)
    wr_t = w_router[l].T
    wr_hi = wr_t.astype(BF16)
    wr_lo = (wr_t - wr_hi.astype(F32)).astype(BF16)
    rbias = router_bias[l].reshape(N_EXPERTS, 1).astype(F32)
    wgs = w_gate_s[l].astype(BF16)
    wus = w_up_s[l].astype(BF16)
    wds = w_down_s[l].astype(BF16)
    t_p = bp * sp
    t_s = bs * ss
    t_all = t_p + t_s

    xp = x_prompt.reshape(t_p, d)
    proj_p, st_ka, st_va, st_kb, st_vb = _project(
        xp, mod_p[0], mod_p[1], n1, w_in_bf, gain, seg, None, tm=sp, seq=sp, states=True)
    oa_p, ob_p = _context_attention(proj_p, seq=sp)

    xs = x_sample.reshape(t_s, d)
    proj_s, = _project(xs, mod_s[0], mod_s[1], n1, w_in_bf, gain, seg, _rope_tables(ss),
                       tm=512, seq=ss, states=False)
    oa_s = _latent_gqa(proj_s, _token_major(cache_k_a[:, l]), _token_major(cache_v_a[:, l]),
                       _query_norm_bound(qn_a[l]), seq=ss, tq=128)
    bias_t = _neighbourhood_bias(rpb[l], ss // GRID_W)
    ob_s = _latent_neighbourhood(proj_s, _token_major(cache_k_b[:, l]), _token_major(cache_v_b[:, l]),
                                 bias_t, _neighbourhood_bounds(qn_b[l], kn_b[l], rpb[l]), seq=ss)

    y1_all, h_all, hp_all = _merge((xp, oa_p, ob_p), (xs, oa_s, ob_s), ona, onb, w_out_bf,
                                   mod_all[2], mod_all[3], mod_all[4], n2, tm=512, lat_seq=ss)
    gates_t, rank_t, counts = _route(h_all, wr_hi, wr_lo, rbias)
    n_tiles = t_all * TOP_K // MOE_TS + N_EXPERTS
    off, pad_slots, tile_expert = _expert_layout(counts[:, 0], n_tiles)
    pos, gtok = _slots(gates_t, rank_t, off.astype(F32).reshape(N_EXPERTS, 1))
    slot_of = pos.reshape(TOP_K * t_all)
    x_slots = _dispatch(hp_all, jnp.concatenate([slot_of, pad_slots]))
    y_slots = _experts(x_slots, tile_expert, w_gate_e[l], w_up_e[l], w_down_e[l])
    rows = _gather_slots(y_slots, slot_of, t_all).reshape(TOP_K, t_all, d // 2)
    y_p = _combine(y1_all, hp_all, mod_p[5], gtok, rows, wgs, wus, wds,
                   first_token=0, tokens=t_p, seq=sp)
    y_s = _combine(y1_all, hp_all, mod_s[5], gtok, rows, wgs, wus, wds,
                   first_token=t_p, tokens=t_s, seq=ss)

    return (y_p.reshape(bp, sp, d), y_s.reshape(bs, ss, d), st_ka, st_va, st_kb, st_vb)
```

```python
import functools

import numpy as np
import jax
import jax.numpy as jnp
from jax import lax
from jax.experimental import pallas as pl
from jax.experimental.pallas import tpu as pltpu
from jax.experimental.pallas import tpu_sc as plsc

F32 = jnp.float32
BF16 = jnp.bfloat16

D_MODEL = 1024
HEAD_DIM = 64
N_HEADS_A = 8
N_KV_A = 2
GROUP_A = N_HEADS_A // N_KV_A
N_HEADS_B = 8
WIDTH_A = N_HEADS_A * HEAD_DIM
WIDTH_B = N_HEADS_B * HEAD_DIM
KV_WIDTH_A = N_KV_A * HEAD_DIM
IN_COLS = WIDTH_A + 2 * KV_WIDTH_A + 3 * WIDTH_B
GRID_W = 64
ROPE_THETA = 10000.0
NA_KH = 8
NA_KW = 16
N_EXPERTS = 64
N_GROUPS = 8
GROUP_SIZE = N_EXPERTS // N_GROUPS
TOPK_GROUPS = 4
TOP_K = 8
D_EXPERT = 256
D_SHARED = 256
ROUTED_SCALE = 2.5
EPS = 1e-6

LANES = 128
MXU_DIM = 256
MASKED = -1e30

COL_QA = 0
COL_KA = WIDTH_A
COL_VA = COL_KA + KV_WIDTH_A
COL_QB = COL_VA + KV_WIDTH_A
COL_KB = COL_QB + WIDTH_B
COL_VB = COL_KB + WIDTH_B

NA_QROWS = 8
NA_KROWS = 2 * NA_KH
NA_TQ = NA_QROWS * GRID_W
NA_TK = NA_KROWS * GRID_W
NA_KBLK = 256

VMEM_LIMIT = 56 * 1024 * 1024


def _cparams(sem):
    return pltpu.CompilerParams(dimension_semantics=sem, vmem_limit_bytes=VMEM_LIMIT)


def _dot(a, b):
    return jnp.dot(a, b, preferred_element_type=F32)


def _dot_nt(a, b):
    return lax.dot_general(a, b, (((1,), (1,)), ((), ())), preferred_element_type=F32)


def _sigmoid(x):
    return 1.0 / (1.0 + jnp.exp(-x))


def _rms(x):
    return x * lax.rsqrt(jnp.mean(x * x, axis=-1, keepdims=True) + EPS)


def _pack_rows(x):
    n = x.shape[1] // 2
    hi = lax.bitcast_convert_type(x[:, :n].astype(BF16).astype(F32), jnp.int32)
    lo = lax.bitcast_convert_type(x[:, n:].astype(BF16).astype(F32), jnp.int32)
    return hi | lax.shift_right_logical(lo, 16)


def _unpack_rows(w):
    left = lax.bitcast_convert_type(w & jnp.int32(-65536), F32)
    right = lax.bitcast_convert_type(lax.shift_left(w, 16), F32)
    return left, right


def _mod_kernel(c_ref, w_ref, b_ref, o_ref):
    c = c_ref[...]
    s = c * _sigmoid(c)
    o_ref[...] = jnp.dot(s, w_ref[...], preferred_element_type=F32,
                         precision=lax.Precision.HIGHEST) + b_ref[...]


def _adaln(cvec, w_mod, b_mod):
    rows, d = cvec.shape
    n = w_mod.shape[1]
    tn = 512
    return pl.pallas_call(
        _mod_kernel,
        out_shape=jax.ShapeDtypeStruct((rows, n), F32),
        grid=(n // tn,),
        in_specs=[pl.BlockSpec((rows, d), lambda j: (0, 0)),
                  pl.BlockSpec((d, tn), lambda j: (0, j)),
                  pl.BlockSpec((1, tn), lambda j: (0, j))],
        out_specs=pl.BlockSpec((rows, tn), lambda j: (0, j)),
        compiler_params=_cparams(("arbitrary",)),
        name="adaln_mod",
    )(cvec, w_mod, b_mod.reshape(1, n))


_PROJ_CHUNKS = (
    [(COL_QA + i * LANES, LANES, True, True) for i in range(WIDTH_A // LANES)]
    + [(COL_KA, LANES, True, True), (COL_VA, LANES, False, False)]
    + [(COL_QB + i * LANES, LANES, True, False) for i in range(WIDTH_B // LANES)]
    + [(COL_KB + i * LANES, LANES, True, False) for i in range(WIDTH_B // LANES)]
    + [(COL_VB + i * LANES, LANES, False, False) for i in range(WIDTH_B // LANES)]
)


def _proj_kernel(*refs, rope, states):
    x_ref, sh_ref, sc_ref, n1_ref, w_ref, gain_ref, seg_ref = refs[:7]
    pos = 7
    if rope:
        cos_ref, sin_ref = refs[pos:pos + 2]
        pos += 2
    out_ref = refs[pos]
    pos += 1
    if states:
        ka_ref, va_ref, kb_ref, vb_ref = refs[pos:pos + 4]

    x = x_ref[...]
    h = _rms(x) * n1_ref[...]
    h = h * (1.0 + sc_ref[0]) + sh_ref[0]
    p = _dot(h.astype(BF16), w_ref[...])
    seg = seg_ref[...]
    if rope:
        cos = cos_ref[...]
        sin = sin_ref[...]
        lane = lax.broadcasted_iota(jnp.int32, cos.shape, 1)
        first_half = (lane % (HEAD_DIM // 2)) < (HEAD_DIM // 4)

    for c0, w, normed, roped in _PROJ_CHUNKS:
        pc = p[:, c0:c0 + w]
        if normed:
            sq = pc * pc
            hi = sq.astype(BF16)
            lo = (sq - hi.astype(F32)).astype(BF16)
            ss = _dot(hi, seg) + _dot(lo, seg)
            pc = pc * lax.rsqrt(ss * (1.0 / HEAD_DIM) + EPS) * gain_ref[:, c0:c0 + w]
        if states:
            if c0 == COL_KA:
                for hh in range(N_KV_A):
                    ka_ref[0, 0, hh] = pc[:, hh * HEAD_DIM:(hh + 1) * HEAD_DIM]
            elif c0 == COL_VA:
                for hh in range(N_KV_A):
                    va_ref[0, 0, hh] = pc[:, hh * HEAD_DIM:(hh + 1) * HEAD_DIM]
            elif COL_KB <= c0 < COL_VB:
                base = (c0 - COL_KB) // HEAD_DIM
                for hh in range(LANES // HEAD_DIM):
                    kb_ref[0, 0, base + hh] = pc[:, hh * HEAD_DIM:(hh + 1) * HEAD_DIM]
            elif c0 >= COL_VB:
                base = (c0 - COL_VB) // HEAD_DIM
                for hh in range(LANES // HEAD_DIM):
                    vb_ref[0, 0, base + hh] = pc[:, hh * HEAD_DIM:(hh + 1) * HEAD_DIM]
        if rope and roped:
            partner = jnp.where(first_half,
                                pltpu.roll(pc, LANES - HEAD_DIM // 4, 1),
                                pltpu.roll(pc, HEAD_DIM // 4, 1))
            pc = pc * cos + partner * sin
        if c0 < COL_KA or COL_QB <= c0 < COL_KB:
            pc = pc * (HEAD_DIM ** -0.5)
        out_ref[:, c0:c0 + w] = pc.astype(BF16)


def _project(x2d, shift, scale, norm1, w_in_bf, gain, seg, rope_tabs, *, tm, seq, states):
    t, d = x2d.shape
    nb = shift.shape[0]
    tiles_per_batch = seq // tm
    rope = rope_tabs is not None

    def mod_map(i):
        return ((i // tiles_per_batch) if nb > 1 else 0, 0, 0)

    in_specs = [pl.BlockSpec((tm, d), lambda i: (i, 0)),
                pl.BlockSpec((1, 1, d), mod_map),
                pl.BlockSpec((1, 1, d), mod_map),
                pl.BlockSpec((1, d), lambda i: (0, 0)),
                pl.BlockSpec((d, IN_COLS), lambda i: (0, 0)),
                pl.BlockSpec((1, IN_COLS), lambda i: (0, 0)),
                pl.BlockSpec((LANES, LANES), lambda i: (0, 0))]
    args = [x2d, shift, scale, norm1, w_in_bf, gain, seg]
    if rope:
        in_specs += [pl.BlockSpec((tm, LANES), lambda i: (i % tiles_per_batch, 0))] * 2
        args += list(rope_tabs)
    out_shape = [jax.ShapeDtypeStruct((t, IN_COLS), BF16)]
    out_specs = [pl.BlockSpec((tm, IN_COLS), lambda i: (i, 0))]
    if states:
        assert tm == seq
        b = t // seq
        for nh in (N_KV_A, N_KV_A, N_HEADS_B, N_HEADS_B):
            out_shape.append(jax.ShapeDtypeStruct((b, 1, nh, seq, HEAD_DIM), F32))
            out_specs.append(pl.BlockSpec((1, 1, nh, seq, HEAD_DIM), lambda i: (i, 0, 0, 0, 0)))
    return pl.pallas_call(
        functools.partial(_proj_kernel, rope=rope, states=states),
        out_shape=out_shape,
        grid=(t // tm,),
        in_specs=in_specs,
        out_specs=out_specs,
        compiler_params=_cparams(("arbitrary",)),
        name="proj_states" if states else "proj_rope",
    )(*args)


def _lane_half(shape):
    return lax.broadcasted_iota(jnp.int32, shape, 1) // HEAD_DIM


def _keep_half(x, half):
    return jnp.where(_lane_half(x.shape) == half, x, jnp.zeros_like(x))


def _transpose_bf16(x):
    return x.astype(F32).T.astype(BF16)


def _attend(q, keys, values_t, biases):
    return _softmax_av(_scores(q, keys, biases), values_t)


def _scores(q, keys, biases):
    scores = []
    for k, b in zip(keys, biases):
        s = _dot_nt(k, q)
        if b is not None:
            s = s + b
        scores.append(s)
    return scores


def _softmax_av(scores, values_t):
    m = functools.reduce(jnp.maximum, [jnp.max(s, axis=0, keepdims=True) for s in scores])
    denom = None
    out = None
    for s, vt in zip(scores, values_t):
        p = jnp.exp(s - m)
        ps = jnp.sum(p, axis=0, keepdims=True)
        po = _dot(vt, p.astype(BF16))
        denom = ps if denom is None else denom + ps
        out = po if out is None else out + po
    return out / denom


def _swap_halves(q_bf16):
    return pltpu.roll(q_bf16.astype(F32), HEAD_DIM, 1).astype(BF16)


def _gqa_heads(q_of_pair, keys_by_group, values_t):
    outs = []
    for h in range(N_HEADS_A):
        g = h // GROUP_A
        q = q_of_pair(h // 2)
        if h % 2 != g:
            q = _swap_halves(q)
        o = _attend(q, keys_by_group[g], values_t, [None] * len(values_t))
        outs.append(o[g * HEAD_DIM:(g + 1) * HEAD_DIM])
    return jnp.concatenate(outs, axis=0)


def _ctx_attn_kernel(p_ref, oa_ref, ob_ref):
    ka = p_ref[:, COL_KA:COL_KA + LANES]
    va_t = [_transpose_bf16(p_ref[:, COL_VA:COL_VA + LANES])]
    keys_by_group = [[_keep_half(ka, g)] for g in range(N_KV_A)]
    oa = _gqa_heads(lambda i: p_ref[:, COL_QA + i * LANES:COL_QA + (i + 1) * LANES],
                    keys_by_group, va_t)
    oa_ref[...] = oa.T

    outs = []
    for i in range(N_HEADS_B // 2):
        q = p_ref[:, COL_QB + i * LANES:COL_QB + (i + 1) * LANES]
        k = p_ref[:, COL_KB + i * LANES:COL_KB + (i + 1) * LANES]
        vt = [_transpose_bf16(p_ref[:, COL_VB + i * LANES:COL_VB + (i + 1) * LANES])]
        for half in range(2):
            o = _attend(q, [_keep_half(k, half)], vt, [None])
            outs.append(o[half * HEAD_DIM:(half + 1) * HEAD_DIM])
    ob_ref[...] = jnp.concatenate(outs, axis=0).T


def _context_attention(proj, *, seq):
    t = proj.shape[0]
    return pl.pallas_call(
        _ctx_attn_kernel,
        out_shape=[jax.ShapeDtypeStruct((t, WIDTH_A), F32), jax.ShapeDtypeStruct((t, WIDTH_B), F32)],
        grid=(t // seq,),
        in_specs=[pl.BlockSpec((seq, IN_COLS), lambda i: (i, 0))],
        out_specs=[pl.BlockSpec((seq, WIDTH_A), lambda i: (i, 0)),
                   pl.BlockSpec((seq, WIDTH_B), lambda i: (i, 0))],
        compiler_params=_cparams(("arbitrary",)),
        name="context_attention",
    )(proj)


ATTN_SAFE_SHIFT = 40.0
ONES_ROWS = 16


def _round_up_bf16(x):
    return (x * (1.0 + 2.0 ** -6)).astype(BF16).astype(F32)


def _query_norm_bound(gain):
    return jnp.max(jnp.abs(gain)).reshape(1, 1).astype(F32)


def _ones_lane(g):
    return (1 - g) * HEAD_DIM


def _gqa_latent_kernel(q_ref, k_ref, v_ref, ck_ref, cv_ref, qmax_ref, o_ref,
                       kg_ref, ckg_ref, vt_ref, cvt_ref, shift_ref):
    lane_k = lax.broadcasted_iota(jnp.int32, (1, LANES), 1)

    @pl.when(pl.program_id(1) == 0)
    def _():
        k = k_ref[...]
        ck = ck_ref[0]
        vt = v_ref[...].astype(F32).T
        cvt = cv_ref[0].astype(F32).T
        for g in range(N_KV_A):
            kf = _keep_half(k, g).astype(F32)
            ckf = _keep_half(ck, g).astype(F32)
            ksq = jnp.maximum(jnp.max(jnp.sum(kf * kf, axis=1, keepdims=True), axis=0, keepdims=True),
                              jnp.max(jnp.sum(ckf * ckf, axis=1, keepdims=True), axis=0, keepdims=True))
            shift_ref[g] = jnp.broadcast_to(_round_up_bf16(qmax_ref[...] * jnp.sqrt(ksq)), shift_ref.shape[1:])
            kg_ref[g] = jnp.where(lane_k == _ones_lane(g), 1.0, kf).astype(BF16)
            ckg_ref[g] = jnp.where(lane_k == _ones_lane(g), 1.0, ckf).astype(BF16)
            rows = slice(g * HEAD_DIM, (g + 1) * HEAD_DIM)
            vt_ref[g] = jnp.concatenate([vt[rows], jnp.ones((ONES_ROWS, vt.shape[1]), F32)], axis=0).astype(BF16)
            cvt_ref[g] = jnp.concatenate([cvt[rows], jnp.ones((ONES_ROWS, cvt.shape[1]), F32)], axis=0).astype(BF16)

    tq = q_ref.shape[0]
    lane_q = lax.broadcasted_iota(jnp.int32, (GROUP_A * tq, LANES), 1)
    queries, shifts = [], []
    for g in range(N_KV_A):
        qs = []
        for j in range(GROUP_A):
            h = g * GROUP_A + j
            q = q_ref[:, (h // 2) * LANES:(h // 2 + 1) * LANES].astype(F32)
            qs.append(q if h % 2 == g else pltpu.roll(q, HEAD_DIM, 1))
        queries.append(jnp.where(lane_q // HEAD_DIM == g, jnp.concatenate(qs, axis=0), 0.0))
        shifts.append(shift_ref[g][0:1, 0:1])
    safe = jnp.max(jnp.maximum(shift_ref[0], shift_ref[1])) <= ATTN_SAFE_SHIFT

    def attend(g, p_lat, p_ctx):
        o = _dot(vt_ref[g], p_lat) + _dot(cvt_ref[g], p_ctx)
        o = o[:HEAD_DIM] / o[HEAD_DIM:HEAD_DIM + 1]
        heads = jnp.concatenate([o[:, j * tq:(j + 1) * tq] for j in range(GROUP_A)], axis=0)
        o_ref[:, g * GROUP_A * HEAD_DIM:(g + 1) * GROUP_A * HEAD_DIM] = heads.T

    def with_bound():
        for g in range(N_KV_A):
            qa = jnp.where(lane_q == _ones_lane(g), -shifts[g], queries[g]).astype(BF16)
            attend(g, jnp.exp(_dot_nt(kg_ref[g], qa)).astype(BF16), jnp.exp(_dot_nt(ckg_ref[g], qa)).astype(BF16))

    def with_row_max():
        for g in range(N_KV_A):
            qa = queries[g].astype(BF16)
            s_lat = _dot_nt(kg_ref[g], qa)
            s_ctx = _dot_nt(ckg_ref[g], qa)
            m = jnp.maximum(jnp.max(s_lat, axis=0, keepdims=True), jnp.max(s_ctx, axis=0, keepdims=True))
            attend(g, jnp.exp(s_lat - m).astype(BF16), jnp.exp(s_ctx - m).astype(BF16))

    pl.when(safe)(with_bound)
    pl.when(jnp.logical_not(safe))(with_row_max)


def _latent_gqa(proj, ctx_k, ctx_v, qmax, *, seq, tq):
    t = proj.shape[0]
    b = t // seq
    nq = seq // tq
    past = ctx_k.shape[1]
    return pl.pallas_call(
        _gqa_latent_kernel,
        out_shape=jax.ShapeDtypeStruct((t, WIDTH_A), F32),
        grid=(b, nq),
        in_specs=[pl.BlockSpec((tq, WIDTH_A), lambda bi, qi: (bi * nq + qi, 0)),
                  pl.BlockSpec((seq, LANES), lambda bi, qi: (bi, COL_KA // LANES)),
                  pl.BlockSpec((seq, LANES), lambda bi, qi: (bi, COL_VA // LANES)),
                  pl.BlockSpec((1, past, LANES), lambda bi, qi: (bi, 0, 0)),
                  pl.BlockSpec((1, past, LANES), lambda bi, qi: (bi, 0, 0)),
                  pl.BlockSpec((1, 1), lambda bi, qi: (0, 0))],
        out_specs=pl.BlockSpec((tq, WIDTH_A), lambda bi, qi: (bi * nq + qi, 0)),
        scratch_shapes=[pltpu.VMEM((N_KV_A, seq, LANES), BF16),
                        pltpu.VMEM((N_KV_A, past, LANES), BF16),
                        pltpu.VMEM((N_KV_A, HEAD_DIM + ONES_ROWS, seq), BF16),
                        pltpu.VMEM((N_KV_A, HEAD_DIM + ONES_ROWS, past), BF16),
                        pltpu.VMEM((N_KV_A, 8, LANES), F32)],
        compiler_params=_cparams(("arbitrary", "arbitrary")),
        name="latent_gqa",
    )(proj, proj, proj, ctx_k, ctx_v, qmax)


def _na_kernel(q_ref, k_ref, v_ref, ck_ref, cv_ref, bias_ref, bound_ref, o_ref, keys_ref, vt_ref, shift_ref,
               *, rows):
    i = pl.program_id(2)
    n_kblk = k_ref.shape[0] // NA_KBLK
    lane_k = lax.broadcasted_iota(jnp.int32, (1, LANES), 1)
    one = jnp.ones((), BF16)

    @pl.when(i == 0)
    def _():
        k = k_ref[...]
        ck = ck_ref[0]
        for half in range(2):
            kh = jnp.where(lane_k == _ones_lane(half), one, _keep_half(k, half))
            keys_ref[half, 0:n_kblk] = kh.reshape(n_kblk, NA_KBLK, LANES)
            ckh = _keep_half(ck, half)
            keys_ref[half, n_kblk] = jnp.where(lane_k == _ones_lane(half), one, ckh)
            ckf = ckh.astype(F32)
            ctx_norm = jnp.sqrt(jnp.max(jnp.sum(ckf * ckf, axis=1, keepdims=True), axis=0, keepdims=True))
            consts = bound_ref[0, half:half + 1, :]
            kmax = jnp.maximum(ctx_norm, consts[:, 2:3])
            shift_ref[half] = jnp.broadcast_to(_round_up_bf16(consts[:, 0:1] * kmax + consts[:, 1:2]),
                                               shift_ref.shape[1:])
        ones_rows = jnp.ones((ONES_ROWS, NA_KBLK), F32)
        vt = v_ref[...].astype(F32).T
        for j in range(n_kblk):
            vt_ref[j] = jnp.concatenate([vt[:, j * NA_KBLK:(j + 1) * NA_KBLK], ones_rows], axis=0).astype(BF16)
        vt_ref[n_kblk] = jnp.concatenate([cv_ref[0].astype(F32).T, ones_rows], axis=0).astype(BF16)

    q = q_ref[...]
    lane_q = lax.broadcasted_iota(jnp.int32, q.shape, 1)
    first = _na_first_key_block(i, rows)
    n_qblk = rows // NA_QROWS
    variant = jnp.where(i == 0, 0, jnp.where(i == n_qblk - 1, 2, 1))
    blocks = [first + j for j in range(NA_TK // NA_KBLK)] + [n_kblk]
    values_t = [vt_ref[blk] for blk in blocks]
    heads = []
    for half in range(2):
        keys = [keys_ref[half, blk] for blk in blocks]
        biases = [bias_ref[variant, half, j * NA_KBLK:(j + 1) * NA_KBLK, :] for j in range(NA_TK // NA_KBLK)] + [None]
        heads.append((shift_ref[half][0:1, 0:1], keys, biases))
    safe = jnp.max(jnp.maximum(shift_ref[0], shift_ref[1])) <= ATTN_SAFE_SHIFT

    def attend(probabilities):
        outs = []
        for half, ps in enumerate(probabilities):
            o = functools.reduce(lambda a, b: a + b, [_dot(vt, p) for vt, p in zip(values_t, ps)])
            outs.append(o[half * HEAD_DIM:(half + 1) * HEAD_DIM] / o[2 * HEAD_DIM:2 * HEAD_DIM + 1])
        o_ref[...] = jnp.concatenate(outs, axis=0).T

    def with_bound():
        probabilities = []
        for half, (shift, keys, biases) in enumerate(heads):
            qa = jnp.where(lane_q == _ones_lane(half), (-shift).astype(BF16), _keep_half(q, half))
            probabilities.append([jnp.exp(s).astype(BF16) for s in _scores(qa, keys, biases)])
        attend(probabilities)

    def with_row_max():
        all_scores = [_scores(_keep_half(q, half), keys, biases) for half, (_, keys, biases) in enumerate(heads)]
        probabilities = []
        for scores in all_scores:
            m = functools.reduce(jnp.maximum, [jnp.max(s, axis=0, keepdims=True) for s in scores])
            probabilities.append([jnp.exp(s - m).astype(BF16) for s in scores])
        attend(probabilities)

    pl.when(safe)(with_bound)
    pl.when(jnp.logical_not(safe))(with_row_max)


def _na_first_key_block(i, rows):
    per_qblock = NA_QROWS * GRID_W // NA_KBLK
    lead = (NA_KH // 2) * GRID_W // NA_KBLK
    return jnp.clip(per_qblock * i - lead, 0, (rows - NA_KROWS) * GRID_W // NA_KBLK)


def _latent_neighbourhood(proj, ctx_k, ctx_v, bias_t, bounds, *, seq):
    t = proj.shape[0]
    b = t // seq
    rows = seq // GRID_W
    nblk = rows // NA_QROWS
    n_kblk = seq // NA_KBLK
    past = ctx_k.shape[1]
    assert past == NA_KBLK
    grid = (N_HEADS_B // 2, b, nblk)
    in_specs = [pl.BlockSpec((NA_TQ, LANES), lambda hp, bi, i: (bi * nblk + i, COL_QB // LANES + hp)),
                pl.BlockSpec((seq, LANES), lambda hp, bi, i: (bi, COL_KB // LANES + hp)),
                pl.BlockSpec((seq, LANES), lambda hp, bi, i: (bi, COL_VB // LANES + hp)),
                pl.BlockSpec((1, past, LANES), lambda hp, bi, i: (bi, 0, hp)),
                pl.BlockSpec((1, past, LANES), lambda hp, bi, i: (bi, 0, hp)),
                pl.BlockSpec((3, 2, NA_TK, NA_TQ), lambda hp, bi, i: (0, hp, 0, 0)),
                pl.BlockSpec((1, 2, LANES), lambda hp, bi, i: (hp, 0, 0))]
    return pl.pallas_call(
        functools.partial(_na_kernel, rows=rows),
        out_shape=jax.ShapeDtypeStruct((t, WIDTH_B), F32),
        grid=grid,
        in_specs=in_specs,
        out_specs=pl.BlockSpec((NA_TQ, LANES), lambda hp, bi, i: (bi * nblk + i, hp)),
        scratch_shapes=[pltpu.VMEM((2, n_kblk + 1, NA_KBLK, LANES), BF16),
                        pltpu.VMEM((n_kblk + 1, 2 * HEAD_DIM + ONES_ROWS, NA_KBLK), BF16),
                        pltpu.VMEM((2, 8, LANES), F32)],
        compiler_params=_cparams(("arbitrary", "arbitrary", "arbitrary")),
        name="latent_neighbourhood",
    )(proj, proj, proj, ctx_k, ctx_v, bias_t, bounds)


def _neighbourhood_bounds(qn_b, kn_b, rpb):
    n_heads = rpb.shape[0]
    qmax = jnp.broadcast_to(_query_norm_bound(qn_b), (n_heads, 1))
    kmax = jnp.broadcast_to(_query_norm_bound(kn_b) * (HEAD_DIM ** 0.5), (n_heads, 1))
    bmax = jnp.maximum(jnp.max(rpb.reshape(n_heads, -1), axis=1, keepdims=True), 0.0).astype(F32)
    table = jnp.concatenate([qmax, bmax, kmax, jnp.zeros((n_heads, LANES - 3), F32)], axis=1)
    return table.reshape(n_heads // 2, 2, LANES)


def _neighbourhood_bias(rpb, rows):
    nblk = rows // NA_QROWS
    n_dr = 2 * NA_KH - 1
    n_dc = 2 * NA_KW - 1
    kc = np.arange(GRID_W)[:, None]
    qc = np.arange(GRID_W)[None, :]
    ws = np.clip(qc - NA_KW // 2, 0, GRID_W - NA_KW)
    col_ok = (kc >= ws) & (kc < ws + NA_KW)
    dc = np.clip(kc - qc + NA_KW - 1, 0, n_dc - 1)
    dc_onehot = (dc[None] == np.arange(n_dc)[:, None, None]).astype(np.float32)
    tiles = jnp.einsum('hab,bkq->hakq', rpb.astype(F32), jnp.asarray(dc_onehot),
                       precision=lax.Precision.HIGHEST)
    tiles = jnp.where(jnp.asarray(col_ok)[None, None], tiles, MASKED)
    masked_tile = jnp.full((rpb.shape[0], 1, GRID_W, GRID_W), MASKED, F32)
    tiles = jnp.concatenate([tiles, masked_tile], axis=1)
    pick = np.zeros((3, NA_KROWS, NA_QROWS, n_dr + 1), np.float32)
    for v, i in enumerate((0, 1, nblk - 1)):
        r0 = i * NA_QROWS
        ks = int(np.clip(r0 - NA_KH // 2, 0, rows - NA_KROWS))
        for kl in range(NA_KROWS):
            for ql in range(NA_QROWS):
                kr, qr = ks + kl, r0 + ql
                rs = int(np.clip(qr - NA_KH // 2, 0, rows - NA_KH))
                ok = rs <= kr < rs + NA_KH
                pick[v, kl, ql, (kr - qr + NA_KH - 1) if ok else n_dr] = 1.0
    bias = jnp.einsum('vkqa,hacd->vhkcqd', jnp.asarray(pick), tiles, precision=lax.Precision.HIGHEST)
    return bias.reshape(3, rpb.shape[0], NA_TK, NA_TQ)


def _merge_kernel(xp_ref, oap_ref, obp_ref, xs_ref, oas_ref, obs_ref, ona_ref, onb_ref, wo_ref,
                  g1_ref, sh2_ref, sc2_ref, n2_ref, y_ref, h_ref, hp_ref, *, ctx_tiles):
    def one_stream(x_ref, oa_ref, ob_ref):
        na = (_rms(oa_ref[...]) * ona_ref[...]).astype(BF16)
        nb = (_rms(ob_ref[...]) * onb_ref[...]).astype(BF16)
        mix = _dot(na, wo_ref[0:WIDTH_A, :]) + _dot(nb, wo_ref[WIDTH_A:WIDTH_A + WIDTH_B, :])
        y = x_ref[...] + g1_ref[0] * mix
        y_ref[...] = y
        h = _rms(y) * n2_ref[...]
        h = h * (1.0 + sc2_ref[0]) + sh2_ref[0]
        h_ref[...] = h
        hp_ref[...] = _pack_rows(h)

    i = pl.program_id(0)
    pl.when(i < ctx_tiles)(lambda: one_stream(xp_ref, oap_ref, obp_ref))
    pl.when(i >= ctx_tiles)(lambda: one_stream(xs_ref, oas_ref, obs_ref))


def _merge(ctx, lat, on_a, on_b, w_out_bf, gate1, shift2, scale2, norm2, *, tm, lat_seq):
    t_c, d = ctx[0].shape
    t_l = lat[0].shape[0]
    ctx_tiles = t_c // tm
    lat_tiles_per_batch = lat_seq // tm

    def ctx_map(i):
        return (jnp.minimum(i, ctx_tiles - 1), 0)

    def lat_map(i):
        return (jnp.maximum(i - ctx_tiles, 0), 0)

    def mod_map(i):
        return (jnp.where(i < ctx_tiles, 0, 1 + (i - ctx_tiles) // lat_tiles_per_batch), 0, 0)

    def stream_specs(index_map):
        return [pl.BlockSpec((tm, d), index_map),
                pl.BlockSpec((tm, WIDTH_A), index_map),
                pl.BlockSpec((tm, WIDTH_B), index_map)]

    return pl.pallas_call(
        functools.partial(_merge_kernel, ctx_tiles=ctx_tiles),
        out_shape=[jax.ShapeDtypeStruct((t_c + t_l, d), F32)] * 2
        + [jax.ShapeDtypeStruct((t_c + t_l, d // 2), jnp.int32)],
        grid=((t_c + t_l) // tm,),
        in_specs=stream_specs(ctx_map) + stream_specs(lat_map) + [
            pl.BlockSpec((1, WIDTH_A), lambda i: (0, 0)),
            pl.BlockSpec((1, WIDTH_B), lambda i: (0, 0)),
            pl.BlockSpec((WIDTH_A + WIDTH_B, d), lambda i: (0, 0)),
            pl.BlockSpec((1, 1, d), mod_map),
            pl.BlockSpec((1, 1, d), mod_map),
            pl.BlockSpec((1, 1, d), mod_map),
            pl.BlockSpec((1, d), lambda i: (0, 0))],
        out_specs=[pl.BlockSpec((tm, d), lambda i: (i, 0))] * 2 + [pl.BlockSpec((tm, d // 2), lambda i: (i, 0))],
        compiler_params=_cparams(("arbitrary",)),
        name="merge_out_proj",
    )(*ctx, *lat, on_a, on_b, w_out_bf, gate1, shift2, scale2, norm2)


def _first_index_of_max(x, iota):
    mx = jnp.max(x, axis=0, keepdims=True)
    idx = jnp.min(jnp.where(x == mx, iota, float(x.shape[0])), axis=0, keepdims=True)
    return mx, iota == idx


def _router_gates(h, wr_hi, wr_lo, rbias):
    h_hi = h.astype(BF16)
    h_lo = (h - h_hi.astype(F32)).astype(BF16)
    logits = _dot_nt(wr_hi, h_hi) + (_dot_nt(wr_lo, h_hi) + _dot_nt(wr_hi, h_lo))
    scores = _sigmoid(logits)
    sel = scores + rbias
    tm = sel.shape[1]
    iota_g = lax.broadcasted_iota(jnp.int32, (GROUP_SIZE, tm), 0).astype(F32)
    group_scores = []
    for g in range(N_GROUPS):
        grp = sel[g * GROUP_SIZE:(g + 1) * GROUP_SIZE]
        m1, first = _first_index_of_max(grp, iota_g)
        m2 = jnp.max(jnp.where(first, -jnp.inf, grp), axis=0, keepdims=True)
        group_scores.append(m1 + m2)
    gs = jnp.concatenate(group_scores, axis=0)
    iota_n = lax.broadcasted_iota(jnp.int32, (N_GROUPS, tm), 0).astype(F32)
    group_on = jnp.zeros((N_GROUPS, tm), F32)
    for _ in range(TOPK_GROUPS):
        _, pick = _first_index_of_max(gs, iota_n)
        group_on = jnp.where(pick, 1.0, group_on)
        gs = jnp.where(pick, -jnp.inf, gs)
    expert_on = jnp.concatenate(
        [jnp.broadcast_to(group_on[g:g + 1], (GROUP_SIZE, tm)) for g in range(N_GROUPS)], axis=0)
    cand = jnp.where(expert_on > 0.0, sel, -jnp.inf)
    iota_e = lax.broadcasted_iota(jnp.int32, (N_EXPERTS, tm), 0).astype(F32)
    w = jnp.zeros((N_EXPERTS, tm), F32)
    chosen = jnp.zeros((N_EXPERTS, tm), F32)
    for _ in range(TOP_K):
        _, pick = _first_index_of_max(cand, iota_e)
        w = jnp.where(pick, scores, w)
        chosen = jnp.where(pick, 1.0, chosen)
        cand = jnp.where(pick, -jnp.inf, cand)
    return w / jnp.sum(w, axis=0, keepdims=True) * ROUTED_SCALE, chosen


MOE_TS = 1024
MOE_ROUTE_TM = 1024
MOE_ROW_TM = 512


def _route_kernel(h_ref, wrh_ref, wrl_ref, rb_ref, tri_ref, gates_ref, rank_ref, count_ref):
    @pl.when(pl.program_id(0) == 0)
    def _():
        count_ref[...] = jnp.zeros_like(count_ref)

    gates, chosen = _router_gates(h_ref[...], wrh_ref[...], wrl_ref[...], rb_ref[...])
    gates_ref[...] = gates
    before = _dot(chosen.astype(BF16), tri_ref[...])
    seen = count_ref[...]
    rank_ref[...] = jnp.where(chosen > 0.0, before + seen[:, 0:1], -1.0)
    count_ref[...] = seen + jnp.sum(chosen, axis=1, keepdims=True)


def _route(h_all, wr_hi, wr_lo, rbias):
    t, d = h_all.shape
    tm = MOE_ROUTE_TM
    tri = jnp.asarray(np.triu(np.ones((tm, tm), np.float32), k=1), BF16)
    return pl.pallas_call(
        _route_kernel,
        out_shape=[jax.ShapeDtypeStruct((N_EXPERTS, t), F32),
                   jax.ShapeDtypeStruct((N_EXPERTS, t), F32),
                   jax.ShapeDtypeStruct((N_EXPERTS, LANES), F32)],
        grid=(t // tm,),
        in_specs=[pl.BlockSpec((tm, d), lambda i: (i, 0)),
                  pl.BlockSpec((N_EXPERTS, d), lambda i: (0, 0)),
                  pl.BlockSpec((N_EXPERTS, d), lambda i: (0, 0)),
                  pl.BlockSpec((N_EXPERTS, 1), lambda i: (0, 0)),
                  pl.BlockSpec((tm, tm), lambda i: (0, 0))],
        out_specs=[pl.BlockSpec((N_EXPERTS, tm), lambda i: (0, i)),
                   pl.BlockSpec((N_EXPERTS, tm), lambda i: (0, i)),
                   pl.BlockSpec((N_EXPERTS, LANES), lambda i: (0, 0))],
        compiler_params=_cparams(("arbitrary",)),
        name="moe_route",
    )(h_all, wr_hi, wr_lo, rbias, tri)


def _slots_kernel(gates_ref, rank_ref, off_ref, pos_ref, gtok_ref):
    gates = gates_ref[...]
    rank = rank_ref[...]
    tm = gates.shape[1]
    slot = off_ref[...] + rank
    left = jnp.where(rank >= 0.0, 1.0, 0.0)
    iota_e = lax.broadcasted_iota(jnp.int32, (N_EXPERTS, tm), 0).astype(F32)
    pos_rows, gate_rows = [], []
    for _ in range(TOP_K):
        _, pick = _first_index_of_max(left, iota_e)
        pos_rows.append(jnp.sum(jnp.where(pick, slot, 0.0), axis=0, keepdims=True))
        gate_rows.append(jnp.sum(jnp.where(pick, gates, 0.0), axis=0, keepdims=True))
        left = jnp.where(pick, 0.0, left)
    pos_ref[...] = jnp.concatenate(pos_rows, axis=0).astype(jnp.int32)
    pad = jnp.zeros((LANES - TOP_K, tm), F32)
    gtok_ref[...] = jnp.concatenate(gate_rows + [pad], axis=0).T


def _slots(gates_t, rank_t, off):
    t = gates_t.shape[1]
    tm = MOE_ROUTE_TM
    return pl.pallas_call(
        _slots_kernel,
        out_shape=[jax.ShapeDtypeStruct((TOP_K, t), jnp.int32), jax.ShapeDtypeStruct((t, LANES), F32)],
        grid=(t // tm,),
        in_specs=[pl.BlockSpec((N_EXPERTS, tm), lambda i: (0, i)),
                  pl.BlockSpec((N_EXPERTS, tm), lambda i: (0, i)),
                  pl.BlockSpec((N_EXPERTS, 1), lambda i: (0, 0))],
        out_specs=[pl.BlockSpec((TOP_K, tm), lambda i: (0, i)),
                   pl.BlockSpec((tm, LANES), lambda i: (i, 0))],
        compiler_params=_cparams(("arbitrary",)),
        name="moe_slots",
    )(gates_t, rank_t, off)


SC_CORES = 2
SC_SUBCORES = 16
SC_ROWS = 64


def _dispatch(hp_all, slot_of):
    t, width = hp_all.shape
    n_slots = slot_of.shape[0]
    n_pad = n_slots - TOP_K * t
    workers = SC_CORES * SC_SUBCORES
    per_worker = t // workers
    pad_per_worker = n_pad // workers
    assert per_worker * workers == t and per_worker % SC_ROWS == 0
    assert pad_per_worker * workers == n_pad and pad_per_worker % SC_ROWS == 0
    mesh = plsc.VectorSubcoreMesh(core_axis_name="core", subcore_axis_name="subcore")

    @functools.partial(
        pl.kernel, mesh=mesh,
        out_type=jax.ShapeDtypeStruct((n_slots, width), jnp.int32),
        scratch_types=[pltpu.VMEM((SC_ROWS,), jnp.int32),
                       pltpu.VMEM((SC_ROWS, width), jnp.int32),
                       pltpu.SemaphoreType.DMA],
    )
    def scatter_rows(h_hbm, slot_hbm, out_hbm, idx_v, rows_v, sem):
        worker = lax.axis_index("subcore") * SC_CORES + lax.axis_index("core")
        base = worker * per_worker

        @pl.loop(0, per_worker // SC_ROWS)
        def _(j):
            first = base + j * SC_ROWS
            pltpu.sync_copy(h_hbm.at[pl.ds(first, SC_ROWS)], rows_v)
            for k in range(TOP_K):
                pltpu.sync_copy(slot_hbm.at[pl.ds(k * t + first, SC_ROWS)], idx_v)
                pltpu.async_copy(rows_v, out_hbm.at[idx_v], sem).wait()

        pltpu.sync_copy(h_hbm.at[pl.ds(0, SC_ROWS)], rows_v)
        pad_base = TOP_K * t + worker * pad_per_worker

        @pl.loop(0, pad_per_worker // SC_ROWS)
        def _(j):
            pltpu.sync_copy(slot_hbm.at[pl.ds(pad_base + j * SC_ROWS, SC_ROWS)], idx_v)
            pltpu.async_copy(rows_v, out_hbm.at[idx_v], sem).wait()

    return scatter_rows(hp_all, slot_of)


def _experts_kernel(te_ref, xs_ref, wg_ref, wu_ref, wd_ref, ys_ref, wg_bf, wu_bf, wd_bf):
    i = pl.program_id(0)

    @pl.when((i == 0) | (te_ref[i] != te_ref[jnp.maximum(i, 1) - 1]))
    def _():
        wg_bf[...] = wg_ref[0].astype(BF16)
        wu_bf[...] = wu_ref[0].astype(BF16)
        wd_bf[...] = wd_ref[0].astype(BF16)

    left, right = _unpack_rows(xs_ref[...])
    x = jnp.concatenate([left, right], axis=1).astype(BF16)
    g = _dot(x, wg_bf[...])
    u = _dot(x, wu_bf[...])
    act = (g * _sigmoid(g)) * u
    ys_ref[...] = _pack_rows(_dot(act.astype(BF16), wd_bf[...]))


def _experts(xs, tile_expert, w_gate, w_up, w_down):
    n_slots, width = xs.shape
    d = 2 * width
    ts = MOE_TS
    return pl.pallas_call(
        _experts_kernel,
        out_shape=jax.ShapeDtypeStruct((n_slots, width), jnp.int32),
        grid_spec=pltpu.PrefetchScalarGridSpec(
            num_scalar_prefetch=1,
            grid=(n_slots // ts,),
            in_specs=[pl.BlockSpec((ts, width), lambda i, te: (i, 0)),
                      pl.BlockSpec((1, d, D_EXPERT), lambda i, te: (te[i], 0, 0)),
                      pl.BlockSpec((1, d, D_EXPERT), lambda i, te: (te[i], 0, 0)),
                      pl.BlockSpec((1, D_EXPERT, d), lambda i, te: (te[i], 0, 0))],
            out_specs=pl.BlockSpec((ts, width), lambda i, te: (i, 0)),
            scratch_shapes=[pltpu.VMEM((d, D_EXPERT), BF16), pltpu.VMEM((d, D_EXPERT), BF16),
                            pltpu.VMEM((D_EXPERT, d), BF16)]),
        compiler_params=_cparams(("arbitrary",)),
        name="moe_experts",
    )(tile_expert, xs, w_gate, w_up, w_down)


def _gather_slots(y_slots, slot_of, t):
    width = y_slots.shape[1]
    workers = SC_CORES * SC_SUBCORES
    per_worker = t // workers
    n_blocks = (per_worker // SC_ROWS) * TOP_K
    assert per_worker * workers == t and per_worker % SC_ROWS == 0 and n_blocks % 2 == 0
    mesh = plsc.VectorSubcoreMesh(core_axis_name="core", subcore_axis_name="subcore")

    @functools.partial(
        pl.kernel, mesh=mesh,
        out_type=jax.ShapeDtypeStruct((TOP_K * t, width), jnp.int32),
        scratch_types=[pltpu.VMEM((SC_ROWS,), jnp.int32), pltpu.VMEM((SC_ROWS,), jnp.int32),
                       pltpu.VMEM((SC_ROWS, width), jnp.int32), pltpu.VMEM((SC_ROWS, width), jnp.int32),
                       pltpu.SemaphoreType.DMA, pltpu.SemaphoreType.DMA],
    )
    def gather_rows(ys_hbm, slot_hbm, out_hbm, idx0, idx1, rows0, rows1, sem0, sem1):
        worker = lax.axis_index("subcore") * SC_CORES + lax.axis_index("core")
        base = worker * per_worker

        def first_row(n):
            return (n % TOP_K) * t + base + (n // TOP_K) * SC_ROWS

        def start(n, idx_v, rows_v, sem):
            pltpu.sync_copy(slot_hbm.at[pl.ds(first_row(n), SC_ROWS)], idx_v)
            pltpu.async_copy(ys_hbm.at[idx_v], rows_v, sem)

        def finish(n, idx_v, rows_v, sem):
            pltpu.make_async_copy(ys_hbm.at[idx_v], rows_v, sem).wait()
            pltpu.sync_copy(rows_v, out_hbm.at[pl.ds(first_row(n), SC_ROWS)])

        start(0, idx0, rows0, sem0)

        @pl.loop(0, n_blocks, step=2)
        def _(n):
            start(n + 1, idx1, rows1, sem1)
            finish(n, idx0, rows0, sem0)

            @pl.when(n + 2 < n_blocks)
            def _():
                start(n + 2, idx0, rows0, sem0)

            finish(n + 1, idx1, rows1, sem1)

    return gather_rows(y_slots, slot_of)


def _combine_kernel(y_ref, h_ref, g2_ref, gtok_ref, rows_ref, wgs_ref, wus_ref, wds_ref, o_ref):
    h_left, h_right = _unpack_rows(h_ref[...])
    h = jnp.concatenate([h_left, h_right], axis=1).astype(BF16)
    gs = _dot(h, wgs_ref[...])
    us = _dot(h, wus_ref[...])
    shared = _dot(((gs * _sigmoid(gs)) * us).astype(BF16), wds_ref[...])

    gtok = gtok_ref[...]
    acc_left = acc_right = None
    for k in range(TOP_K):
        left, right = _unpack_rows(rows_ref[k])
        gate = gtok[:, k:k + 1]
        acc_left = gate * left if acc_left is None else acc_left + gate * left
        acc_right = gate * right if acc_right is None else acc_right + gate * right
    routed = jnp.concatenate([acc_left, acc_right], axis=1)
    o_ref[...] = y_ref[...] + g2_ref[0] * (routed + shared)


def _combine(y_all, hp_all, gate2, gtok, rows, wgs, wus, wds, *, first_token, tokens, seq):
    d = y_all.shape[1]
    width = hp_all.shape[1]
    tm = MOE_ROW_TM
    tile0 = first_token // tm
    nb = gate2.shape[0]
    tiles_per_batch = seq // tm

    def mod_map(i):
        return ((i // tiles_per_batch) if nb > 1 else 0, 0, 0)

    return pl.pallas_call(
        _combine_kernel,
        out_shape=jax.ShapeDtypeStruct((tokens, d), F32),
        grid=(tokens // tm,),
        in_specs=[pl.BlockSpec((tm, d), lambda i: (tile0 + i, 0)),
                  pl.BlockSpec((tm, width), lambda i: (tile0 + i, 0)),
                  pl.BlockSpec((1, 1, d), mod_map),
                  pl.BlockSpec((tm, LANES), lambda i: (tile0 + i, 0)),
                  pl.BlockSpec((TOP_K, tm, width), lambda i: (0, tile0 + i, 0)),
                  pl.BlockSpec((d, D_SHARED), lambda i: (0, 0)),
                  pl.BlockSpec((d, D_SHARED), lambda i: (0, 0)),
                  pl.BlockSpec((D_SHARED, d), lambda i: (0, 0))],
        out_specs=pl.BlockSpec((tm, d), lambda i: (i, 0)),
        compiler_params=_cparams(("arbitrary",)),
        name="moe_combine",
    )(y_all, hp_all, gate2, gtok, rows, wgs, wus, wds)


def _expert_layout(counts, n_tiles):
    cnt = counts.astype(jnp.int32)
    tiles = (cnt + (MOE_TS - 1)) // MOE_TS
    last_tile = jnp.cumsum(tiles)
    off = (last_tile - tiles) * MOE_TS
    pad_lo = off + cnt
    pad_hi = (off + tiles * MOE_TS).at[N_EXPERTS - 1].set(n_tiles * MOE_TS)
    pad_cnt = pad_hi - pad_lo
    pad_last = jnp.cumsum(pad_cnt)
    shift = pad_lo - (pad_last - pad_cnt)
    j = jnp.arange(N_EXPERTS * MOE_TS, dtype=jnp.int32)
    past = (pad_last[None, :-1] <= j[:, None]).astype(jnp.int32)
    pad_slots = j + shift[0] + jnp.sum(past * (shift[1:] - shift[:-1])[None, :], axis=1)
    tile_ids = jnp.arange(n_tiles, dtype=jnp.int32)
    tile_expert = jnp.minimum(
        jnp.sum((last_tile[None, :] <= tile_ids[:, None]).astype(jnp.int32), axis=1), N_EXPERTS - 1)
    return off, pad_slots, tile_expert


def _rope_tables(n_tokens):
    t = jnp.arange(n_tokens)
    row = (t // GRID_W).astype(F32)
    col = (t % GRID_W).astype(F32)
    nf = HEAD_DIM // 4
    freqs = ROPE_THETA ** (-jnp.arange(nf, dtype=F32) / nf)
    ang_r = row[:, None] * freqs
    ang_c = col[:, None] * freqs
    cos = jnp.concatenate([jnp.cos(ang_r)] * 2 + [jnp.cos(ang_c)] * 2, axis=1)
    sin = jnp.concatenate([-jnp.sin(ang_r), jnp.sin(ang_r), -jnp.sin(ang_c), jnp.sin(ang_c)], axis=1)
    reps = LANES // HEAD_DIM
    return jnp.tile(cos, (1, reps)), jnp.tile(sin, (1, reps))


def _head_gains(qn_a, kn_a, qn_b, kn_b):
    ones = jnp.ones((HEAD_DIM,), F32)
    parts = ([qn_a] * N_HEADS_A + [kn_a] * N_KV_A + [ones] * N_KV_A
             + [qn_b] * N_HEADS_B + [kn_b] * N_HEADS_B + [ones] * N_HEADS_B)
    return jnp.concatenate(parts).reshape(1, IN_COLS).astype(F32)


def _same_head_indicator():
    i = np.arange(LANES)
    return jnp.asarray((i[:, None] // HEAD_DIM) == (i[None, :] // HEAD_DIM), BF16)


def _token_major(cache):
    b, h, s, hd = cache.shape
    return cache.transpose(0, 2, 1, 3).reshape(b, s, h * hd).astype(BF16)


def kernel(x_prompt, x_sample, cache_k_a, cache_v_a, cache_k_b, cache_v_b, c, c_ctx, w_mod, b_mod, norm1, norm2, w_in, qn_a, kn_a, qn_b, kn_b, rpb, on_a, on_b, w_out, w_router, router_bias, w_gate_e, w_up_e, w_down_e, w_gate_s, w_up_s, w_down_s):
    depth = w_mod.shape[0]
    assert depth == 1
    l = 0
    bp, sp, d = x_prompt.shape
    bs, ss, _ = x_sample.shape

    cvec = jnp.concatenate([c_ctx[None, :], c], axis=0)
    rows = -(-cvec.shape[0] // 8) * 8
    cvec = jnp.pad(cvec, ((0, rows - cvec.shape[0]), (0, 0)))
    mod = _adaln(cvec, w_mod[l], b_mod[l])
    mod_p = [m.reshape(1, 1, d) for m in jnp.split(mod[0:1], 6, axis=-1)]
    mod_s = [m.reshape(bs, 1, d) for m in jnp.split(mod[1:1 + bs], 6, axis=-1)]
    mod_all = [m.reshape(1 + bs, 1, d) for m in jnp.split(mod[0:1 + bs], 6, axis=-1)]

    w_in_bf = w_in[l].astype(BF16)
    w_out_bf = w_out[l].astype(BF16)
    gain = _head_gains(qn_a[l], kn_a[l], qn_b[l], kn_b[l])
    seg = _same_head_indicator()
    n1 = norm1[l].reshape(1, d)
    n2 = norm2[l].reshape(1, d)
    ona = on_a[l].reshape(1, WIDTH_A)
    onb = on_b[l].reshape(1, WIDTH_B)
    wr_t = w_router[l].T
    wr_hi = wr_t.astype(BF16)
    wr_lo = (wr_t - wr_hi.astype(F32)).astype(BF16)
    rbias = router_bias[l].reshape(N_EXPERTS, 1).astype(F32)
    wgs = w_gate_s[l].astype(BF16)
    wus = w_up_s[l].astype(BF16)
    wds = w_down_s[l].astype(BF16)
    t_p = bp * sp
    t_s = bs * ss
    t_all = t_p + t_s

    xp = x_prompt.reshape(t_p, d)
    proj_p, st_ka, st_va, st_kb, st_vb = _project(
        xp, mod_p[0], mod_p[1], n1, w_in_bf, gain, seg, None, tm=sp, seq=sp, states=True)
    oa_p, ob_p = _context_attention(proj_p, seq=sp)

    xs = x_sample.reshape(t_s, d)
    proj_s, = _project(xs, mod_s[0], mod_s[1], n1, w_in_bf, gain, seg, _rope_tables(ss),
                       tm=1024, seq=ss, states=False)
    oa_s = _latent_gqa(proj_s, _token_major(cache_k_a[:, l]), _token_major(cache_v_a[:, l]),
                       _query_norm_bound(qn_a[l]), seq=ss, tq=256)
    bias_t = _neighbourhood_bias(rpb[l], ss // GRID_W)
    ob_s = _latent_neighbourhood(proj_s, _token_major(cache_k_b[:, l]), _token_major(cache_v_b[:, l]),
                                 bias_t, _neighbourhood_bounds(qn_b[l], kn_b[l], rpb[l]), seq=ss)

    y1_all, h_all, hp_all = _merge((xp, oa_p, ob_p), (xs, oa_s, ob_s), ona, onb, w_out_bf,
                                   mod_all[2], mod_all[3], mod_all[4], n2, tm=512, lat_seq=ss)
    gates_t, rank_t, counts = _route(h_all, wr_hi, wr_lo, rbias)
    n_tiles = t_all * TOP_K // MOE_TS + N_EXPERTS
    off, pad_slots, tile_expert = _expert_layout(counts[:, 0], n_tiles)
    pos, gtok = _slots(gates_t, rank_t, off.astype(F32).reshape(N_EXPERTS, 1))
    slot_of = pos.reshape(TOP_K * t_all)
    x_slots = _dispatch(hp_all, jnp.concatenate([slot_of, pad_slots]))
    y_slots = _experts(x_slots, tile_expert, w_gate_e[l], w_up_e[l], w_down_e[l])
    rows = _gather_slots(y_slots, slot_of, t_all).reshape(TOP_K, t_all, d // 2)
    y_p = _combine(y1_all, hp_all, mod_p[5], gtok, rows, wgs, wus, wds,
                   first_token=0, tokens=t_p, seq=sp)
    y_s = _combine(y1_all, hp_all, mod_s[5], gtok, rows, wgs, wus, wds,
                   first_token=t_p, tokens=t_s, seq=ss)

    return (y_p.reshape(bp, sp, d), y_s.reshape(bs, ss, d), st_ka, st_va, st_kb, st_vb)
```

```python
import functools

import numpy as np
import jax
import jax.numpy as jnp
from jax import lax
from jax.experimental import pallas as pl
from jax.experimental.pallas import tpu as pltpu
from jax.experimental.pallas import tpu_sc as plsc

F32 = jnp.float32
BF16 = jnp.bfloat16

D_MODEL = 1024
HEAD_DIM = 64
N_HEADS_A = 8
N_KV_A = 2
GROUP_A = N_HEADS_A // N_KV_A
N_HEADS_B = 8
WIDTH_A = N_HEADS_A * HEAD_DIM
WIDTH_B = N_HEADS_B * HEAD_DIM
KV_WIDTH_A = N_KV_A * HEAD_DIM
IN_COLS = WIDTH_A + 2 * KV_WIDTH_A + 3 * WIDTH_B
GRID_W = 64
ROPE_THETA = 10000.0
NA_KH = 8
NA_KW = 16
N_EXPERTS = 64
N_GROUPS = 8
GROUP_SIZE = N_EXPERTS // N_GROUPS
TOPK_GROUPS = 4
TOP_K = 8
D_EXPERT = 256
D_SHARED = 256
ROUTED_SCALE = 2.5
EPS = 1e-6

LANES = 128
MXU_DIM = 256
MASKED = -1e30

COL_QA = 0
COL_KA = WIDTH_A
COL_VA = COL_KA + KV_WIDTH_A
COL_QB = COL_VA + KV_WIDTH_A
COL_KB = COL_QB + WIDTH_B
COL_VB = COL_KB + WIDTH_B

NA_QROWS = 8
NA_KROWS = 2 * NA_KH
NA_TQ = NA_QROWS * GRID_W
NA_TK = NA_KROWS * GRID_W
NA_KBLK = 256

VMEM_LIMIT = 56 * 1024 * 1024


def _cparams(sem):
    return pltpu.CompilerParams(dimension_semantics=sem, vmem_limit_bytes=VMEM_LIMIT)


def _dot(a, b):
    return jnp.dot(a, b, preferred_element_type=F32)


def _dot_nt(a, b):
    return lax.dot_general(a, b, (((1,), (1,)), ((), ())), preferred_element_type=F32)


def _sigmoid(x):
    return 1.0 / (1.0 + jnp.exp(-x))


def _rms(x):
    return x * lax.rsqrt(jnp.mean(x * x, axis=-1, keepdims=True) + EPS)


def _pack_rows(x):
    n = x.shape[1] // 2
    hi = lax.bitcast_convert_type(x[:, :n].astype(BF16).astype(F32), jnp.int32)
    lo = lax.bitcast_convert_type(x[:, n:].astype(BF16).astype(F32), jnp.int32)
    return hi | lax.shift_right_logical(lo, 16)


def _unpack_rows(w):
    left = lax.bitcast_convert_type(w & jnp.int32(-65536), F32)
    right = lax.bitcast_convert_type(lax.shift_left(w, 16), F32)
    return left, right


def _mod_kernel(c_ref, w_ref, b_ref, o_ref):
    c = c_ref[...]
    s = c * _sigmoid(c)
    o_ref[...] = jnp.dot(s, w_ref[...], preferred_element_type=F32,
                         precision=lax.Precision.HIGHEST) + b_ref[...]


def _adaln(cvec, w_mod, b_mod):
    rows, d = cvec.shape
    n = w_mod.shape[1]
    tn = 512
    return pl.pallas_call(
        _mod_kernel,
        out_shape=jax.ShapeDtypeStruct((rows, n), F32),
        grid=(n // tn,),
        in_specs=[pl.BlockSpec((rows, d), lambda j: (0, 0)),
                  pl.BlockSpec((d, tn), lambda j: (0, j)),
                  pl.BlockSpec((1, tn), lambda j: (0, j))],
        out_specs=pl.BlockSpec((rows, tn), lambda j: (0, j)),
        compiler_params=_cparams(("arbitrary",)),
        name="adaln_mod",
    )(cvec, w_mod, b_mod.reshape(1, n))


_PROJ_CHUNKS = (
    [(COL_QA + i * LANES, LANES, True, True) for i in range(WIDTH_A // LANES)]
    + [(COL_KA, LANES, True, True), (COL_VA, LANES, False, False)]
    + [(COL_QB + i * LANES, LANES, True, False) for i in range(WIDTH_B // LANES)]
    + [(COL_KB + i * LANES, LANES, True, False) for i in range(WIDTH_B // LANES)]
    + [(COL_VB + i * LANES, LANES, False, False) for i in range(WIDTH_B // LANES)]
)


def _proj_kernel(*refs, rope, states):
    x_ref, sh_ref, sc_ref, n1_ref, w_ref, gain_ref, seg_ref = refs[:7]
    pos = 7
    if rope:
        cos_ref, sin_ref = refs[pos:pos + 2]
        pos += 2
    out_ref = refs[pos]
    pos += 1
    if states:
        ka_ref, va_ref, kb_ref, vb_ref = refs[pos:pos + 4]

    x = x_ref[...]
    h = _rms(x) * n1_ref[...]
    h = h * (1.0 + sc_ref[0]) + sh_ref[0]
    p = _dot(h.astype(BF16), w_ref[...])
    seg = seg_ref[...]
    if rope:
        cos = cos_ref[...]
        sin = sin_ref[...]
        lane = lax.broadcasted_iota(jnp.int32, cos.shape, 1)
        first_half = (lane % (HEAD_DIM // 2)) < (HEAD_DIM // 4)

    for c0, w, normed, roped in _PROJ_CHUNKS:
        pc = p[:, c0:c0 + w]
        if normed:
            sq = pc * pc
            hi = sq.astype(BF16)
            lo = (sq - hi.astype(F32)).astype(BF16)
            ss = _dot(hi, seg) + _dot(lo, seg)
            pc = pc * lax.rsqrt(ss * (1.0 / HEAD_DIM) + EPS) * gain_ref[:, c0:c0 + w]
        if states:
            if c0 == COL_KA:
                for hh in range(N_KV_A):
                    ka_ref[0, 0, hh] = pc[:, hh * HEAD_DIM:(hh + 1) * HEAD_DIM]
            elif c0 == COL_VA:
                for hh in range(N_KV_A):
                    va_ref[0, 0, hh] = pc[:, hh * HEAD_DIM:(hh + 1) * HEAD_DIM]
            elif COL_KB <= c0 < COL_VB:
                base = (c0 - COL_KB) // HEAD_DIM
                for hh in range(LANES // HEAD_DIM):
                    kb_ref[0, 0, base + hh] = pc[:, hh * HEAD_DIM:(hh + 1) * HEAD_DIM]
            elif c0 >= COL_VB:
                base = (c0 - COL_VB) // HEAD_DIM
                for hh in range(LANES // HEAD_DIM):
                    vb_ref[0, 0, base + hh] = pc[:, hh * HEAD_DIM:(hh + 1) * HEAD_DIM]
        if rope and roped:
            partner = jnp.where(first_half,
                                pltpu.roll(pc, LANES - HEAD_DIM // 4, 1),
                                pltpu.roll(pc, HEAD_DIM // 4, 1))
            pc = pc * cos + partner * sin
        if c0 < COL_KA or COL_QB <= c0 < COL_KB:
            pc = pc * (HEAD_DIM ** -0.5)
        out_ref[:, c0:c0 + w] = pc.astype(BF16)


def _project(x2d, shift, scale, norm1, w_in_bf, gain, seg, rope_tabs, *, tm, seq, states):
    t, d = x2d.shape
    nb = shift.shape[0]
    tiles_per_batch = seq // tm
    rope = rope_tabs is not None

    def mod_map(i):
        return ((i // tiles_per_batch) if nb > 1 else 0, 0, 0)

    in_specs = [pl.BlockSpec((tm, d), lambda i: (i, 0)),
                pl.BlockSpec((1, 1, d), mod_map),
                pl.BlockSpec((1, 1, d), mod_map),
                pl.BlockSpec((1, d), lambda i: (0, 0)),
                pl.BlockSpec((d, IN_COLS), lambda i: (0, 0)),
                pl.BlockSpec((1, IN_COLS), lambda i: (0, 0)),
                pl.BlockSpec((LANES, LANES), lambda i: (0, 0))]
    args = [x2d, shift, scale, norm1, w_in_bf, gain, seg]
    if rope:
        in_specs += [pl.BlockSpec((tm, LANES), lambda i: (i % tiles_per_batch, 0))] * 2
        args += list(rope_tabs)
    out_shape = [jax.ShapeDtypeStruct((t, IN_COLS), BF16)]
    out_specs = [pl.BlockSpec((tm, IN_COLS), lambda i: (i, 0))]
    if states:
        assert tm == seq
        b = t // seq
        for nh in (N_KV_A, N_KV_A, N_HEADS_B, N_HEADS_B):
            out_shape.append(jax.ShapeDtypeStruct((b, 1, nh, seq, HEAD_DIM), F32))
            out_specs.append(pl.BlockSpec((1, 1, nh, seq, HEAD_DIM), lambda i: (i, 0, 0, 0, 0)))
    return pl.pallas_call(
        functools.partial(_proj_kernel, rope=rope, states=states),
        out_shape=out_shape,
        grid=(t // tm,),
        in_specs=in_specs,
        out_specs=out_specs,
        compiler_params=_cparams(("arbitrary",)),
        name="proj_states" if states else "proj_rope",
    )(*args)


def _lane_half(shape):
    return lax.broadcasted_iota(jnp.int32, shape, 1) // HEAD_DIM


def _keep_half(x, half):
    return jnp.where(_lane_half(x.shape) == half, x, jnp.zeros_like(x))


def _transpose_bf16(x):
    return x.astype(F32).T.astype(BF16)


def _attend(q, keys, values_t, biases):
    return _softmax_av(_scores(q, keys, biases), values_t)


def _scores(q, keys, biases):
    scores = []
    for k, b in zip(keys, biases):
        s = _dot_nt(k, q)
        if b is not None:
            s = s + b
        scores.append(s)
    return scores


def _softmax_av(scores, values_t):
    m = functools.reduce(jnp.maximum, [jnp.max(s, axis=0, keepdims=True) for s in scores])
    denom = None
    out = None
    for s, vt in zip(scores, values_t):
        p = jnp.exp(s - m)
        ps = jnp.sum(p, axis=0, keepdims=True)
        po = _dot(vt, p.astype(BF16))
        denom = ps if denom is None else denom + ps
        out = po if out is None else out + po
    return out / denom


def _swap_halves(q_bf16):
    return pltpu.roll(q_bf16.astype(F32), HEAD_DIM, 1).astype(BF16)


def _gqa_heads(q_of_pair, keys_by_group, values_t):
    outs = []
    for h in range(N_HEADS_A):
        g = h // GROUP_A
        q = q_of_pair(h // 2)
        if h % 2 != g:
            q = _swap_halves(q)
        o = _attend(q, keys_by_group[g], values_t, [None] * len(values_t))
        outs.append(o[g * HEAD_DIM:(g + 1) * HEAD_DIM])
    return jnp.concatenate(outs, axis=0)


def _ctx_attn_kernel(p_ref, oa_ref, ob_ref):
    ka = p_ref[:, COL_KA:COL_KA + LANES]
    va_t = [_transpose_bf16(p_ref[:, COL_VA:COL_VA + LANES])]
    keys_by_group = [[_keep_half(ka, g)] for g in range(N_KV_A)]
    oa = _gqa_heads(lambda i: p_ref[:, COL_QA + i * LANES:COL_QA + (i + 1) * LANES],
                    keys_by_group, va_t)
    oa_ref[...] = oa.T

    outs = []
    for i in range(N_HEADS_B // 2):
        q = p_ref[:, COL_QB + i * LANES:COL_QB + (i + 1) * LANES]
        k = p_ref[:, COL_KB + i * LANES:COL_KB + (i + 1) * LANES]
        vt = [_transpose_bf16(p_ref[:, COL_VB + i * LANES:COL_VB + (i + 1) * LANES])]
        for half in range(2):
            o = _attend(q, [_keep_half(k, half)], vt, [None])
            outs.append(o[half * HEAD_DIM:(half + 1) * HEAD_DIM])
    ob_ref[...] = jnp.concatenate(outs, axis=0).T


def _context_attention(proj, *, seq):
    t = proj.shape[0]
    return pl.pallas_call(
        _ctx_attn_kernel,
        out_shape=[jax.ShapeDtypeStruct((t, WIDTH_A), F32), jax.ShapeDtypeStruct((t, WIDTH_B), F32)],
        grid=(t // seq,),
        in_specs=[pl.BlockSpec((seq, IN_COLS), lambda i: (i, 0))],
        out_specs=[pl.BlockSpec((seq, WIDTH_A), lambda i: (i, 0)),
                   pl.BlockSpec((seq, WIDTH_B), lambda i: (i, 0))],
        compiler_params=_cparams(("arbitrary",)),
        name="context_attention",
    )(proj)


ATTN_SAFE_SHIFT = 40.0
ONES_ROWS = 16


def _round_up_bf16(x):
    return (x * (1.0 + 2.0 ** -6)).astype(BF16).astype(F32)


def _query_norm_bound(gain):
    return jnp.max(jnp.abs(gain)).reshape(1, 1).astype(F32)


def _ones_lane(g):
    return (1 - g) * HEAD_DIM


def _gqa_latent_kernel(q_ref, k_ref, v_ref, ck_ref, cv_ref, qmax_ref, o_ref,
                       kg_ref, ckg_ref, vt_ref, cvt_ref, shift_ref):
    lane_k = lax.broadcasted_iota(jnp.int32, (1, LANES), 1)

    @pl.when(pl.program_id(1) == 0)
    def _():
        k = k_ref[...]
        ck = ck_ref[0]
        vt = v_ref[...].astype(F32).T
        cvt = cv_ref[0].astype(F32).T
        for g in range(N_KV_A):
            kf = _keep_half(k, g).astype(F32)
            ckf = _keep_half(ck, g).astype(F32)
            ksq = jnp.maximum(jnp.max(jnp.sum(kf * kf, axis=1, keepdims=True), axis=0, keepdims=True),
                              jnp.max(jnp.sum(ckf * ckf, axis=1, keepdims=True), axis=0, keepdims=True))
            shift_ref[g] = jnp.broadcast_to(_round_up_bf16(qmax_ref[...] * jnp.sqrt(ksq)), shift_ref.shape[1:])
            kg_ref[g] = jnp.where(lane_k == _ones_lane(g), 1.0, kf).astype(BF16)
            ckg_ref[g] = jnp.where(lane_k == _ones_lane(g), 1.0, ckf).astype(BF16)
            rows = slice(g * HEAD_DIM, (g + 1) * HEAD_DIM)
            vt_ref[g] = jnp.concatenate([vt[rows], jnp.ones((ONES_ROWS, vt.shape[1]), F32)], axis=0).astype(BF16)
            cvt_ref[g] = jnp.concatenate([cvt[rows], jnp.ones((ONES_ROWS, cvt.shape[1]), F32)], axis=0).astype(BF16)

    tq = q_ref.shape[0]
    lane_q = lax.broadcasted_iota(jnp.int32, (GROUP_A * tq, LANES), 1)
    queries, shifts = [], []
    for g in range(N_KV_A):
        qs = []
        for j in range(GROUP_A):
            h = g * GROUP_A + j
            q = q_ref[:, (h // 2) * LANES:(h // 2 + 1) * LANES].astype(F32)
            qs.append(q if h % 2 == g else pltpu.roll(q, HEAD_DIM, 1))
        queries.append(jnp.where(lane_q // HEAD_DIM == g, jnp.concatenate(qs, axis=0), 0.0))
        shifts.append(shift_ref[g][0:1, 0:1])
    safe = jnp.max(jnp.maximum(shift_ref[0], shift_ref[1])) <= ATTN_SAFE_SHIFT

    def attend(g, p_lat, p_ctx):
        o = _dot(vt_ref[g], p_lat) + _dot(cvt_ref[g], p_ctx)
        o = o[:HEAD_DIM] / o[HEAD_DIM:HEAD_DIM + 1]
        heads = jnp.concatenate([o[:, j * tq:(j + 1) * tq] for j in range(GROUP_A)], axis=0)
        o_ref[:, g * GROUP_A * HEAD_DIM:(g + 1) * GROUP_A * HEAD_DIM] = heads.T

    def with_bound():
        for g in range(N_KV_A):
            qa = jnp.where(lane_q == _ones_lane(g), -shifts[g], queries[g]).astype(BF16)
            attend(g, jnp.exp(_dot_nt(kg_ref[g], qa)).astype(BF16), jnp.exp(_dot_nt(ckg_ref[g], qa)).astype(BF16))

    def with_row_max():
        for g in range(N_KV_A):
            qa = queries[g].astype(BF16)
            s_lat = _dot_nt(kg_ref[g], qa)
            s_ctx = _dot_nt(ckg_ref[g], qa)
            m = jnp.maximum(jnp.max(s_lat, axis=0, keepdims=True), jnp.max(s_ctx, axis=0, keepdims=True))
            attend(g, jnp.exp(s_lat - m).astype(BF16), jnp.exp(s_ctx - m).astype(BF16))

    pl.when(safe)(with_bound)
    pl.when(jnp.logical_not(safe))(with_row_max)


def _latent_gqa(proj, ctx_k, ctx_v, qmax, *, seq, tq):
    t = proj.shape[0]
    b = t // seq
    nq = seq // tq
    past = ctx_k.shape[1]
    return pl.pallas_call(
        _gqa_latent_kernel,
        out_shape=jax.ShapeDtypeStruct((t, WIDTH_A), F32),
        grid=(b, nq),
        in_specs=[pl.BlockSpec((tq, WIDTH_A), lambda bi, qi: (bi * nq + qi, 0)),
                  pl.BlockSpec((seq, LANES), lambda bi, qi: (bi, COL_KA // LANES)),
                  pl.BlockSpec((seq, LANES), lambda bi, qi: (bi, COL_VA // LANES)),
                  pl.BlockSpec((1, past, LANES), lambda bi, qi: (bi, 0, 0)),
                  pl.BlockSpec((1, past, LANES), lambda bi, qi: (bi, 0, 0)),
                  pl.BlockSpec((1, 1), lambda bi, qi: (0, 0))],
        out_specs=pl.BlockSpec((tq, WIDTH_A), lambda bi, qi: (bi * nq + qi, 0)),
        scratch_shapes=[pltpu.VMEM((N_KV_A, seq, LANES), BF16),
                        pltpu.VMEM((N_KV_A, past, LANES), BF16),
                        pltpu.VMEM((N_KV_A, HEAD_DIM + ONES_ROWS, seq), BF16),
                        pltpu.VMEM((N_KV_A, HEAD_DIM + ONES_ROWS, past), BF16),
                        pltpu.VMEM((N_KV_A, 8, LANES), F32)],
        compiler_params=_cparams(("arbitrary", "arbitrary")),
        name="latent_gqa",
    )(proj, proj, proj, ctx_k, ctx_v, qmax)


def _na_kernel(q_ref, k_ref, v_ref, ck_ref, cv_ref, bias_ref, bound_ref, o_ref, keys_ref, vt_ref, shift_ref,
               *, rows):
    i = pl.program_id(2)
    n_kblk = k_ref.shape[0] // NA_KBLK
    lane_k = lax.broadcasted_iota(jnp.int32, (1, LANES), 1)
    one = jnp.ones((), BF16)

    @pl.when(i == 0)
    def _():
        k = k_ref[...]
        ck = ck_ref[0]
        for half in range(2):
            kh = jnp.where(lane_k == _ones_lane(half), one, _keep_half(k, half))
            keys_ref[half, 0:n_kblk] = kh.reshape(n_kblk, NA_KBLK, LANES)
            ckh = _keep_half(ck, half)
            keys_ref[half, n_kblk] = jnp.where(lane_k == _ones_lane(half), one, ckh)
            ckf = ckh.astype(F32)
            ctx_norm = jnp.sqrt(jnp.max(jnp.sum(ckf * ckf, axis=1, keepdims=True), axis=0, keepdims=True))
            consts = bound_ref[0, half:half + 1, :]
            kmax = jnp.maximum(ctx_norm, consts[:, 2:3])
            shift_ref[half] = jnp.broadcast_to(_round_up_bf16(consts[:, 0:1] * kmax + consts[:, 1:2]),
                                               shift_ref.shape[1:])
        ones_rows = jnp.ones((ONES_ROWS, NA_KBLK), F32)
        vt = v_ref[...].astype(F32).T
        for j in range(n_kblk):
            vt_ref[j] = jnp.concatenate([vt[:, j * NA_KBLK:(j + 1) * NA_KBLK], ones_rows], axis=0).astype(BF16)
        vt_ref[n_kblk] = jnp.concatenate([cv_ref[0].astype(F32).T, ones_rows], axis=0).astype(BF16)

    q = q_ref[...]
    lane_q = lax.broadcasted_iota(jnp.int32, q.shape, 1)
    first = _na_first_key_block(i, rows)
    n_qblk = rows // NA_QROWS
    variant = jnp.where(i == 0, 0, jnp.where(i == n_qblk - 1, 2, 1))
    blocks = [first + j for j in range(NA_TK // NA_KBLK)] + [n_kblk]
    values_t = [vt_ref[blk] for blk in blocks]
    heads = []
    for half in range(2):
        keys = [keys_ref[half, blk] for blk in blocks]
        biases = [bias_ref[variant, half, j * NA_KBLK:(j + 1) * NA_KBLK, :] for j in range(NA_TK // NA_KBLK)] + [None]
        heads.append((shift_ref[half][0:1, 0:1], keys, biases))
    safe = jnp.max(jnp.maximum(shift_ref[0], shift_ref[1])) <= ATTN_SAFE_SHIFT

    def attend(probabilities):
        outs = []
        for half, ps in enumerate(probabilities):
            o = functools.reduce(lambda a, b: a + b, [_dot(vt, p) for vt, p in zip(values_t, ps)])
            outs.append(o[half * HEAD_DIM:(half + 1) * HEAD_DIM] / o[2 * HEAD_DIM:2 * HEAD_DIM + 1])
        o_ref[...] = jnp.concatenate(outs, axis=0).T

    def with_bound():
        probabilities = []
        for half, (shift, keys, biases) in enumerate(heads):
            qa = jnp.where(lane_q == _ones_lane(half), (-shift).astype(BF16), _keep_half(q, half))
            probabilities.append([jnp.exp(s).astype(BF16) for s in _scores(qa, keys, biases)])
        attend(probabilities)

    def with_row_max():
        all_scores = [_scores(_keep_half(q, half), keys, biases) for half, (_, keys, biases) in enumerate(heads)]
        probabilities = []
        for scores in all_scores:
            m = functools.reduce(jnp.maximum, [jnp.max(s, axis=0, keepdims=True) for s in scores])
            probabilities.append([jnp.exp(s - m).astype(BF16) for s in scores])
        attend(probabilities)

    pl.when(safe)(with_bound)
    pl.when(jnp.logical_not(safe))(with_row_max)


def _na_first_key_block(i, rows):
    per_qblock = NA_QROWS * GRID_W // NA_KBLK
    lead = (NA_KH // 2) * GRID_W // NA_KBLK
    return jnp.clip(per_qblock * i - lead, 0, (rows - NA_KROWS) * GRID_W // NA_KBLK)


def _latent_neighbourhood(proj, ctx_k, ctx_v, bias_t, bounds, *, seq):
    t = proj.shape[0]
    b = t // seq
    rows = seq // GRID_W
    nblk = rows // NA_QROWS
    n_kblk = seq // NA_KBLK
    past = ctx_k.shape[1]
    assert past == NA_KBLK
    grid = (N_HEADS_B // 2, b, nblk)
    in_specs = [pl.BlockSpec((NA_TQ, LANES), lambda hp, bi, i: (bi * nblk + i, COL_QB // LANES + hp)),
                pl.BlockSpec((seq, LANES), lambda hp, bi, i: (bi, COL_KB // LANES + hp)),
                pl.BlockSpec((seq, LANES), lambda hp, bi, i: (bi, COL_VB // LANES + hp)),
                pl.BlockSpec((1, past, LANES), lambda hp, bi, i: (bi, 0, hp)),
                pl.BlockSpec((1, past, LANES), lambda hp, bi, i: (bi, 0, hp)),
                pl.BlockSpec((3, 2, NA_TK, NA_TQ), lambda hp, bi, i: (0, hp, 0, 0)),
                pl.BlockSpec((1, 2, LANES), lambda hp, bi, i: (hp, 0, 0))]
    return pl.pallas_call(
        functools.partial(_na_kernel, rows=rows),
        out_shape=jax.ShapeDtypeStruct((t, WIDTH_B), F32),
        grid=grid,
        in_specs=in_specs,
        out_specs=pl.BlockSpec((NA_TQ, LANES), lambda hp, bi, i: (bi * nblk + i, hp)),
        scratch_shapes=[pltpu.VMEM((2, n_kblk + 1, NA_KBLK, LANES), BF16),
                        pltpu.VMEM((n_kblk + 1, 2 * HEAD_DIM + ONES_ROWS, NA_KBLK), BF16),
                        pltpu.VMEM((2, 8, LANES), F32)],
        compiler_params=_cparams(("arbitrary", "arbitrary", "arbitrary")),
        name="latent_neighbourhood",
    )(proj, proj, proj, ctx_k, ctx_v, bias_t, bounds)


def _neighbourhood_bounds(qn_b, kn_b, rpb):
    n_heads = rpb.shape[0]
    qmax = jnp.broadcast_to(_query_norm_bound(qn_b), (n_heads, 1))
    kmax = jnp.broadcast_to(_query_norm_bound(kn_b) * (HEAD_DIM ** 0.5), (n_heads, 1))
    bmax = jnp.maximum(jnp.max(rpb.reshape(n_heads, -1), axis=1, keepdims=True), 0.0).astype(F32)
    table = jnp.concatenate([qmax, bmax, kmax, jnp.zeros((n_heads, LANES - 3), F32)], axis=1)
    return table.reshape(n_heads // 2, 2, LANES)


def _neighbourhood_bias(rpb, rows):
    nblk = rows // NA_QROWS
    n_dr = 2 * NA_KH - 1
    n_dc = 2 * NA_KW - 1
    kc = np.arange(GRID_W)[:, None]
    qc = np.arange(GRID_W)[None, :]
    ws = np.clip(qc - NA_KW // 2, 0, GRID_W - NA_KW)
    col_ok = (kc >= ws) & (kc < ws + NA_KW)
    dc = np.clip(kc - qc + NA_KW - 1, 0, n_dc - 1)
    dc_onehot = (dc[None] == np.arange(n_dc)[:, None, None]).astype(np.float32)
    tiles = jnp.einsum('hab,bkq->hakq', rpb.astype(F32), jnp.asarray(dc_onehot),
                       precision=lax.Precision.HIGHEST)
    tiles = jnp.where(jnp.asarray(col_ok)[None, None], tiles, MASKED)
    masked_tile = jnp.full((rpb.shape[0], 1, GRID_W, GRID_W), MASKED, F32)
    tiles = jnp.concatenate([tiles, masked_tile], axis=1)
    pick = np.zeros((3, NA_KROWS, NA_QROWS, n_dr + 1), np.float32)
    for v, i in enumerate((0, 1, nblk - 1)):
        r0 = i * NA_QROWS
        ks = int(np.clip(r0 - NA_KH // 2, 0, rows - NA_KROWS))
        for kl in range(NA_KROWS):
            for ql in range(NA_QROWS):
                kr, qr = ks + kl, r0 + ql
                rs = int(np.clip(qr - NA_KH // 2, 0, rows - NA_KH))
                ok = rs <= kr < rs + NA_KH
                pick[v, kl, ql, (kr - qr + NA_KH - 1) if ok else n_dr] = 1.0
    bias = jnp.einsum('vkqa,hacd->vhkcqd', jnp.asarray(pick), tiles, precision=lax.Precision.HIGHEST)
    return bias.reshape(3, rpb.shape[0], NA_TK, NA_TQ)


def _merge_kernel(xp_ref, oap_ref, obp_ref, xs_ref, oas_ref, obs_ref, ona_ref, onb_ref, wo_ref,
                  g1_ref, sh2_ref, sc2_ref, n2_ref, wrh_ref, wrl_ref, rb_ref, tri_ref,
                  y_ref, hp_ref, gates_ref, rank_ref, count_ref, *, ctx_tiles):
    i = pl.program_id(0)

    @pl.when(i == 0)
    def _():
        count_ref[...] = jnp.zeros_like(count_ref)

    def one_stream(x_ref, oa_ref, ob_ref):
        na = (_rms(oa_ref[...]) * ona_ref[...]).astype(BF16)
        nb = (_rms(ob_ref[...]) * onb_ref[...]).astype(BF16)
        mix = _dot(na, wo_ref[0:WIDTH_A, :]) + _dot(nb, wo_ref[WIDTH_A:WIDTH_A + WIDTH_B, :])
        y = x_ref[...] + g1_ref[0] * mix
        y_ref[...] = y
        h = _rms(y) * n2_ref[...]
        h = h * (1.0 + sc2_ref[0]) + sh2_ref[0]
        hp_ref[...] = _pack_rows(h)
        gates, chosen = _router_gates(h, wrh_ref[...], wrl_ref[...], rb_ref[...])
        gates_ref[...] = gates
        before = _dot(chosen.astype(BF16), tri_ref[...])
        seen = count_ref[...]
        rank_ref[...] = jnp.where(chosen > 0.0, before + seen[:, 0:1], -1.0)
        count_ref[...] = seen + jnp.sum(chosen, axis=1, keepdims=True)

    pl.when(i < ctx_tiles)(lambda: one_stream(xp_ref, oap_ref, obp_ref))
    pl.when(i >= ctx_tiles)(lambda: one_stream(xs_ref, oas_ref, obs_ref))


def _merge(ctx, lat, on_a, on_b, w_out_bf, gate1, shift2, scale2, norm2, wr_hi, wr_lo, rbias, *, tm, lat_seq):
    t_c, d = ctx[0].shape
    t_l = lat[0].shape[0]
    t = t_c + t_l
    tri = jnp.asarray(np.triu(np.ones((tm, tm), np.float32), k=1), BF16)
    ctx_tiles = t_c // tm
    lat_tiles_per_batch = lat_seq // tm

    def ctx_map(i):
        return (jnp.minimum(i, ctx_tiles - 1), 0)

    def lat_map(i):
        return (jnp.maximum(i - ctx_tiles, 0), 0)

    def mod_map(i):
        return (jnp.where(i < ctx_tiles, 0, 1 + (i - ctx_tiles) // lat_tiles_per_batch), 0, 0)

    def stream_specs(index_map):
        return [pl.BlockSpec((tm, d), index_map),
                pl.BlockSpec((tm, WIDTH_A), index_map),
                pl.BlockSpec((tm, WIDTH_B), index_map)]

    return pl.pallas_call(
        functools.partial(_merge_kernel, ctx_tiles=ctx_tiles),
        out_shape=[jax.ShapeDtypeStruct((t, d), F32),
                   jax.ShapeDtypeStruct((t, d // 2), jnp.int32),
                   jax.ShapeDtypeStruct((N_EXPERTS, t), F32),
                   jax.ShapeDtypeStruct((N_EXPERTS, t), F32),
                   jax.ShapeDtypeStruct((N_EXPERTS, LANES), F32)],
        grid=(t // tm,),
        in_specs=stream_specs(ctx_map) + stream_specs(lat_map) + [
            pl.BlockSpec((1, WIDTH_A), lambda i: (0, 0)),
            pl.BlockSpec((1, WIDTH_B), lambda i: (0, 0)),
            pl.BlockSpec((WIDTH_A + WIDTH_B, d), lambda i: (0, 0)),
            pl.BlockSpec((1, 1, d), mod_map),
            pl.BlockSpec((1, 1, d), mod_map),
            pl.BlockSpec((1, 1, d), mod_map),
            pl.BlockSpec((1, d), lambda i: (0, 0)),
            pl.BlockSpec((N_EXPERTS, d), lambda i: (0, 0)),
            pl.BlockSpec((N_EXPERTS, d), lambda i: (0, 0)),
            pl.BlockSpec((N_EXPERTS, 1), lambda i: (0, 0)),
            pl.BlockSpec((tm, tm), lambda i: (0, 0))],
        out_specs=[pl.BlockSpec((tm, d), lambda i: (i, 0)),
                   pl.BlockSpec((tm, d // 2), lambda i: (i, 0)),
                   pl.BlockSpec((N_EXPERTS, tm), lambda i: (0, i)),
                   pl.BlockSpec((N_EXPERTS, tm), lambda i: (0, i)),
                   pl.BlockSpec((N_EXPERTS, LANES), lambda i: (0, 0))],
        compiler_params=_cparams(("arbitrary",)),
        name="merge_route",
    )(*ctx, *lat, on_a, on_b, w_out_bf, gate1, shift2, scale2, norm2, wr_hi, wr_lo, rbias, tri)


def _first_index_of_max(x, iota):
    mx = jnp.max(x, axis=0, keepdims=True)
    idx = jnp.min(jnp.where(x == mx, iota, float(x.shape[0])), axis=0, keepdims=True)
    return mx, iota == idx


def _router_gates(h, wr_hi, wr_lo, rbias):
    h_hi = h.astype(BF16)
    h_lo = (h - h_hi.astype(F32)).astype(BF16)
    logits = _dot_nt(wr_hi, h_hi) + (_dot_nt(wr_lo, h_hi) + _dot_nt(wr_hi, h_lo))
    scores = _sigmoid(logits)
    sel = scores + rbias
    tm = sel.shape[1]
    iota_g = lax.broadcasted_iota(jnp.int32, (GROUP_SIZE, tm), 0).astype(F32)
    group_scores = []
    for g in range(N_GROUPS):
        grp = sel[g * GROUP_SIZE:(g + 1) * GROUP_SIZE]
        m1, first = _first_index_of_max(grp, iota_g)
        m2 = jnp.max(jnp.where(first, -jnp.inf, grp), axis=0, keepdims=True)
        group_scores.append(m1 + m2)
    gs = jnp.concatenate(group_scores, axis=0)
    iota_n = lax.broadcasted_iota(jnp.int32, (N_GROUPS, tm), 0).astype(F32)
    group_on = jnp.zeros((N_GROUPS, tm), F32)
    for _ in range(TOPK_GROUPS):
        _, pick = _first_index_of_max(gs, iota_n)
        group_on = jnp.where(pick, 1.0, group_on)
        gs = jnp.where(pick, -jnp.inf, gs)
    expert_on = jnp.concatenate(
        [jnp.broadcast_to(group_on[g:g + 1], (GROUP_SIZE, tm)) for g in range(N_GROUPS)], axis=0)
    cand = jnp.where(expert_on > 0.0, sel, -jnp.inf)
    iota_e = lax.broadcasted_iota(jnp.int32, (N_EXPERTS, tm), 0).astype(F32)
    w = jnp.zeros((N_EXPERTS, tm), F32)
    chosen = jnp.zeros((N_EXPERTS, tm), F32)
    for _ in range(TOP_K):
        _, pick = _first_index_of_max(cand, iota_e)
        w = jnp.where(pick, scores, w)
        chosen = jnp.where(pick, 1.0, chosen)
        cand = jnp.where(pick, -jnp.inf, cand)
    return w / jnp.sum(w, axis=0, keepdims=True) * ROUTED_SCALE, chosen


MOE_TS = 1024
MOE_ROUTE_TM = 1024
MOE_ROW_TM = 512


def _slots_kernel(gates_ref, rank_ref, off_ref, pos_ref, gtok_ref):
    gates = gates_ref[...]
    rank = rank_ref[...]
    tm = gates.shape[1]
    slot = off_ref[...] + rank
    left = jnp.where(rank >= 0.0, 1.0, 0.0)
    iota_e = lax.broadcasted_iota(jnp.int32, (N_EXPERTS, tm), 0).astype(F32)
    pos_rows, gate_rows = [], []
    for _ in range(TOP_K):
        _, pick = _first_index_of_max(left, iota_e)
        pos_rows.append(jnp.sum(jnp.where(pick, slot, 0.0), axis=0, keepdims=True))
        gate_rows.append(jnp.sum(jnp.where(pick, gates, 0.0), axis=0, keepdims=True))
        left = jnp.where(pick, 0.0, left)
    pos_ref[...] = jnp.concatenate(pos_rows, axis=0).astype(jnp.int32)
    pad = jnp.zeros((LANES - TOP_K, tm), F32)
    gtok_ref[...] = jnp.concatenate(gate_rows + [pad], axis=0).T


def _slots(gates_t, rank_t, off):
    t = gates_t.shape[1]
    tm = MOE_ROUTE_TM
    return pl.pallas_call(
        _slots_kernel,
        out_shape=[jax.ShapeDtypeStruct((TOP_K, t), jnp.int32), jax.ShapeDtypeStruct((t, LANES), F32)],
        grid=(t // tm,),
        in_specs=[pl.BlockSpec((N_EXPERTS, tm), lambda i: (0, i)),
                  pl.BlockSpec((N_EXPERTS, tm), lambda i: (0, i)),
                  pl.BlockSpec((N_EXPERTS, 1), lambda i: (0, 0))],
        out_specs=[pl.BlockSpec((TOP_K, tm), lambda i: (0, i)),
                   pl.BlockSpec((tm, LANES), lambda i: (i, 0))],
        compiler_params=_cparams(("arbitrary",)),
        name="moe_slots",
    )(gates_t, rank_t, off)


SC_CORES = 2
SC_SUBCORES = 16
SC_ROWS = 64


def _dispatch(hp_all, slot_of):
    t, width = hp_all.shape
    n_slots = slot_of.shape[0]
    n_pad = n_slots - TOP_K * t
    workers = SC_CORES * SC_SUBCORES
    per_worker = t // workers
    pad_per_worker = n_pad // workers
    assert per_worker * workers == t and per_worker % SC_ROWS == 0
    assert pad_per_worker * workers == n_pad and pad_per_worker % SC_ROWS == 0
    mesh = plsc.VectorSubcoreMesh(core_axis_name="core", subcore_axis_name="subcore")

    @functools.partial(
        pl.kernel, mesh=mesh,
        out_type=jax.ShapeDtypeStruct((n_slots, width), jnp.int32),
        scratch_types=[pltpu.VMEM((SC_ROWS,), jnp.int32),
                       pltpu.VMEM((SC_ROWS, width), jnp.int32),
                       pltpu.SemaphoreType.DMA],
    )
    def scatter_rows(h_hbm, slot_hbm, out_hbm, idx_v, rows_v, sem):
        worker = lax.axis_index("subcore") * SC_CORES + lax.axis_index("core")
        base = worker * per_worker

        @pl.loop(0, per_worker // SC_ROWS)
        def _(j):
            first = base + j * SC_ROWS
            pltpu.sync_copy(h_hbm.at[pl.ds(first, SC_ROWS)], rows_v)
            for k in range(TOP_K):
                pltpu.sync_copy(slot_hbm.at[pl.ds(k * t + first, SC_ROWS)], idx_v)
                pltpu.async_copy(rows_v, out_hbm.at[idx_v], sem).wait()

        pltpu.sync_copy(h_hbm.at[pl.ds(0, SC_ROWS)], rows_v)
        pad_base = TOP_K * t + worker * pad_per_worker

        @pl.loop(0, pad_per_worker // SC_ROWS)
        def _(j):
            pltpu.sync_copy(slot_hbm.at[pl.ds(pad_base + j * SC_ROWS, SC_ROWS)], idx_v)
            pltpu.async_copy(rows_v, out_hbm.at[idx_v], sem).wait()

    return scatter_rows(hp_all, slot_of)


def _experts_kernel(te_ref, xs_ref, wg_ref, wu_ref, wd_ref, ys_ref, wgu_bf, wd_bf):
    i = pl.program_id(0)

    @pl.when((i == 0) | (te_ref[i] != te_ref[jnp.maximum(i, 1) - 1]))
    def _():
        wgu_bf[:, 0:D_EXPERT] = wg_ref[0].astype(BF16)
        wgu_bf[:, D_EXPERT:2 * D_EXPERT] = wu_ref[0].astype(BF16)
        wd_bf[...] = wd_ref[0].astype(BF16)

    left, right = _unpack_rows(xs_ref[...])
    x = jnp.concatenate([left, right], axis=1).astype(BF16)
    gu = _dot(x, wgu_bf[...])
    g = gu[:, 0:D_EXPERT]
    u = gu[:, D_EXPERT:2 * D_EXPERT]
    act = (g * _sigmoid(g)) * u
    ys_ref[...] = _pack_rows(_dot(act.astype(BF16), wd_bf[...]))


def _experts(xs, tile_expert, w_gate, w_up, w_down):
    n_slots, width = xs.shape
    d = 2 * width
    ts = MOE_TS
    return pl.pallas_call(
        _experts_kernel,
        out_shape=jax.ShapeDtypeStruct((n_slots, width), jnp.int32),
        grid_spec=pltpu.PrefetchScalarGridSpec(
            num_scalar_prefetch=1,
            grid=(n_slots // ts,),
            in_specs=[pl.BlockSpec((ts, width), lambda i, te: (i, 0)),
                      pl.BlockSpec((1, d, D_EXPERT), lambda i, te: (te[i], 0, 0)),
                      pl.BlockSpec((1, d, D_EXPERT), lambda i, te: (te[i], 0, 0)),
                      pl.BlockSpec((1, D_EXPERT, d), lambda i, te: (te[i], 0, 0))],
            out_specs=pl.BlockSpec((ts, width), lambda i, te: (i, 0)),
            scratch_shapes=[pltpu.VMEM((d, 2 * D_EXPERT), BF16), pltpu.VMEM((D_EXPERT, d), BF16)]),
        compiler_params=_cparams(("arbitrary",)),
        name="moe_experts",
    )(tile_expert, xs, w_gate, w_up, w_down)


def _gather_slots(y_slots, slot_of, t):
    width = y_slots.shape[1]
    workers = SC_CORES * SC_SUBCORES
    per_worker = t // workers
    n_blocks = (per_worker // SC_ROWS) * TOP_K
    assert per_worker * workers == t and per_worker % SC_ROWS == 0 and n_blocks % 2 == 0
    mesh = plsc.VectorSubcoreMesh(core_axis_name="core", subcore_axis_name="subcore")

    @functools.partial(
        pl.kernel, mesh=mesh,
        out_type=jax.ShapeDtypeStruct((TOP_K * t, width), jnp.int32),
        scratch_types=[pltpu.VMEM((SC_ROWS,), jnp.int32), pltpu.VMEM((SC_ROWS,), jnp.int32),
                       pltpu.VMEM((SC_ROWS, width), jnp.int32), pltpu.VMEM((SC_ROWS, width), jnp.int32),
                       pltpu.SemaphoreType.DMA, pltpu.SemaphoreType.DMA],
    )
    def gather_rows(ys_hbm, slot_hbm, out_hbm, idx0, idx1, rows0, rows1, sem0, sem1):
        worker = lax.axis_index("subcore") * SC_CORES + lax.axis_index("core")
        base = worker * per_worker

        def first_row(n):
            return (n % TOP_K) * t + base + (n // TOP_K) * SC_ROWS

        def start(n, idx_v, rows_v, sem):
            pltpu.sync_copy(slot_hbm.at[pl.ds(first_row(n), SC_ROWS)], idx_v)
            pltpu.async_copy(ys_hbm.at[idx_v], rows_v, sem)

        def finish(n, idx_v, rows_v, sem):
            pltpu.make_async_copy(ys_hbm.at[idx_v], rows_v, sem).wait()
            pltpu.sync_copy(rows_v, out_hbm.at[pl.ds(first_row(n), SC_ROWS)])

        start(0, idx0, rows0, sem0)

        @pl.loop(0, n_blocks, step=2)
        def _(n):
            start(n + 1, idx1, rows1, sem1)
            finish(n, idx0, rows0, sem0)

            @pl.when(n + 2 < n_blocks)
            def _():
                start(n + 2, idx0, rows0, sem0)

            finish(n + 1, idx1, rows1, sem1)

    return gather_rows(y_slots, slot_of)


def _combine_kernel(y_ref, h_ref, g2_ref, gtok_ref, rows_ref, wgs_ref, wus_ref, wds_ref, o_ref):
    h_left, h_right = _unpack_rows(h_ref[...])
    h = jnp.concatenate([h_left, h_right], axis=1).astype(BF16)
    gs = _dot(h, wgs_ref[...])
    us = _dot(h, wus_ref[...])
    shared = _dot(((gs * _sigmoid(gs)) * us).astype(BF16), wds_ref[...])

    gtok = gtok_ref[...]
    acc_left = acc_right = None
    for k in range(TOP_K):
        left, right = _unpack_rows(rows_ref[k])
        gate = gtok[:, k:k + 1]
        acc_left = gate * left if acc_left is None else acc_left + gate * left
        acc_right = gate * right if acc_right is None else acc_right + gate * right
    routed = jnp.concatenate([acc_left, acc_right], axis=1)
    o_ref[...] = y_ref[...] + g2_ref[0] * (routed + shared)


def _combine(y_all, hp_all, gate2, gtok, rows, wgs, wus, wds, *, first_token, tokens, seq):
    d = y_all.shape[1]
    width = hp_all.shape[1]
    tm = MOE_ROW_TM
    tile0 = first_token // tm
    nb = gate2.shape[0]
    tiles_per_batch = seq // tm

    def mod_map(i):
        return ((i // tiles_per_batch) if nb > 1 else 0, 0, 0)

    return pl.pallas_call(
        _combine_kernel,
        out_shape=jax.ShapeDtypeStruct((tokens, d), F32),
        grid=(tokens // tm,),
        in_specs=[pl.BlockSpec((tm, d), lambda i: (tile0 + i, 0)),
                  pl.BlockSpec((tm, width), lambda i: (tile0 + i, 0)),
                  pl.BlockSpec((1, 1, d), mod_map),
                  pl.BlockSpec((tm, LANES), lambda i: (tile0 + i, 0)),
                  pl.BlockSpec((TOP_K, tm, width), lambda i: (0, tile0 + i, 0)),
                  pl.BlockSpec((d, D_SHARED), lambda i: (0, 0)),
                  pl.BlockSpec((d, D_SHARED), lambda i: (0, 0)),
                  pl.BlockSpec((D_SHARED, d), lambda i: (0, 0))],
        out_specs=pl.BlockSpec((tm, d), lambda i: (i, 0)),
        compiler_params=_cparams(("arbitrary",)),
        name="moe_combine",
    )(y_all, hp_all, gate2, gtok, rows, wgs, wus, wds)


def _expert_layout(counts, n_tiles):
    cnt = counts.astype(jnp.int32)
    tiles = (cnt + (MOE_TS - 1)) // MOE_TS
    last_tile = jnp.cumsum(tiles)
    off = (last_tile - tiles) * MOE_TS
    pad_lo = off + cnt
    pad_hi = (off + tiles * MOE_TS).at[N_EXPERTS - 1].set(n_tiles * MOE_TS)
    pad_cnt = pad_hi - pad_lo
    pad_last = jnp.cumsum(pad_cnt)
    shift = pad_lo - (pad_last - pad_cnt)
    j = jnp.arange(N_EXPERTS * MOE_TS, dtype=jnp.int32)
    past = (pad_last[None, :-1] <= j[:, None]).astype(jnp.int32)
    pad_slots = j + shift[0] + jnp.sum(past * (shift[1:] - shift[:-1])[None, :], axis=1)
    tile_ids = jnp.arange(n_tiles, dtype=jnp.int32)
    tile_expert = jnp.minimum(
        jnp.sum((last_tile[None, :] <= tile_ids[:, None]).astype(jnp.int32), axis=1), N_EXPERTS - 1)
    return off, pad_slots, tile_expert


def _rope_tables(n_tokens):
    t = jnp.arange(n_tokens)
    row = (t // GRID_W).astype(F32)
    col = (t % GRID_W).astype(F32)
    nf = HEAD_DIM // 4
    freqs = ROPE_THETA ** (-jnp.arange(nf, dtype=F32) / nf)
    ang_r = row[:, None] * freqs
    ang_c = col[:, None] * freqs
    cos = jnp.concatenate([jnp.cos(ang_r)] * 2 + [jnp.cos(ang_c)] * 2, axis=1)
    sin = jnp.concatenate([-jnp.sin(ang_r), jnp.sin(ang_r), -jnp.sin(ang_c), jnp.sin(ang_c)], axis=1)
    reps = LANES // HEAD_DIM
    return jnp.tile(cos, (1, reps)), jnp.tile(sin, (1, reps))


def _head_gains(qn_a, kn_a, qn_b, kn_b):
    ones = jnp.ones((HEAD_DIM,), F32)
    parts = ([qn_a] * N_HEADS_A + [kn_a] * N_KV_A + [ones] * N_KV_A
             + [qn_b] * N_HEADS_B + [kn_b] * N_HEADS_B + [ones] * N_HEADS_B)
    return jnp.concatenate(parts).reshape(1, IN_COLS).astype(F32)


def _same_head_indicator():
    i = np.arange(LANES)
    return jnp.asarray((i[:, None] // HEAD_DIM) == (i[None, :] // HEAD_DIM), BF16)


def _token_major(cache):
    b, h, s, hd = cache.shape
    return cache.transpose(0, 2, 1, 3).reshape(b, s, h * hd).astype(BF16)


def kernel(x_prompt, x_sample, cache_k_a, cache_v_a, cache_k_b, cache_v_b, c, c_ctx, w_mod, b_mod, norm1, norm2, w_in, qn_a, kn_a, qn_b, kn_b, rpb, on_a, on_b, w_out, w_router, router_bias, w_gate_e, w_up_e, w_down_e, w_gate_s, w_up_s, w_down_s):
    depth = w_mod.shape[0]
    assert depth == 1
    l = 0
    bp, sp, d = x_prompt.shape
    bs, ss, _ = x_sample.shape

    cvec = jnp.concatenate([c_ctx[None, :], c], axis=0)
    rows = -(-cvec.shape[0] // 8) * 8
    cvec = jnp.pad(cvec, ((0, rows - cvec.shape[0]), (0, 0)))
    mod = _adaln(cvec, w_mod[l], b_mod[l])
    mod_p = [m.reshape(1, 1, d) for m in jnp.split(mod[0:1], 6, axis=-1)]
    mod_s = [m.reshape(bs, 1, d) for m in jnp.split(mod[1:1 + bs], 6, axis=-1)]
    mod_all = [m.reshape(1 + bs, 1, d) for m in jnp.split(mod[0:1 + bs], 6, axis=-1)]

    w_in_bf = w_in[l].astype(BF16)
    w_out_bf = w_out[l].astype(BF16)
    gain = _head_gains(qn_a[l], kn_a[l], qn_b[l], kn_b[l])
    seg = _same_head_indicator()
    n1 = norm1[l].reshape(1, d)
    n2 = norm2[l].reshape(1, d)
    ona = on_a[l].reshape(1, WIDTH_A)
    onb = on_b[l].reshape(1, WIDTH_B)
    wr_t = w_router[l].T
    wr_hi = wr_t.astype(BF16)
    wr_lo = (wr_t - wr_hi.astype(F32)).astype(BF16)
    rbias = router_bias[l].reshape(N_EXPERTS, 1).astype(F32)
    wgs = w_gate_s[l].astype(BF16)
    wus = w_up_s[l].astype(BF16)
    wds = w_down_s[l].astype(BF16)
    t_p = bp * sp
    t_s = bs * ss
    t_all = t_p + t_s

    xp = x_prompt.reshape(t_p, d)
    proj_p, st_ka, st_va, st_kb, st_vb = _project(
        xp, mod_p[0], mod_p[1], n1, w_in_bf, gain, seg, None, tm=sp, seq=sp, states=True)
    oa_p, ob_p = _context_attention(proj_p, seq=sp)

    xs = x_sample.reshape(t_s, d)
    proj_s, = _project(xs, mod_s[0], mod_s[1], n1, w_in_bf, gain, seg, _rope_tables(ss),
                       tm=1024, seq=ss, states=False)
    oa_s = _latent_gqa(proj_s, _token_major(cache_k_a[:, l]), _token_major(cache_v_a[:, l]),
                       _query_norm_bound(qn_a[l]), seq=ss, tq=256)
    bias_t = _neighbourhood_bias(rpb[l], ss // GRID_W)
    ob_s = _latent_neighbourhood(proj_s, _token_major(cache_k_b[:, l]), _token_major(cache_v_b[:, l]),
                                 bias_t, _neighbourhood_bounds(qn_b[l], kn_b[l], rpb[l]), seq=ss)

    y1_all, hp_all, gates_t, rank_t, counts = _merge(
        (xp, oa_p, ob_p), (xs, oa_s, ob_s), ona, onb, w_out_bf, mod_all[2], mod_all[3], mod_all[4], n2,
        wr_hi, wr_lo, rbias, tm=512, lat_seq=ss)
    n_tiles = t_all * TOP_K // MOE_TS + N_EXPERTS
    off, pad_slots, tile_expert = _expert_layout(counts[:, 0], n_tiles)
    pos, gtok = _slots(gates_t, rank_t, off.astype(F32).reshape(N_EXPERTS, 1))
    slot_of = pos.reshape(TOP_K * t_all)
    x_slots = _dispatch(hp_all, jnp.concatenate([slot_of, pad_slots]))
    y_slots = _experts(x_slots, tile_expert, w_gate_e[l], w_up_e[l], w_down_e[l])
    rows = _gather_slots(y_slots, slot_of, t_all).reshape(TOP_K, t_all, d // 2)
    y_p = _combine(y1_all, hp_all, mod_p[5], gtok, rows, wgs, wus, wds,
                   first_token=0, tokens=t_p, seq=sp)
    y_s = _combine(y1_all, hp_all, mod_s[5], gtok, rows, wgs, wus, wds,
                   first_token=t_p, tokens=t_s, seq=ss)

    return (y_p.reshape(bp, sp, d), y_s.reshape(bs, ss, d), st_ka, st_va, st_kb, st_vb)
```

```python
import functools

import numpy as np
import jax
import jax.numpy as jnp
from jax import lax
from jax.experimental import pallas as pl
from jax.experimental.pallas import tpu as pltpu
from jax.experimental.pallas import tpu_sc as plsc

F32 = jnp.float32
BF16 = jnp.bfloat16

D_MODEL = 1024
HEAD_DIM = 64
N_HEADS_A = 8
N_KV_A = 2
GROUP_A = N_HEADS_A // N_KV_A
N_HEADS_B = 8
WIDTH_A = N_HEADS_A * HEAD_DIM
WIDTH_B = N_HEADS_B * HEAD_DIM
KV_WIDTH_A = N_KV_A * HEAD_DIM
IN_COLS = WIDTH_A + 2 * KV_WIDTH_A + 3 * WIDTH_B
GRID_W = 64
ROPE_THETA = 10000.0
NA_KH = 8
NA_KW = 16
N_EXPERTS = 64
N_GROUPS = 8
GROUP_SIZE = N_EXPERTS // N_GROUPS
TOPK_GROUPS = 4
TOP_K = 8
D_EXPERT = 256
D_SHARED = 256
ROUTED_SCALE = 2.5
EPS = 1e-6

LANES = 128
MXU_DIM = 256
MASKED = -1e30

COL_QA = 0
COL_KA = WIDTH_A
COL_VA = COL_KA + KV_WIDTH_A
COL_QB = COL_VA + KV_WIDTH_A
COL_KB = COL_QB + WIDTH_B
COL_VB = COL_KB + WIDTH_B

NA_QROWS = 8
NA_KROWS = 2 * NA_KH
NA_TQ = NA_QROWS * GRID_W
NA_TK = NA_KROWS * GRID_W
NA_KBLK = 256

VMEM_LIMIT = 56 * 1024 * 1024


def _cparams(sem):
    return pltpu.CompilerParams(dimension_semantics=sem, vmem_limit_bytes=VMEM_LIMIT)


def _dot(a, b):
    return jnp.dot(a, b, preferred_element_type=F32)


def _dot_nt(a, b):
    return lax.dot_general(a, b, (((1,), (1,)), ((), ())), preferred_element_type=F32)


def _sigmoid(x):
    return 1.0 / (1.0 + jnp.exp(-x))


def _rms(x):
    return x * lax.rsqrt(jnp.mean(x * x, axis=-1, keepdims=True) + EPS)


def _pack_rows(x):
    n = x.shape[1] // 2
    hi = lax.bitcast_convert_type(x[:, :n].astype(BF16).astype(F32), jnp.int32)
    lo = lax.bitcast_convert_type(x[:, n:].astype(BF16).astype(F32), jnp.int32)
    return hi | lax.shift_right_logical(lo, 16)


def _unpack_rows(w):
    left = lax.bitcast_convert_type(w & jnp.int32(-65536), F32)
    right = lax.bitcast_convert_type(lax.shift_left(w, 16), F32)
    return left, right


def _mod_kernel(c_ref, w_ref, b_ref, o_ref):
    c = c_ref[...]
    s = c * _sigmoid(c)
    o_ref[...] = jnp.dot(s, w_ref[...], preferred_element_type=F32,
                         precision=lax.Precision.HIGHEST) + b_ref[...]


def _adaln(cvec, w_mod, b_mod):
    rows, d = cvec.shape
    n = w_mod.shape[1]
    tn = 512
    return pl.pallas_call(
        _mod_kernel,
        out_shape=jax.ShapeDtypeStruct((rows, n), F32),
        grid=(n // tn,),
        in_specs=[pl.BlockSpec((rows, d), lambda j: (0, 0)),
                  pl.BlockSpec((d, tn), lambda j: (0, j)),
                  pl.BlockSpec((1, tn), lambda j: (0, j))],
        out_specs=pl.BlockSpec((rows, tn), lambda j: (0, j)),
        compiler_params=_cparams(("arbitrary",)),
        name="adaln_mod",
    )(cvec, w_mod, b_mod.reshape(1, n))


def _proj_chunks():
    def split(c0, width, step, *flags):
        return [(c0 + i, min(step, width - i)) + flags for i in range(0, width, step)]
    return (split(COL_QA, WIDTH_A, MXU_DIM, True, True, True)
            + split(COL_KA, KV_WIDTH_A, MXU_DIM, True, True, False)
            + split(COL_VA, KV_WIDTH_A, MXU_DIM, False, False, False)
            + split(COL_QB, WIDTH_B, MXU_DIM, True, False, True)
            + split(COL_KB, WIDTH_B, MXU_DIM, True, False, False)
            + split(COL_VB, WIDTH_B, MXU_DIM, False, False, False))


_PROJ_CHUNKS = _proj_chunks()


def _proj_kernel(*refs, rope, states):
    x_ref, sh_ref, sc_ref, n1_ref, w_ref, gain_ref, seg_ref = refs[:7]
    pos = 7
    if rope:
        cos_ref, sin_ref = refs[pos:pos + 2]
        pos += 2
    out_ref = refs[pos]
    pos += 1
    if states:
        ka_ref, va_ref, kb_ref, vb_ref = refs[pos:pos + 4]
        state_of = {COL_KA: ka_ref, COL_VA: va_ref, COL_KB: kb_ref, COL_VB: vb_ref}

    x = x_ref[...]
    h = _rms(x) * n1_ref[...]
    h = h * (1.0 + sc_ref[0]) + sh_ref[0]
    p = _dot(h.astype(BF16), w_ref[...])

    for c0, w, normed, roped, is_query in _PROJ_CHUNKS:
        pc = p[:, c0:c0 + w]
        if normed:
            seg = seg_ref[0:w, 0:w]
            sq = pc * pc
            hi = sq.astype(BF16)
            lo = (sq - hi.astype(F32)).astype(BF16)
            ss = _dot(hi, seg) + _dot(lo, seg)
            pc = pc * lax.rsqrt(ss * (1.0 / HEAD_DIM) + EPS) * gain_ref[:, c0:c0 + w]
        if states:
            for start, ref in state_of.items():
                if start <= c0 < start + ref.shape[2] * HEAD_DIM:
                    base = (c0 - start) // HEAD_DIM
                    for hh in range(w // HEAD_DIM):
                        ref[0, 0, base + hh] = pc[:, hh * HEAD_DIM:(hh + 1) * HEAD_DIM]
        if rope and roped:
            reps = w // LANES
            cos = jnp.concatenate([cos_ref[...]] * reps, axis=1) if reps > 1 else cos_ref[...]
            sin = jnp.concatenate([sin_ref[...]] * reps, axis=1) if reps > 1 else sin_ref[...]
            lane = lax.broadcasted_iota(jnp.int32, pc.shape, 1)
            first_half = (lane % (HEAD_DIM // 2)) < (HEAD_DIM // 4)
            partner = jnp.where(first_half,
                                pltpu.roll(pc, w - HEAD_DIM // 4, 1),
                                pltpu.roll(pc, HEAD_DIM // 4, 1))
            pc = pc * cos + partner * sin
        if is_query:
            pc = pc * (HEAD_DIM ** -0.5)
        out_ref[:, c0:c0 + w] = pc.astype(BF16)


def _project(x2d, shift, scale, norm1, w_in_bf, gain, seg, rope_tabs, *, tm, seq, states):
    t, d = x2d.shape
    nb = shift.shape[0]
    tiles_per_batch = seq // tm
    rope = rope_tabs is not None

    def mod_map(i):
        return ((i // tiles_per_batch) if nb > 1 else 0, 0, 0)

    in_specs = [pl.BlockSpec((tm, d), lambda i: (i, 0)),
                pl.BlockSpec((1, 1, d), mod_map),
                pl.BlockSpec((1, 1, d), mod_map),
                pl.BlockSpec((1, d), lambda i: (0, 0)),
                pl.BlockSpec((d, IN_COLS), lambda i: (0, 0)),
                pl.BlockSpec((1, IN_COLS), lambda i: (0, 0)),
                pl.BlockSpec((MXU_DIM, MXU_DIM), lambda i: (0, 0))]
    args = [x2d, shift, scale, norm1, w_in_bf, gain, seg]
    if rope:
        in_specs += [pl.BlockSpec((tm, LANES), lambda i: (i % tiles_per_batch, 0))] * 2
        args += list(rope_tabs)
    out_shape = [jax.ShapeDtypeStruct((t, IN_COLS), BF16)]
    out_specs = [pl.BlockSpec((tm, IN_COLS), lambda i: (i, 0))]
    if states:
        assert tm == seq
        b = t // seq
        for nh in (N_KV_A, N_KV_A, N_HEADS_B, N_HEADS_B):
            out_shape.append(jax.ShapeDtypeStruct((b, 1, nh, seq, HEAD_DIM), F32))
            out_specs.append(pl.BlockSpec((1, 1, nh, seq, HEAD_DIM), lambda i: (i, 0, 0, 0, 0)))
    return pl.pallas_call(
        functools.partial(_proj_kernel, rope=rope, states=states),
        out_shape=out_shape,
        grid=(t // tm,),
        in_specs=in_specs,
        out_specs=out_specs,
        compiler_params=_cparams(("arbitrary",)),
        name="proj_states" if states else "proj_rope",
    )(*args)


def _lane_half(shape):
    return lax.broadcasted_iota(jnp.int32, shape, 1) // HEAD_DIM


def _keep_half(x, half):
    return jnp.where(_lane_half(x.shape) == half, x, jnp.zeros_like(x))


def _transpose_bf16(x):
    return x.astype(F32).T.astype(BF16)


def _attend(q, keys, values_t, biases):
    return _softmax_av(_scores(q, keys, biases), values_t)


def _scores(q, keys, biases):
    scores = []
    for k, b in zip(keys, biases):
        s = _dot_nt(k, q)
        if b is not None:
            s = s + b
        scores.append(s)
    return scores


def _softmax_av(scores, values_t):
    m = functools.reduce(jnp.maximum, [jnp.max(s, axis=0, keepdims=True) for s in scores])
    denom = None
    out = None
    for s, vt in zip(scores, values_t):
        p = jnp.exp(s - m)
        ps = jnp.sum(p, axis=0, keepdims=True)
        po = _dot(vt, p.astype(BF16))
        denom = ps if denom is None else denom + ps
        out = po if out is None else out + po
    return out / denom


def _swap_halves(q_bf16):
    return pltpu.roll(q_bf16.astype(F32), HEAD_DIM, 1).astype(BF16)


def _gqa_heads(q_of_pair, keys_by_group, values_t):
    outs = []
    for h in range(N_HEADS_A):
        g = h // GROUP_A
        q = q_of_pair(h // 2)
        if h % 2 != g:
            q = _swap_halves(q)
        o = _attend(q, keys_by_group[g], values_t, [None] * len(values_t))
        outs.append(o[g * HEAD_DIM:(g + 1) * HEAD_DIM])
    return jnp.concatenate(outs, axis=0)


def _ctx_attn_kernel(p_ref, oa_ref, ob_ref):
    ka = p_ref[:, COL_KA:COL_KA + LANES]
    va_t = [_transpose_bf16(p_ref[:, COL_VA:COL_VA + LANES])]
    keys_by_group = [[_keep_half(ka, g)] for g in range(N_KV_A)]
    oa = _gqa_heads(lambda i: p_ref[:, COL_QA + i * LANES:COL_QA + (i + 1) * LANES],
                    keys_by_group, va_t)
    oa_ref[...] = oa.T

    outs = []
    for i in range(N_HEADS_B // 2):
        q = p_ref[:, COL_QB + i * LANES:COL_QB + (i + 1) * LANES]
        k = p_ref[:, COL_KB + i * LANES:COL_KB + (i + 1) * LANES]
        vt = [_transpose_bf16(p_ref[:, COL_VB + i * LANES:COL_VB + (i + 1) * LANES])]
        for half in range(2):
            o = _attend(q, [_keep_half(k, half)], vt, [None])
            outs.append(o[half * HEAD_DIM:(half + 1) * HEAD_DIM])
    ob_ref[...] = jnp.concatenate(outs, axis=0).T


def _context_attention(proj, *, seq):
    t = proj.shape[0]
    return pl.pallas_call(
        _ctx_attn_kernel,
        out_shape=[jax.ShapeDtypeStruct((t, WIDTH_A), F32), jax.ShapeDtypeStruct((t, WIDTH_B), F32)],
        grid=(t // seq,),
        in_specs=[pl.BlockSpec((seq, IN_COLS), lambda i: (i, 0))],
        out_specs=[pl.BlockSpec((seq, WIDTH_A), lambda i: (i, 0)),
                   pl.BlockSpec((seq, WIDTH_B), lambda i: (i, 0))],
        compiler_params=_cparams(("arbitrary",)),
        name="context_attention",
    )(proj)


ATTN_SAFE_SHIFT = 40.0
ONES_ROWS = 16


def _round_up_bf16(x):
    return (x * (1.0 + 2.0 ** -6)).astype(BF16).astype(F32)


def _query_norm_bound(gain):
    return jnp.max(jnp.abs(gain)).reshape(1, 1).astype(F32)


def _ones_lane(g):
    return (1 - g) * HEAD_DIM


def _gqa_latent_kernel(q_ref, k_ref, v_ref, ck_ref, cv_ref, qmax_ref, o_ref,
                       kg_ref, ckg_ref, vt_ref, cvt_ref, shift_ref):
    lane_k = lax.broadcasted_iota(jnp.int32, (1, LANES), 1)

    @pl.when(pl.program_id(1) == 0)
    def _():
        k = k_ref[...]
        ck = ck_ref[0]
        vt = v_ref[...].astype(F32).T
        cvt = cv_ref[0].astype(F32).T
        for g in range(N_KV_A):
            kf = _keep_half(k, g).astype(F32)
            ckf = _keep_half(ck, g).astype(F32)
            ksq = jnp.maximum(jnp.max(jnp.sum(kf * kf, axis=1, keepdims=True), axis=0, keepdims=True),
                              jnp.max(jnp.sum(ckf * ckf, axis=1, keepdims=True), axis=0, keepdims=True))
            shift_ref[g] = jnp.broadcast_to(_round_up_bf16(qmax_ref[...] * jnp.sqrt(ksq)), shift_ref.shape[1:])
            kg_ref[g] = jnp.where(lane_k == _ones_lane(g), 1.0, kf).astype(BF16)
            ckg_ref[g] = jnp.where(lane_k == _ones_lane(g), 1.0, ckf).astype(BF16)
            rows = slice(g * HEAD_DIM, (g + 1) * HEAD_DIM)
            vt_ref[g] = jnp.concatenate([vt[rows], jnp.ones((ONES_ROWS, vt.shape[1]), F32)], axis=0).astype(BF16)
            cvt_ref[g] = jnp.concatenate([cvt[rows], jnp.ones((ONES_ROWS, cvt.shape[1]), F32)], axis=0).astype(BF16)

    tq = q_ref.shape[0]
    lane_q = lax.broadcasted_iota(jnp.int32, (GROUP_A * tq, LANES), 1)
    queries, shifts = [], []
    for g in range(N_KV_A):
        qs = []
        for j in range(GROUP_A):
            h = g * GROUP_A + j
            q = q_ref[:, (h // 2) * LANES:(h // 2 + 1) * LANES].astype(F32)
            qs.append(q if h % 2 == g else pltpu.roll(q, HEAD_DIM, 1))
        queries.append(jnp.where(lane_q // HEAD_DIM == g, jnp.concatenate(qs, axis=0), 0.0))
        shifts.append(shift_ref[g][0:1, 0:1])
    safe = jnp.max(jnp.maximum(shift_ref[0], shift_ref[1])) <= ATTN_SAFE_SHIFT

    def attend(g, p_lat, p_ctx):
        o = _dot(vt_ref[g], p_lat) + _dot(cvt_ref[g], p_ctx)
        o = o[:HEAD_DIM] / o[HEAD_DIM:HEAD_DIM + 1]
        heads = jnp.concatenate([o[:, j * tq:(j + 1) * tq] for j in range(GROUP_A)], axis=0)
        o_ref[:, g * GROUP_A * HEAD_DIM:(g + 1) * GROUP_A * HEAD_DIM] = heads.T

    def with_bound():
        for g in range(N_KV_A):
            qa = jnp.where(lane_q == _ones_lane(g), -shifts[g], queries[g]).astype(BF16)
            attend(g, jnp.exp(_dot_nt(kg_ref[g], qa)).astype(BF16), jnp.exp(_dot_nt(ckg_ref[g], qa)).astype(BF16))

    def with_row_max():
        for g in range(N_KV_A):
            qa = queries[g].astype(BF16)
            s_lat = _dot_nt(kg_ref[g], qa)
            s_ctx = _dot_nt(ckg_ref[g], qa)
            m = jnp.maximum(jnp.max(s_lat, axis=0, keepdims=True), jnp.max(s_ctx, axis=0, keepdims=True))
            attend(g, jnp.exp(s_lat - m).astype(BF16), jnp.exp(s_ctx - m).astype(BF16))

    pl.when(safe)(with_bound)
    pl.when(jnp.logical_not(safe))(with_row_max)


def _latent_gqa(proj, ctx_k, ctx_v, qmax, *, seq, tq):
    t = proj.shape[0]
    b = t // seq
    nq = seq // tq
    past = ctx_k.shape[1]
    return pl.pallas_call(
        _gqa_latent_kernel,
        out_shape=jax.ShapeDtypeStruct((t, WIDTH_A), F32),
        grid=(b, nq),
        in_specs=[pl.BlockSpec((tq, WIDTH_A), lambda bi, qi: (bi * nq + qi, 0)),
                  pl.BlockSpec((seq, LANES), lambda bi, qi: (bi, COL_KA // LANES)),
                  pl.BlockSpec((seq, LANES), lambda bi, qi: (bi, COL_VA // LANES)),
                  pl.BlockSpec((1, past, LANES), lambda bi, qi: (bi, 0, 0)),
                  pl.BlockSpec((1, past, LANES), lambda bi, qi: (bi, 0, 0)),
                  pl.BlockSpec((1, 1), lambda bi, qi: (0, 0))],
        out_specs=pl.BlockSpec((tq, WIDTH_A), lambda bi, qi: (bi * nq + qi, 0)),
        scratch_shapes=[pltpu.VMEM((N_KV_A, seq, LANES), BF16),
                        pltpu.VMEM((N_KV_A, past, LANES), BF16),
                        pltpu.VMEM((N_KV_A, HEAD_DIM + ONES_ROWS, seq), BF16),
                        pltpu.VMEM((N_KV_A, HEAD_DIM + ONES_ROWS, past), BF16),
                        pltpu.VMEM((N_KV_A, 8, LANES), F32)],
        compiler_params=_cparams(("arbitrary", "arbitrary")),
        name="latent_gqa",
    )(proj, proj, proj, ctx_k, ctx_v, qmax)


def _na_kernel(q_ref, k_ref, v_ref, ck_ref, cv_ref, bias_ref, bound_ref, o_ref, keys_ref, vt_ref, shift_ref,
               *, rows):
    i = pl.program_id(2)
    n_kblk = k_ref.shape[0] // NA_KBLK
    lane_k = lax.broadcasted_iota(jnp.int32, (1, LANES), 1)
    one = jnp.ones((), BF16)

    @pl.when(i == 0)
    def _():
        k = k_ref[...]
        ck = ck_ref[0]
        for half in range(2):
            kh = jnp.where(lane_k == _ones_lane(half), one, _keep_half(k, half))
            keys_ref[half, 0:n_kblk] = kh.reshape(n_kblk, NA_KBLK, LANES)
            ckh = _keep_half(ck, half)
            keys_ref[half, n_kblk] = jnp.where(lane_k == _ones_lane(half), one, ckh)
            ckf = ckh.astype(F32)
            ctx_norm = jnp.sqrt(jnp.max(jnp.sum(ckf * ckf, axis=1, keepdims=True), axis=0, keepdims=True))
            consts = bound_ref[0, half:half + 1, :]
            kmax = jnp.maximum(ctx_norm, consts[:, 2:3])
            shift_ref[half] = jnp.broadcast_to(_round_up_bf16(consts[:, 0:1] * kmax + consts[:, 1:2]),
                                               shift_ref.shape[1:])
        ones_rows = jnp.ones((ONES_ROWS, NA_KBLK), F32)
        vt = v_ref[...].astype(F32).T
        for j in range(n_kblk):
            vt_ref[j] = jnp.concatenate([vt[:, j * NA_KBLK:(j + 1) * NA_KBLK], ones_rows], axis=0).astype(BF16)
        vt_ref[n_kblk] = jnp.concatenate([cv_ref[0].astype(F32).T, ones_rows], axis=0).astype(BF16)

    q = q_ref[...]
    lane_q = lax.broadcasted_iota(jnp.int32, q.shape, 1)
    first = _na_first_key_block(i, rows)
    n_qblk = rows // NA_QROWS
    variant = jnp.where(i == 0, 0, jnp.where(i == n_qblk - 1, 2, 1))
    blocks = [first + j for j in range(NA_TK // NA_KBLK)] + [n_kblk]
    values_t = [vt_ref[blk] for blk in blocks]
    heads = []
    for half in range(2):
        keys = [keys_ref[half, blk] for blk in blocks]
        biases = [bias_ref[variant, half, j * NA_KBLK:(j + 1) * NA_KBLK, :] for j in range(NA_TK // NA_KBLK)] + [None]
        heads.append((shift_ref[half][0:1, 0:1], keys, biases))
    safe = jnp.max(jnp.maximum(shift_ref[0], shift_ref[1])) <= ATTN_SAFE_SHIFT

    def attend(probabilities):
        outs = []
        for half, ps in enumerate(probabilities):
            o = functools.reduce(lambda a, b: a + b, [_dot(vt, p) for vt, p in zip(values_t, ps)])
            outs.append(o[half * HEAD_DIM:(half + 1) * HEAD_DIM] / o[2 * HEAD_DIM:2 * HEAD_DIM + 1])
        o_ref[...] = jnp.concatenate(outs, axis=0).T

    def with_bound():
        probabilities = []
        for half, (shift, keys, biases) in enumerate(heads):
            qa = jnp.where(lane_q == _ones_lane(half), (-shift).astype(BF16), _keep_half(q, half))
            probabilities.append([jnp.exp(s).astype(BF16) for s in _scores(qa, keys, biases)])
        attend(probabilities)

    def with_row_max():
        all_scores = [_scores(_keep_half(q, half), keys, biases) for half, (_, keys, biases) in enumerate(heads)]
        probabilities = []
        for scores in all_scores:
            m = functools.reduce(jnp.maximum, [jnp.max(s, axis=0, keepdims=True) for s in scores])
            probabilities.append([jnp.exp(s - m).astype(BF16) for s in scores])
        attend(probabilities)

    pl.when(safe)(with_bound)
    pl.when(jnp.logical_not(safe))(with_row_max)


def _na_first_key_block(i, rows):
    per_qblock = NA_QROWS * GRID_W // NA_KBLK
    lead = (NA_KH // 2) * GRID_W // NA_KBLK
    return jnp.clip(per_qblock * i - lead, 0, (rows - NA_KROWS) * GRID_W // NA_KBLK)


def _latent_neighbourhood(proj, ctx_k, ctx_v, bias_t, bounds, *, seq):
    t = proj.shape[0]
    b = t // seq
    rows = seq // GRID_W
    nblk = rows // NA_QROWS
    n_kblk = seq // NA_KBLK
    past = ctx_k.shape[1]
    assert past == NA_KBLK
    grid = (N_HEADS_B // 2, b, nblk)
    in_specs = [pl.BlockSpec((NA_TQ, LANES), lambda hp, bi, i: (bi * nblk + i, COL_QB // LANES + hp)),
                pl.BlockSpec((seq, LANES), lambda hp, bi, i: (bi, COL_KB // LANES + hp)),
                pl.BlockSpec((seq, LANES), lambda hp, bi, i: (bi, COL_VB // LANES + hp)),
                pl.BlockSpec((1, past, LANES), lambda hp, bi, i: (bi, 0, hp)),
                pl.BlockSpec((1, past, LANES), lambda hp, bi, i: (bi, 0, hp)),
                pl.BlockSpec((3, 2, NA_TK, NA_TQ), lambda hp, bi, i: (0, hp, 0, 0)),
                pl.BlockSpec((1, 2, LANES), lambda hp, bi, i: (hp, 0, 0))]
    return pl.pallas_call(
        functools.partial(_na_kernel, rows=rows),
        out_shape=jax.ShapeDtypeStruct((t, WIDTH_B), F32),
        grid=grid,
        in_specs=in_specs,
        out_specs=pl.BlockSpec((NA_TQ, LANES), lambda hp, bi, i: (bi * nblk + i, hp)),
        scratch_shapes=[pltpu.VMEM((2, n_kblk + 1, NA_KBLK, LANES), BF16),
                        pltpu.VMEM((n_kblk + 1, 2 * HEAD_DIM + ONES_ROWS, NA_KBLK), BF16),
                        pltpu.VMEM((2, 8, LANES), F32)],
        compiler_params=_cparams(("arbitrary", "arbitrary", "arbitrary")),
        name="latent_neighbourhood",
    )(proj, proj, proj, ctx_k, ctx_v, bias_t, bounds)


def _neighbourhood_bounds(qn_b, kn_b, rpb):
    n_heads = rpb.shape[0]
    qmax = jnp.broadcast_to(_query_norm_bound(qn_b), (n_heads, 1))
    kmax = jnp.broadcast_to(_query_norm_bound(kn_b) * (HEAD_DIM ** 0.5), (n_heads, 1))
    bmax = jnp.maximum(jnp.max(rpb.reshape(n_heads, -1), axis=1, keepdims=True), 0.0).astype(F32)
    table = jnp.concatenate([qmax, bmax, kmax, jnp.zeros((n_heads, LANES - 3), F32)], axis=1)
    return table.reshape(n_heads // 2, 2, LANES)


def _neighbourhood_bias(rpb, rows):
    nblk = rows // NA_QROWS
    n_dr = 2 * NA_KH - 1
    n_dc = 2 * NA_KW - 1
    kc = np.arange(GRID_W)[:, None]
    qc = np.arange(GRID_W)[None, :]
    ws = np.clip(qc - NA_KW // 2, 0, GRID_W - NA_KW)
    col_ok = (kc >= ws) & (kc < ws + NA_KW)
    dc = np.clip(kc - qc + NA_KW - 1, 0, n_dc - 1)
    dc_onehot = (dc[None] == np.arange(n_dc)[:, None, None]).astype(np.float32)
    tiles = jnp.einsum('hab,bkq->hakq', rpb.astype(F32), jnp.asarray(dc_onehot),
                       precision=lax.Precision.HIGHEST)
    tiles = jnp.where(jnp.asarray(col_ok)[None, None], tiles, MASKED)
    masked_tile = jnp.full((rpb.shape[0], 1, GRID_W, GRID_W), MASKED, F32)
    tiles = jnp.concatenate([tiles, masked_tile], axis=1)
    pick = np.zeros((3, NA_KROWS, NA_QROWS, n_dr + 1), np.float32)
    for v, i in enumerate((0, 1, nblk - 1)):
        r0 = i * NA_QROWS
        ks = int(np.clip(r0 - NA_KH // 2, 0, rows - NA_KROWS))
        for kl in range(NA_KROWS):
            for ql in range(NA_QROWS):
                kr, qr = ks + kl, r0 + ql
                rs = int(np.clip(qr - NA_KH // 2, 0, rows - NA_KH))
                ok = rs <= kr < rs + NA_KH
                pick[v, kl, ql, (kr - qr + NA_KH - 1) if ok else n_dr] = 1.0
    bias = jnp.einsum('vkqa,hacd->vhkcqd', jnp.asarray(pick), tiles, precision=lax.Precision.HIGHEST)
    return bias.reshape(3, rpb.shape[0], NA_TK, NA_TQ)


def _merge_kernel(xp_ref, oap_ref, obp_ref, xs_ref, oas_ref, obs_ref, ona_ref, onb_ref, wo_ref,
                  g1_ref, sh2_ref, sc2_ref, n2_ref, wrh_ref, wrl_ref, rb_ref, tri_ref,
                  y_ref, hp_ref, gates_ref, rank_ref, count_ref, *, ctx_tiles):
    i = pl.program_id(0)

    @pl.when(i == 0)
    def _():
        count_ref[...] = jnp.zeros_like(count_ref)

    def one_stream(x_ref, oa_ref, ob_ref):
        na = (_rms(oa_ref[...]) * ona_ref[...]).astype(BF16)
        nb = (_rms(ob_ref[...]) * onb_ref[...]).astype(BF16)
        mix = _dot(na, wo_ref[0:WIDTH_A, :]) + _dot(nb, wo_ref[WIDTH_A:WIDTH_A + WIDTH_B, :])
        y = x_ref[...] + g1_ref[0] * mix
        y_ref[...] = y
        h = _rms(y) * n2_ref[...]
        h = h * (1.0 + sc2_ref[0]) + sh2_ref[0]
        hp_ref[...] = _pack_rows(h)
        gates, chosen = _router_gates(h, wrh_ref[...], wrl_ref[...], rb_ref[...])
        gates_ref[...] = gates
        before = _dot(chosen.astype(BF16), tri_ref[...])
        seen = count_ref[...]
        rank_ref[...] = jnp.where(chosen > 0.0, before + seen[:, 0:1], -1.0)
        count_ref[...] = seen + jnp.sum(chosen, axis=1, keepdims=True)

    pl.when(i < ctx_tiles)(lambda: one_stream(xp_ref, oap_ref, obp_ref))
    pl.when(i >= ctx_tiles)(lambda: one_stream(xs_ref, oas_ref, obs_ref))


def _merge(ctx, lat, on_a, on_b, w_out_bf, gate1, shift2, scale2, norm2, wr_hi, wr_lo, rbias, *, tm, lat_seq):
    t_c, d = ctx[0].shape
    t_l = lat[0].shape[0]
    t = t_c + t_l
    tri = jnp.asarray(np.triu(np.ones((tm, tm), np.float32), k=1), BF16)
    ctx_tiles = t_c // tm
    lat_tiles_per_batch = lat_seq // tm

    def ctx_map(i):
        return (jnp.minimum(i, ctx_tiles - 1), 0)

    def lat_map(i):
        return (jnp.maximum(i - ctx_tiles, 0), 0)

    def mod_map(i):
        return (jnp.where(i < ctx_tiles, 0, 1 + (i - ctx_tiles) // lat_tiles_per_batch), 0, 0)

    def stream_specs(index_map):
        return [pl.BlockSpec((tm, d), index_map),
                pl.BlockSpec((tm, WIDTH_A), index_map),
                pl.BlockSpec((tm, WIDTH_B), index_map)]

    return pl.pallas_call(
        functools.partial(_merge_kernel, ctx_tiles=ctx_tiles),
        out_shape=[jax.ShapeDtypeStruct((t, d), F32),
                   jax.ShapeDtypeStruct((t, d // 2), jnp.int32),
                   jax.ShapeDtypeStruct((N_EXPERTS, t), F32),
                   jax.ShapeDtypeStruct((N_EXPERTS, t), F32),
                   jax.ShapeDtypeStruct((N_EXPERTS, LANES), F32)],
        grid=(t // tm,),
        in_specs=stream_specs(ctx_map) + stream_specs(lat_map) + [
            pl.BlockSpec((1, WIDTH_A), lambda i: (0, 0)),
            pl.BlockSpec((1, WIDTH_B), lambda i: (0, 0)),
            pl.BlockSpec((WIDTH_A + WIDTH_B, d), lambda i: (0, 0)),
            pl.BlockSpec((1, 1, d), mod_map),
            pl.BlockSpec((1, 1, d), mod_map),
            pl.BlockSpec((1, 1, d), mod_map),
            pl.BlockSpec((1, d), lambda i: (0, 0)),
            pl.BlockSpec((N_EXPERTS, d), lambda i: (0, 0)),
            pl.BlockSpec((N_EXPERTS, d), lambda i: (0, 0)),
            pl.BlockSpec((N_EXPERTS, 1), lambda i: (0, 0)),
            pl.BlockSpec((tm, tm), lambda i: (0, 0))],
        out_specs=[pl.BlockSpec((tm, d), lambda i: (i, 0)),
                   pl.BlockSpec((tm, d // 2), lambda i: (i, 0)),
                   pl.BlockSpec((N_EXPERTS, tm), lambda i: (0, i)),
                   pl.BlockSpec((N_EXPERTS, tm), lambda i: (0, i)),
                   pl.BlockSpec((N_EXPERTS, LANES), lambda i: (0, 0))],
        compiler_params=_cparams(("arbitrary",)),
        name="merge_route",
    )(*ctx, *lat, on_a, on_b, w_out_bf, gate1, shift2, scale2, norm2, wr_hi, wr_lo, rbias, tri)


def _first_index_of_max(x, iota):
    mx = jnp.max(x, axis=0, keepdims=True)
    idx = jnp.min(jnp.where(x == mx, iota, float(x.shape[0])), axis=0, keepdims=True)
    return mx, iota == idx


def _router_gates(h, wr_hi, wr_lo, rbias):
    h_hi = h.astype(BF16)
    h_lo = (h - h_hi.astype(F32)).astype(BF16)
    logits = _dot_nt(wr_hi, h_hi) + (_dot_nt(wr_lo, h_hi) + _dot_nt(wr_hi, h_lo))
    scores = _sigmoid(logits)
    sel = scores + rbias
    tm = sel.shape[1]
    iota_g = lax.broadcasted_iota(jnp.int32, (GROUP_SIZE, tm), 0).astype(F32)
    group_scores = []
    for g in range(N_GROUPS):
        grp = sel[g * GROUP_SIZE:(g + 1) * GROUP_SIZE]
        m1, first = _first_index_of_max(grp, iota_g)
        m2 = jnp.max(jnp.where(first, -jnp.inf, grp), axis=0, keepdims=True)
        group_scores.append(m1 + m2)
    gs = jnp.concatenate(group_scores, axis=0)
    iota_n = lax.broadcasted_iota(jnp.int32, (N_GROUPS, tm), 0).astype(F32)
    group_on = jnp.zeros((N_GROUPS, tm), F32)
    for _ in range(TOPK_GROUPS):
        _, pick = _first_index_of_max(gs, iota_n)
        group_on = jnp.where(pick, 1.0, group_on)
        gs = jnp.where(pick, -jnp.inf, gs)
    expert_on = jnp.concatenate(
        [jnp.broadcast_to(group_on[g:g + 1], (GROUP_SIZE, tm)) for g in range(N_GROUPS)], axis=0)
    cand = jnp.where(expert_on > 0.0, sel, -jnp.inf)
    iota_e = lax.broadcasted_iota(jnp.int32, (N_EXPERTS, tm), 0).astype(F32)
    w = jnp.zeros((N_EXPERTS, tm), F32)
    chosen = jnp.zeros((N_EXPERTS, tm), F32)
    for _ in range(TOP_K):
        _, pick = _first_index_of_max(cand, iota_e)
        w = jnp.where(pick, scores, w)
        chosen = jnp.where(pick, 1.0, chosen)
        cand = jnp.where(pick, -jnp.inf, cand)
    return w / jnp.sum(w, axis=0, keepdims=True) * ROUTED_SCALE, chosen


MOE_TS = 1024
MOE_ROUTE_TM = 1024
MOE_ROW_TM = 512


def _slots_kernel(gates_ref, rank_ref, off_ref, pos_ref, gtok_ref):
    gates = gates_ref[...]
    rank = rank_ref[...]
    tm = gates.shape[1]
    slot = off_ref[...] + rank
    left = jnp.where(rank >= 0.0, 1.0, 0.0)
    iota_e = lax.broadcasted_iota(jnp.int32, (N_EXPERTS, tm), 0).astype(F32)
    pos_rows, gate_rows = [], []
    for _ in range(TOP_K):
        _, pick = _first_index_of_max(left, iota_e)
        pos_rows.append(jnp.sum(jnp.where(pick, slot, 0.0), axis=0, keepdims=True))
        gate_rows.append(jnp.sum(jnp.where(pick, gates, 0.0), axis=0, keepdims=True))
        left = jnp.where(pick, 0.0, left)
    pos_ref[...] = jnp.concatenate(pos_rows, axis=0).astype(jnp.int32)
    pad = jnp.zeros((LANES - TOP_K, tm), F32)
    gtok_ref[...] = jnp.concatenate(gate_rows + [pad], axis=0).T


def _slots(gates_t, rank_t, off):
    t = gates_t.shape[1]
    tm = MOE_ROUTE_TM
    return pl.pallas_call(
        _slots_kernel,
        out_shape=[jax.ShapeDtypeStruct((TOP_K, t), jnp.int32), jax.ShapeDtypeStruct((t, LANES), F32)],
        grid=(t // tm,),
        in_specs=[pl.BlockSpec((N_EXPERTS, tm), lambda i: (0, i)),
                  pl.BlockSpec((N_EXPERTS, tm), lambda i: (0, i)),
                  pl.BlockSpec((N_EXPERTS, 1), lambda i: (0, 0))],
        out_specs=[pl.BlockSpec((TOP_K, tm), lambda i: (0, i)),
                   pl.BlockSpec((tm, LANES), lambda i: (i, 0))],
        compiler_params=_cparams(("arbitrary",)),
        name="moe_slots",
    )(gates_t, rank_t, off)


SC_CORES = 2
SC_SUBCORES = 16
SC_ROWS = 64


def _dispatch(hp_all, slot_of):
    t, width = hp_all.shape
    n_slots = slot_of.shape[0]
    n_pad = n_slots - TOP_K * t
    workers = SC_CORES * SC_SUBCORES
    per_worker = t // workers
    pad_per_worker = n_pad // workers
    assert per_worker * workers == t and per_worker % SC_ROWS == 0
    assert pad_per_worker * workers == n_pad and pad_per_worker % SC_ROWS == 0
    mesh = plsc.VectorSubcoreMesh(core_axis_name="core", subcore_axis_name="subcore")

    @functools.partial(
        pl.kernel, mesh=mesh,
        out_type=jax.ShapeDtypeStruct((n_slots, width), jnp.int32),
        scratch_types=[pltpu.VMEM((SC_ROWS,), jnp.int32),
                       pltpu.VMEM((SC_ROWS, width), jnp.int32),
                       pltpu.SemaphoreType.DMA],
    )
    def scatter_rows(h_hbm, slot_hbm, out_hbm, idx_v, rows_v, sem):
        worker = lax.axis_index("subcore") * SC_CORES + lax.axis_index("core")
        base = worker * per_worker

        @pl.loop(0, per_worker // SC_ROWS)
        def _(j):
            first = base + j * SC_ROWS
            pltpu.sync_copy(h_hbm.at[pl.ds(first, SC_ROWS)], rows_v)
            for k in range(TOP_K):
                pltpu.sync_copy(slot_hbm.at[pl.ds(k * t + first, SC_ROWS)], idx_v)
                pltpu.async_copy(rows_v, out_hbm.at[idx_v], sem).wait()

        pltpu.sync_copy(h_hbm.at[pl.ds(0, SC_ROWS)], rows_v)
        pad_base = TOP_K * t + worker * pad_per_worker

        @pl.loop(0, pad_per_worker // SC_ROWS)
        def _(j):
            pltpu.sync_copy(slot_hbm.at[pl.ds(pad_base + j * SC_ROWS, SC_ROWS)], idx_v)
            pltpu.async_copy(rows_v, out_hbm.at[idx_v], sem).wait()

    return scatter_rows(hp_all, slot_of)


def _experts_kernel(te_ref, xs_ref, wg_ref, wu_ref, wd_ref, ys_ref, wgu_bf, wd_bf):
    i = pl.program_id(0)

    @pl.when((i == 0) | (te_ref[i] != te_ref[jnp.maximum(i, 1) - 1]))
    def _():
        wgu_bf[:, 0:D_EXPERT] = wg_ref[0].astype(BF16)
        wgu_bf[:, D_EXPERT:2 * D_EXPERT] = wu_ref[0].astype(BF16)
        wd_bf[...] = wd_ref[0].astype(BF16)

    left, right = _unpack_rows(xs_ref[...])
    x = jnp.concatenate([left, right], axis=1).astype(BF16)
    gu = _dot(x, wgu_bf[...])
    g = gu[:, 0:D_EXPERT]
    u = gu[:, D_EXPERT:2 * D_EXPERT]
    act = (g * _sigmoid(g)) * u
    ys_ref[...] = _pack_rows(_dot(act.astype(BF16), wd_bf[...]))


def _experts(xs, tile_expert, w_gate, w_up, w_down):
    n_slots, width = xs.shape
    d = 2 * width
    ts = MOE_TS
    return pl.pallas_call(
        _experts_kernel,
        out_shape=jax.ShapeDtypeStruct((n_slots, width), jnp.int32),
        grid_spec=pltpu.PrefetchScalarGridSpec(
            num_scalar_prefetch=1,
            grid=(n_slots // ts,),
            in_specs=[pl.BlockSpec((ts, width), lambda i, te: (i, 0)),
                      pl.BlockSpec((1, d, D_EXPERT), lambda i, te: (te[i], 0, 0)),
                      pl.BlockSpec((1, d, D_EXPERT), lambda i, te: (te[i], 0, 0)),
                      pl.BlockSpec((1, D_EXPERT, d), lambda i, te: (te[i], 0, 0))],
            out_specs=pl.BlockSpec((ts, width), lambda i, te: (i, 0)),
            scratch_shapes=[pltpu.VMEM((d, 2 * D_EXPERT), BF16), pltpu.VMEM((D_EXPERT, d), BF16)]),
        compiler_params=_cparams(("arbitrary",)),
        name="moe_experts",
    )(tile_expert, xs, w_gate, w_up, w_down)


def _gather_slots(y_slots, slot_of, t):
    width = y_slots.shape[1]
    workers = SC_CORES * SC_SUBCORES
    per_worker = t // workers
    n_blocks = (per_worker // SC_ROWS) * TOP_K
    assert per_worker * workers == t and per_worker % SC_ROWS == 0 and n_blocks % 2 == 0
    mesh = plsc.VectorSubcoreMesh(core_axis_name="core", subcore_axis_name="subcore")

    @functools.partial(
        pl.kernel, mesh=mesh,
        out_type=jax.ShapeDtypeStruct((TOP_K * t, width), jnp.int32),
        scratch_types=[pltpu.VMEM((SC_ROWS,), jnp.int32), pltpu.VMEM((SC_ROWS,), jnp.int32),
                       pltpu.VMEM((SC_ROWS, width), jnp.int32), pltpu.VMEM((SC_ROWS, width), jnp.int32),
                       pltpu.SemaphoreType.DMA, pltpu.SemaphoreType.DMA],
    )
    def gather_rows(ys_hbm, slot_hbm, out_hbm, idx0, idx1, rows0, rows1, sem0, sem1):
        worker = lax.axis_index("subcore") * SC_CORES + lax.axis_index("core")
        base = worker * per_worker

        def first_row(n):
            return (n % TOP_K) * t + base + (n // TOP_K) * SC_ROWS

        def start(n, idx_v, rows_v, sem):
            pltpu.sync_copy(slot_hbm.at[pl.ds(first_row(n), SC_ROWS)], idx_v)
            pltpu.async_copy(ys_hbm.at[idx_v], rows_v, sem)

        def finish(n, idx_v, rows_v, sem):
            pltpu.make_async_copy(ys_hbm.at[idx_v], rows_v, sem).wait()
            pltpu.sync_copy(rows_v, out_hbm.at[pl.ds(first_row(n), SC_ROWS)])

        start(0, idx0, rows0, sem0)

        @pl.loop(0, n_blocks, step=2)
        def _(n):
            start(n + 1, idx1, rows1, sem1)
            finish(n, idx0, rows0, sem0)

            @pl.when(n + 2 < n_blocks)
            def _():
                start(n + 2, idx0, rows0, sem0)

            finish(n + 1, idx1, rows1, sem1)

    return gather_rows(y_slots, slot_of)


def _combine_kernel(y_ref, h_ref, g2_ref, gtok_ref, rows_ref, wgs_ref, wus_ref, wds_ref, o_ref):
    h_left, h_right = _unpack_rows(h_ref[...])
    h = jnp.concatenate([h_left, h_right], axis=1).astype(BF16)
    gs = _dot(h, wgs_ref[...])
    us = _dot(h, wus_ref[...])
    shared = _dot(((gs * _sigmoid(gs)) * us).astype(BF16), wds_ref[...])

    gtok = gtok_ref[...]
    acc_left = acc_right = None
    for k in range(TOP_K):
        left, right = _unpack_rows(rows_ref[k])
        gate = gtok[:, k:k + 1]
        acc_left = gate * left if acc_left is None else acc_left + gate * left
        acc_right = gate * right if acc_right is None else acc_right + gate * right
    routed = jnp.concatenate([acc_left, acc_right], axis=1)
    o_ref[...] = y_ref[...] + g2_ref[0] * (routed + shared)


def _combine(y_all, hp_all, gate2, gtok, rows, wgs, wus, wds, *, first_token, tokens, seq):
    d = y_all.shape[1]
    width = hp_all.shape[1]
    tm = MOE_ROW_TM
    tile0 = first_token // tm
    nb = gate2.shape[0]
    tiles_per_batch = seq // tm

    def mod_map(i):
        return ((i // tiles_per_batch) if nb > 1 else 0, 0, 0)

    return pl.pallas_call(
        _combine_kernel,
        out_shape=jax.ShapeDtypeStruct((tokens, d), F32),
        grid=(tokens // tm,),
        in_specs=[pl.BlockSpec((tm, d), lambda i: (tile0 + i, 0)),
                  pl.BlockSpec((tm, width), lambda i: (tile0 + i, 0)),
                  pl.BlockSpec((1, 1, d), mod_map),
                  pl.BlockSpec((tm, LANES), lambda i: (tile0 + i, 0)),
                  pl.BlockSpec((TOP_K, tm, width), lambda i: (0, tile0 + i, 0)),
                  pl.BlockSpec((d, D_SHARED), lambda i: (0, 0)),
                  pl.BlockSpec((d, D_SHARED), lambda i: (0, 0)),
                  pl.BlockSpec((D_SHARED, d), lambda i: (0, 0))],
        out_specs=pl.BlockSpec((tm, d), lambda i: (i, 0)),
        compiler_params=_cparams(("arbitrary",)),
        name="moe_combine",
    )(y_all, hp_all, gate2, gtok, rows, wgs, wus, wds)


def _expert_layout(counts, n_tiles):
    cnt = counts.astype(jnp.int32)
    tiles = (cnt + (MOE_TS - 1)) // MOE_TS
    last_tile = jnp.cumsum(tiles)
    off = (last_tile - tiles) * MOE_TS
    pad_lo = off + cnt
    pad_hi = (off + tiles * MOE_TS).at[N_EXPERTS - 1].set(n_tiles * MOE_TS)
    pad_cnt = pad_hi - pad_lo
    pad_last = jnp.cumsum(pad_cnt)
    shift = pad_lo - (pad_last - pad_cnt)
    j = jnp.arange(N_EXPERTS * MOE_TS, dtype=jnp.int32)
    past = (pad_last[None, :-1] <= j[:, None]).astype(jnp.int32)
    pad_slots = j + shift[0] + jnp.sum(past * (shift[1:] - shift[:-1])[None, :], axis=1)
    tile_ids = jnp.arange(n_tiles, dtype=jnp.int32)
    tile_expert = jnp.minimum(
        jnp.sum((last_tile[None, :] <= tile_ids[:, None]).astype(jnp.int32), axis=1), N_EXPERTS - 1)
    return off, pad_slots, tile_expert


def _rope_tables(n_tokens):
    t = jnp.arange(n_tokens)
    row = (t // GRID_W).astype(F32)
    col = (t % GRID_W).astype(F32)
    nf = HEAD_DIM // 4
    freqs = ROPE_THETA ** (-jnp.arange(nf, dtype=F32) / nf)
    ang_r = row[:, None] * freqs
    ang_c = col[:, None] * freqs
    cos = jnp.concatenate([jnp.cos(ang_r)] * 2 + [jnp.cos(ang_c)] * 2, axis=1)
    sin = jnp.concatenate([-jnp.sin(ang_r), jnp.sin(ang_r), -jnp.sin(ang_c), jnp.sin(ang_c)], axis=1)
    reps = LANES // HEAD_DIM
    return jnp.tile(cos, (1, reps)), jnp.tile(sin, (1, reps))


def _head_gains(qn_a, kn_a, qn_b, kn_b):
    ones = jnp.ones((HEAD_DIM,), F32)
    parts = ([qn_a] * N_HEADS_A + [kn_a] * N_KV_A + [ones] * N_KV_A
             + [qn_b] * N_HEADS_B + [kn_b] * N_HEADS_B + [ones] * N_HEADS_B)
    return jnp.concatenate(parts).reshape(1, IN_COLS).astype(F32)


def _same_head_indicator():
    i = np.arange(MXU_DIM)
    return jnp.asarray((i[:, None] // HEAD_DIM) == (i[None, :] // HEAD_DIM), BF16)


def _token_major(cache):
    b, h, s, hd = cache.shape
    return cache.transpose(0, 2, 1, 3).reshape(b, s, h * hd).astype(BF16)


def kernel(x_prompt, x_sample, cache_k_a, cache_v_a, cache_k_b, cache_v_b, c, c_ctx, w_mod, b_mod, norm1, norm2, w_in, qn_a, kn_a, qn_b, kn_b, rpb, on_a, on_b, w_out, w_router, router_bias, w_gate_e, w_up_e, w_down_e, w_gate_s, w_up_s, w_down_s):
    depth = w_mod.shape[0]
    assert depth == 1
    l = 0
    bp, sp, d = x_prompt.shape
    bs, ss, _ = x_sample.shape

    cvec = jnp.concatenate([c_ctx[None, :], c], axis=0)
    rows = -(-cvec.shape[0] // 8) * 8
    cvec = jnp.pad(cvec, ((0, rows - cvec.shape[0]), (0, 0)))
    mod = _adaln(cvec, w_mod[l], b_mod[l])
    mod_p = [m.reshape(1, 1, d) for m in jnp.split(mod[0:1], 6, axis=-1)]
    mod_s = [m.reshape(bs, 1, d) for m in jnp.split(mod[1:1 + bs], 6, axis=-1)]
    mod_all = [m.reshape(1 + bs, 1, d) for m in jnp.split(mod[0:1 + bs], 6, axis=-1)]

    w_in_bf = w_in[l].astype(BF16)
    w_out_bf = w_out[l].astype(BF16)
    gain = _head_gains(qn_a[l], kn_a[l], qn_b[l], kn_b[l])
    seg = _same_head_indicator()
    n1 = norm1[l].reshape(1, d)
    n2 = norm2[l].reshape(1, d)
    ona = on_a[l].reshape(1, WIDTH_A)
    onb = on_b[l].reshape(1, WIDTH_B)
    wr_t = w_router[l].T
    wr_hi = wr_t.astype(BF16)
    wr_lo = (wr_t - wr_hi.astype(F32)).astype(BF16)
    rbias = router_bias[l].reshape(N_EXPERTS, 1).astype(F32)
    wgs = w_gate_s[l].astype(BF16)
    wus = w_up_s[l].astype(BF16)
    wds = w_down_s[l].astype(BF16)
    t_p = bp * sp
    t_s = bs * ss
    t_all = t_p + t_s

    xp = x_prompt.reshape(t_p, d)
    proj_p, st_ka, st_va, st_kb, st_vb = _project(
        xp, mod_p[0], mod_p[1], n1, w_in_bf, gain, seg, None, tm=sp, seq=sp, states=True)
    oa_p, ob_p = _context_attention(proj_p, seq=sp)

    xs = x_sample.reshape(t_s, d)
    proj_s, = _project(xs, mod_s[0], mod_s[1], n1, w_in_bf, gain, seg, _rope_tables(ss),
                       tm=1024, seq=ss, states=False)
    oa_s = _latent_gqa(proj_s, _token_major(cache_k_a[:, l]), _token_major(cache_v_a[:, l]),
                       _query_norm_bound(qn_a[l]), seq=ss, tq=256)
    bias_t = _neighbourhood_bias(rpb[l], ss // GRID_W)
    ob_s = _latent_neighbourhood(proj_s, _token_major(cache_k_b[:, l]), _token_major(cache_v_b[:, l]),
                                 bias_t, _neighbourhood_bounds(qn_b[l], kn_b[l], rpb[l]), seq=ss)

    y1_all, hp_all, gates_t, rank_t, counts = _merge(
        (xp, oa_p, ob_p), (xs, oa_s, ob_s), ona, onb, w_out_bf, mod_all[2], mod_all[3], mod_all[4], n2,
        wr_hi, wr_lo, rbias, tm=512, lat_seq=ss)
    n_tiles = t_all * TOP_K // MOE_TS + N_EXPERTS
    off, pad_slots, tile_expert = _expert_layout(counts[:, 0], n_tiles)
    pos, gtok = _slots(gates_t, rank_t, off.astype(F32).reshape(N_EXPERTS, 1))
    slot_of = pos.reshape(TOP_K * t_all)
    x_slots = _dispatch(hp_all, jnp.concatenate([slot_of, pad_slots]))
    y_slots = _experts(x_slots, tile_expert, w_gate_e[l], w_up_e[l], w_down_e[l])
    rows = _gather_slots(y_slots, slot_of, t_all).reshape(TOP_K, t_all, d // 2)
    y_p = _combine(y1_all, hp_all, mod_p[5], gtok, rows, wgs, wus, wds,
                   first_token=0, tokens=t_p, seq=sp)
    y_s = _combine(y1_all, hp_all, mod_s[5], gtok, rows, wgs, wus, wds,
                   first_token=t_p, tokens=t_s, seq=ss)

    return (y_p.reshape(bp, sp, d), y_s.reshape(bs, ss, d), st_ka, st_va, st_kb, st_vb)
```

```python
import functools

import numpy as np
import jax
import jax.numpy as jnp
from jax import lax
from jax.experimental import pallas as pl
from jax.experimental.pallas import tpu as pltpu
from jax.experimental.pallas import tpu_sc as plsc

F32 = jnp.float32
BF16 = jnp.bfloat16

D_MODEL = 1024
HEAD_DIM = 64
N_HEADS_A = 8
N_KV_A = 2
GROUP_A = N_HEADS_A // N_KV_A
N_HEADS_B = 8
WIDTH_A = N_HEADS_A * HEAD_DIM
WIDTH_B = N_HEADS_B * HEAD_DIM
KV_WIDTH_A = N_KV_A * HEAD_DIM
IN_COLS = WIDTH_A + 2 * KV_WIDTH_A + 3 * WIDTH_B
GRID_W = 64
ROPE_THETA = 10000.0
NA_KH = 8
NA_KW = 16
N_EXPERTS = 64
N_GROUPS = 8
GROUP_SIZE = N_EXPERTS // N_GROUPS
TOPK_GROUPS = 4
TOP_K = 8
D_EXPERT = 256
D_SHARED = 256
ROUTED_SCALE = 2.5
EPS = 1e-6

LANES = 128
MXU_DIM = 256
MASKED = -1e30

COL_QA = 0
COL_KA = WIDTH_A
COL_VA = COL_KA + KV_WIDTH_A
COL_QB = COL_VA + KV_WIDTH_A
COL_KB = COL_QB + WIDTH_B
COL_VB = COL_KB + WIDTH_B

NA_QROWS = 8
NA_KROWS = 2 * NA_KH
NA_TQ = NA_QROWS * GRID_W
NA_TK = NA_KROWS * GRID_W
NA_KBLK = 256

VMEM_LIMIT = 56 * 1024 * 1024


def _cparams(sem):
    return pltpu.CompilerParams(dimension_semantics=sem, vmem_limit_bytes=VMEM_LIMIT)


def _dot(a, b):
    return jnp.dot(a, b, preferred_element_type=F32)


def _dot_nt(a, b):
    return lax.dot_general(a, b, (((1,), (1,)), ((), ())), preferred_element_type=F32)


def _sigmoid(x):
    return 1.0 / (1.0 + jnp.exp(-x))


def _rms(x):
    return x * lax.rsqrt(jnp.mean(x * x, axis=-1, keepdims=True) + EPS)


def _pack_rows(x):
    n = x.shape[1] // 2
    hi = lax.bitcast_convert_type(x[:, :n].astype(BF16).astype(F32), jnp.int32)
    lo = lax.bitcast_convert_type(x[:, n:].astype(BF16).astype(F32), jnp.int32)
    return hi | lax.shift_right_logical(lo, 16)


def _unpack_rows(w):
    left = lax.bitcast_convert_type(w & jnp.int32(-65536), F32)
    right = lax.bitcast_convert_type(lax.shift_left(w, 16), F32)
    return left, right


def _mod_kernel(c_ref, w_ref, b_ref, o_ref):
    c = c_ref[...]
    s = c * _sigmoid(c)
    o_ref[...] = jnp.dot(s, w_ref[...], preferred_element_type=F32,
                         precision=lax.Precision.HIGHEST) + b_ref[...]


def _adaln(cvec, w_mod, b_mod):
    rows, d = cvec.shape
    n = w_mod.shape[1]
    tn = 512
    return pl.pallas_call(
        _mod_kernel,
        out_shape=jax.ShapeDtypeStruct((rows, n), F32),
        grid=(n // tn,),
        in_specs=[pl.BlockSpec((rows, d), lambda j: (0, 0)),
                  pl.BlockSpec((d, tn), lambda j: (0, j)),
                  pl.BlockSpec((1, tn), lambda j: (0, j))],
        out_specs=pl.BlockSpec((rows, tn), lambda j: (0, j)),
        compiler_params=_cparams(("arbitrary",)),
        name="adaln_mod",
    )(cvec, w_mod, b_mod.reshape(1, n))


def _proj_chunks():
    def split(c0, width, step, *flags):
        return [(c0 + i, min(step, width - i)) + flags for i in range(0, width, step)]
    return (split(COL_QA, WIDTH_A, MXU_DIM, True, True, True)
            + split(COL_KA, KV_WIDTH_A, MXU_DIM, True, True, False)
            + split(COL_VA, KV_WIDTH_A, MXU_DIM, False, False, False)
            + split(COL_QB, WIDTH_B, MXU_DIM, True, False, True)
            + split(COL_KB, WIDTH_B, MXU_DIM, True, False, False)
            + split(COL_VB, WIDTH_B, MXU_DIM, False, False, False))


_PROJ_CHUNKS = _proj_chunks()


def _proj_kernel(*refs, rope, states):
    x_ref, sh_ref, sc_ref, n1_ref, w_ref, gain_ref, seg_ref = refs[:7]
    pos = 7
    if rope:
        cos_ref, sin_ref = refs[pos:pos + 2]
        pos += 2
    out_ref = refs[pos]
    pos += 1
    if states:
        ka_ref, va_ref, kb_ref, vb_ref = refs[pos:pos + 4]
        state_of = {COL_KA: ka_ref, COL_VA: va_ref, COL_KB: kb_ref, COL_VB: vb_ref}

    x = x_ref[...]
    h = _rms(x) * n1_ref[...]
    h = h * (1.0 + sc_ref[0]) + sh_ref[0]
    p = _dot(h.astype(BF16), w_ref[...])

    for c0, w, normed, roped, is_query in _PROJ_CHUNKS:
        pc = p[:, c0:c0 + w]
        if normed:
            seg = seg_ref[0:w, 0:w]
            sq = pc * pc
            hi = sq.astype(BF16)
            lo = (sq - hi.astype(F32)).astype(BF16)
            ss = _dot(hi, seg) + _dot(lo, seg)
            pc = pc * lax.rsqrt(ss * (1.0 / HEAD_DIM) + EPS) * gain_ref[:, c0:c0 + w]
        if states:
            for start, ref in state_of.items():
                if start <= c0 < start + ref.shape[2] * HEAD_DIM:
                    base = (c0 - start) // HEAD_DIM
                    for hh in range(w // HEAD_DIM):
                        ref[0, 0, base + hh] = pc[:, hh * HEAD_DIM:(hh + 1) * HEAD_DIM]
        if rope and roped:
            reps = w // LANES
            cos = jnp.concatenate([cos_ref[...]] * reps, axis=1) if reps > 1 else cos_ref[...]
            sin = jnp.concatenate([sin_ref[...]] * reps, axis=1) if reps > 1 else sin_ref[...]
            lane = lax.broadcasted_iota(jnp.int32, pc.shape, 1)
            first_half = (lane % (HEAD_DIM // 2)) < (HEAD_DIM // 4)
            partner = jnp.where(first_half,
                                pltpu.roll(pc, w - HEAD_DIM // 4, 1),
                                pltpu.roll(pc, HEAD_DIM // 4, 1))
            pc = pc * cos + partner * sin
        if is_query:
            pc = pc * (HEAD_DIM ** -0.5)
        out_ref[:, c0:c0 + w] = pc.astype(BF16)


def _project(x2d, shift, scale, norm1, w_in_bf, gain, seg, rope_tabs, *, tm, seq, states):
    t, d = x2d.shape
    nb = shift.shape[0]
    tiles_per_batch = seq // tm
    rope = rope_tabs is not None

    def mod_map(i):
        return ((i // tiles_per_batch) if nb > 1 else 0, 0, 0)

    in_specs = [pl.BlockSpec((tm, d), lambda i: (i, 0)),
                pl.BlockSpec((1, 1, d), mod_map),
                pl.BlockSpec((1, 1, d), mod_map),
                pl.BlockSpec((1, d), lambda i: (0, 0)),
                pl.BlockSpec((d, IN_COLS), lambda i: (0, 0)),
                pl.BlockSpec((1, IN_COLS), lambda i: (0, 0)),
                pl.BlockSpec((MXU_DIM, MXU_DIM), lambda i: (0, 0))]
    args = [x2d, shift, scale, norm1, w_in_bf, gain, seg]
    if rope:
        in_specs += [pl.BlockSpec((tm, LANES), lambda i: (i % tiles_per_batch, 0))] * 2
        args += list(rope_tabs)
    out_shape = [jax.ShapeDtypeStruct((t, IN_COLS), BF16)]
    out_specs = [pl.BlockSpec((tm, IN_COLS), lambda i: (i, 0))]
    if states:
        assert tm == seq
        b = t // seq
        for nh in (N_KV_A, N_KV_A, N_HEADS_B, N_HEADS_B):
            out_shape.append(jax.ShapeDtypeStruct((b, 1, nh, seq, HEAD_DIM), F32))
            out_specs.append(pl.BlockSpec((1, 1, nh, seq, HEAD_DIM), lambda i: (i, 0, 0, 0, 0)))
    return pl.pallas_call(
        functools.partial(_proj_kernel, rope=rope, states=states),
        out_shape=out_shape,
        grid=(t // tm,),
        in_specs=in_specs,
        out_specs=out_specs,
        compiler_params=_cparams(("arbitrary",)),
        name="proj_states" if states else "proj_rope",
    )(*args)


def _lane_half(shape):
    return lax.broadcasted_iota(jnp.int32, shape, 1) // HEAD_DIM


def _keep_half(x, half):
    return jnp.where(_lane_half(x.shape) == half, x, jnp.zeros_like(x))


def _transpose_bf16(x):
    return x.astype(F32).T.astype(BF16)


def _attend(q, keys, values_t, biases):
    return _softmax_av(_scores(q, keys, biases), values_t)


def _scores(q, keys, biases):
    scores = []
    for k, b in zip(keys, biases):
        s = _dot_nt(k, q)
        if b is not None:
            s = s + b
        scores.append(s)
    return scores


def _softmax_av(scores, values_t):
    m = functools.reduce(jnp.maximum, [jnp.max(s, axis=0, keepdims=True) for s in scores])
    denom = None
    out = None
    for s, vt in zip(scores, values_t):
        p = jnp.exp(s - m)
        ps = jnp.sum(p, axis=0, keepdims=True)
        po = _dot(vt, p.astype(BF16))
        denom = ps if denom is None else denom + ps
        out = po if out is None else out + po
    return out / denom


def _swap_halves(q_bf16):
    return pltpu.roll(q_bf16.astype(F32), HEAD_DIM, 1).astype(BF16)


def _gqa_heads(q_of_pair, keys_by_group, values_t):
    outs = []
    for h in range(N_HEADS_A):
        g = h // GROUP_A
        q = q_of_pair(h // 2)
        if h % 2 != g:
            q = _swap_halves(q)
        o = _attend(q, keys_by_group[g], values_t, [None] * len(values_t))
        outs.append(o[g * HEAD_DIM:(g + 1) * HEAD_DIM])
    return jnp.concatenate(outs, axis=0)


def _ctx_attn_kernel(p_ref, oa_ref, ob_ref):
    ka = p_ref[:, COL_KA:COL_KA + LANES]
    va_t = [_transpose_bf16(p_ref[:, COL_VA:COL_VA + LANES])]
    keys_by_group = [[_keep_half(ka, g)] for g in range(N_KV_A)]
    oa = _gqa_heads(lambda i: p_ref[:, COL_QA + i * LANES:COL_QA + (i + 1) * LANES],
                    keys_by_group, va_t)
    oa_ref[...] = oa.T

    outs = []
    for i in range(N_HEADS_B // 2):
        q = p_ref[:, COL_QB + i * LANES:COL_QB + (i + 1) * LANES]
        k = p_ref[:, COL_KB + i * LANES:COL_KB + (i + 1) * LANES]
        vt = [_transpose_bf16(p_ref[:, COL_VB + i * LANES:COL_VB + (i + 1) * LANES])]
        for half in range(2):
            o = _attend(q, [_keep_half(k, half)], vt, [None])
            outs.append(o[half * HEAD_DIM:(half + 1) * HEAD_DIM])
    ob_ref[...] = jnp.concatenate(outs, axis=0).T


def _context_attention(proj, *, seq):
    t = proj.shape[0]
    return pl.pallas_call(
        _ctx_attn_kernel,
        out_shape=[jax.ShapeDtypeStruct((t, WIDTH_A), F32), jax.ShapeDtypeStruct((t, WIDTH_B), F32)],
        grid=(t // seq,),
        in_specs=[pl.BlockSpec((seq, IN_COLS), lambda i: (i, 0))],
        out_specs=[pl.BlockSpec((seq, WIDTH_A), lambda i: (i, 0)),
                   pl.BlockSpec((seq, WIDTH_B), lambda i: (i, 0))],
        compiler_params=_cparams(("arbitrary",)),
        name="context_attention",
    )(proj)


ATTN_SAFE_SHIFT = 40.0
ONES_ROWS = 16


def _round_up_bf16(x):
    return (x * (1.0 + 2.0 ** -6)).astype(BF16).astype(F32)


def _query_norm_bound(gain):
    return jnp.max(jnp.abs(gain)).reshape(1, 1).astype(F32)


def _ones_lane(g):
    return (1 - g) * HEAD_DIM


def _gqa_latent_kernel(q_ref, k_ref, v_ref, ck_ref, cv_ref, qmax_ref, o_ref,
                       kg_ref, ckg_ref, vt_ref, cvt_ref, shift_ref):
    lane_k = lax.broadcasted_iota(jnp.int32, (1, LANES), 1)

    @pl.when(pl.program_id(1) == 0)
    def _():
        k = k_ref[...]
        ck = ck_ref[0]
        vt = v_ref[...].astype(F32).T
        cvt = cv_ref[0].astype(F32).T
        for g in range(N_KV_A):
            kf = _keep_half(k, g).astype(F32)
            ckf = _keep_half(ck, g).astype(F32)
            ksq = jnp.maximum(jnp.max(jnp.sum(kf * kf, axis=1, keepdims=True), axis=0, keepdims=True),
                              jnp.max(jnp.sum(ckf * ckf, axis=1, keepdims=True), axis=0, keepdims=True))
            shift_ref[g] = jnp.broadcast_to(_round_up_bf16(qmax_ref[...] * jnp.sqrt(ksq)), shift_ref.shape[1:])
            kg_ref[g] = jnp.where(lane_k == _ones_lane(g), 1.0, kf).astype(BF16)
            ckg_ref[g] = jnp.where(lane_k == _ones_lane(g), 1.0, ckf).astype(BF16)
            rows = slice(g * HEAD_DIM, (g + 1) * HEAD_DIM)
            vt_ref[g] = jnp.concatenate([vt[rows], jnp.ones((ONES_ROWS, vt.shape[1]), F32)], axis=0).astype(BF16)
            cvt_ref[g] = jnp.concatenate([cvt[rows], jnp.ones((ONES_ROWS, cvt.shape[1]), F32)], axis=0).astype(BF16)

    tq = q_ref.shape[0]
    lane_q = lax.broadcasted_iota(jnp.int32, (GROUP_A * tq, LANES), 1)
    queries, shifts = [], []
    for g in range(N_KV_A):
        qs = []
        for j in range(GROUP_A):
            h = g * GROUP_A + j
            q = q_ref[:, (h // 2) * LANES:(h // 2 + 1) * LANES].astype(F32)
            qs.append(q if h % 2 == g else pltpu.roll(q, HEAD_DIM, 1))
        queries.append(jnp.where(lane_q // HEAD_DIM == g, jnp.concatenate(qs, axis=0), 0.0))
        shifts.append(shift_ref[g][0:1, 0:1])
    safe = jnp.max(jnp.maximum(shift_ref[0], shift_ref[1])) <= ATTN_SAFE_SHIFT

    def attend(g, p_lat, p_ctx):
        half = p_lat.shape[1] // 2
        o = jnp.concatenate([_dot(vt_ref[g], p_lat[:, :half]) + _dot(cvt_ref[g], p_ctx[:, :half]),
                             _dot(vt_ref[g], p_lat[:, half:]) + _dot(cvt_ref[g], p_ctx[:, half:])],
                            axis=1)
        o = o[:HEAD_DIM] / o[HEAD_DIM:HEAD_DIM + 1]
        heads = jnp.concatenate([o[:, j * tq:(j + 1) * tq] for j in range(GROUP_A)], axis=0)
        o_ref[:, g * GROUP_A * HEAD_DIM:(g + 1) * GROUP_A * HEAD_DIM] = heads.T

    def with_bound():
        for g in range(N_KV_A):
            qa = jnp.where(lane_q == _ones_lane(g), -shifts[g], queries[g]).astype(BF16)
            attend(g, jnp.exp(_dot_nt(kg_ref[g], qa)).astype(BF16), jnp.exp(_dot_nt(ckg_ref[g], qa)).astype(BF16))

    def with_row_max():
        for g in range(N_KV_A):
            qa = queries[g].astype(BF16)
            s_lat = _dot_nt(kg_ref[g], qa)
            s_ctx = _dot_nt(ckg_ref[g], qa)
            m = jnp.maximum(jnp.max(s_lat, axis=0, keepdims=True), jnp.max(s_ctx, axis=0, keepdims=True))
            attend(g, jnp.exp(s_lat - m).astype(BF16), jnp.exp(s_ctx - m).astype(BF16))

    pl.when(safe)(with_bound)
    pl.when(jnp.logical_not(safe))(with_row_max)


def _latent_gqa(proj, ctx_k, ctx_v, qmax, *, seq, tq):
    t = proj.shape[0]
    b = t // seq
    nq = seq // tq
    past = ctx_k.shape[1]
    return pl.pallas_call(
        _gqa_latent_kernel,
        out_shape=jax.ShapeDtypeStruct((t, WIDTH_A), F32),
        grid=(b, nq),
        in_specs=[pl.BlockSpec((tq, WIDTH_A), lambda bi, qi: (bi * nq + qi, 0)),
                  pl.BlockSpec((seq, LANES), lambda bi, qi: (bi, COL_KA // LANES)),
                  pl.BlockSpec((seq, LANES), lambda bi, qi: (bi, COL_VA // LANES)),
                  pl.BlockSpec((1, past, LANES), lambda bi, qi: (bi, 0, 0)),
                  pl.BlockSpec((1, past, LANES), lambda bi, qi: (bi, 0, 0)),
                  pl.BlockSpec((1, 1), lambda bi, qi: (0, 0))],
        out_specs=pl.BlockSpec((tq, WIDTH_A), lambda bi, qi: (bi * nq + qi, 0)),
        scratch_shapes=[pltpu.VMEM((N_KV_A, seq, LANES), BF16),
                        pltpu.VMEM((N_KV_A, past, LANES), BF16),
                        pltpu.VMEM((N_KV_A, HEAD_DIM + ONES_ROWS, seq), BF16),
                        pltpu.VMEM((N_KV_A, HEAD_DIM + ONES_ROWS, past), BF16),
                        pltpu.VMEM((N_KV_A, 8, LANES), F32)],
        compiler_params=_cparams(("arbitrary", "arbitrary")),
        name="latent_gqa",
    )(proj, proj, proj, ctx_k, ctx_v, qmax)


def _na_kernel(q_ref, k_ref, v_ref, ck_ref, cv_ref, bias_ref, bound_ref, o_ref, keys_ref, vt_ref, shift_ref,
               *, rows):
    i = pl.program_id(2)
    n_kblk = k_ref.shape[0] // NA_KBLK
    lane_k = lax.broadcasted_iota(jnp.int32, (1, LANES), 1)
    one = jnp.ones((), BF16)

    @pl.when(i == 0)
    def _():
        k = k_ref[...]
        ck = ck_ref[0]
        for half in range(2):
            kh = jnp.where(lane_k == _ones_lane(half), one, _keep_half(k, half))
            keys_ref[half, 0:n_kblk] = kh.reshape(n_kblk, NA_KBLK, LANES)
            ckh = _keep_half(ck, half)
            keys_ref[half, n_kblk] = jnp.where(lane_k == _ones_lane(half), one, ckh)
            ckf = ckh.astype(F32)
            ctx_norm = jnp.sqrt(jnp.max(jnp.sum(ckf * ckf, axis=1, keepdims=True), axis=0, keepdims=True))
            consts = bound_ref[0, half:half + 1, :]
            kmax = jnp.maximum(ctx_norm, consts[:, 2:3])
            shift_ref[half] = jnp.broadcast_to(_round_up_bf16(consts[:, 0:1] * kmax + consts[:, 1:2]),
                                               shift_ref.shape[1:])
        ones_rows = jnp.ones((ONES_ROWS, NA_KBLK), F32)
        vt = v_ref[...].astype(F32).T
        for j in range(n_kblk):
            vt_ref[j] = jnp.concatenate([vt[:, j * NA_KBLK:(j + 1) * NA_KBLK], ones_rows], axis=0).astype(BF16)
        vt_ref[n_kblk] = jnp.concatenate([cv_ref[0].astype(F32).T, ones_rows], axis=0).astype(BF16)

    q = q_ref[...]
    lane_q = lax.broadcasted_iota(jnp.int32, q.shape, 1)
    first = _na_first_key_block(i, rows)
    n_qblk = rows // NA_QROWS
    variant = jnp.where(i == 0, 0, jnp.where(i == n_qblk - 1, 2, 1))
    blocks = [first + j for j in range(NA_TK // NA_KBLK)] + [n_kblk]
    values_t = [vt_ref[blk] for blk in blocks]
    heads = []
    for half in range(2):
        keys = [keys_ref[half, blk] for blk in blocks]
        biases = [bias_ref[variant, half, j * NA_KBLK:(j + 1) * NA_KBLK, :] for j in range(NA_TK // NA_KBLK)] + [None]
        heads.append((shift_ref[half][0:1, 0:1], keys, biases))
    safe = jnp.max(jnp.maximum(shift_ref[0], shift_ref[1])) <= ATTN_SAFE_SHIFT

    def attend(probabilities):
        outs = []
        for half, ps in enumerate(probabilities):
            o = functools.reduce(lambda a, b: a + b, [_dot(vt, p) for vt, p in zip(values_t, ps)])
            outs.append(o[half * HEAD_DIM:(half + 1) * HEAD_DIM] / o[2 * HEAD_DIM:2 * HEAD_DIM + 1])
        o_ref[...] = jnp.concatenate(outs, axis=0).T

    def with_bound():
        probabilities = []
        for half, (shift, keys, biases) in enumerate(heads):
            qa = jnp.where(lane_q == _ones_lane(half), (-shift).astype(BF16), _keep_half(q, half))
            probabilities.append([jnp.exp(s).astype(BF16) for s in _scores(qa, keys, biases)])
        attend(probabilities)

    def with_row_max():
        all_scores = [_scores(_keep_half(q, half), keys, biases) for half, (_, keys, biases) in enumerate(heads)]
        probabilities = []
        for scores in all_scores:
            m = functools.reduce(jnp.maximum, [jnp.max(s, axis=0, keepdims=True) for s in scores])
            probabilities.append([jnp.exp(s - m).astype(BF16) for s in scores])
        attend(probabilities)

    pl.when(safe)(with_bound)
    pl.when(jnp.logical_not(safe))(with_row_max)


def _na_first_key_block(i, rows):
    per_qblock = NA_QROWS * GRID_W // NA_KBLK
    lead = (NA_KH // 2) * GRID_W // NA_KBLK
    return jnp.clip(per_qblock * i - lead, 0, (rows - NA_KROWS) * GRID_W // NA_KBLK)


def _latent_neighbourhood(proj, ctx_k, ctx_v, bias_t, bounds, *, seq):
    t = proj.shape[0]
    b = t // seq
    rows = seq // GRID_W
    nblk = rows // NA_QROWS
    n_kblk = seq // NA_KBLK
    past = ctx_k.shape[1]
    assert past == NA_KBLK
    grid = (N_HEADS_B // 2, b, nblk)
    in_specs = [pl.BlockSpec((NA_TQ, LANES), lambda hp, bi, i: (bi * nblk + i, COL_QB // LANES + hp)),
                pl.BlockSpec((seq, LANES), lambda hp, bi, i: (bi, COL_KB // LANES + hp)),
                pl.BlockSpec((seq, LANES), lambda hp, bi, i: (bi, COL_VB // LANES + hp)),
                pl.BlockSpec((1, past, LANES), lambda hp, bi, i: (bi, 0, hp)),
                pl.BlockSpec((1, past, LANES), lambda hp, bi, i: (bi, 0, hp)),
                pl.BlockSpec((3, 2, NA_TK, NA_TQ), lambda hp, bi, i: (0, hp, 0, 0)),
                pl.BlockSpec((1, 2, LANES), lambda hp, bi, i: (hp, 0, 0))]
    return pl.pallas_call(
        functools.partial(_na_kernel, rows=rows),
        out_shape=jax.ShapeDtypeStruct((t, WIDTH_B), F32),
        grid=grid,
        in_specs=in_specs,
        out_specs=pl.BlockSpec((NA_TQ, LANES), lambda hp, bi, i: (bi * nblk + i, hp)),
        scratch_shapes=[pltpu.VMEM((2, n_kblk + 1, NA_KBLK, LANES), BF16),
                        pltpu.VMEM((n_kblk + 1, 2 * HEAD_DIM + ONES_ROWS, NA_KBLK), BF16),
                        pltpu.VMEM((2, 8, LANES), F32)],
        compiler_params=_cparams(("arbitrary", "arbitrary", "arbitrary")),
        name="latent_neighbourhood",
    )(proj, proj, proj, ctx_k, ctx_v, bias_t, bounds)


def _neighbourhood_bounds(qn_b, kn_b, rpb):
    n_heads = rpb.shape[0]
    qmax = jnp.broadcast_to(_query_norm_bound(qn_b), (n_heads, 1))
    kmax = jnp.broadcast_to(_query_norm_bound(kn_b) * (HEAD_DIM ** 0.5), (n_heads, 1))
    bmax = jnp.maximum(jnp.max(rpb.reshape(n_heads, -1), axis=1, keepdims=True), 0.0).astype(F32)
    table = jnp.concatenate([qmax, bmax, kmax, jnp.zeros((n_heads, LANES - 3), F32)], axis=1)
    return table.reshape(n_heads // 2, 2, LANES)


def _neighbourhood_bias(rpb, rows):
    nblk = rows // NA_QROWS
    n_dr = 2 * NA_KH - 1
    n_dc = 2 * NA_KW - 1
    kc = np.arange(GRID_W)[:, None]
    qc = np.arange(GRID_W)[None, :]
    ws = np.clip(qc - NA_KW // 2, 0, GRID_W - NA_KW)
    col_ok = (kc >= ws) & (kc < ws + NA_KW)
    dc = np.clip(kc - qc + NA_KW - 1, 0, n_dc - 1)
    dc_onehot = (dc[None] == np.arange(n_dc)[:, None, None]).astype(np.float32)
    tiles = jnp.einsum('hab,bkq->hakq', rpb.astype(F32), jnp.asarray(dc_onehot),
                       precision=lax.Precision.HIGHEST)
    tiles = jnp.where(jnp.asarray(col_ok)[None, None], tiles, MASKED)
    masked_tile = jnp.full((rpb.shape[0], 1, GRID_W, GRID_W), MASKED, F32)
    tiles = jnp.concatenate([tiles, masked_tile], axis=1)
    pick = np.zeros((3, NA_KROWS, NA_QROWS, n_dr + 1), np.float32)
    for v, i in enumerate((0, 1, nblk - 1)):
        r0 = i * NA_QROWS
        ks = int(np.clip(r0 - NA_KH // 2, 0, rows - NA_KROWS))
        for kl in range(NA_KROWS):
            for ql in range(NA_QROWS):
                kr, qr = ks + kl, r0 + ql
                rs = int(np.clip(qr - NA_KH // 2, 0, rows - NA_KH))
                ok = rs <= kr < rs + NA_KH
                pick[v, kl, ql, (kr - qr + NA_KH - 1) if ok else n_dr] = 1.0
    bias = jnp.einsum('vkqa,hacd->vhkcqd', jnp.asarray(pick), tiles, precision=lax.Precision.HIGHEST)
    return bias.reshape(3, rpb.shape[0], NA_TK, NA_TQ)


def _merge_kernel(xp_ref, oap_ref, obp_ref, xs_ref, oas_ref, obs_ref, ona_ref, onb_ref, wo_ref,
                  g1_ref, sh2_ref, sc2_ref, n2_ref, wrh_ref, wrl_ref, rb_ref, tri_ref,
                  y_ref, hp_ref, gates_ref, rank_ref, count_ref, *, ctx_tiles):
    i = pl.program_id(0)

    @pl.when(i == 0)
    def _():
        count_ref[...] = jnp.zeros_like(count_ref)

    def one_stream(x_ref, oa_ref, ob_ref):
        na = (_rms(oa_ref[...]) * ona_ref[...]).astype(BF16)
        nb = (_rms(ob_ref[...]) * onb_ref[...]).astype(BF16)
        mix = _dot(na, wo_ref[0:WIDTH_A, :]) + _dot(nb, wo_ref[WIDTH_A:WIDTH_A + WIDTH_B, :])
        y = x_ref[...] + g1_ref[0] * mix
        y_ref[...] = y
        h = _rms(y) * n2_ref[...]
        h = h * (1.0 + sc2_ref[0]) + sh2_ref[0]
        hp_ref[...] = _pack_rows(h)
        gates, chosen = _router_gates(h, wrh_ref[...], wrl_ref[...], rb_ref[...])
        gates_ref[...] = gates
        before = _dot(chosen.astype(BF16), tri_ref[...])
        seen = count_ref[...]
        rank_ref[...] = jnp.where(chosen > 0.0, before + seen[:, 0:1], -1.0)
        count_ref[...] = seen + jnp.sum(chosen, axis=1, keepdims=True)

    pl.when(i < ctx_tiles)(lambda: one_stream(xp_ref, oap_ref, obp_ref))
    pl.when(i >= ctx_tiles)(lambda: one_stream(xs_ref, oas_ref, obs_ref))


def _merge(ctx, lat, on_a, on_b, w_out_bf, gate1, shift2, scale2, norm2, wr_hi, wr_lo, rbias, *, tm, lat_seq):
    t_c, d = ctx[0].shape
    t_l = lat[0].shape[0]
    t = t_c + t_l
    tri = jnp.asarray(np.triu(np.ones((tm, tm), np.float32), k=1), BF16)
    ctx_tiles = t_c // tm
    lat_tiles_per_batch = lat_seq // tm

    def ctx_map(i):
        return (jnp.minimum(i, ctx_tiles - 1), 0)

    def lat_map(i):
        return (jnp.maximum(i - ctx_tiles, 0), 0)

    def mod_map(i):
        return (jnp.where(i < ctx_tiles, 0, 1 + (i - ctx_tiles) // lat_tiles_per_batch), 0, 0)

    def stream_specs(index_map):
        return [pl.BlockSpec((tm, d), index_map),
                pl.BlockSpec((tm, WIDTH_A), index_map),
                pl.BlockSpec((tm, WIDTH_B), index_map)]

    return pl.pallas_call(
        functools.partial(_merge_kernel, ctx_tiles=ctx_tiles),
        out_shape=[jax.ShapeDtypeStruct((t, d), F32),
                   jax.ShapeDtypeStruct((t, d // 2), jnp.int32),
                   jax.ShapeDtypeStruct((N_EXPERTS, t), F32),
                   jax.ShapeDtypeStruct((N_EXPERTS, t), F32),
                   jax.ShapeDtypeStruct((N_EXPERTS, LANES), F32)],
        grid=(t // tm,),
        in_specs=stream_specs(ctx_map) + stream_specs(lat_map) + [
            pl.BlockSpec((1, WIDTH_A), lambda i: (0, 0)),
            pl.BlockSpec((1, WIDTH_B), lambda i: (0, 0)),
            pl.BlockSpec((WIDTH_A + WIDTH_B, d), lambda i: (0, 0)),
            pl.BlockSpec((1, 1, d), mod_map),
            pl.BlockSpec((1, 1, d), mod_map),
            pl.BlockSpec((1, 1, d), mod_map),
            pl.BlockSpec((1, d), lambda i: (0, 0)),
            pl.BlockSpec((N_EXPERTS, d), lambda i: (0, 0)),
            pl.BlockSpec((N_EXPERTS, d), lambda i: (0, 0)),
            pl.BlockSpec((N_EXPERTS, 1), lambda i: (0, 0)),
            pl.BlockSpec((tm, tm), lambda i: (0, 0))],
        out_specs=[pl.BlockSpec((tm, d), lambda i: (i, 0)),
                   pl.BlockSpec((tm, d // 2), lambda i: (i, 0)),
                   pl.BlockSpec((N_EXPERTS, tm), lambda i: (0, i)),
                   pl.BlockSpec((N_EXPERTS, tm), lambda i: (0, i)),
                   pl.BlockSpec((N_EXPERTS, LANES), lambda i: (0, 0))],
        compiler_params=_cparams(("arbitrary",)),
        name="merge_route",
    )(*ctx, *lat, on_a, on_b, w_out_bf, gate1, shift2, scale2, norm2, wr_hi, wr_lo, rbias, tri)


def _first_index_of_max(x, iota):
    mx = jnp.max(x, axis=0, keepdims=True)
    idx = jnp.min(jnp.where(x == mx, iota, float(x.shape[0])), axis=0, keepdims=True)
    return mx, iota == idx


def _router_gates(h, wr_hi, wr_lo, rbias):
    h_hi = h.astype(BF16)
    h_lo = (h - h_hi.astype(F32)).astype(BF16)
    logits = _dot_nt(wr_hi, h_hi) + (_dot_nt(wr_lo, h_hi) + _dot_nt(wr_hi, h_lo))
    scores = _sigmoid(logits)
    sel = scores + rbias
    tm = sel.shape[1]
    iota_g = lax.broadcasted_iota(jnp.int32, (GROUP_SIZE, tm), 0).astype(F32)
    group_scores = []
    for g in range(N_GROUPS):
        grp = sel[g * GROUP_SIZE:(g + 1) * GROUP_SIZE]
        m1, first = _first_index_of_max(grp, iota_g)
        m2 = jnp.max(jnp.where(first, -jnp.inf, grp), axis=0, keepdims=True)
        group_scores.append(m1 + m2)
    gs = jnp.concatenate(group_scores, axis=0)
    iota_n = lax.broadcasted_iota(jnp.int32, (N_GROUPS, tm), 0).astype(F32)
    group_on = jnp.zeros((N_GROUPS, tm), F32)
    for _ in range(TOPK_GROUPS):
        _, pick = _first_index_of_max(gs, iota_n)
        group_on = jnp.where(pick, 1.0, group_on)
        gs = jnp.where(pick, -jnp.inf, gs)
    expert_on = jnp.concatenate(
        [jnp.broadcast_to(group_on[g:g + 1], (GROUP_SIZE, tm)) for g in range(N_GROUPS)], axis=0)
    cand = jnp.where(expert_on > 0.0, sel, -jnp.inf)
    iota_e = lax.broadcasted_iota(jnp.int32, (N_EXPERTS, tm), 0).astype(F32)
    w = jnp.zeros((N_EXPERTS, tm), F32)
    chosen = jnp.zeros((N_EXPERTS, tm), F32)
    for _ in range(TOP_K):
        _, pick = _first_index_of_max(cand, iota_e)
        w = jnp.where(pick, scores, w)
        chosen = jnp.where(pick, 1.0, chosen)
        cand = jnp.where(pick, -jnp.inf, cand)
    return w / jnp.sum(w, axis=0, keepdims=True) * ROUTED_SCALE, chosen


MOE_TS = 1024
MOE_ROUTE_TM = 1024
MOE_ROW_TM = 512


def _slots_kernel(gates_ref, rank_ref, off_ref, pos_ref, gtok_ref):
    gates = gates_ref[...]
    rank = rank_ref[...]
    tm = gates.shape[1]
    slot = off_ref[...] + rank
    left = jnp.where(rank >= 0.0, 1.0, 0.0)
    iota_e = lax.broadcasted_iota(jnp.int32, (N_EXPERTS, tm), 0).astype(F32)
    pos_rows, gate_rows = [], []
    for _ in range(TOP_K):
        _, pick = _first_index_of_max(left, iota_e)
        pos_rows.append(jnp.sum(jnp.where(pick, slot, 0.0), axis=0, keepdims=True))
        gate_rows.append(jnp.sum(jnp.where(pick, gates, 0.0), axis=0, keepdims=True))
        left = jnp.where(pick, 0.0, left)
    pos_ref[...] = jnp.concatenate(pos_rows, axis=0).astype(jnp.int32)
    pad = jnp.zeros((LANES - TOP_K, tm), F32)
    gtok_ref[...] = jnp.concatenate(gate_rows + [pad], axis=0).T


def _slots(gates_t, rank_t, off):
    t = gates_t.shape[1]
    tm = MOE_ROUTE_TM
    return pl.pallas_call(
        _slots_kernel,
        out_shape=[jax.ShapeDtypeStruct((TOP_K, t), jnp.int32), jax.ShapeDtypeStruct((t, LANES), F32)],
        grid=(t // tm,),
        in_specs=[pl.BlockSpec((N_EXPERTS, tm), lambda i: (0, i)),
                  pl.BlockSpec((N_EXPERTS, tm), lambda i: (0, i)),
                  pl.BlockSpec((N_EXPERTS, 1), lambda i: (0, 0))],
        out_specs=[pl.BlockSpec((TOP_K, tm), lambda i: (0, i)),
                   pl.BlockSpec((tm, LANES), lambda i: (i, 0))],
        compiler_params=_cparams(("arbitrary",)),
        name="moe_slots",
    )(gates_t, rank_t, off)


SC_CORES = 2
SC_SUBCORES = 16
SC_ROWS = 64


def _dispatch(hp_all, slot_of):
    t, width = hp_all.shape
    n_slots = slot_of.shape[0]
    n_pad = n_slots - TOP_K * t
    workers = SC_CORES * SC_SUBCORES
    per_worker = t // workers
    pad_per_worker = n_pad // workers
    assert per_worker * workers == t and per_worker % SC_ROWS == 0
    assert pad_per_worker * workers == n_pad and pad_per_worker % SC_ROWS == 0
    mesh = plsc.VectorSubcoreMesh(core_axis_name="core", subcore_axis_name="subcore")

    @functools.partial(
        pl.kernel, mesh=mesh,
        out_type=jax.ShapeDtypeStruct((n_slots, width), jnp.int32),
        scratch_types=[pltpu.VMEM((SC_ROWS,), jnp.int32),
                       pltpu.VMEM((SC_ROWS, width), jnp.int32),
                       pltpu.SemaphoreType.DMA],
    )
    def scatter_rows(h_hbm, slot_hbm, out_hbm, idx_v, rows_v, sem):
        worker = lax.axis_index("subcore") * SC_CORES + lax.axis_index("core")
        base = worker * per_worker

        @pl.loop(0, per_worker // SC_ROWS)
        def _(j):
            first = base + j * SC_ROWS
            pltpu.sync_copy(h_hbm.at[pl.ds(first, SC_ROWS)], rows_v)
            for k in range(TOP_K):
                pltpu.sync_copy(slot_hbm.at[pl.ds(k * t + first, SC_ROWS)], idx_v)
                pltpu.async_copy(rows_v, out_hbm.at[idx_v], sem).wait()

        pltpu.sync_copy(h_hbm.at[pl.ds(0, SC_ROWS)], rows_v)
        pad_base = TOP_K * t + worker * pad_per_worker

        @pl.loop(0, pad_per_worker // SC_ROWS)
        def _(j):
            pltpu.sync_copy(slot_hbm.at[pl.ds(pad_base + j * SC_ROWS, SC_ROWS)], idx_v)
            pltpu.async_copy(rows_v, out_hbm.at[idx_v], sem).wait()

    return scatter_rows(hp_all, slot_of)


def _experts_kernel(te_ref, xs_ref, wg_ref, wu_ref, wd_ref, ys_ref, wgu_bf, wd_bf):
    i = pl.program_id(0)

    @pl.when((i == 0) | (te_ref[i] != te_ref[jnp.maximum(i, 1) - 1]))
    def _():
        wgu_bf[:, 0:D_EXPERT] = wg_ref[0].astype(BF16)
        wgu_bf[:, D_EXPERT:2 * D_EXPERT] = wu_ref[0].astype(BF16)
        wd_bf[...] = wd_ref[0].astype(BF16)

    left, right = _unpack_rows(xs_ref[...])
    x = jnp.concatenate([left, right], axis=1).astype(BF16)
    gu = _dot(x, wgu_bf[...])
    g = gu[:, 0:D_EXPERT]
    u = gu[:, D_EXPERT:2 * D_EXPERT]
    act = (g * _sigmoid(g)) * u
    ys_ref[...] = _pack_rows(_dot(act.astype(BF16), wd_bf[...]))


def _experts(xs, tile_expert, w_gate, w_up, w_down):
    n_slots, width = xs.shape
    d = 2 * width
    ts = MOE_TS
    return pl.pallas_call(
        _experts_kernel,
        out_shape=jax.ShapeDtypeStruct((n_slots, width), jnp.int32),
        grid_spec=pltpu.PrefetchScalarGridSpec(
            num_scalar_prefetch=1,
            grid=(n_slots // ts,),
            in_specs=[pl.BlockSpec((ts, width), lambda i, te: (i, 0)),
                      pl.BlockSpec((1, d, D_EXPERT), lambda i, te: (te[i], 0, 0)),
                      pl.BlockSpec((1, d, D_EXPERT), lambda i, te: (te[i], 0, 0)),
                      pl.BlockSpec((1, D_EXPERT, d), lambda i, te: (te[i], 0, 0))],
            out_specs=pl.BlockSpec((ts, width), lambda i, te: (i, 0)),
            scratch_shapes=[pltpu.VMEM((d, 2 * D_EXPERT), BF16), pltpu.VMEM((D_EXPERT, d), BF16)]),
        compiler_params=_cparams(("arbitrary",)),
        name="moe_experts",
    )(tile_expert, xs, w_gate, w_up, w_down)


def _gather_slots(y_slots, slot_of, t):
    width = y_slots.shape[1]
    workers = SC_CORES * SC_SUBCORES
    per_worker = t // workers
    n_blocks = (per_worker // SC_ROWS) * TOP_K
    assert per_worker * workers == t and per_worker % SC_ROWS == 0 and n_blocks % 2 == 0
    mesh = plsc.VectorSubcoreMesh(core_axis_name="core", subcore_axis_name="subcore")

    @functools.partial(
        pl.kernel, mesh=mesh,
        out_type=jax.ShapeDtypeStruct((TOP_K * t, width), jnp.int32),
        scratch_types=[pltpu.VMEM((SC_ROWS,), jnp.int32), pltpu.VMEM((SC_ROWS,), jnp.int32),
                       pltpu.VMEM((SC_ROWS, width), jnp.int32), pltpu.VMEM((SC_ROWS, width), jnp.int32),
                       pltpu.SemaphoreType.DMA, pltpu.SemaphoreType.DMA],
    )
    def gather_rows(ys_hbm, slot_hbm, out_hbm, idx0, idx1, rows0, rows1, sem0, sem1):
        worker = lax.axis_index("subcore") * SC_CORES + lax.axis_index("core")
        base = worker * per_worker

        def first_row(n):
            return (n % TOP_K) * t + base + (n // TOP_K) * SC_ROWS

        def start(n, idx_v, rows_v, sem):
            pltpu.sync_copy(slot_hbm.at[pl.ds(first_row(n), SC_ROWS)], idx_v)
            pltpu.async_copy(ys_hbm.at[idx_v], rows_v, sem)

        def finish(n, idx_v, rows_v, sem):
            pltpu.make_async_copy(ys_hbm.at[idx_v], rows_v, sem).wait()
            pltpu.sync_copy(rows_v, out_hbm.at[pl.ds(first_row(n), SC_ROWS)])

        start(0, idx0, rows0, sem0)

        @pl.loop(0, n_blocks, step=2)
        def _(n):
            start(n + 1, idx1, rows1, sem1)
            finish(n, idx0, rows0, sem0)

            @pl.when(n + 2 < n_blocks)
            def _():
                start(n + 2, idx0, rows0, sem0)

            finish(n + 1, idx1, rows1, sem1)

    return gather_rows(y_slots, slot_of)


def _combine_kernel(y_ref, h_ref, g2_ref, gtok_ref, rows_ref, wgs_ref, wus_ref, wds_ref, o_ref):
    h_left, h_right = _unpack_rows(h_ref[...])
    h = jnp.concatenate([h_left, h_right], axis=1).astype(BF16)
    gs = _dot(h, wgs_ref[...])
    us = _dot(h, wus_ref[...])
    shared = _dot(((gs * _sigmoid(gs)) * us).astype(BF16), wds_ref[...])

    gtok = gtok_ref[...]
    acc_left = acc_right = None
    for k in range(TOP_K):
        left, right = _unpack_rows(rows_ref[k])
        gate = gtok[:, k:k + 1]
        acc_left = gate * left if acc_left is None else acc_left + gate * left
        acc_right = gate * right if acc_right is None else acc_right + gate * right
    routed = jnp.concatenate([acc_left, acc_right], axis=1)
    o_ref[...] = y_ref[...] + g2_ref[0] * (routed + shared)


def _combine(y_all, hp_all, gate2, gtok, rows, wgs, wus, wds, *, first_token, tokens, seq):
    d = y_all.shape[1]
    width = hp_all.shape[1]
    tm = MOE_ROW_TM
    tile0 = first_token // tm
    nb = gate2.shape[0]
    tiles_per_batch = seq // tm

    def mod_map(i):
        return ((i // tiles_per_batch) if nb > 1 else 0, 0, 0)

    return pl.pallas_call(
        _combine_kernel,
        out_shape=jax.ShapeDtypeStruct((tokens, d), F32),
        grid=(tokens // tm,),
        in_specs=[pl.BlockSpec((tm, d), lambda i: (tile0 + i, 0)),
                  pl.BlockSpec((tm, width), lambda i: (tile0 + i, 0)),
                  pl.BlockSpec((1, 1, d), mod_map),
                  pl.BlockSpec((tm, LANES), lambda i: (tile0 + i, 0)),
                  pl.BlockSpec((TOP_K, tm, width), lambda i: (0, tile0 + i, 0)),
                  pl.BlockSpec((d, D_SHARED), lambda i: (0, 0)),
                  pl.BlockSpec((d, D_SHARED), lambda i: (0, 0)),
                  pl.BlockSpec((D_SHARED, d), lambda i: (0, 0))],
        out_specs=pl.BlockSpec((tm, d), lambda i: (i, 0)),
        compiler_params=_cparams(("arbitrary",)),
        name="moe_combine",
    )(y_all, hp_all, gate2, gtok, rows, wgs, wus, wds)


def _expert_layout(counts, n_tiles):
    cnt = counts.astype(jnp.int32)
    tiles = (cnt + (MOE_TS - 1)) // MOE_TS
    last_tile = jnp.cumsum(tiles)
    off = (last_tile - tiles) * MOE_TS
    pad_lo = off + cnt
    pad_hi = (off + tiles * MOE_TS).at[N_EXPERTS - 1].set(n_tiles * MOE_TS)
    pad_cnt = pad_hi - pad_lo
    pad_last = jnp.cumsum(pad_cnt)
    shift = pad_lo - (pad_last - pad_cnt)
    j = jnp.arange(N_EXPERTS * MOE_TS, dtype=jnp.int32)
    past = (pad_last[None, :-1] <= j[:, None]).astype(jnp.int32)
    pad_slots = j + shift[0] + jnp.sum(past * (shift[1:] - shift[:-1])[None, :], axis=1)
    tile_ids = jnp.arange(n_tiles, dtype=jnp.int32)
    tile_expert = jnp.minimum(
        jnp.sum((last_tile[None, :] <= tile_ids[:, None]).astype(jnp.int32), axis=1), N_EXPERTS - 1)
    return off, pad_slots, tile_expert


def _rope_tables(n_tokens):
    t = jnp.arange(n_tokens)
    row = (t // GRID_W).astype(F32)
    col = (t % GRID_W).astype(F32)
    nf = HEAD_DIM // 4
    freqs = ROPE_THETA ** (-jnp.arange(nf, dtype=F32) / nf)
    ang_r = row[:, None] * freqs
    ang_c = col[:, None] * freqs
    cos = jnp.concatenate([jnp.cos(ang_r)] * 2 + [jnp.cos(ang_c)] * 2, axis=1)
    sin = jnp.concatenate([-jnp.sin(ang_r), jnp.sin(ang_r), -jnp.sin(ang_c), jnp.sin(ang_c)], axis=1)
    reps = LANES // HEAD_DIM
    return jnp.tile(cos, (1, reps)), jnp.tile(sin, (1, reps))


def _head_gains(qn_a, kn_a, qn_b, kn_b):
    ones = jnp.ones((HEAD_DIM,), F32)
    parts = ([qn_a] * N_HEADS_A + [kn_a] * N_KV_A + [ones] * N_KV_A
             + [qn_b] * N_HEADS_B + [kn_b] * N_HEADS_B + [ones] * N_HEADS_B)
    return jnp.concatenate(parts).reshape(1, IN_COLS).astype(F32)


def _same_head_indicator():
    i = np.arange(MXU_DIM)
    return jnp.asarray((i[:, None] // HEAD_DIM) == (i[None, :] // HEAD_DIM), BF16)


def _token_major(cache):
    b, h, s, hd = cache.shape
    return cache.transpose(0, 2, 1, 3).reshape(b, s, h * hd).astype(BF16)


def kernel(x_prompt, x_sample, cache_k_a, cache_v_a, cache_k_b, cache_v_b, c, c_ctx, w_mod, b_mod, norm1, norm2, w_in, qn_a, kn_a, qn_b, kn_b, rpb, on_a, on_b, w_out, w_router, router_bias, w_gate_e, w_up_e, w_down_e, w_gate_s, w_up_s, w_down_s):
    depth = w_mod.shape[0]
    assert depth == 1
    l = 0
    bp, sp, d = x_prompt.shape
    bs, ss, _ = x_sample.shape

    cvec = jnp.concatenate([c_ctx[None, :], c], axis=0)
    rows = -(-cvec.shape[0] // 8) * 8
    cvec = jnp.pad(cvec, ((0, rows - cvec.shape[0]), (0, 0)))
    mod = _adaln(cvec, w_mod[l], b_mod[l])
    mod_p = [m.reshape(1, 1, d) for m in jnp.split(mod[0:1], 6, axis=-1)]
    mod_s = [m.reshape(bs, 1, d) for m in jnp.split(mod[1:1 + bs], 6, axis=-1)]
    mod_all = [m.reshape(1 + bs, 1, d) for m in jnp.split(mod[0:1 + bs], 6, axis=-1)]

    w_in_bf = w_in[l].astype(BF16)
    w_out_bf = w_out[l].astype(BF16)
    gain = _head_gains(qn_a[l], kn_a[l], qn_b[l], kn_b[l])
    seg = _same_head_indicator()
    n1 = norm1[l].reshape(1, d)
    n2 = norm2[l].reshape(1, d)
    ona = on_a[l].reshape(1, WIDTH_A)
    onb = on_b[l].reshape(1, WIDTH_B)
    wr_t = w_router[l].T
    wr_hi = wr_t.astype(BF16)
    wr_lo = (wr_t - wr_hi.astype(F32)).astype(BF16)
    rbias = router_bias[l].reshape(N_EXPERTS, 1).astype(F32)
    wgs = w_gate_s[l].astype(BF16)
    wus = w_up_s[l].astype(BF16)
    wds = w_down_s[l].astype(BF16)
    t_p = bp * sp
    t_s = bs * ss
    t_all = t_p + t_s

    xp = x_prompt.reshape(t_p, d)
    proj_p, st_ka, st_va, st_kb, st_vb = _project(
        xp, mod_p[0], mod_p[1], n1, w_in_bf, gain, seg, None, tm=sp, seq=sp, states=True)
    oa_p, ob_p = _context_attention(proj_p, seq=sp)

    xs = x_sample.reshape(t_s, d)
    proj_s, = _project(xs, mod_s[0], mod_s[1], n1, w_in_bf, gain, seg, _rope_tables(ss),
                       tm=1024, seq=ss, states=False)
    oa_s = _latent_gqa(proj_s, _token_major(cache_k_a[:, l]), _token_major(cache_v_a[:, l]),
                       _query_norm_bound(qn_a[l]), seq=ss, tq=256)
    bias_t = _neighbourhood_bias(rpb[l], ss // GRID_W)
    ob_s = _latent_neighbourhood(proj_s, _token_major(cache_k_b[:, l]), _token_major(cache_v_b[:, l]),
                                 bias_t, _neighbourhood_bounds(qn_b[l], kn_b[l], rpb[l]), seq=ss)

    y1_all, hp_all, gates_t, rank_t, counts = _merge(
        (xp, oa_p, ob_p), (xs, oa_s, ob_s), ona, onb, w_out_bf, mod_all[2], mod_all[3], mod_all[4], n2,
        wr_hi, wr_lo, rbias, tm=512, lat_seq=ss)
    n_tiles = t_all * TOP_K // MOE_TS + N_EXPERTS
    off, pad_slots, tile_expert = _expert_layout(counts[:, 0], n_tiles)
    pos, gtok = _slots(gates_t, rank_t, off.astype(F32).reshape(N_EXPERTS, 1))
    slot_of = pos.reshape(TOP_K * t_all)
    x_slots = _dispatch(hp_all, jnp.concatenate([slot_of, pad_slots]))
    y_slots = _experts(x_slots, tile_expert, w_gate_e[l], w_up_e[l], w_down_e[l])
    rows = _gather_slots(y_slots, slot_of, t_all).reshape(TOP_K, t_all, d // 2)
    y_p = _combine(y1_all, hp_all, mod_p[5], gtok, rows, wgs, wus, wds,
                   first_token=0, tokens=t_p, seq=sp)
    y_s = _combine(y1_all, hp_all, mod_s[5], gtok, rows, wgs, wus, wds,
                   first_token=t_p, tokens=t_s, seq=ss)

    return (y_p.reshape(bp, sp, d), y_s.reshape(bs, ss, d), st_ka, st_va, st_kb, st_vb)
```

```python
import functools

import numpy as np
import jax
import jax.numpy as jnp
from jax import lax
from jax.experimental import pallas as pl
from jax.experimental.pallas import tpu as pltpu
from jax.experimental.pallas import tpu_sc as plsc

F32 = jnp.float32
BF16 = jnp.bfloat16

D_MODEL = 1024
HEAD_DIM = 64
N_HEADS_A = 8
N_KV_A = 2
GROUP_A = N_HEADS_A // N_KV_A
N_HEADS_B = 8
WIDTH_A = N_HEADS_A * HEAD_DIM
WIDTH_B = N_HEADS_B * HEAD_DIM
KV_WIDTH_A = N_KV_A * HEAD_DIM
IN_COLS = WIDTH_A + 2 * KV_WIDTH_A + 3 * WIDTH_B
GRID_W = 64
ROPE_THETA = 10000.0
NA_KH = 8
NA_KW = 16
N_EXPERTS = 64
N_GROUPS = 8
GROUP_SIZE = N_EXPERTS // N_GROUPS
TOPK_GROUPS = 4
TOP_K = 8
D_EXPERT = 256
D_SHARED = 256
ROUTED_SCALE = 2.5
EPS = 1e-6

LANES = 128
MXU_DIM = 256
MASKED = -1e30

COL_QA = 0
COL_KA = WIDTH_A
COL_VA = COL_KA + KV_WIDTH_A
COL_QB = COL_VA + KV_WIDTH_A
COL_KB = COL_QB + WIDTH_B
COL_VB = COL_KB + WIDTH_B

NA_QROWS = 8
NA_KROWS = 2 * NA_KH
NA_TQ = NA_QROWS * GRID_W
NA_TK = NA_KROWS * GRID_W
NA_KBLK = 256

VMEM_LIMIT = 56 * 1024 * 1024


def _cparams(sem):
    return pltpu.CompilerParams(dimension_semantics=sem, vmem_limit_bytes=VMEM_LIMIT)


def _dot(a, b):
    return jnp.dot(a, b, preferred_element_type=F32)


def _dot_nt(a, b):
    return lax.dot_general(a, b, (((1,), (1,)), ((), ())), preferred_element_type=F32)


def _sigmoid(x):
    return 1.0 / (1.0 + jnp.exp(-x))


def _rms(x):
    return x * lax.rsqrt(jnp.mean(x * x, axis=-1, keepdims=True) + EPS)


def _pack_rows(x):
    n = x.shape[1] // 2
    hi = lax.bitcast_convert_type(x[:, :n].astype(BF16).astype(F32), jnp.int32)
    lo = lax.bitcast_convert_type(x[:, n:].astype(BF16).astype(F32), jnp.int32)
    return hi | lax.shift_right_logical(lo, 16)


def _unpack_rows(w):
    left = lax.bitcast_convert_type(w & jnp.int32(-65536), F32)
    right = lax.bitcast_convert_type(lax.shift_left(w, 16), F32)
    return left, right


def _mod_kernel(c_ref, w_ref, b_ref, o_ref):
    c = c_ref[...]
    s = c * _sigmoid(c)
    o_ref[...] = jnp.dot(s, w_ref[...], preferred_element_type=F32,
                         precision=lax.Precision.HIGHEST) + b_ref[...]


def _adaln(cvec, w_mod, b_mod):
    rows, d = cvec.shape
    n = w_mod.shape[1]
    tn = 512
    return pl.pallas_call(
        _mod_kernel,
        out_shape=jax.ShapeDtypeStruct((rows, n), F32),
        grid=(n // tn,),
        in_specs=[pl.BlockSpec((rows, d), lambda j: (0, 0)),
                  pl.BlockSpec((d, tn), lambda j: (0, j)),
                  pl.BlockSpec((1, tn), lambda j: (0, j))],
        out_specs=pl.BlockSpec((rows, tn), lambda j: (0, j)),
        compiler_params=_cparams(("arbitrary",)),
        name="adaln_mod",
    )(cvec, w_mod, b_mod.reshape(1, n))


def _proj_chunks():
    def split(c0, width, step, *flags):
        return [(c0 + i, min(step, width - i)) + flags for i in range(0, width, step)]
    return (split(COL_QA, WIDTH_A, MXU_DIM, True, True, True)
            + split(COL_KA, KV_WIDTH_A, MXU_DIM, True, True, False)
            + split(COL_VA, KV_WIDTH_A, MXU_DIM, False, False, False)
            + split(COL_QB, WIDTH_B, MXU_DIM, True, False, True)
            + split(COL_KB, WIDTH_B, MXU_DIM, True, False, False)
            + split(COL_VB, WIDTH_B, MXU_DIM, False, False, False))


_PROJ_CHUNKS = _proj_chunks()


def _proj_kernel(*refs, rope, states):
    x_ref, sh_ref, sc_ref, n1_ref, w_ref, gain_ref, seg_ref = refs[:7]
    pos = 7
    if rope:
        cos_ref, sin_ref = refs[pos:pos + 2]
        pos += 2
    out_ref = refs[pos]
    pos += 1
    if states:
        ka_ref, va_ref, kb_ref, vb_ref = refs[pos:pos + 4]
        state_of = {COL_KA: ka_ref, COL_VA: va_ref, COL_KB: kb_ref, COL_VB: vb_ref}

    x = x_ref[...]
    h = _rms(x) * n1_ref[...]
    h = h * (1.0 + sc_ref[0]) + sh_ref[0]
    p = _dot(h.astype(BF16), w_ref[...])

    for c0, w, normed, roped, is_query in _PROJ_CHUNKS:
        pc = p[:, c0:c0 + w]
        if normed:
            seg = seg_ref[0:w, 0:w]
            sq = pc * pc
            hi = sq.astype(BF16)
            lo = (sq - hi.astype(F32)).astype(BF16)
            ss = _dot(hi, seg) + _dot(lo, seg)
            pc = pc * lax.rsqrt(ss * (1.0 / HEAD_DIM) + EPS) * gain_ref[:, c0:c0 + w]
        if states:
            for start, ref in state_of.items():
                if start <= c0 < start + ref.shape[2] * HEAD_DIM:
                    base = (c0 - start) // HEAD_DIM
                    for hh in range(w // HEAD_DIM):
                        ref[0, 0, base + hh] = pc[:, hh * HEAD_DIM:(hh + 1) * HEAD_DIM]
        if rope and roped:
            reps = w // LANES
            cos = jnp.concatenate([cos_ref[...]] * reps, axis=1) if reps > 1 else cos_ref[...]
            sin = jnp.concatenate([sin_ref[...]] * reps, axis=1) if reps > 1 else sin_ref[...]
            lane = lax.broadcasted_iota(jnp.int32, pc.shape, 1)
            first_half = (lane % (HEAD_DIM // 2)) < (HEAD_DIM // 4)
            partner = jnp.where(first_half,
                                pltpu.roll(pc, w - HEAD_DIM // 4, 1),
                                pltpu.roll(pc, HEAD_DIM // 4, 1))
            pc = pc * cos + partner * sin
        if is_query:
            pc = pc * (HEAD_DIM ** -0.5)
        out_ref[:, c0:c0 + w] = pc.astype(BF16)


def _project(x2d, shift, scale, norm1, w_in_bf, gain, seg, rope_tabs, *, tm, seq, states):
    t, d = x2d.shape
    nb = shift.shape[0]
    tiles_per_batch = seq // tm
    rope = rope_tabs is not None

    def mod_map(i):
        return ((i // tiles_per_batch) if nb > 1 else 0, 0, 0)

    in_specs = [pl.BlockSpec((tm, d), lambda i: (i, 0)),
                pl.BlockSpec((1, 1, d), mod_map),
                pl.BlockSpec((1, 1, d), mod_map),
                pl.BlockSpec((1, d), lambda i: (0, 0)),
                pl.BlockSpec((d, IN_COLS), lambda i: (0, 0)),
                pl.BlockSpec((1, IN_COLS), lambda i: (0, 0)),
                pl.BlockSpec((MXU_DIM, MXU_DIM), lambda i: (0, 0))]
    args = [x2d, shift, scale, norm1, w_in_bf, gain, seg]
    if rope:
        in_specs += [pl.BlockSpec((tm, LANES), lambda i: (i % tiles_per_batch, 0))] * 2
        args += list(rope_tabs)
    out_shape = [jax.ShapeDtypeStruct((t, IN_COLS), BF16)]
    out_specs = [pl.BlockSpec((tm, IN_COLS), lambda i: (i, 0))]
    if states:
        assert tm == seq
        b = t // seq
        for nh in (N_KV_A, N_KV_A, N_HEADS_B, N_HEADS_B):
            out_shape.append(jax.ShapeDtypeStruct((b, 1, nh, seq, HEAD_DIM), F32))
            out_specs.append(pl.BlockSpec((1, 1, nh, seq, HEAD_DIM), lambda i: (i, 0, 0, 0, 0)))
    return pl.pallas_call(
        functools.partial(_proj_kernel, rope=rope, states=states),
        out_shape=out_shape,
        grid=(t // tm,),
        in_specs=in_specs,
        out_specs=out_specs,
        compiler_params=_cparams(("arbitrary",)),
        name="proj_states" if states else "proj_rope",
    )(*args)


def _lane_half(shape):
    return lax.broadcasted_iota(jnp.int32, shape, 1) // HEAD_DIM


def _keep_half(x, half):
    return jnp.where(_lane_half(x.shape) == half, x, jnp.zeros_like(x))


def _transpose_bf16(x):
    return x.astype(F32).T.astype(BF16)


def _attend(q, keys, values_t, biases):
    return _softmax_av(_scores(q, keys, biases), values_t)


def _scores(q, keys, biases):
    scores = []
    for k, b in zip(keys, biases):
        s = _dot_nt(k, q)
        if b is not None:
            s = s + b
        scores.append(s)
    return scores


def _softmax_av(scores, values_t):
    m = functools.reduce(jnp.maximum, [jnp.max(s, axis=0, keepdims=True) for s in scores])
    denom = None
    out = None
    for s, vt in zip(scores, values_t):
        p = jnp.exp(s - m)
        ps = jnp.sum(p, axis=0, keepdims=True)
        po = _dot(vt, p.astype(BF16))
        denom = ps if denom is None else denom + ps
        out = po if out is None else out + po
    return out / denom


def _swap_halves(q_bf16):
    return pltpu.roll(q_bf16.astype(F32), HEAD_DIM, 1).astype(BF16)


def _gqa_heads(q_of_pair, keys_by_group, values_t):
    outs = []
    for h in range(N_HEADS_A):
        g = h // GROUP_A
        q = q_of_pair(h // 2)
        if h % 2 != g:
            q = _swap_halves(q)
        o = _attend(q, keys_by_group[g], values_t, [None] * len(values_t))
        outs.append(o[g * HEAD_DIM:(g + 1) * HEAD_DIM])
    return jnp.concatenate(outs, axis=0)


def _ctx_attn_kernel(p_ref, oa_ref, ob_ref):
    ka = p_ref[:, COL_KA:COL_KA + LANES]
    va_t = [_transpose_bf16(p_ref[:, COL_VA:COL_VA + LANES])]
    keys_by_group = [[_keep_half(ka, g)] for g in range(N_KV_A)]
    oa = _gqa_heads(lambda i: p_ref[:, COL_QA + i * LANES:COL_QA + (i + 1) * LANES],
                    keys_by_group, va_t)
    oa_ref[...] = oa.T

    outs = []
    for i in range(N_HEADS_B // 2):
        q = p_ref[:, COL_QB + i * LANES:COL_QB + (i + 1) * LANES]
        k = p_ref[:, COL_KB + i * LANES:COL_KB + (i + 1) * LANES]
        vt = [_transpose_bf16(p_ref[:, COL_VB + i * LANES:COL_VB + (i + 1) * LANES])]
        for half in range(2):
            o = _attend(q, [_keep_half(k, half)], vt, [None])
            outs.append(o[half * HEAD_DIM:(half + 1) * HEAD_DIM])
    ob_ref[...] = jnp.concatenate(outs, axis=0).T


def _context_attention(proj, *, seq):
    t = proj.shape[0]
    return pl.pallas_call(
        _ctx_attn_kernel,
        out_shape=[jax.ShapeDtypeStruct((t, WIDTH_A), F32), jax.ShapeDtypeStruct((t, WIDTH_B), F32)],
        grid=(t // seq,),
        in_specs=[pl.BlockSpec((seq, IN_COLS), lambda i: (i, 0))],
        out_specs=[pl.BlockSpec((seq, WIDTH_A), lambda i: (i, 0)),
                   pl.BlockSpec((seq, WIDTH_B), lambda i: (i, 0))],
        compiler_params=_cparams(("arbitrary",)),
        name="context_attention",
    )(proj)


ATTN_SAFE_SHIFT = 40.0
ONES_ROWS = 16


def _round_up_bf16(x):
    return (x * (1.0 + 2.0 ** -6)).astype(BF16).astype(F32)


def _query_norm_bound(gain):
    return jnp.max(jnp.abs(gain)).reshape(1, 1).astype(F32)


def _ones_lane(g):
    return (1 - g) * HEAD_DIM


def _gqa_latent_kernel(q_ref, k_ref, v_ref, ck_ref, cv_ref, qmax_ref, o_ref,
                       kg_ref, ckg_ref, vt_ref, cvt_ref, shift_ref):
    lane_k = lax.broadcasted_iota(jnp.int32, (1, LANES), 1)

    @pl.when(pl.program_id(1) == 0)
    def _():
        k = k_ref[...]
        ck = ck_ref[0]
        vt = v_ref[...].astype(F32).T
        cvt = cv_ref[0].astype(F32).T
        for g in range(N_KV_A):
            kf = _keep_half(k, g).astype(F32)
            ckf = _keep_half(ck, g).astype(F32)
            ksq = jnp.maximum(jnp.max(jnp.sum(kf * kf, axis=1, keepdims=True), axis=0, keepdims=True),
                              jnp.max(jnp.sum(ckf * ckf, axis=1, keepdims=True), axis=0, keepdims=True))
            shift_ref[g] = jnp.broadcast_to(_round_up_bf16(qmax_ref[...] * jnp.sqrt(ksq)), shift_ref.shape[1:])
            kg_ref[g] = jnp.where(lane_k == _ones_lane(g), 1.0, kf).astype(BF16)
            ckg_ref[g] = jnp.where(lane_k == _ones_lane(g), 1.0, ckf).astype(BF16)
            rows = slice(g * HEAD_DIM, (g + 1) * HEAD_DIM)
            vt_ref[g] = jnp.concatenate([vt[rows], jnp.ones((ONES_ROWS, vt.shape[1]), F32)], axis=0).astype(BF16)
            cvt_ref[g] = jnp.concatenate([cvt[rows], jnp.ones((ONES_ROWS, cvt.shape[1]), F32)], axis=0).astype(BF16)

    tq = q_ref.shape[0]
    lane_q = lax.broadcasted_iota(jnp.int32, (GROUP_A * tq, LANES), 1)
    queries, shifts = [], []
    for g in range(N_KV_A):
        qs = []
        for j in range(GROUP_A):
            h = g * GROUP_A + j
            q = q_ref[:, (h // 2) * LANES:(h // 2 + 1) * LANES].astype(F32)
            qs.append(q if h % 2 == g else pltpu.roll(q, HEAD_DIM, 1))
        queries.append(jnp.where(lane_q // HEAD_DIM == g, jnp.concatenate(qs, axis=0), 0.0))
        shifts.append(shift_ref[g][0:1, 0:1])
    safe = jnp.max(jnp.maximum(shift_ref[0], shift_ref[1])) <= ATTN_SAFE_SHIFT

    def attend(g, p_lat, p_ctx):
        half = p_lat.shape[1] // 2
        o = jnp.concatenate([_dot(vt_ref[g], p_lat[:, :half]) + _dot(cvt_ref[g], p_ctx[:, :half]),
                             _dot(vt_ref[g], p_lat[:, half:]) + _dot(cvt_ref[g], p_ctx[:, half:])],
                            axis=1)
        o = o[:HEAD_DIM] / o[HEAD_DIM:HEAD_DIM + 1]
        heads = jnp.concatenate([o[:, j * tq:(j + 1) * tq] for j in range(GROUP_A)], axis=0)
        o_ref[:, g * GROUP_A * HEAD_DIM:(g + 1) * GROUP_A * HEAD_DIM] = heads.T

    def with_bound():
        for g in range(N_KV_A):
            qa = jnp.where(lane_q == _ones_lane(g), -shifts[g], queries[g]).astype(BF16)
            attend(g, jnp.exp(_dot_nt(kg_ref[g], qa)).astype(BF16), jnp.exp(_dot_nt(ckg_ref[g], qa)).astype(BF16))

    def with_row_max():
        for g in range(N_KV_A):
            qa = queries[g].astype(BF16)
            s_lat = _dot_nt(kg_ref[g], qa)
            s_ctx = _dot_nt(ckg_ref[g], qa)
            m = jnp.maximum(jnp.max(s_lat, axis=0, keepdims=True), jnp.max(s_ctx, axis=0, keepdims=True))
            attend(g, jnp.exp(s_lat - m).astype(BF16), jnp.exp(s_ctx - m).astype(BF16))

    pl.when(safe)(with_bound)
    pl.when(jnp.logical_not(safe))(with_row_max)


def _latent_gqa(proj, ctx_k, ctx_v, qmax, *, seq, tq):
    t = proj.shape[0]
    b = t // seq
    nq = seq // tq
    past = ctx_k.shape[1]
    return pl.pallas_call(
        _gqa_latent_kernel,
        out_shape=jax.ShapeDtypeStruct((t, WIDTH_A), F32),
        grid=(b, nq),
        in_specs=[pl.BlockSpec((tq, WIDTH_A), lambda bi, qi: (bi * nq + qi, 0)),
                  pl.BlockSpec((seq, LANES), lambda bi, qi: (bi, COL_KA // LANES)),
                  pl.BlockSpec((seq, LANES), lambda bi, qi: (bi, COL_VA // LANES)),
                  pl.BlockSpec((1, past, LANES), lambda bi, qi: (bi, 0, 0)),
                  pl.BlockSpec((1, past, LANES), lambda bi, qi: (bi, 0, 0)),
                  pl.BlockSpec((1, 1), lambda bi, qi: (0, 0))],
        out_specs=pl.BlockSpec((tq, WIDTH_A), lambda bi, qi: (bi * nq + qi, 0)),
        scratch_shapes=[pltpu.VMEM((N_KV_A, seq, LANES), BF16),
                        pltpu.VMEM((N_KV_A, past, LANES), BF16),
                        pltpu.VMEM((N_KV_A, HEAD_DIM + ONES_ROWS, seq), BF16),
                        pltpu.VMEM((N_KV_A, HEAD_DIM + ONES_ROWS, past), BF16),
                        pltpu.VMEM((N_KV_A, 8, LANES), F32)],
        compiler_params=_cparams(("arbitrary", "arbitrary")),
        name="latent_gqa",
    )(proj, proj, proj, ctx_k, ctx_v, qmax)


def _na_kernel(q_ref, k_ref, v_ref, ck_ref, cv_ref, bias_ref, bound_ref, o_ref, keys_ref, vt_ref, shift_ref,
               *, rows):
    i = pl.program_id(2)
    n_kblk = k_ref.shape[0] // NA_KBLK
    lane_k = lax.broadcasted_iota(jnp.int32, (1, LANES), 1)
    one = jnp.ones((), BF16)

    @pl.when(i == 0)
    def _():
        k = k_ref[...]
        ck = ck_ref[0]
        for half in range(2):
            kh = jnp.where(lane_k == _ones_lane(half), one, _keep_half(k, half))
            keys_ref[half, 0:n_kblk] = kh.reshape(n_kblk, NA_KBLK, LANES)
            ckh = _keep_half(ck, half)
            keys_ref[half, n_kblk] = jnp.where(lane_k == _ones_lane(half), one, ckh)
            ckf = ckh.astype(F32)
            ctx_norm = jnp.sqrt(jnp.max(jnp.sum(ckf * ckf, axis=1, keepdims=True), axis=0, keepdims=True))
            consts = bound_ref[0, half:half + 1, :]
            kmax = jnp.maximum(ctx_norm, consts[:, 2:3])
            shift_ref[half] = jnp.broadcast_to(_round_up_bf16(consts[:, 0:1] * kmax + consts[:, 1:2]),
                                               shift_ref.shape[1:])
        ones_rows = jnp.ones((ONES_ROWS, NA_KBLK), F32)
        vt = v_ref[...].astype(F32).T
        for j in range(n_kblk):
            vt_ref[j] = jnp.concatenate([vt[:, j * NA_KBLK:(j + 1) * NA_KBLK], ones_rows], axis=0).astype(BF16)
        vt_ref[n_kblk] = jnp.concatenate([cv_ref[0].astype(F32).T, ones_rows], axis=0).astype(BF16)

    q = q_ref[...]
    lane_q = lax.broadcasted_iota(jnp.int32, q.shape, 1)
    first = _na_first_key_block(i, rows)
    n_qblk = rows // NA_QROWS
    variant = jnp.where(i == 0, 0, jnp.where(i == n_qblk - 1, 2, 1))
    blocks = [first + j for j in range(NA_TK // NA_KBLK)] + [n_kblk]
    values_t = [vt_ref[blk] for blk in blocks]
    heads = []
    for half in range(2):
        keys = [keys_ref[half, blk] for blk in blocks]
        biases = [bias_ref[variant, half, j * NA_KBLK:(j + 1) * NA_KBLK, :] for j in range(NA_TK // NA_KBLK)] + [None]
        heads.append((shift_ref[half][0:1, 0:1], keys, biases))
    safe = jnp.max(jnp.maximum(shift_ref[0], shift_ref[1])) <= ATTN_SAFE_SHIFT

    def attend(probabilities):
        outs = []
        for half, ps in enumerate(probabilities):
            o = functools.reduce(lambda a, b: a + b, [_dot(vt, p) for vt, p in zip(values_t, ps)])
            outs.append(o[half * HEAD_DIM:(half + 1) * HEAD_DIM] / o[2 * HEAD_DIM:2 * HEAD_DIM + 1])
        o_ref[...] = jnp.concatenate(outs, axis=0).T

    def with_bound():
        probabilities = []
        for half, (shift, keys, biases) in enumerate(heads):
            qa = jnp.where(lane_q == _ones_lane(half), (-shift).astype(BF16), _keep_half(q, half))
            probabilities.append([jnp.exp(s).astype(BF16) for s in _scores(qa, keys, biases)])
        attend(probabilities)

    def with_row_max():
        all_scores = [_scores(_keep_half(q, half), keys, biases) for half, (_, keys, biases) in enumerate(heads)]
        probabilities = []
        for scores in all_scores:
            m = functools.reduce(jnp.maximum, [jnp.max(s, axis=0, keepdims=True) for s in scores])
            probabilities.append([jnp.exp(s - m).astype(BF16) for s in scores])
        attend(probabilities)

    pl.when(safe)(with_bound)
    pl.when(jnp.logical_not(safe))(with_row_max)


def _na_first_key_block(i, rows):
    per_qblock = NA_QROWS * GRID_W // NA_KBLK
    lead = (NA_KH // 2) * GRID_W // NA_KBLK
    return jnp.clip(per_qblock * i - lead, 0, (rows - NA_KROWS) * GRID_W // NA_KBLK)


def _latent_neighbourhood(proj, ctx_k, ctx_v, bias_t, bounds, *, seq):
    t = proj.shape[0]
    b = t // seq
    rows = seq // GRID_W
    nblk = rows // NA_QROWS
    n_kblk = seq // NA_KBLK
    past = ctx_k.shape[1]
    assert past == NA_KBLK
    grid = (N_HEADS_B // 2, b, nblk)
    in_specs = [pl.BlockSpec((NA_TQ, LANES), lambda hp, bi, i: (bi * nblk + i, COL_QB // LANES + hp)),
                pl.BlockSpec((seq, LANES), lambda hp, bi, i: (bi, COL_KB // LANES + hp)),
                pl.BlockSpec((seq, LANES), lambda hp, bi, i: (bi, COL_VB // LANES + hp)),
                pl.BlockSpec((1, past, LANES), lambda hp, bi, i: (bi, 0, hp)),
                pl.BlockSpec((1, past, LANES), lambda hp, bi, i: (bi, 0, hp)),
                pl.BlockSpec((3, 2, NA_TK, NA_TQ), lambda hp, bi, i: (0, hp, 0, 0)),
                pl.BlockSpec((1, 2, LANES), lambda hp, bi, i: (hp, 0, 0))]
    return pl.pallas_call(
        functools.partial(_na_kernel, rows=rows),
        out_shape=jax.ShapeDtypeStruct((t, WIDTH_B), F32),
        grid=grid,
        in_specs=in_specs,
        out_specs=pl.BlockSpec((NA_TQ, LANES), lambda hp, bi, i: (bi * nblk + i, hp)),
        scratch_shapes=[pltpu.VMEM((2, n_kblk + 1, NA_KBLK, LANES), BF16),
                        pltpu.VMEM((n_kblk + 1, 2 * HEAD_DIM + ONES_ROWS, NA_KBLK), BF16),
                        pltpu.VMEM((2, 8, LANES), F32)],
        compiler_params=_cparams(("arbitrary", "arbitrary", "arbitrary")),
        name="latent_neighbourhood",
    )(proj, proj, proj, ctx_k, ctx_v, bias_t, bounds)


def _neighbourhood_bounds(qn_b, kn_b, rpb):
    n_heads = rpb.shape[0]
    qmax = jnp.broadcast_to(_query_norm_bound(qn_b), (n_heads, 1))
    kmax = jnp.broadcast_to(_query_norm_bound(kn_b) * (HEAD_DIM ** 0.5), (n_heads, 1))
    bmax = jnp.maximum(jnp.max(rpb.reshape(n_heads, -1), axis=1, keepdims=True), 0.0).astype(F32)
    table = jnp.concatenate([qmax, bmax, kmax, jnp.zeros((n_heads, LANES - 3), F32)], axis=1)
    return table.reshape(n_heads // 2, 2, LANES)


def _neighbourhood_bias(rpb, rows):
    nblk = rows // NA_QROWS
    n_dr = 2 * NA_KH - 1
    n_dc = 2 * NA_KW - 1
    kc = np.arange(GRID_W)[:, None]
    qc = np.arange(GRID_W)[None, :]
    ws = np.clip(qc - NA_KW // 2, 0, GRID_W - NA_KW)
    col_ok = (kc >= ws) & (kc < ws + NA_KW)
    dc = np.clip(kc - qc + NA_KW - 1, 0, n_dc - 1)
    dc_onehot = (dc[None] == np.arange(n_dc)[:, None, None]).astype(np.float32)
    tiles = jnp.einsum('hab,bkq->hakq', rpb.astype(F32), jnp.asarray(dc_onehot),
                       precision=lax.Precision.HIGHEST)
    tiles = jnp.where(jnp.asarray(col_ok)[None, None], tiles, MASKED)
    masked_tile = jnp.full((rpb.shape[0], 1, GRID_W, GRID_W), MASKED, F32)
    tiles = jnp.concatenate([tiles, masked_tile], axis=1)
    pick = np.zeros((3, NA_KROWS, NA_QROWS, n_dr + 1), np.float32)
    for v, i in enumerate((0, 1, nblk - 1)):
        r0 = i * NA_QROWS
        ks = int(np.clip(r0 - NA_KH // 2, 0, rows - NA_KROWS))
        for kl in range(NA_KROWS):
            for ql in range(NA_QROWS):
                kr, qr = ks + kl, r0 + ql
                rs = int(np.clip(qr - NA_KH // 2, 0, rows - NA_KH))
                ok = rs <= kr < rs + NA_KH
                pick[v, kl, ql, (kr - qr + NA_KH - 1) if ok else n_dr] = 1.0
    bias = jnp.einsum('vkqa,hacd->vhkcqd', jnp.asarray(pick), tiles, precision=lax.Precision.HIGHEST)
    return bias.reshape(3, rpb.shape[0], NA_TK, NA_TQ)


def _merge_kernel(x_ref, oa_ref, ob_ref, ona_ref, onb_ref, wo_ref, g1_ref, sh2_ref, sc2_ref, n2_ref,
                  wrh_ref, wrl_ref, rb_ref, tri_ref, y_ref, hp_ref, gates_ref, rank_ref, count_ref):
    @pl.when(pl.program_id(0) == 0)
    def _():
        count_ref[...] = jnp.zeros_like(count_ref)

    na = (_rms(oa_ref[...]) * ona_ref[...]).astype(BF16)
    nb = (_rms(ob_ref[...]) * onb_ref[...]).astype(BF16)
    mix = _dot(na, wo_ref[0:WIDTH_A, :]) + _dot(nb, wo_ref[WIDTH_A:WIDTH_A + WIDTH_B, :])
    y = x_ref[...] + g1_ref[0] * mix
    y_ref[...] = y
    h = _rms(y) * n2_ref[...]
    h = h * (1.0 + sc2_ref[0]) + sh2_ref[0]
    hp_ref[...] = _pack_rows(h)
    gates, chosen = _router_gates(h, wrh_ref[...], wrl_ref[...], rb_ref[...])
    gates_ref[...] = gates
    before = _dot(chosen.astype(BF16), tri_ref[...])
    seen = count_ref[...]
    rank_ref[...] = jnp.where(chosen > 0.0, before + seen[:, 0:1], -1.0)
    count_ref[...] = seen + jnp.sum(chosen, axis=1, keepdims=True)


def _merge(x2d, oa, ob, on_a, on_b, w_out_bf, gate1, shift2, scale2, norm2, wr_hi, wr_lo, rbias, *, tm, seq):
    t, d = x2d.shape
    nb = gate1.shape[0]
    tiles_per_batch = seq // tm
    tri = jnp.asarray(np.triu(np.ones((tm, tm), np.float32), k=1), BF16)

    def mod_map(i):
        return ((i // tiles_per_batch) if nb > 1 else 0, 0, 0)

    return pl.pallas_call(
        _merge_kernel,
        out_shape=[jax.ShapeDtypeStruct((t, d), F32),
                   jax.ShapeDtypeStruct((t, d // 2), jnp.int32),
                   jax.ShapeDtypeStruct((N_EXPERTS, t), F32),
                   jax.ShapeDtypeStruct((N_EXPERTS, t), F32),
                   jax.ShapeDtypeStruct((N_EXPERTS, LANES), F32)],
        grid=(t // tm,),
        in_specs=[pl.BlockSpec((tm, d), lambda i: (i, 0)),
                  pl.BlockSpec((tm, WIDTH_A), lambda i: (i, 0)),
                  pl.BlockSpec((tm, WIDTH_B), lambda i: (i, 0)),
                  pl.BlockSpec((1, WIDTH_A), lambda i: (0, 0)),
                  pl.BlockSpec((1, WIDTH_B), lambda i: (0, 0)),
                  pl.BlockSpec((WIDTH_A + WIDTH_B, d), lambda i: (0, 0)),
                  pl.BlockSpec((1, 1, d), mod_map),
                  pl.BlockSpec((1, 1, d), mod_map),
                  pl.BlockSpec((1, 1, d), mod_map),
                  pl.BlockSpec((1, d), lambda i: (0, 0)),
                  pl.BlockSpec((N_EXPERTS, d), lambda i: (0, 0)),
                  pl.BlockSpec((N_EXPERTS, d), lambda i: (0, 0)),
                  pl.BlockSpec((N_EXPERTS, 1), lambda i: (0, 0)),
                  pl.BlockSpec((tm, tm), lambda i: (0, 0))],
        out_specs=[pl.BlockSpec((tm, d), lambda i: (i, 0)),
                   pl.BlockSpec((tm, d // 2), lambda i: (i, 0)),
                   pl.BlockSpec((N_EXPERTS, tm), lambda i: (0, i)),
                   pl.BlockSpec((N_EXPERTS, tm), lambda i: (0, i)),
                   pl.BlockSpec((N_EXPERTS, LANES), lambda i: (0, 0))],
        compiler_params=_cparams(("arbitrary",)),
        name="merge_route",
    )(x2d, oa, ob, on_a, on_b, w_out_bf, gate1, shift2, scale2, norm2, wr_hi, wr_lo, rbias, tri)


def _first_index_of_max(x, iota):
    mx = jnp.max(x, axis=0, keepdims=True)
    idx = jnp.min(jnp.where(x == mx, iota, float(x.shape[0])), axis=0, keepdims=True)
    return mx, iota == idx


def _router_gates(h, wr_hi, wr_lo, rbias):
    h_hi = h.astype(BF16)
    h_lo = (h - h_hi.astype(F32)).astype(BF16)
    logits = _dot_nt(wr_hi, h_hi) + (_dot_nt(wr_lo, h_hi) + _dot_nt(wr_hi, h_lo))
    scores = _sigmoid(logits)
    sel = scores + rbias
    tm = sel.shape[1]
    iota_g = lax.broadcasted_iota(jnp.int32, (GROUP_SIZE, tm), 0).astype(F32)
    group_scores = []
    for g in range(N_GROUPS):
        grp = sel[g * GROUP_SIZE:(g + 1) * GROUP_SIZE]
        m1, first = _first_index_of_max(grp, iota_g)
        m2 = jnp.max(jnp.where(first, -jnp.inf, grp), axis=0, keepdims=True)
        group_scores.append(m1 + m2)
    gs = jnp.concatenate(group_scores, axis=0)
    iota_n = lax.broadcasted_iota(jnp.int32, (N_GROUPS, tm), 0).astype(F32)
    group_on = jnp.zeros((N_GROUPS, tm), F32)
    for _ in range(TOPK_GROUPS):
        _, pick = _first_index_of_max(gs, iota_n)
        group_on = jnp.where(pick, 1.0, group_on)
        gs = jnp.where(pick, -jnp.inf, gs)
    expert_on = jnp.concatenate(
        [jnp.broadcast_to(group_on[g:g + 1], (GROUP_SIZE, tm)) for g in range(N_GROUPS)], axis=0)
    cand = jnp.where(expert_on > 0.0, sel, -jnp.inf)
    iota_e = lax.broadcasted_iota(jnp.int32, (N_EXPERTS, tm), 0).astype(F32)
    w = jnp.zeros((N_EXPERTS, tm), F32)
    chosen = jnp.zeros((N_EXPERTS, tm), F32)
    for _ in range(TOP_K):
        _, pick = _first_index_of_max(cand, iota_e)
        w = jnp.where(pick, scores, w)
        chosen = jnp.where(pick, 1.0, chosen)
        cand = jnp.where(pick, -jnp.inf, cand)
    return w / jnp.sum(w, axis=0, keepdims=True) * ROUTED_SCALE, chosen


MOE_TS = 1024
MOE_ROUTE_TM = 1024
MOE_ROW_TM = 512


def _slots_kernel(gates_ref, rank_ref, off_ref, pos_ref, gtok_ref):
    gates = gates_ref[...]
    rank = rank_ref[...]
    tm = gates.shape[1]
    slot = off_ref[...] + rank
    left = jnp.where(rank >= 0.0, 1.0, 0.0)
    iota_e = lax.broadcasted_iota(jnp.int32, (N_EXPERTS, tm), 0).astype(F32)
    pos_rows, gate_rows = [], []
    for _ in range(TOP_K):
        _, pick = _first_index_of_max(left, iota_e)
        pos_rows.append(jnp.sum(jnp.where(pick, slot, 0.0), axis=0, keepdims=True))
        gate_rows.append(jnp.sum(jnp.where(pick, gates, 0.0), axis=0, keepdims=True))
        left = jnp.where(pick, 0.0, left)
    pos_ref[...] = jnp.concatenate(pos_rows, axis=0).astype(jnp.int32)
    pad = jnp.zeros((LANES - TOP_K, tm), F32)
    gtok_ref[...] = jnp.concatenate(gate_rows + [pad], axis=0).T


def _slots(gates_t, rank_t, off):
    t = gates_t.shape[1]
    tm = MOE_ROUTE_TM
    return pl.pallas_call(
        _slots_kernel,
        out_shape=[jax.ShapeDtypeStruct((TOP_K, t), jnp.int32), jax.ShapeDtypeStruct((t, LANES), F32)],
        grid=(t // tm,),
        in_specs=[pl.BlockSpec((N_EXPERTS, tm), lambda i: (0, i)),
                  pl.BlockSpec((N_EXPERTS, tm), lambda i: (0, i)),
                  pl.BlockSpec((N_EXPERTS, 1), lambda i: (0, 0))],
        out_specs=[pl.BlockSpec((TOP_K, tm), lambda i: (0, i)),
                   pl.BlockSpec((tm, LANES), lambda i: (i, 0))],
        compiler_params=_cparams(("arbitrary",)),
        name="moe_slots",
    )(gates_t, rank_t, off)


SC_CORES = 2
SC_SUBCORES = 16
SC_ROWS = 64


def _dispatch(hp_all, slot_of):
    t, width = hp_all.shape
    n_slots = slot_of.shape[0]
    n_pad = n_slots - TOP_K * t
    workers = SC_CORES * SC_SUBCORES
    per_worker = t // workers
    pad_per_worker = n_pad // workers
    assert per_worker * workers == t and per_worker % SC_ROWS == 0
    assert pad_per_worker * workers == n_pad and pad_per_worker % SC_ROWS == 0
    mesh = plsc.VectorSubcoreMesh(core_axis_name="core", subcore_axis_name="subcore")

    @functools.partial(
        pl.kernel, mesh=mesh,
        out_type=jax.ShapeDtypeStruct((n_slots, width), jnp.int32),
        scratch_types=[pltpu.VMEM((SC_ROWS,), jnp.int32),
                       pltpu.VMEM((SC_ROWS, width), jnp.int32),
                       pltpu.SemaphoreType.DMA],
    )
    def scatter_rows(h_hbm, slot_hbm, out_hbm, idx_v, rows_v, sem):
        worker = lax.axis_index("subcore") * SC_CORES + lax.axis_index("core")
        base = worker * per_worker

        @pl.loop(0, per_worker // SC_ROWS)
        def _(j):
            first = base + j * SC_ROWS
            pltpu.sync_copy(h_hbm.at[pl.ds(first, SC_ROWS)], rows_v)
            for k in range(TOP_K):
                pltpu.sync_copy(slot_hbm.at[pl.ds(k * t + first, SC_ROWS)], idx_v)
                pltpu.async_copy(rows_v, out_hbm.at[idx_v], sem).wait()

        pltpu.sync_copy(h_hbm.at[pl.ds(0, SC_ROWS)], rows_v)
        pad_base = TOP_K * t + worker * pad_per_worker

        @pl.loop(0, pad_per_worker // SC_ROWS)
        def _(j):
            pltpu.sync_copy(slot_hbm.at[pl.ds(pad_base + j * SC_ROWS, SC_ROWS)], idx_v)
            pltpu.async_copy(rows_v, out_hbm.at[idx_v], sem).wait()

    return scatter_rows(hp_all, slot_of)


def _experts_kernel(te_ref, xs_ref, wg_ref, wu_ref, wd_ref, ys_ref, wgu_bf, wd_bf):
    i = pl.program_id(0)

    @pl.when((i == 0) | (te_ref[i] != te_ref[jnp.maximum(i, 1) - 1]))
    def _():
        wgu_bf[:, 0:D_EXPERT] = wg_ref[0].astype(BF16)
        wgu_bf[:, D_EXPERT:2 * D_EXPERT] = wu_ref[0].astype(BF16)
        wd_bf[...] = wd_ref[0].astype(BF16)

    left, right = _unpack_rows(xs_ref[...])
    x = jnp.concatenate([left, right], axis=1).astype(BF16)
    gu = _dot(x, wgu_bf[...])
    g = gu[:, 0:D_EXPERT]
    u = gu[:, D_EXPERT:2 * D_EXPERT]
    act = (g * _sigmoid(g)) * u
    ys_ref[...] = _pack_rows(_dot(act.astype(BF16), wd_bf[...]))


def _experts(xs, tile_expert, w_gate, w_up, w_down):
    n_slots, width = xs.shape
    d = 2 * width
    ts = MOE_TS
    return pl.pallas_call(
        _experts_kernel,
        out_shape=jax.ShapeDtypeStruct((n_slots, width), jnp.int32),
        grid_spec=pltpu.PrefetchScalarGridSpec(
            num_scalar_prefetch=1,
            grid=(n_slots // ts,),
            in_specs=[pl.BlockSpec((ts, width), lambda i, te: (i, 0)),
                      pl.BlockSpec((1, d, D_EXPERT), lambda i, te: (te[i], 0, 0)),
                      pl.BlockSpec((1, d, D_EXPERT), lambda i, te: (te[i], 0, 0)),
                      pl.BlockSpec((1, D_EXPERT, d), lambda i, te: (te[i], 0, 0))],
            out_specs=pl.BlockSpec((ts, width), lambda i, te: (i, 0)),
            scratch_shapes=[pltpu.VMEM((d, 2 * D_EXPERT), BF16), pltpu.VMEM((D_EXPERT, d), BF16)]),
        compiler_params=_cparams(("arbitrary",)),
        name="moe_experts",
    )(tile_expert, xs, w_gate, w_up, w_down)


def _gather_slots(y_slots, slot_of, t):
    width = y_slots.shape[1]
    workers = SC_CORES * SC_SUBCORES
    per_worker = t // workers
    n_blocks = (per_worker // SC_ROWS) * TOP_K
    assert per_worker * workers == t and per_worker % SC_ROWS == 0 and n_blocks % 2 == 0
    mesh = plsc.VectorSubcoreMesh(core_axis_name="core", subcore_axis_name="subcore")

    @functools.partial(
        pl.kernel, mesh=mesh,
        out_type=jax.ShapeDtypeStruct((TOP_K * t, width), jnp.int32),
        scratch_types=[pltpu.VMEM((SC_ROWS,), jnp.int32), pltpu.VMEM((SC_ROWS,), jnp.int32),
                       pltpu.VMEM((SC_ROWS, width), jnp.int32), pltpu.VMEM((SC_ROWS, width), jnp.int32),
                       pltpu.SemaphoreType.DMA, pltpu.SemaphoreType.DMA],
    )
    def gather_rows(ys_hbm, slot_hbm, out_hbm, idx0, idx1, rows0, rows1, sem0, sem1):
        worker = lax.axis_index("subcore") * SC_CORES + lax.axis_index("core")
        base = worker * per_worker

        def first_row(n):
            return (n % TOP_K) * t + base + (n // TOP_K) * SC_ROWS

        def start(n, idx_v, rows_v, sem):
            pltpu.sync_copy(slot_hbm.at[pl.ds(first_row(n), SC_ROWS)], idx_v)
            pltpu.async_copy(ys_hbm.at[idx_v], rows_v, sem)

        def finish(n, idx_v, rows_v, sem):
            pltpu.make_async_copy(ys_hbm.at[idx_v], rows_v, sem).wait()
            pltpu.sync_copy(rows_v, out_hbm.at[pl.ds(first_row(n), SC_ROWS)])

        start(0, idx0, rows0, sem0)

        @pl.loop(0, n_blocks, step=2)
        def _(n):
            start(n + 1, idx1, rows1, sem1)
            finish(n, idx0, rows0, sem0)

            @pl.when(n + 2 < n_blocks)
            def _():
                start(n + 2, idx0, rows0, sem0)

            finish(n + 1, idx1, rows1, sem1)

    return gather_rows(y_slots, slot_of)


def _combine_kernel(y_ref, h_ref, g2_ref, gtok_ref, rows_ref, wgs_ref, wus_ref, wds_ref, o_ref):
    h_left, h_right = _unpack_rows(h_ref[...])
    h = jnp.concatenate([h_left, h_right], axis=1).astype(BF16)
    gs = _dot(h, wgs_ref[...])
    us = _dot(h, wus_ref[...])
    shared = _dot(((gs * _sigmoid(gs)) * us).astype(BF16), wds_ref[...])

    gtok = gtok_ref[...]
    acc_left = acc_right = None
    for k in range(TOP_K):
        left, right = _unpack_rows(rows_ref[k])
        gate = gtok[:, k:k + 1]
        acc_left = gate * left if acc_left is None else acc_left + gate * left
        acc_right = gate * right if acc_right is None else acc_right + gate * right
    routed = jnp.concatenate([acc_left, acc_right], axis=1)
    o_ref[...] = y_ref[...] + g2_ref[0] * (routed + shared)


def _combine(y_all, hp_all, gate2, gtok, rows, wgs, wus, wds, *, first_token, tokens, seq):
    d = y_all.shape[1]
    width = hp_all.shape[1]
    tm = MOE_ROW_TM
    tile0 = first_token // tm
    nb = gate2.shape[0]
    tiles_per_batch = seq // tm

    def mod_map(i):
        return ((i // tiles_per_batch) if nb > 1 else 0, 0, 0)

    return pl.pallas_call(
        _combine_kernel,
        out_shape=jax.ShapeDtypeStruct((tokens, d), F32),
        grid=(tokens // tm,),
        in_specs=[pl.BlockSpec((tm, d), lambda i: (tile0 + i, 0)),
                  pl.BlockSpec((tm, width), lambda i: (tile0 + i, 0)),
                  pl.BlockSpec((1, 1, d), mod_map),
                  pl.BlockSpec((tm, LANES), lambda i: (tile0 + i, 0)),
                  pl.BlockSpec((TOP_K, tm, width), lambda i: (0, tile0 + i, 0)),
                  pl.BlockSpec((d, D_SHARED), lambda i: (0, 0)),
                  pl.BlockSpec((d, D_SHARED), lambda i: (0, 0)),
                  pl.BlockSpec((D_SHARED, d), lambda i: (0, 0))],
        out_specs=pl.BlockSpec((tm, d), lambda i: (i, 0)),
        compiler_params=_cparams(("arbitrary",)),
        name="moe_combine",
    )(y_all, hp_all, gate2, gtok, rows, wgs, wus, wds)


def _expert_layout(counts, n_tiles):
    cnt = counts.astype(jnp.int32)
    tiles = (cnt + (MOE_TS - 1)) // MOE_TS
    last_tile = jnp.cumsum(tiles)
    off = (last_tile - tiles) * MOE_TS
    pad_lo = off + cnt
    pad_hi = (off + tiles * MOE_TS).at[N_EXPERTS - 1].set(n_tiles * MOE_TS)
    pad_cnt = pad_hi - pad_lo
    pad_last = jnp.cumsum(pad_cnt)
    shift = pad_lo - (pad_last - pad_cnt)
    j = jnp.arange(N_EXPERTS * MOE_TS, dtype=jnp.int32)
    past = (pad_last[None, :-1] <= j[:, None]).astype(jnp.int32)
    pad_slots = j + shift[0] + jnp.sum(past * (shift[1:] - shift[:-1])[None, :], axis=1)
    tile_ids = jnp.arange(n_tiles, dtype=jnp.int32)
    tile_expert = jnp.minimum(
        jnp.sum((last_tile[None, :] <= tile_ids[:, None]).astype(jnp.int32), axis=1), N_EXPERTS - 1)
    return off, pad_slots, tile_expert


def _rope_tables(n_tokens):
    t = jnp.arange(n_tokens)
    row = (t // GRID_W).astype(F32)
    col = (t % GRID_W).astype(F32)
    nf = HEAD_DIM // 4
    freqs = ROPE_THETA ** (-jnp.arange(nf, dtype=F32) / nf)
    ang_r = row[:, None] * freqs
    ang_c = col[:, None] * freqs
    cos = jnp.concatenate([jnp.cos(ang_r)] * 2 + [jnp.cos(ang_c)] * 2, axis=1)
    sin = jnp.concatenate([-jnp.sin(ang_r), jnp.sin(ang_r), -jnp.sin(ang_c), jnp.sin(ang_c)], axis=1)
    reps = LANES // HEAD_DIM
    return jnp.tile(cos, (1, reps)), jnp.tile(sin, (1, reps))


def _head_gains(qn_a, kn_a, qn_b, kn_b):
    ones = jnp.ones((HEAD_DIM,), F32)
    parts = ([qn_a] * N_HEADS_A + [kn_a] * N_KV_A + [ones] * N_KV_A
             + [qn_b] * N_HEADS_B + [kn_b] * N_HEADS_B + [ones] * N_HEADS_B)
    return jnp.concatenate(parts).reshape(1, IN_COLS).astype(F32)


def _same_head_indicator():
    i = np.arange(MXU_DIM)
    return jnp.asarray((i[:, None] // HEAD_DIM) == (i[None, :] // HEAD_DIM), BF16)


def _token_major(cache):
    b, h, s, hd = cache.shape
    return cache.transpose(0, 2, 1, 3).reshape(b, s, h * hd).astype(BF16)


def kernel(x_prompt, x_sample, cache_k_a, cache_v_a, cache_k_b, cache_v_b, c, c_ctx, w_mod, b_mod, norm1, norm2, w_in, qn_a, kn_a, qn_b, kn_b, rpb, on_a, on_b, w_out, w_router, router_bias, w_gate_e, w_up_e, w_down_e, w_gate_s, w_up_s, w_down_s):
    depth = w_mod.shape[0]
    assert depth == 1
    l = 0
    bp, sp, d = x_prompt.shape
    bs, ss, _ = x_sample.shape

    cvec = jnp.concatenate([c_ctx[None, :], c], axis=0)
    rows = -(-cvec.shape[0] // 8) * 8
    cvec = jnp.pad(cvec, ((0, rows - cvec.shape[0]), (0, 0)))
    mod = _adaln(cvec, w_mod[l], b_mod[l])
    mod_p = [m.reshape(1, 1, d) for m in jnp.split(mod[0:1], 6, axis=-1)]
    mod_s = [m.reshape(bs, 1, d) for m in jnp.split(mod[1:1 + bs], 6, axis=-1)]
    mod_all = [m.reshape(1 + bs, 1, d) for m in jnp.split(mod[0:1 + bs], 6, axis=-1)]

    w_in_bf = w_in[l].astype(BF16)
    w_out_bf = w_out[l].astype(BF16)
    gain = _head_gains(qn_a[l], kn_a[l], qn_b[l], kn_b[l])
    seg = _same_head_indicator()
    n1 = norm1[l].reshape(1, d)
    n2 = norm2[l].reshape(1, d)
    ona = on_a[l].reshape(1, WIDTH_A)
    onb = on_b[l].reshape(1, WIDTH_B)
    wr_t = w_router[l].T
    wr_hi = wr_t.astype(BF16)
    wr_lo = (wr_t - wr_hi.astype(F32)).astype(BF16)
    rbias = router_bias[l].reshape(N_EXPERTS, 1).astype(F32)
    wgs = w_gate_s[l].astype(BF16)
    wus = w_up_s[l].astype(BF16)
    wds = w_down_s[l].astype(BF16)
    t_p = bp * sp
    t_s = bs * ss
    t_all = t_p + t_s

    xp = x_prompt.reshape(t_p, d)
    proj_p, st_ka, st_va, st_kb, st_vb = _project(
        xp, mod_p[0], mod_p[1], n1, w_in_bf, gain, seg, None, tm=sp, seq=sp, states=True)
    oa_p, ob_p = _context_attention(proj_p, seq=sp)

    xs = x_sample.reshape(t_s, d)
    proj_s, = _project(xs, mod_s[0], mod_s[1], n1, w_in_bf, gain, seg, _rope_tables(ss),
                       tm=1024, seq=ss, states=False)
    oa_s = _latent_gqa(proj_s, _token_major(cache_k_a[:, l]), _token_major(cache_v_a[:, l]),
                       _query_norm_bound(qn_a[l]), seq=ss, tq=256)
    bias_t = _neighbourhood_bias(rpb[l], ss // GRID_W)
    ob_s = _latent_neighbourhood(proj_s, _token_major(cache_k_b[:, l]), _token_major(cache_v_b[:, l]),
                                 bias_t, _neighbourhood_bounds(qn_b[l], kn_b[l], rpb[l]), seq=ss)

    def moe_tail(x2d, oa, ob, mods, seq):
        t = x2d.shape[0]
        y1, hp, gates_t, rank_t, counts = _merge(x2d, oa, ob, ona, onb, w_out_bf, mods[2], mods[3], mods[4], n2,
                                                 wr_hi, wr_lo, rbias, tm=512, seq=seq)
        n_tiles = t * TOP_K // MOE_TS + N_EXPERTS
        off, pad_slots, tile_expert = _expert_layout(counts[:, 0], n_tiles)
        pos, gtok = _slots(gates_t, rank_t, off.astype(F32).reshape(N_EXPERTS, 1))
        slot_of = pos.reshape(TOP_K * t)
        x_slots = _dispatch(hp, jnp.concatenate([slot_of, pad_slots]))
        y_slots = _experts(x_slots, tile_expert, w_gate_e[l], w_up_e[l], w_down_e[l])
        rows = _gather_slots(y_slots, slot_of, t).reshape(TOP_K, t, d // 2)
        return _combine(y1, hp, mods[5], gtok, rows, wgs, wus, wds, first_token=0, tokens=t, seq=seq)

    y_p = moe_tail(xp, oa_p, ob_p, mod_p, sp)
    y_s = moe_tail(xs, oa_s, ob_s, mod_s, ss)

    return (y_p.reshape(bp, sp, d), y_s.reshape(bs, ss, d), st_ka, st_va, st_kb, st_vb)
```

```python
import functools

import numpy as np
import jax
import jax.numpy as jnp
from jax import lax
from jax.experimental import pallas as pl
from jax.experimental.pallas import tpu as pltpu
from jax.experimental.pallas import tpu_sc as plsc

F32 = jnp.float32
BF16 = jnp.bfloat16

D_MODEL = 1024
HEAD_DIM = 64
N_HEADS_A = 8
N_KV_A = 2
GROUP_A = N_HEADS_A // N_KV_A
N_HEADS_B = 8
WIDTH_A = N_HEADS_A * HEAD_DIM
WIDTH_B = N_HEADS_B * HEAD_DIM
KV_WIDTH_A = N_KV_A * HEAD_DIM
IN_COLS = WIDTH_A + 2 * KV_WIDTH_A + 3 * WIDTH_B
GRID_W = 64
ROPE_THETA = 10000.0
NA_KH = 8
NA_KW = 16
N_EXPERTS = 64
N_GROUPS = 8
GROUP_SIZE = N_EXPERTS // N_GROUPS
TOPK_GROUPS = 4
TOP_K = 8
D_EXPERT = 256
D_SHARED = 256
ROUTED_SCALE = 2.5
EPS = 1e-6

LANES = 128
MXU_DIM = 256
MASKED = -1e30

COL_QA = 0
COL_KA = WIDTH_A
COL_VA = COL_KA + KV_WIDTH_A
COL_QB = COL_VA + KV_WIDTH_A
COL_KB = COL_QB + WIDTH_B
COL_VB = COL_KB + WIDTH_B

NA_QROWS = 8
NA_KROWS = 2 * NA_KH
NA_TQ = NA_QROWS * GRID_W
NA_TK = NA_KROWS * GRID_W
NA_KBLK = 256

VMEM_LIMIT = 56 * 1024 * 1024


def _cparams(sem):
    return pltpu.CompilerParams(dimension_semantics=sem, vmem_limit_bytes=VMEM_LIMIT)


def _dot(a, b):
    return jnp.dot(a, b, preferred_element_type=F32)


def _dot_nt(a, b):
    return lax.dot_general(a, b, (((1,), (1,)), ((), ())), preferred_element_type=F32)


def _sigmoid(x):
    return 1.0 / (1.0 + jnp.exp(-x))


def _rms(x):
    return x * lax.rsqrt(jnp.mean(x * x, axis=-1, keepdims=True) + EPS)


def _pack_rows(x):
    n = x.shape[1] // 2
    hi = lax.bitcast_convert_type(x[:, :n].astype(BF16).astype(F32), jnp.int32)
    lo = lax.bitcast_convert_type(x[:, n:].astype(BF16).astype(F32), jnp.int32)
    return hi | lax.shift_right_logical(lo, 16)


def _unpack_rows(w):
    left = lax.bitcast_convert_type(w & jnp.int32(-65536), F32)
    right = lax.bitcast_convert_type(lax.shift_left(w, 16), F32)
    return left, right


def _mod_kernel(c_ref, w_ref, b_ref, o_ref):
    c = c_ref[...]
    s = c * _sigmoid(c)
    o_ref[...] = jnp.dot(s, w_ref[...], preferred_element_type=F32,
                         precision=lax.Precision.HIGHEST) + b_ref[...]


def _adaln(cvec, w_mod, b_mod):
    rows, d = cvec.shape
    n = w_mod.shape[1]
    tn = 512
    return pl.pallas_call(
        _mod_kernel,
        out_shape=jax.ShapeDtypeStruct((rows, n), F32),
        grid=(n // tn,),
        in_specs=[pl.BlockSpec((rows, d), lambda j: (0, 0)),
                  pl.BlockSpec((d, tn), lambda j: (0, j)),
                  pl.BlockSpec((1, tn), lambda j: (0, j))],
        out_specs=pl.BlockSpec((rows, tn), lambda j: (0, j)),
        compiler_params=_cparams(("arbitrary",)),
        name="adaln_mod",
    )(cvec, w_mod, b_mod.reshape(1, n))


def _proj_chunks():
    def split(c0, width, step, *flags):
        return [(c0 + i, min(step, width - i)) + flags for i in range(0, width, step)]
    return (split(COL_QA, WIDTH_A, MXU_DIM, True, True, True)
            + split(COL_KA, KV_WIDTH_A, MXU_DIM, True, True, False)
            + split(COL_VA, KV_WIDTH_A, MXU_DIM, False, False, False)
            + split(COL_QB, WIDTH_B, MXU_DIM, True, False, True)
            + split(COL_KB, WIDTH_B, MXU_DIM, True, False, False)
            + split(COL_VB, WIDTH_B, MXU_DIM, False, False, False))


_PROJ_CHUNKS = _proj_chunks()


def _proj_kernel(*refs, rope, states):
    x_ref, sh_ref, sc_ref, n1_ref, w_ref, gain_ref, seg_ref = refs[:7]
    pos = 7
    if rope:
        cos_ref, sin_ref = refs[pos:pos + 2]
        pos += 2
    out_ref = refs[pos]
    pos += 1
    if states:
        ka_ref, va_ref, kb_ref, vb_ref = refs[pos:pos + 4]
        state_of = {COL_KA: ka_ref, COL_VA: va_ref, COL_KB: kb_ref, COL_VB: vb_ref}

    x = x_ref[...]
    h = _rms(x) * n1_ref[...]
    h = h * (1.0 + sc_ref[0]) + sh_ref[0]
    p = _dot(h.astype(BF16), w_ref[...])

    for c0, w, normed, roped, is_query in _PROJ_CHUNKS:
        pc = p[:, c0:c0 + w]
        if normed:
            seg = seg_ref[0:w, 0:w]
            sq = pc * pc
            hi = sq.astype(BF16)
            lo = (sq - hi.astype(F32)).astype(BF16)
            ss = _dot(hi, seg) + _dot(lo, seg)
            pc = pc * lax.rsqrt(ss * (1.0 / HEAD_DIM) + EPS) * gain_ref[:, c0:c0 + w]
        if states:
            for start, ref in state_of.items():
                if start <= c0 < start + ref.shape[2] * HEAD_DIM:
                    base = (c0 - start) // HEAD_DIM
                    for hh in range(w // HEAD_DIM):
                        ref[0, 0, base + hh] = pc[:, hh * HEAD_DIM:(hh + 1) * HEAD_DIM]
        if rope and roped:
            reps = w // LANES
            cos = jnp.concatenate([cos_ref[...]] * reps, axis=1) if reps > 1 else cos_ref[...]
            sin = jnp.concatenate([sin_ref[...]] * reps, axis=1) if reps > 1 else sin_ref[...]
            lane = lax.broadcasted_iota(jnp.int32, pc.shape, 1)
            first_half = (lane % (HEAD_DIM // 2)) < (HEAD_DIM // 4)
            partner = jnp.where(first_half,
                                pltpu.roll(pc, w - HEAD_DIM // 4, 1),
                                pltpu.roll(pc, HEAD_DIM // 4, 1))
            pc = pc * cos + partner * sin
        if is_query:
            pc = pc * (HEAD_DIM ** -0.5)
        out_ref[:, c0:c0 + w] = pc.astype(BF16)


def _project(x2d, shift, scale, norm1, w_in_bf, gain, seg, rope_tabs, *, tm, seq, states):
    t, d = x2d.shape
    nb = shift.shape[0]
    tiles_per_batch = seq // tm
    rope = rope_tabs is not None

    def mod_map(i):
        return ((i // tiles_per_batch) if nb > 1 else 0, 0, 0)

    in_specs = [pl.BlockSpec((tm, d), lambda i: (i, 0)),
                pl.BlockSpec((1, 1, d), mod_map),
                pl.BlockSpec((1, 1, d), mod_map),
                pl.BlockSpec((1, d), lambda i: (0, 0)),
                pl.BlockSpec((d, IN_COLS), lambda i: (0, 0)),
                pl.BlockSpec((1, IN_COLS), lambda i: (0, 0)),
                pl.BlockSpec((MXU_DIM, MXU_DIM), lambda i: (0, 0))]
    args = [x2d, shift, scale, norm1, w_in_bf, gain, seg]
    if rope:
        in_specs += [pl.BlockSpec((tm, LANES), lambda i: (i % tiles_per_batch, 0))] * 2
        args += list(rope_tabs)
    out_shape = [jax.ShapeDtypeStruct((t, IN_COLS), BF16)]
    out_specs = [pl.BlockSpec((tm, IN_COLS), lambda i: (i, 0))]
    if states:
        assert tm == seq
        b = t // seq
        for nh in (N_KV_A, N_KV_A, N_HEADS_B, N_HEADS_B):
            out_shape.append(jax.ShapeDtypeStruct((b, 1, nh, seq, HEAD_DIM), F32))
            out_specs.append(pl.BlockSpec((1, 1, nh, seq, HEAD_DIM), lambda i: (i, 0, 0, 0, 0)))
    return pl.pallas_call(
        functools.partial(_proj_kernel, rope=rope, states=states),
        out_shape=out_shape,
        grid=(t // tm,),
        in_specs=in_specs,
        out_specs=out_specs,
        compiler_params=_cparams(("arbitrary",)),
        name="proj_states" if states else "proj_rope",
    )(*args)


def _lane_half(shape):
    return lax.broadcasted_iota(jnp.int32, shape, 1) // HEAD_DIM


def _keep_half(x, half):
    return jnp.where(_lane_half(x.shape) == half, x, jnp.zeros_like(x))


def _transpose_bf16(x):
    return x.astype(F32).T.astype(BF16)


def _attend(q, keys, values_t, biases):
    return _softmax_av(_scores(q, keys, biases), values_t)


def _scores(q, keys, biases):
    scores = []
    for k, b in zip(keys, biases):
        s = _dot_nt(k, q)
        if b is not None:
            s = s + b
        scores.append(s)
    return scores


def _softmax_av(scores, values_t):
    m = functools.reduce(jnp.maximum, [jnp.max(s, axis=0, keepdims=True) for s in scores])
    denom = None
    out = None
    for s, vt in zip(scores, values_t):
        p = jnp.exp(s - m)
        ps = jnp.sum(p, axis=0, keepdims=True)
        po = _dot(vt, p.astype(BF16))
        denom = ps if denom is None else denom + ps
        out = po if out is None else out + po
    return out / denom


def _swap_halves(q_bf16):
    return pltpu.roll(q_bf16.astype(F32), HEAD_DIM, 1).astype(BF16)


def _gqa_heads(q_of_pair, keys_by_group, values_t):
    outs = []
    for h in range(N_HEADS_A):
        g = h // GROUP_A
        q = q_of_pair(h // 2)
        if h % 2 != g:
            q = _swap_halves(q)
        o = _attend(q, keys_by_group[g], values_t, [None] * len(values_t))
        outs.append(o[g * HEAD_DIM:(g + 1) * HEAD_DIM])
    return jnp.concatenate(outs, axis=0)


def _ctx_attn_kernel(p_ref, oa_ref, ob_ref):
    ka = p_ref[:, COL_KA:COL_KA + LANES]
    va_t = [_transpose_bf16(p_ref[:, COL_VA:COL_VA + LANES])]
    keys_by_group = [[_keep_half(ka, g)] for g in range(N_KV_A)]
    oa = _gqa_heads(lambda i: p_ref[:, COL_QA + i * LANES:COL_QA + (i + 1) * LANES],
                    keys_by_group, va_t)
    oa_ref[...] = oa.T

    outs = []
    for i in range(N_HEADS_B // 2):
        q = p_ref[:, COL_QB + i * LANES:COL_QB + (i + 1) * LANES]
        k = p_ref[:, COL_KB + i * LANES:COL_KB + (i + 1) * LANES]
        vt = [_transpose_bf16(p_ref[:, COL_VB + i * LANES:COL_VB + (i + 1) * LANES])]
        for half in range(2):
            o = _attend(q, [_keep_half(k, half)], vt, [None])
            outs.append(o[half * HEAD_DIM:(half + 1) * HEAD_DIM])
    ob_ref[...] = jnp.concatenate(outs, axis=0).T


def _context_attention(proj, *, seq):
    t = proj.shape[0]
    return pl.pallas_call(
        _ctx_attn_kernel,
        out_shape=[jax.ShapeDtypeStruct((t, WIDTH_A), F32), jax.ShapeDtypeStruct((t, WIDTH_B), F32)],
        grid=(t // seq,),
        in_specs=[pl.BlockSpec((seq, IN_COLS), lambda i: (i, 0))],
        out_specs=[pl.BlockSpec((seq, WIDTH_A), lambda i: (i, 0)),
                   pl.BlockSpec((seq, WIDTH_B), lambda i: (i, 0))],
        compiler_params=_cparams(("arbitrary",)),
        name="context_attention",
    )(proj)


ATTN_SAFE_SHIFT = 40.0
ONES_ROWS = 16


def _round_up_bf16(x):
    return (x * (1.0 + 2.0 ** -6)).astype(BF16).astype(F32)


def _query_norm_bound(gain):
    return jnp.max(jnp.abs(gain)).reshape(1, 1).astype(F32)


def _ones_lane(g):
    return (1 - g) * HEAD_DIM


def _gqa_latent_kernel(q_ref, k_ref, v_ref, ck_ref, cv_ref, qmax_ref, o_ref,
                       kg_ref, ckg_ref, vt_ref, cvt_ref, shift_ref):
    lane_k = lax.broadcasted_iota(jnp.int32, (1, LANES), 1)

    @pl.when(pl.program_id(1) == 0)
    def _():
        k = k_ref[...]
        ck = ck_ref[0]
        vt = v_ref[...].astype(F32).T
        cvt = cv_ref[0].astype(F32).T
        for g in range(N_KV_A):
            kf = _keep_half(k, g).astype(F32)
            ckf = _keep_half(ck, g).astype(F32)
            ksq = jnp.maximum(jnp.max(jnp.sum(kf * kf, axis=1, keepdims=True), axis=0, keepdims=True),
                              jnp.max(jnp.sum(ckf * ckf, axis=1, keepdims=True), axis=0, keepdims=True))
            shift_ref[g] = jnp.broadcast_to(_round_up_bf16(qmax_ref[...] * jnp.sqrt(ksq)), shift_ref.shape[1:])
            kg_ref[g] = jnp.where(lane_k == _ones_lane(g), 1.0, kf).astype(BF16)
            ckg_ref[g] = jnp.where(lane_k == _ones_lane(g), 1.0, ckf).astype(BF16)
            rows = slice(g * HEAD_DIM, (g + 1) * HEAD_DIM)
            vt_ref[g] = jnp.concatenate([vt[rows], jnp.ones((ONES_ROWS, vt.shape[1]), F32)], axis=0).astype(BF16)
            cvt_ref[g] = jnp.concatenate([cvt[rows], jnp.ones((ONES_ROWS, cvt.shape[1]), F32)], axis=0).astype(BF16)

    tq = q_ref.shape[0]
    lane_q = lax.broadcasted_iota(jnp.int32, (GROUP_A * tq, LANES), 1)
    queries, shifts = [], []
    for g in range(N_KV_A):
        qs = []
        for j in range(GROUP_A):
            h = g * GROUP_A + j
            q = q_ref[:, (h // 2) * LANES:(h // 2 + 1) * LANES].astype(F32)
            qs.append(q if h % 2 == g else pltpu.roll(q, HEAD_DIM, 1))
        queries.append(jnp.where(lane_q // HEAD_DIM == g, jnp.concatenate(qs, axis=0), 0.0))
        shifts.append(shift_ref[g][0:1, 0:1])
    safe = jnp.max(jnp.maximum(shift_ref[0], shift_ref[1])) <= ATTN_SAFE_SHIFT

    def attend(g, p_lat, p_ctx):
        half = p_lat.shape[1] // 2
        o = jnp.concatenate([_dot(vt_ref[g], p_lat[:, :half]) + _dot(cvt_ref[g], p_ctx[:, :half]),
                             _dot(vt_ref[g], p_lat[:, half:]) + _dot(cvt_ref[g], p_ctx[:, half:])],
                            axis=1)
        o = o[:HEAD_DIM] / o[HEAD_DIM:HEAD_DIM + 1]
        heads = jnp.concatenate([o[:, j * tq:(j + 1) * tq] for j in range(GROUP_A)], axis=0)
        o_ref[:, g * GROUP_A * HEAD_DIM:(g + 1) * GROUP_A * HEAD_DIM] = heads.T

    def with_bound():
        for g in range(N_KV_A):
            qa = jnp.where(lane_q == _ones_lane(g), -shifts[g], queries[g]).astype(BF16)
            attend(g, jnp.exp(_dot_nt(kg_ref[g], qa)).astype(BF16), jnp.exp(_dot_nt(ckg_ref[g], qa)).astype(BF16))

    def with_row_max():
        for g in range(N_KV_A):
            qa = queries[g].astype(BF16)
            s_lat = _dot_nt(kg_ref[g], qa)
            s_ctx = _dot_nt(ckg_ref[g], qa)
            m = jnp.maximum(jnp.max(s_lat, axis=0, keepdims=True), jnp.max(s_ctx, axis=0, keepdims=True))
            attend(g, jnp.exp(s_lat - m).astype(BF16), jnp.exp(s_ctx - m).astype(BF16))

    pl.when(safe)(with_bound)
    pl.when(jnp.logical_not(safe))(with_row_max)


def _latent_gqa(proj, ctx_k, ctx_v, qmax, *, seq, tq):
    t = proj.shape[0]
    b = t // seq
    nq = seq // tq
    past = ctx_k.shape[1]
    return pl.pallas_call(
        _gqa_latent_kernel,
        out_shape=jax.ShapeDtypeStruct((t, WIDTH_A), F32),
        grid=(b, nq),
        in_specs=[pl.BlockSpec((tq, WIDTH_A), lambda bi, qi: (bi * nq + qi, 0)),
                  pl.BlockSpec((seq, LANES), lambda bi, qi: (bi, COL_KA // LANES)),
                  pl.BlockSpec((seq, LANES), lambda bi, qi: (bi, COL_VA // LANES)),
                  pl.BlockSpec((1, past, LANES), lambda bi, qi: (bi, 0, 0)),
                  pl.BlockSpec((1, past, LANES), lambda bi, qi: (bi, 0, 0)),
                  pl.BlockSpec((1, 1), lambda bi, qi: (0, 0))],
        out_specs=pl.BlockSpec((tq, WIDTH_A), lambda bi, qi: (bi * nq + qi, 0)),
        scratch_shapes=[pltpu.VMEM((N_KV_A, seq, LANES), BF16),
                        pltpu.VMEM((N_KV_A, past, LANES), BF16),
                        pltpu.VMEM((N_KV_A, HEAD_DIM + ONES_ROWS, seq), BF16),
                        pltpu.VMEM((N_KV_A, HEAD_DIM + ONES_ROWS, past), BF16),
                        pltpu.VMEM((N_KV_A, 8, LANES), F32)],
        compiler_params=_cparams(("arbitrary", "arbitrary")),
        name="latent_gqa",
    )(proj, proj, proj, ctx_k, ctx_v, qmax)


def _na_kernel(q_ref, k_ref, v_ref, ck_ref, cv_ref, bias_ref, bound_ref, o_ref, keys_ref, vt_ref, shift_ref,
               *, rows):
    i = pl.program_id(2)
    n_kblk = k_ref.shape[0] // NA_KBLK
    lane_k = lax.broadcasted_iota(jnp.int32, (1, LANES), 1)
    one = jnp.ones((), BF16)

    @pl.when(i == 0)
    def _():
        k = k_ref[...]
        ck = ck_ref[0]
        for half in range(2):
            kh = jnp.where(lane_k == _ones_lane(half), one, _keep_half(k, half))
            keys_ref[half, 0:n_kblk] = kh.reshape(n_kblk, NA_KBLK, LANES)
            ckh = _keep_half(ck, half)
            keys_ref[half, n_kblk] = jnp.where(lane_k == _ones_lane(half), one, ckh)
            ckf = ckh.astype(F32)
            ctx_norm = jnp.sqrt(jnp.max(jnp.sum(ckf * ckf, axis=1, keepdims=True), axis=0, keepdims=True))
            consts = bound_ref[0, half:half + 1, :]
            kmax = jnp.maximum(ctx_norm, consts[:, 2:3])
            shift_ref[half] = jnp.broadcast_to(_round_up_bf16(consts[:, 0:1] * kmax + consts[:, 1:2]),
                                               shift_ref.shape[1:])
        ones_rows = jnp.ones((ONES_ROWS, NA_KBLK), F32)
        vt = v_ref[...].astype(F32).T
        for j in range(n_kblk):
            vt_ref[j] = jnp.concatenate([vt[:, j * NA_KBLK:(j + 1) * NA_KBLK], ones_rows], axis=0).astype(BF16)
        vt_ref[n_kblk] = jnp.concatenate([cv_ref[0].astype(F32).T, ones_rows], axis=0).astype(BF16)

    q = q_ref[...]
    lane_q = lax.broadcasted_iota(jnp.int32, q.shape, 1)
    first = _na_first_key_block(i, rows)
    n_qblk = rows // NA_QROWS
    variant = jnp.where(i == 0, 0, jnp.where(i == n_qblk - 1, 2, 1))
    blocks = [first + j for j in range(NA_TK // NA_KBLK)] + [n_kblk]
    values_t = [vt_ref[blk] for blk in blocks]
    heads = []
    for half in range(2):
        keys = [keys_ref[half, blk] for blk in blocks]
        biases = [bias_ref[variant, half, j * NA_KBLK:(j + 1) * NA_KBLK, :] for j in range(NA_TK // NA_KBLK)] + [None]
        heads.append((shift_ref[half][0:1, 0:1], keys, biases))
    safe = jnp.max(jnp.maximum(shift_ref[0], shift_ref[1])) <= ATTN_SAFE_SHIFT

    def attend(probabilities):
        outs = []
        for half, ps in enumerate(probabilities):
            o = functools.reduce(lambda a, b: a + b, [_dot(vt, p) for vt, p in zip(values_t, ps)])
            outs.append(o[half * HEAD_DIM:(half + 1) * HEAD_DIM] / o[2 * HEAD_DIM:2 * HEAD_DIM + 1])
        o_ref[...] = jnp.concatenate(outs, axis=0).T

    def with_bound():
        probabilities = []
        for half, (shift, keys, biases) in enumerate(heads):
            qa = jnp.where(lane_q == _ones_lane(half), (-shift).astype(BF16), _keep_half(q, half))
            probabilities.append([jnp.exp(s).astype(BF16) for s in _scores(qa, keys, biases)])
        attend(probabilities)

    def with_row_max():
        all_scores = [_scores(_keep_half(q, half), keys, biases) for half, (_, keys, biases) in enumerate(heads)]
        probabilities = []
        for scores in all_scores:
            m = functools.reduce(jnp.maximum, [jnp.max(s, axis=0, keepdims=True) for s in scores])
            probabilities.append([jnp.exp(s - m).astype(BF16) for s in scores])
        attend(probabilities)

    pl.when(safe)(with_bound)
    pl.when(jnp.logical_not(safe))(with_row_max)


def _na_first_key_block(i, rows):
    per_qblock = NA_QROWS * GRID_W // NA_KBLK
    lead = (NA_KH // 2) * GRID_W // NA_KBLK
    return jnp.clip(per_qblock * i - lead, 0, (rows - NA_KROWS) * GRID_W // NA_KBLK)


def _latent_neighbourhood(proj, ctx_k, ctx_v, bias_t, bounds, *, seq):
    t = proj.shape[0]
    b = t // seq
    rows = seq // GRID_W
    nblk = rows // NA_QROWS
    n_kblk = seq // NA_KBLK
    past = ctx_k.shape[1]
    assert past == NA_KBLK
    grid = (N_HEADS_B // 2, b, nblk)
    in_specs = [pl.BlockSpec((NA_TQ, LANES), lambda hp, bi, i: (bi * nblk + i, COL_QB // LANES + hp)),
                pl.BlockSpec((seq, LANES), lambda hp, bi, i: (bi, COL_KB // LANES + hp)),
                pl.BlockSpec((seq, LANES), lambda hp, bi, i: (bi, COL_VB // LANES + hp)),
                pl.BlockSpec((1, past, LANES), lambda hp, bi, i: (bi, 0, hp)),
                pl.BlockSpec((1, past, LANES), lambda hp, bi, i: (bi, 0, hp)),
                pl.BlockSpec((3, 2, NA_TK, NA_TQ), lambda hp, bi, i: (0, hp, 0, 0)),
                pl.BlockSpec((1, 2, LANES), lambda hp, bi, i: (hp, 0, 0))]
    return pl.pallas_call(
        functools.partial(_na_kernel, rows=rows),
        out_shape=jax.ShapeDtypeStruct((t, WIDTH_B), F32),
        grid=grid,
        in_specs=in_specs,
        out_specs=pl.BlockSpec((NA_TQ, LANES), lambda hp, bi, i: (bi * nblk + i, hp)),
        scratch_shapes=[pltpu.VMEM((2, n_kblk + 1, NA_KBLK, LANES), BF16),
                        pltpu.VMEM((n_kblk + 1, 2 * HEAD_DIM + ONES_ROWS, NA_KBLK), BF16),
                        pltpu.VMEM((2, 8, LANES), F32)],
        compiler_params=_cparams(("arbitrary", "arbitrary", "arbitrary")),
        name="latent_neighbourhood",
    )(proj, proj, proj, ctx_k, ctx_v, bias_t, bounds)


def _neighbourhood_bounds(qn_b, kn_b, rpb):
    n_heads = rpb.shape[0]
    qmax = jnp.broadcast_to(_query_norm_bound(qn_b), (n_heads, 1))
    kmax = jnp.broadcast_to(_query_norm_bound(kn_b) * (HEAD_DIM ** 0.5), (n_heads, 1))
    bmax = jnp.maximum(jnp.max(rpb.reshape(n_heads, -1), axis=1, keepdims=True), 0.0).astype(F32)
    table = jnp.concatenate([qmax, bmax, kmax, jnp.zeros((n_heads, LANES - 3), F32)], axis=1)
    return table.reshape(n_heads // 2, 2, LANES)


def _neighbourhood_bias(rpb, rows):
    nblk = rows // NA_QROWS
    n_dr = 2 * NA_KH - 1
    n_dc = 2 * NA_KW - 1
    kc = np.arange(GRID_W)[:, None]
    qc = np.arange(GRID_W)[None, :]
    ws = np.clip(qc - NA_KW // 2, 0, GRID_W - NA_KW)
    col_ok = (kc >= ws) & (kc < ws + NA_KW)
    dc = np.clip(kc - qc + NA_KW - 1, 0, n_dc - 1)
    dc_onehot = (dc[None] == np.arange(n_dc)[:, None, None]).astype(np.float32)
    tiles = jnp.einsum('hab,bkq->hakq', rpb.astype(F32), jnp.asarray(dc_onehot),
                       precision=lax.Precision.HIGHEST)
    tiles = jnp.where(jnp.asarray(col_ok)[None, None], tiles, MASKED)
    masked_tile = jnp.full((rpb.shape[0], 1, GRID_W, GRID_W), MASKED, F32)
    tiles = jnp.concatenate([tiles, masked_tile], axis=1)
    pick = np.zeros((3, NA_KROWS, NA_QROWS, n_dr + 1), np.float32)
    for v, i in enumerate((0, 1, nblk - 1)):
        r0 = i * NA_QROWS
        ks = int(np.clip(r0 - NA_KH // 2, 0, rows - NA_KROWS))
        for kl in range(NA_KROWS):
            for ql in range(NA_QROWS):
                kr, qr = ks + kl, r0 + ql
                rs = int(np.clip(qr - NA_KH // 2, 0, rows - NA_KH))
                ok = rs <= kr < rs + NA_KH
                pick[v, kl, ql, (kr - qr + NA_KH - 1) if ok else n_dr] = 1.0
    bias = jnp.einsum('vkqa,hacd->vhkcqd', jnp.asarray(pick), tiles, precision=lax.Precision.HIGHEST)
    return bias.reshape(3, rpb.shape[0], NA_TK, NA_TQ)


def _merge_kernel(x_ref, oa_ref, ob_ref, ona_ref, onb_ref, wo_ref, g1_ref, sh2_ref, sc2_ref, n2_ref,
                  wrh_ref, wrl_ref, rb_ref, tri_ref, y_ref, hp_ref, gates_ref, rank_ref, count_ref):
    @pl.when(pl.program_id(0) == 0)
    def _():
        count_ref[...] = jnp.zeros_like(count_ref)

    na = (_rms(oa_ref[...]) * ona_ref[...]).astype(BF16)
    nb = (_rms(ob_ref[...]) * onb_ref[...]).astype(BF16)
    mix = _dot(na, wo_ref[0:WIDTH_A, :]) + _dot(nb, wo_ref[WIDTH_A:WIDTH_A + WIDTH_B, :])
    y = x_ref[...] + g1_ref[0] * mix
    y_ref[...] = y
    h = _rms(y) * n2_ref[...]
    h = h * (1.0 + sc2_ref[0]) + sh2_ref[0]
    hp_ref[...] = _pack_rows(h)
    gates, chosen = _router_gates(h, wrh_ref[...], wrl_ref[...], rb_ref[...])
    gates_ref[...] = gates
    before = _dot(chosen.astype(BF16), tri_ref[...])
    seen = count_ref[...]
    rank_ref[...] = jnp.where(chosen > 0.0, before + seen[:, 0:1], -1.0)
    count_ref[...] = seen + jnp.sum(chosen, axis=1, keepdims=True)


def _merge(x2d, oa, ob, on_a, on_b, w_out_bf, gate1, shift2, scale2, norm2, wr_hi, wr_lo, rbias, *, tm, seq):
    t, d = x2d.shape
    nb = gate1.shape[0]
    tiles_per_batch = seq // tm
    tri = jnp.asarray(np.triu(np.ones((tm, tm), np.float32), k=1), BF16)

    def mod_map(i):
        return ((i // tiles_per_batch) if nb > 1 else 0, 0, 0)

    return pl.pallas_call(
        _merge_kernel,
        out_shape=[jax.ShapeDtypeStruct((t, d), F32),
                   jax.ShapeDtypeStruct((t, d // 2), jnp.int32),
                   jax.ShapeDtypeStruct((N_EXPERTS, t), F32),
                   jax.ShapeDtypeStruct((N_EXPERTS, t), F32),
                   jax.ShapeDtypeStruct((N_EXPERTS, LANES), F32)],
        grid=(t // tm,),
        in_specs=[pl.BlockSpec((tm, d), lambda i: (i, 0)),
                  pl.BlockSpec((tm, WIDTH_A), lambda i: (i, 0)),
                  pl.BlockSpec((tm, WIDTH_B), lambda i: (i, 0)),
                  pl.BlockSpec((1, WIDTH_A), lambda i: (0, 0)),
                  pl.BlockSpec((1, WIDTH_B), lambda i: (0, 0)),
                  pl.BlockSpec((WIDTH_A + WIDTH_B, d), lambda i: (0, 0)),
                  pl.BlockSpec((1, 1, d), mod_map),
                  pl.BlockSpec((1, 1, d), mod_map),
                  pl.BlockSpec((1, 1, d), mod_map),
                  pl.BlockSpec((1, d), lambda i: (0, 0)),
                  pl.BlockSpec((N_EXPERTS, d), lambda i: (0, 0)),
                  pl.BlockSpec((N_EXPERTS, d), lambda i: (0, 0)),
                  pl.BlockSpec((N_EXPERTS, 1), lambda i: (0, 0)),
                  pl.BlockSpec((tm, tm), lambda i: (0, 0))],
        out_specs=[pl.BlockSpec((tm, d), lambda i: (i, 0)),
                   pl.BlockSpec((tm, d // 2), lambda i: (i, 0)),
                   pl.BlockSpec((N_EXPERTS, tm), lambda i: (0, i)),
                   pl.BlockSpec((N_EXPERTS, tm), lambda i: (0, i)),
                   pl.BlockSpec((N_EXPERTS, LANES), lambda i: (0, 0))],
        compiler_params=_cparams(("arbitrary",)),
        name="merge_route",
    )(x2d, oa, ob, on_a, on_b, w_out_bf, gate1, shift2, scale2, norm2, wr_hi, wr_lo, rbias, tri)


def _first_index_of_max(x, iota):
    mx = jnp.max(x, axis=0, keepdims=True)
    idx = jnp.min(jnp.where(x == mx, iota, float(x.shape[0])), axis=0, keepdims=True)
    return mx, iota == idx


def _router_gates(h, wr_hi, wr_lo, rbias):
    h_hi = h.astype(BF16)
    h_lo = (h - h_hi.astype(F32)).astype(BF16)
    logits = _dot_nt(wr_hi, h_hi) + (_dot_nt(wr_lo, h_hi) + _dot_nt(wr_hi, h_lo))
    scores = _sigmoid(logits)
    sel = scores + rbias
    tm = sel.shape[1]
    iota_g = lax.broadcasted_iota(jnp.int32, (GROUP_SIZE, tm), 0).astype(F32)
    group_scores = []
    for g in range(N_GROUPS):
        grp = sel[g * GROUP_SIZE:(g + 1) * GROUP_SIZE]
        m1, first = _first_index_of_max(grp, iota_g)
        m2 = jnp.max(jnp.where(first, -jnp.inf, grp), axis=0, keepdims=True)
        group_scores.append(m1 + m2)
    gs = jnp.concatenate(group_scores, axis=0)
    iota_n = lax.broadcasted_iota(jnp.int32, (N_GROUPS, tm), 0).astype(F32)
    group_on = jnp.zeros((N_GROUPS, tm), F32)
    for _ in range(TOPK_GROUPS):
        _, pick = _first_index_of_max(gs, iota_n)
        group_on = jnp.where(pick, 1.0, group_on)
        gs = jnp.where(pick, -jnp.inf, gs)
    expert_on = jnp.concatenate(
        [jnp.broadcast_to(group_on[g:g + 1], (GROUP_SIZE, tm)) for g in range(N_GROUPS)], axis=0)
    cand = jnp.where(expert_on > 0.0, sel, -jnp.inf)
    iota_e = lax.broadcasted_iota(jnp.int32, (N_EXPERTS, tm), 0).astype(F32)
    w = jnp.zeros((N_EXPERTS, tm), F32)
    chosen = jnp.zeros((N_EXPERTS, tm), F32)
    for _ in range(TOP_K):
        _, pick = _first_index_of_max(cand, iota_e)
        w = jnp.where(pick, scores, w)
        chosen = jnp.where(pick, 1.0, chosen)
        cand = jnp.where(pick, -jnp.inf, cand)
    return w / jnp.sum(w, axis=0, keepdims=True) * ROUTED_SCALE, chosen


MOE_TS = 1024
MOE_TS_CONTEXT = 256
MOE_ROUTE_TM = 1024
MOE_ROW_TM = 512


def _slots_kernel(gates_ref, rank_ref, off_ref, pos_ref, gtok_ref):
    gates = gates_ref[...]
    rank = rank_ref[...]
    tm = gates.shape[1]
    slot = off_ref[...] + rank
    left = jnp.where(rank >= 0.0, 1.0, 0.0)
    iota_e = lax.broadcasted_iota(jnp.int32, (N_EXPERTS, tm), 0).astype(F32)
    pos_rows, gate_rows = [], []
    for _ in range(TOP_K):
        _, pick = _first_index_of_max(left, iota_e)
        pos_rows.append(jnp.sum(jnp.where(pick, slot, 0.0), axis=0, keepdims=True))
        gate_rows.append(jnp.sum(jnp.where(pick, gates, 0.0), axis=0, keepdims=True))
        left = jnp.where(pick, 0.0, left)
    pos_ref[...] = jnp.concatenate(pos_rows, axis=0).astype(jnp.int32)
    pad = jnp.zeros((LANES - TOP_K, tm), F32)
    gtok_ref[...] = jnp.concatenate(gate_rows + [pad], axis=0).T


def _slots(gates_t, rank_t, off):
    t = gates_t.shape[1]
    tm = MOE_ROUTE_TM
    return pl.pallas_call(
        _slots_kernel,
        out_shape=[jax.ShapeDtypeStruct((TOP_K, t), jnp.int32), jax.ShapeDtypeStruct((t, LANES), F32)],
        grid=(t // tm,),
        in_specs=[pl.BlockSpec((N_EXPERTS, tm), lambda i: (0, i)),
                  pl.BlockSpec((N_EXPERTS, tm), lambda i: (0, i)),
                  pl.BlockSpec((N_EXPERTS, 1), lambda i: (0, 0))],
        out_specs=[pl.BlockSpec((TOP_K, tm), lambda i: (0, i)),
                   pl.BlockSpec((tm, LANES), lambda i: (i, 0))],
        compiler_params=_cparams(("arbitrary",)),
        name="moe_slots",
    )(gates_t, rank_t, off)


SC_CORES = 2
SC_SUBCORES = 16
SC_ROWS = 64


def _dispatch(hp_all, slot_of):
    t, width = hp_all.shape
    n_slots = slot_of.shape[0]
    n_pad = n_slots - TOP_K * t
    workers = SC_CORES * SC_SUBCORES
    per_worker = t // workers
    pad_per_worker = n_pad // workers
    assert per_worker * workers == t and per_worker % SC_ROWS == 0
    assert pad_per_worker * workers == n_pad and pad_per_worker % SC_ROWS == 0
    mesh = plsc.VectorSubcoreMesh(core_axis_name="core", subcore_axis_name="subcore")

    @functools.partial(
        pl.kernel, mesh=mesh,
        out_type=jax.ShapeDtypeStruct((n_slots, width), jnp.int32),
        scratch_types=[pltpu.VMEM((SC_ROWS,), jnp.int32),
                       pltpu.VMEM((SC_ROWS, width), jnp.int32),
                       pltpu.SemaphoreType.DMA],
    )
    def scatter_rows(h_hbm, slot_hbm, out_hbm, idx_v, rows_v, sem):
        worker = lax.axis_index("subcore") * SC_CORES + lax.axis_index("core")
        base = worker * per_worker

        @pl.loop(0, per_worker // SC_ROWS)
        def _(j):
            first = base + j * SC_ROWS
            pltpu.sync_copy(h_hbm.at[pl.ds(first, SC_ROWS)], rows_v)
            for k in range(TOP_K):
                pltpu.sync_copy(slot_hbm.at[pl.ds(k * t + first, SC_ROWS)], idx_v)
                pltpu.async_copy(rows_v, out_hbm.at[idx_v], sem).wait()

        pltpu.sync_copy(h_hbm.at[pl.ds(0, SC_ROWS)], rows_v)
        pad_base = TOP_K * t + worker * pad_per_worker

        @pl.loop(0, pad_per_worker // SC_ROWS)
        def _(j):
            pltpu.sync_copy(slot_hbm.at[pl.ds(pad_base + j * SC_ROWS, SC_ROWS)], idx_v)
            pltpu.async_copy(rows_v, out_hbm.at[idx_v], sem).wait()

    return scatter_rows(hp_all, slot_of)


def _experts_kernel(te_ref, xs_ref, wg_ref, wu_ref, wd_ref, ys_ref, wgu_bf, wd_bf):
    i = pl.program_id(0)

    @pl.when((i == 0) | (te_ref[i] != te_ref[jnp.maximum(i, 1) - 1]))
    def _():
        wgu_bf[:, 0:D_EXPERT] = wg_ref[0].astype(BF16)
        wgu_bf[:, D_EXPERT:2 * D_EXPERT] = wu_ref[0].astype(BF16)
        wd_bf[...] = wd_ref[0].astype(BF16)

    left, right = _unpack_rows(xs_ref[...])
    x = jnp.concatenate([left, right], axis=1).astype(BF16)
    gu = _dot(x, wgu_bf[...])
    g = gu[:, 0:D_EXPERT]
    u = gu[:, D_EXPERT:2 * D_EXPERT]
    act = (g * _sigmoid(g)) * u
    ys_ref[...] = _pack_rows(_dot(act.astype(BF16), wd_bf[...]))


def _experts(xs, tile_expert, w_gate, w_up, w_down, *, ts):
    n_slots, width = xs.shape
    d = 2 * width
    return pl.pallas_call(
        _experts_kernel,
        out_shape=jax.ShapeDtypeStruct((n_slots, width), jnp.int32),
        grid_spec=pltpu.PrefetchScalarGridSpec(
            num_scalar_prefetch=1,
            grid=(n_slots // ts,),
            in_specs=[pl.BlockSpec((ts, width), lambda i, te: (i, 0)),
                      pl.BlockSpec((1, d, D_EXPERT), lambda i, te: (te[i], 0, 0)),
                      pl.BlockSpec((1, d, D_EXPERT), lambda i, te: (te[i], 0, 0)),
                      pl.BlockSpec((1, D_EXPERT, d), lambda i, te: (te[i], 0, 0))],
            out_specs=pl.BlockSpec((ts, width), lambda i, te: (i, 0)),
            scratch_shapes=[pltpu.VMEM((d, 2 * D_EXPERT), BF16), pltpu.VMEM((D_EXPERT, d), BF16)]),
        compiler_params=_cparams(("arbitrary",)),
        name="moe_experts",
    )(tile_expert, xs, w_gate, w_up, w_down)


def _gather_slots(y_slots, slot_of, t):
    width = y_slots.shape[1]
    workers = SC_CORES * SC_SUBCORES
    per_worker = t // workers
    n_blocks = (per_worker // SC_ROWS) * TOP_K
    assert per_worker * workers == t and per_worker % SC_ROWS == 0 and n_blocks % 2 == 0
    mesh = plsc.VectorSubcoreMesh(core_axis_name="core", subcore_axis_name="subcore")

    @functools.partial(
        pl.kernel, mesh=mesh,
        out_type=jax.ShapeDtypeStruct((TOP_K * t, width), jnp.int32),
        scratch_types=[pltpu.VMEM((SC_ROWS,), jnp.int32), pltpu.VMEM((SC_ROWS,), jnp.int32),
                       pltpu.VMEM((SC_ROWS, width), jnp.int32), pltpu.VMEM((SC_ROWS, width), jnp.int32),
                       pltpu.SemaphoreType.DMA, pltpu.SemaphoreType.DMA],
    )
    def gather_rows(ys_hbm, slot_hbm, out_hbm, idx0, idx1, rows0, rows1, sem0, sem1):
        worker = lax.axis_index("subcore") * SC_CORES + lax.axis_index("core")
        base = worker * per_worker

        def first_row(n):
            return (n % TOP_K) * t + base + (n // TOP_K) * SC_ROWS

        def start(n, idx_v, rows_v, sem):
            pltpu.sync_copy(slot_hbm.at[pl.ds(first_row(n), SC_ROWS)], idx_v)
            pltpu.async_copy(ys_hbm.at[idx_v], rows_v, sem)

        def finish(n, idx_v, rows_v, sem):
            pltpu.make_async_copy(ys_hbm.at[idx_v], rows_v, sem).wait()
            pltpu.sync_copy(rows_v, out_hbm.at[pl.ds(first_row(n), SC_ROWS)])

        start(0, idx0, rows0, sem0)

        @pl.loop(0, n_blocks, step=2)
        def _(n):
            start(n + 1, idx1, rows1, sem1)
            finish(n, idx0, rows0, sem0)

            @pl.when(n + 2 < n_blocks)
            def _():
                start(n + 2, idx0, rows0, sem0)

            finish(n + 1, idx1, rows1, sem1)

    return gather_rows(y_slots, slot_of)


def _combine_kernel(y_ref, h_ref, g2_ref, gtok_ref, rows_ref, wgs_ref, wus_ref, wds_ref, o_ref):
    h_left, h_right = _unpack_rows(h_ref[...])
    h = jnp.concatenate([h_left, h_right], axis=1).astype(BF16)
    gs = _dot(h, wgs_ref[...])
    us = _dot(h, wus_ref[...])
    shared = _dot(((gs * _sigmoid(gs)) * us).astype(BF16), wds_ref[...])

    gtok = gtok_ref[...]
    acc_left = acc_right = None
    for k in range(TOP_K):
        left, right = _unpack_rows(rows_ref[k])
        gate = gtok[:, k:k + 1]
        acc_left = gate * left if acc_left is None else acc_left + gate * left
        acc_right = gate * right if acc_right is None else acc_right + gate * right
    routed = jnp.concatenate([acc_left, acc_right], axis=1)
    o_ref[...] = y_ref[...] + g2_ref[0] * (routed + shared)


def _combine(y_all, hp_all, gate2, gtok, rows, wgs, wus, wds, *, first_token, tokens, seq):
    d = y_all.shape[1]
    width = hp_all.shape[1]
    tm = MOE_ROW_TM
    tile0 = first_token // tm
    nb = gate2.shape[0]
    tiles_per_batch = seq // tm

    def mod_map(i):
        return ((i // tiles_per_batch) if nb > 1 else 0, 0, 0)

    return pl.pallas_call(
        _combine_kernel,
        out_shape=jax.ShapeDtypeStruct((tokens, d), F32),
        grid=(tokens // tm,),
        in_specs=[pl.BlockSpec((tm, d), lambda i: (tile0 + i, 0)),
                  pl.BlockSpec((tm, width), lambda i: (tile0 + i, 0)),
                  pl.BlockSpec((1, 1, d), mod_map),
                  pl.BlockSpec((tm, LANES), lambda i: (tile0 + i, 0)),
                  pl.BlockSpec((TOP_K, tm, width), lambda i: (0, tile0 + i, 0)),
                  pl.BlockSpec((d, D_SHARED), lambda i: (0, 0)),
                  pl.BlockSpec((d, D_SHARED), lambda i: (0, 0)),
                  pl.BlockSpec((D_SHARED, d), lambda i: (0, 0))],
        out_specs=pl.BlockSpec((tm, d), lambda i: (i, 0)),
        compiler_params=_cparams(("arbitrary",)),
        name="moe_combine",
    )(y_all, hp_all, gate2, gtok, rows, wgs, wus, wds)


def _expert_layout(counts, n_tiles, ts):
    cnt = counts.astype(jnp.int32)
    tiles = (cnt + (ts - 1)) // ts
    last_tile = jnp.cumsum(tiles)
    off = (last_tile - tiles) * ts
    pad_lo = off + cnt
    pad_hi = (off + tiles * ts).at[N_EXPERTS - 1].set(n_tiles * ts)
    pad_cnt = pad_hi - pad_lo
    pad_last = jnp.cumsum(pad_cnt)
    shift = pad_lo - (pad_last - pad_cnt)
    j = jnp.arange(N_EXPERTS * ts, dtype=jnp.int32)
    past = (pad_last[None, :-1] <= j[:, None]).astype(jnp.int32)
    pad_slots = j + shift[0] + jnp.sum(past * (shift[1:] - shift[:-1])[None, :], axis=1)
    tile_ids = jnp.arange(n_tiles, dtype=jnp.int32)
    tile_expert = jnp.minimum(
        jnp.sum((last_tile[None, :] <= tile_ids[:, None]).astype(jnp.int32), axis=1), N_EXPERTS - 1)
    return off, pad_slots, tile_expert


def _rope_tables(n_tokens):
    t = jnp.arange(n_tokens)
    row = (t // GRID_W).astype(F32)
    col = (t % GRID_W).astype(F32)
    nf = HEAD_DIM // 4
    freqs = ROPE_THETA ** (-jnp.arange(nf, dtype=F32) / nf)
    ang_r = row[:, None] * freqs
    ang_c = col[:, None] * freqs
    cos = jnp.concatenate([jnp.cos(ang_r)] * 2 + [jnp.cos(ang_c)] * 2, axis=1)
    sin = jnp.concatenate([-jnp.sin(ang_r), jnp.sin(ang_r), -jnp.sin(ang_c), jnp.sin(ang_c)], axis=1)
    reps = LANES // HEAD_DIM
    return jnp.tile(cos, (1, reps)), jnp.tile(sin, (1, reps))


def _head_gains(qn_a, kn_a, qn_b, kn_b):
    ones = jnp.ones((HEAD_DIM,), F32)
    parts = ([qn_a] * N_HEADS_A + [kn_a] * N_KV_A + [ones] * N_KV_A
             + [qn_b] * N_HEADS_B + [kn_b] * N_HEADS_B + [ones] * N_HEADS_B)
    return jnp.concatenate(parts).reshape(1, IN_COLS).astype(F32)


def _same_head_indicator():
    i = np.arange(MXU_DIM)
    return jnp.asarray((i[:, None] // HEAD_DIM) == (i[None, :] // HEAD_DIM), BF16)


def _token_major(cache):
    b, h, s, hd = cache.shape
    return cache.transpose(0, 2, 1, 3).reshape(b, s, h * hd).astype(BF16)


def kernel(x_prompt, x_sample, cache_k_a, cache_v_a, cache_k_b, cache_v_b, c, c_ctx, w_mod, b_mod, norm1, norm2, w_in, qn_a, kn_a, qn_b, kn_b, rpb, on_a, on_b, w_out, w_router, router_bias, w_gate_e, w_up_e, w_down_e, w_gate_s, w_up_s, w_down_s):
    depth = w_mod.shape[0]
    assert depth == 1
    l = 0
    bp, sp, d = x_prompt.shape
    bs, ss, _ = x_sample.shape

    cvec = jnp.concatenate([c_ctx[None, :], c], axis=0)
    rows = -(-cvec.shape[0] // 8) * 8
    cvec = jnp.pad(cvec, ((0, rows - cvec.shape[0]), (0, 0)))
    mod = _adaln(cvec, w_mod[l], b_mod[l])
    mod_p = [m.reshape(1, 1, d) for m in jnp.split(mod[0:1], 6, axis=-1)]
    mod_s = [m.reshape(bs, 1, d) for m in jnp.split(mod[1:1 + bs], 6, axis=-1)]
    mod_all = [m.reshape(1 + bs, 1, d) for m in jnp.split(mod[0:1 + bs], 6, axis=-1)]

    w_in_bf = w_in[l].astype(BF16)
    w_out_bf = w_out[l].astype(BF16)
    gain = _head_gains(qn_a[l], kn_a[l], qn_b[l], kn_b[l])
    seg = _same_head_indicator()
    n1 = norm1[l].reshape(1, d)
    n2 = norm2[l].reshape(1, d)
    ona = on_a[l].reshape(1, WIDTH_A)
    onb = on_b[l].reshape(1, WIDTH_B)
    wr_t = w_router[l].T
    wr_hi = wr_t.astype(BF16)
    wr_lo = (wr_t - wr_hi.astype(F32)).astype(BF16)
    rbias = router_bias[l].reshape(N_EXPERTS, 1).astype(F32)
    wgs = w_gate_s[l].astype(BF16)
    wus = w_up_s[l].astype(BF16)
    wds = w_down_s[l].astype(BF16)
    t_p = bp * sp
    t_s = bs * ss
    t_all = t_p + t_s

    xp = x_prompt.reshape(t_p, d)
    proj_p, st_ka, st_va, st_kb, st_vb = _project(
        xp, mod_p[0], mod_p[1], n1, w_in_bf, gain, seg, None, tm=sp, seq=sp, states=True)
    oa_p, ob_p = _context_attention(proj_p, seq=sp)

    xs = x_sample.reshape(t_s, d)
    proj_s, = _project(xs, mod_s[0], mod_s[1], n1, w_in_bf, gain, seg, _rope_tables(ss),
                       tm=1024, seq=ss, states=False)
    oa_s = _latent_gqa(proj_s, _token_major(cache_k_a[:, l]), _token_major(cache_v_a[:, l]),
                       _query_norm_bound(qn_a[l]), seq=ss, tq=256)
    bias_t = _neighbourhood_bias(rpb[l], ss // GRID_W)
    ob_s = _latent_neighbourhood(proj_s, _token_major(cache_k_b[:, l]), _token_major(cache_v_b[:, l]),
                                 bias_t, _neighbourhood_bounds(qn_b[l], kn_b[l], rpb[l]), seq=ss)

    def moe_tail(x2d, oa, ob, mods, seq, ts):
        t = x2d.shape[0]
        y1, hp, gates_t, rank_t, counts = _merge(x2d, oa, ob, ona, onb, w_out_bf, mods[2], mods[3], mods[4], n2,
                                                 wr_hi, wr_lo, rbias, tm=512, seq=seq)
        n_tiles = t * TOP_K // ts + N_EXPERTS
        off, pad_slots, tile_expert = _expert_layout(counts[:, 0], n_tiles, ts)
        pos, gtok = _slots(gates_t, rank_t, off.astype(F32).reshape(N_EXPERTS, 1))
        slot_of = pos.reshape(TOP_K * t)
        x_slots = _dispatch(hp, jnp.concatenate([slot_of, pad_slots]))
        y_slots = _experts(x_slots, tile_expert, w_gate_e[l], w_up_e[l], w_down_e[l], ts=ts)
        rows = _gather_slots(y_slots, slot_of, t).reshape(TOP_K, t, d // 2)
        return _combine(y1, hp, mods[5], gtok, rows, wgs, wus, wds, first_token=0, tokens=t, seq=seq)

    y_p = moe_tail(xp, oa_p, ob_p, mod_p, sp, MOE_TS_CONTEXT)
    y_s = moe_tail(xs, oa_s, ob_s, mod_s, ss, MOE_TS)

    return (y_p.reshape(bp, sp, d), y_s.reshape(bs, ss, d), st_ka, st_va, st_kb, st_vb)
```

```python
import functools

import numpy as np
import jax
import jax.numpy as jnp
from jax import lax
from jax.experimental import pallas as pl
from jax.experimental.pallas import tpu as pltpu
from jax.experimental.pallas import tpu_sc as plsc

F32 = jnp.float32
BF16 = jnp.bfloat16

D_MODEL = 1024
HEAD_DIM = 64
N_HEADS_A = 8
N_KV_A = 2
GROUP_A = N_HEADS_A // N_KV_A
N_HEADS_B = 8
WIDTH_A = N_HEADS_A * HEAD_DIM
WIDTH_B = N_HEADS_B * HEAD_DIM
KV_WIDTH_A = N_KV_A * HEAD_DIM
IN_COLS = WIDTH_A + 2 * KV_WIDTH_A + 3 * WIDTH_B
GRID_W = 64
ROPE_THETA = 10000.0
NA_KH = 8
NA_KW = 16
N_EXPERTS = 64
N_GROUPS = 8
GROUP_SIZE = N_EXPERTS // N_GROUPS
TOPK_GROUPS = 4
TOP_K = 8
D_EXPERT = 256
D_SHARED = 256
ROUTED_SCALE = 2.5
EPS = 1e-6

LANES = 128
MXU_DIM = 256
MASKED = -1e30

COL_QA = 0
COL_KA = WIDTH_A
COL_VA = COL_KA + KV_WIDTH_A
COL_QB = COL_VA + KV_WIDTH_A
COL_KB = COL_QB + WIDTH_B
COL_VB = COL_KB + WIDTH_B

NA_QROWS = 8
NA_KROWS = 2 * NA_KH
NA_TQ = NA_QROWS * GRID_W
NA_TK = NA_KROWS * GRID_W
NA_KBLK = 256

VMEM_LIMIT = 56 * 1024 * 1024


def _cparams(sem):
    return pltpu.CompilerParams(dimension_semantics=sem, vmem_limit_bytes=VMEM_LIMIT)


def _dot(a, b):
    return jnp.dot(a, b, preferred_element_type=F32)


def _dot_nt(a, b):
    return lax.dot_general(a, b, (((1,), (1,)), ((), ())), preferred_element_type=F32)


def _sigmoid(x):
    return 1.0 / (1.0 + jnp.exp(-x))


def _rms(x):
    return x * lax.rsqrt(jnp.mean(x * x, axis=-1, keepdims=True) + EPS)


def _pack_rows(x):
    n = x.shape[1] // 2
    hi = lax.bitcast_convert_type(x[:, :n].astype(BF16).astype(F32), jnp.int32)
    lo = lax.bitcast_convert_type(x[:, n:].astype(BF16).astype(F32), jnp.int32)
    return hi | lax.shift_right_logical(lo, 16)


def _unpack_rows(w):
    left = lax.bitcast_convert_type(w & jnp.int32(-65536), F32)
    right = lax.bitcast_convert_type(lax.shift_left(w, 16), F32)
    return left, right


def _mod_kernel(c_ref, w_ref, b_ref, o_ref):
    c = c_ref[...]
    s = c * _sigmoid(c)
    o_ref[...] = jnp.dot(s, w_ref[...], preferred_element_type=F32,
                         precision=lax.Precision.HIGHEST) + b_ref[...]


def _adaln(cvec, w_mod, b_mod):
    rows, d = cvec.shape
    n = w_mod.shape[1]
    tn = 512
    return pl.pallas_call(
        _mod_kernel,
        out_shape=jax.ShapeDtypeStruct((rows, n), F32),
        grid=(n // tn,),
        in_specs=[pl.BlockSpec((rows, d), lambda j: (0, 0)),
                  pl.BlockSpec((d, tn), lambda j: (0, j)),
                  pl.BlockSpec((1, tn), lambda j: (0, j))],
        out_specs=pl.BlockSpec((rows, tn), lambda j: (0, j)),
        compiler_params=_cparams(("arbitrary",)),
        name="adaln_mod",
    )(cvec, w_mod, b_mod.reshape(1, n))


def _proj_chunks():
    def split(c0, width, step, *flags):
        return [(c0 + i, min(step, width - i)) + flags for i in range(0, width, step)]
    return (split(COL_QA, WIDTH_A, MXU_DIM, True, True, True)
            + split(COL_KA, KV_WIDTH_A, MXU_DIM, True, True, False)
            + split(COL_VA, KV_WIDTH_A, MXU_DIM, False, False, False)
            + split(COL_QB, WIDTH_B, MXU_DIM, True, False, True)
            + split(COL_KB, WIDTH_B, MXU_DIM, True, False, False)
            + split(COL_VB, WIDTH_B, MXU_DIM, False, False, False))


_PROJ_CHUNKS = _proj_chunks()


def _proj_kernel(*refs, rope, states):
    x_ref, sh_ref, sc_ref, n1_ref, w_ref, gain_ref, seg_ref = refs[:7]
    pos = 7
    if rope:
        cos_ref, sin_ref = refs[pos:pos + 2]
        pos += 2
    out_ref = refs[pos]
    pos += 1
    if states:
        ka_ref, va_ref, kb_ref, vb_ref = refs[pos:pos + 4]
        state_of = {COL_KA: ka_ref, COL_VA: va_ref, COL_KB: kb_ref, COL_VB: vb_ref}

    x = x_ref[...]
    h = _rms(x) * n1_ref[...]
    h = h * (1.0 + sc_ref[0]) + sh_ref[0]
    p = _dot(h.astype(BF16), w_ref[...])

    for c0, w, normed, roped, is_query in _PROJ_CHUNKS:
        pc = p[:, c0:c0 + w]
        if normed:
            seg = seg_ref[0:w, 0:w]
            sq = pc * pc
            hi = sq.astype(BF16)
            lo = (sq - hi.astype(F32)).astype(BF16)
            ss = _dot(hi, seg) + _dot(lo, seg)
            pc = pc * lax.rsqrt(ss * (1.0 / HEAD_DIM) + EPS) * gain_ref[:, c0:c0 + w]
        if states:
            for start, ref in state_of.items():
                if start <= c0 < start + ref.shape[2] * HEAD_DIM:
                    base = (c0 - start) // HEAD_DIM
                    for hh in range(w // HEAD_DIM):
                        ref[0, 0, base + hh] = pc[:, hh * HEAD_DIM:(hh + 1) * HEAD_DIM]
        if rope and roped:
            reps = w // LANES
            cos = jnp.concatenate([cos_ref[...]] * reps, axis=1) if reps > 1 else cos_ref[...]
            sin = jnp.concatenate([sin_ref[...]] * reps, axis=1) if reps > 1 else sin_ref[...]
            lane = lax.broadcasted_iota(jnp.int32, pc.shape, 1)
            first_half = (lane % (HEAD_DIM // 2)) < (HEAD_DIM // 4)
            partner = jnp.where(first_half,
                                pltpu.roll(pc, w - HEAD_DIM // 4, 1),
                                pltpu.roll(pc, HEAD_DIM // 4, 1))
            pc = pc * cos + partner * sin
        if is_query:
            pc = pc * (HEAD_DIM ** -0.5)
        out_ref[:, c0:c0 + w] = pc.astype(BF16)


def _project(x2d, shift, scale, norm1, w_in_bf, gain, seg, rope_tabs, *, tm, seq, states):
    t, d = x2d.shape
    nb = shift.shape[0]
    tiles_per_batch = seq // tm
    rope = rope_tabs is not None

    def mod_map(i):
        return ((i // tiles_per_batch) if nb > 1 else 0, 0, 0)

    in_specs = [pl.BlockSpec((tm, d), lambda i: (i, 0)),
                pl.BlockSpec((1, 1, d), mod_map),
                pl.BlockSpec((1, 1, d), mod_map),
                pl.BlockSpec((1, d), lambda i: (0, 0)),
                pl.BlockSpec((d, IN_COLS), lambda i: (0, 0)),
                pl.BlockSpec((1, IN_COLS), lambda i: (0, 0)),
                pl.BlockSpec((MXU_DIM, MXU_DIM), lambda i: (0, 0))]
    args = [x2d, shift, scale, norm1, w_in_bf, gain, seg]
    if rope:
        in_specs += [pl.BlockSpec((tm, LANES), lambda i: (i % tiles_per_batch, 0))] * 2
        args += list(rope_tabs)
    out_shape = [jax.ShapeDtypeStruct((t, IN_COLS), BF16)]
    out_specs = [pl.BlockSpec((tm, IN_COLS), lambda i: (i, 0))]
    if states:
        assert tm == seq
        b = t // seq
        for nh in (N_KV_A, N_KV_A, N_HEADS_B, N_HEADS_B):
            out_shape.append(jax.ShapeDtypeStruct((b, 1, nh, seq, HEAD_DIM), F32))
            out_specs.append(pl.BlockSpec((1, 1, nh, seq, HEAD_DIM), lambda i: (i, 0, 0, 0, 0)))
    return pl.pallas_call(
        functools.partial(_proj_kernel, rope=rope, states=states),
        out_shape=out_shape,
        grid=(t // tm,),
        in_specs=in_specs,
        out_specs=out_specs,
        compiler_params=_cparams(("arbitrary",)),
        name="proj_states" if states else "proj_rope",
    )(*args)


def _lane_half(shape):
    return lax.broadcasted_iota(jnp.int32, shape, 1) // HEAD_DIM


def _keep_half(x, half):
    return jnp.where(_lane_half(x.shape) == half, x, jnp.zeros_like(x))


def _transpose_bf16(x):
    return x.astype(F32).T.astype(BF16)


def _attend(q, keys, values_t, biases):
    return _softmax_av(_scores(q, keys, biases), values_t)


def _scores(q, keys, biases):
    scores = []
    for k, b in zip(keys, biases):
        s = _dot_nt(k, q)
        if b is not None:
            s = s + b
        scores.append(s)
    return scores


def _softmax_av(scores, values_t):
    m = functools.reduce(jnp.maximum, [jnp.max(s, axis=0, keepdims=True) for s in scores])
    denom = None
    out = None
    for s, vt in zip(scores, values_t):
        p = jnp.exp(s - m)
        ps = jnp.sum(p, axis=0, keepdims=True)
        po = _dot(vt, p.astype(BF16))
        denom = ps if denom is None else denom + ps
        out = po if out is None else out + po
    return out / denom


def _swap_halves(q_bf16):
    return pltpu.roll(q_bf16.astype(F32), HEAD_DIM, 1).astype(BF16)


def _gqa_heads(q_of_pair, keys_by_group, values_t):
    outs = []
    for h in range(N_HEADS_A):
        g = h // GROUP_A
        q = q_of_pair(h // 2)
        if h % 2 != g:
            q = _swap_halves(q)
        o = _attend(q, keys_by_group[g], values_t, [None] * len(values_t))
        outs.append(o[g * HEAD_DIM:(g + 1) * HEAD_DIM])
    return jnp.concatenate(outs, axis=0)


def _ctx_attn_kernel(p_ref, oa_ref, ob_ref):
    ka = p_ref[:, COL_KA:COL_KA + LANES]
    va_t = [_transpose_bf16(p_ref[:, COL_VA:COL_VA + LANES])]
    keys_by_group = [[_keep_half(ka, g)] for g in range(N_KV_A)]
    oa = _gqa_heads(lambda i: p_ref[:, COL_QA + i * LANES:COL_QA + (i + 1) * LANES],
                    keys_by_group, va_t)
    oa_ref[...] = oa.T

    outs = []
    for i in range(N_HEADS_B // 2):
        q = p_ref[:, COL_QB + i * LANES:COL_QB + (i + 1) * LANES]
        k = p_ref[:, COL_KB + i * LANES:COL_KB + (i + 1) * LANES]
        vt = [_transpose_bf16(p_ref[:, COL_VB + i * LANES:COL_VB + (i + 1) * LANES])]
        for half in range(2):
            o = _attend(q, [_keep_half(k, half)], vt, [None])
            outs.append(o[half * HEAD_DIM:(half + 1) * HEAD_DIM])
    ob_ref[...] = jnp.concatenate(outs, axis=0).T


def _context_attention(proj, *, seq):
    t = proj.shape[0]
    return pl.pallas_call(
        _ctx_attn_kernel,
        out_shape=[jax.ShapeDtypeStruct((t, WIDTH_A), F32), jax.ShapeDtypeStruct((t, WIDTH_B), F32)],
        grid=(t // seq,),
        in_specs=[pl.BlockSpec((seq, IN_COLS), lambda i: (i, 0))],
        out_specs=[pl.BlockSpec((seq, WIDTH_A), lambda i: (i, 0)),
                   pl.BlockSpec((seq, WIDTH_B), lambda i: (i, 0))],
        compiler_params=_cparams(("arbitrary",)),
        name="context_attention",
    )(proj)


ATTN_SAFE_SHIFT = 40.0
ONES_ROWS = 16


def _round_up_bf16(x):
    return (x * (1.0 + 2.0 ** -6)).astype(BF16).astype(F32)


def _query_norm_bound(gain):
    return jnp.max(jnp.abs(gain)).reshape(1, 1).astype(F32)


def _ones_lane(g):
    return (1 - g) * HEAD_DIM


def _gqa_latent_kernel(q_ref, k_ref, v_ref, ck_ref, cv_ref, qmax_ref, o_ref,
                       kg_ref, ckg_ref, vt_ref, cvt_ref, shift_ref):
    lane_k = lax.broadcasted_iota(jnp.int32, (1, LANES), 1)

    @pl.when(pl.program_id(1) == 0)
    def _():
        k = k_ref[...]
        ck = ck_ref[0]
        vt = v_ref[...].astype(F32).T
        cvt = cv_ref[0].astype(F32).T
        for g in range(N_KV_A):
            kf = _keep_half(k, g).astype(F32)
            ckf = _keep_half(ck, g).astype(F32)
            ksq = jnp.maximum(jnp.max(jnp.sum(kf * kf, axis=1, keepdims=True), axis=0, keepdims=True),
                              jnp.max(jnp.sum(ckf * ckf, axis=1, keepdims=True), axis=0, keepdims=True))
            shift_ref[g] = jnp.broadcast_to(_round_up_bf16(qmax_ref[...] * jnp.sqrt(ksq)), shift_ref.shape[1:])
            kg_ref[g] = jnp.where(lane_k == _ones_lane(g), 1.0, kf).astype(BF16)
            ckg_ref[g] = jnp.where(lane_k == _ones_lane(g), 1.0, ckf).astype(BF16)
            rows = slice(g * HEAD_DIM, (g + 1) * HEAD_DIM)
            vt_ref[g] = jnp.concatenate([vt[rows], jnp.ones((ONES_ROWS, vt.shape[1]), F32)], axis=0).astype(BF16)
            cvt_ref[g] = jnp.concatenate([cvt[rows], jnp.ones((ONES_ROWS, cvt.shape[1]), F32)], axis=0).astype(BF16)

    tq = q_ref.shape[0]
    lane_q = lax.broadcasted_iota(jnp.int32, (GROUP_A * tq, LANES), 1)
    queries, shifts = [], []
    for g in range(N_KV_A):
        qs = []
        for j in range(GROUP_A):
            h = g * GROUP_A + j
            q = q_ref[:, (h // 2) * LANES:(h // 2 + 1) * LANES].astype(F32)
            qs.append(q if h % 2 == g else pltpu.roll(q, HEAD_DIM, 1))
        queries.append(jnp.where(lane_q // HEAD_DIM == g, jnp.concatenate(qs, axis=0), 0.0))
        shifts.append(shift_ref[g][0:1, 0:1])
    safe = jnp.max(jnp.maximum(shift_ref[0], shift_ref[1])) <= ATTN_SAFE_SHIFT

    def attend(g, p_lat, p_ctx):
        half = p_lat.shape[1] // 2
        o = jnp.concatenate([_dot(vt_ref[g], p_lat[:, :half]) + _dot(cvt_ref[g], p_ctx[:, :half]),
                             _dot(vt_ref[g], p_lat[:, half:]) + _dot(cvt_ref[g], p_ctx[:, half:])],
                            axis=1)
        o = o[:HEAD_DIM] / o[HEAD_DIM:HEAD_DIM + 1]
        heads = jnp.concatenate([o[:, j * tq:(j + 1) * tq] for j in range(GROUP_A)], axis=0)
        o_ref[:, g * GROUP_A * HEAD_DIM:(g + 1) * GROUP_A * HEAD_DIM] = heads.T

    def with_bound():
        for g in range(N_KV_A):
            qa = jnp.where(lane_q == _ones_lane(g), -shifts[g], queries[g]).astype(BF16)
            attend(g, jnp.exp(_dot_nt(kg_ref[g], qa)).astype(BF16), jnp.exp(_dot_nt(ckg_ref[g], qa)).astype(BF16))

    def with_row_max():
        for g in range(N_KV_A):
            qa = queries[g].astype(BF16)
            s_lat = _dot_nt(kg_ref[g], qa)
            s_ctx = _dot_nt(ckg_ref[g], qa)
            m = jnp.maximum(jnp.max(s_lat, axis=0, keepdims=True), jnp.max(s_ctx, axis=0, keepdims=True))
            attend(g, jnp.exp(s_lat - m).astype(BF16), jnp.exp(s_ctx - m).astype(BF16))

    pl.when(safe)(with_bound)
    pl.when(jnp.logical_not(safe))(with_row_max)


def _latent_gqa(proj, ctx_k, ctx_v, qmax, *, seq, tq):
    t = proj.shape[0]
    b = t // seq
    nq = seq // tq
    past = ctx_k.shape[1]
    return pl.pallas_call(
        _gqa_latent_kernel,
        out_shape=jax.ShapeDtypeStruct((t, WIDTH_A), F32),
        grid=(b, nq),
        in_specs=[pl.BlockSpec((tq, WIDTH_A), lambda bi, qi: (bi * nq + qi, 0)),
                  pl.BlockSpec((seq, LANES), lambda bi, qi: (bi, COL_KA // LANES)),
                  pl.BlockSpec((seq, LANES), lambda bi, qi: (bi, COL_VA // LANES)),
                  pl.BlockSpec((1, past, LANES), lambda bi, qi: (bi, 0, 0)),
                  pl.BlockSpec((1, past, LANES), lambda bi, qi: (bi, 0, 0)),
                  pl.BlockSpec((1, 1), lambda bi, qi: (0, 0))],
        out_specs=pl.BlockSpec((tq, WIDTH_A), lambda bi, qi: (bi * nq + qi, 0)),
        scratch_shapes=[pltpu.VMEM((N_KV_A, seq, LANES), BF16),
                        pltpu.VMEM((N_KV_A, past, LANES), BF16),
                        pltpu.VMEM((N_KV_A, HEAD_DIM + ONES_ROWS, seq), BF16),
                        pltpu.VMEM((N_KV_A, HEAD_DIM + ONES_ROWS, past), BF16),
                        pltpu.VMEM((N_KV_A, 8, LANES), F32)],
        compiler_params=_cparams(("arbitrary", "arbitrary")),
        name="latent_gqa",
    )(proj, proj, proj, ctx_k, ctx_v, qmax)


def _na_kernel(q_ref, k_ref, v_ref, ck_ref, cv_ref, bias_ref, bound_ref, o_ref, keys_ref, vt_ref, shift_ref,
               *, rows):
    i = pl.program_id(2)
    n_kblk = k_ref.shape[0] // NA_KBLK
    lane_k = lax.broadcasted_iota(jnp.int32, (1, LANES), 1)
    one = jnp.ones((), BF16)

    @pl.when(i == 0)
    def _():
        k = k_ref[...]
        ck = ck_ref[0]
        for half in range(2):
            kh = jnp.where(lane_k == _ones_lane(half), one, _keep_half(k, half))
            keys_ref[half, 0:n_kblk] = kh.reshape(n_kblk, NA_KBLK, LANES)
            ckh = _keep_half(ck, half)
            keys_ref[half, n_kblk] = jnp.where(lane_k == _ones_lane(half), one, ckh)
            ckf = ckh.astype(F32)
            ctx_norm = jnp.sqrt(jnp.max(jnp.sum(ckf * ckf, axis=1, keepdims=True), axis=0, keepdims=True))
            consts = bound_ref[0, half:half + 1, :]
            kmax = jnp.maximum(ctx_norm, consts[:, 2:3])
            shift_ref[half] = jnp.broadcast_to(_round_up_bf16(consts[:, 0:1] * kmax + consts[:, 1:2]),
                                               shift_ref.shape[1:])
        ones_rows = jnp.ones((ONES_ROWS, NA_KBLK), F32)
        vt = v_ref[...].astype(F32).T
        for j in range(n_kblk):
            vt_ref[j] = jnp.concatenate([vt[:, j * NA_KBLK:(j + 1) * NA_KBLK], ones_rows], axis=0).astype(BF16)
        vt_ref[n_kblk] = jnp.concatenate([cv_ref[0].astype(F32).T, ones_rows], axis=0).astype(BF16)

    q = q_ref[...]
    lane_q = lax.broadcasted_iota(jnp.int32, q.shape, 1)
    first = _na_first_key_block(i, rows)
    n_qblk = rows // NA_QROWS
    variant = jnp.where(i == 0, 0, jnp.where(i == n_qblk - 1, 2, 1))
    blocks = [first + j for j in range(NA_TK // NA_KBLK)] + [n_kblk]
    values_t = [vt_ref[blk] for blk in blocks]
    heads = []
    for half in range(2):
        keys = [keys_ref[half, blk] for blk in blocks]
        biases = [bias_ref[variant, half, j * NA_KBLK:(j + 1) * NA_KBLK, :] for j in range(NA_TK // NA_KBLK)] + [None]
        heads.append((shift_ref[half][0:1, 0:1], keys, biases))
    safe = jnp.max(jnp.maximum(shift_ref[0], shift_ref[1])) <= ATTN_SAFE_SHIFT

    def attend(probabilities):
        outs = []
        for half, ps in enumerate(probabilities):
            o = functools.reduce(lambda a, b: a + b, [_dot(vt, p) for vt, p in zip(values_t, ps)])
            outs.append(o[half * HEAD_DIM:(half + 1) * HEAD_DIM] / o[2 * HEAD_DIM:2 * HEAD_DIM + 1])
        o_ref[...] = jnp.concatenate(outs, axis=0).T

    def with_bound():
        probabilities = []
        for half, (shift, keys, biases) in enumerate(heads):
            qa = jnp.where(lane_q == _ones_lane(half), (-shift).astype(BF16), _keep_half(q, half))
            probabilities.append([jnp.exp(s).astype(BF16) for s in _scores(qa, keys, biases)])
        attend(probabilities)

    def with_row_max():
        all_scores = [_scores(_keep_half(q, half), keys, biases) for half, (_, keys, biases) in enumerate(heads)]
        probabilities = []
        for scores in all_scores:
            m = functools.reduce(jnp.maximum, [jnp.max(s, axis=0, keepdims=True) for s in scores])
            probabilities.append([jnp.exp(s - m).astype(BF16) for s in scores])
        attend(probabilities)

    pl.when(safe)(with_bound)
    pl.when(jnp.logical_not(safe))(with_row_max)


def _na_first_key_block(i, rows):
    per_qblock = NA_QROWS * GRID_W // NA_KBLK
    lead = (NA_KH // 2) * GRID_W // NA_KBLK
    return jnp.clip(per_qblock * i - lead, 0, (rows - NA_KROWS) * GRID_W // NA_KBLK)


def _latent_neighbourhood(proj, ctx_k, ctx_v, bias_t, bounds, *, seq):
    t = proj.shape[0]
    b = t // seq
    rows = seq // GRID_W
    nblk = rows // NA_QROWS
    n_kblk = seq // NA_KBLK
    past = ctx_k.shape[1]
    assert past == NA_KBLK
    grid = (N_HEADS_B // 2, b, nblk)
    in_specs = [pl.BlockSpec((NA_TQ, LANES), lambda hp, bi, i: (bi * nblk + i, COL_QB // LANES + hp)),
                pl.BlockSpec((seq, LANES), lambda hp, bi, i: (bi, COL_KB // LANES + hp)),
                pl.BlockSpec((seq, LANES), lambda hp, bi, i: (bi, COL_VB // LANES + hp)),
                pl.BlockSpec((1, past, LANES), lambda hp, bi, i: (bi, 0, hp)),
                pl.BlockSpec((1, past, LANES), lambda hp, bi, i: (bi, 0, hp)),
                pl.BlockSpec((3, 2, NA_TK, NA_TQ), lambda hp, bi, i: (0, hp, 0, 0)),
                pl.BlockSpec((1, 2, LANES), lambda hp, bi, i: (hp, 0, 0))]
    return pl.pallas_call(
        functools.partial(_na_kernel, rows=rows),
        out_shape=jax.ShapeDtypeStruct((t, WIDTH_B), F32),
        grid=grid,
        in_specs=in_specs,
        out_specs=pl.BlockSpec((NA_TQ, LANES), lambda hp, bi, i: (bi * nblk + i, hp)),
        scratch_shapes=[pltpu.VMEM((2, n_kblk + 1, NA_KBLK, LANES), BF16),
                        pltpu.VMEM((n_kblk + 1, 2 * HEAD_DIM + ONES_ROWS, NA_KBLK), BF16),
                        pltpu.VMEM((2, 8, LANES), F32)],
        compiler_params=_cparams(("arbitrary", "arbitrary", "arbitrary")),
        name="latent_neighbourhood",
    )(proj, proj, proj, ctx_k, ctx_v, bias_t, bounds)


def _neighbourhood_bounds(qn_b, kn_b, rpb):
    n_heads = rpb.shape[0]
    qmax = jnp.broadcast_to(_query_norm_bound(qn_b), (n_heads, 1))
    kmax = jnp.broadcast_to(_query_norm_bound(kn_b) * (HEAD_DIM ** 0.5), (n_heads, 1))
    bmax = jnp.maximum(jnp.max(rpb.reshape(n_heads, -1), axis=1, keepdims=True), 0.0).astype(F32)
    table = jnp.concatenate([qmax, bmax, kmax, jnp.zeros((n_heads, LANES - 3), F32)], axis=1)
    return table.reshape(n_heads // 2, 2, LANES)


def _neighbourhood_bias(rpb, rows):
    nblk = rows // NA_QROWS
    n_dr = 2 * NA_KH - 1
    n_dc = 2 * NA_KW - 1
    kc = np.arange(GRID_W)[:, None]
    qc = np.arange(GRID_W)[None, :]
    ws = np.clip(qc - NA_KW // 2, 0, GRID_W - NA_KW)
    col_ok = (kc >= ws) & (kc < ws + NA_KW)
    dc = np.clip(kc - qc + NA_KW - 1, 0, n_dc - 1)
    dc_onehot = (dc[None] == np.arange(n_dc)[:, None, None]).astype(np.float32)
    tiles = jnp.einsum('hab,bkq->hakq', rpb.astype(F32), jnp.asarray(dc_onehot),
                       precision=lax.Precision.HIGHEST)
    tiles = jnp.where(jnp.asarray(col_ok)[None, None], tiles, MASKED)
    masked_tile = jnp.full((rpb.shape[0], 1, GRID_W, GRID_W), MASKED, F32)
    tiles = jnp.concatenate([tiles, masked_tile], axis=1)
    tile_of = np.zeros((3, NA_KROWS, NA_QROWS), np.int32)
    for v, i in enumerate((0, 1, nblk - 1)):
        r0 = i * NA_QROWS
        ks = int(np.clip(r0 - NA_KH // 2, 0, rows - NA_KROWS))
        for kl in range(NA_KROWS):
            for ql in range(NA_QROWS):
                kr, qr = ks + kl, r0 + ql
                rs = int(np.clip(qr - NA_KH // 2, 0, rows - NA_KH))
                ok = rs <= kr < rs + NA_KH
                tile_of[v, kl, ql] = (kr - qr + NA_KH - 1) if ok else n_dr
    n_heads, n_tiles = tiles.shape[:2]
    return pl.pallas_call(
        _bias_table_kernel,
        out_shape=jax.ShapeDtypeStruct((3, n_heads, NA_TK, NA_TQ), F32),
        grid_spec=pltpu.PrefetchScalarGridSpec(
            num_scalar_prefetch=1, grid=(3, n_heads),
            in_specs=[pl.BlockSpec((1, n_tiles, GRID_W, GRID_W), lambda v, h, tile_of_ref: (h, 0, 0, 0))],
            out_specs=pl.BlockSpec((1, 1, NA_TK, NA_TQ), lambda v, h, tile_of_ref: (v, h, 0, 0))),
        compiler_params=_cparams(("arbitrary", "arbitrary")),
        name="neighbourhood_bias_table",
    )(jnp.asarray(tile_of.reshape(-1)), tiles)


def _bias_table_kernel(tile_of_ref, tiles_ref, o_ref):
    v = pl.program_id(0)
    for kl in range(NA_KROWS):
        row = [tiles_ref[0, tile_of_ref[(v * NA_KROWS + kl) * NA_QROWS + ql]] for ql in range(NA_QROWS)]
        o_ref[0, 0, kl * GRID_W:(kl + 1) * GRID_W, :] = jnp.concatenate(row, axis=1)


def _merge_kernel(xp_ref, oap_ref, obp_ref, xs_ref, oas_ref, obs_ref, ona_ref, onb_ref, wo_ref,
                  g1_ref, sh2_ref, sc2_ref, n2_ref, wrh_ref, wrl_ref, rb_ref, tri_ref,
                  y_ref, hp_ref, gates_ref, rank_ref, count_ref, *, ctx_tiles):
    i = pl.program_id(0)

    @pl.when(i == 0)
    def _():
        count_ref[...] = jnp.zeros_like(count_ref)

    def one_stream(x_ref, oa_ref, ob_ref):
        na = (_rms(oa_ref[...]) * ona_ref[...]).astype(BF16)
        nb = (_rms(ob_ref[...]) * onb_ref[...]).astype(BF16)
        mix = _dot(na, wo_ref[0:WIDTH_A, :]) + _dot(nb, wo_ref[WIDTH_A:WIDTH_A + WIDTH_B, :])
        y = x_ref[...] + g1_ref[0] * mix
        y_ref[...] = y
        h = _rms(y) * n2_ref[...]
        h = h * (1.0 + sc2_ref[0]) + sh2_ref[0]
        hp_ref[...] = _pack_rows(h)
        gates, chosen = _router_gates(h, wrh_ref[...], wrl_ref[...], rb_ref[...])
        gates_ref[...] = gates
        before = _dot(chosen.astype(BF16), tri_ref[...])
        seen = count_ref[...]
        rank_ref[...] = jnp.where(chosen > 0.0, before + seen[:, 0:1], -1.0)
        count_ref[...] = seen + jnp.sum(chosen, axis=1, keepdims=True)

    pl.when(i < ctx_tiles)(lambda: one_stream(xp_ref, oap_ref, obp_ref))
    pl.when(i >= ctx_tiles)(lambda: one_stream(xs_ref, oas_ref, obs_ref))


def _merge(ctx, lat, on_a, on_b, w_out_bf, gate1, shift2, scale2, norm2, wr_hi, wr_lo, rbias, *, tm, lat_seq):
    t_c, d = ctx[0].shape
    t_l = lat[0].shape[0]
    t = t_c + t_l
    tri = jnp.asarray(np.triu(np.ones((tm, tm), np.float32), k=1), BF16)
    ctx_tiles = t_c // tm
    lat_tiles_per_batch = lat_seq // tm

    def ctx_map(i):
        return (jnp.minimum(i, ctx_tiles - 1), 0)

    def lat_map(i):
        return (jnp.maximum(i - ctx_tiles, 0), 0)

    def mod_map(i):
        return (jnp.where(i < ctx_tiles, 0, 1 + (i - ctx_tiles) // lat_tiles_per_batch), 0, 0)

    def stream_specs(index_map):
        return [pl.BlockSpec((tm, d), index_map),
                pl.BlockSpec((tm, WIDTH_A), index_map),
                pl.BlockSpec((tm, WIDTH_B), index_map)]

    return pl.pallas_call(
        functools.partial(_merge_kernel, ctx_tiles=ctx_tiles),
        out_shape=[jax.ShapeDtypeStruct((t, d), F32),
                   jax.ShapeDtypeStruct((t, d // 2), jnp.int32),
                   jax.ShapeDtypeStruct((N_EXPERTS, t), F32),
                   jax.ShapeDtypeStruct((N_EXPERTS, t), F32),
                   jax.ShapeDtypeStruct((N_EXPERTS, LANES), F32)],
        grid=(t // tm,),
        in_specs=stream_specs(ctx_map) + stream_specs(lat_map) + [
            pl.BlockSpec((1, WIDTH_A), lambda i: (0, 0)),
            pl.BlockSpec((1, WIDTH_B), lambda i: (0, 0)),
            pl.BlockSpec((WIDTH_A + WIDTH_B, d), lambda i: (0, 0)),
            pl.BlockSpec((1, 1, d), mod_map),
            pl.BlockSpec((1, 1, d), mod_map),
            pl.BlockSpec((1, 1, d), mod_map),
            pl.BlockSpec((1, d), lambda i: (0, 0)),
            pl.BlockSpec((N_EXPERTS, d), lambda i: (0, 0)),
            pl.BlockSpec((N_EXPERTS, d), lambda i: (0, 0)),
            pl.BlockSpec((N_EXPERTS, 1), lambda i: (0, 0)),
            pl.BlockSpec((tm, tm), lambda i: (0, 0))],
        out_specs=[pl.BlockSpec((tm, d), lambda i: (i, 0)),
                   pl.BlockSpec((tm, d // 2), lambda i: (i, 0)),
                   pl.BlockSpec((N_EXPERTS, tm), lambda i: (0, i)),
                   pl.BlockSpec((N_EXPERTS, tm), lambda i: (0, i)),
                   pl.BlockSpec((N_EXPERTS, LANES), lambda i: (0, 0))],
        compiler_params=_cparams(("arbitrary",)),
        name="merge_route",
    )(*ctx, *lat, on_a, on_b, w_out_bf, gate1, shift2, scale2, norm2, wr_hi, wr_lo, rbias, tri)


def _first_index_of_max(x, iota):
    mx = jnp.max(x, axis=0, keepdims=True)
    idx = jnp.min(jnp.where(x == mx, iota, float(x.shape[0])), axis=0, keepdims=True)
    return mx, iota == idx


def _router_gates(h, wr_hi, wr_lo, rbias):
    h_hi = h.astype(BF16)
    h_lo = (h - h_hi.astype(F32)).astype(BF16)
    logits = _dot_nt(wr_hi, h_hi) + (_dot_nt(wr_lo, h_hi) + _dot_nt(wr_hi, h_lo))
    scores = _sigmoid(logits)
    sel = scores + rbias
    tm = sel.shape[1]
    iota_g = lax.broadcasted_iota(jnp.int32, (GROUP_SIZE, tm), 0).astype(F32)
    group_scores = []
    for g in range(N_GROUPS):
        grp = sel[g * GROUP_SIZE:(g + 1) * GROUP_SIZE]
        m1, first = _first_index_of_max(grp, iota_g)
        m2 = jnp.max(jnp.where(first, -jnp.inf, grp), axis=0, keepdims=True)
        group_scores.append(m1 + m2)
    gs = jnp.concatenate(group_scores, axis=0)
    iota_n = lax.broadcasted_iota(jnp.int32, (N_GROUPS, tm), 0).astype(F32)
    group_on = jnp.zeros((N_GROUPS, tm), F32)
    for _ in range(TOPK_GROUPS):
        _, pick = _first_index_of_max(gs, iota_n)
        group_on = jnp.where(pick, 1.0, group_on)
        gs = jnp.where(pick, -jnp.inf, gs)
    expert_on = jnp.concatenate(
        [jnp.broadcast_to(group_on[g:g + 1], (GROUP_SIZE, tm)) for g in range(N_GROUPS)], axis=0)
    cand = jnp.where(expert_on > 0.0, sel, -jnp.inf)
    iota_e = lax.broadcasted_iota(jnp.int32, (N_EXPERTS, tm), 0).astype(F32)
    w = jnp.zeros((N_EXPERTS, tm), F32)
    chosen = jnp.zeros((N_EXPERTS, tm), F32)
    for _ in range(TOP_K):
        _, pick = _first_index_of_max(cand, iota_e)
        w = jnp.where(pick, scores, w)
        chosen = jnp.where(pick, 1.0, chosen)
        cand = jnp.where(pick, -jnp.inf, cand)
    return w / jnp.sum(w, axis=0, keepdims=True) * ROUTED_SCALE, chosen


MOE_TS = 1024
MOE_ROUTE_TM = 1024
MOE_ROW_TM = 512


def _slots_kernel(gates_ref, rank_ref, off_ref, pos_ref, gtok_ref):
    gates = gates_ref[...]
    rank = rank_ref[...]
    tm = gates.shape[1]
    slot = off_ref[...] + rank
    left = jnp.where(rank >= 0.0, 1.0, 0.0)
    iota_e = lax.broadcasted_iota(jnp.int32, (N_EXPERTS, tm), 0).astype(F32)
    pos_rows, gate_rows = [], []
    for _ in range(TOP_K):
        _, pick = _first_index_of_max(left, iota_e)
        pos_rows.append(jnp.sum(jnp.where(pick, slot, 0.0), axis=0, keepdims=True))
        gate_rows.append(jnp.sum(jnp.where(pick, gates, 0.0), axis=0, keepdims=True))
        left = jnp.where(pick, 0.0, left)
    pos_ref[...] = jnp.concatenate(pos_rows, axis=0).astype(jnp.int32)
    pad = jnp.zeros((LANES - TOP_K, tm), F32)
    gtok_ref[...] = jnp.concatenate(gate_rows + [pad], axis=0).T


def _slots(gates_t, rank_t, off):
    t = gates_t.shape[1]
    tm = MOE_ROUTE_TM
    return pl.pallas_call(
        _slots_kernel,
        out_shape=[jax.ShapeDtypeStruct((TOP_K, t), jnp.int32), jax.ShapeDtypeStruct((t, LANES), F32)],
        grid=(t // tm,),
        in_specs=[pl.BlockSpec((N_EXPERTS, tm), lambda i: (0, i)),
                  pl.BlockSpec((N_EXPERTS, tm), lambda i: (0, i)),
                  pl.BlockSpec((N_EXPERTS, 1), lambda i: (0, 0))],
        out_specs=[pl.BlockSpec((TOP_K, tm), lambda i: (0, i)),
                   pl.BlockSpec((tm, LANES), lambda i: (i, 0))],
        compiler_params=_cparams(("arbitrary",)),
        name="moe_slots",
    )(gates_t, rank_t, off)


SC_CORES = 2
SC_SUBCORES = 16
SC_ROWS = 64


def _dispatch(hp_all, slot_of):
    t, width = hp_all.shape
    n_slots = slot_of.shape[0]
    n_pad = n_slots - TOP_K * t
    workers = SC_CORES * SC_SUBCORES
    per_worker = t // workers
    pad_per_worker = n_pad // workers
    assert per_worker * workers == t and per_worker % SC_ROWS == 0
    assert pad_per_worker * workers == n_pad and pad_per_worker % SC_ROWS == 0
    mesh = plsc.VectorSubcoreMesh(core_axis_name="core", subcore_axis_name="subcore")

    @functools.partial(
        pl.kernel, mesh=mesh,
        out_type=jax.ShapeDtypeStruct((n_slots, width), jnp.int32),
        scratch_types=[pltpu.VMEM((SC_ROWS,), jnp.int32),
                       pltpu.VMEM((SC_ROWS, width), jnp.int32),
                       pltpu.SemaphoreType.DMA],
    )
    def scatter_rows(h_hbm, slot_hbm, out_hbm, idx_v, rows_v, sem):
        worker = lax.axis_index("subcore") * SC_CORES + lax.axis_index("core")
        base = worker * per_worker

        @pl.loop(0, per_worker // SC_ROWS)
        def _(j):
            first = base + j * SC_ROWS
            pltpu.sync_copy(h_hbm.at[pl.ds(first, SC_ROWS)], rows_v)
            for k in range(TOP_K):
                pltpu.sync_copy(slot_hbm.at[pl.ds(k * t + first, SC_ROWS)], idx_v)
                pltpu.async_copy(rows_v, out_hbm.at[idx_v], sem).wait()

        pltpu.sync_copy(h_hbm.at[pl.ds(0, SC_ROWS)], rows_v)
        pad_base = TOP_K * t + worker * pad_per_worker

        @pl.loop(0, pad_per_worker // SC_ROWS)
        def _(j):
            pltpu.sync_copy(slot_hbm.at[pl.ds(pad_base + j * SC_ROWS, SC_ROWS)], idx_v)
            pltpu.async_copy(rows_v, out_hbm.at[idx_v], sem).wait()

    return scatter_rows(hp_all, slot_of)


def _experts_kernel(te_ref, xs_ref, wg_ref, wu_ref, wd_ref, ys_ref, wgu_bf, wd_bf):
    i = pl.program_id(0)

    @pl.when((i == 0) | (te_ref[i] != te_ref[jnp.maximum(i, 1) - 1]))
    def _():
        wgu_bf[:, 0:D_EXPERT] = wg_ref[0].astype(BF16)
        wgu_bf[:, D_EXPERT:2 * D_EXPERT] = wu_ref[0].astype(BF16)
        wd_bf[...] = wd_ref[0].astype(BF16)

    left, right = _unpack_rows(xs_ref[...])
    x = jnp.concatenate([left, right], axis=1).astype(BF16)
    gu = _dot(x, wgu_bf[...])
    g = gu[:, 0:D_EXPERT]
    u = gu[:, D_EXPERT:2 * D_EXPERT]
    act = (g * _sigmoid(g)) * u
    ys_ref[...] = _pack_rows(_dot(act.astype(BF16), wd_bf[...]))


def _experts(xs, tile_expert, w_gate, w_up, w_down):
    n_slots, width = xs.shape
    d = 2 * width
    ts = MOE_TS
    return pl.pallas_call(
        _experts_kernel,
        out_shape=jax.ShapeDtypeStruct((n_slots, width), jnp.int32),
        grid_spec=pltpu.PrefetchScalarGridSpec(
            num_scalar_prefetch=1,
            grid=(n_slots // ts,),
            in_specs=[pl.BlockSpec((ts, width), lambda i, te: (i, 0)),
                      pl.BlockSpec((1, d, D_EXPERT), lambda i, te: (te[i], 0, 0)),
                      pl.BlockSpec((1, d, D_EXPERT), lambda i, te: (te[i], 0, 0)),
                      pl.BlockSpec((1, D_EXPERT, d), lambda i, te: (te[i], 0, 0))],
            out_specs=pl.BlockSpec((ts, width), lambda i, te: (i, 0)),
            scratch_shapes=[pltpu.VMEM((d, 2 * D_EXPERT), BF16), pltpu.VMEM((D_EXPERT, d), BF16)]),
        compiler_params=_cparams(("arbitrary",)),
        name="moe_experts",
    )(tile_expert, xs, w_gate, w_up, w_down)


def _gather_slots(y_slots, slot_of, t):
    width = y_slots.shape[1]
    workers = SC_CORES * SC_SUBCORES
    per_worker = t // workers
    n_blocks = (per_worker // SC_ROWS) * TOP_K
    assert per_worker * workers == t and per_worker % SC_ROWS == 0 and n_blocks % 2 == 0
    mesh = plsc.VectorSubcoreMesh(core_axis_name="core", subcore_axis_name="subcore")

    @functools.partial(
        pl.kernel, mesh=mesh,
        out_type=jax.ShapeDtypeStruct((TOP_K * t, width), jnp.int32),
        scratch_types=[pltpu.VMEM((SC_ROWS,), jnp.int32), pltpu.VMEM((SC_ROWS,), jnp.int32),
                       pltpu.VMEM((SC_ROWS, width), jnp.int32), pltpu.VMEM((SC_ROWS, width), jnp.int32),
                       pltpu.SemaphoreType.DMA, pltpu.SemaphoreType.DMA],
    )
    def gather_rows(ys_hbm, slot_hbm, out_hbm, idx0, idx1, rows0, rows1, sem0, sem1):
        worker = lax.axis_index("subcore") * SC_CORES + lax.axis_index("core")
        base = worker * per_worker

        def first_row(n):
            return (n % TOP_K) * t + base + (n // TOP_K) * SC_ROWS

        def start(n, idx_v, rows_v, sem):
            pltpu.sync_copy(slot_hbm.at[pl.ds(first_row(n), SC_ROWS)], idx_v)
            pltpu.async_copy(ys_hbm.at[idx_v], rows_v, sem)

        def finish(n, idx_v, rows_v, sem):
            pltpu.make_async_copy(ys_hbm.at[idx_v], rows_v, sem).wait()
            pltpu.sync_copy(rows_v, out_hbm.at[pl.ds(first_row(n), SC_ROWS)])

        start(0, idx0, rows0, sem0)

        @pl.loop(0, n_blocks, step=2)
        def _(n):
            start(n + 1, idx1, rows1, sem1)
            finish(n, idx0, rows0, sem0)

            @pl.when(n + 2 < n_blocks)
            def _():
                start(n + 2, idx0, rows0, sem0)

            finish(n + 1, idx1, rows1, sem1)

    return gather_rows(y_slots, slot_of)


def _combine_kernel(y_ref, h_ref, g2_ref, gtok_ref, rows_ref, wgs_ref, wus_ref, wds_ref, o_ref):
    h_left, h_right = _unpack_rows(h_ref[...])
    h = jnp.concatenate([h_left, h_right], axis=1).astype(BF16)
    gs = _dot(h, wgs_ref[...])
    us = _dot(h, wus_ref[...])
    shared = _dot(((gs * _sigmoid(gs)) * us).astype(BF16), wds_ref[...])

    gtok = gtok_ref[...]
    acc_left = acc_right = None
    for k in range(TOP_K):
        left, right = _unpack_rows(rows_ref[k])
        gate = gtok[:, k:k + 1]
        acc_left = gate * left if acc_left is None else acc_left + gate * left
        acc_right = gate * right if acc_right is None else acc_right + gate * right
    routed = jnp.concatenate([acc_left, acc_right], axis=1)
    o_ref[...] = y_ref[...] + g2_ref[0] * (routed + shared)


def _combine(y_all, hp_all, gate2, gtok, rows, wgs, wus, wds, *, first_token, tokens, seq):
    d = y_all.shape[1]
    width = hp_all.shape[1]
    tm = MOE_ROW_TM
    tile0 = first_token // tm
    nb = gate2.shape[0]
    tiles_per_batch = seq // tm

    def mod_map(i):
        return ((i // tiles_per_batch) if nb > 1 else 0, 0, 0)

    return pl.pallas_call(
        _combine_kernel,
        out_shape=jax.ShapeDtypeStruct((tokens, d), F32),
        grid=(tokens // tm,),
        in_specs=[pl.BlockSpec((tm, d), lambda i: (tile0 + i, 0)),
                  pl.BlockSpec((tm, width), lambda i: (tile0 + i, 0)),
                  pl.BlockSpec((1, 1, d), mod_map),
                  pl.BlockSpec((tm, LANES), lambda i: (tile0 + i, 0)),
                  pl.BlockSpec((TOP_K, tm, width), lambda i: (0, tile0 + i, 0)),
                  pl.BlockSpec((d, D_SHARED), lambda i: (0, 0)),
                  pl.BlockSpec((d, D_SHARED), lambda i: (0, 0)),
                  pl.BlockSpec((D_SHARED, d), lambda i: (0, 0))],
        out_specs=pl.BlockSpec((tm, d), lambda i: (i, 0)),
        compiler_params=_cparams(("arbitrary",)),
        name="moe_combine",
    )(y_all, hp_all, gate2, gtok, rows, wgs, wus, wds)


def _expert_layout(counts, n_tiles):
    cnt = counts.astype(jnp.int32)
    tiles = (cnt + (MOE_TS - 1)) // MOE_TS
    last_tile = jnp.cumsum(tiles)
    off = (last_tile - tiles) * MOE_TS
    pad_lo = off + cnt
    pad_hi = (off + tiles * MOE_TS).at[N_EXPERTS - 1].set(n_tiles * MOE_TS)
    pad_cnt = pad_hi - pad_lo
    pad_last = jnp.cumsum(pad_cnt)
    shift = pad_lo - (pad_last - pad_cnt)
    j = jnp.arange(N_EXPERTS * MOE_TS, dtype=jnp.int32)
    past = (pad_last[None, :-1] <= j[:, None]).astype(jnp.int32)
    pad_slots = j + shift[0] + jnp.sum(past * (shift[1:] - shift[:-1])[None, :], axis=1)
    tile_ids = jnp.arange(n_tiles, dtype=jnp.int32)
    tile_expert = jnp.minimum(
        jnp.sum((last_tile[None, :] <= tile_ids[:, None]).astype(jnp.int32), axis=1), N_EXPERTS - 1)
    return off, pad_slots, tile_expert


def _rope_tables(n_tokens):
    t = jnp.arange(n_tokens)
    row = (t // GRID_W).astype(F32)
    col = (t % GRID_W).astype(F32)
    nf = HEAD_DIM // 4
    freqs = ROPE_THETA ** (-jnp.arange(nf, dtype=F32) / nf)
    ang_r = row[:, None] * freqs
    ang_c = col[:, None] * freqs
    cos = jnp.concatenate([jnp.cos(ang_r)] * 2 + [jnp.cos(ang_c)] * 2, axis=1)
    sin = jnp.concatenate([-jnp.sin(ang_r), jnp.sin(ang_r), -jnp.sin(ang_c), jnp.sin(ang_c)], axis=1)
    reps = LANES // HEAD_DIM
    return jnp.tile(cos, (1, reps)), jnp.tile(sin, (1, reps))


def _head_gains(qn_a, kn_a, qn_b, kn_b):
    ones = jnp.ones((HEAD_DIM,), F32)
    parts = ([qn_a] * N_HEADS_A + [kn_a] * N_KV_A + [ones] * N_KV_A
             + [qn_b] * N_HEADS_B + [kn_b] * N_HEADS_B + [ones] * N_HEADS_B)
    return jnp.concatenate(parts).reshape(1, IN_COLS).astype(F32)


def _same_head_indicator():
    i = np.arange(MXU_DIM)
    return jnp.asarray((i[:, None] // HEAD_DIM) == (i[None, :] // HEAD_DIM), BF16)


def _token_major(cache):
    b, h, s, hd = cache.shape
    return cache.transpose(0, 2, 1, 3).reshape(b, s, h * hd).astype(BF16)


def kernel(x_prompt, x_sample, cache_k_a, cache_v_a, cache_k_b, cache_v_b, c, c_ctx, w_mod, b_mod, norm1, norm2, w_in, qn_a, kn_a, qn_b, kn_b, rpb, on_a, on_b, w_out, w_router, router_bias, w_gate_e, w_up_e, w_down_e, w_gate_s, w_up_s, w_down_s):
    depth = w_mod.shape[0]
    assert depth == 1
    l = 0
    bp, sp, d = x_prompt.shape
    bs, ss, _ = x_sample.shape

    cvec = jnp.concatenate([c_ctx[None, :], c], axis=0)
    rows = -(-cvec.shape[0] // 8) * 8
    cvec = jnp.pad(cvec, ((0, rows - cvec.shape[0]), (0, 0)))
    mod = _adaln(cvec, w_mod[l], b_mod[l])
    mod_p = [m.reshape(1, 1, d) for m in jnp.split(mod[0:1], 6, axis=-1)]
    mod_s = [m.reshape(bs, 1, d) for m in jnp.split(mod[1:1 + bs], 6, axis=-1)]
    mod_all = [m.reshape(1 + bs, 1, d) for m in jnp.split(mod[0:1 + bs], 6, axis=-1)]

    w_in_bf = w_in[l].astype(BF16)
    w_out_bf = w_out[l].astype(BF16)
    gain = _head_gains(qn_a[l], kn_a[l], qn_b[l], kn_b[l])
    seg = _same_head_indicator()
    n1 = norm1[l].reshape(1, d)
    n2 = norm2[l].reshape(1, d)
    ona = on_a[l].reshape(1, WIDTH_A)
    onb = on_b[l].reshape(1, WIDTH_B)
    wr_t = w_router[l].T
    wr_hi = wr_t.astype(BF16)
    wr_lo = (wr_t - wr_hi.astype(F32)).astype(BF16)
    rbias = router_bias[l].reshape(N_EXPERTS, 1).astype(F32)
    wgs = w_gate_s[l].astype(BF16)
    wus = w_up_s[l].astype(BF16)
    wds = w_down_s[l].astype(BF16)
    t_p = bp * sp
    t_s = bs * ss
    t_all = t_p + t_s

    xp = x_prompt.reshape(t_p, d)
    proj_p, st_ka, st_va, st_kb, st_vb = _project(
        xp, mod_p[0], mod_p[1], n1, w_in_bf, gain, seg, None, tm=sp, seq=sp, states=True)
    oa_p, ob_p = _context_attention(proj_p, seq=sp)

    xs = x_sample.reshape(t_s, d)
    proj_s, = _project(xs, mod_s[0], mod_s[1], n1, w_in_bf, gain, seg, _rope_tables(ss),
                       tm=1024, seq=ss, states=False)
    oa_s = _latent_gqa(proj_s, _token_major(cache_k_a[:, l]), _token_major(cache_v_a[:, l]),
                       _query_norm_bound(qn_a[l]), seq=ss, tq=256)
    bias_t = _neighbourhood_bias(rpb[l], ss // GRID_W)
    ob_s = _latent_neighbourhood(proj_s, _token_major(cache_k_b[:, l]), _token_major(cache_v_b[:, l]),
                                 bias_t, _neighbourhood_bounds(qn_b[l], kn_b[l], rpb[l]), seq=ss)

    y1_all, hp_all, gates_t, rank_t, counts = _merge(
        (xp, oa_p, ob_p), (xs, oa_s, ob_s), ona, onb, w_out_bf, mod_all[2], mod_all[3], mod_all[4], n2,
        wr_hi, wr_lo, rbias, tm=512, lat_seq=ss)
    n_tiles = t_all * TOP_K // MOE_TS + N_EXPERTS
    off, pad_slots, tile_expert = _expert_layout(counts[:, 0], n_tiles)
    pos, gtok = _slots(gates_t, rank_t, off.astype(F32).reshape(N_EXPERTS, 1))
    slot_of = pos.reshape(TOP_K * t_all)
    x_slots = _dispatch(hp_all, jnp.concatenate([slot_of, pad_slots]))
    y_slots = _experts(x_slots, tile_expert, w_gate_e[l], w_up_e[l], w_down_e[l])
    rows = _gather_slots(y_slots, slot_of, t_all).reshape(TOP_K, t_all, d // 2)
    y_p = _combine(y1_all, hp_all, mod_p[5], gtok, rows, wgs, wus, wds,
                   first_token=0, tokens=t_p, seq=sp)
    y_s = _combine(y1_all, hp_all, mod_s[5], gtok, rows, wgs, wus, wds,
                   first_token=t_p, tokens=t_s, seq=ss)

    return (y_p.reshape(bp, sp, d), y_s.reshape(bs, ss, d), st_ka, st_va, st_kb, st_vb)
```

```python
import functools

import numpy as np
import jax
import jax.numpy as jnp
from jax import lax
from jax.experimental import pallas as pl
from jax.experimental.pallas import tpu as pltpu
from jax.experimental.pallas import tpu_sc as plsc

F32 = jnp.float32
BF16 = jnp.bfloat16

D_MODEL = 1024
HEAD_DIM = 64
N_HEADS_A = 8
N_KV_A = 2
GROUP_A = N_HEADS_A // N_KV_A
N_HEADS_B = 8
WIDTH_A = N_HEADS_A * HEAD_DIM
WIDTH_B = N_HEADS_B * HEAD_DIM
KV_WIDTH_A = N_KV_A * HEAD_DIM
IN_COLS = WIDTH_A + 2 * KV_WIDTH_A + 3 * WIDTH_B
GRID_W = 64
ROPE_THETA = 10000.0
NA_KH = 8
NA_KW = 16
N_EXPERTS = 64
N_GROUPS = 8
GROUP_SIZE = N_EXPERTS // N_GROUPS
TOPK_GROUPS = 4
TOP_K = 8
D_EXPERT = 256
D_SHARED = 256
ROUTED_SCALE = 2.5
EPS = 1e-6

LANES = 128
MXU_DIM = 256
MASKED = -1e30

COL_QA = 0
COL_KA = WIDTH_A
COL_VA = COL_KA + KV_WIDTH_A
COL_QB = COL_VA + KV_WIDTH_A
COL_KB = COL_QB + WIDTH_B
COL_VB = COL_KB + WIDTH_B

NA_QROWS = 8
NA_KROWS = 2 * NA_KH
NA_TQ = NA_QROWS * GRID_W
NA_TK = NA_KROWS * GRID_W
NA_KBLK = 256

VMEM_LIMIT = 56 * 1024 * 1024


def _cparams(sem):
    return pltpu.CompilerParams(dimension_semantics=sem, vmem_limit_bytes=VMEM_LIMIT)


def _dot(a, b):
    return jnp.dot(a, b, preferred_element_type=F32)


def _dot_nt(a, b):
    return lax.dot_general(a, b, (((1,), (1,)), ((), ())), preferred_element_type=F32)


def _sigmoid(x):
    return 1.0 / (1.0 + jnp.exp(-x))


def _rms(x):
    return x * lax.rsqrt(jnp.mean(x * x, axis=-1, keepdims=True) + EPS)


def _pack_rows(x):
    n = x.shape[1] // 2
    hi = lax.bitcast_convert_type(x[:, :n].astype(BF16).astype(F32), jnp.int32)
    lo = lax.bitcast_convert_type(x[:, n:].astype(BF16).astype(F32), jnp.int32)
    return hi | lax.shift_right_logical(lo, 16)


def _unpack_rows(w):
    left = lax.bitcast_convert_type(w & jnp.int32(-65536), F32)
    right = lax.bitcast_convert_type(lax.shift_left(w, 16), F32)
    return left, right


def _mod_kernel(c_ref, w_ref, b_ref, o_ref):
    c = c_ref[...]
    s = c * _sigmoid(c)
    o_ref[...] = jnp.dot(s, w_ref[...], preferred_element_type=F32,
                         precision=lax.Precision.HIGHEST) + b_ref[...]


def _adaln(cvec, w_mod, b_mod):
    rows, d = cvec.shape
    n = w_mod.shape[1]
    tn = 512
    return pl.pallas_call(
        _mod_kernel,
        out_shape=jax.ShapeDtypeStruct((rows, n), F32),
        grid=(n // tn,),
        in_specs=[pl.BlockSpec((rows, d), lambda j: (0, 0)),
                  pl.BlockSpec((d, tn), lambda j: (0, j)),
                  pl.BlockSpec((1, tn), lambda j: (0, j))],
        out_specs=pl.BlockSpec((rows, tn), lambda j: (0, j)),
        compiler_params=_cparams(("arbitrary",)),
        name="adaln_mod",
    )(cvec, w_mod, b_mod.reshape(1, n))


def _proj_chunks():
    def split(c0, width, step, *flags):
        return [(c0 + i, min(step, width - i)) + flags for i in range(0, width, step)]
    return (split(COL_QA, WIDTH_A, MXU_DIM, True, True, True)
            + split(COL_KA, KV_WIDTH_A, MXU_DIM, True, True, False)
            + split(COL_VA, KV_WIDTH_A, MXU_DIM, False, False, False)
            + split(COL_QB, WIDTH_B, MXU_DIM, True, False, True)
            + split(COL_KB, WIDTH_B, MXU_DIM, True, False, False)
            + split(COL_VB, WIDTH_B, MXU_DIM, False, False, False))


_PROJ_CHUNKS = _proj_chunks()


def _proj_kernel(*refs, rope, states):
    x_ref, sh_ref, sc_ref, n1_ref, w_ref, gain_ref, seg_ref = refs[:7]
    pos = 7
    if rope:
        cos_ref, sin_ref = refs[pos:pos + 2]
        pos += 2
    out_ref = refs[pos]
    pos += 1
    if states:
        ka_ref, va_ref, kb_ref, vb_ref = refs[pos:pos + 4]
        state_of = {COL_KA: ka_ref, COL_VA: va_ref, COL_KB: kb_ref, COL_VB: vb_ref}

    x = x_ref[...]
    h = _rms(x) * n1_ref[...]
    h = h * (1.0 + sc_ref[0]) + sh_ref[0]
    p = _dot(h.astype(BF16), w_ref[...])

    for c0, w, normed, roped, is_query in _PROJ_CHUNKS:
        pc = p[:, c0:c0 + w]
        if normed:
            seg = seg_ref[0:w, 0:w]
            sq = pc * pc
            hi = sq.astype(BF16)
            lo = (sq - hi.astype(F32)).astype(BF16)
            ss = _dot(hi, seg) + _dot(lo, seg)
            pc = pc * lax.rsqrt(ss * (1.0 / HEAD_DIM) + EPS) * gain_ref[:, c0:c0 + w]
        if states:
            for start, ref in state_of.items():
                if start <= c0 < start + ref.shape[2] * HEAD_DIM:
                    base = (c0 - start) // HEAD_DIM
                    for hh in range(w // HEAD_DIM):
                        ref[0, 0, base + hh] = pc[:, hh * HEAD_DIM:(hh + 1) * HEAD_DIM]
        if rope and roped:
            reps = w // LANES
            cos = jnp.concatenate([cos_ref[...]] * reps, axis=1) if reps > 1 else cos_ref[...]
            sin = jnp.concatenate([sin_ref[...]] * reps, axis=1) if reps > 1 else sin_ref[...]
            lane = lax.broadcasted_iota(jnp.int32, pc.shape, 1)
            first_half = (lane % (HEAD_DIM // 2)) < (HEAD_DIM // 4)
            partner = jnp.where(first_half,
                                pltpu.roll(pc, w - HEAD_DIM // 4, 1),
                                pltpu.roll(pc, HEAD_DIM // 4, 1))
            pc = pc * cos + partner * sin
        if is_query:
            pc = pc * (HEAD_DIM ** -0.5)
        out_ref[:, c0:c0 + w] = pc.astype(BF16)


def _project(x2d, shift, scale, norm1, w_in_bf, gain, seg, rope_tabs, *, tm, seq, states):
    t, d = x2d.shape
    nb = shift.shape[0]
    tiles_per_batch = seq // tm
    rope = rope_tabs is not None

    def mod_map(i):
        return ((i // tiles_per_batch) if nb > 1 else 0, 0, 0)

    in_specs = [pl.BlockSpec((tm, d), lambda i: (i, 0)),
                pl.BlockSpec((1, 1, d), mod_map),
                pl.BlockSpec((1, 1, d), mod_map),
                pl.BlockSpec((1, d), lambda i: (0, 0)),
                pl.BlockSpec((d, IN_COLS), lambda i: (0, 0)),
                pl.BlockSpec((1, IN_COLS), lambda i: (0, 0)),
                pl.BlockSpec((MXU_DIM, MXU_DIM), lambda i: (0, 0))]
    args = [x2d, shift, scale, norm1, w_in_bf, gain, seg]
    if rope:
        in_specs += [pl.BlockSpec((tm, LANES), lambda i: (i % tiles_per_batch, 0))] * 2
        args += list(rope_tabs)
    out_shape = [jax.ShapeDtypeStruct((t, IN_COLS), BF16)]
    out_specs = [pl.BlockSpec((tm, IN_COLS), lambda i: (i, 0))]
    if states:
        assert tm == seq
        b = t // seq
        for nh in (N_KV_A, N_KV_A, N_HEADS_B, N_HEADS_B):
            out_shape.append(jax.ShapeDtypeStruct((b, 1, nh, seq, HEAD_DIM), F32))
            out_specs.append(pl.BlockSpec((1, 1, nh, seq, HEAD_DIM), lambda i: (i, 0, 0, 0, 0)))
    return pl.pallas_call(
        functools.partial(_proj_kernel, rope=rope, states=states),
        out_shape=out_shape,
        grid=(t // tm,),
        in_specs=in_specs,
        out_specs=out_specs,
        compiler_params=_cparams(("arbitrary",)),
        name="proj_states" if states else "proj_rope",
    )(*args)


def _lane_half(shape):
    return lax.broadcasted_iota(jnp.int32, shape, 1) // HEAD_DIM


def _keep_half(x, half):
    return jnp.where(_lane_half(x.shape) == half, x, jnp.zeros_like(x))


def _transpose_bf16(x):
    return x.astype(F32).T.astype(BF16)


def _attend(q, keys, values_t, biases):
    return _softmax_av(_scores(q, keys, biases), values_t)


def _scores(q, keys, biases):
    scores = []
    for k, b in zip(keys, biases):
        s = _dot_nt(k, q)
        if b is not None:
            s = s + b
        scores.append(s)
    return scores


def _softmax_av(scores, values_t):
    m = functools.reduce(jnp.maximum, [jnp.max(s, axis=0, keepdims=True) for s in scores])
    denom = None
    out = None
    for s, vt in zip(scores, values_t):
        p = jnp.exp(s - m)
        ps = jnp.sum(p, axis=0, keepdims=True)
        po = _dot(vt, p.astype(BF16))
        denom = ps if denom is None else denom + ps
        out = po if out is None else out + po
    return out / denom


def _swap_halves(q_bf16):
    return pltpu.roll(q_bf16.astype(F32), HEAD_DIM, 1).astype(BF16)


def _gqa_heads(q_of_pair, keys_by_group, values_t):
    outs = []
    for h in range(N_HEADS_A):
        g = h // GROUP_A
        q = q_of_pair(h // 2)
        if h % 2 != g:
            q = _swap_halves(q)
        o = _attend(q, keys_by_group[g], values_t, [None] * len(values_t))
        outs.append(o[g * HEAD_DIM:(g + 1) * HEAD_DIM])
    return jnp.concatenate(outs, axis=0)


def _ctx_attn_kernel(p_ref, oa_ref, ob_ref):
    ka = p_ref[:, COL_KA:COL_KA + LANES]
    va_t = [_transpose_bf16(p_ref[:, COL_VA:COL_VA + LANES])]
    keys_by_group = [[_keep_half(ka, g)] for g in range(N_KV_A)]
    oa = _gqa_heads(lambda i: p_ref[:, COL_QA + i * LANES:COL_QA + (i + 1) * LANES],
                    keys_by_group, va_t)
    oa_ref[...] = oa.T

    outs = []
    for i in range(N_HEADS_B // 2):
        q = p_ref[:, COL_QB + i * LANES:COL_QB + (i + 1) * LANES]
        k = p_ref[:, COL_KB + i * LANES:COL_KB + (i + 1) * LANES]
        vt = [_transpose_bf16(p_ref[:, COL_VB + i * LANES:COL_VB + (i + 1) * LANES])]
        for half in range(2):
            o = _attend(q, [_keep_half(k, half)], vt, [None])
            outs.append(o[half * HEAD_DIM:(half + 1) * HEAD_DIM])
    ob_ref[...] = jnp.concatenate(outs, axis=0).T


def _context_attention(proj, *, seq):
    t = proj.shape[0]
    return pl.pallas_call(
        _ctx_attn_kernel,
        out_shape=[jax.ShapeDtypeStruct((t, WIDTH_A), F32), jax.ShapeDtypeStruct((t, WIDTH_B), F32)],
        grid=(t // seq,),
        in_specs=[pl.BlockSpec((seq, IN_COLS), lambda i: (i, 0))],
        out_specs=[pl.BlockSpec((seq, WIDTH_A), lambda i: (i, 0)),
                   pl.BlockSpec((seq, WIDTH_B), lambda i: (i, 0))],
        compiler_params=_cparams(("arbitrary",)),
        name="context_attention",
    )(proj)


ATTN_SAFE_SHIFT = 40.0
ONES_ROWS = 16


def _round_up_bf16(x):
    return (x * (1.0 + 2.0 ** -6)).astype(BF16).astype(F32)


def _query_norm_bound(gain):
    return jnp.max(jnp.abs(gain)).reshape(1, 1).astype(F32)


def _ones_lane(g):
    return (1 - g) * HEAD_DIM


def _gqa_latent_kernel(q_ref, k_ref, v_ref, ck_ref, cv_ref, qmax_ref, o_ref,
                       kg_ref, ckg_ref, vt_ref, cvt_ref, shift_ref):
    lane_k = lax.broadcasted_iota(jnp.int32, (1, LANES), 1)

    @pl.when(pl.program_id(1) == 0)
    def _():
        k = k_ref[...]
        ck = ck_ref[0]
        vt = v_ref[...].astype(F32).T
        cvt = cv_ref[0].astype(F32).T
        for g in range(N_KV_A):
            kf = _keep_half(k, g).astype(F32)
            ckf = _keep_half(ck, g).astype(F32)
            ksq = jnp.maximum(jnp.max(jnp.sum(kf * kf, axis=1, keepdims=True), axis=0, keepdims=True),
                              jnp.max(jnp.sum(ckf * ckf, axis=1, keepdims=True), axis=0, keepdims=True))
            shift_ref[g] = jnp.broadcast_to(_round_up_bf16(qmax_ref[...] * jnp.sqrt(ksq)), shift_ref.shape[1:])
            kg_ref[g] = jnp.where(lane_k == _ones_lane(g), 1.0, kf).astype(BF16)
            ckg_ref[g] = jnp.where(lane_k == _ones_lane(g), 1.0, ckf).astype(BF16)
            rows = slice(g * HEAD_DIM, (g + 1) * HEAD_DIM)
            vt_ref[g] = jnp.concatenate([vt[rows], jnp.ones((ONES_ROWS, vt.shape[1]), F32)], axis=0).astype(BF16)
            cvt_ref[g] = jnp.concatenate([cvt[rows], jnp.ones((ONES_ROWS, cvt.shape[1]), F32)], axis=0).astype(BF16)

    tq = q_ref.shape[0]
    lane_q = lax.broadcasted_iota(jnp.int32, (GROUP_A * tq, LANES), 1)
    queries, shifts = [], []
    for g in range(N_KV_A):
        qs = []
        for j in range(GROUP_A):
            h = g * GROUP_A + j
            q = q_ref[:, (h // 2) * LANES:(h // 2 + 1) * LANES].astype(F32)
            qs.append(q if h % 2 == g else pltpu.roll(q, HEAD_DIM, 1))
        queries.append(jnp.where(lane_q // HEAD_DIM == g, jnp.concatenate(qs, axis=0), 0.0))
        shifts.append(shift_ref[g][0:1, 0:1])
    safe = jnp.max(jnp.maximum(shift_ref[0], shift_ref[1])) <= ATTN_SAFE_SHIFT

    def attend(g, p_lat, p_ctx):
        half = p_lat.shape[1] // 2
        o = jnp.concatenate([_dot(vt_ref[g], p_lat[:, :half]) + _dot(cvt_ref[g], p_ctx[:, :half]),
                             _dot(vt_ref[g], p_lat[:, half:]) + _dot(cvt_ref[g], p_ctx[:, half:])],
                            axis=1)
        o = o[:HEAD_DIM] / o[HEAD_DIM:HEAD_DIM + 1]
        heads = jnp.concatenate([o[:, j * tq:(j + 1) * tq] for j in range(GROUP_A)], axis=0)
        o_ref[:, g * GROUP_A * HEAD_DIM:(g + 1) * GROUP_A * HEAD_DIM] = heads.T

    def with_bound():
        for g in range(N_KV_A):
            qa = jnp.where(lane_q == _ones_lane(g), -shifts[g], queries[g]).astype(BF16)
            attend(g, jnp.exp(_dot_nt(kg_ref[g], qa)).astype(BF16), jnp.exp(_dot_nt(ckg_ref[g], qa)).astype(BF16))

    def with_row_max():
        for g in range(N_KV_A):
            qa = queries[g].astype(BF16)
            s_lat = _dot_nt(kg_ref[g], qa)
            s_ctx = _dot_nt(ckg_ref[g], qa)
            m = jnp.maximum(jnp.max(s_lat, axis=0, keepdims=True), jnp.max(s_ctx, axis=0, keepdims=True))
            attend(g, jnp.exp(s_lat - m).astype(BF16), jnp.exp(s_ctx - m).astype(BF16))

    pl.when(safe)(with_bound)
    pl.when(jnp.logical_not(safe))(with_row_max)


def _latent_gqa(proj, ctx_k, ctx_v, qmax, *, seq, tq):
    t = proj.shape[0]
    b = t // seq
    nq = seq // tq
    past = ctx_k.shape[1]
    return pl.pallas_call(
        _gqa_latent_kernel,
        out_shape=jax.ShapeDtypeStruct((t, WIDTH_A), F32),
        grid=(b, nq),
        in_specs=[pl.BlockSpec((tq, WIDTH_A), lambda bi, qi: (bi * nq + qi, 0)),
                  pl.BlockSpec((seq, LANES), lambda bi, qi: (bi, COL_KA // LANES)),
                  pl.BlockSpec((seq, LANES), lambda bi, qi: (bi, COL_VA // LANES)),
                  pl.BlockSpec((1, past, LANES), lambda bi, qi: (bi, 0, 0)),
                  pl.BlockSpec((1, past, LANES), lambda bi, qi: (bi, 0, 0)),
                  pl.BlockSpec((1, 1), lambda bi, qi: (0, 0))],
        out_specs=pl.BlockSpec((tq, WIDTH_A), lambda bi, qi: (bi * nq + qi, 0)),
        scratch_shapes=[pltpu.VMEM((N_KV_A, seq, LANES), BF16),
                        pltpu.VMEM((N_KV_A, past, LANES), BF16),
                        pltpu.VMEM((N_KV_A, HEAD_DIM + ONES_ROWS, seq), BF16),
                        pltpu.VMEM((N_KV_A, HEAD_DIM + ONES_ROWS, past), BF16),
                        pltpu.VMEM((N_KV_A, 8, LANES), F32)],
        compiler_params=_cparams(("arbitrary", "arbitrary")),
        name="latent_gqa",
    )(proj, proj, proj, ctx_k, ctx_v, qmax)


def _na_kernel(q_ref, k_ref, v_ref, ck_ref, cv_ref, bias_ref, bound_ref, o_ref, keys_ref, vt_ref, shift_ref,
               *, rows):
    i = pl.program_id(2)
    n_kblk = k_ref.shape[0] // NA_KBLK
    lane_k = lax.broadcasted_iota(jnp.int32, (1, LANES), 1)
    one = jnp.ones((), BF16)

    @pl.when(i == 0)
    def _():
        k = k_ref[...]
        ck = ck_ref[0]
        for half in range(2):
            kh = jnp.where(lane_k == _ones_lane(half), one, _keep_half(k, half))
            keys_ref[half, 0:n_kblk] = kh.reshape(n_kblk, NA_KBLK, LANES)
            ckh = _keep_half(ck, half)
            keys_ref[half, n_kblk] = jnp.where(lane_k == _ones_lane(half), one, ckh)
            ckf = ckh.astype(F32)
            ctx_norm = jnp.sqrt(jnp.max(jnp.sum(ckf * ckf, axis=1, keepdims=True), axis=0, keepdims=True))
            consts = bound_ref[0, half:half + 1, :]
            kmax = jnp.maximum(ctx_norm, consts[:, 2:3])
            shift_ref[half] = jnp.broadcast_to(_round_up_bf16(consts[:, 0:1] * kmax + consts[:, 1:2]),
                                               shift_ref.shape[1:])
        ones_rows = jnp.ones((ONES_ROWS, NA_KBLK), F32)
        vt = v_ref[...].astype(F32).T
        for j in range(n_kblk):
            vt_ref[j] = jnp.concatenate([vt[:, j * NA_KBLK:(j + 1) * NA_KBLK], ones_rows], axis=0).astype(BF16)
        vt_ref[n_kblk] = jnp.concatenate([cv_ref[0].astype(F32).T, ones_rows], axis=0).astype(BF16)

    q = q_ref[...]
    lane_q = lax.broadcasted_iota(jnp.int32, q.shape, 1)
    first = _na_first_key_block(i, rows)
    n_qblk = rows // NA_QROWS
    variant = jnp.where(i == 0, 0, jnp.where(i == n_qblk - 1, 2, 1))
    blocks = [first + j for j in range(NA_TK // NA_KBLK)] + [n_kblk]
    values_t = [vt_ref[blk] for blk in blocks]
    heads = []
    for half in range(2):
        keys = [keys_ref[half, blk] for blk in blocks]
        biases = [bias_ref[variant, half, j * NA_KBLK:(j + 1) * NA_KBLK, :] for j in range(NA_TK // NA_KBLK)] + [None]
        heads.append((shift_ref[half][0:1, 0:1], keys, biases))
    safe = jnp.max(jnp.maximum(shift_ref[0], shift_ref[1])) <= ATTN_SAFE_SHIFT

    def attend(probabilities):
        outs = []
        for half, ps in enumerate(probabilities):
            o = functools.reduce(lambda a, b: a + b, [_dot(vt, p) for vt, p in zip(values_t, ps)])
            outs.append(o[half * HEAD_DIM:(half + 1) * HEAD_DIM] / o[2 * HEAD_DIM:2 * HEAD_DIM + 1])
        o_ref[...] = jnp.concatenate(outs, axis=0).T

    def with_bound():
        probabilities = []
        for half, (shift, keys, biases) in enumerate(heads):
            qa = jnp.where(lane_q == _ones_lane(half), (-shift).astype(BF16), _keep_half(q, half))
            probabilities.append([jnp.exp(s).astype(BF16) for s in _scores(qa, keys, biases)])
        attend(probabilities)

    def with_row_max():
        all_scores = [_scores(_keep_half(q, half), keys, biases) for half, (_, keys, biases) in enumerate(heads)]
        probabilities = []
        for scores in all_scores:
            m = functools.reduce(jnp.maximum, [jnp.max(s, axis=0, keepdims=True) for s in scores])
            probabilities.append([jnp.exp(s - m).astype(BF16) for s in scores])
        attend(probabilities)

    pl.when(safe)(with_bound)
    pl.when(jnp.logical_not(safe))(with_row_max)


def _na_first_key_block(i, rows):
    per_qblock = NA_QROWS * GRID_W // NA_KBLK
    lead = (NA_KH // 2) * GRID_W // NA_KBLK
    return jnp.clip(per_qblock * i - lead, 0, (rows - NA_KROWS) * GRID_W // NA_KBLK)


def _latent_neighbourhood(proj, ctx_k, ctx_v, bias_t, bounds, *, seq):
    t = proj.shape[0]
    b = t // seq
    rows = seq // GRID_W
    nblk = rows // NA_QROWS
    n_kblk = seq // NA_KBLK
    past = ctx_k.shape[1]
    assert past == NA_KBLK
    grid = (N_HEADS_B // 2, b, nblk)
    in_specs = [pl.BlockSpec((NA_TQ, LANES), lambda hp, bi, i: (bi * nblk + i, COL_QB // LANES + hp)),
                pl.BlockSpec((seq, LANES), lambda hp, bi, i: (bi, COL_KB // LANES + hp)),
                pl.BlockSpec((seq, LANES), lambda hp, bi, i: (bi, COL_VB // LANES + hp)),
                pl.BlockSpec((1, past, LANES), lambda hp, bi, i: (bi, 0, hp)),
                pl.BlockSpec((1, past, LANES), lambda hp, bi, i: (bi, 0, hp)),
                pl.BlockSpec((3, 2, NA_TK, NA_TQ), lambda hp, bi, i: (0, hp, 0, 0)),
                pl.BlockSpec((1, 2, LANES), lambda hp, bi, i: (hp, 0, 0))]
    return pl.pallas_call(
        functools.partial(_na_kernel, rows=rows),
        out_shape=jax.ShapeDtypeStruct((t, WIDTH_B), F32),
        grid=grid,
        in_specs=in_specs,
        out_specs=pl.BlockSpec((NA_TQ, LANES), lambda hp, bi, i: (bi * nblk + i, hp)),
        scratch_shapes=[pltpu.VMEM((2, n_kblk + 1, NA_KBLK, LANES), BF16),
                        pltpu.VMEM((n_kblk + 1, 2 * HEAD_DIM + ONES_ROWS, NA_KBLK), BF16),
                        pltpu.VMEM((2, 8, LANES), F32)],
        compiler_params=_cparams(("arbitrary", "arbitrary", "arbitrary")),
        name="latent_neighbourhood",
    )(proj, proj, proj, ctx_k, ctx_v, bias_t, bounds)


def _neighbourhood_bounds(qn_b, kn_b, rpb):
    n_heads = rpb.shape[0]
    qmax = jnp.broadcast_to(_query_norm_bound(qn_b), (n_heads, 1))
    kmax = jnp.broadcast_to(_query_norm_bound(kn_b) * (HEAD_DIM ** 0.5), (n_heads, 1))
    bmax = jnp.maximum(jnp.max(rpb.reshape(n_heads, -1), axis=1, keepdims=True), 0.0).astype(F32)
    table = jnp.concatenate([qmax, bmax, kmax, jnp.zeros((n_heads, LANES - 3), F32)], axis=1)
    return table.reshape(n_heads // 2, 2, LANES)


def _neighbourhood_bias(rpb, rows):
    nblk = rows // NA_QROWS
    n_dr = 2 * NA_KH - 1
    n_dc = 2 * NA_KW - 1
    kc = np.arange(GRID_W)[:, None]
    qc = np.arange(GRID_W)[None, :]
    ws = np.clip(qc - NA_KW // 2, 0, GRID_W - NA_KW)
    col_ok = (kc >= ws) & (kc < ws + NA_KW)
    dc = np.clip(kc - qc + NA_KW - 1, 0, n_dc - 1)
    dc_onehot = (dc[None] == np.arange(n_dc)[:, None, None]).astype(np.float32)
    tiles = jnp.einsum('hab,bkq->hakq', rpb.astype(F32), jnp.asarray(dc_onehot),
                       precision=lax.Precision.HIGHEST)
    tiles = jnp.where(jnp.asarray(col_ok)[None, None], tiles, MASKED)
    masked_tile = jnp.full((rpb.shape[0], 1, GRID_W, GRID_W), MASKED, F32)
    tiles = jnp.concatenate([tiles, masked_tile], axis=1)
    tile_of = np.zeros((3, NA_KROWS, NA_QROWS), np.int32)
    for v, i in enumerate((0, 1, nblk - 1)):
        r0 = i * NA_QROWS
        ks = int(np.clip(r0 - NA_KH // 2, 0, rows - NA_KROWS))
        for kl in range(NA_KROWS):
            for ql in range(NA_QROWS):
                kr, qr = ks + kl, r0 + ql
                rs = int(np.clip(qr - NA_KH // 2, 0, rows - NA_KH))
                ok = rs <= kr < rs + NA_KH
                tile_of[v, kl, ql] = (kr - qr + NA_KH - 1) if ok else n_dr
    n_heads, n_tiles = tiles.shape[:2]
    return pl.pallas_call(
        _bias_table_kernel,
        out_shape=jax.ShapeDtypeStruct((3, n_heads, NA_TK, NA_TQ), F32),
        grid_spec=pltpu.PrefetchScalarGridSpec(
            num_scalar_prefetch=1, grid=(3, n_heads),
            in_specs=[pl.BlockSpec((1, n_tiles, GRID_W, GRID_W), lambda v, h, tile_of_ref: (h, 0, 0, 0))],
            out_specs=pl.BlockSpec((1, 1, NA_TK, NA_TQ), lambda v, h, tile_of_ref: (v, h, 0, 0))),
        compiler_params=_cparams(("arbitrary", "arbitrary")),
        name="neighbourhood_bias_table",
    )(jnp.asarray(tile_of.reshape(-1)), tiles)


def _bias_table_kernel(tile_of_ref, tiles_ref, o_ref):
    v = pl.program_id(0)
    for kl in range(NA_KROWS):
        row = [tiles_ref[0, tile_of_ref[(v * NA_KROWS + kl) * NA_QROWS + ql]] for ql in range(NA_QROWS)]
        o_ref[0, 0, kl * GRID_W:(kl + 1) * GRID_W, :] = jnp.concatenate(row, axis=1)


def _merge_kernel(xp_ref, oap_ref, obp_ref, xs_ref, oas_ref, obs_ref, ona_ref, onb_ref, wo_ref,
                  g1_ref, sh2_ref, sc2_ref, n2_ref, wrh_ref, wrl_ref, rb_ref, tri_ref,
                  y_ref, hp_ref, gates_ref, rank_ref, count_ref, *, ctx_tiles):
    i = pl.program_id(0)

    @pl.when(i == 0)
    def _():
        count_ref[...] = jnp.zeros_like(count_ref)

    def one_stream(x_ref, oa_ref, ob_ref):
        na = (_rms(oa_ref[...]) * ona_ref[...]).astype(BF16)
        nb = (_rms(ob_ref[...]) * onb_ref[...]).astype(BF16)
        mix = _dot(na, wo_ref[0:WIDTH_A, :]) + _dot(nb, wo_ref[WIDTH_A:WIDTH_A + WIDTH_B, :])
        y = x_ref[...] + g1_ref[0] * mix
        y_ref[...] = y
        h = _rms(y) * n2_ref[...]
        h = h * (1.0 + sc2_ref[0]) + sh2_ref[0]
        hp_ref[...] = _pack_rows(h)
        gates, chosen = _router_gates(h, wrh_ref[...], wrl_ref[...], rb_ref[...])
        gates_ref[...] = gates
        before = _dot(chosen.astype(BF16), tri_ref[...])
        seen = count_ref[...]
        rank_ref[...] = jnp.where(chosen > 0.0, before + seen[:, 0:1], -1.0)
        count_ref[...] = seen + jnp.sum(chosen, axis=1, keepdims=True)

    pl.when(i < ctx_tiles)(lambda: one_stream(xp_ref, oap_ref, obp_ref))
    pl.when(i >= ctx_tiles)(lambda: one_stream(xs_ref, oas_ref, obs_ref))


def _merge(ctx, lat, on_a, on_b, w_out_bf, gate1, shift2, scale2, norm2, wr_hi, wr_lo, rbias, *, tm, lat_seq):
    t_c, d = ctx[0].shape
    t_l = lat[0].shape[0]
    t = t_c + t_l
    tri = jnp.asarray(np.triu(np.ones((tm, tm), np.float32), k=1), BF16)
    ctx_tiles = t_c // tm
    lat_tiles_per_batch = lat_seq // tm

    def ctx_map(i):
        return (jnp.minimum(i, ctx_tiles - 1), 0)

    def lat_map(i):
        return (jnp.maximum(i - ctx_tiles, 0), 0)

    def mod_map(i):
        return (jnp.where(i < ctx_tiles, 0, 1 + (i - ctx_tiles) // lat_tiles_per_batch), 0, 0)

    def stream_specs(index_map):
        return [pl.BlockSpec((tm, d), index_map),
                pl.BlockSpec((tm, WIDTH_A), index_map),
                pl.BlockSpec((tm, WIDTH_B), index_map)]

    return pl.pallas_call(
        functools.partial(_merge_kernel, ctx_tiles=ctx_tiles),
        out_shape=[jax.ShapeDtypeStruct((t, d), F32),
                   jax.ShapeDtypeStruct((t, d // 2), jnp.int32),
                   jax.ShapeDtypeStruct((N_EXPERTS, t), F32),
                   jax.ShapeDtypeStruct((N_EXPERTS, t), F32),
                   jax.ShapeDtypeStruct((N_EXPERTS, LANES), F32)],
        grid=(t // tm,),
        in_specs=stream_specs(ctx_map) + stream_specs(lat_map) + [
            pl.BlockSpec((1, WIDTH_A), lambda i: (0, 0)),
            pl.BlockSpec((1, WIDTH_B), lambda i: (0, 0)),
            pl.BlockSpec((WIDTH_A + WIDTH_B, d), lambda i: (0, 0)),
            pl.BlockSpec((1, 1, d), mod_map),
            pl.BlockSpec((1, 1, d), mod_map),
            pl.BlockSpec((1, 1, d), mod_map),
            pl.BlockSpec((1, d), lambda i: (0, 0)),
            pl.BlockSpec((N_EXPERTS, d), lambda i: (0, 0)),
            pl.BlockSpec((N_EXPERTS, d), lambda i: (0, 0)),
            pl.BlockSpec((N_EXPERTS, 1), lambda i: (0, 0)),
            pl.BlockSpec((tm, tm), lambda i: (0, 0))],
        out_specs=[pl.BlockSpec((tm, d), lambda i: (i, 0)),
                   pl.BlockSpec((tm, d // 2), lambda i: (i, 0)),
                   pl.BlockSpec((N_EXPERTS, tm), lambda i: (0, i)),
                   pl.BlockSpec((N_EXPERTS, tm), lambda i: (0, i)),
                   pl.BlockSpec((N_EXPERTS, LANES), lambda i: (0, 0))],
        compiler_params=_cparams(("arbitrary",)),
        name="merge_route",
    )(*ctx, *lat, on_a, on_b, w_out_bf, gate1, shift2, scale2, norm2, wr_hi, wr_lo, rbias, tri)


def _first_index_of_max(x, iota):
    mx = jnp.max(x, axis=0, keepdims=True)
    idx = jnp.min(jnp.where(x == mx, iota, float(x.shape[0])), axis=0, keepdims=True)
    return mx, iota == idx


def _router_gates(h, wr_hi, wr_lo, rbias):
    h_hi = h.astype(BF16)
    h_lo = (h - h_hi.astype(F32)).astype(BF16)
    logits = _dot_nt(wr_hi, h_hi) + (_dot_nt(wr_lo, h_hi) + _dot_nt(wr_hi, h_lo))
    scores = _sigmoid(logits)
    sel = scores + rbias
    tm = sel.shape[1]
    iota_g = lax.broadcasted_iota(jnp.int32, (GROUP_SIZE, tm), 0).astype(F32)
    group_scores = []
    for g in range(N_GROUPS):
        grp = sel[g * GROUP_SIZE:(g + 1) * GROUP_SIZE]
        m1, first = _first_index_of_max(grp, iota_g)
        m2 = jnp.max(jnp.where(first, -jnp.inf, grp), axis=0, keepdims=True)
        group_scores.append(m1 + m2)
    gs = jnp.concatenate(group_scores, axis=0)
    iota_n = lax.broadcasted_iota(jnp.int32, (N_GROUPS, tm), 0).astype(F32)
    group_on = jnp.zeros((N_GROUPS, tm), F32)
    for _ in range(TOPK_GROUPS):
        _, pick = _first_index_of_max(gs, iota_n)
        group_on = jnp.where(pick, 1.0, group_on)
        gs = jnp.where(pick, -jnp.inf, gs)
    expert_on = jnp.concatenate(
        [jnp.broadcast_to(group_on[g:g + 1], (GROUP_SIZE, tm)) for g in range(N_GROUPS)], axis=0)
    cand = jnp.where(expert_on > 0.0, sel, -jnp.inf)
    iota_e = lax.broadcasted_iota(jnp.int32, (N_EXPERTS, tm), 0).astype(F32)
    w = jnp.zeros((N_EXPERTS, tm), F32)
    chosen = jnp.zeros((N_EXPERTS, tm), F32)
    for _ in range(TOP_K):
        _, pick = _first_index_of_max(cand, iota_e)
        w = jnp.where(pick, scores, w)
        chosen = jnp.where(pick, 1.0, chosen)
        cand = jnp.where(pick, -jnp.inf, cand)
    return w / jnp.sum(w, axis=0, keepdims=True) * ROUTED_SCALE, chosen


MOE_TS = 1024
MOE_ROUTE_TM = 1024
MOE_ROW_TM = 512


def _slots_kernel(gates_ref, rank_ref, off_ref, pos_ref, gtok_ref):
    gates = gates_ref[...]
    rank = rank_ref[...]
    tm = gates.shape[1]
    slot = off_ref[...] + rank
    left = jnp.where(rank >= 0.0, 1.0, 0.0)
    iota_e = lax.broadcasted_iota(jnp.int32, (N_EXPERTS, tm), 0).astype(F32)
    pos_rows, gate_rows = [], []
    for _ in range(TOP_K):
        _, pick = _first_index_of_max(left, iota_e)
        pos_rows.append(jnp.sum(jnp.where(pick, slot, 0.0), axis=0, keepdims=True))
        gate_rows.append(jnp.sum(jnp.where(pick, gates, 0.0), axis=0, keepdims=True))
        left = jnp.where(pick, 0.0, left)
    pos_ref[...] = jnp.concatenate(pos_rows, axis=0).astype(jnp.int32)
    pad = jnp.zeros((LANES - TOP_K, tm), F32)
    gtok_ref[...] = jnp.concatenate(gate_rows + [pad], axis=0).T


def _slots(gates_t, rank_t, off):
    t = gates_t.shape[1]
    tm = MOE_ROUTE_TM
    return pl.pallas_call(
        _slots_kernel,
        out_shape=[jax.ShapeDtypeStruct((TOP_K, t), jnp.int32), jax.ShapeDtypeStruct((t, LANES), F32)],
        grid=(t // tm,),
        in_specs=[pl.BlockSpec((N_EXPERTS, tm), lambda i: (0, i)),
                  pl.BlockSpec((N_EXPERTS, tm), lambda i: (0, i)),
                  pl.BlockSpec((N_EXPERTS, 1), lambda i: (0, 0))],
        out_specs=[pl.BlockSpec((TOP_K, tm), lambda i: (0, i)),
                   pl.BlockSpec((tm, LANES), lambda i: (i, 0))],
        compiler_params=_cparams(("arbitrary",)),
        name="moe_slots",
    )(gates_t, rank_t, off)


SC_CORES = 2
SC_SUBCORES = 16
SC_ROWS = 64


def _dispatch(hp_all, slot_of):
    t, width = hp_all.shape
    n_slots = slot_of.shape[0]
    n_pad = n_slots - TOP_K * t
    workers = SC_CORES * SC_SUBCORES
    per_worker = t // workers
    pad_per_worker = n_pad // workers
    assert per_worker * workers == t and per_worker % SC_ROWS == 0
    assert pad_per_worker * workers == n_pad and pad_per_worker % SC_ROWS == 0
    mesh = plsc.VectorSubcoreMesh(core_axis_name="core", subcore_axis_name="subcore")

    assert pad_per_worker % (TOP_K * SC_ROWS) == 0

    @functools.partial(
        pl.kernel, mesh=mesh,
        out_type=jax.ShapeDtypeStruct((n_slots, width), jnp.int32),
        scratch_types=[pltpu.VMEM((SC_ROWS,), jnp.int32) for _ in range(TOP_K)]
        + [pltpu.VMEM((SC_ROWS, width), jnp.int32), pltpu.SemaphoreType.DMA, pltpu.SemaphoreType.DMA],
    )
    def scatter_rows(h_hbm, slot_hbm, out_hbm, *scratch):
        idx = scratch[:TOP_K]
        rows_v, sem_idx, sem_rows = scratch[TOP_K:]
        worker = lax.axis_index("subcore") * SC_CORES + lax.axis_index("core")

        def scatter_group(index_starts):
            for k, start in enumerate(index_starts):
                pltpu.async_copy(slot_hbm.at[pl.ds(start, SC_ROWS)], idx[k], sem_idx)
            for k, start in enumerate(index_starts):
                pltpu.make_async_copy(slot_hbm.at[pl.ds(start, SC_ROWS)], idx[k], sem_idx).wait()
            for k in range(TOP_K):
                pltpu.async_copy(rows_v, out_hbm.at[idx[k]], sem_rows)
            for k in range(TOP_K):
                pltpu.make_async_copy(rows_v, out_hbm.at[idx[k]], sem_rows).wait()

        pltpu.sync_copy(h_hbm.at[pl.ds(0, SC_ROWS)], rows_v)
        pad_base = TOP_K * t + worker * pad_per_worker

        @pl.loop(0, pad_per_worker // (TOP_K * SC_ROWS))
        def _(j):
            first = pad_base + j * (TOP_K * SC_ROWS)
            scatter_group([first + k * SC_ROWS for k in range(TOP_K)])

        base = worker * per_worker

        @pl.loop(0, per_worker // SC_ROWS)
        def _(j):
            first = base + j * SC_ROWS
            pltpu.sync_copy(h_hbm.at[pl.ds(first, SC_ROWS)], rows_v)
            scatter_group([k * t + first for k in range(TOP_K)])

    return scatter_rows(hp_all, slot_of)


def _experts_kernel(te_ref, xs_ref, wg_ref, wu_ref, wd_ref, ys_ref, wgu_bf, wd_bf):
    i = pl.program_id(0)

    @pl.when((i == 0) | (te_ref[i] != te_ref[jnp.maximum(i, 1) - 1]))
    def _():
        wgu_bf[:, 0:D_EXPERT] = wg_ref[0].astype(BF16)
        wgu_bf[:, D_EXPERT:2 * D_EXPERT] = wu_ref[0].astype(BF16)
        wd_bf[...] = wd_ref[0].astype(BF16)

    left, right = _unpack_rows(xs_ref[...])
    x = jnp.concatenate([left, right], axis=1).astype(BF16)
    gu = _dot(x, wgu_bf[...])
    g = gu[:, 0:D_EXPERT]
    u = gu[:, D_EXPERT:2 * D_EXPERT]
    act = (g * _sigmoid(g)) * u
    ys_ref[...] = _pack_rows(_dot(act.astype(BF16), wd_bf[...]))


def _experts(xs, tile_expert, w_gate, w_up, w_down):
    n_slots, width = xs.shape
    d = 2 * width
    ts = MOE_TS
    return pl.pallas_call(
        _experts_kernel,
        out_shape=jax.ShapeDtypeStruct((n_slots, width), jnp.int32),
        grid_spec=pltpu.PrefetchScalarGridSpec(
            num_scalar_prefetch=1,
            grid=(n_slots // ts,),
            in_specs=[pl.BlockSpec((ts, width), lambda i, te: (i, 0)),
                      pl.BlockSpec((1, d, D_EXPERT), lambda i, te: (te[i], 0, 0)),
                      pl.BlockSpec((1, d, D_EXPERT), lambda i, te: (te[i], 0, 0)),
                      pl.BlockSpec((1, D_EXPERT, d), lambda i, te: (te[i], 0, 0))],
            out_specs=pl.BlockSpec((ts, width), lambda i, te: (i, 0)),
            scratch_shapes=[pltpu.VMEM((d, 2 * D_EXPERT), BF16), pltpu.VMEM((D_EXPERT, d), BF16)]),
        compiler_params=_cparams(("arbitrary",)),
        name="moe_experts",
    )(tile_expert, xs, w_gate, w_up, w_down)


def _gather_slots(y_slots, slot_of, t):
    width = y_slots.shape[1]
    workers = SC_CORES * SC_SUBCORES
    per_worker = t // workers
    n_blocks = (per_worker // SC_ROWS) * TOP_K
    assert per_worker * workers == t and per_worker % SC_ROWS == 0 and n_blocks % 2 == 0
    mesh = plsc.VectorSubcoreMesh(core_axis_name="core", subcore_axis_name="subcore")

    @functools.partial(
        pl.kernel, mesh=mesh,
        out_type=jax.ShapeDtypeStruct((TOP_K * t, width), jnp.int32),
        scratch_types=[pltpu.VMEM((SC_ROWS,), jnp.int32), pltpu.VMEM((SC_ROWS,), jnp.int32),
                       pltpu.VMEM((SC_ROWS, width), jnp.int32), pltpu.VMEM((SC_ROWS, width), jnp.int32),
                       pltpu.SemaphoreType.DMA, pltpu.SemaphoreType.DMA],
    )
    def gather_rows(ys_hbm, slot_hbm, out_hbm, idx0, idx1, rows0, rows1, sem0, sem1):
        worker = lax.axis_index("subcore") * SC_CORES + lax.axis_index("core")
        base = worker * per_worker

        def first_row(n):
            return (n % TOP_K) * t + base + (n // TOP_K) * SC_ROWS

        def start(n, idx_v, rows_v, sem):
            pltpu.sync_copy(slot_hbm.at[pl.ds(first_row(n), SC_ROWS)], idx_v)
            pltpu.async_copy(ys_hbm.at[idx_v], rows_v, sem)

        def finish(n, idx_v, rows_v, sem):
            pltpu.make_async_copy(ys_hbm.at[idx_v], rows_v, sem).wait()
            pltpu.sync_copy(rows_v, out_hbm.at[pl.ds(first_row(n), SC_ROWS)])

        start(0, idx0, rows0, sem0)

        @pl.loop(0, n_blocks, step=2)
        def _(n):
            start(n + 1, idx1, rows1, sem1)
            finish(n, idx0, rows0, sem0)

            @pl.when(n + 2 < n_blocks)
            def _():
                start(n + 2, idx0, rows0, sem0)

            finish(n + 1, idx1, rows1, sem1)

    return gather_rows(y_slots, slot_of)


def _combine_kernel(y_ref, h_ref, g2_ref, gtok_ref, rows_ref, wgs_ref, wus_ref, wds_ref, o_ref):
    h_left, h_right = _unpack_rows(h_ref[...])
    h = jnp.concatenate([h_left, h_right], axis=1).astype(BF16)
    gs = _dot(h, wgs_ref[...])
    us = _dot(h, wus_ref[...])
    shared = _dot(((gs * _sigmoid(gs)) * us).astype(BF16), wds_ref[...])

    gtok = gtok_ref[...]
    acc_left = acc_right = None
    for k in range(TOP_K):
        left, right = _unpack_rows(rows_ref[k])
        gate = gtok[:, k:k + 1]
        acc_left = gate * left if acc_left is None else acc_left + gate * left
        acc_right = gate * right if acc_right is None else acc_right + gate * right
    routed = jnp.concatenate([acc_left, acc_right], axis=1)
    o_ref[...] = y_ref[...] + g2_ref[0] * (routed + shared)


def _combine(y_all, hp_all, gate2, gtok, rows, wgs, wus, wds, *, first_token, tokens, seq):
    d = y_all.shape[1]
    width = hp_all.shape[1]
    tm = MOE_ROW_TM
    tile0 = first_token // tm
    nb = gate2.shape[0]
    tiles_per_batch = seq // tm

    def mod_map(i):
        return ((i // tiles_per_batch) if nb > 1 else 0, 0, 0)

    return pl.pallas_call(
        _combine_kernel,
        out_shape=jax.ShapeDtypeStruct((tokens, d), F32),
        grid=(tokens // tm,),
        in_specs=[pl.BlockSpec((tm, d), lambda i: (tile0 + i, 0)),
                  pl.BlockSpec((tm, width), lambda i: (tile0 + i, 0)),
                  pl.BlockSpec((1, 1, d), mod_map),
                  pl.BlockSpec((tm, LANES), lambda i: (tile0 + i, 0)),
                  pl.BlockSpec((TOP_K, tm, width), lambda i: (0, tile0 + i, 0)),
                  pl.BlockSpec((d, D_SHARED), lambda i: (0, 0)),
                  pl.BlockSpec((d, D_SHARED), lambda i: (0, 0)),
                  pl.BlockSpec((D_SHARED, d), lambda i: (0, 0))],
        out_specs=pl.BlockSpec((tm, d), lambda i: (i, 0)),
        compiler_params=_cparams(("arbitrary",)),
        name="moe_combine",
    )(y_all, hp_all, gate2, gtok, rows, wgs, wus, wds)


def _expert_layout(counts, n_tiles):
    cnt = counts.astype(jnp.int32)
    tiles = (cnt + (MOE_TS - 1)) // MOE_TS
    last_tile = jnp.cumsum(tiles)
    off = (last_tile - tiles) * MOE_TS
    pad_lo = off + cnt
    pad_hi = (off + tiles * MOE_TS).at[N_EXPERTS - 1].set(n_tiles * MOE_TS)
    pad_cnt = pad_hi - pad_lo
    pad_last = jnp.cumsum(pad_cnt)
    shift = pad_lo - (pad_last - pad_cnt)
    j = jnp.arange(N_EXPERTS * MOE_TS, dtype=jnp.int32)
    past = (pad_last[None, :-1] <= j[:, None]).astype(jnp.int32)
    pad_slots = j + shift[0] + jnp.sum(past * (shift[1:] - shift[:-1])[None, :], axis=1)
    tile_ids = jnp.arange(n_tiles, dtype=jnp.int32)
    tile_expert = jnp.minimum(
        jnp.sum((last_tile[None, :] <= tile_ids[:, None]).astype(jnp.int32), axis=1), N_EXPERTS - 1)
    return off, pad_slots, tile_expert


def _rope_tables(n_tokens):
    t = jnp.arange(n_tokens)
    row = (t // GRID_W).astype(F32)
    col = (t % GRID_W).astype(F32)
    nf = HEAD_DIM // 4
    freqs = ROPE_THETA ** (-jnp.arange(nf, dtype=F32) / nf)
    ang_r = row[:, None] * freqs
    ang_c = col[:, None] * freqs
    cos = jnp.concatenate([jnp.cos(ang_r)] * 2 + [jnp.cos(ang_c)] * 2, axis=1)
    sin = jnp.concatenate([-jnp.sin(ang_r), jnp.sin(ang_r), -jnp.sin(ang_c), jnp.sin(ang_c)], axis=1)
    reps = LANES // HEAD_DIM
    return jnp.tile(cos, (1, reps)), jnp.tile(sin, (1, reps))


def _head_gains(qn_a, kn_a, qn_b, kn_b):
    ones = jnp.ones((HEAD_DIM,), F32)
    parts = ([qn_a] * N_HEADS_A + [kn_a] * N_KV_A + [ones] * N_KV_A
             + [qn_b] * N_HEADS_B + [kn_b] * N_HEADS_B + [ones] * N_HEADS_B)
    return jnp.concatenate(parts).reshape(1, IN_COLS).astype(F32)


def _same_head_indicator():
    i = np.arange(MXU_DIM)
    return jnp.asarray((i[:, None] // HEAD_DIM) == (i[None, :] // HEAD_DIM), BF16)


def _token_major(cache):
    b, h, s, hd = cache.shape
    return cache.transpose(0, 2, 1, 3).reshape(b, s, h * hd).astype(BF16)


def kernel(x_prompt, x_sample, cache_k_a, cache_v_a, cache_k_b, cache_v_b, c, c_ctx, w_mod, b_mod, norm1, norm2, w_in, qn_a, kn_a, qn_b, kn_b, rpb, on_a, on_b, w_out, w_router, router_bias, w_gate_e, w_up_e, w_down_e, w_gate_s, w_up_s, w_down_s):
    depth = w_mod.shape[0]
    assert depth == 1
    l = 0
    bp, sp, d = x_prompt.shape
    bs, ss, _ = x_sample.shape

    cvec = jnp.concatenate([c_ctx[None, :], c], axis=0)
    rows = -(-cvec.shape[0] // 8) * 8
    cvec = jnp.pad(cvec, ((0, rows - cvec.shape[0]), (0, 0)))
    mod = _adaln(cvec, w_mod[l], b_mod[l])
    mod_p = [m.reshape(1, 1, d) for m in jnp.split(mod[0:1], 6, axis=-1)]
    mod_s = [m.reshape(bs, 1, d) for m in jnp.split(mod[1:1 + bs], 6, axis=-1)]
    mod_all = [m.reshape(1 + bs, 1, d) for m in jnp.split(mod[0:1 + bs], 6, axis=-1)]

    w_in_bf = w_in[l].astype(BF16)
    w_out_bf = w_out[l].astype(BF16)
    gain = _head_gains(qn_a[l], kn_a[l], qn_b[l], kn_b[l])
    seg = _same_head_indicator()
    n1 = norm1[l].reshape(1, d)
    n2 = norm2[l].reshape(1, d)
    ona = on_a[l].reshape(1, WIDTH_A)
    onb = on_b[l].reshape(1, WIDTH_B)
    wr_t = w_router[l].T
    wr_hi = wr_t.astype(BF16)
    wr_lo = (wr_t - wr_hi.astype(F32)).astype(BF16)
    rbias = router_bias[l].reshape(N_EXPERTS, 1).astype(F32)
    wgs = w_gate_s[l].astype(BF16)
    wus = w_up_s[l].astype(BF16)
    wds = w_down_s[l].astype(BF16)
    t_p = bp * sp
    t_s = bs * ss
    t_all = t_p + t_s

    xp = x_prompt.reshape(t_p, d)
    proj_p, st_ka, st_va, st_kb, st_vb = _project(
        xp, mod_p[0], mod_p[1], n1, w_in_bf, gain, seg, None, tm=sp, seq=sp, states=True)
    oa_p, ob_p = _context_attention(proj_p, seq=sp)

    xs = x_sample.reshape(t_s, d)
    proj_s, = _project(xs, mod_s[0], mod_s[1], n1, w_in_bf, gain, seg, _rope_tables(ss),
                       tm=1024, seq=ss, states=False)
    oa_s = _latent_gqa(proj_s, _token_major(cache_k_a[:, l]), _token_major(cache_v_a[:, l]),
                       _query_norm_bound(qn_a[l]), seq=ss, tq=256)
    bias_t = _neighbourhood_bias(rpb[l], ss // GRID_W)
    ob_s = _latent_neighbourhood(proj_s, _token_major(cache_k_b[:, l]), _token_major(cache_v_b[:, l]),
                                 bias_t, _neighbourhood_bounds(qn_b[l], kn_b[l], rpb[l]), seq=ss)

    y1_all, hp_all, gates_t, rank_t, counts = _merge(
        (xp, oa_p, ob_p), (xs, oa_s, ob_s), ona, onb, w_out_bf, mod_all[2], mod_all[3], mod_all[4], n2,
        wr_hi, wr_lo, rbias, tm=512, lat_seq=ss)
    n_tiles = t_all * TOP_K // MOE_TS + N_EXPERTS
    off, pad_slots, tile_expert = _expert_layout(counts[:, 0], n_tiles)
    pos, gtok = _slots(gates_t, rank_t, off.astype(F32).reshape(N_EXPERTS, 1))
    slot_of = pos.reshape(TOP_K * t_all)
    x_slots = _dispatch(hp_all, jnp.concatenate([slot_of, pad_slots]))
    y_slots = _experts(x_slots, tile_expert, w_gate_e[l], w_up_e[l], w_down_e[l])
    rows = _gather_slots(y_slots, slot_of, t_all).reshape(TOP_K, t_all, d // 2)
    y_p = _combine(y1_all, hp_all, mod_p[5], gtok, rows, wgs, wus, wds,
                   first_token=0, tokens=t_p, seq=sp)
    y_s = _combine(y1_all, hp_all, mod_s[5], gtok, rows, wgs, wus, wds,
                   first_token=t_p, tokens=t_s, seq=ss)

    return (y_p.reshape(bp, sp, d), y_s.reshape(bs, ss, d), st_ka, st_va, st_kb, st_vb)
```

```python
import functools

import numpy as np
import jax
import jax.numpy as jnp
from jax import lax
from jax.experimental import pallas as pl
from jax.experimental.pallas import tpu as pltpu
from jax.experimental.pallas import tpu_sc as plsc

F32 = jnp.float32
BF16 = jnp.bfloat16

D_MODEL = 1024
HEAD_DIM = 64
N_HEADS_A = 8
N_KV_A = 2
GROUP_A = N_HEADS_A // N_KV_A
N_HEADS_B = 8
WIDTH_A = N_HEADS_A * HEAD_DIM
WIDTH_B = N_HEADS_B * HEAD_DIM
KV_WIDTH_A = N_KV_A * HEAD_DIM
IN_COLS = WIDTH_A + 2 * KV_WIDTH_A + 3 * WIDTH_B
GRID_W = 64
ROPE_THETA = 10000.0
NA_KH = 8
NA_KW = 16
N_EXPERTS = 64
N_GROUPS = 8
GROUP_SIZE = N_EXPERTS // N_GROUPS
TOPK_GROUPS = 4
TOP_K = 8
D_EXPERT = 256
D_SHARED = 256
ROUTED_SCALE = 2.5
EPS = 1e-6

LANES = 128
MXU_DIM = 256
MASKED = -1e30

COL_QA = 0
COL_KA = WIDTH_A
COL_VA = COL_KA + KV_WIDTH_A
COL_QB = COL_VA + KV_WIDTH_A
COL_KB = COL_QB + WIDTH_B
COL_VB = COL_KB + WIDTH_B

NA_QROWS = 8
NA_KROWS = 2 * NA_KH
NA_TQ = NA_QROWS * GRID_W
NA_TK = NA_KROWS * GRID_W
NA_KBLK = 256

VMEM_LIMIT = 56 * 1024 * 1024


def _cparams(sem):
    return pltpu.CompilerParams(dimension_semantics=sem, vmem_limit_bytes=VMEM_LIMIT)


def _dot(a, b):
    return jnp.dot(a, b, preferred_element_type=F32)


def _dot_nt(a, b):
    return lax.dot_general(a, b, (((1,), (1,)), ((), ())), preferred_element_type=F32)


def _sigmoid(x):
    return 1.0 / (1.0 + jnp.exp(-x))


def _rms(x):
    return x * lax.rsqrt(jnp.mean(x * x, axis=-1, keepdims=True) + EPS)


def _pack_rows(x):
    n = x.shape[1] // 2
    hi = lax.bitcast_convert_type(x[:, :n].astype(BF16).astype(F32), jnp.int32)
    lo = lax.bitcast_convert_type(x[:, n:].astype(BF16).astype(F32), jnp.int32)
    return hi | lax.shift_right_logical(lo, 16)


def _unpack_rows(w):
    left = lax.bitcast_convert_type(w & jnp.int32(-65536), F32)
    right = lax.bitcast_convert_type(lax.shift_left(w, 16), F32)
    return left, right


def _mod_kernel(c_ref, w_ref, b_ref, o_ref):
    c = c_ref[...]
    s = c * _sigmoid(c)
    o_ref[...] = jnp.dot(s, w_ref[...], preferred_element_type=F32,
                         precision=lax.Precision.HIGHEST) + b_ref[...]


def _adaln(cvec, w_mod, b_mod):
    rows, d = cvec.shape
    n = w_mod.shape[1]
    tn = 512
    return pl.pallas_call(
        _mod_kernel,
        out_shape=jax.ShapeDtypeStruct((rows, n), F32),
        grid=(n // tn,),
        in_specs=[pl.BlockSpec((rows, d), lambda j: (0, 0)),
                  pl.BlockSpec((d, tn), lambda j: (0, j)),
                  pl.BlockSpec((1, tn), lambda j: (0, j))],
        out_specs=pl.BlockSpec((rows, tn), lambda j: (0, j)),
        compiler_params=_cparams(("arbitrary",)),
        name="adaln_mod",
    )(cvec, w_mod, b_mod.reshape(1, n))


def _proj_chunks():
    def split(c0, width, step, *flags):
        return [(c0 + i, min(step, width - i)) + flags for i in range(0, width, step)]
    return (split(COL_QA, WIDTH_A, MXU_DIM, True, True, True)
            + split(COL_KA, KV_WIDTH_A, MXU_DIM, True, True, False)
            + split(COL_VA, KV_WIDTH_A, MXU_DIM, False, False, False)
            + split(COL_QB, WIDTH_B, MXU_DIM, True, False, True)
            + split(COL_KB, WIDTH_B, MXU_DIM, True, False, False)
            + split(COL_VB, WIDTH_B, MXU_DIM, False, False, False))


_PROJ_CHUNKS = _proj_chunks()


def _proj_kernel(*refs, rope, states):
    x_ref, sh_ref, sc_ref, n1_ref, w_ref, gain_ref, seg_ref = refs[:7]
    pos = 7
    if rope:
        cos_ref, sin_ref = refs[pos:pos + 2]
        pos += 2
    out_ref = refs[pos]
    pos += 1
    if states:
        ka_ref, va_ref, kb_ref, vb_ref = refs[pos:pos + 4]
        state_of = {COL_KA: ka_ref, COL_VA: va_ref, COL_KB: kb_ref, COL_VB: vb_ref}

    x = x_ref[...]
    h = _rms(x) * n1_ref[...]
    h = h * (1.0 + sc_ref[0]) + sh_ref[0]
    p = _dot(h.astype(BF16), w_ref[...])

    for c0, w, normed, roped, is_query in _PROJ_CHUNKS:
        pc = p[:, c0:c0 + w]
        if normed:
            seg = seg_ref[0:w, 0:w]
            sq = pc * pc
            hi = sq.astype(BF16)
            lo = (sq - hi.astype(F32)).astype(BF16)
            ss = _dot(hi, seg) + _dot(lo, seg)
            pc = pc * lax.rsqrt(ss * (1.0 / HEAD_DIM) + EPS) * gain_ref[:, c0:c0 + w]
        if states:
            for start, ref in state_of.items():
                if start <= c0 < start + ref.shape[2] * HEAD_DIM:
                    base = (c0 - start) // HEAD_DIM
                    for hh in range(w // HEAD_DIM):
                        ref[0, 0, base + hh] = pc[:, hh * HEAD_DIM:(hh + 1) * HEAD_DIM]
        if rope and roped:
            reps = w // LANES
            cos = jnp.concatenate([cos_ref[...]] * reps, axis=1) if reps > 1 else cos_ref[...]
            sin = jnp.concatenate([sin_ref[...]] * reps, axis=1) if reps > 1 else sin_ref[...]
            lane = lax.broadcasted_iota(jnp.int32, pc.shape, 1)
            first_half = (lane % (HEAD_DIM // 2)) < (HEAD_DIM // 4)
            partner = jnp.where(first_half,
                                pltpu.roll(pc, w - HEAD_DIM // 4, 1),
                                pltpu.roll(pc, HEAD_DIM // 4, 1))
            pc = pc * cos + partner * sin
        if is_query:
            pc = pc * (HEAD_DIM ** -0.5)
        out_ref[:, c0:c0 + w] = pc.astype(BF16)


def _project(x2d, shift, scale, norm1, w_in_bf, gain, seg, rope_tabs, *, tm, seq, states):
    t, d = x2d.shape
    nb = shift.shape[0]
    tiles_per_batch = seq // tm
    rope = rope_tabs is not None

    def mod_map(i):
        return ((i // tiles_per_batch) if nb > 1 else 0, 0, 0)

    in_specs = [pl.BlockSpec((tm, d), lambda i: (i, 0)),
                pl.BlockSpec((1, 1, d), mod_map),
                pl.BlockSpec((1, 1, d), mod_map),
                pl.BlockSpec((1, d), lambda i: (0, 0)),
                pl.BlockSpec((d, IN_COLS), lambda i: (0, 0)),
                pl.BlockSpec((1, IN_COLS), lambda i: (0, 0)),
                pl.BlockSpec((MXU_DIM, MXU_DIM), lambda i: (0, 0))]
    args = [x2d, shift, scale, norm1, w_in_bf, gain, seg]
    if rope:
        in_specs += [pl.BlockSpec((tm, LANES), lambda i: (i % tiles_per_batch, 0))] * 2
        args += list(rope_tabs)
    out_shape = [jax.ShapeDtypeStruct((t, IN_COLS), BF16)]
    out_specs = [pl.BlockSpec((tm, IN_COLS), lambda i: (i, 0))]
    if states:
        assert tm == seq
        b = t // seq
        for nh in (N_KV_A, N_KV_A, N_HEADS_B, N_HEADS_B):
            out_shape.append(jax.ShapeDtypeStruct((b, 1, nh, seq, HEAD_DIM), F32))
            out_specs.append(pl.BlockSpec((1, 1, nh, seq, HEAD_DIM), lambda i: (i, 0, 0, 0, 0)))
    return pl.pallas_call(
        functools.partial(_proj_kernel, rope=rope, states=states),
        out_shape=out_shape,
        grid=(t // tm,),
        in_specs=in_specs,
        out_specs=out_specs,
        compiler_params=_cparams(("arbitrary",)),
        name="proj_states" if states else "proj_rope",
    )(*args)


def _lane_half(shape):
    return lax.broadcasted_iota(jnp.int32, shape, 1) // HEAD_DIM


def _keep_half(x, half):
    return jnp.where(_lane_half(x.shape) == half, x, jnp.zeros_like(x))


def _transpose_bf16(x):
    return x.astype(F32).T.astype(BF16)


def _attend(q, keys, values_t, biases):
    return _softmax_av(_scores(q, keys, biases), values_t)


def _scores(q, keys, biases):
    scores = []
    for k, b in zip(keys, biases):
        s = _dot_nt(k, q)
        if b is not None:
            s = s + b
        scores.append(s)
    return scores


def _softmax_av(scores, values_t):
    m = functools.reduce(jnp.maximum, [jnp.max(s, axis=0, keepdims=True) for s in scores])
    denom = None
    out = None
    for s, vt in zip(scores, values_t):
        p = jnp.exp(s - m)
        ps = jnp.sum(p, axis=0, keepdims=True)
        po = _dot(vt, p.astype(BF16))
        denom = ps if denom is None else denom + ps
        out = po if out is None else out + po
    return out / denom


def _swap_halves(q_bf16):
    return pltpu.roll(q_bf16.astype(F32), HEAD_DIM, 1).astype(BF16)


def _gqa_heads(q_of_pair, keys_by_group, values_t):
    outs = []
    for h in range(N_HEADS_A):
        g = h // GROUP_A
        q = q_of_pair(h // 2)
        if h % 2 != g:
            q = _swap_halves(q)
        o = _attend(q, keys_by_group[g], values_t, [None] * len(values_t))
        outs.append(o[g * HEAD_DIM:(g + 1) * HEAD_DIM])
    return jnp.concatenate(outs, axis=0)


def _ctx_attn_kernel(p_ref, oa_ref, ob_ref):
    ka = p_ref[:, COL_KA:COL_KA + LANES]
    va_t = [_transpose_bf16(p_ref[:, COL_VA:COL_VA + LANES])]
    keys_by_group = [[_keep_half(ka, g)] for g in range(N_KV_A)]
    oa = _gqa_heads(lambda i: p_ref[:, COL_QA + i * LANES:COL_QA + (i + 1) * LANES],
                    keys_by_group, va_t)
    oa_ref[...] = oa.T

    outs = []
    for i in range(N_HEADS_B // 2):
        q = p_ref[:, COL_QB + i * LANES:COL_QB + (i + 1) * LANES]
        k = p_ref[:, COL_KB + i * LANES:COL_KB + (i + 1) * LANES]
        vt = [_transpose_bf16(p_ref[:, COL_VB + i * LANES:COL_VB + (i + 1) * LANES])]
        for half in range(2):
            o = _attend(q, [_keep_half(k, half)], vt, [None])
            outs.append(o[half * HEAD_DIM:(half + 1) * HEAD_DIM])
    ob_ref[...] = jnp.concatenate(outs, axis=0).T


def _context_attention(proj, *, seq):
    t = proj.shape[0]
    return pl.pallas_call(
        _ctx_attn_kernel,
        out_shape=[jax.ShapeDtypeStruct((t, WIDTH_A), F32), jax.ShapeDtypeStruct((t, WIDTH_B), F32)],
        grid=(t // seq,),
        in_specs=[pl.BlockSpec((seq, IN_COLS), lambda i: (i, 0))],
        out_specs=[pl.BlockSpec((seq, WIDTH_A), lambda i: (i, 0)),
                   pl.BlockSpec((seq, WIDTH_B), lambda i: (i, 0))],
        compiler_params=_cparams(("arbitrary",)),
        name="context_attention",
    )(proj)


ATTN_SAFE_SHIFT = 40.0
ONES_ROWS = 16


def _round_up_bf16(x):
    return (x * (1.0 + 2.0 ** -6)).astype(BF16).astype(F32)


def _query_norm_bound(gain):
    return jnp.max(jnp.abs(gain)).reshape(1, 1).astype(F32)


def _ones_lane(g):
    return (1 - g) * HEAD_DIM


def _gqa_latent_kernel(q_ref, k_ref, v_ref, ck_ref, cv_ref, qmax_ref, o_ref,
                       kg_ref, ckg_ref, vt_ref, cvt_ref, shift_ref):
    lane_k = lax.broadcasted_iota(jnp.int32, (1, LANES), 1)

    @pl.when(pl.program_id(1) == 0)
    def _():
        k = k_ref[...]
        ck = ck_ref[0]
        vt = v_ref[...].astype(F32).T
        cvt = cv_ref[0].astype(F32).T
        for g in range(N_KV_A):
            kf = _keep_half(k, g).astype(F32)
            ckf = _keep_half(ck, g).astype(F32)
            ksq = jnp.maximum(jnp.max(jnp.sum(kf * kf, axis=1, keepdims=True), axis=0, keepdims=True),
                              jnp.max(jnp.sum(ckf * ckf, axis=1, keepdims=True), axis=0, keepdims=True))
            shift_ref[g] = jnp.broadcast_to(_round_up_bf16(qmax_ref[...] * jnp.sqrt(ksq)), shift_ref.shape[1:])
            kg_ref[g] = jnp.where(lane_k == _ones_lane(g), 1.0, kf).astype(BF16)
            ckg_ref[g] = jnp.where(lane_k == _ones_lane(g), 1.0, ckf).astype(BF16)
            rows = slice(g * HEAD_DIM, (g + 1) * HEAD_DIM)
            vt_ref[g] = jnp.concatenate([vt[rows], jnp.ones((ONES_ROWS, vt.shape[1]), F32)], axis=0).astype(BF16)
            cvt_ref[g] = jnp.concatenate([cvt[rows], jnp.ones((ONES_ROWS, cvt.shape[1]), F32)], axis=0).astype(BF16)

    tq = q_ref.shape[0]
    lane_q = lax.broadcasted_iota(jnp.int32, (GROUP_A * tq, LANES), 1)
    queries, shifts = [], []
    for g in range(N_KV_A):
        qs = []
        for j in range(GROUP_A):
            h = g * GROUP_A + j
            q = q_ref[:, (h // 2) * LANES:(h // 2 + 1) * LANES].astype(F32)
            qs.append(q if h % 2 == g else pltpu.roll(q, HEAD_DIM, 1))
        queries.append(jnp.where(lane_q // HEAD_DIM == g, jnp.concatenate(qs, axis=0), 0.0))
        shifts.append(shift_ref[g][0:1, 0:1])
    safe = jnp.max(jnp.maximum(shift_ref[0], shift_ref[1])) <= ATTN_SAFE_SHIFT

    def attend(g, p_lat, p_ctx):
        half = p_lat.shape[1] // 2
        o = jnp.concatenate([_dot(vt_ref[g], p_lat[:, :half]) + _dot(cvt_ref[g], p_ctx[:, :half]),
                             _dot(vt_ref[g], p_lat[:, half:]) + _dot(cvt_ref[g], p_ctx[:, half:])],
                            axis=1)
        o = o[:HEAD_DIM] / o[HEAD_DIM:HEAD_DIM + 1]
        heads = jnp.concatenate([o[:, j * tq:(j + 1) * tq] for j in range(GROUP_A)], axis=0)
        o_ref[:, g * GROUP_A * HEAD_DIM:(g + 1) * GROUP_A * HEAD_DIM] = heads.T

    def with_bound():
        for g in range(N_KV_A):
            qa = jnp.where(lane_q == _ones_lane(g), -shifts[g], queries[g]).astype(BF16)
            attend(g, jnp.exp(_dot_nt(kg_ref[g], qa)).astype(BF16), jnp.exp(_dot_nt(ckg_ref[g], qa)).astype(BF16))

    def with_row_max():
        for g in range(N_KV_A):
            qa = queries[g].astype(BF16)
            s_lat = _dot_nt(kg_ref[g], qa)
            s_ctx = _dot_nt(ckg_ref[g], qa)
            m = jnp.maximum(jnp.max(s_lat, axis=0, keepdims=True), jnp.max(s_ctx, axis=0, keepdims=True))
            attend(g, jnp.exp(s_lat - m).astype(BF16), jnp.exp(s_ctx - m).astype(BF16))

    pl.when(safe)(with_bound)
    pl.when(jnp.logical_not(safe))(with_row_max)


def _latent_gqa(proj, ctx_k, ctx_v, qmax, *, seq, tq):
    t = proj.shape[0]
    b = t // seq
    nq = seq // tq
    past = ctx_k.shape[1]
    return pl.pallas_call(
        _gqa_latent_kernel,
        out_shape=jax.ShapeDtypeStruct((t, WIDTH_A), F32),
        grid=(b, nq),
        in_specs=[pl.BlockSpec((tq, WIDTH_A), lambda bi, qi: (bi * nq + qi, 0)),
                  pl.BlockSpec((seq, LANES), lambda bi, qi: (bi, COL_KA // LANES)),
                  pl.BlockSpec((seq, LANES), lambda bi, qi: (bi, COL_VA // LANES)),
                  pl.BlockSpec((1, past, LANES), lambda bi, qi: (bi, 0, 0)),
                  pl.BlockSpec((1, past, LANES), lambda bi, qi: (bi, 0, 0)),
                  pl.BlockSpec((1, 1), lambda bi, qi: (0, 0))],
        out_specs=pl.BlockSpec((tq, WIDTH_A), lambda bi, qi: (bi * nq + qi, 0)),
        scratch_shapes=[pltpu.VMEM((N_KV_A, seq, LANES), BF16),
                        pltpu.VMEM((N_KV_A, past, LANES), BF16),
                        pltpu.VMEM((N_KV_A, HEAD_DIM + ONES_ROWS, seq), BF16),
                        pltpu.VMEM((N_KV_A, HEAD_DIM + ONES_ROWS, past), BF16),
                        pltpu.VMEM((N_KV_A, 8, LANES), F32)],
        compiler_params=_cparams(("arbitrary", "arbitrary")),
        name="latent_gqa",
    )(proj, proj, proj, ctx_k, ctx_v, qmax)


def _na_kernel(q_ref, k_ref, v_ref, ck_ref, cv_ref, bias_ref, bound_ref, o_ref, keys_ref, vt_ref, shift_ref,
               *, rows):
    i = pl.program_id(2)
    n_kblk = k_ref.shape[0] // NA_KBLK
    lane_k = lax.broadcasted_iota(jnp.int32, (1, LANES), 1)
    one = jnp.ones((), BF16)

    @pl.when(i == 0)
    def _():
        k = k_ref[...]
        ck = ck_ref[0]
        for half in range(2):
            kh = jnp.where(lane_k == _ones_lane(half), one, _keep_half(k, half))
            keys_ref[half, 0:n_kblk] = kh.reshape(n_kblk, NA_KBLK, LANES)
            ckh = _keep_half(ck, half)
            keys_ref[half, n_kblk] = jnp.where(lane_k == _ones_lane(half), one, ckh)
            ckf = ckh.astype(F32)
            ctx_norm = jnp.sqrt(jnp.max(jnp.sum(ckf * ckf, axis=1, keepdims=True), axis=0, keepdims=True))
            consts = bound_ref[0, half:half + 1, :]
            kmax = jnp.maximum(ctx_norm, consts[:, 2:3])
            shift_ref[half] = jnp.broadcast_to(_round_up_bf16(consts[:, 0:1] * kmax + consts[:, 1:2]),
                                               shift_ref.shape[1:])
        ones_rows = jnp.ones((ONES_ROWS, NA_KBLK), F32)
        vt = v_ref[...].astype(F32).T
        for j in range(n_kblk):
            vt_ref[j] = jnp.concatenate([vt[:, j * NA_KBLK:(j + 1) * NA_KBLK], ones_rows], axis=0).astype(BF16)
        vt_ref[n_kblk] = jnp.concatenate([cv_ref[0].astype(F32).T, ones_rows], axis=0).astype(BF16)

    q = q_ref[...]
    lane_q = lax.broadcasted_iota(jnp.int32, q.shape, 1)
    first = _na_first_key_block(i, rows)
    n_qblk = rows // NA_QROWS
    variant = jnp.where(i == 0, 0, jnp.where(i == n_qblk - 1, 2, 1))
    blocks = [first + j for j in range(NA_TK // NA_KBLK)] + [n_kblk]
    values_t = [vt_ref[blk] for blk in blocks]
    heads = []
    for half in range(2):
        keys = [keys_ref[half, blk] for blk in blocks]
        biases = [bias_ref[variant, half, j * NA_KBLK:(j + 1) * NA_KBLK, :] for j in range(NA_TK // NA_KBLK)] + [None]
        heads.append((shift_ref[half][0:1, 0:1], keys, biases))
    safe = jnp.max(jnp.maximum(shift_ref[0], shift_ref[1])) <= ATTN_SAFE_SHIFT

    def attend(probabilities):
        outs = []
        for half, ps in enumerate(probabilities):
            o = functools.reduce(lambda a, b: a + b, [_dot(vt, p) for vt, p in zip(values_t, ps)])
            outs.append(o[half * HEAD_DIM:(half + 1) * HEAD_DIM] / o[2 * HEAD_DIM:2 * HEAD_DIM + 1])
        o_ref[...] = jnp.concatenate(outs, axis=0).T

    def with_bound():
        probabilities = []
        for half, (shift, keys, biases) in enumerate(heads):
            qa = jnp.where(lane_q == _ones_lane(half), (-shift).astype(BF16), _keep_half(q, half))
            probabilities.append([jnp.exp(s).astype(BF16) for s in _scores(qa, keys, biases)])
        attend(probabilities)

    def with_row_max():
        all_scores = [_scores(_keep_half(q, half), keys, biases) for half, (_, keys, biases) in enumerate(heads)]
        probabilities = []
        for scores in all_scores:
            m = functools.reduce(jnp.maximum, [jnp.max(s, axis=0, keepdims=True) for s in scores])
            probabilities.append([jnp.exp(s - m).astype(BF16) for s in scores])
        attend(probabilities)

    pl.when(safe)(with_bound)
    pl.when(jnp.logical_not(safe))(with_row_max)


def _na_first_key_block(i, rows):
    per_qblock = NA_QROWS * GRID_W // NA_KBLK
    lead = (NA_KH // 2) * GRID_W // NA_KBLK
    return jnp.clip(per_qblock * i - lead, 0, (rows - NA_KROWS) * GRID_W // NA_KBLK)


def _latent_neighbourhood(proj, ctx_k, ctx_v, bias_t, bounds, *, seq):
    t = proj.shape[0]
    b = t // seq
    rows = seq // GRID_W
    nblk = rows // NA_QROWS
    n_kblk = seq // NA_KBLK
    past = ctx_k.shape[1]
    assert past == NA_KBLK
    grid = (N_HEADS_B // 2, b, nblk)
    in_specs = [pl.BlockSpec((NA_TQ, LANES), lambda hp, bi, i: (bi * nblk + i, COL_QB // LANES + hp)),
                pl.BlockSpec((seq, LANES), lambda hp, bi, i: (bi, COL_KB // LANES + hp)),
                pl.BlockSpec((seq, LANES), lambda hp, bi, i: (bi, COL_VB // LANES + hp)),
                pl.BlockSpec((1, past, LANES), lambda hp, bi, i: (bi, 0, hp)),
                pl.BlockSpec((1, past, LANES), lambda hp, bi, i: (bi, 0, hp)),
                pl.BlockSpec((3, 2, NA_TK, NA_TQ), lambda hp, bi, i: (0, hp, 0, 0)),
                pl.BlockSpec((1, 2, LANES), lambda hp, bi, i: (hp, 0, 0))]
    return pl.pallas_call(
        functools.partial(_na_kernel, rows=rows),
        out_shape=jax.ShapeDtypeStruct((t, WIDTH_B), F32),
        grid=grid,
        in_specs=in_specs,
        out_specs=pl.BlockSpec((NA_TQ, LANES), lambda hp, bi, i: (bi * nblk + i, hp)),
        scratch_shapes=[pltpu.VMEM((2, n_kblk + 1, NA_KBLK, LANES), BF16),
                        pltpu.VMEM((n_kblk + 1, 2 * HEAD_DIM + ONES_ROWS, NA_KBLK), BF16),
                        pltpu.VMEM((2, 8, LANES), F32)],
        compiler_params=_cparams(("arbitrary", "arbitrary", "arbitrary")),
        name="latent_neighbourhood",
    )(proj, proj, proj, ctx_k, ctx_v, bias_t, bounds)


def _neighbourhood_bounds(qn_b, kn_b, rpb):
    n_heads = rpb.shape[0]
    qmax = jnp.broadcast_to(_query_norm_bound(qn_b), (n_heads, 1))
    kmax = jnp.broadcast_to(_query_norm_bound(kn_b) * (HEAD_DIM ** 0.5), (n_heads, 1))
    bmax = jnp.maximum(jnp.max(rpb.reshape(n_heads, -1), axis=1, keepdims=True), 0.0).astype(F32)
    table = jnp.concatenate([qmax, bmax, kmax, jnp.zeros((n_heads, LANES - 3), F32)], axis=1)
    return table.reshape(n_heads // 2, 2, LANES)


def _neighbourhood_bias(rpb, rows):
    nblk = rows // NA_QROWS
    n_dr = 2 * NA_KH - 1
    n_dc = 2 * NA_KW - 1
    kc = np.arange(GRID_W)[:, None]
    qc = np.arange(GRID_W)[None, :]
    ws = np.clip(qc - NA_KW // 2, 0, GRID_W - NA_KW)
    col_ok = (kc >= ws) & (kc < ws + NA_KW)
    dc = np.clip(kc - qc + NA_KW - 1, 0, n_dc - 1)
    dc_onehot = (dc[None] == np.arange(n_dc)[:, None, None]).astype(np.float32)
    tiles = jnp.einsum('hab,bkq->hakq', rpb.astype(F32), jnp.asarray(dc_onehot),
                       precision=lax.Precision.HIGHEST)
    tiles = jnp.where(jnp.asarray(col_ok)[None, None], tiles, MASKED)
    masked_tile = jnp.full((rpb.shape[0], 1, GRID_W, GRID_W), MASKED, F32)
    tiles = jnp.concatenate([tiles, masked_tile], axis=1)
    tile_of = np.zeros((3, NA_KROWS, NA_QROWS), np.int32)
    for v, i in enumerate((0, 1, nblk - 1)):
        r0 = i * NA_QROWS
        ks = int(np.clip(r0 - NA_KH // 2, 0, rows - NA_KROWS))
        for kl in range(NA_KROWS):
            for ql in range(NA_QROWS):
                kr, qr = ks + kl, r0 + ql
                rs = int(np.clip(qr - NA_KH // 2, 0, rows - NA_KH))
                ok = rs <= kr < rs + NA_KH
                tile_of[v, kl, ql] = (kr - qr + NA_KH - 1) if ok else n_dr
    n_heads, n_tiles = tiles.shape[:2]
    return pl.pallas_call(
        _bias_table_kernel,
        out_shape=jax.ShapeDtypeStruct((3, n_heads, NA_TK, NA_TQ), F32),
        grid_spec=pltpu.PrefetchScalarGridSpec(
            num_scalar_prefetch=1, grid=(3, n_heads),
            in_specs=[pl.BlockSpec((1, n_tiles, GRID_W, GRID_W), lambda v, h, tile_of_ref: (h, 0, 0, 0))],
            out_specs=pl.BlockSpec((1, 1, NA_TK, NA_TQ), lambda v, h, tile_of_ref: (v, h, 0, 0))),
        compiler_params=_cparams(("arbitrary", "arbitrary")),
        name="neighbourhood_bias_table",
    )(jnp.asarray(tile_of.reshape(-1)), tiles)


def _bias_table_kernel(tile_of_ref, tiles_ref, o_ref):
    v = pl.program_id(0)
    for kl in range(NA_KROWS):
        row = [tiles_ref[0, tile_of_ref[(v * NA_KROWS + kl) * NA_QROWS + ql]] for ql in range(NA_QROWS)]
        o_ref[0, 0, kl * GRID_W:(kl + 1) * GRID_W, :] = jnp.concatenate(row, axis=1)


def _merge_kernel(xp_ref, oap_ref, obp_ref, xs_ref, oas_ref, obs_ref, ona_ref, onb_ref, wo_ref,
                  g1_ref, sh2_ref, sc2_ref, n2_ref, wrh_ref, wrl_ref, rb_ref, tri_ref,
                  y_ref, hp_ref, gates_ref, rank_ref, count_ref, *, ctx_tiles):
    i = pl.program_id(0)

    @pl.when(i == 0)
    def _():
        count_ref[...] = jnp.zeros_like(count_ref)

    def one_stream(x_ref, oa_ref, ob_ref):
        na = (_rms(oa_ref[...]) * ona_ref[...]).astype(BF16)
        nb = (_rms(ob_ref[...]) * onb_ref[...]).astype(BF16)
        mix = _dot(na, wo_ref[0:WIDTH_A, :]) + _dot(nb, wo_ref[WIDTH_A:WIDTH_A + WIDTH_B, :])
        y = x_ref[...] + g1_ref[0] * mix
        y_ref[...] = y
        h = _rms(y) * n2_ref[...]
        h = h * (1.0 + sc2_ref[0]) + sh2_ref[0]
        hp_ref[...] = _pack_rows(h)
        gates, chosen = _router_gates(h, wrh_ref[...], wrl_ref[...], rb_ref[...])
        gates_ref[...] = gates
        before = _dot(chosen.astype(BF16), tri_ref[...])
        seen = count_ref[...]
        rank_ref[...] = jnp.where(chosen > 0.0, before + seen[:, 0:1], -1.0)
        count_ref[...] = seen + jnp.sum(chosen, axis=1, keepdims=True)

    pl.when(i < ctx_tiles)(lambda: one_stream(xp_ref, oap_ref, obp_ref))
    pl.when(i >= ctx_tiles)(lambda: one_stream(xs_ref, oas_ref, obs_ref))


def _merge(ctx, lat, on_a, on_b, w_out_bf, gate1, shift2, scale2, norm2, wr_hi, wr_lo, rbias, *, tm, lat_seq):
    t_c, d = ctx[0].shape
    t_l = lat[0].shape[0]
    t = t_c + t_l
    tri = jnp.asarray(np.triu(np.ones((tm, tm), np.float32), k=1), BF16)
    ctx_tiles = t_c // tm
    lat_tiles_per_batch = lat_seq // tm

    def ctx_map(i):
        return (jnp.minimum(i, ctx_tiles - 1), 0)

    def lat_map(i):
        return (jnp.maximum(i - ctx_tiles, 0), 0)

    def mod_map(i):
        return (jnp.where(i < ctx_tiles, 0, 1 + (i - ctx_tiles) // lat_tiles_per_batch), 0, 0)

    def stream_specs(index_map):
        return [pl.BlockSpec((tm, d), index_map),
                pl.BlockSpec((tm, WIDTH_A), index_map),
                pl.BlockSpec((tm, WIDTH_B), index_map)]

    return pl.pallas_call(
        functools.partial(_merge_kernel, ctx_tiles=ctx_tiles),
        out_shape=[jax.ShapeDtypeStruct((t, d), F32),
                   jax.ShapeDtypeStruct((t, d // 2), jnp.int32),
                   jax.ShapeDtypeStruct((N_EXPERTS, t), F32),
                   jax.ShapeDtypeStruct((N_EXPERTS, t), F32),
                   jax.ShapeDtypeStruct((N_EXPERTS, LANES), F32)],
        grid=(t // tm,),
        in_specs=stream_specs(ctx_map) + stream_specs(lat_map) + [
            pl.BlockSpec((1, WIDTH_A), lambda i: (0, 0)),
            pl.BlockSpec((1, WIDTH_B), lambda i: (0, 0)),
            pl.BlockSpec((WIDTH_A + WIDTH_B, d), lambda i: (0, 0)),
            pl.BlockSpec((1, 1, d), mod_map),
            pl.BlockSpec((1, 1, d), mod_map),
            pl.BlockSpec((1, 1, d), mod_map),
            pl.BlockSpec((1, d), lambda i: (0, 0)),
            pl.BlockSpec((N_EXPERTS, d), lambda i: (0, 0)),
            pl.BlockSpec((N_EXPERTS, d), lambda i: (0, 0)),
            pl.BlockSpec((N_EXPERTS, 1), lambda i: (0, 0)),
            pl.BlockSpec((tm, tm), lambda i: (0, 0))],
        out_specs=[pl.BlockSpec((tm, d), lambda i: (i, 0)),
                   pl.BlockSpec((tm, d // 2), lambda i: (i, 0)),
                   pl.BlockSpec((N_EXPERTS, tm), lambda i: (0, i)),
                   pl.BlockSpec((N_EXPERTS, tm), lambda i: (0, i)),
                   pl.BlockSpec((N_EXPERTS, LANES), lambda i: (0, 0))],
        compiler_params=_cparams(("arbitrary",)),
        name="merge_route",
    )(*ctx, *lat, on_a, on_b, w_out_bf, gate1, shift2, scale2, norm2, wr_hi, wr_lo, rbias, tri)


def _first_index_of_max(x, iota):
    mx = jnp.max(x, axis=0, keepdims=True)
    idx = jnp.min(jnp.where(x == mx, iota, float(x.shape[0])), axis=0, keepdims=True)
    return mx, iota == idx


def _router_gates(h, wr_hi, wr_lo, rbias):
    h_hi = h.astype(BF16)
    h_lo = (h - h_hi.astype(F32)).astype(BF16)
    logits = _dot_nt(wr_hi, h_hi) + (_dot_nt(wr_lo, h_hi) + _dot_nt(wr_hi, h_lo))
    scores = _sigmoid(logits)
    sel = scores + rbias
    tm = sel.shape[1]
    iota_g = lax.broadcasted_iota(jnp.int32, (GROUP_SIZE, tm), 0).astype(F32)
    group_scores = []
    for g in range(N_GROUPS):
        grp = sel[g * GROUP_SIZE:(g + 1) * GROUP_SIZE]
        m1, first = _first_index_of_max(grp, iota_g)
        m2 = jnp.max(jnp.where(first, -jnp.inf, grp), axis=0, keepdims=True)
        group_scores.append(m1 + m2)
    gs = jnp.concatenate(group_scores, axis=0)
    iota_n = lax.broadcasted_iota(jnp.int32, (N_GROUPS, tm), 0).astype(F32)
    group_on = jnp.zeros((N_GROUPS, tm), F32)
    for _ in range(TOPK_GROUPS):
        _, pick = _first_index_of_max(gs, iota_n)
        group_on = jnp.where(pick, 1.0, group_on)
        gs = jnp.where(pick, -jnp.inf, gs)
    expert_on = jnp.concatenate(
        [jnp.broadcast_to(group_on[g:g + 1], (GROUP_SIZE, tm)) for g in range(N_GROUPS)], axis=0)
    cand = jnp.where(expert_on > 0.0, sel, -jnp.inf)
    iota_e = lax.broadcasted_iota(jnp.int32, (N_EXPERTS, tm), 0).astype(F32)
    w = jnp.zeros((N_EXPERTS, tm), F32)
    chosen = jnp.zeros((N_EXPERTS, tm), F32)
    for _ in range(TOP_K):
        _, pick = _first_index_of_max(cand, iota_e)
        w = jnp.where(pick, scores, w)
        chosen = jnp.where(pick, 1.0, chosen)
        cand = jnp.where(pick, -jnp.inf, cand)
    return w / jnp.sum(w, axis=0, keepdims=True) * ROUTED_SCALE, chosen


MOE_TS = 1024
MOE_ROUTE_TM = 1024
MOE_ROW_TM = 512


def _slots_kernel(gates_ref, rank_ref, off_ref, pos_ref, gtok_ref):
    gates = gates_ref[...]
    rank = rank_ref[...]
    tm = gates.shape[1]
    slot = off_ref[...] + rank
    left = jnp.where(rank >= 0.0, 1.0, 0.0)
    iota_e = lax.broadcasted_iota(jnp.int32, (N_EXPERTS, tm), 0).astype(F32)
    pos_rows, gate_rows = [], []
    for _ in range(TOP_K):
        _, pick = _first_index_of_max(left, iota_e)
        pos_rows.append(jnp.sum(jnp.where(pick, slot, 0.0), axis=0, keepdims=True))
        gate_rows.append(jnp.sum(jnp.where(pick, gates, 0.0), axis=0, keepdims=True))
        left = jnp.where(pick, 0.0, left)
    pos_ref[...] = jnp.concatenate(pos_rows, axis=0).astype(jnp.int32)
    pad = jnp.zeros((LANES - TOP_K, tm), F32)
    gtok_ref[...] = jnp.concatenate(gate_rows + [pad], axis=0).T


def _slots(gates_t, rank_t, off):
    t = gates_t.shape[1]
    tm = MOE_ROUTE_TM
    return pl.pallas_call(
        _slots_kernel,
        out_shape=[jax.ShapeDtypeStruct((TOP_K, t), jnp.int32), jax.ShapeDtypeStruct((t, LANES), F32)],
        grid=(t // tm,),
        in_specs=[pl.BlockSpec((N_EXPERTS, tm), lambda i: (0, i)),
                  pl.BlockSpec((N_EXPERTS, tm), lambda i: (0, i)),
                  pl.BlockSpec((N_EXPERTS, 1), lambda i: (0, 0))],
        out_specs=[pl.BlockSpec((TOP_K, tm), lambda i: (0, i)),
                   pl.BlockSpec((tm, LANES), lambda i: (i, 0))],
        compiler_params=_cparams(("arbitrary",)),
        name="moe_slots",
    )(gates_t, rank_t, off)


SC_CORES = 2
SC_SUBCORES = 16
SC_ROWS = 64


def _dispatch(hp_all, slot_of):
    t, width = hp_all.shape
    n_slots = slot_of.shape[0]
    n_pad = n_slots - TOP_K * t
    workers = SC_CORES * SC_SUBCORES
    per_worker = t // workers
    pad_per_worker = n_pad // workers
    assert per_worker * workers == t and per_worker % SC_ROWS == 0
    assert pad_per_worker * workers == n_pad and pad_per_worker % SC_ROWS == 0
    mesh = plsc.VectorSubcoreMesh(core_axis_name="core", subcore_axis_name="subcore")

    assert pad_per_worker % (TOP_K * SC_ROWS) == 0

    @functools.partial(
        pl.kernel, mesh=mesh,
        out_type=jax.ShapeDtypeStruct((n_slots, width), jnp.int32),
        scratch_types=[pltpu.VMEM((SC_ROWS,), jnp.int32) for _ in range(TOP_K)]
        + [pltpu.VMEM((SC_ROWS, width), jnp.int32), pltpu.SemaphoreType.DMA, pltpu.SemaphoreType.DMA],
    )
    def scatter_rows(h_hbm, slot_hbm, out_hbm, *scratch):
        idx = scratch[:TOP_K]
        rows_v, sem_idx, sem_rows = scratch[TOP_K:]
        worker = lax.axis_index("subcore") * SC_CORES + lax.axis_index("core")

        def scatter_group(index_starts):
            for k, start in enumerate(index_starts):
                pltpu.async_copy(slot_hbm.at[pl.ds(start, SC_ROWS)], idx[k], sem_idx)
            for k, start in enumerate(index_starts):
                pltpu.make_async_copy(slot_hbm.at[pl.ds(start, SC_ROWS)], idx[k], sem_idx).wait()
            for k in range(TOP_K):
                pltpu.async_copy(rows_v, out_hbm.at[idx[k]], sem_rows)
            for k in range(TOP_K):
                pltpu.make_async_copy(rows_v, out_hbm.at[idx[k]], sem_rows).wait()

        pltpu.sync_copy(h_hbm.at[pl.ds(0, SC_ROWS)], rows_v)
        pad_base = TOP_K * t + worker * pad_per_worker

        @pl.loop(0, pad_per_worker // (TOP_K * SC_ROWS))
        def _(j):
            first = pad_base + j * (TOP_K * SC_ROWS)
            scatter_group([first + k * SC_ROWS for k in range(TOP_K)])

        base = worker * per_worker

        @pl.loop(0, per_worker // SC_ROWS)
        def _(j):
            first = base + j * SC_ROWS
            pltpu.sync_copy(h_hbm.at[pl.ds(first, SC_ROWS)], rows_v)
            scatter_group([k * t + first for k in range(TOP_K)])

    return scatter_rows(hp_all, slot_of)


def _experts_kernel(te_ref, xs_ref, wg_ref, wu_ref, wd_ref, ys_ref, wgu_bf, wd_bf):
    i = pl.program_id(0)

    @pl.when((i == 0) | (te_ref[i] != te_ref[jnp.maximum(i, 1) - 1]))
    def _():
        wgu_bf[:, 0:D_EXPERT] = wg_ref[0].astype(BF16)
        wgu_bf[:, D_EXPERT:2 * D_EXPERT] = wu_ref[0].astype(BF16)
        wd_bf[...] = wd_ref[0].astype(BF16)

    left, right = _unpack_rows(xs_ref[...])
    x = jnp.concatenate([left, right], axis=1).astype(BF16)
    gu = _dot(x, wgu_bf[...])
    g = gu[:, 0:D_EXPERT]
    u = gu[:, D_EXPERT:2 * D_EXPERT]
    act = (g * _sigmoid(g)) * u
    ys_ref[...] = _pack_rows(_dot(act.astype(BF16), wd_bf[...]))


def _experts(xs, tile_expert, w_gate, w_up, w_down):
    n_slots, width = xs.shape
    d = 2 * width
    ts = MOE_TS
    return pl.pallas_call(
        _experts_kernel,
        out_shape=jax.ShapeDtypeStruct((n_slots, width), jnp.int32),
        grid_spec=pltpu.PrefetchScalarGridSpec(
            num_scalar_prefetch=1,
            grid=(n_slots // ts,),
            in_specs=[pl.BlockSpec((ts, width), lambda i, te: (i, 0)),
                      pl.BlockSpec((1, d, D_EXPERT), lambda i, te: (te[i], 0, 0)),
                      pl.BlockSpec((1, d, D_EXPERT), lambda i, te: (te[i], 0, 0)),
                      pl.BlockSpec((1, D_EXPERT, d), lambda i, te: (te[i], 0, 0))],
            out_specs=pl.BlockSpec((ts, width), lambda i, te: (i, 0)),
            scratch_shapes=[pltpu.VMEM((d, 2 * D_EXPERT), BF16), pltpu.VMEM((D_EXPERT, d), BF16)]),
        compiler_params=_cparams(("arbitrary",)),
        name="moe_experts",
    )(tile_expert, xs, w_gate, w_up, w_down)


def _gather_slots(y_slots, slot_of, t):
    width = y_slots.shape[1]
    workers = SC_CORES * SC_SUBCORES
    per_worker = t // workers
    n_blocks = (per_worker // SC_ROWS) * TOP_K
    assert per_worker * workers == t and per_worker % SC_ROWS == 0 and n_blocks % 2 == 0
    mesh = plsc.VectorSubcoreMesh(core_axis_name="core", subcore_axis_name="subcore")

    @functools.partial(
        pl.kernel, mesh=mesh,
        out_type=jax.ShapeDtypeStruct((TOP_K * t, width), jnp.int32),
        scratch_types=[pltpu.VMEM((TOP_K * per_worker,), jnp.int32),
                       pltpu.VMEM((SC_ROWS, width), jnp.int32), pltpu.VMEM((SC_ROWS, width), jnp.int32),
                       pltpu.SemaphoreType.DMA, pltpu.SemaphoreType.DMA],
    )
    def gather_rows(ys_hbm, slot_hbm, out_hbm, idx_all, rows0, rows1, sem0, sem1):
        worker = lax.axis_index("subcore") * SC_CORES + lax.axis_index("core")
        base = worker * per_worker
        for k in range(TOP_K):
            pltpu.sync_copy(slot_hbm.at[pl.ds(k * t + base, per_worker)],
                            idx_all.at[pl.ds(k * per_worker, per_worker)])

        def first_row(n):
            return (n % TOP_K) * t + base + (n // TOP_K) * SC_ROWS

        def indices(n):
            first = pl.multiple_of((n % TOP_K) * per_worker + (n // TOP_K) * SC_ROWS, SC_ROWS)
            return idx_all.at[pl.ds(first, SC_ROWS)]

        def start(n, rows_v, sem):
            pltpu.async_copy(ys_hbm.at[indices(n)], rows_v, sem)

        def finish(n, rows_v, sem):
            pltpu.make_async_copy(ys_hbm.at[indices(n)], rows_v, sem).wait()
            pltpu.sync_copy(rows_v, out_hbm.at[pl.ds(first_row(n), SC_ROWS)])

        start(0, rows0, sem0)

        @pl.loop(0, n_blocks, step=2)
        def _(n):
            start(n + 1, rows1, sem1)
            finish(n, rows0, sem0)

            @pl.when(n + 2 < n_blocks)
            def _():
                start(n + 2, rows0, sem0)

            finish(n + 1, rows1, sem1)

    return gather_rows(y_slots, slot_of)


def _combine_kernel(y_ref, h_ref, g2_ref, gtok_ref, rows_ref, wgs_ref, wus_ref, wds_ref, o_ref):
    h_left, h_right = _unpack_rows(h_ref[...])
    h = jnp.concatenate([h_left, h_right], axis=1).astype(BF16)
    gs = _dot(h, wgs_ref[...])
    us = _dot(h, wus_ref[...])
    shared = _dot(((gs * _sigmoid(gs)) * us).astype(BF16), wds_ref[...])

    gtok = gtok_ref[...]
    acc_left = acc_right = None
    for k in range(TOP_K):
        left, right = _unpack_rows(rows_ref[k])
        gate = gtok[:, k:k + 1]
        acc_left = gate * left if acc_left is None else acc_left + gate * left
        acc_right = gate * right if acc_right is None else acc_right + gate * right
    routed = jnp.concatenate([acc_left, acc_right], axis=1)
    o_ref[...] = y_ref[...] + g2_ref[0] * (routed + shared)


def _combine(y_all, hp_all, gate2, gtok, rows, wgs, wus, wds, *, first_token, tokens, seq):
    d = y_all.shape[1]
    width = hp_all.shape[1]
    tm = MOE_ROW_TM
    tile0 = first_token // tm
    nb = gate2.shape[0]
    tiles_per_batch = seq // tm

    def mod_map(i):
        return ((i // tiles_per_batch) if nb > 1 else 0, 0, 0)

    return pl.pallas_call(
        _combine_kernel,
        out_shape=jax.ShapeDtypeStruct((tokens, d), F32),
        grid=(tokens // tm,),
        in_specs=[pl.BlockSpec((tm, d), lambda i: (tile0 + i, 0)),
                  pl.BlockSpec((tm, width), lambda i: (tile0 + i, 0)),
                  pl.BlockSpec((1, 1, d), mod_map),
                  pl.BlockSpec((tm, LANES), lambda i: (tile0 + i, 0)),
                  pl.BlockSpec((TOP_K, tm, width), lambda i: (0, tile0 + i, 0)),
                  pl.BlockSpec((d, D_SHARED), lambda i: (0, 0)),
                  pl.BlockSpec((d, D_SHARED), lambda i: (0, 0)),
                  pl.BlockSpec((D_SHARED, d), lambda i: (0, 0))],
        out_specs=pl.BlockSpec((tm, d), lambda i: (i, 0)),
        compiler_params=_cparams(("arbitrary",)),
        name="moe_combine",
    )(y_all, hp_all, gate2, gtok, rows, wgs, wus, wds)


def _expert_layout(counts, n_tiles):
    cnt = counts.astype(jnp.int32)
    tiles = (cnt + (MOE_TS - 1)) // MOE_TS
    last_tile = jnp.cumsum(tiles)
    off = (last_tile - tiles) * MOE_TS
    pad_lo = off + cnt
    pad_hi = (off + tiles * MOE_TS).at[N_EXPERTS - 1].set(n_tiles * MOE_TS)
    pad_cnt = pad_hi - pad_lo
    pad_last = jnp.cumsum(pad_cnt)
    shift = pad_lo - (pad_last - pad_cnt)
    j = jnp.arange(N_EXPERTS * MOE_TS, dtype=jnp.int32)
    past = (pad_last[None, :-1] <= j[:, None]).astype(jnp.int32)
    pad_slots = j + shift[0] + jnp.sum(past * (shift[1:] - shift[:-1])[None, :], axis=1)
    tile_ids = jnp.arange(n_tiles, dtype=jnp.int32)
    tile_expert = jnp.minimum(
        jnp.sum((last_tile[None, :] <= tile_ids[:, None]).astype(jnp.int32), axis=1), N_EXPERTS - 1)
    return off, pad_slots, tile_expert


def _rope_tables(n_tokens):
    t = jnp.arange(n_tokens)
    row = (t // GRID_W).astype(F32)
    col = (t % GRID_W).astype(F32)
    nf = HEAD_DIM // 4
    freqs = ROPE_THETA ** (-jnp.arange(nf, dtype=F32) / nf)
    ang_r = row[:, None] * freqs
    ang_c = col[:, None] * freqs
    cos = jnp.concatenate([jnp.cos(ang_r)] * 2 + [jnp.cos(ang_c)] * 2, axis=1)
    sin = jnp.concatenate([-jnp.sin(ang_r), jnp.sin(ang_r), -jnp.sin(ang_c), jnp.sin(ang_c)], axis=1)
    reps = LANES // HEAD_DIM
    return jnp.tile(cos, (1, reps)), jnp.tile(sin, (1, reps))


def _head_gains(qn_a, kn_a, qn_b, kn_b):
    ones = jnp.ones((HEAD_DIM,), F32)
    parts = ([qn_a] * N_HEADS_A + [kn_a] * N_KV_A + [ones] * N_KV_A
             + [qn_b] * N_HEADS_B + [kn_b] * N_HEADS_B + [ones] * N_HEADS_B)
    return jnp.concatenate(parts).reshape(1, IN_COLS).astype(F32)


def _same_head_indicator():
    i = np.arange(MXU_DIM)
    return jnp.asarray((i[:, None] // HEAD_DIM) == (i[None, :] // HEAD_DIM), BF16)


def _token_major(cache):
    b, h, s, hd = cache.shape
    return cache.transpose(0, 2, 1, 3).reshape(b, s, h * hd).astype(BF16)


def kernel(x_prompt, x_sample, cache_k_a, cache_v_a, cache_k_b, cache_v_b, c, c_ctx, w_mod, b_mod, norm1, norm2, w_in, qn_a, kn_a, qn_b, kn_b, rpb, on_a, on_b, w_out, w_router, router_bias, w_gate_e, w_up_e, w_down_e, w_gate_s, w_up_s, w_down_s):
    depth = w_mod.shape[0]
    assert depth == 1
    l = 0
    bp, sp, d = x_prompt.shape
    bs, ss, _ = x_sample.shape

    cvec = jnp.concatenate([c_ctx[None, :], c], axis=0)
    rows = -(-cvec.shape[0] // 8) * 8
    cvec = jnp.pad(cvec, ((0, rows - cvec.shape[0]), (0, 0)))
    mod = _adaln(cvec, w_mod[l], b_mod[l])
    mod_p = [m.reshape(1, 1, d) for m in jnp.split(mod[0:1], 6, axis=-1)]
    mod_s = [m.reshape(bs, 1, d) for m in jnp.split(mod[1:1 + bs], 6, axis=-1)]
    mod_all = [m.reshape(1 + bs, 1, d) for m in jnp.split(mod[0:1 + bs], 6, axis=-1)]

    w_in_bf = w_in[l].astype(BF16)
    w_out_bf = w_out[l].astype(BF16)
    gain = _head_gains(qn_a[l], kn_a[l], qn_b[l], kn_b[l])
    seg = _same_head_indicator()
    n1 = norm1[l].reshape(1, d)
    n2 = norm2[l].reshape(1, d)
    ona = on_a[l].reshape(1, WIDTH_A)
    onb = on_b[l].reshape(1, WIDTH_B)
    wr_t = w_router[l].T
    wr_hi = wr_t.astype(BF16)
    wr_lo = (wr_t - wr_hi.astype(F32)).astype(BF16)
    rbias = router_bias[l].reshape(N_EXPERTS, 1).astype(F32)
    wgs = w_gate_s[l].astype(BF16)
    wus = w_up_s[l].astype(BF16)
    wds = w_down_s[l].astype(BF16)
    t_p = bp * sp
    t_s = bs * ss
    t_all = t_p + t_s

    xp = x_prompt.reshape(t_p, d)
    proj_p, st_ka, st_va, st_kb, st_vb = _project(
        xp, mod_p[0], mod_p[1], n1, w_in_bf, gain, seg, None, tm=sp, seq=sp, states=True)
    oa_p, ob_p = _context_attention(proj_p, seq=sp)

    xs = x_sample.reshape(t_s, d)
    proj_s, = _project(xs, mod_s[0], mod_s[1], n1, w_in_bf, gain, seg, _rope_tables(ss),
                       tm=1024, seq=ss, states=False)
    oa_s = _latent_gqa(proj_s, _token_major(cache_k_a[:, l]), _token_major(cache_v_a[:, l]),
                       _query_norm_bound(qn_a[l]), seq=ss, tq=256)
    bias_t = _neighbourhood_bias(rpb[l], ss // GRID_W)
    ob_s = _latent_neighbourhood(proj_s, _token_major(cache_k_b[:, l]), _token_major(cache_v_b[:, l]),
                                 bias_t, _neighbourhood_bounds(qn_b[l], kn_b[l], rpb[l]), seq=ss)

    y1_all, hp_all, gates_t, rank_t, counts = _merge(
        (xp, oa_p, ob_p), (xs, oa_s, ob_s), ona, onb, w_out_bf, mod_all[2], mod_all[3], mod_all[4], n2,
        wr_hi, wr_lo, rbias, tm=512, lat_seq=ss)
    n_tiles = t_all * TOP_K // MOE_TS + N_EXPERTS
    off, pad_slots, tile_expert = _expert_layout(counts[:, 0], n_tiles)
    pos, gtok = _slots(gates_t, rank_t, off.astype(F32).reshape(N_EXPERTS, 1))
    slot_of = pos.reshape(TOP_K * t_all)
    x_slots = _dispatch(hp_all, jnp.concatenate([slot_of, pad_slots]))
    y_slots = _experts(x_slots, tile_expert, w_gate_e[l], w_up_e[l], w_down_e[l])
    rows = _gather_slots(y_slots, slot_of, t_all).reshape(TOP_K, t_all, d // 2)
    y_p = _combine(y1_all, hp_all, mod_p[5], gtok, rows, wgs, wus, wds,
                   first_token=0, tokens=t_p, seq=sp)
    y_s = _combine(y1_all, hp_all, mod_s[5], gtok, rows, wgs, wus, wds,
                   first_token=t_p, tokens=t_s, seq=ss)

    return (y_p.reshape(bp, sp, d), y_s.reshape(bs, ss, d), st_ka, st_va, st_kb, st_vb)
```

```python
import functools

import numpy as np
import jax
import jax.numpy as jnp
from jax import lax
from jax.experimental import pallas as pl
from jax.experimental.pallas import tpu as pltpu
from jax.experimental.pallas import tpu_sc as plsc

F32 = jnp.float32
BF16 = jnp.bfloat16

D_MODEL = 1024
HEAD_DIM = 64
N_HEADS_A = 8
N_KV_A = 2
GROUP_A = N_HEADS_A // N_KV_A
N_HEADS_B = 8
WIDTH_A = N_HEADS_A * HEAD_DIM
WIDTH_B = N_HEADS_B * HEAD_DIM
KV_WIDTH_A = N_KV_A * HEAD_DIM
IN_COLS = WIDTH_A + 2 * KV_WIDTH_A + 3 * WIDTH_B
GRID_W = 64
ROPE_THETA = 10000.0
NA_KH = 8
NA_KW = 16
N_EXPERTS = 64
N_GROUPS = 8
GROUP_SIZE = N_EXPERTS // N_GROUPS
TOPK_GROUPS = 4
TOP_K = 8
D_EXPERT = 256
D_SHARED = 256
ROUTED_SCALE = 2.5
EPS = 1e-6

LANES = 128
MXU_DIM = 256
MASKED = -1e30

COL_QA = 0
COL_KA = WIDTH_A
COL_VA = COL_KA + KV_WIDTH_A
COL_QB = COL_VA + KV_WIDTH_A
COL_KB = COL_QB + WIDTH_B
COL_VB = COL_KB + WIDTH_B

NA_QROWS = 8
NA_KROWS = 2 * NA_KH
NA_TQ = NA_QROWS * GRID_W
NA_TK = NA_KROWS * GRID_W
NA_KBLK = 256

VMEM_LIMIT = 56 * 1024 * 1024


def _cparams(sem):
    return pltpu.CompilerParams(dimension_semantics=sem, vmem_limit_bytes=VMEM_LIMIT)


def _dot(a, b):
    return jnp.dot(a, b, preferred_element_type=F32)


def _dot_nt(a, b):
    return lax.dot_general(a, b, (((1,), (1,)), ((), ())), preferred_element_type=F32)


def _sigmoid(x):
    return 1.0 / (1.0 + jnp.exp(-x))


def _rms(x):
    return x * lax.rsqrt(jnp.mean(x * x, axis=-1, keepdims=True) + EPS)


def _pack_rows(x):
    n = x.shape[1] // 2
    hi = lax.bitcast_convert_type(x[:, :n].astype(BF16).astype(F32), jnp.int32)
    lo = lax.bitcast_convert_type(x[:, n:].astype(BF16).astype(F32), jnp.int32)
    return hi | lax.shift_right_logical(lo, 16)


def _unpack_rows(w):
    left = lax.bitcast_convert_type(w & jnp.int32(-65536), F32)
    right = lax.bitcast_convert_type(lax.shift_left(w, 16), F32)
    return left, right


def _mod_kernel(c_ref, w_ref, b_ref, o_ref):
    c = c_ref[...]
    s = c * _sigmoid(c)
    o_ref[...] = jnp.dot(s, w_ref[...], preferred_element_type=F32,
                         precision=lax.Precision.HIGHEST) + b_ref[...]


def _adaln(cvec, w_mod, b_mod):
    rows, d = cvec.shape
    n = w_mod.shape[1]
    tn = 512
    return pl.pallas_call(
        _mod_kernel,
        out_shape=jax.ShapeDtypeStruct((rows, n), F32),
        grid=(n // tn,),
        in_specs=[pl.BlockSpec((rows, d), lambda j: (0, 0)),
                  pl.BlockSpec((d, tn), lambda j: (0, j)),
                  pl.BlockSpec((1, tn), lambda j: (0, j))],
        out_specs=pl.BlockSpec((rows, tn), lambda j: (0, j)),
        compiler_params=_cparams(("arbitrary",)),
        name="adaln_mod",
    )(cvec, w_mod, b_mod.reshape(1, n))


def _proj_chunks():
    def split(c0, width, step, *flags):
        return [(c0 + i, min(step, width - i)) + flags for i in range(0, width, step)]
    return (split(COL_QA, WIDTH_A, MXU_DIM, True, True, True)
            + split(COL_KA, KV_WIDTH_A, MXU_DIM, True, True, False)
            + split(COL_VA, KV_WIDTH_A, MXU_DIM, False, False, False)
            + split(COL_QB, WIDTH_B, MXU_DIM, True, False, True)
            + split(COL_KB, WIDTH_B, MXU_DIM, True, False, False)
            + split(COL_VB, WIDTH_B, MXU_DIM, False, False, False))


_PROJ_CHUNKS = _proj_chunks()


def _proj_kernel(*refs, rope, states):
    x_ref, sh_ref, sc_ref, n1_ref, w_ref, gain_ref, seg_ref = refs[:7]
    pos = 7
    if rope:
        cos_ref, sin_ref = refs[pos:pos + 2]
        pos += 2
    out_ref = refs[pos]
    pos += 1
    if states:
        ka_ref, va_ref, kb_ref, vb_ref = refs[pos:pos + 4]
        state_of = {COL_KA: ka_ref, COL_VA: va_ref, COL_KB: kb_ref, COL_VB: vb_ref}

    x = x_ref[...]
    h = _rms(x) * n1_ref[...]
    h = h * (1.0 + sc_ref[0]) + sh_ref[0]
    p = _dot(h.astype(BF16), w_ref[...])

    for c0, w, normed, roped, is_query in _PROJ_CHUNKS:
        pc = p[:, c0:c0 + w]
        if normed:
            seg = seg_ref[0:w, 0:w]
            sq = pc * pc
            hi = sq.astype(BF16)
            lo = (sq - hi.astype(F32)).astype(BF16)
            ss = _dot(hi, seg) + _dot(lo, seg)
            pc = pc * lax.rsqrt(ss * (1.0 / HEAD_DIM) + EPS) * gain_ref[:, c0:c0 + w]
        if states:
            for start, ref in state_of.items():
                if start <= c0 < start + ref.shape[2] * HEAD_DIM:
                    base = (c0 - start) // HEAD_DIM
                    for hh in range(w // HEAD_DIM):
                        ref[0, 0, base + hh] = pc[:, hh * HEAD_DIM:(hh + 1) * HEAD_DIM]
        if rope and roped:
            reps = w // LANES
            cos = jnp.concatenate([cos_ref[...]] * reps, axis=1) if reps > 1 else cos_ref[...]
            sin = jnp.concatenate([sin_ref[...]] * reps, axis=1) if reps > 1 else sin_ref[...]
            lane = lax.broadcasted_iota(jnp.int32, pc.shape, 1)
            first_half = (lane % (HEAD_DIM // 2)) < (HEAD_DIM // 4)
            partner = jnp.where(first_half,
                                pltpu.roll(pc, w - HEAD_DIM // 4, 1),
                                pltpu.roll(pc, HEAD_DIM // 4, 1))
            pc = pc * cos + partner * sin
        if is_query:
            pc = pc * (HEAD_DIM ** -0.5)
        out_ref[:, c0:c0 + w] = pc.astype(BF16)


def _project(x2d, shift, scale, norm1, w_in_bf, gain, seg, rope_tabs, *, tm, seq, states):
    t, d = x2d.shape
    nb = shift.shape[0]
    tiles_per_batch = seq // tm
    rope = rope_tabs is not None

    def mod_map(i):
        return ((i // tiles_per_batch) if nb > 1 else 0, 0, 0)

    in_specs = [pl.BlockSpec((tm, d), lambda i: (i, 0)),
                pl.BlockSpec((1, 1, d), mod_map),
                pl.BlockSpec((1, 1, d), mod_map),
                pl.BlockSpec((1, d), lambda i: (0, 0)),
                pl.BlockSpec((d, IN_COLS), lambda i: (0, 0)),
                pl.BlockSpec((1, IN_COLS), lambda i: (0, 0)),
                pl.BlockSpec((MXU_DIM, MXU_DIM), lambda i: (0, 0))]
    args = [x2d, shift, scale, norm1, w_in_bf, gain, seg]
    if rope:
        in_specs += [pl.BlockSpec((tm, LANES), lambda i: (i % tiles_per_batch, 0))] * 2
        args += list(rope_tabs)
    out_shape = [jax.ShapeDtypeStruct((t, IN_COLS), BF16)]
    out_specs = [pl.BlockSpec((tm, IN_COLS), lambda i: (i, 0))]
    if states:
        assert tm == seq
        b = t // seq
        for nh in (N_KV_A, N_KV_A, N_HEADS_B, N_HEADS_B):
            out_shape.append(jax.ShapeDtypeStruct((b, 1, nh, seq, HEAD_DIM), F32))
            out_specs.append(pl.BlockSpec((1, 1, nh, seq, HEAD_DIM), lambda i: (i, 0, 0, 0, 0)))
    return pl.pallas_call(
        functools.partial(_proj_kernel, rope=rope, states=states),
        out_shape=out_shape,
        grid=(t // tm,),
        in_specs=in_specs,
        out_specs=out_specs,
        compiler_params=_cparams(("arbitrary",)),
        name="proj_states" if states else "proj_rope",
    )(*args)


def _lane_half(shape):
    return lax.broadcasted_iota(jnp.int32, shape, 1) // HEAD_DIM


def _keep_half(x, half):
    return jnp.where(_lane_half(x.shape) == half, x, jnp.zeros_like(x))


def _transpose_bf16(x):
    return x.astype(F32).T.astype(BF16)


def _attend(q, keys, values_t, biases):
    return _softmax_av(_scores(q, keys, biases), values_t)


def _scores(q, keys, biases):
    scores = []
    for k, b in zip(keys, biases):
        s = _dot_nt(k, q)
        if b is not None:
            s = s + b
        scores.append(s)
    return scores


def _softmax_av(scores, values_t):
    m = functools.reduce(jnp.maximum, [jnp.max(s, axis=0, keepdims=True) for s in scores])
    denom = None
    out = None
    for s, vt in zip(scores, values_t):
        p = jnp.exp(s - m)
        ps = jnp.sum(p, axis=0, keepdims=True)
        po = _dot(vt, p.astype(BF16))
        denom = ps if denom is None else denom + ps
        out = po if out is None else out + po
    return out / denom


def _swap_halves(q_bf16):
    return pltpu.roll(q_bf16.astype(F32), HEAD_DIM, 1).astype(BF16)


def _gqa_heads(q_of_pair, keys_by_group, values_t):
    outs = []
    for h in range(N_HEADS_A):
        g = h // GROUP_A
        q = q_of_pair(h // 2)
        if h % 2 != g:
            q = _swap_halves(q)
        o = _attend(q, keys_by_group[g], values_t, [None] * len(values_t))
        outs.append(o[g * HEAD_DIM:(g + 1) * HEAD_DIM])
    return jnp.concatenate(outs, axis=0)


def _ctx_attn_kernel(p_ref, oa_ref, ob_ref):
    ka = p_ref[:, COL_KA:COL_KA + LANES]
    va_t = [_transpose_bf16(p_ref[:, COL_VA:COL_VA + LANES])]
    keys_by_group = [[_keep_half(ka, g)] for g in range(N_KV_A)]
    oa = _gqa_heads(lambda i: p_ref[:, COL_QA + i * LANES:COL_QA + (i + 1) * LANES],
                    keys_by_group, va_t)
    oa_ref[...] = oa.T

    outs = []
    for i in range(N_HEADS_B // 2):
        q = p_ref[:, COL_QB + i * LANES:COL_QB + (i + 1) * LANES]
        k = p_ref[:, COL_KB + i * LANES:COL_KB + (i + 1) * LANES]
        vt = [_transpose_bf16(p_ref[:, COL_VB + i * LANES:COL_VB + (i + 1) * LANES])]
        for half in range(2):
            o = _attend(q, [_keep_half(k, half)], vt, [None])
            outs.append(o[half * HEAD_DIM:(half + 1) * HEAD_DIM])
    ob_ref[...] = jnp.concatenate(outs, axis=0).T


def _context_attention(proj, *, seq):
    t = proj.shape[0]
    return pl.pallas_call(
        _ctx_attn_kernel,
        out_shape=[jax.ShapeDtypeStruct((t, WIDTH_A), F32), jax.ShapeDtypeStruct((t, WIDTH_B), F32)],
        grid=(t // seq,),
        in_specs=[pl.BlockSpec((seq, IN_COLS), lambda i: (i, 0))],
        out_specs=[pl.BlockSpec((seq, WIDTH_A), lambda i: (i, 0)),
                   pl.BlockSpec((seq, WIDTH_B), lambda i: (i, 0))],
        compiler_params=_cparams(("arbitrary",)),
        name="context_attention",
    )(proj)


ATTN_SAFE_SHIFT = 40.0
ONES_ROWS = 16


def _round_up_bf16(x):
    return (x * (1.0 + 2.0 ** -6)).astype(BF16).astype(F32)


def _query_norm_bound(gain):
    return jnp.max(jnp.abs(gain)).reshape(1, 1).astype(F32)


def _ones_lane(g):
    return (1 - g) * HEAD_DIM


def _gqa_latent_kernel(q_ref, k_ref, v_ref, ck_ref, cv_ref, qmax_ref, o_ref,
                       kg_ref, ckg_ref, vt_ref, cvt_ref, shift_ref):
    lane_k = lax.broadcasted_iota(jnp.int32, (1, LANES), 1)

    @pl.when(pl.program_id(1) == 0)
    def _():
        k = k_ref[...]
        ck = ck_ref[0]
        vt = v_ref[...].astype(F32).T
        cvt = cv_ref[0].astype(F32).T
        for g in range(N_KV_A):
            kf = _keep_half(k, g).astype(F32)
            ckf = _keep_half(ck, g).astype(F32)
            ksq = jnp.maximum(jnp.max(jnp.sum(kf * kf, axis=1, keepdims=True), axis=0, keepdims=True),
                              jnp.max(jnp.sum(ckf * ckf, axis=1, keepdims=True), axis=0, keepdims=True))
            shift_ref[g] = jnp.broadcast_to(_round_up_bf16(qmax_ref[...] * jnp.sqrt(ksq)), shift_ref.shape[1:])
            kg_ref[g] = jnp.where(lane_k == _ones_lane(g), 1.0, kf).astype(BF16)
            ckg_ref[g] = jnp.where(lane_k == _ones_lane(g), 1.0, ckf).astype(BF16)
            rows = slice(g * HEAD_DIM, (g + 1) * HEAD_DIM)
            vt_ref[g] = jnp.concatenate([vt[rows], jnp.ones((ONES_ROWS, vt.shape[1]), F32)], axis=0).astype(BF16)
            cvt_ref[g] = jnp.concatenate([cvt[rows], jnp.ones((ONES_ROWS, cvt.shape[1]), F32)], axis=0).astype(BF16)

    tq = q_ref.shape[0]
    lane_q = lax.broadcasted_iota(jnp.int32, (GROUP_A * tq, LANES), 1)
    queries, shifts = [], []
    for g in range(N_KV_A):
        qs = []
        for j in range(GROUP_A):
            h = g * GROUP_A + j
            q = q_ref[:, (h // 2) * LANES:(h // 2 + 1) * LANES].astype(F32)
            qs.append(q if h % 2 == g else pltpu.roll(q, HEAD_DIM, 1))
        queries.append(jnp.where(lane_q // HEAD_DIM == g, jnp.concatenate(qs, axis=0), 0.0))
        shifts.append(shift_ref[g][0:1, 0:1])
    safe = jnp.max(jnp.maximum(shift_ref[0], shift_ref[1])) <= ATTN_SAFE_SHIFT

    def attend(g, p_lat, p_ctx):
        half = p_lat.shape[1] // 2
        o = jnp.concatenate([_dot(vt_ref[g], p_lat[:, :half]) + _dot(cvt_ref[g], p_ctx[:, :half]),
                             _dot(vt_ref[g], p_lat[:, half:]) + _dot(cvt_ref[g], p_ctx[:, half:])],
                            axis=1)
        o = o[:HEAD_DIM] / o[HEAD_DIM:HEAD_DIM + 1]
        heads = jnp.concatenate([o[:, j * tq:(j + 1) * tq] for j in range(GROUP_A)], axis=0)
        o_ref[:, g * GROUP_A * HEAD_DIM:(g + 1) * GROUP_A * HEAD_DIM] = heads.T

    def with_bound():
        for g in range(N_KV_A):
            qa = jnp.where(lane_q == _ones_lane(g), -shifts[g], queries[g]).astype(BF16)
            attend(g, jnp.exp(_dot_nt(kg_ref[g], qa)).astype(BF16), jnp.exp(_dot_nt(ckg_ref[g], qa)).astype(BF16))

    def with_row_max():
        for g in range(N_KV_A):
            qa = queries[g].astype(BF16)
            s_lat = _dot_nt(kg_ref[g], qa)
            s_ctx = _dot_nt(ckg_ref[g], qa)
            m = jnp.maximum(jnp.max(s_lat, axis=0, keepdims=True), jnp.max(s_ctx, axis=0, keepdims=True))
            attend(g, jnp.exp(s_lat - m).astype(BF16), jnp.exp(s_ctx - m).astype(BF16))

    pl.when(safe)(with_bound)
    pl.when(jnp.logical_not(safe))(with_row_max)


def _latent_gqa(proj, ctx_k, ctx_v, qmax, *, seq, tq):
    t = proj.shape[0]
    b = t // seq
    nq = seq // tq
    past = ctx_k.shape[1]
    return pl.pallas_call(
        _gqa_latent_kernel,
        out_shape=jax.ShapeDtypeStruct((t, WIDTH_A), F32),
        grid=(b, nq),
        in_specs=[pl.BlockSpec((tq, WIDTH_A), lambda bi, qi: (bi * nq + qi, 0)),
                  pl.BlockSpec((seq, LANES), lambda bi, qi: (bi, COL_KA // LANES)),
                  pl.BlockSpec((seq, LANES), lambda bi, qi: (bi, COL_VA // LANES)),
                  pl.BlockSpec((1, past, LANES), lambda bi, qi: (bi, 0, 0)),
                  pl.BlockSpec((1, past, LANES), lambda bi, qi: (bi, 0, 0)),
                  pl.BlockSpec((1, 1), lambda bi, qi: (0, 0))],
        out_specs=pl.BlockSpec((tq, WIDTH_A), lambda bi, qi: (bi * nq + qi, 0)),
        scratch_shapes=[pltpu.VMEM((N_KV_A, seq, LANES), BF16),
                        pltpu.VMEM((N_KV_A, past, LANES), BF16),
                        pltpu.VMEM((N_KV_A, HEAD_DIM + ONES_ROWS, seq), BF16),
                        pltpu.VMEM((N_KV_A, HEAD_DIM + ONES_ROWS, past), BF16),
                        pltpu.VMEM((N_KV_A, 8, LANES), F32)],
        compiler_params=_cparams(("arbitrary", "arbitrary")),
        name="latent_gqa",
    )(proj, proj, proj, ctx_k, ctx_v, qmax)


def _na_kernel(q_ref, k_ref, v_ref, ck_ref, cv_ref, bias_ref, bound_ref, o_ref, keys_ref, vt_ref, shift_ref,
               *, rows):
    i = pl.program_id(2)
    n_kblk = k_ref.shape[0] // NA_KBLK
    lane_k = lax.broadcasted_iota(jnp.int32, (1, LANES), 1)
    one = jnp.ones((), BF16)

    @pl.when(i == 0)
    def _():
        k = k_ref[...]
        ck = ck_ref[0]
        for half in range(2):
            kh = jnp.where(lane_k == _ones_lane(half), one, _keep_half(k, half))
            keys_ref[half, 0:n_kblk] = kh.reshape(n_kblk, NA_KBLK, LANES)
            ckh = _keep_half(ck, half)
            keys_ref[half, n_kblk] = jnp.where(lane_k == _ones_lane(half), one, ckh)
            ckf = ckh.astype(F32)
            ctx_norm = jnp.sqrt(jnp.max(jnp.sum(ckf * ckf, axis=1, keepdims=True), axis=0, keepdims=True))
            consts = bound_ref[0, half:half + 1, :]
            kmax = jnp.maximum(ctx_norm, consts[:, 2:3])
            shift_ref[half] = jnp.broadcast_to(_round_up_bf16(consts[:, 0:1] * kmax + consts[:, 1:2]),
                                               shift_ref.shape[1:])
        ones_rows = jnp.ones((ONES_ROWS, NA_KBLK), F32)
        vt = v_ref[...].astype(F32).T
        for j in range(n_kblk):
            vt_ref[j] = jnp.concatenate([vt[:, j * NA_KBLK:(j + 1) * NA_KBLK], ones_rows], axis=0).astype(BF16)
        vt_ref[n_kblk] = jnp.concatenate([cv_ref[0].astype(F32).T, ones_rows], axis=0).astype(BF16)

    q = q_ref[...]
    lane_q = lax.broadcasted_iota(jnp.int32, q.shape, 1)
    first = _na_first_key_block(i, rows)
    n_qblk = rows // NA_QROWS
    variant = jnp.where(i == 0, 0, jnp.where(i == n_qblk - 1, 2, 1))
    blocks = [first + j for j in range(NA_TK // NA_KBLK)] + [n_kblk]
    values_t = [vt_ref[blk] for blk in blocks]
    heads = []
    for half in range(2):
        keys = [keys_ref[half, blk] for blk in blocks]
        biases = [bias_ref[variant, half, j * NA_KBLK:(j + 1) * NA_KBLK, :] for j in range(NA_TK // NA_KBLK)] + [None]
        heads.append((shift_ref[half][0:1, 0:1], keys, biases))
    safe = jnp.max(jnp.maximum(shift_ref[0], shift_ref[1])) <= ATTN_SAFE_SHIFT

    def attend(probabilities):
        outs = []
        for half, ps in enumerate(probabilities):
            o = functools.reduce(lambda a, b: a + b, [_dot(vt, p) for vt, p in zip(values_t, ps)])
            outs.append(o[half * HEAD_DIM:(half + 1) * HEAD_DIM] / o[2 * HEAD_DIM:2 * HEAD_DIM + 1])
        o_ref[...] = jnp.concatenate(outs, axis=0).T

    def with_bound():
        probabilities = []
        for half, (shift, keys, biases) in enumerate(heads):
            qa = jnp.where(lane_q == _ones_lane(half), (-shift).astype(BF16), _keep_half(q, half))
            probabilities.append([jnp.exp(s).astype(BF16) for s in _scores(qa, keys, biases)])
        attend(probabilities)

    def with_row_max():
        all_scores = [_scores(_keep_half(q, half), keys, biases) for half, (_, keys, biases) in enumerate(heads)]
        probabilities = []
        for scores in all_scores:
            m = functools.reduce(jnp.maximum, [jnp.max(s, axis=0, keepdims=True) for s in scores])
            probabilities.append([jnp.exp(s - m).astype(BF16) for s in scores])
        attend(probabilities)

    pl.when(safe)(with_bound)
    pl.when(jnp.logical_not(safe))(with_row_max)


def _na_first_key_block(i, rows):
    per_qblock = NA_QROWS * GRID_W // NA_KBLK
    lead = (NA_KH // 2) * GRID_W // NA_KBLK
    return jnp.clip(per_qblock * i - lead, 0, (rows - NA_KROWS) * GRID_W // NA_KBLK)


def _latent_neighbourhood(proj, ctx_k, ctx_v, bias_t, bounds, *, seq):
    t = proj.shape[0]
    b = t // seq
    rows = seq // GRID_W
    nblk = rows // NA_QROWS
    n_kblk = seq // NA_KBLK
    past = ctx_k.shape[1]
    assert past == NA_KBLK
    grid = (N_HEADS_B // 2, b, nblk)
    in_specs = [pl.BlockSpec((NA_TQ, LANES), lambda hp, bi, i: (bi * nblk + i, COL_QB // LANES + hp)),
                pl.BlockSpec((seq, LANES), lambda hp, bi, i: (bi, COL_KB // LANES + hp)),
                pl.BlockSpec((seq, LANES), lambda hp, bi, i: (bi, COL_VB // LANES + hp)),
                pl.BlockSpec((1, past, LANES), lambda hp, bi, i: (bi, 0, hp)),
                pl.BlockSpec((1, past, LANES), lambda hp, bi, i: (bi, 0, hp)),
                pl.BlockSpec((3, 2, NA_TK, NA_TQ), lambda hp, bi, i: (0, hp, 0, 0)),
                pl.BlockSpec((1, 2, LANES), lambda hp, bi, i: (hp, 0, 0))]
    return pl.pallas_call(
        functools.partial(_na_kernel, rows=rows),
        out_shape=jax.ShapeDtypeStruct((t, WIDTH_B), F32),
        grid=grid,
        in_specs=in_specs,
        out_specs=pl.BlockSpec((NA_TQ, LANES), lambda hp, bi, i: (bi * nblk + i, hp)),
        scratch_shapes=[pltpu.VMEM((2, n_kblk + 1, NA_KBLK, LANES), BF16),
                        pltpu.VMEM((n_kblk + 1, 2 * HEAD_DIM + ONES_ROWS, NA_KBLK), BF16),
                        pltpu.VMEM((2, 8, LANES), F32)],
        compiler_params=_cparams(("arbitrary", "arbitrary", "arbitrary")),
        name="latent_neighbourhood",
    )(proj, proj, proj, ctx_k, ctx_v, bias_t, bounds)


def _neighbourhood_bounds(qn_b, kn_b, rpb):
    n_heads = rpb.shape[0]
    qmax = jnp.broadcast_to(_query_norm_bound(qn_b), (n_heads, 1))
    kmax = jnp.broadcast_to(_query_norm_bound(kn_b) * (HEAD_DIM ** 0.5), (n_heads, 1))
    bmax = jnp.maximum(jnp.max(rpb.reshape(n_heads, -1), axis=1, keepdims=True), 0.0).astype(F32)
    table = jnp.concatenate([qmax, bmax, kmax, jnp.zeros((n_heads, LANES - 3), F32)], axis=1)
    return table.reshape(n_heads // 2, 2, LANES)


def _neighbourhood_bias(rpb, rows):
    nblk = rows // NA_QROWS
    n_dr = 2 * NA_KH - 1
    n_dc = 2 * NA_KW - 1
    kc = np.arange(GRID_W)[:, None]
    qc = np.arange(GRID_W)[None, :]
    ws = np.clip(qc - NA_KW // 2, 0, GRID_W - NA_KW)
    col_ok = (kc >= ws) & (kc < ws + NA_KW)
    dc = np.clip(kc - qc + NA_KW - 1, 0, n_dc - 1)
    dc_onehot = (dc[None] == np.arange(n_dc)[:, None, None]).astype(np.float32)
    tiles = jnp.einsum('hab,bkq->hakq', rpb.astype(F32), jnp.asarray(dc_onehot),
                       precision=lax.Precision.HIGHEST)
    tiles = jnp.where(jnp.asarray(col_ok)[None, None], tiles, MASKED)
    masked_tile = jnp.full((rpb.shape[0], 1, GRID_W, GRID_W), MASKED, F32)
    tiles = jnp.concatenate([tiles, masked_tile], axis=1)
    tile_of = np.zeros((3, NA_KROWS, NA_QROWS), np.int32)
    for v, i in enumerate((0, 1, nblk - 1)):
        r0 = i * NA_QROWS
        ks = int(np.clip(r0 - NA_KH // 2, 0, rows - NA_KROWS))
        for kl in range(NA_KROWS):
            for ql in range(NA_QROWS):
                kr, qr = ks + kl, r0 + ql
                rs = int(np.clip(qr - NA_KH // 2, 0, rows - NA_KH))
                ok = rs <= kr < rs + NA_KH
                tile_of[v, kl, ql] = (kr - qr + NA_KH - 1) if ok else n_dr
    n_heads, n_tiles = tiles.shape[:2]
    return pl.pallas_call(
        _bias_table_kernel,
        out_shape=jax.ShapeDtypeStruct((3, n_heads, NA_TK, NA_TQ), F32),
        grid_spec=pltpu.PrefetchScalarGridSpec(
            num_scalar_prefetch=1, grid=(3, n_heads),
            in_specs=[pl.BlockSpec((1, n_tiles, GRID_W, GRID_W), lambda v, h, tile_of_ref: (h, 0, 0, 0))],
            out_specs=pl.BlockSpec((1, 1, NA_TK, NA_TQ), lambda v, h, tile_of_ref: (v, h, 0, 0))),
        compiler_params=_cparams(("arbitrary", "arbitrary")),
        name="neighbourhood_bias_table",
    )(jnp.asarray(tile_of.reshape(-1)), tiles)


def _bias_table_kernel(tile_of_ref, tiles_ref, o_ref):
    v = pl.program_id(0)
    for kl in range(NA_KROWS):
        row = [tiles_ref[0, tile_of_ref[(v * NA_KROWS + kl) * NA_QROWS + ql]] for ql in range(NA_QROWS)]
        o_ref[0, 0, kl * GRID_W:(kl + 1) * GRID_W, :] = jnp.concatenate(row, axis=1)


def _merge_kernel(xp_ref, oap_ref, obp_ref, xs_ref, oas_ref, obs_ref, ona_ref, onb_ref, wo_ref,
                  g1_ref, sh2_ref, sc2_ref, n2_ref, wrh_ref, wrl_ref, rb_ref, tri_ref,
                  y_ref, hp_ref, gates_ref, rank_ref, count_ref, *, ctx_tiles):
    i = pl.program_id(0)

    @pl.when(i == 0)
    def _():
        count_ref[...] = jnp.zeros_like(count_ref)

    def one_stream(x_ref, oa_ref, ob_ref):
        na = (_rms(oa_ref[...]) * ona_ref[...]).astype(BF16)
        nb = (_rms(ob_ref[...]) * onb_ref[...]).astype(BF16)
        mix = _dot(na, wo_ref[0:WIDTH_A, :]) + _dot(nb, wo_ref[WIDTH_A:WIDTH_A + WIDTH_B, :])
        y = x_ref[...] + g1_ref[0] * mix
        y_ref[...] = y
        h = _rms(y) * n2_ref[...]
        h = h * (1.0 + sc2_ref[0]) + sh2_ref[0]
        hp_ref[...] = _pack_rows(h)
        gates, chosen = _router_gates(h, wrh_ref[...], wrl_ref[...], rb_ref[...])
        gates_ref[...] = gates
        before = _dot(chosen.astype(BF16), tri_ref[...])
        seen = count_ref[...]
        rank_ref[...] = jnp.where(chosen > 0.0, before + seen[:, 0:1], -1.0)
        count_ref[...] = seen + jnp.sum(chosen, axis=1, keepdims=True)

    pl.when(i < ctx_tiles)(lambda: one_stream(xp_ref, oap_ref, obp_ref))
    pl.when(i >= ctx_tiles)(lambda: one_stream(xs_ref, oas_ref, obs_ref))


def _merge(ctx, lat, on_a, on_b, w_out_bf, gate1, shift2, scale2, norm2, wr_hi, wr_lo, rbias, *, tm, lat_seq):
    t_c, d = ctx[0].shape
    t_l = lat[0].shape[0]
    t = t_c + t_l
    tri = jnp.asarray(np.triu(np.ones((tm, tm), np.float32), k=1), BF16)
    ctx_tiles = t_c // tm
    lat_tiles_per_batch = lat_seq // tm

    def ctx_map(i):
        return (jnp.minimum(i, ctx_tiles - 1), 0)

    def lat_map(i):
        return (jnp.maximum(i - ctx_tiles, 0), 0)

    def mod_map(i):
        return (jnp.where(i < ctx_tiles, 0, 1 + (i - ctx_tiles) // lat_tiles_per_batch), 0, 0)

    def stream_specs(index_map):
        return [pl.BlockSpec((tm, d), index_map),
                pl.BlockSpec((tm, WIDTH_A), index_map),
                pl.BlockSpec((tm, WIDTH_B), index_map)]

    return pl.pallas_call(
        functools.partial(_merge_kernel, ctx_tiles=ctx_tiles),
        out_shape=[jax.ShapeDtypeStruct((t, d), F32),
                   jax.ShapeDtypeStruct((t, d // 2), jnp.int32),
                   jax.ShapeDtypeStruct((N_EXPERTS, t), F32),
                   jax.ShapeDtypeStruct((N_EXPERTS, t), F32),
                   jax.ShapeDtypeStruct((N_EXPERTS, LANES), F32)],
        grid=(t // tm,),
        in_specs=stream_specs(ctx_map) + stream_specs(lat_map) + [
            pl.BlockSpec((1, WIDTH_A), lambda i: (0, 0)),
            pl.BlockSpec((1, WIDTH_B), lambda i: (0, 0)),
            pl.BlockSpec((WIDTH_A + WIDTH_B, d), lambda i: (0, 0)),
            pl.BlockSpec((1, 1, d), mod_map),
            pl.BlockSpec((1, 1, d), mod_map),
            pl.BlockSpec((1, 1, d), mod_map),
            pl.BlockSpec((1, d), lambda i: (0, 0)),
            pl.BlockSpec((N_EXPERTS, d), lambda i: (0, 0)),
            pl.BlockSpec((N_EXPERTS, d), lambda i: (0, 0)),
            pl.BlockSpec((N_EXPERTS, 1), lambda i: (0, 0)),
            pl.BlockSpec((tm, tm), lambda i: (0, 0))],
        out_specs=[pl.BlockSpec((tm, d), lambda i: (i, 0)),
                   pl.BlockSpec((tm, d // 2), lambda i: (i, 0)),
                   pl.BlockSpec((N_EXPERTS, tm), lambda i: (0, i)),
                   pl.BlockSpec((N_EXPERTS, tm), lambda i: (0, i)),
                   pl.BlockSpec((N_EXPERTS, LANES), lambda i: (0, 0))],
        compiler_params=_cparams(("arbitrary",)),
        name="merge_route",
    )(*ctx, *lat, on_a, on_b, w_out_bf, gate1, shift2, scale2, norm2, wr_hi, wr_lo, rbias, tri)


def _first_index_of_max(x, iota):
    mx = jnp.max(x, axis=0, keepdims=True)
    idx = jnp.min(jnp.where(x == mx, iota, float(x.shape[0])), axis=0, keepdims=True)
    return mx, iota == idx


def _router_gates(h, wr_hi, wr_lo, rbias):
    h_hi = h.astype(BF16)
    h_lo = (h - h_hi.astype(F32)).astype(BF16)
    logits = _dot_nt(wr_hi, h_hi) + (_dot_nt(wr_lo, h_hi) + _dot_nt(wr_hi, h_lo))
    scores = _sigmoid(logits)
    sel = scores + rbias
    tm = sel.shape[1]
    iota_g = lax.broadcasted_iota(jnp.int32, (GROUP_SIZE, tm), 0).astype(F32)
    group_scores = []
    for g in range(N_GROUPS):
        grp = sel[g * GROUP_SIZE:(g + 1) * GROUP_SIZE]
        m1, first = _first_index_of_max(grp, iota_g)
        m2 = jnp.max(jnp.where(first, -jnp.inf, grp), axis=0, keepdims=True)
        group_scores.append(m1 + m2)
    gs = jnp.concatenate(group_scores, axis=0)
    iota_n = lax.broadcasted_iota(jnp.int32, (N_GROUPS, tm), 0).astype(F32)
    group_on = jnp.zeros((N_GROUPS, tm), F32)
    for _ in range(TOPK_GROUPS):
        _, pick = _first_index_of_max(gs, iota_n)
        group_on = jnp.where(pick, 1.0, group_on)
        gs = jnp.where(pick, -jnp.inf, gs)
    expert_on = jnp.concatenate(
        [jnp.broadcast_to(group_on[g:g + 1], (GROUP_SIZE, tm)) for g in range(N_GROUPS)], axis=0)
    cand = jnp.where(expert_on > 0.0, sel, -jnp.inf)
    iota_e = lax.broadcasted_iota(jnp.int32, (N_EXPERTS, tm), 0).astype(F32)
    w = jnp.zeros((N_EXPERTS, tm), F32)
    chosen = jnp.zeros((N_EXPERTS, tm), F32)
    for _ in range(TOP_K):
        _, pick = _first_index_of_max(cand, iota_e)
        w = jnp.where(pick, scores, w)
        chosen = jnp.where(pick, 1.0, chosen)
        cand = jnp.where(pick, -jnp.inf, cand)
    return w / jnp.sum(w, axis=0, keepdims=True) * ROUTED_SCALE, chosen


MOE_TS = 1024
MOE_ROUTE_TM = 1024
MOE_ROW_TM = 512


def _slots_kernel(gates_ref, rank_ref, off_ref, pos_ref, gtok_ref):
    gates = gates_ref[...]
    rank = rank_ref[...]
    tm = gates.shape[1]
    slot = off_ref[...] + rank
    left = jnp.where(rank >= 0.0, 1.0, 0.0)
    iota_e = lax.broadcasted_iota(jnp.int32, (N_EXPERTS, tm), 0).astype(F32)
    pos_rows, gate_rows = [], []
    for _ in range(TOP_K):
        _, pick = _first_index_of_max(left, iota_e)
        pos_rows.append(jnp.sum(jnp.where(pick, slot, 0.0), axis=0, keepdims=True))
        gate_rows.append(jnp.sum(jnp.where(pick, gates, 0.0), axis=0, keepdims=True))
        left = jnp.where(pick, 0.0, left)
    pos_ref[...] = jnp.concatenate(pos_rows, axis=0).astype(jnp.int32)
    pad = jnp.zeros((LANES - TOP_K, tm), F32)
    gtok_ref[...] = jnp.concatenate(gate_rows + [pad], axis=0).T


def _slots(gates_t, rank_t, off):
    t = gates_t.shape[1]
    tm = MOE_ROUTE_TM
    return pl.pallas_call(
        _slots_kernel,
        out_shape=[jax.ShapeDtypeStruct((TOP_K, t), jnp.int32), jax.ShapeDtypeStruct((t, LANES), F32)],
        grid=(t // tm,),
        in_specs=[pl.BlockSpec((N_EXPERTS, tm), lambda i: (0, i)),
                  pl.BlockSpec((N_EXPERTS, tm), lambda i: (0, i)),
                  pl.BlockSpec((N_EXPERTS, 1), lambda i: (0, 0))],
        out_specs=[pl.BlockSpec((TOP_K, tm), lambda i: (0, i)),
                   pl.BlockSpec((tm, LANES), lambda i: (i, 0))],
        compiler_params=_cparams(("arbitrary",)),
        name="moe_slots",
    )(gates_t, rank_t, off)


SC_CORES = 2
SC_SUBCORES = 16
SC_ROWS = 64


def _dispatch(hp_all, slot_of):
    t, width = hp_all.shape
    n_slots = slot_of.shape[0]
    n_pad = n_slots - TOP_K * t
    workers = SC_CORES * SC_SUBCORES
    per_worker = t // workers
    pad_per_worker = n_pad // workers
    assert per_worker * workers == t and per_worker % SC_ROWS == 0
    assert pad_per_worker * workers == n_pad and pad_per_worker % SC_ROWS == 0
    mesh = plsc.VectorSubcoreMesh(core_axis_name="core", subcore_axis_name="subcore")

    assert pad_per_worker % (TOP_K * SC_ROWS) == 0

    @functools.partial(
        pl.kernel, mesh=mesh,
        out_type=jax.ShapeDtypeStruct((n_slots, width), jnp.int32),
        scratch_types=[pltpu.VMEM((SC_ROWS,), jnp.int32) for _ in range(TOP_K)]
        + [pltpu.VMEM((SC_ROWS, width), jnp.int32), pltpu.SemaphoreType.DMA, pltpu.SemaphoreType.DMA],
    )
    def scatter_rows(h_hbm, slot_hbm, out_hbm, *scratch):
        idx = scratch[:TOP_K]
        rows_v, sem_idx, sem_rows = scratch[TOP_K:]
        worker = lax.axis_index("subcore") * SC_CORES + lax.axis_index("core")

        def scatter_group(index_starts):
            for k, start in enumerate(index_starts):
                pltpu.async_copy(slot_hbm.at[pl.ds(start, SC_ROWS)], idx[k], sem_idx)
            for k, start in enumerate(index_starts):
                pltpu.make_async_copy(slot_hbm.at[pl.ds(start, SC_ROWS)], idx[k], sem_idx).wait()
            for k in range(TOP_K):
                pltpu.async_copy(rows_v, out_hbm.at[idx[k]], sem_rows)
            for k in range(TOP_K):
                pltpu.make_async_copy(rows_v, out_hbm.at[idx[k]], sem_rows).wait()

        pltpu.sync_copy(h_hbm.at[pl.ds(0, SC_ROWS)], rows_v)
        pad_base = TOP_K * t + worker * pad_per_worker

        @pl.loop(0, pad_per_worker // (TOP_K * SC_ROWS))
        def _(j):
            first = pad_base + j * (TOP_K * SC_ROWS)
            scatter_group([first + k * SC_ROWS for k in range(TOP_K)])

        base = worker * per_worker

        @pl.loop(0, per_worker // SC_ROWS)
        def _(j):
            first = base + j * SC_ROWS
            pltpu.sync_copy(h_hbm.at[pl.ds(first, SC_ROWS)], rows_v)
            scatter_group([k * t + first for k in range(TOP_K)])

    return scatter_rows(hp_all, slot_of)


def _experts_kernel(te_ref, used_ref, xs_ref, wg_ref, wu_ref, wd_ref, ys_ref, wgu_bf, wd_bf):
    i = pl.program_id(0)

    @pl.when(i >= used_ref[0])
    def _():
        ys_ref[...] = jnp.zeros_like(ys_ref)

    @pl.when(i < used_ref[0])
    def _():
        @pl.when((i == 0) | (te_ref[i] != te_ref[jnp.maximum(i, 1) - 1]))
        def _():
            wgu_bf[:, 0:D_EXPERT] = wg_ref[0].astype(BF16)
            wgu_bf[:, D_EXPERT:2 * D_EXPERT] = wu_ref[0].astype(BF16)
            wd_bf[...] = wd_ref[0].astype(BF16)

        left, right = _unpack_rows(xs_ref[...])
        x = jnp.concatenate([left, right], axis=1).astype(BF16)
        gu = _dot(x, wgu_bf[...])
        g = gu[:, 0:D_EXPERT]
        u = gu[:, D_EXPERT:2 * D_EXPERT]
        act = (g * _sigmoid(g)) * u
        ys_ref[...] = _pack_rows(_dot(act.astype(BF16), wd_bf[...]))


def _experts(xs, tile_expert, tiles_used, w_gate, w_up, w_down):
    n_slots, width = xs.shape
    d = 2 * width
    ts = MOE_TS

    def x_map(i, te, used):
        return (jnp.minimum(i, used[0] - 1), 0)

    return pl.pallas_call(
        _experts_kernel,
        out_shape=jax.ShapeDtypeStruct((n_slots, width), jnp.int32),
        grid_spec=pltpu.PrefetchScalarGridSpec(
            num_scalar_prefetch=2,
            grid=(n_slots // ts,),
            in_specs=[pl.BlockSpec((ts, width), x_map),
                      pl.BlockSpec((1, d, D_EXPERT), lambda i, te, used: (te[i], 0, 0)),
                      pl.BlockSpec((1, d, D_EXPERT), lambda i, te, used: (te[i], 0, 0)),
                      pl.BlockSpec((1, D_EXPERT, d), lambda i, te, used: (te[i], 0, 0))],
            out_specs=pl.BlockSpec((ts, width), lambda i, te, used: (i, 0)),
            scratch_shapes=[pltpu.VMEM((d, 2 * D_EXPERT), BF16), pltpu.VMEM((D_EXPERT, d), BF16)]),
        compiler_params=_cparams(("arbitrary",)),
        name="moe_experts",
    )(tile_expert, tiles_used, xs, w_gate, w_up, w_down)


def _gather_slots(y_slots, slot_of, t):
    width = y_slots.shape[1]
    workers = SC_CORES * SC_SUBCORES
    per_worker = t // workers
    n_blocks = (per_worker // SC_ROWS) * TOP_K
    assert per_worker * workers == t and per_worker % SC_ROWS == 0 and n_blocks % 2 == 0
    mesh = plsc.VectorSubcoreMesh(core_axis_name="core", subcore_axis_name="subcore")

    @functools.partial(
        pl.kernel, mesh=mesh,
        out_type=jax.ShapeDtypeStruct((TOP_K * t, width), jnp.int32),
        scratch_types=[pltpu.VMEM((SC_ROWS,), jnp.int32), pltpu.VMEM((SC_ROWS,), jnp.int32),
                       pltpu.VMEM((SC_ROWS, width), jnp.int32), pltpu.VMEM((SC_ROWS, width), jnp.int32),
                       pltpu.SemaphoreType.DMA, pltpu.SemaphoreType.DMA],
    )
    def gather_rows(ys_hbm, slot_hbm, out_hbm, idx0, idx1, rows0, rows1, sem0, sem1):
        worker = lax.axis_index("subcore") * SC_CORES + lax.axis_index("core")
        base = worker * per_worker

        def first_row(n):
            return (n % TOP_K) * t + base + (n // TOP_K) * SC_ROWS

        def start(n, idx_v, rows_v, sem):
            pltpu.sync_copy(slot_hbm.at[pl.ds(first_row(n), SC_ROWS)], idx_v)
            pltpu.async_copy(ys_hbm.at[idx_v], rows_v, sem)

        def finish(n, idx_v, rows_v, sem):
            pltpu.make_async_copy(ys_hbm.at[idx_v], rows_v, sem).wait()
            pltpu.sync_copy(rows_v, out_hbm.at[pl.ds(first_row(n), SC_ROWS)])

        start(0, idx0, rows0, sem0)

        @pl.loop(0, n_blocks, step=2)
        def _(n):
            start(n + 1, idx1, rows1, sem1)
            finish(n, idx0, rows0, sem0)

            @pl.when(n + 2 < n_blocks)
            def _():
                start(n + 2, idx0, rows0, sem0)

            finish(n + 1, idx1, rows1, sem1)

    return gather_rows(y_slots, slot_of)


def _combine_kernel(y_ref, h_ref, g2_ref, gtok_ref, rows_ref, wgs_ref, wus_ref, wds_ref, o_ref):
    h_left, h_right = _unpack_rows(h_ref[...])
    h = jnp.concatenate([h_left, h_right], axis=1).astype(BF16)
    gs = _dot(h, wgs_ref[...])
    us = _dot(h, wus_ref[...])
    shared = _dot(((gs * _sigmoid(gs)) * us).astype(BF16), wds_ref[...])

    gtok = gtok_ref[...]
    acc_left = acc_right = None
    for k in range(TOP_K):
        left, right = _unpack_rows(rows_ref[k])
        gate = gtok[:, k:k + 1]
        acc_left = gate * left if acc_left is None else acc_left + gate * left
        acc_right = gate * right if acc_right is None else acc_right + gate * right
    routed = jnp.concatenate([acc_left, acc_right], axis=1)
    o_ref[...] = y_ref[...] + g2_ref[0] * (routed + shared)


def _combine(y_all, hp_all, gate2, gtok, rows, wgs, wus, wds, *, first_token, tokens, seq):
    d = y_all.shape[1]
    width = hp_all.shape[1]
    tm = MOE_ROW_TM
    tile0 = first_token // tm
    nb = gate2.shape[0]
    tiles_per_batch = seq // tm

    def mod_map(i):
        return ((i // tiles_per_batch) if nb > 1 else 0, 0, 0)

    return pl.pallas_call(
        _combine_kernel,
        out_shape=jax.ShapeDtypeStruct((tokens, d), F32),
        grid=(tokens // tm,),
        in_specs=[pl.BlockSpec((tm, d), lambda i: (tile0 + i, 0)),
                  pl.BlockSpec((tm, width), lambda i: (tile0 + i, 0)),
                  pl.BlockSpec((1, 1, d), mod_map),
                  pl.BlockSpec((tm, LANES), lambda i: (tile0 + i, 0)),
                  pl.BlockSpec((TOP_K, tm, width), lambda i: (0, tile0 + i, 0)),
                  pl.BlockSpec((d, D_SHARED), lambda i: (0, 0)),
                  pl.BlockSpec((d, D_SHARED), lambda i: (0, 0)),
                  pl.BlockSpec((D_SHARED, d), lambda i: (0, 0))],
        out_specs=pl.BlockSpec((tm, d), lambda i: (i, 0)),
        compiler_params=_cparams(("arbitrary",)),
        name="moe_combine",
    )(y_all, hp_all, gate2, gtok, rows, wgs, wus, wds)


def _expert_layout(counts, n_tiles):
    cnt = counts.astype(jnp.int32)
    tiles = (cnt + (MOE_TS - 1)) // MOE_TS
    last_tile = jnp.cumsum(tiles)
    off = (last_tile - tiles) * MOE_TS
    pad_lo = off + cnt
    pad_hi = (off + tiles * MOE_TS).at[N_EXPERTS - 1].set(n_tiles * MOE_TS)
    pad_cnt = pad_hi - pad_lo
    pad_last = jnp.cumsum(pad_cnt)
    shift = pad_lo - (pad_last - pad_cnt)
    j = jnp.arange(N_EXPERTS * MOE_TS, dtype=jnp.int32)
    past = (pad_last[None, :-1] <= j[:, None]).astype(jnp.int32)
    pad_slots = j + shift[0] + jnp.sum(past * (shift[1:] - shift[:-1])[None, :], axis=1)
    tile_ids = jnp.arange(n_tiles, dtype=jnp.int32)
    tile_expert = jnp.minimum(
        jnp.sum((last_tile[None, :] <= tile_ids[:, None]).astype(jnp.int32), axis=1), N_EXPERTS - 1)
    return off, pad_slots, tile_expert, last_tile[-1:].astype(jnp.int32)


def _rope_tables(n_tokens):
    t = jnp.arange(n_tokens)
    row = (t // GRID_W).astype(F32)
    col = (t % GRID_W).astype(F32)
    nf = HEAD_DIM // 4
    freqs = ROPE_THETA ** (-jnp.arange(nf, dtype=F32) / nf)
    ang_r = row[:, None] * freqs
    ang_c = col[:, None] * freqs
    cos = jnp.concatenate([jnp.cos(ang_r)] * 2 + [jnp.cos(ang_c)] * 2, axis=1)
    sin = jnp.concatenate([-jnp.sin(ang_r), jnp.sin(ang_r), -jnp.sin(ang_c), jnp.sin(ang_c)], axis=1)
    reps = LANES // HEAD_DIM
    return jnp.tile(cos, (1, reps)), jnp.tile(sin, (1, reps))


def _head_gains(qn_a, kn_a, qn_b, kn_b):
    ones = jnp.ones((HEAD_DIM,), F32)
    parts = ([qn_a] * N_HEADS_A + [kn_a] * N_KV_A + [ones] * N_KV_A
             + [qn_b] * N_HEADS_B + [kn_b] * N_HEADS_B + [ones] * N_HEADS_B)
    return jnp.concatenate(parts).reshape(1, IN_COLS).astype(F32)


def _same_head_indicator():
    i = np.arange(MXU_DIM)
    return jnp.asarray((i[:, None] // HEAD_DIM) == (i[None, :] // HEAD_DIM), BF16)


def _token_major(cache):
    b, h, s, hd = cache.shape
    return cache.transpose(0, 2, 1, 3).reshape(b, s, h * hd).astype(BF16)


def kernel(x_prompt, x_sample, cache_k_a, cache_v_a, cache_k_b, cache_v_b, c, c_ctx, w_mod, b_mod, norm1, norm2, w_in, qn_a, kn_a, qn_b, kn_b, rpb, on_a, on_b, w_out, w_router, router_bias, w_gate_e, w_up_e, w_down_e, w_gate_s, w_up_s, w_down_s):
    depth = w_mod.shape[0]
    assert depth == 1
    l = 0
    bp, sp, d = x_prompt.shape
    bs, ss, _ = x_sample.shape

    cvec = jnp.concatenate([c_ctx[None, :], c], axis=0)
    rows = -(-cvec.shape[0] // 8) * 8
    cvec = jnp.pad(cvec, ((0, rows - cvec.shape[0]), (0, 0)))
    mod = _adaln(cvec, w_mod[l], b_mod[l])
    mod_p = [m.reshape(1, 1, d) for m in jnp.split(mod[0:1], 6, axis=-1)]
    mod_s = [m.reshape(bs, 1, d) for m in jnp.split(mod[1:1 + bs], 6, axis=-1)]
    mod_all = [m.reshape(1 + bs, 1, d) for m in jnp.split(mod[0:1 + bs], 6, axis=-1)]

    w_in_bf = w_in[l].astype(BF16)
    w_out_bf = w_out[l].astype(BF16)
    gain = _head_gains(qn_a[l], kn_a[l], qn_b[l], kn_b[l])
    seg = _same_head_indicator()
    n1 = norm1[l].reshape(1, d)
    n2 = norm2[l].reshape(1, d)
    ona = on_a[l].reshape(1, WIDTH_A)
    onb = on_b[l].reshape(1, WIDTH_B)
    wr_t = w_router[l].T
    wr_hi = wr_t.astype(BF16)
    wr_lo = (wr_t - wr_hi.astype(F32)).astype(BF16)
    rbias = router_bias[l].reshape(N_EXPERTS, 1).astype(F32)
    wgs = w_gate_s[l].astype(BF16)
    wus = w_up_s[l].astype(BF16)
    wds = w_down_s[l].astype(BF16)
    t_p = bp * sp
    t_s = bs * ss
    t_all = t_p + t_s

    xp = x_prompt.reshape(t_p, d)
    proj_p, st_ka, st_va, st_kb, st_vb = _project(
        xp, mod_p[0], mod_p[1], n1, w_in_bf, gain, seg, None, tm=sp, seq=sp, states=True)
    oa_p, ob_p = _context_attention(proj_p, seq=sp)

    xs = x_sample.reshape(t_s, d)
    proj_s, = _project(xs, mod_s[0], mod_s[1], n1, w_in_bf, gain, seg, _rope_tables(ss),
                       tm=1024, seq=ss, states=False)
    oa_s = _latent_gqa(proj_s, _token_major(cache_k_a[:, l]), _token_major(cache_v_a[:, l]),
                       _query_norm_bound(qn_a[l]), seq=ss, tq=256)
    bias_t = _neighbourhood_bias(rpb[l], ss // GRID_W)
    ob_s = _latent_neighbourhood(proj_s, _token_major(cache_k_b[:, l]), _token_major(cache_v_b[:, l]),
                                 bias_t, _neighbourhood_bounds(qn_b[l], kn_b[l], rpb[l]), seq=ss)

    y1_all, hp_all, gates_t, rank_t, counts = _merge(
        (xp, oa_p, ob_p), (xs, oa_s, ob_s), ona, onb, w_out_bf, mod_all[2], mod_all[3], mod_all[4], n2,
        wr_hi, wr_lo, rbias, tm=512, lat_seq=ss)
    n_tiles = t_all * TOP_K // MOE_TS + N_EXPERTS
    off, pad_slots, tile_expert, tiles_used = _expert_layout(counts[:, 0], n_tiles)
    pos, gtok = _slots(gates_t, rank_t, off.astype(F32).reshape(N_EXPERTS, 1))
    slot_of = pos.reshape(TOP_K * t_all)
    x_slots = _dispatch(hp_all, jnp.concatenate([slot_of, pad_slots]))
    y_slots = _experts(x_slots, tile_expert, tiles_used, w_gate_e[l], w_up_e[l], w_down_e[l])
    rows = _gather_slots(y_slots, slot_of, t_all).reshape(TOP_K, t_all, d // 2)
    y_p = _combine(y1_all, hp_all, mod_p[5], gtok, rows, wgs, wus, wds,
                   first_token=0, tokens=t_p, seq=sp)
    y_s = _combine(y1_all, hp_all, mod_s[5], gtok, rows, wgs, wus, wds,
                   first_token=t_p, tokens=t_s, seq=ss)

    return (y_p.reshape(bp, sp, d), y_s.reshape(bs, ss, d), st_ka, st_va, st_kb, st_vb)
```

```python
import functools

import numpy as np
import jax
import jax.numpy as jnp
from jax import lax
from jax.experimental import pallas as pl
from jax.experimental.pallas import tpu as pltpu
from jax.experimental.pallas import tpu_sc as plsc

F32 = jnp.float32
BF16 = jnp.bfloat16

D_MODEL = 1024
HEAD_DIM = 64
N_HEADS_A = 8
N_KV_A = 2
GROUP_A = N_HEADS_A // N_KV_A
N_HEADS_B = 8
WIDTH_A = N_HEADS_A * HEAD_DIM
WIDTH_B = N_HEADS_B * HEAD_DIM
KV_WIDTH_A = N_KV_A * HEAD_DIM
IN_COLS = WIDTH_A + 2 * KV_WIDTH_A + 3 * WIDTH_B
GRID_W = 64
ROPE_THETA = 10000.0
NA_KH = 8
NA_KW = 16
N_EXPERTS = 64
N_GROUPS = 8
GROUP_SIZE = N_EXPERTS // N_GROUPS
TOPK_GROUPS = 4
TOP_K = 8
D_EXPERT = 256
D_SHARED = 256
ROUTED_SCALE = 2.5
EPS = 1e-6

LANES = 128
MXU_DIM = 256
MASKED = -1e30

COL_QA = 0
COL_KA = WIDTH_A
COL_VA = COL_KA + KV_WIDTH_A
COL_QB = COL_VA + KV_WIDTH_A
COL_KB = COL_QB + WIDTH_B
COL_VB = COL_KB + WIDTH_B

NA_QROWS = 8
NA_KROWS = 2 * NA_KH
NA_TQ = NA_QROWS * GRID_W
NA_TK = NA_KROWS * GRID_W
NA_KBLK = 256

VMEM_LIMIT = 56 * 1024 * 1024


def _cparams(sem):
    return pltpu.CompilerParams(dimension_semantics=sem, vmem_limit_bytes=VMEM_LIMIT)


def _dot(a, b):
    return jnp.dot(a, b, preferred_element_type=F32)


def _dot_nt(a, b):
    return lax.dot_general(a, b, (((1,), (1,)), ((), ())), preferred_element_type=F32)


def _sigmoid(x):
    return 1.0 / (1.0 + jnp.exp(-x))


def _rms(x):
    return x * lax.rsqrt(jnp.mean(x * x, axis=-1, keepdims=True) + EPS)


def _pack_rows(x):
    n = x.shape[1] // 2
    hi = lax.bitcast_convert_type(x[:, :n].astype(BF16).astype(F32), jnp.int32)
    lo = lax.bitcast_convert_type(x[:, n:].astype(BF16).astype(F32), jnp.int32)
    return hi | lax.shift_right_logical(lo, 16)


def _unpack_rows(w):
    left = lax.bitcast_convert_type(w & jnp.int32(-65536), F32)
    right = lax.bitcast_convert_type(lax.shift_left(w, 16), F32)
    return left, right


def _mod_kernel(c_ref, w_ref, b_ref, o_ref):
    c = c_ref[...]
    s = c * _sigmoid(c)
    o_ref[...] = jnp.dot(s, w_ref[...], preferred_element_type=F32,
                         precision=lax.Precision.HIGHEST) + b_ref[...]


def _adaln(cvec, w_mod, b_mod):
    rows, d = cvec.shape
    n = w_mod.shape[1]
    tn = 512
    return pl.pallas_call(
        _mod_kernel,
        out_shape=jax.ShapeDtypeStruct((rows, n), F32),
        grid=(n // tn,),
        in_specs=[pl.BlockSpec((rows, d), lambda j: (0, 0)),
                  pl.BlockSpec((d, tn), lambda j: (0, j)),
                  pl.BlockSpec((1, tn), lambda j: (0, j))],
        out_specs=pl.BlockSpec((rows, tn), lambda j: (0, j)),
        compiler_params=_cparams(("arbitrary",)),
        name="adaln_mod",
    )(cvec, w_mod, b_mod.reshape(1, n))


def _proj_chunks():
    def split(c0, width, step, *flags):
        return [(c0 + i, min(step, width - i)) + flags for i in range(0, width, step)]
    return (split(COL_QA, WIDTH_A, MXU_DIM, True, True, True)
            + split(COL_KA, KV_WIDTH_A, MXU_DIM, True, True, False)
            + split(COL_VA, KV_WIDTH_A, MXU_DIM, False, False, False)
            + split(COL_QB, WIDTH_B, MXU_DIM, True, False, True)
            + split(COL_KB, WIDTH_B, MXU_DIM, True, False, False)
            + split(COL_VB, WIDTH_B, MXU_DIM, False, False, False))


_PROJ_CHUNKS = _proj_chunks()


def _proj_kernel(*refs, rope, states):
    x_ref, sh_ref, sc_ref, n1_ref, w_ref, gain_ref, seg_ref = refs[:7]
    pos = 7
    if rope:
        cos_ref, sin_ref = refs[pos:pos + 2]
        pos += 2
    out_ref = refs[pos]
    pos += 1
    if states:
        ka_ref, va_ref, kb_ref, vb_ref = refs[pos:pos + 4]
        state_of = {COL_KA: ka_ref, COL_VA: va_ref, COL_KB: kb_ref, COL_VB: vb_ref}

    x = x_ref[...]
    h = _rms(x) * n1_ref[...]
    h = h * (1.0 + sc_ref[0]) + sh_ref[0]
    p = _dot(h.astype(BF16), w_ref[...])

    for c0, w, normed, roped, is_query in _PROJ_CHUNKS:
        pc = p[:, c0:c0 + w]
        if normed:
            seg = seg_ref[0:w, 0:w]
            sq = pc * pc
            hi = sq.astype(BF16)
            lo = (sq - hi.astype(F32)).astype(BF16)
            ss = _dot(hi, seg) + _dot(lo, seg)
            pc = pc * lax.rsqrt(ss * (1.0 / HEAD_DIM) + EPS) * gain_ref[:, c0:c0 + w]
        if states:
            for start, ref in state_of.items():
                if start <= c0 < start + ref.shape[2] * HEAD_DIM:
                    base = (c0 - start) // HEAD_DIM
                    for hh in range(w // HEAD_DIM):
                        ref[0, 0, base + hh] = pc[:, hh * HEAD_DIM:(hh + 1) * HEAD_DIM]
        if rope and roped:
            reps = w // LANES
            cos = jnp.concatenate([cos_ref[...]] * reps, axis=1) if reps > 1 else cos_ref[...]
            sin = jnp.concatenate([sin_ref[...]] * reps, axis=1) if reps > 1 else sin_ref[...]
            lane = lax.broadcasted_iota(jnp.int32, pc.shape, 1)
            first_half = (lane % (HEAD_DIM // 2)) < (HEAD_DIM // 4)
            partner = jnp.where(first_half,
                                pltpu.roll(pc, w - HEAD_DIM // 4, 1),
                                pltpu.roll(pc, HEAD_DIM // 4, 1))
            pc = pc * cos + partner * sin
        if is_query:
            pc = pc * (HEAD_DIM ** -0.5)
        out_ref[:, c0:c0 + w] = pc.astype(BF16)


def _project(x2d, shift, scale, norm1, w_in_bf, gain, seg, rope_tabs, *, tm, seq, states):
    t, d = x2d.shape
    nb = shift.shape[0]
    tiles_per_batch = seq // tm
    rope = rope_tabs is not None

    def mod_map(i):
        return ((i // tiles_per_batch) if nb > 1 else 0, 0, 0)

    in_specs = [pl.BlockSpec((tm, d), lambda i: (i, 0)),
                pl.BlockSpec((1, 1, d), mod_map),
                pl.BlockSpec((1, 1, d), mod_map),
                pl.BlockSpec((1, d), lambda i: (0, 0)),
                pl.BlockSpec((d, IN_COLS), lambda i: (0, 0)),
                pl.BlockSpec((1, IN_COLS), lambda i: (0, 0)),
                pl.BlockSpec((MXU_DIM, MXU_DIM), lambda i: (0, 0))]
    args = [x2d, shift, scale, norm1, w_in_bf, gain, seg]
    if rope:
        in_specs += [pl.BlockSpec((tm, LANES), lambda i: (i % tiles_per_batch, 0))] * 2
        args += list(rope_tabs)
    out_shape = [jax.ShapeDtypeStruct((t, IN_COLS), BF16)]
    out_specs = [pl.BlockSpec((tm, IN_COLS), lambda i: (i, 0))]
    if states:
        assert tm == seq
        b = t // seq
        for nh in (N_KV_A, N_KV_A, N_HEADS_B, N_HEADS_B):
            out_shape.append(jax.ShapeDtypeStruct((b, 1, nh, seq, HEAD_DIM), F32))
            out_specs.append(pl.BlockSpec((1, 1, nh, seq, HEAD_DIM), lambda i: (i, 0, 0, 0, 0)))
    return pl.pallas_call(
        functools.partial(_proj_kernel, rope=rope, states=states),
        out_shape=out_shape,
        grid=(t // tm,),
        in_specs=in_specs,
        out_specs=out_specs,
        compiler_params=_cparams(("arbitrary",)),
        name="proj_states" if states else "proj_rope",
    )(*args)


def _lane_half(shape):
    return lax.broadcasted_iota(jnp.int32, shape, 1) // HEAD_DIM


def _keep_half(x, half):
    return jnp.where(_lane_half(x.shape) == half, x, jnp.zeros_like(x))


def _transpose_bf16(x):
    return x.astype(F32).T.astype(BF16)


def _attend(q, keys, values_t, biases):
    return _softmax_av(_scores(q, keys, biases), values_t)


def _scores(q, keys, biases):
    scores = []
    for k, b in zip(keys, biases):
        s = _dot_nt(k, q)
        if b is not None:
            s = s + b
        scores.append(s)
    return scores


def _softmax_av(scores, values_t):
    m = functools.reduce(jnp.maximum, [jnp.max(s, axis=0, keepdims=True) for s in scores])
    denom = None
    out = None
    for s, vt in zip(scores, values_t):
        p = jnp.exp(s - m)
        ps = jnp.sum(p, axis=0, keepdims=True)
        po = _dot(vt, p.astype(BF16))
        denom = ps if denom is None else denom + ps
        out = po if out is None else out + po
    return out / denom


def _swap_halves(q_bf16):
    return pltpu.roll(q_bf16.astype(F32), HEAD_DIM, 1).astype(BF16)


def _gqa_heads(q_of_pair, keys_by_group, values_t):
    outs = []
    for h in range(N_HEADS_A):
        g = h // GROUP_A
        q = q_of_pair(h // 2)
        if h % 2 != g:
            q = _swap_halves(q)
        o = _attend(q, keys_by_group[g], values_t, [None] * len(values_t))
        outs.append(o[g * HEAD_DIM:(g + 1) * HEAD_DIM])
    return jnp.concatenate(outs, axis=0)


def _ctx_attn_kernel(p_ref, oa_ref, ob_ref):
    ka = p_ref[:, COL_KA:COL_KA + LANES]
    va_t = [_transpose_bf16(p_ref[:, COL_VA:COL_VA + LANES])]
    keys_by_group = [[_keep_half(ka, g)] for g in range(N_KV_A)]
    oa = _gqa_heads(lambda i: p_ref[:, COL_QA + i * LANES:COL_QA + (i + 1) * LANES],
                    keys_by_group, va_t)
    oa_ref[...] = oa.T

    outs = []
    for i in range(N_HEADS_B // 2):
        q = p_ref[:, COL_QB + i * LANES:COL_QB + (i + 1) * LANES]
        k = p_ref[:, COL_KB + i * LANES:COL_KB + (i + 1) * LANES]
        vt = [_transpose_bf16(p_ref[:, COL_VB + i * LANES:COL_VB + (i + 1) * LANES])]
        for half in range(2):
            o = _attend(q, [_keep_half(k, half)], vt, [None])
            outs.append(o[half * HEAD_DIM:(half + 1) * HEAD_DIM])
    ob_ref[...] = jnp.concatenate(outs, axis=0).T


def _context_attention(proj, *, seq):
    t = proj.shape[0]
    return pl.pallas_call(
        _ctx_attn_kernel,
        out_shape=[jax.ShapeDtypeStruct((t, WIDTH_A), F32), jax.ShapeDtypeStruct((t, WIDTH_B), F32)],
        grid=(t // seq,),
        in_specs=[pl.BlockSpec((seq, IN_COLS), lambda i: (i, 0))],
        out_specs=[pl.BlockSpec((seq, WIDTH_A), lambda i: (i, 0)),
                   pl.BlockSpec((seq, WIDTH_B), lambda i: (i, 0))],
        compiler_params=_cparams(("arbitrary",)),
        name="context_attention",
    )(proj)


ATTN_SAFE_SHIFT = 40.0
ONES_ROWS = 16


def _round_up_bf16(x):
    return (x * (1.0 + 2.0 ** -6)).astype(BF16).astype(F32)


def _query_norm_bound(gain):
    return jnp.max(jnp.abs(gain)).reshape(1, 1).astype(F32)


def _ones_lane(g):
    return (1 - g) * HEAD_DIM


def _gqa_latent_kernel(q_ref, k_ref, v_ref, ck_ref, cv_ref, qmax_ref, o_ref,
                       kg_ref, ckg_ref, vt_ref, cvt_ref, shift_ref):
    lane_k = lax.broadcasted_iota(jnp.int32, (1, LANES), 1)

    @pl.when(pl.program_id(1) == 0)
    def _():
        k = k_ref[...]
        ck = ck_ref[0]
        vt = v_ref[...].astype(F32).T
        cvt = cv_ref[0].astype(F32).T
        for g in range(N_KV_A):
            kf = _keep_half(k, g).astype(F32)
            ckf = _keep_half(ck, g).astype(F32)
            ksq = jnp.maximum(jnp.max(jnp.sum(kf * kf, axis=1, keepdims=True), axis=0, keepdims=True),
                              jnp.max(jnp.sum(ckf * ckf, axis=1, keepdims=True), axis=0, keepdims=True))
            shift_ref[g] = jnp.broadcast_to(_round_up_bf16(qmax_ref[...] * jnp.sqrt(ksq)), shift_ref.shape[1:])
            kg_ref[g] = jnp.where(lane_k == _ones_lane(g), 1.0, kf).astype(BF16)
            ckg_ref[g] = jnp.where(lane_k == _ones_lane(g), 1.0, ckf).astype(BF16)
            rows = slice(g * HEAD_DIM, (g + 1) * HEAD_DIM)
            vt_ref[g] = jnp.concatenate([vt[rows], jnp.ones((ONES_ROWS, vt.shape[1]), F32)], axis=0).astype(BF16)
            cvt_ref[g] = jnp.concatenate([cvt[rows], jnp.ones((ONES_ROWS, cvt.shape[1]), F32)], axis=0).astype(BF16)

    tq = q_ref.shape[0]
    lane_q = lax.broadcasted_iota(jnp.int32, (GROUP_A * tq, LANES), 1)
    queries, shifts = [], []
    for g in range(N_KV_A):
        qs = []
        for j in range(GROUP_A):
            h = g * GROUP_A + j
            q = q_ref[:, (h // 2) * LANES:(h // 2 + 1) * LANES].astype(F32)
            qs.append(q if h % 2 == g else pltpu.roll(q, HEAD_DIM, 1))
        queries.append(jnp.where(lane_q // HEAD_DIM == g, jnp.concatenate(qs, axis=0), 0.0))
        shifts.append(shift_ref[g][0:1, 0:1])
    safe = jnp.max(jnp.maximum(shift_ref[0], shift_ref[1])) <= ATTN_SAFE_SHIFT

    def attend(g, p_lat, p_ctx):
        half = p_lat.shape[1] // 2
        o = jnp.concatenate([_dot(vt_ref[g], p_lat[:, :half]) + _dot(cvt_ref[g], p_ctx[:, :half]),
                             _dot(vt_ref[g], p_lat[:, half:]) + _dot(cvt_ref[g], p_ctx[:, half:])],
                            axis=1)
        o = o[:HEAD_DIM] / o[HEAD_DIM:HEAD_DIM + 1]
        heads = jnp.concatenate([o[:, j * tq:(j + 1) * tq] for j in range(GROUP_A)], axis=0)
        o_ref[:, g * GROUP_A * HEAD_DIM:(g + 1) * GROUP_A * HEAD_DIM] = heads.T

    def with_bound():
        for g in range(N_KV_A):
            qa = jnp.where(lane_q == _ones_lane(g), -shifts[g], queries[g]).astype(BF16)
            attend(g, jnp.exp(_dot_nt(kg_ref[g], qa)).astype(BF16), jnp.exp(_dot_nt(ckg_ref[g], qa)).astype(BF16))

    def with_row_max():
        for g in range(N_KV_A):
            qa = queries[g].astype(BF16)
            s_lat = _dot_nt(kg_ref[g], qa)
            s_ctx = _dot_nt(ckg_ref[g], qa)
            m = jnp.maximum(jnp.max(s_lat, axis=0, keepdims=True), jnp.max(s_ctx, axis=0, keepdims=True))
            attend(g, jnp.exp(s_lat - m).astype(BF16), jnp.exp(s_ctx - m).astype(BF16))

    pl.when(safe)(with_bound)
    pl.when(jnp.logical_not(safe))(with_row_max)


def _latent_gqa(proj, ctx_k, ctx_v, qmax, *, seq, tq):
    t = proj.shape[0]
    b = t // seq
    nq = seq // tq
    past = ctx_k.shape[1]
    return pl.pallas_call(
        _gqa_latent_kernel,
        out_shape=jax.ShapeDtypeStruct((t, WIDTH_A), F32),
        grid=(b, nq),
        in_specs=[pl.BlockSpec((tq, WIDTH_A), lambda bi, qi: (bi * nq + qi, 0)),
                  pl.BlockSpec((seq, LANES), lambda bi, qi: (bi, COL_KA // LANES)),
                  pl.BlockSpec((seq, LANES), lambda bi, qi: (bi, COL_VA // LANES)),
                  pl.BlockSpec((1, past, LANES), lambda bi, qi: (bi, 0, 0)),
                  pl.BlockSpec((1, past, LANES), lambda bi, qi: (bi, 0, 0)),
                  pl.BlockSpec((1, 1), lambda bi, qi: (0, 0))],
        out_specs=pl.BlockSpec((tq, WIDTH_A), lambda bi, qi: (bi * nq + qi, 0)),
        scratch_shapes=[pltpu.VMEM((N_KV_A, seq, LANES), BF16),
                        pltpu.VMEM((N_KV_A, past, LANES), BF16),
                        pltpu.VMEM((N_KV_A, HEAD_DIM + ONES_ROWS, seq), BF16),
                        pltpu.VMEM((N_KV_A, HEAD_DIM + ONES_ROWS, past), BF16),
                        pltpu.VMEM((N_KV_A, 8, LANES), F32)],
        compiler_params=_cparams(("arbitrary", "arbitrary")),
        name="latent_gqa",
    )(proj, proj, proj, ctx_k, ctx_v, qmax)


def _na_kernel(q_ref, k_ref, v_ref, ck_ref, cv_ref, bias_ref, bound_ref, o_ref, keys_ref, vt_ref, shift_ref,
               *, rows):
    i = pl.program_id(2)
    n_kblk = k_ref.shape[0] // NA_KBLK
    lane_k = lax.broadcasted_iota(jnp.int32, (1, LANES), 1)
    one = jnp.ones((), BF16)

    @pl.when(i == 0)
    def _():
        k = k_ref[...]
        ck = ck_ref[0]
        for half in range(2):
            kh = jnp.where(lane_k == _ones_lane(half), one, _keep_half(k, half))
            keys_ref[half, 0:n_kblk] = kh.reshape(n_kblk, NA_KBLK, LANES)
            ckh = _keep_half(ck, half)
            keys_ref[half, n_kblk] = jnp.where(lane_k == _ones_lane(half), one, ckh)
            ckf = ckh.astype(F32)
            ctx_norm = jnp.sqrt(jnp.max(jnp.sum(ckf * ckf, axis=1, keepdims=True), axis=0, keepdims=True))
            consts = bound_ref[0, half:half + 1, :]
            kmax = jnp.maximum(ctx_norm, consts[:, 2:3])
            shift_ref[half] = jnp.broadcast_to(_round_up_bf16(consts[:, 0:1] * kmax + consts[:, 1:2]),
                                               shift_ref.shape[1:])
        ones_rows = jnp.ones((ONES_ROWS, NA_KBLK), F32)
        vt = v_ref[...].astype(F32).T
        for j in range(n_kblk):
            vt_ref[j] = jnp.concatenate([vt[:, j * NA_KBLK:(j + 1) * NA_KBLK], ones_rows], axis=0).astype(BF16)
        vt_ref[n_kblk] = jnp.concatenate([cv_ref[0].astype(F32).T, ones_rows], axis=0).astype(BF16)

    q = q_ref[...]
    lane_q = lax.broadcasted_iota(jnp.int32, q.shape, 1)
    first = _na_first_key_block(i, rows)
    n_qblk = rows // NA_QROWS
    variant = jnp.where(i == 0, 0, jnp.where(i == n_qblk - 1, 2, 1))
    blocks = [first + j for j in range(NA_TK // NA_KBLK)] + [n_kblk]
    values_t = [vt_ref[blk] for blk in blocks]
    heads = []
    for half in range(2):
        keys = [keys_ref[half, blk] for blk in blocks]
        biases = [bias_ref[variant, half, j * NA_KBLK:(j + 1) * NA_KBLK, :] for j in range(NA_TK // NA_KBLK)] + [None]
        heads.append((shift_ref[half][0:1, 0:1], keys, biases))
    safe = jnp.max(jnp.maximum(shift_ref[0], shift_ref[1])) <= ATTN_SAFE_SHIFT

    def attend(probabilities):
        outs = []
        for half, ps in enumerate(probabilities):
            o = functools.reduce(lambda a, b: a + b, [_dot(vt, p) for vt, p in zip(values_t, ps)])
            outs.append(o[half * HEAD_DIM:(half + 1) * HEAD_DIM] / o[2 * HEAD_DIM:2 * HEAD_DIM + 1])
        o_ref[...] = jnp.concatenate(outs, axis=0).T

    def with_bound():
        probabilities = []
        for half, (shift, keys, biases) in enumerate(heads):
            qa = jnp.where(lane_q == _ones_lane(half), (-shift).astype(BF16), _keep_half(q, half))
            probabilities.append([jnp.exp(s).astype(BF16) for s in _scores(qa, keys, biases)])
        attend(probabilities)

    def with_row_max():
        all_scores = [_scores(_keep_half(q, half), keys, biases) for half, (_, keys, biases) in enumerate(heads)]
        probabilities = []
        for scores in all_scores:
            m = functools.reduce(jnp.maximum, [jnp.max(s, axis=0, keepdims=True) for s in scores])
            probabilities.append([jnp.exp(s - m).astype(BF16) for s in scores])
        attend(probabilities)

    pl.when(safe)(with_bound)
    pl.when(jnp.logical_not(safe))(with_row_max)


def _na_first_key_block(i, rows):
    per_qblock = NA_QROWS * GRID_W // NA_KBLK
    lead = (NA_KH // 2) * GRID_W // NA_KBLK
    return jnp.clip(per_qblock * i - lead, 0, (rows - NA_KROWS) * GRID_W // NA_KBLK)


def _latent_neighbourhood(proj, ctx_k, ctx_v, bias_t, bounds, *, seq):
    t = proj.shape[0]
    b = t // seq
    rows = seq // GRID_W
    nblk = rows // NA_QROWS
    n_kblk = seq // NA_KBLK
    past = ctx_k.shape[1]
    assert past == NA_KBLK
    grid = (N_HEADS_B // 2, b, nblk)
    in_specs = [pl.BlockSpec((NA_TQ, LANES), lambda hp, bi, i: (bi * nblk + i, COL_QB // LANES + hp)),
                pl.BlockSpec((seq, LANES), lambda hp, bi, i: (bi, COL_KB // LANES + hp)),
                pl.BlockSpec((seq, LANES), lambda hp, bi, i: (bi, COL_VB // LANES + hp)),
                pl.BlockSpec((1, past, LANES), lambda hp, bi, i: (bi, 0, hp)),
                pl.BlockSpec((1, past, LANES), lambda hp, bi, i: (bi, 0, hp)),
                pl.BlockSpec((3, 2, NA_TK, NA_TQ), lambda hp, bi, i: (0, hp, 0, 0)),
                pl.BlockSpec((1, 2, LANES), lambda hp, bi, i: (hp, 0, 0))]
    return pl.pallas_call(
        functools.partial(_na_kernel, rows=rows),
        out_shape=jax.ShapeDtypeStruct((t, WIDTH_B), F32),
        grid=grid,
        in_specs=in_specs,
        out_specs=pl.BlockSpec((NA_TQ, LANES), lambda hp, bi, i: (bi * nblk + i, hp)),
        scratch_shapes=[pltpu.VMEM((2, n_kblk + 1, NA_KBLK, LANES), BF16),
                        pltpu.VMEM((n_kblk + 1, 2 * HEAD_DIM + ONES_ROWS, NA_KBLK), BF16),
                        pltpu.VMEM((2, 8, LANES), F32)],
        compiler_params=_cparams(("arbitrary", "arbitrary", "arbitrary")),
        name="latent_neighbourhood",
    )(proj, proj, proj, ctx_k, ctx_v, bias_t, bounds)


def _neighbourhood_bounds(qn_b, kn_b, rpb):
    n_heads = rpb.shape[0]
    qmax = jnp.broadcast_to(_query_norm_bound(qn_b), (n_heads, 1))
    kmax = jnp.broadcast_to(_query_norm_bound(kn_b) * (HEAD_DIM ** 0.5), (n_heads, 1))
    bmax = jnp.maximum(jnp.max(rpb.reshape(n_heads, -1), axis=1, keepdims=True), 0.0).astype(F32)
    table = jnp.concatenate([qmax, bmax, kmax, jnp.zeros((n_heads, LANES - 3), F32)], axis=1)
    return table.reshape(n_heads // 2, 2, LANES)


def _neighbourhood_bias(rpb, rows):
    nblk = rows // NA_QROWS
    n_dr = 2 * NA_KH - 1
    n_dc = 2 * NA_KW - 1
    kc = np.arange(GRID_W)[:, None]
    qc = np.arange(GRID_W)[None, :]
    ws = np.clip(qc - NA_KW // 2, 0, GRID_W - NA_KW)
    col_ok = (kc >= ws) & (kc < ws + NA_KW)
    dc = np.clip(kc - qc + NA_KW - 1, 0, n_dc - 1)
    dc_onehot = (dc[None] == np.arange(n_dc)[:, None, None]).astype(np.float32)
    tiles = jnp.einsum('hab,bkq->hakq', rpb.astype(F32), jnp.asarray(dc_onehot),
                       precision=lax.Precision.HIGHEST)
    tiles = jnp.where(jnp.asarray(col_ok)[None, None], tiles, MASKED)
    masked_tile = jnp.full((rpb.shape[0], 1, GRID_W, GRID_W), MASKED, F32)
    tiles = jnp.concatenate([tiles, masked_tile], axis=1)
    tile_of = np.zeros((3, NA_KROWS, NA_QROWS), np.int32)
    for v, i in enumerate((0, 1, nblk - 1)):
        r0 = i * NA_QROWS
        ks = int(np.clip(r0 - NA_KH // 2, 0, rows - NA_KROWS))
        for kl in range(NA_KROWS):
            for ql in range(NA_QROWS):
                kr, qr = ks + kl, r0 + ql
                rs = int(np.clip(qr - NA_KH // 2, 0, rows - NA_KH))
                ok = rs <= kr < rs + NA_KH
                tile_of[v, kl, ql] = (kr - qr + NA_KH - 1) if ok else n_dr
    n_heads, n_tiles = tiles.shape[:2]
    return pl.pallas_call(
        _bias_table_kernel,
        out_shape=jax.ShapeDtypeStruct((3, n_heads, NA_TK, NA_TQ), F32),
        grid_spec=pltpu.PrefetchScalarGridSpec(
            num_scalar_prefetch=1, grid=(3, n_heads),
            in_specs=[pl.BlockSpec((1, n_tiles, GRID_W, GRID_W), lambda v, h, tile_of_ref: (h, 0, 0, 0))],
            out_specs=pl.BlockSpec((1, 1, NA_TK, NA_TQ), lambda v, h, tile_of_ref: (v, h, 0, 0))),
        compiler_params=_cparams(("arbitrary", "arbitrary")),
        name="neighbourhood_bias_table",
    )(jnp.asarray(tile_of.reshape(-1)), tiles)


def _bias_table_kernel(tile_of_ref, tiles_ref, o_ref):
    v = pl.program_id(0)
    for kl in range(NA_KROWS):
        row = [tiles_ref[0, tile_of_ref[(v * NA_KROWS + kl) * NA_QROWS + ql]] for ql in range(NA_QROWS)]
        o_ref[0, 0, kl * GRID_W:(kl + 1) * GRID_W, :] = jnp.concatenate(row, axis=1)


def _merge_kernel(xp_ref, oap_ref, obp_ref, xs_ref, oas_ref, obs_ref, ona_ref, onb_ref, wo_ref,
                  g1_ref, sh2_ref, sc2_ref, n2_ref, wrh_ref, wrl_ref, rb_ref, tri_ref,
                  y_ref, hp_ref, gates_ref, rank_ref, count_ref, *, ctx_tiles):
    i = pl.program_id(0)

    @pl.when(i == 0)
    def _():
        count_ref[...] = jnp.zeros_like(count_ref)

    def one_stream(x_ref, oa_ref, ob_ref):
        na = (_rms(oa_ref[...]) * ona_ref[...]).astype(BF16)
        nb = (_rms(ob_ref[...]) * onb_ref[...]).astype(BF16)
        mix = _dot(na, wo_ref[0:WIDTH_A, :]) + _dot(nb, wo_ref[WIDTH_A:WIDTH_A + WIDTH_B, :])
        y = x_ref[...] + g1_ref[0] * mix
        y_ref[...] = y
        h = _rms(y) * n2_ref[...]
        h = h * (1.0 + sc2_ref[0]) + sh2_ref[0]
        hp_ref[...] = _pack_rows(h)
        gates, chosen = _router_gates(h, wrh_ref[...], wrl_ref[...], rb_ref[...])
        gates_ref[...] = gates
        before = _dot(chosen.astype(BF16), tri_ref[...])
        seen = count_ref[...]
        rank_ref[...] = jnp.where(chosen > 0.0, before + seen[:, 0:1], -1.0)
        count_ref[...] = seen + jnp.sum(chosen, axis=1, keepdims=True)

    pl.when(i < ctx_tiles)(lambda: one_stream(xp_ref, oap_ref, obp_ref))
    pl.when(i >= ctx_tiles)(lambda: one_stream(xs_ref, oas_ref, obs_ref))


def _merge(ctx, lat, on_a, on_b, w_out_bf, gate1, shift2, scale2, norm2, wr_hi, wr_lo, rbias, *, tm, lat_seq):
    t_c, d = ctx[0].shape
    t_l = lat[0].shape[0]
    t = t_c + t_l
    tri = jnp.asarray(np.triu(np.ones((tm, tm), np.float32), k=1), BF16)
    ctx_tiles = t_c // tm
    lat_tiles_per_batch = lat_seq // tm

    def ctx_map(i):
        return (jnp.minimum(i, ctx_tiles - 1), 0)

    def lat_map(i):
        return (jnp.maximum(i - ctx_tiles, 0), 0)

    def mod_map(i):
        return (jnp.where(i < ctx_tiles, 0, 1 + (i - ctx_tiles) // lat_tiles_per_batch), 0, 0)

    def stream_specs(index_map):
        return [pl.BlockSpec((tm, d), index_map),
                pl.BlockSpec((tm, WIDTH_A), index_map),
                pl.BlockSpec((tm, WIDTH_B), index_map)]

    return pl.pallas_call(
        functools.partial(_merge_kernel, ctx_tiles=ctx_tiles),
        out_shape=[jax.ShapeDtypeStruct((t, d), F32),
                   jax.ShapeDtypeStruct((t, d // 2), jnp.int32),
                   jax.ShapeDtypeStruct((N_EXPERTS, t), F32),
                   jax.ShapeDtypeStruct((N_EXPERTS, t), F32),
                   jax.ShapeDtypeStruct((N_EXPERTS, LANES), F32)],
        grid=(t // tm,),
        in_specs=stream_specs(ctx_map) + stream_specs(lat_map) + [
            pl.BlockSpec((1, WIDTH_A), lambda i: (0, 0)),
            pl.BlockSpec((1, WIDTH_B), lambda i: (0, 0)),
            pl.BlockSpec((WIDTH_A + WIDTH_B, d), lambda i: (0, 0)),
            pl.BlockSpec((1, 1, d), mod_map),
            pl.BlockSpec((1, 1, d), mod_map),
            pl.BlockSpec((1, 1, d), mod_map),
            pl.BlockSpec((1, d), lambda i: (0, 0)),
            pl.BlockSpec((N_EXPERTS, d), lambda i: (0, 0)),
            pl.BlockSpec((N_EXPERTS, d), lambda i: (0, 0)),
            pl.BlockSpec((N_EXPERTS, 1), lambda i: (0, 0)),
            pl.BlockSpec((tm, tm), lambda i: (0, 0))],
        out_specs=[pl.BlockSpec((tm, d), lambda i: (i, 0)),
                   pl.BlockSpec((tm, d // 2), lambda i: (i, 0)),
                   pl.BlockSpec((N_EXPERTS, tm), lambda i: (0, i)),
                   pl.BlockSpec((N_EXPERTS, tm), lambda i: (0, i)),
                   pl.BlockSpec((N_EXPERTS, LANES), lambda i: (0, 0))],
        compiler_params=_cparams(("arbitrary",)),
        name="merge_route",
    )(*ctx, *lat, on_a, on_b, w_out_bf, gate1, shift2, scale2, norm2, wr_hi, wr_lo, rbias, tri)


def _first_index_of_max(x, iota):
    mx = jnp.max(x, axis=0, keepdims=True)
    idx = jnp.min(jnp.where(x == mx, iota, float(x.shape[0])), axis=0, keepdims=True)
    return mx, iota == idx


def _router_gates(h, wr_hi, wr_lo, rbias):
    h_hi = h.astype(BF16)
    h_lo = (h - h_hi.astype(F32)).astype(BF16)
    logits = _dot_nt(wr_hi, h_hi) + (_dot_nt(wr_lo, h_hi) + _dot_nt(wr_hi, h_lo))
    scores = _sigmoid(logits)
    sel = scores + rbias
    tm = sel.shape[1]
    iota_g = lax.broadcasted_iota(jnp.int32, (GROUP_SIZE, tm), 0).astype(F32)
    group_scores = []
    for g in range(N_GROUPS):
        grp = sel[g * GROUP_SIZE:(g + 1) * GROUP_SIZE]
        m1, first = _first_index_of_max(grp, iota_g)
        m2 = jnp.max(jnp.where(first, -jnp.inf, grp), axis=0, keepdims=True)
        group_scores.append(m1 + m2)
    gs = jnp.concatenate(group_scores, axis=0)
    iota_n = lax.broadcasted_iota(jnp.int32, (N_GROUPS, tm), 0).astype(F32)
    group_on = jnp.zeros((N_GROUPS, tm), F32)
    for _ in range(TOPK_GROUPS):
        _, pick = _first_index_of_max(gs, iota_n)
        group_on = jnp.where(pick, 1.0, group_on)
        gs = jnp.where(pick, -jnp.inf, gs)
    expert_on = jnp.concatenate(
        [jnp.broadcast_to(group_on[g:g + 1], (GROUP_SIZE, tm)) for g in range(N_GROUPS)], axis=0)
    cand = jnp.where(expert_on > 0.0, sel, -jnp.inf)
    iota_e = lax.broadcasted_iota(jnp.int32, (N_EXPERTS, tm), 0).astype(F32)
    w = jnp.zeros((N_EXPERTS, tm), F32)
    chosen = jnp.zeros((N_EXPERTS, tm), F32)
    for _ in range(TOP_K):
        _, pick = _first_index_of_max(cand, iota_e)
        w = jnp.where(pick, scores, w)
        chosen = jnp.where(pick, 1.0, chosen)
        cand = jnp.where(pick, -jnp.inf, cand)
    return w / jnp.sum(w, axis=0, keepdims=True) * ROUTED_SCALE, chosen


MOE_TS = 1024
MOE_ROUTE_TM = 1024
MOE_ROW_TM = 512


def _slots_kernel(gates_ref, rank_ref, off_ref, pos_ref, gtok_ref):
    gates = gates_ref[...]
    rank = rank_ref[...]
    tm = gates.shape[1]
    slot = off_ref[...] + rank
    left = jnp.where(rank >= 0.0, 1.0, 0.0)
    iota_e = lax.broadcasted_iota(jnp.int32, (N_EXPERTS, tm), 0).astype(F32)
    pos_rows, gate_rows = [], []
    for _ in range(TOP_K):
        _, pick = _first_index_of_max(left, iota_e)
        pos_rows.append(jnp.sum(jnp.where(pick, slot, 0.0), axis=0, keepdims=True))
        gate_rows.append(jnp.sum(jnp.where(pick, gates, 0.0), axis=0, keepdims=True))
        left = jnp.where(pick, 0.0, left)
    pos_ref[...] = jnp.concatenate(pos_rows, axis=0).astype(jnp.int32)
    pad = jnp.zeros((LANES - TOP_K, tm), F32)
    gtok_ref[...] = jnp.concatenate(gate_rows + [pad], axis=0).T


def _slots(gates_t, rank_t, off):
    t = gates_t.shape[1]
    tm = MOE_ROUTE_TM
    return pl.pallas_call(
        _slots_kernel,
        out_shape=[jax.ShapeDtypeStruct((TOP_K, t), jnp.int32), jax.ShapeDtypeStruct((t, LANES), F32)],
        grid=(t // tm,),
        in_specs=[pl.BlockSpec((N_EXPERTS, tm), lambda i: (0, i)),
                  pl.BlockSpec((N_EXPERTS, tm), lambda i: (0, i)),
                  pl.BlockSpec((N_EXPERTS, 1), lambda i: (0, 0))],
        out_specs=[pl.BlockSpec((TOP_K, tm), lambda i: (0, i)),
                   pl.BlockSpec((tm, LANES), lambda i: (i, 0))],
        compiler_params=_cparams(("arbitrary",)),
        name="moe_slots",
    )(gates_t, rank_t, off)


SC_CORES = 2
SC_SUBCORES = 16
SC_ROWS = 64


def _dispatch(hp_all, slot_of):
    t, width = hp_all.shape
    n_slots = slot_of.shape[0]
    n_pad = n_slots - TOP_K * t
    workers = SC_CORES * SC_SUBCORES
    per_worker = t // workers
    pad_per_worker = n_pad // workers
    assert per_worker * workers == t and per_worker % SC_ROWS == 0
    assert pad_per_worker * workers == n_pad and pad_per_worker % SC_ROWS == 0
    mesh = plsc.VectorSubcoreMesh(core_axis_name="core", subcore_axis_name="subcore")

    assert pad_per_worker % (TOP_K * SC_ROWS) == 0 and (per_worker // SC_ROWS) % 2 == 0

    @functools.partial(
        pl.kernel, mesh=mesh,
        out_type=jax.ShapeDtypeStruct((n_slots, width), jnp.int32),
        scratch_types=[pltpu.VMEM((SC_ROWS,), jnp.int32) for _ in range(TOP_K)]
        + [pltpu.VMEM((SC_ROWS, width), jnp.int32), pltpu.VMEM((SC_ROWS, width), jnp.int32),
           pltpu.SemaphoreType.DMA, pltpu.SemaphoreType.DMA, pltpu.SemaphoreType.DMA],
    )
    def scatter_rows(h_hbm, slot_hbm, out_hbm, *scratch):
        idx = scratch[:TOP_K]
        rows_a, rows_b, sem_idx, sem_rows, sem_load = scratch[TOP_K:]
        worker = lax.axis_index("subcore") * SC_CORES + lax.axis_index("core")

        def scatter_group(rows_v, index_starts):
            for k, start in enumerate(index_starts):
                pltpu.async_copy(slot_hbm.at[pl.ds(start, SC_ROWS)], idx[k], sem_idx)
            for k, start in enumerate(index_starts):
                pltpu.make_async_copy(slot_hbm.at[pl.ds(start, SC_ROWS)], idx[k], sem_idx).wait()
            for k in range(TOP_K):
                pltpu.async_copy(rows_v, out_hbm.at[idx[k]], sem_rows)
            for k in range(TOP_K):
                pltpu.make_async_copy(rows_v, out_hbm.at[idx[k]], sem_rows).wait()

        pltpu.sync_copy(h_hbm.at[pl.ds(0, SC_ROWS)], rows_a)
        pad_base = TOP_K * t + worker * pad_per_worker

        @pl.loop(0, pad_per_worker // (TOP_K * SC_ROWS))
        def _(j):
            first = pad_base + j * (TOP_K * SC_ROWS)
            scatter_group(rows_a, [first + k * SC_ROWS for k in range(TOP_K)])

        base = worker * per_worker
        n_chunks = per_worker // SC_ROWS

        def load(j, rows_v):
            return pltpu.make_async_copy(h_hbm.at[pl.ds(base + j * SC_ROWS, SC_ROWS)], rows_v, sem_load)

        def scatter_chunk(j, rows_v):
            scatter_group(rows_v, [k * t + base + j * SC_ROWS for k in range(TOP_K)])

        load(0, rows_a).start()
        load(0, rows_a).wait()

        @pl.loop(0, n_chunks, step=2)
        def _(j):
            load(j + 1, rows_b).start()
            scatter_chunk(j, rows_a)
            load(j + 1, rows_b).wait()

            @pl.when(j + 2 < n_chunks)
            def _():
                load(j + 2, rows_a).start()

            scatter_chunk(j + 1, rows_b)

            @pl.when(j + 2 < n_chunks)
            def _():
                load(j + 2, rows_a).wait()

    return scatter_rows(hp_all, slot_of)


def _experts_kernel(te_ref, used_ref, xs_ref, wg_ref, wu_ref, wd_ref, ys_ref, wgu_bf, wd_bf):
    i = pl.program_id(0)

    @pl.when(i >= used_ref[0])
    def _():
        ys_ref[...] = jnp.zeros_like(ys_ref)

    @pl.when(i < used_ref[0])
    def _():
        @pl.when((i == 0) | (te_ref[i] != te_ref[jnp.maximum(i, 1) - 1]))
        def _():
            wgu_bf[:, 0:D_EXPERT] = wg_ref[0].astype(BF16)
            wgu_bf[:, D_EXPERT:2 * D_EXPERT] = wu_ref[0].astype(BF16)
            wd_bf[...] = wd_ref[0].astype(BF16)

        left, right = _unpack_rows(xs_ref[...])
        x = jnp.concatenate([left, right], axis=1).astype(BF16)
        gu = _dot(x, wgu_bf[...])
        g = gu[:, 0:D_EXPERT]
        u = gu[:, D_EXPERT:2 * D_EXPERT]
        act = (g * _sigmoid(g)) * u
        ys_ref[...] = _pack_rows(_dot(act.astype(BF16), wd_bf[...]))


def _experts(xs, tile_expert, tiles_used, w_gate, w_up, w_down):
    n_slots, width = xs.shape
    d = 2 * width
    ts = MOE_TS

    def x_map(i, te, used):
        return (jnp.minimum(i, used[0] - 1), 0)

    return pl.pallas_call(
        _experts_kernel,
        out_shape=jax.ShapeDtypeStruct((n_slots, width), jnp.int32),
        grid_spec=pltpu.PrefetchScalarGridSpec(
            num_scalar_prefetch=2,
            grid=(n_slots // ts,),
            in_specs=[pl.BlockSpec((ts, width), x_map),
                      pl.BlockSpec((1, d, D_EXPERT), lambda i, te, used: (te[i], 0, 0)),
                      pl.BlockSpec((1, d, D_EXPERT), lambda i, te, used: (te[i], 0, 0)),
                      pl.BlockSpec((1, D_EXPERT, d), lambda i, te, used: (te[i], 0, 0))],
            out_specs=pl.BlockSpec((ts, width), lambda i, te, used: (i, 0)),
            scratch_shapes=[pltpu.VMEM((d, 2 * D_EXPERT), BF16), pltpu.VMEM((D_EXPERT, d), BF16)]),
        compiler_params=_cparams(("arbitrary",)),
        name="moe_experts",
    )(tile_expert, tiles_used, xs, w_gate, w_up, w_down)


def _gather_slots(y_slots, slot_of, t):
    width = y_slots.shape[1]
    workers = SC_CORES * SC_SUBCORES
    per_worker = t // workers
    n_blocks = (per_worker // SC_ROWS) * TOP_K
    assert per_worker * workers == t and per_worker % SC_ROWS == 0 and n_blocks % 2 == 0
    mesh = plsc.VectorSubcoreMesh(core_axis_name="core", subcore_axis_name="subcore")

    @functools.partial(
        pl.kernel, mesh=mesh,
        out_type=jax.ShapeDtypeStruct((TOP_K * t, width), jnp.int32),
        scratch_types=[pltpu.VMEM((SC_ROWS,), jnp.int32), pltpu.VMEM((SC_ROWS,), jnp.int32),
                       pltpu.VMEM((SC_ROWS, width), jnp.int32), pltpu.VMEM((SC_ROWS, width), jnp.int32),
                       pltpu.SemaphoreType.DMA, pltpu.SemaphoreType.DMA],
    )
    def gather_rows(ys_hbm, slot_hbm, out_hbm, idx0, idx1, rows0, rows1, sem0, sem1):
        worker = lax.axis_index("subcore") * SC_CORES + lax.axis_index("core")
        base = worker * per_worker

        def first_row(n):
            return (n % TOP_K) * t + base + (n // TOP_K) * SC_ROWS

        def start(n, idx_v, rows_v, sem):
            pltpu.sync_copy(slot_hbm.at[pl.ds(first_row(n), SC_ROWS)], idx_v)
            pltpu.async_copy(ys_hbm.at[idx_v], rows_v, sem)

        def finish(n, idx_v, rows_v, sem):
            pltpu.make_async_copy(ys_hbm.at[idx_v], rows_v, sem).wait()
            pltpu.sync_copy(rows_v, out_hbm.at[pl.ds(first_row(n), SC_ROWS)])

        start(0, idx0, rows0, sem0)

        @pl.loop(0, n_blocks, step=2)
        def _(n):
            start(n + 1, idx1, rows1, sem1)
            finish(n, idx0, rows0, sem0)

            @pl.when(n + 2 < n_blocks)
            def _():
                start(n + 2, idx0, rows0, sem0)

            finish(n + 1, idx1, rows1, sem1)

    return gather_rows(y_slots, slot_of)


def _combine_kernel(y_ref, h_ref, g2_ref, gtok_ref, rows_ref, wgs_ref, wus_ref, wds_ref, o_ref):
    h_left, h_right = _unpack_rows(h_ref[...])
    h = jnp.concatenate([h_left, h_right], axis=1).astype(BF16)
    gs = _dot(h, wgs_ref[...])
    us = _dot(h, wus_ref[...])
    shared = _dot(((gs * _sigmoid(gs)) * us).astype(BF16), wds_ref[...])

    gtok = gtok_ref[...]
    acc_left = acc_right = None
    for k in range(TOP_K):
        left, right = _unpack_rows(rows_ref[k])
        gate = gtok[:, k:k + 1]
        acc_left = gate * left if acc_left is None else acc_left + gate * left
        acc_right = gate * right if acc_right is None else acc_right + gate * right
    routed = jnp.concatenate([acc_left, acc_right], axis=1)
    o_ref[...] = y_ref[...] + g2_ref[0] * (routed + shared)


def _combine(y_all, hp_all, gate2, gtok, rows, wgs, wus, wds, *, first_token, tokens, seq):
    d = y_all.shape[1]
    width = hp_all.shape[1]
    tm = MOE_ROW_TM
    tile0 = first_token // tm
    nb = gate2.shape[0]
    tiles_per_batch = seq // tm

    def mod_map(i):
        return ((i // tiles_per_batch) if nb > 1 else 0, 0, 0)

    return pl.pallas_call(
        _combine_kernel,
        out_shape=jax.ShapeDtypeStruct((tokens, d), F32),
        grid=(tokens // tm,),
        in_specs=[pl.BlockSpec((tm, d), lambda i: (tile0 + i, 0)),
                  pl.BlockSpec((tm, width), lambda i: (tile0 + i, 0)),
                  pl.BlockSpec((1, 1, d), mod_map),
                  pl.BlockSpec((tm, LANES), lambda i: (tile0 + i, 0)),
                  pl.BlockSpec((TOP_K, tm, width), lambda i: (0, tile0 + i, 0)),
                  pl.BlockSpec((d, D_SHARED), lambda i: (0, 0)),
                  pl.BlockSpec((d, D_SHARED), lambda i: (0, 0)),
                  pl.BlockSpec((D_SHARED, d), lambda i: (0, 0))],
        out_specs=pl.BlockSpec((tm, d), lambda i: (i, 0)),
        compiler_params=_cparams(("arbitrary",)),
        name="moe_combine",
    )(y_all, hp_all, gate2, gtok, rows, wgs, wus, wds)


def _expert_layout(counts, n_tiles):
    cnt = counts.astype(jnp.int32)
    tiles = (cnt + (MOE_TS - 1)) // MOE_TS
    last_tile = jnp.cumsum(tiles)
    off = (last_tile - tiles) * MOE_TS
    pad_lo = off + cnt
    pad_hi = (off + tiles * MOE_TS).at[N_EXPERTS - 1].set(n_tiles * MOE_TS)
    pad_cnt = pad_hi - pad_lo
    pad_last = jnp.cumsum(pad_cnt)
    shift = pad_lo - (pad_last - pad_cnt)
    j = jnp.arange(N_EXPERTS * MOE_TS, dtype=jnp.int32)
    past = (pad_last[None, :-1] <= j[:, None]).astype(jnp.int32)
    pad_slots = j + shift[0] + jnp.sum(past * (shift[1:] - shift[:-1])[None, :], axis=1)
    tile_ids = jnp.arange(n_tiles, dtype=jnp.int32)
    tile_expert = jnp.minimum(
        jnp.sum((last_tile[None, :] <= tile_ids[:, None]).astype(jnp.int32), axis=1), N_EXPERTS - 1)
    return off, pad_slots, tile_expert, last_tile[-1:].astype(jnp.int32)


def _rope_tables(n_tokens):
    t = jnp.arange(n_tokens)
    row = (t // GRID_W).astype(F32)
    col = (t % GRID_W).astype(F32)
    nf = HEAD_DIM // 4
    freqs = ROPE_THETA ** (-jnp.arange(nf, dtype=F32) / nf)
    ang_r = row[:, None] * freqs
    ang_c = col[:, None] * freqs
    cos = jnp.concatenate([jnp.cos(ang_r)] * 2 + [jnp.cos(ang_c)] * 2, axis=1)
    sin = jnp.concatenate([-jnp.sin(ang_r), jnp.sin(ang_r), -jnp.sin(ang_c), jnp.sin(ang_c)], axis=1)
    reps = LANES // HEAD_DIM
    return jnp.tile(cos, (1, reps)), jnp.tile(sin, (1, reps))


def _head_gains(qn_a, kn_a, qn_b, kn_b):
    ones = jnp.ones((HEAD_DIM,), F32)
    parts = ([qn_a] * N_HEADS_A + [kn_a] * N_KV_A + [ones] * N_KV_A
             + [qn_b] * N_HEADS_B + [kn_b] * N_HEADS_B + [ones] * N_HEADS_B)
    return jnp.concatenate(parts).reshape(1, IN_COLS).astype(F32)


def _same_head_indicator():
    i = np.arange(MXU_DIM)
    return jnp.asarray((i[:, None] // HEAD_DIM) == (i[None, :] // HEAD_DIM), BF16)


def _token_major(cache):
    b, h, s, hd = cache.shape
    return cache.transpose(0, 2, 1, 3).reshape(b, s, h * hd).astype(BF16)


def kernel(x_prompt, x_sample, cache_k_a, cache_v_a, cache_k_b, cache_v_b, c, c_ctx, w_mod, b_mod, norm1, norm2, w_in, qn_a, kn_a, qn_b, kn_b, rpb, on_a, on_b, w_out, w_router, router_bias, w_gate_e, w_up_e, w_down_e, w_gate_s, w_up_s, w_down_s):
    depth = w_mod.shape[0]
    assert depth == 1
    l = 0
    bp, sp, d = x_prompt.shape
    bs, ss, _ = x_sample.shape

    cvec = jnp.concatenate([c_ctx[None, :], c], axis=0)
    rows = -(-cvec.shape[0] // 8) * 8
    cvec = jnp.pad(cvec, ((0, rows - cvec.shape[0]), (0, 0)))
    mod = _adaln(cvec, w_mod[l], b_mod[l])
    mod_p = [m.reshape(1, 1, d) for m in jnp.split(mod[0:1], 6, axis=-1)]
    mod_s = [m.reshape(bs, 1, d) for m in jnp.split(mod[1:1 + bs], 6, axis=-1)]
    mod_all = [m.reshape(1 + bs, 1, d) for m in jnp.split(mod[0:1 + bs], 6, axis=-1)]

    w_in_bf = w_in[l].astype(BF16)
    w_out_bf = w_out[l].astype(BF16)
    gain = _head_gains(qn_a[l], kn_a[l], qn_b[l], kn_b[l])
    seg = _same_head_indicator()
    n1 = norm1[l].reshape(1, d)
    n2 = norm2[l].reshape(1, d)
    ona = on_a[l].reshape(1, WIDTH_A)
    onb = on_b[l].reshape(1, WIDTH_B)
    wr_t = w_router[l].T
    wr_hi = wr_t.astype(BF16)
    wr_lo = (wr_t - wr_hi.astype(F32)).astype(BF16)
    rbias = router_bias[l].reshape(N_EXPERTS, 1).astype(F32)
    wgs = w_gate_s[l].astype(BF16)
    wus = w_up_s[l].astype(BF16)
    wds = w_down_s[l].astype(BF16)
    t_p = bp * sp
    t_s = bs * ss
    t_all = t_p + t_s

    xp = x_prompt.reshape(t_p, d)
    proj_p, st_ka, st_va, st_kb, st_vb = _project(
        xp, mod_p[0], mod_p[1], n1, w_in_bf, gain, seg, None, tm=sp, seq=sp, states=True)
    oa_p, ob_p = _context_attention(proj_p, seq=sp)

    xs = x_sample.reshape(t_s, d)
    proj_s, = _project(xs, mod_s[0], mod_s[1], n1, w_in_bf, gain, seg, _rope_tables(ss),
                       tm=1024, seq=ss, states=False)
    oa_s = _latent_gqa(proj_s, _token_major(cache_k_a[:, l]), _token_major(cache_v_a[:, l]),
                       _query_norm_bound(qn_a[l]), seq=ss, tq=256)
    bias_t = _neighbourhood_bias(rpb[l], ss // GRID_W)
    ob_s = _latent_neighbourhood(proj_s, _token_major(cache_k_b[:, l]), _token_major(cache_v_b[:, l]),
                                 bias_t, _neighbourhood_bounds(qn_b[l], kn_b[l], rpb[l]), seq=ss)

    y1_all, hp_all, gates_t, rank_t, counts = _merge(
        (xp, oa_p, ob_p), (xs, oa_s, ob_s), ona, onb, w_out_bf, mod_all[2], mod_all[3], mod_all[4], n2,
        wr_hi, wr_lo, rbias, tm=512, lat_seq=ss)
    n_tiles = t_all * TOP_K // MOE_TS + N_EXPERTS
    off, pad_slots, tile_expert, tiles_used = _expert_layout(counts[:, 0], n_tiles)
    pos, gtok = _slots(gates_t, rank_t, off.astype(F32).reshape(N_EXPERTS, 1))
    slot_of = pos.reshape(TOP_K * t_all)
    x_slots = _dispatch(hp_all, jnp.concatenate([slot_of, pad_slots]))
    y_slots = _experts(x_slots, tile_expert, tiles_used, w_gate_e[l], w_up_e[l], w_down_e[l])
    rows = _gather_slots(y_slots, slot_of, t_all).reshape(TOP_K, t_all, d // 2)
    y_p = _combine(y1_all, hp_all, mod_p[5], gtok, rows, wgs, wus, wds,
                   first_token=0, tokens=t_p, seq=sp)
    y_s = _combine(y1_all, hp_all, mod_s[5], gtok, rows, wgs, wus, wds,
                   first_token=t_p, tokens=t_s, seq=ss)

    return (y_p.reshape(bp, sp, d), y_s.reshape(bs, ss, d), st_ka, st_va, st_kb, st_vb)
```

```python
import functools

import numpy as np
import jax
import jax.numpy as jnp
from jax import lax
from jax.experimental import pallas as pl
from jax.experimental.pallas import tpu as pltpu
from jax.experimental.pallas import tpu_sc as plsc

F32 = jnp.float32
BF16 = jnp.bfloat16

D_MODEL = 1024
HEAD_DIM = 64
N_HEADS_A = 8
N_KV_A = 2
GROUP_A = N_HEADS_A // N_KV_A
N_HEADS_B = 8
WIDTH_A = N_HEADS_A * HEAD_DIM
WIDTH_B = N_HEADS_B * HEAD_DIM
KV_WIDTH_A = N_KV_A * HEAD_DIM
IN_COLS = WIDTH_A + 2 * KV_WIDTH_A + 3 * WIDTH_B
GRID_W = 64
ROPE_THETA = 10000.0
NA_KH = 8
NA_KW = 16
N_EXPERTS = 64
N_GROUPS = 8
GROUP_SIZE = N_EXPERTS // N_GROUPS
TOPK_GROUPS = 4
TOP_K = 8
D_EXPERT = 256
D_SHARED = 256
ROUTED_SCALE = 2.5
EPS = 1e-6

LANES = 128
MXU_DIM = 256
MASKED = -1e30

COL_QA = 0
COL_KA = WIDTH_A
COL_VA = COL_KA + KV_WIDTH_A
COL_QB = COL_VA + KV_WIDTH_A
COL_KB = COL_QB + WIDTH_B
COL_VB = COL_KB + WIDTH_B

NA_QROWS = 8
NA_KROWS = 2 * NA_KH
NA_TQ = NA_QROWS * GRID_W
NA_TK = NA_KROWS * GRID_W
NA_KBLK = 256

VMEM_LIMIT = 56 * 1024 * 1024


def _cparams(sem):
    return pltpu.CompilerParams(dimension_semantics=sem, vmem_limit_bytes=VMEM_LIMIT)


def _dot(a, b):
    return jnp.dot(a, b, preferred_element_type=F32)


def _dot_nt(a, b):
    return lax.dot_general(a, b, (((1,), (1,)), ((), ())), preferred_element_type=F32)


def _sigmoid(x):
    return 1.0 / (1.0 + jnp.exp(-x))


def _rms(x):
    return x * lax.rsqrt(jnp.mean(x * x, axis=-1, keepdims=True) + EPS)


def _pack_rows(x):
    n = x.shape[1] // 2
    hi = lax.bitcast_convert_type(x[:, :n].astype(BF16).astype(F32), jnp.int32)
    lo = lax.bitcast_convert_type(x[:, n:].astype(BF16).astype(F32), jnp.int32)
    return hi | lax.shift_right_logical(lo, 16)


def _unpack_rows(w):
    left = lax.bitcast_convert_type(w & jnp.int32(-65536), F32)
    right = lax.bitcast_convert_type(lax.shift_left(w, 16), F32)
    return left, right


def _mod_kernel(c_ref, w_ref, b_ref, o_ref):
    c = c_ref[...]
    s = c * _sigmoid(c)
    o_ref[...] = jnp.dot(s, w_ref[...], preferred_element_type=F32,
                         precision=lax.Precision.HIGHEST) + b_ref[...]


def _adaln(cvec, w_mod, b_mod):
    rows, d = cvec.shape
    n = w_mod.shape[1]
    tn = 512
    return pl.pallas_call(
        _mod_kernel,
        out_shape=jax.ShapeDtypeStruct((rows, n), F32),
        grid=(n // tn,),
        in_specs=[pl.BlockSpec((rows, d), lambda j: (0, 0)),
                  pl.BlockSpec((d, tn), lambda j: (0, j)),
                  pl.BlockSpec((1, tn), lambda j: (0, j))],
        out_specs=pl.BlockSpec((rows, tn), lambda j: (0, j)),
        compiler_params=_cparams(("arbitrary",)),
        name="adaln_mod",
    )(cvec, w_mod, b_mod.reshape(1, n))


def _proj_chunks():
    def split(c0, width, step, *flags):
        return [(c0 + i, min(step, width - i)) + flags for i in range(0, width, step)]
    return (split(COL_QA, WIDTH_A, MXU_DIM, True, True, True)
            + split(COL_KA, KV_WIDTH_A, MXU_DIM, True, True, False)
            + split(COL_VA, KV_WIDTH_A, MXU_DIM, False, False, False)
            + split(COL_QB, WIDTH_B, MXU_DIM, True, False, True)
            + split(COL_KB, WIDTH_B, MXU_DIM, True, False, False)
            + split(COL_VB, WIDTH_B, MXU_DIM, False, False, False))


_PROJ_CHUNKS = _proj_chunks()


def _proj_kernel(*refs, rope, states):
    x_ref, sh_ref, sc_ref, n1_ref, w_ref, gain_ref, seg_ref = refs[:7]
    pos = 7
    if rope:
        cos_ref, sin_ref = refs[pos:pos + 2]
        pos += 2
    out_ref = refs[pos]
    pos += 1
    if states:
        ka_ref, va_ref, kb_ref, vb_ref = refs[pos:pos + 4]
        state_of = {COL_KA: ka_ref, COL_VA: va_ref, COL_KB: kb_ref, COL_VB: vb_ref}

    x = x_ref[...]
    h = _rms(x) * n1_ref[...]
    h = h * (1.0 + sc_ref[0]) + sh_ref[0]
    p = _dot(h.astype(BF16), w_ref[...])

    for c0, w, normed, roped, is_query in _PROJ_CHUNKS:
        pc = p[:, c0:c0 + w]
        if normed:
            seg = seg_ref[0:w, 0:w]
            sq = pc * pc
            hi = sq.astype(BF16)
            lo = (sq - hi.astype(F32)).astype(BF16)
            ss = _dot(hi, seg) + _dot(lo, seg)
            pc = pc * lax.rsqrt(ss * (1.0 / HEAD_DIM) + EPS) * gain_ref[:, c0:c0 + w]
        if states:
            for start, ref in state_of.items():
                if start <= c0 < start + ref.shape[2] * HEAD_DIM:
                    base = (c0 - start) // HEAD_DIM
                    for hh in range(w // HEAD_DIM):
                        ref[0, 0, base + hh] = pc[:, hh * HEAD_DIM:(hh + 1) * HEAD_DIM]
        if rope and roped:
            reps = w // LANES
            cos = jnp.concatenate([cos_ref[...]] * reps, axis=1) if reps > 1 else cos_ref[...]
            sin = jnp.concatenate([sin_ref[...]] * reps, axis=1) if reps > 1 else sin_ref[...]
            lane = lax.broadcasted_iota(jnp.int32, pc.shape, 1)
            first_half = (lane % (HEAD_DIM // 2)) < (HEAD_DIM // 4)
            partner = jnp.where(first_half,
                                pltpu.roll(pc, w - HEAD_DIM // 4, 1),
                                pltpu.roll(pc, HEAD_DIM // 4, 1))
            pc = pc * cos + partner * sin
        if is_query:
            pc = pc * (HEAD_DIM ** -0.5)
        out_ref[:, c0:c0 + w] = pc.astype(BF16)


def _project(x2d, shift, scale, norm1, w_in_bf, gain, seg, rope_tabs, *, tm, seq, states):
    t, d = x2d.shape
    nb = shift.shape[0]
    tiles_per_batch = seq // tm
    rope = rope_tabs is not None

    def mod_map(i):
        return ((i // tiles_per_batch) if nb > 1 else 0, 0, 0)

    in_specs = [pl.BlockSpec((tm, d), lambda i: (i, 0)),
                pl.BlockSpec((1, 1, d), mod_map),
                pl.BlockSpec((1, 1, d), mod_map),
                pl.BlockSpec((1, d), lambda i: (0, 0)),
                pl.BlockSpec((d, IN_COLS), lambda i: (0, 0)),
                pl.BlockSpec((1, IN_COLS), lambda i: (0, 0)),
                pl.BlockSpec((MXU_DIM, MXU_DIM), lambda i: (0, 0))]
    args = [x2d, shift, scale, norm1, w_in_bf, gain, seg]
    if rope:
        in_specs += [pl.BlockSpec((tm, LANES), lambda i: (i % tiles_per_batch, 0))] * 2
        args += list(rope_tabs)
    out_shape = [jax.ShapeDtypeStruct((t, IN_COLS), BF16)]
    out_specs = [pl.BlockSpec((tm, IN_COLS), lambda i: (i, 0))]
    if states:
        assert tm == seq
        b = t // seq
        for nh in (N_KV_A, N_KV_A, N_HEADS_B, N_HEADS_B):
            out_shape.append(jax.ShapeDtypeStruct((b, 1, nh, seq, HEAD_DIM), F32))
            out_specs.append(pl.BlockSpec((1, 1, nh, seq, HEAD_DIM), lambda i: (i, 0, 0, 0, 0)))
    return pl.pallas_call(
        functools.partial(_proj_kernel, rope=rope, states=states),
        out_shape=out_shape,
        grid=(t // tm,),
        in_specs=in_specs,
        out_specs=out_specs,
        compiler_params=_cparams(("arbitrary",)),
        name="proj_states" if states else "proj_rope",
    )(*args)


def _lane_half(shape):
    return lax.broadcasted_iota(jnp.int32, shape, 1) // HEAD_DIM


def _keep_half(x, half):
    return jnp.where(_lane_half(x.shape) == half, x, jnp.zeros_like(x))


def _transpose_bf16(x):
    return x.astype(F32).T.astype(BF16)


def _attend(q, keys, values_t, biases):
    return _softmax_av(_scores(q, keys, biases), values_t)


def _scores(q, keys, biases):
    scores = []
    for k, b in zip(keys, biases):
        s = _dot_nt(k, q)
        if b is not None:
            s = s + b
        scores.append(s)
    return scores


def _softmax_av(scores, values_t):
    m = functools.reduce(jnp.maximum, [jnp.max(s, axis=0, keepdims=True) for s in scores])
    denom = None
    out = None
    for s, vt in zip(scores, values_t):
        p = jnp.exp(s - m)
        ps = jnp.sum(p, axis=0, keepdims=True)
        po = _dot(vt, p.astype(BF16))
        denom = ps if denom is None else denom + ps
        out = po if out is None else out + po
    return out / denom


def _swap_halves(q_bf16):
    return pltpu.roll(q_bf16.astype(F32), HEAD_DIM, 1).astype(BF16)


def _ctx_attn_kernel(p_ref, oa_ref, ob_ref):
    ka = p_ref[:, COL_KA:COL_KA + LANES]
    va_t = [_transpose_bf16(p_ref[:, COL_VA:COL_VA + LANES])]
    seq = p_ref.shape[0]
    outs = []
    for g in range(N_KV_A):
        qs = []
        for j in range(GROUP_A):
            h = g * GROUP_A + j
            q = p_ref[:, COL_QA + (h // 2) * LANES:COL_QA + (h // 2 + 1) * LANES]
            qs.append(q if h % 2 == g else _swap_halves(q))
        o = _softmax_av(_scores(jnp.concatenate(qs, axis=0), [_keep_half(ka, g)], [None]), va_t)
        o = o[g * HEAD_DIM:(g + 1) * HEAD_DIM]
        outs += [o[:, j * seq:(j + 1) * seq] for j in range(GROUP_A)]
    oa_ref[...] = jnp.concatenate(outs, axis=0).T

    outs = []
    for i in range(N_HEADS_B // 2):
        q = p_ref[:, COL_QB + i * LANES:COL_QB + (i + 1) * LANES]
        k = p_ref[:, COL_KB + i * LANES:COL_KB + (i + 1) * LANES]
        vt = [_transpose_bf16(p_ref[:, COL_VB + i * LANES:COL_VB + (i + 1) * LANES])]
        for half in range(2):
            o = _attend(q, [_keep_half(k, half)], vt, [None])
            outs.append(o[half * HEAD_DIM:(half + 1) * HEAD_DIM])
    ob_ref[...] = jnp.concatenate(outs, axis=0).T


def _context_attention(proj, *, seq):
    t = proj.shape[0]
    return pl.pallas_call(
        _ctx_attn_kernel,
        out_shape=[jax.ShapeDtypeStruct((t, WIDTH_A), F32), jax.ShapeDtypeStruct((t, WIDTH_B), F32)],
        grid=(t // seq,),
        in_specs=[pl.BlockSpec((seq, IN_COLS), lambda i: (i, 0))],
        out_specs=[pl.BlockSpec((seq, WIDTH_A), lambda i: (i, 0)),
                   pl.BlockSpec((seq, WIDTH_B), lambda i: (i, 0))],
        compiler_params=_cparams(("arbitrary",)),
        name="context_attention",
    )(proj)


ATTN_SAFE_SHIFT = 40.0
ONES_ROWS = 16


def _round_up_bf16(x):
    return (x * (1.0 + 2.0 ** -6)).astype(BF16).astype(F32)


def _query_norm_bound(gain):
    return jnp.max(jnp.abs(gain)).reshape(1, 1).astype(F32)


def _ones_lane(g):
    return (1 - g) * HEAD_DIM


def _gqa_latent_kernel(q_ref, k_ref, v_ref, ck_ref, cv_ref, qmax_ref, o_ref,
                       kg_ref, ckg_ref, vt_ref, cvt_ref, shift_ref):
    lane_k = lax.broadcasted_iota(jnp.int32, (1, LANES), 1)

    @pl.when(pl.program_id(1) == 0)
    def _():
        k = k_ref[...]
        ck = ck_ref[0]
        vt = v_ref[...].astype(F32).T
        cvt = cv_ref[0].astype(F32).T
        for g in range(N_KV_A):
            kf = _keep_half(k, g).astype(F32)
            ckf = _keep_half(ck, g).astype(F32)
            ksq = jnp.maximum(jnp.max(jnp.sum(kf * kf, axis=1, keepdims=True), axis=0, keepdims=True),
                              jnp.max(jnp.sum(ckf * ckf, axis=1, keepdims=True), axis=0, keepdims=True))
            shift_ref[g] = jnp.broadcast_to(_round_up_bf16(qmax_ref[...] * jnp.sqrt(ksq)), shift_ref.shape[1:])
            kg_ref[g] = jnp.where(lane_k == _ones_lane(g), 1.0, kf).astype(BF16)
            ckg_ref[g] = jnp.where(lane_k == _ones_lane(g), 1.0, ckf).astype(BF16)
            rows = slice(g * HEAD_DIM, (g + 1) * HEAD_DIM)
            vt_ref[g] = jnp.concatenate([vt[rows], jnp.ones((ONES_ROWS, vt.shape[1]), F32)], axis=0).astype(BF16)
            cvt_ref[g] = jnp.concatenate([cvt[rows], jnp.ones((ONES_ROWS, cvt.shape[1]), F32)], axis=0).astype(BF16)

    tq = q_ref.shape[0]
    lane_q = lax.broadcasted_iota(jnp.int32, (GROUP_A * tq, LANES), 1)
    queries, shifts = [], []
    for g in range(N_KV_A):
        qs = []
        for j in range(GROUP_A):
            h = g * GROUP_A + j
            q = q_ref[:, (h // 2) * LANES:(h // 2 + 1) * LANES].astype(F32)
            qs.append(q if h % 2 == g else pltpu.roll(q, HEAD_DIM, 1))
        queries.append(jnp.where(lane_q // HEAD_DIM == g, jnp.concatenate(qs, axis=0), 0.0))
        shifts.append(shift_ref[g][0:1, 0:1])
    safe = jnp.max(jnp.maximum(shift_ref[0], shift_ref[1])) <= ATTN_SAFE_SHIFT

    def attend(g, p_lat, p_ctx):
        half = p_lat.shape[1] // 2
        o = jnp.concatenate([_dot(vt_ref[g], p_lat[:, :half]) + _dot(cvt_ref[g], p_ctx[:, :half]),
                             _dot(vt_ref[g], p_lat[:, half:]) + _dot(cvt_ref[g], p_ctx[:, half:])],
                            axis=1)
        o = o[:HEAD_DIM] / o[HEAD_DIM:HEAD_DIM + 1]
        heads = jnp.concatenate([o[:, j * tq:(j + 1) * tq] for j in range(GROUP_A)], axis=0)
        o_ref[:, g * GROUP_A * HEAD_DIM:(g + 1) * GROUP_A * HEAD_DIM] = heads.T

    def with_bound():
        for g in range(N_KV_A):
            qa = jnp.where(lane_q == _ones_lane(g), -shifts[g], queries[g]).astype(BF16)
            attend(g, jnp.exp(_dot_nt(kg_ref[g], qa)).astype(BF16), jnp.exp(_dot_nt(ckg_ref[g], qa)).astype(BF16))

    def with_row_max():
        for g in range(N_KV_A):
            qa = queries[g].astype(BF16)
            s_lat = _dot_nt(kg_ref[g], qa)
            s_ctx = _dot_nt(ckg_ref[g], qa)
            m = jnp.maximum(jnp.max(s_lat, axis=0, keepdims=True), jnp.max(s_ctx, axis=0, keepdims=True))
            attend(g, jnp.exp(s_lat - m).astype(BF16), jnp.exp(s_ctx - m).astype(BF16))

    pl.when(safe)(with_bound)
    pl.when(jnp.logical_not(safe))(with_row_max)


def _latent_gqa(proj, ctx_k, ctx_v, qmax, *, seq, tq):
    t = proj.shape[0]
    b = t // seq
    nq = seq // tq
    past = ctx_k.shape[1]
    return pl.pallas_call(
        _gqa_latent_kernel,
        out_shape=jax.ShapeDtypeStruct((t, WIDTH_A), F32),
        grid=(b, nq),
        in_specs=[pl.BlockSpec((tq, WIDTH_A), lambda bi, qi: (bi * nq + qi, 0)),
                  pl.BlockSpec((seq, LANES), lambda bi, qi: (bi, COL_KA // LANES)),
                  pl.BlockSpec((seq, LANES), lambda bi, qi: (bi, COL_VA // LANES)),
                  pl.BlockSpec((1, past, LANES), lambda bi, qi: (bi, 0, 0)),
                  pl.BlockSpec((1, past, LANES), lambda bi, qi: (bi, 0, 0)),
                  pl.BlockSpec((1, 1), lambda bi, qi: (0, 0))],
        out_specs=pl.BlockSpec((tq, WIDTH_A), lambda bi, qi: (bi * nq + qi, 0)),
        scratch_shapes=[pltpu.VMEM((N_KV_A, seq, LANES), BF16),
                        pltpu.VMEM((N_KV_A, past, LANES), BF16),
                        pltpu.VMEM((N_KV_A, HEAD_DIM + ONES_ROWS, seq), BF16),
                        pltpu.VMEM((N_KV_A, HEAD_DIM + ONES_ROWS, past), BF16),
                        pltpu.VMEM((N_KV_A, 8, LANES), F32)],
        compiler_params=_cparams(("arbitrary", "arbitrary")),
        name="latent_gqa",
    )(proj, proj, proj, ctx_k, ctx_v, qmax)


def _na_kernel(q_ref, k_ref, v_ref, ck_ref, cv_ref, bias_ref, bound_ref, o_ref, keys_ref, vt_ref, shift_ref,
               *, rows):
    i = pl.program_id(2)
    n_kblk = k_ref.shape[0] // NA_KBLK
    lane_k = lax.broadcasted_iota(jnp.int32, (1, LANES), 1)
    one = jnp.ones((), BF16)

    @pl.when(i == 0)
    def _():
        k = k_ref[...]
        ck = ck_ref[0]
        for half in range(2):
            kh = jnp.where(lane_k == _ones_lane(half), one, _keep_half(k, half))
            keys_ref[half, 0:n_kblk] = kh.reshape(n_kblk, NA_KBLK, LANES)
            ckh = _keep_half(ck, half)
            keys_ref[half, n_kblk] = jnp.where(lane_k == _ones_lane(half), one, ckh)
            ckf = ckh.astype(F32)
            ctx_norm = jnp.sqrt(jnp.max(jnp.sum(ckf * ckf, axis=1, keepdims=True), axis=0, keepdims=True))
            consts = bound_ref[0, half:half + 1, :]
            kmax = jnp.maximum(ctx_norm, consts[:, 2:3])
            shift_ref[half] = jnp.broadcast_to(_round_up_bf16(consts[:, 0:1] * kmax + consts[:, 1:2]),
                                               shift_ref.shape[1:])
        ones_rows = jnp.ones((ONES_ROWS, NA_KBLK), F32)
        vt = v_ref[...].astype(F32).T
        for j in range(n_kblk):
            vt_ref[j] = jnp.concatenate([vt[:, j * NA_KBLK:(j + 1) * NA_KBLK], ones_rows], axis=0).astype(BF16)
        vt_ref[n_kblk] = jnp.concatenate([cv_ref[0].astype(F32).T, ones_rows], axis=0).astype(BF16)

    q = q_ref[...]
    lane_q = lax.broadcasted_iota(jnp.int32, q.shape, 1)
    first = _na_first_key_block(i, rows)
    n_qblk = rows // NA_QROWS
    variant = jnp.where(i == 0, 0, jnp.where(i == n_qblk - 1, 2, 1))
    blocks = [first + j for j in range(NA_TK // NA_KBLK)] + [n_kblk]
    values_t = [vt_ref[blk] for blk in blocks]
    heads = []
    for half in range(2):
        keys = [keys_ref[half, blk] for blk in blocks]
        biases = [bias_ref[variant, half, j * NA_KBLK:(j + 1) * NA_KBLK, :] for j in range(NA_TK // NA_KBLK)] + [None]
        heads.append((shift_ref[half][0:1, 0:1], keys, biases))
    safe = jnp.max(jnp.maximum(shift_ref[0], shift_ref[1])) <= ATTN_SAFE_SHIFT

    def attend(probabilities):
        outs = []
        for half, ps in enumerate(probabilities):
            o = functools.reduce(lambda a, b: a + b, [_dot(vt, p) for vt, p in zip(values_t, ps)])
            outs.append(o[half * HEAD_DIM:(half + 1) * HEAD_DIM] / o[2 * HEAD_DIM:2 * HEAD_DIM + 1])
        o_ref[...] = jnp.concatenate(outs, axis=0).T

    def with_bound():
        probabilities = []
        for half, (shift, keys, biases) in enumerate(heads):
            qa = jnp.where(lane_q == _ones_lane(half), (-shift).astype(BF16), _keep_half(q, half))
            probabilities.append([jnp.exp(s).astype(BF16) for s in _scores(qa, keys, biases)])
        attend(probabilities)

    def with_row_max():
        all_scores = [_scores(_keep_half(q, half), keys, biases) for half, (_, keys, biases) in enumerate(heads)]
        probabilities = []
        for scores in all_scores:
            m = functools.reduce(jnp.maximum, [jnp.max(s, axis=0, keepdims=True) for s in scores])
            probabilities.append([jnp.exp(s - m).astype(BF16) for s in scores])
        attend(probabilities)

    pl.when(safe)(with_bound)
    pl.when(jnp.logical_not(safe))(with_row_max)


def _na_first_key_block(i, rows):
    per_qblock = NA_QROWS * GRID_W // NA_KBLK
    lead = (NA_KH // 2) * GRID_W // NA_KBLK
    return jnp.clip(per_qblock * i - lead, 0, (rows - NA_KROWS) * GRID_W // NA_KBLK)


def _latent_neighbourhood(proj, ctx_k, ctx_v, bias_t, bounds, *, seq):
    t = proj.shape[0]
    b = t // seq
    rows = seq // GRID_W
    nblk = rows // NA_QROWS
    n_kblk = seq // NA_KBLK
    past = ctx_k.shape[1]
    assert past == NA_KBLK
    grid = (N_HEADS_B // 2, b, nblk)
    in_specs = [pl.BlockSpec((NA_TQ, LANES), lambda hp, bi, i: (bi * nblk + i, COL_QB // LANES + hp)),
                pl.BlockSpec((seq, LANES), lambda hp, bi, i: (bi, COL_KB // LANES + hp)),
                pl.BlockSpec((seq, LANES), lambda hp, bi, i: (bi, COL_VB // LANES + hp)),
                pl.BlockSpec((1, past, LANES), lambda hp, bi, i: (bi, 0, hp)),
                pl.BlockSpec((1, past, LANES), lambda hp, bi, i: (bi, 0, hp)),
                pl.BlockSpec((3, 2, NA_TK, NA_TQ), lambda hp, bi, i: (0, hp, 0, 0)),
                pl.BlockSpec((1, 2, LANES), lambda hp, bi, i: (hp, 0, 0))]
    return pl.pallas_call(
        functools.partial(_na_kernel, rows=rows),
        out_shape=jax.ShapeDtypeStruct((t, WIDTH_B), F32),
        grid=grid,
        in_specs=in_specs,
        out_specs=pl.BlockSpec((NA_TQ, LANES), lambda hp, bi, i: (bi * nblk + i, hp)),
        scratch_shapes=[pltpu.VMEM((2, n_kblk + 1, NA_KBLK, LANES), BF16),
                        pltpu.VMEM((n_kblk + 1, 2 * HEAD_DIM + ONES_ROWS, NA_KBLK), BF16),
                        pltpu.VMEM((2, 8, LANES), F32)],
        compiler_params=_cparams(("arbitrary", "arbitrary", "arbitrary")),
        name="latent_neighbourhood",
    )(proj, proj, proj, ctx_k, ctx_v, bias_t, bounds)


def _neighbourhood_bounds(qn_b, kn_b, rpb):
    n_heads = rpb.shape[0]
    qmax = jnp.broadcast_to(_query_norm_bound(qn_b), (n_heads, 1))
    kmax = jnp.broadcast_to(_query_norm_bound(kn_b) * (HEAD_DIM ** 0.5), (n_heads, 1))
    bmax = jnp.maximum(jnp.max(rpb.reshape(n_heads, -1), axis=1, keepdims=True), 0.0).astype(F32)
    table = jnp.concatenate([qmax, bmax, kmax, jnp.zeros((n_heads, LANES - 3), F32)], axis=1)
    return table.reshape(n_heads // 2, 2, LANES)


def _neighbourhood_bias(rpb, rows):
    nblk = rows // NA_QROWS
    n_dr = 2 * NA_KH - 1
    n_dc = 2 * NA_KW - 1
    kc = np.arange(GRID_W)[:, None]
    qc = np.arange(GRID_W)[None, :]
    ws = np.clip(qc - NA_KW // 2, 0, GRID_W - NA_KW)
    col_ok = (kc >= ws) & (kc < ws + NA_KW)
    dc = np.clip(kc - qc + NA_KW - 1, 0, n_dc - 1)
    dc_onehot = (dc[None] == np.arange(n_dc)[:, None, None]).astype(np.float32)
    tiles = jnp.einsum('hab,bkq->hakq', rpb.astype(F32), jnp.asarray(dc_onehot),
                       precision=lax.Precision.HIGHEST)
    tiles = jnp.where(jnp.asarray(col_ok)[None, None], tiles, MASKED)
    masked_tile = jnp.full((rpb.shape[0], 1, GRID_W, GRID_W), MASKED, F32)
    tiles = jnp.concatenate([tiles, masked_tile], axis=1)
    tile_of = np.zeros((3, NA_KROWS, NA_QROWS), np.int32)
    for v, i in enumerate((0, 1, nblk - 1)):
        r0 = i * NA_QROWS
        ks = int(np.clip(r0 - NA_KH // 2, 0, rows - NA_KROWS))
        for kl in range(NA_KROWS):
            for ql in range(NA_QROWS):
                kr, qr = ks + kl, r0 + ql
                rs = int(np.clip(qr - NA_KH // 2, 0, rows - NA_KH))
                ok = rs <= kr < rs + NA_KH
                tile_of[v, kl, ql] = (kr - qr + NA_KH - 1) if ok else n_dr
    n_heads, n_tiles = tiles.shape[:2]
    return pl.pallas_call(
        _bias_table_kernel,
        out_shape=jax.ShapeDtypeStruct((3, n_heads, NA_TK, NA_TQ), F32),
        grid_spec=pltpu.PrefetchScalarGridSpec(
            num_scalar_prefetch=1, grid=(3, n_heads),
            in_specs=[pl.BlockSpec((1, n_tiles, GRID_W, GRID_W), lambda v, h, tile_of_ref: (h, 0, 0, 0))],
            out_specs=pl.BlockSpec((1, 1, NA_TK, NA_TQ), lambda v, h, tile_of_ref: (v, h, 0, 0))),
        compiler_params=_cparams(("arbitrary", "arbitrary")),
        name="neighbourhood_bias_table",
    )(jnp.asarray(tile_of.reshape(-1)), tiles)


def _bias_table_kernel(tile_of_ref, tiles_ref, o_ref):
    v = pl.program_id(0)
    for kl in range(NA_KROWS):
        row = [tiles_ref[0, tile_of_ref[(v * NA_KROWS + kl) * NA_QROWS + ql]] for ql in range(NA_QROWS)]
        o_ref[0, 0, kl * GRID_W:(kl + 1) * GRID_W, :] = jnp.concatenate(row, axis=1)


def _merge_kernel(xp_ref, oap_ref, obp_ref, xs_ref, oas_ref, obs_ref, ona_ref, onb_ref, wo_ref,
                  g1_ref, sh2_ref, sc2_ref, n2_ref, wrh_ref, wrl_ref, rb_ref, tri_ref,
                  y_ref, hp_ref, gates_ref, rank_ref, count_ref, *, ctx_tiles):
    i = pl.program_id(0)

    @pl.when(i == 0)
    def _():
        count_ref[...] = jnp.zeros_like(count_ref)

    def one_stream(x_ref, oa_ref, ob_ref):
        na = (_rms(oa_ref[...]) * ona_ref[...]).astype(BF16)
        nb = (_rms(ob_ref[...]) * onb_ref[...]).astype(BF16)
        mix = _dot(na, wo_ref[0:WIDTH_A, :]) + _dot(nb, wo_ref[WIDTH_A:WIDTH_A + WIDTH_B, :])
        y = x_ref[...] + g1_ref[0] * mix
        y_ref[...] = y
        h = _rms(y) * n2_ref[...]
        h = h * (1.0 + sc2_ref[0]) + sh2_ref[0]
        hp_ref[...] = _pack_rows(h)
        gates, chosen = _router_gates(h, wrh_ref[...], wrl_ref[...], rb_ref[...])
        gates_ref[...] = gates
        before = _dot(chosen.astype(BF16), tri_ref[...])
        seen = count_ref[...]
        rank_ref[...] = jnp.where(chosen > 0.0, before + seen[:, 0:1], -1.0)
        count_ref[...] = seen + jnp.sum(chosen, axis=1, keepdims=True)

    pl.when(i < ctx_tiles)(lambda: one_stream(xp_ref, oap_ref, obp_ref))
    pl.when(i >= ctx_tiles)(lambda: one_stream(xs_ref, oas_ref, obs_ref))


def _merge(ctx, lat, on_a, on_b, w_out_bf, gate1, shift2, scale2, norm2, wr_hi, wr_lo, rbias, *, tm, lat_seq):
    t_c, d = ctx[0].shape
    t_l = lat[0].shape[0]
    t = t_c + t_l
    tri = jnp.asarray(np.triu(np.ones((tm, tm), np.float32), k=1), BF16)
    ctx_tiles = t_c // tm
    lat_tiles_per_batch = lat_seq // tm

    def ctx_map(i):
        return (jnp.minimum(i, ctx_tiles - 1), 0)

    def lat_map(i):
        return (jnp.maximum(i - ctx_tiles, 0), 0)

    def mod_map(i):
        return (jnp.where(i < ctx_tiles, 0, 1 + (i - ctx_tiles) // lat_tiles_per_batch), 0, 0)

    def stream_specs(index_map):
        return [pl.BlockSpec((tm, d), index_map),
                pl.BlockSpec((tm, WIDTH_A), index_map),
                pl.BlockSpec((tm, WIDTH_B), index_map)]

    return pl.pallas_call(
        functools.partial(_merge_kernel, ctx_tiles=ctx_tiles),
        out_shape=[jax.ShapeDtypeStruct((t, d), F32),
                   jax.ShapeDtypeStruct((t, d // 2), jnp.int32),
                   jax.ShapeDtypeStruct((N_EXPERTS, t), F32),
                   jax.ShapeDtypeStruct((N_EXPERTS, t), F32),
                   jax.ShapeDtypeStruct((N_EXPERTS, LANES), F32)],
        grid=(t // tm,),
        in_specs=stream_specs(ctx_map) + stream_specs(lat_map) + [
            pl.BlockSpec((1, WIDTH_A), lambda i: (0, 0)),
            pl.BlockSpec((1, WIDTH_B), lambda i: (0, 0)),
            pl.BlockSpec((WIDTH_A + WIDTH_B, d), lambda i: (0, 0)),
            pl.BlockSpec((1, 1, d), mod_map),
            pl.BlockSpec((1, 1, d), mod_map),
            pl.BlockSpec((1, 1, d), mod_map),
            pl.BlockSpec((1, d), lambda i: (0, 0)),
            pl.BlockSpec((N_EXPERTS, d), lambda i: (0, 0)),
            pl.BlockSpec((N_EXPERTS, d), lambda i: (0, 0)),
            pl.BlockSpec((N_EXPERTS, 1), lambda i: (0, 0)),
            pl.BlockSpec((tm, tm), lambda i: (0, 0))],
        out_specs=[pl.BlockSpec((tm, d), lambda i: (i, 0)),
                   pl.BlockSpec((tm, d // 2), lambda i: (i, 0)),
                   pl.BlockSpec((N_EXPERTS, tm), lambda i: (0, i)),
                   pl.BlockSpec((N_EXPERTS, tm), lambda i: (0, i)),
                   pl.BlockSpec((N_EXPERTS, LANES), lambda i: (0, 0))],
        compiler_params=_cparams(("arbitrary",)),
        name="merge_route",
    )(*ctx, *lat, on_a, on_b, w_out_bf, gate1, shift2, scale2, norm2, wr_hi, wr_lo, rbias, tri)


def _first_index_of_max(x, iota):
    mx = jnp.max(x, axis=0, keepdims=True)
    idx = jnp.min(jnp.where(x == mx, iota, float(x.shape[0])), axis=0, keepdims=True)
    return mx, iota == idx


def _router_gates(h, wr_hi, wr_lo, rbias):
    h_hi = h.astype(BF16)
    h_lo = (h - h_hi.astype(F32)).astype(BF16)
    logits = _dot_nt(wr_hi, h_hi) + (_dot_nt(wr_lo, h_hi) + _dot_nt(wr_hi, h_lo))
    scores = _sigmoid(logits)
    sel = scores + rbias
    tm = sel.shape[1]
    iota_g = lax.broadcasted_iota(jnp.int32, (GROUP_SIZE, tm), 0).astype(F32)
    group_scores = []
    for g in range(N_GROUPS):
        grp = sel[g * GROUP_SIZE:(g + 1) * GROUP_SIZE]
        m1, first = _first_index_of_max(grp, iota_g)
        m2 = jnp.max(jnp.where(first, -jnp.inf, grp), axis=0, keepdims=True)
        group_scores.append(m1 + m2)
    gs = jnp.concatenate(group_scores, axis=0)
    iota_n = lax.broadcasted_iota(jnp.int32, (N_GROUPS, tm), 0).astype(F32)
    group_on = jnp.zeros((N_GROUPS, tm), F32)
    for _ in range(TOPK_GROUPS):
        _, pick = _first_index_of_max(gs, iota_n)
        group_on = jnp.where(pick, 1.0, group_on)
        gs = jnp.where(pick, -jnp.inf, gs)
    expert_on = jnp.concatenate(
        [jnp.broadcast_to(group_on[g:g + 1], (GROUP_SIZE, tm)) for g in range(N_GROUPS)], axis=0)
    cand = jnp.where(expert_on > 0.0, sel, -jnp.inf)
    iota_e = lax.broadcasted_iota(jnp.int32, (N_EXPERTS, tm), 0).astype(F32)
    w = jnp.zeros((N_EXPERTS, tm), F32)
    chosen = jnp.zeros((N_EXPERTS, tm), F32)
    for _ in range(TOP_K):
        _, pick = _first_index_of_max(cand, iota_e)
        w = jnp.where(pick, scores, w)
        chosen = jnp.where(pick, 1.0, chosen)
        cand = jnp.where(pick, -jnp.inf, cand)
    return w / jnp.sum(w, axis=0, keepdims=True) * ROUTED_SCALE, chosen


MOE_TS = 1024
MOE_ROUTE_TM = 1024
MOE_ROW_TM = 512


def _slots_kernel(gates_ref, rank_ref, off_ref, pos_ref, gtok_ref):
    gates = gates_ref[...]
    rank = rank_ref[...]
    tm = gates.shape[1]
    slot = off_ref[...] + rank
    left = jnp.where(rank >= 0.0, 1.0, 0.0)
    iota_e = lax.broadcasted_iota(jnp.int32, (N_EXPERTS, tm), 0).astype(F32)
    pos_rows, gate_rows = [], []
    for _ in range(TOP_K):
        _, pick = _first_index_of_max(left, iota_e)
        pos_rows.append(jnp.sum(jnp.where(pick, slot, 0.0), axis=0, keepdims=True))
        gate_rows.append(jnp.sum(jnp.where(pick, gates, 0.0), axis=0, keepdims=True))
        left = jnp.where(pick, 0.0, left)
    pos_ref[...] = jnp.concatenate(pos_rows, axis=0).astype(jnp.int32)
    pad = jnp.zeros((LANES - TOP_K, tm), F32)
    gtok_ref[...] = jnp.concatenate(gate_rows + [pad], axis=0).T


def _slots(gates_t, rank_t, off):
    t = gates_t.shape[1]
    tm = MOE_ROUTE_TM
    return pl.pallas_call(
        _slots_kernel,
        out_shape=[jax.ShapeDtypeStruct((TOP_K, t), jnp.int32), jax.ShapeDtypeStruct((t, LANES), F32)],
        grid=(t // tm,),
        in_specs=[pl.BlockSpec((N_EXPERTS, tm), lambda i: (0, i)),
                  pl.BlockSpec((N_EXPERTS, tm), lambda i: (0, i)),
                  pl.BlockSpec((N_EXPERTS, 1), lambda i: (0, 0))],
        out_specs=[pl.BlockSpec((TOP_K, tm), lambda i: (0, i)),
                   pl.BlockSpec((tm, LANES), lambda i: (i, 0))],
        compiler_params=_cparams(("arbitrary",)),
        name="moe_slots",
    )(gates_t, rank_t, off)


SC_CORES = 2
SC_SUBCORES = 16
SC_ROWS = 64


def _dispatch(hp_all, slot_of):
    t, width = hp_all.shape
    n_slots = slot_of.shape[0]
    n_pad = n_slots - TOP_K * t
    workers = SC_CORES * SC_SUBCORES
    per_worker = t // workers
    pad_per_worker = n_pad // workers
    assert per_worker * workers == t and per_worker % SC_ROWS == 0
    assert pad_per_worker * workers == n_pad and pad_per_worker % SC_ROWS == 0
    mesh = plsc.VectorSubcoreMesh(core_axis_name="core", subcore_axis_name="subcore")

    assert pad_per_worker % (TOP_K * SC_ROWS) == 0 and (per_worker // SC_ROWS) % 2 == 0

    @functools.partial(
        pl.kernel, mesh=mesh,
        out_type=jax.ShapeDtypeStruct((n_slots, width), jnp.int32),
        scratch_types=[pltpu.VMEM((SC_ROWS,), jnp.int32) for _ in range(TOP_K)]
        + [pltpu.VMEM((SC_ROWS, width), jnp.int32), pltpu.VMEM((SC_ROWS, width), jnp.int32),
           pltpu.SemaphoreType.DMA, pltpu.SemaphoreType.DMA, pltpu.SemaphoreType.DMA],
    )
    def scatter_rows(h_hbm, slot_hbm, out_hbm, *scratch):
        idx = scratch[:TOP_K]
        rows_a, rows_b, sem_idx, sem_rows, sem_load = scratch[TOP_K:]
        worker = lax.axis_index("subcore") * SC_CORES + lax.axis_index("core")

        def scatter_group(rows_v, index_starts):
            for k, start in enumerate(index_starts):
                pltpu.async_copy(slot_hbm.at[pl.ds(start, SC_ROWS)], idx[k], sem_idx)
            for k, start in enumerate(index_starts):
                pltpu.make_async_copy(slot_hbm.at[pl.ds(start, SC_ROWS)], idx[k], sem_idx).wait()
            for k in range(TOP_K):
                pltpu.async_copy(rows_v, out_hbm.at[idx[k]], sem_rows)
            for k in range(TOP_K):
                pltpu.make_async_copy(rows_v, out_hbm.at[idx[k]], sem_rows).wait()

        pltpu.sync_copy(h_hbm.at[pl.ds(0, SC_ROWS)], rows_a)
        pad_base = TOP_K * t + worker * pad_per_worker

        @pl.loop(0, pad_per_worker // (TOP_K * SC_ROWS))
        def _(j):
            first = pad_base + j * (TOP_K * SC_ROWS)
            scatter_group(rows_a, [first + k * SC_ROWS for k in range(TOP_K)])

        base = worker * per_worker
        n_chunks = per_worker // SC_ROWS

        def load(j, rows_v):
            return pltpu.make_async_copy(h_hbm.at[pl.ds(base + j * SC_ROWS, SC_ROWS)], rows_v, sem_load)

        def scatter_chunk(j, rows_v):
            scatter_group(rows_v, [k * t + base + j * SC_ROWS for k in range(TOP_K)])

        load(0, rows_a).start()
        load(0, rows_a).wait()

        @pl.loop(0, n_chunks, step=2)
        def _(j):
            load(j + 1, rows_b).start()
            scatter_chunk(j, rows_a)
            load(j + 1, rows_b).wait()

            @pl.when(j + 2 < n_chunks)
            def _():
                load(j + 2, rows_a).start()

            scatter_chunk(j + 1, rows_b)

            @pl.when(j + 2 < n_chunks)
            def _():
                load(j + 2, rows_a).wait()

    return scatter_rows(hp_all, slot_of)


def _experts_kernel(te_ref, used_ref, xs_ref, wg_ref, wu_ref, wd_ref, ys_ref, wgu_bf, wd_bf):
    i = pl.program_id(0)

    @pl.when(i >= used_ref[0])
    def _():
        ys_ref[...] = jnp.zeros_like(ys_ref)

    @pl.when(i < used_ref[0])
    def _():
        @pl.when((i == 0) | (te_ref[i] != te_ref[jnp.maximum(i, 1) - 1]))
        def _():
            wgu_bf[:, 0:D_EXPERT] = wg_ref[0].astype(BF16)
            wgu_bf[:, D_EXPERT:2 * D_EXPERT] = wu_ref[0].astype(BF16)
            wd_bf[...] = wd_ref[0].astype(BF16)

        left, right = _unpack_rows(xs_ref[...])
        x = jnp.concatenate([left, right], axis=1).astype(BF16)
        gu = _dot(x, wgu_bf[...])
        g = gu[:, 0:D_EXPERT]
        u = gu[:, D_EXPERT:2 * D_EXPERT]
        act = (g * _sigmoid(g)) * u
        ys_ref[...] = _pack_rows(_dot(act.astype(BF16), wd_bf[...]))


def _experts(xs, tile_expert, tiles_used, w_gate, w_up, w_down):
    n_slots, width = xs.shape
    d = 2 * width
    ts = MOE_TS

    def x_map(i, te, used):
        return (jnp.minimum(i, used[0] - 1), 0)

    return pl.pallas_call(
        _experts_kernel,
        out_shape=jax.ShapeDtypeStruct((n_slots, width), jnp.int32),
        grid_spec=pltpu.PrefetchScalarGridSpec(
            num_scalar_prefetch=2,
            grid=(n_slots // ts,),
            in_specs=[pl.BlockSpec((ts, width), x_map),
                      pl.BlockSpec((1, d, D_EXPERT), lambda i, te, used: (te[i], 0, 0)),
                      pl.BlockSpec((1, d, D_EXPERT), lambda i, te, used: (te[i], 0, 0)),
                      pl.BlockSpec((1, D_EXPERT, d), lambda i, te, used: (te[i], 0, 0))],
            out_specs=pl.BlockSpec((ts, width), lambda i, te, used: (i, 0)),
            scratch_shapes=[pltpu.VMEM((d, 2 * D_EXPERT), BF16), pltpu.VMEM((D_EXPERT, d), BF16)]),
        compiler_params=_cparams(("arbitrary",)),
        name="moe_experts",
    )(tile_expert, tiles_used, xs, w_gate, w_up, w_down)


def _gather_slots(y_slots, slot_of, t):
    width = y_slots.shape[1]
    workers = SC_CORES * SC_SUBCORES
    per_worker = t // workers
    n_blocks = (per_worker // SC_ROWS) * TOP_K
    assert per_worker * workers == t and per_worker % SC_ROWS == 0 and n_blocks % 2 == 0
    mesh = plsc.VectorSubcoreMesh(core_axis_name="core", subcore_axis_name="subcore")

    @functools.partial(
        pl.kernel, mesh=mesh,
        out_type=jax.ShapeDtypeStruct((TOP_K * t, width), jnp.int32),
        scratch_types=[pltpu.VMEM((SC_ROWS,), jnp.int32), pltpu.VMEM((SC_ROWS,), jnp.int32),
                       pltpu.VMEM((SC_ROWS, width), jnp.int32), pltpu.VMEM((SC_ROWS, width), jnp.int32),
                       pltpu.SemaphoreType.DMA, pltpu.SemaphoreType.DMA],
    )
    def gather_rows(ys_hbm, slot_hbm, out_hbm, idx0, idx1, rows0, rows1, sem0, sem1):
        worker = lax.axis_index("subcore") * SC_CORES + lax.axis_index("core")
        base = worker * per_worker

        def first_row(n):
            return (n % TOP_K) * t + base + (n // TOP_K) * SC_ROWS

        def start(n, idx_v, rows_v, sem):
            pltpu.sync_copy(slot_hbm.at[pl.ds(first_row(n), SC_ROWS)], idx_v)
            pltpu.async_copy(ys_hbm.at[idx_v], rows_v, sem)

        def finish(n, idx_v, rows_v, sem):
            pltpu.make_async_copy(ys_hbm.at[idx_v], rows_v, sem).wait()
            pltpu.sync_copy(rows_v, out_hbm.at[pl.ds(first_row(n), SC_ROWS)])

        start(0, idx0, rows0, sem0)

        @pl.loop(0, n_blocks, step=2)
        def _(n):
            start(n + 1, idx1, rows1, sem1)
            finish(n, idx0, rows0, sem0)

            @pl.when(n + 2 < n_blocks)
            def _():
                start(n + 2, idx0, rows0, sem0)

            finish(n + 1, idx1, rows1, sem1)

    return gather_rows(y_slots, slot_of)


def _combine_kernel(y_ref, h_ref, g2_ref, gtok_ref, rows_ref, wgs_ref, wus_ref, wds_ref, o_ref):
    h_left, h_right = _unpack_rows(h_ref[...])
    h = jnp.concatenate([h_left, h_right], axis=1).astype(BF16)
    gs = _dot(h, wgs_ref[...])
    us = _dot(h, wus_ref[...])
    shared = _dot(((gs * _sigmoid(gs)) * us).astype(BF16), wds_ref[...])

    gtok = gtok_ref[...]
    acc_left = acc_right = None
    for k in range(TOP_K):
        left, right = _unpack_rows(rows_ref[k])
        gate = gtok[:, k:k + 1]
        acc_left = gate * left if acc_left is None else acc_left + gate * left
        acc_right = gate * right if acc_right is None else acc_right + gate * right
    routed = jnp.concatenate([acc_left, acc_right], axis=1)
    o_ref[...] = y_ref[...] + g2_ref[0] * (routed + shared)


def _combine(y_all, hp_all, gate2, gtok, rows, wgs, wus, wds, *, first_token, tokens, seq):
    d = y_all.shape[1]
    width = hp_all.shape[1]
    tm = MOE_ROW_TM
    tile0 = first_token // tm
    nb = gate2.shape[0]
    tiles_per_batch = seq // tm

    def mod_map(i):
        return ((i // tiles_per_batch) if nb > 1 else 0, 0, 0)

    return pl.pallas_call(
        _combine_kernel,
        out_shape=jax.ShapeDtypeStruct((tokens, d), F32),
        grid=(tokens // tm,),
        in_specs=[pl.BlockSpec((tm, d), lambda i: (tile0 + i, 0)),
                  pl.BlockSpec((tm, width), lambda i: (tile0 + i, 0)),
                  pl.BlockSpec((1, 1, d), mod_map),
                  pl.BlockSpec((tm, LANES), lambda i: (tile0 + i, 0)),
                  pl.BlockSpec((TOP_K, tm, width), lambda i: (0, tile0 + i, 0)),
                  pl.BlockSpec((d, D_SHARED), lambda i: (0, 0)),
                  pl.BlockSpec((d, D_SHARED), lambda i: (0, 0)),
                  pl.BlockSpec((D_SHARED, d), lambda i: (0, 0))],
        out_specs=pl.BlockSpec((tm, d), lambda i: (i, 0)),
        compiler_params=_cparams(("arbitrary",)),
        name="moe_combine",
    )(y_all, hp_all, gate2, gtok, rows, wgs, wus, wds)


def _expert_layout(counts, n_tiles):
    cnt = counts.astype(jnp.int32)
    tiles = (cnt + (MOE_TS - 1)) // MOE_TS
    last_tile = jnp.cumsum(tiles)
    off = (last_tile - tiles) * MOE_TS
    pad_lo = off + cnt
    pad_hi = (off + tiles * MOE_TS).at[N_EXPERTS - 1].set(n_tiles * MOE_TS)
    pad_cnt = pad_hi - pad_lo
    pad_last = jnp.cumsum(pad_cnt)
    shift = pad_lo - (pad_last - pad_cnt)
    j = jnp.arange(N_EXPERTS * MOE_TS, dtype=jnp.int32)
    past = (pad_last[None, :-1] <= j[:, None]).astype(jnp.int32)
    pad_slots = j + shift[0] + jnp.sum(past * (shift[1:] - shift[:-1])[None, :], axis=1)
    tile_ids = jnp.arange(n_tiles, dtype=jnp.int32)
    tile_expert = jnp.minimum(
        jnp.sum((last_tile[None, :] <= tile_ids[:, None]).astype(jnp.int32), axis=1), N_EXPERTS - 1)
    return off, pad_slots, tile_expert, last_tile[-1:].astype(jnp.int32)


def _rope_tables(n_tokens):
    t = jnp.arange(n_tokens)
    row = (t // GRID_W).astype(F32)
    col = (t % GRID_W).astype(F32)
    nf = HEAD_DIM // 4
    freqs = ROPE_THETA ** (-jnp.arange(nf, dtype=F32) / nf)
    ang_r = row[:, None] * freqs
    ang_c = col[:, None] * freqs
    cos = jnp.concatenate([jnp.cos(ang_r)] * 2 + [jnp.cos(ang_c)] * 2, axis=1)
    sin = jnp.concatenate([-jnp.sin(ang_r), jnp.sin(ang_r), -jnp.sin(ang_c), jnp.sin(ang_c)], axis=1)
    reps = LANES // HEAD_DIM
    return jnp.tile(cos, (1, reps)), jnp.tile(sin, (1, reps))


def _head_gains(qn_a, kn_a, qn_b, kn_b):
    ones = jnp.ones((HEAD_DIM,), F32)
    parts = ([qn_a] * N_HEADS_A + [kn_a] * N_KV_A + [ones] * N_KV_A
             + [qn_b] * N_HEADS_B + [kn_b] * N_HEADS_B + [ones] * N_HEADS_B)
    return jnp.concatenate(parts).reshape(1, IN_COLS).astype(F32)


def _same_head_indicator():
    i = np.arange(MXU_DIM)
    return jnp.asarray((i[:, None] // HEAD_DIM) == (i[None, :] // HEAD_DIM), BF16)


def _token_major(cache):
    b, h, s, hd = cache.shape
    return cache.transpose(0, 2, 1, 3).reshape(b, s, h * hd).astype(BF16)


def kernel(x_prompt, x_sample, cache_k_a, cache_v_a, cache_k_b, cache_v_b, c, c_ctx, w_mod, b_mod, norm1, norm2, w_in, qn_a, kn_a, qn_b, kn_b, rpb, on_a, on_b, w_out, w_router, router_bias, w_gate_e, w_up_e, w_down_e, w_gate_s, w_up_s, w_down_s):
    depth = w_mod.shape[0]
    assert depth == 1
    l = 0
    bp, sp, d = x_prompt.shape
    bs, ss, _ = x_sample.shape

    cvec = jnp.concatenate([c_ctx[None, :], c], axis=0)
    rows = -(-cvec.shape[0] // 8) * 8
    cvec = jnp.pad(cvec, ((0, rows - cvec.shape[0]), (0, 0)))
    mod = _adaln(cvec, w_mod[l], b_mod[l])
    mod_p = [m.reshape(1, 1, d) for m in jnp.split(mod[0:1], 6, axis=-1)]
    mod_s = [m.reshape(bs, 1, d) for m in jnp.split(mod[1:1 + bs], 6, axis=-1)]
    mod_all = [m.reshape(1 + bs, 1, d) for m in jnp.split(mod[0:1 + bs], 6, axis=-1)]

    w_in_bf = w_in[l].astype(BF16)
    w_out_bf = w_out[l].astype(BF16)
    gain = _head_gains(qn_a[l], kn_a[l], qn_b[l], kn_b[l])
    seg = _same_head_indicator()
    n1 = norm1[l].reshape(1, d)
    n2 = norm2[l].reshape(1, d)
    ona = on_a[l].reshape(1, WIDTH_A)
    onb = on_b[l].reshape(1, WIDTH_B)
    wr_t = w_router[l].T
    wr_hi = wr_t.astype(BF16)
    wr_lo = (wr_t - wr_hi.astype(F32)).astype(BF16)
    rbias = router_bias[l].reshape(N_EXPERTS, 1).astype(F32)
    wgs = w_gate_s[l].astype(BF16)
    wus = w_up_s[l].astype(BF16)
    wds = w_down_s[l].astype(BF16)
    t_p = bp * sp
    t_s = bs * ss
    t_all = t_p + t_s

    xp = x_prompt.reshape(t_p, d)
    proj_p, st_ka, st_va, st_kb, st_vb = _project(
        xp, mod_p[0], mod_p[1], n1, w_in_bf, gain, seg, None, tm=sp, seq=sp, states=True)
    oa_p, ob_p = _context_attention(proj_p, seq=sp)

    xs = x_sample.reshape(t_s, d)
    proj_s, = _project(xs, mod_s[0], mod_s[1], n1, w_in_bf, gain, seg, _rope_tables(ss),
                       tm=1024, seq=ss, states=False)
    oa_s = _latent_gqa(proj_s, _token_major(cache_k_a[:, l]), _token_major(cache_v_a[:, l]),
                       _query_norm_bound(qn_a[l]), seq=ss, tq=256)
    bias_t = _neighbourhood_bias(rpb[l], ss // GRID_W)
    ob_s = _latent_neighbourhood(proj_s, _token_major(cache_k_b[:, l]), _token_major(cache_v_b[:, l]),
                                 bias_t, _neighbourhood_bounds(qn_b[l], kn_b[l], rpb[l]), seq=ss)

    y1_all, hp_all, gates_t, rank_t, counts = _merge(
        (xp, oa_p, ob_p), (xs, oa_s, ob_s), ona, onb, w_out_bf, mod_all[2], mod_all[3], mod_all[4], n2,
        wr_hi, wr_lo, rbias, tm=512, lat_seq=ss)
    n_tiles = t_all * TOP_K // MOE_TS + N_EXPERTS
    off, pad_slots, tile_expert, tiles_used = _expert_layout(counts[:, 0], n_tiles)
    pos, gtok = _slots(gates_t, rank_t, off.astype(F32).reshape(N_EXPERTS, 1))
    slot_of = pos.reshape(TOP_K * t_all)
    x_slots = _dispatch(hp_all, jnp.concatenate([slot_of, pad_slots]))
    y_slots = _experts(x_slots, tile_expert, tiles_used, w_gate_e[l], w_up_e[l], w_down_e[l])
    rows = _gather_slots(y_slots, slot_of, t_all).reshape(TOP_K, t_all, d // 2)
    y_p = _combine(y1_all, hp_all, mod_p[5], gtok, rows, wgs, wus, wds,
                   first_token=0, tokens=t_p, seq=sp)
    y_s = _combine(y1_all, hp_all, mod_s[5], gtok, rows, wgs, wus, wds,
                   first_token=t_p, tokens=t_s, seq=ss)

    return (y_p.reshape(bp, sp, d), y_s.reshape(bs, ss, d), st_ka, st_va, st_kb, st_vb)
```

```python
import functools

import numpy as np
import jax
import jax.numpy as jnp
from jax import lax
from jax.experimental import pallas as pl
from jax.experimental.pallas import tpu as pltpu
from jax.experimental.pallas import tpu_sc as plsc

F32 = jnp.float32
BF16 = jnp.bfloat16

D_MODEL = 1024
HEAD_DIM = 64
N_HEADS_A = 8
N_KV_A = 2
GROUP_A = N_HEADS_A // N_KV_A
N_HEADS_B = 8
WIDTH_A = N_HEADS_A * HEAD_DIM
WIDTH_B = N_HEADS_B * HEAD_DIM
KV_WIDTH_A = N_KV_A * HEAD_DIM
IN_COLS = WIDTH_A + 2 * KV_WIDTH_A + 3 * WIDTH_B
GRID_W = 64
ROPE_THETA = 10000.0
NA_KH = 8
NA_KW = 16
N_EXPERTS = 64
N_GROUPS = 8
GROUP_SIZE = N_EXPERTS // N_GROUPS
TOPK_GROUPS = 4
TOP_K = 8
D_EXPERT = 256
D_SHARED = 256
ROUTED_SCALE = 2.5
EPS = 1e-6

LANES = 128
MXU_DIM = 256
MASKED = -1e30

COL_QA = 0
COL_KA = WIDTH_A
COL_VA = COL_KA + KV_WIDTH_A
COL_QB = COL_VA + KV_WIDTH_A
COL_KB = COL_QB + WIDTH_B
COL_VB = COL_KB + WIDTH_B

NA_QROWS = 8
NA_KROWS = 2 * NA_KH
NA_TQ = NA_QROWS * GRID_W
NA_TK = NA_KROWS * GRID_W
NA_KBLK = 256

VMEM_LIMIT = 56 * 1024 * 1024


def _cparams(sem):
    return pltpu.CompilerParams(dimension_semantics=sem, vmem_limit_bytes=VMEM_LIMIT)


def _dot(a, b):
    return jnp.dot(a, b, preferred_element_type=F32)


def _dot_nt(a, b):
    return lax.dot_general(a, b, (((1,), (1,)), ((), ())), preferred_element_type=F32)


def _sigmoid(x):
    return 1.0 / (1.0 + jnp.exp(-x))


def _rms(x):
    return x * lax.rsqrt(jnp.mean(x * x, axis=-1, keepdims=True) + EPS)


def _pack_rows(x):
    n = x.shape[1] // 2
    hi = lax.bitcast_convert_type(x[:, :n].astype(BF16).astype(F32), jnp.int32)
    lo = lax.bitcast_convert_type(x[:, n:].astype(BF16).astype(F32), jnp.int32)
    return hi | lax.shift_right_logical(lo, 16)


def _unpack_rows(w):
    left = lax.bitcast_convert_type(w & jnp.int32(-65536), F32)
    right = lax.bitcast_convert_type(lax.shift_left(w, 16), F32)
    return left, right


def _mod_kernel(c_ref, w_ref, b_ref, o_ref):
    c = c_ref[...]
    s = c * _sigmoid(c)
    o_ref[...] = jnp.dot(s, w_ref[...], preferred_element_type=F32,
                         precision=lax.Precision.HIGHEST) + b_ref[...]


def _adaln(cvec, w_mod, b_mod):
    rows, d = cvec.shape
    n = w_mod.shape[1]
    tn = 512
    return pl.pallas_call(
        _mod_kernel,
        out_shape=jax.ShapeDtypeStruct((rows, n), F32),
        grid=(n // tn,),
        in_specs=[pl.BlockSpec((rows, d), lambda j: (0, 0)),
                  pl.BlockSpec((d, tn), lambda j: (0, j)),
                  pl.BlockSpec((1, tn), lambda j: (0, j))],
        out_specs=pl.BlockSpec((rows, tn), lambda j: (0, j)),
        compiler_params=_cparams(("arbitrary",)),
        name="adaln_mod",
    )(cvec, w_mod, b_mod.reshape(1, n))


def _proj_chunks():
    def split(c0, width, step, *flags):
        return [(c0 + i, min(step, width - i)) + flags for i in range(0, width, step)]
    return (split(COL_QA, WIDTH_A, MXU_DIM, True, True, True)
            + split(COL_KA, KV_WIDTH_A, MXU_DIM, True, True, False)
            + split(COL_VA, KV_WIDTH_A, MXU_DIM, False, False, False)
            + split(COL_QB, WIDTH_B, MXU_DIM, True, False, True)
            + split(COL_KB, WIDTH_B, MXU_DIM, True, False, False)
            + split(COL_VB, WIDTH_B, MXU_DIM, False, False, False))


_PROJ_CHUNKS = _proj_chunks()


def _proj_kernel(*refs, rope, states):
    x_ref, sh_ref, sc_ref, n1_ref, w_ref, gain_ref, seg_ref = refs[:7]
    pos = 7
    if rope:
        cos_ref, sin_ref = refs[pos:pos + 2]
        pos += 2
    out_ref = refs[pos]
    pos += 1
    if states:
        ka_ref, va_ref, kb_ref, vb_ref = refs[pos:pos + 4]
        state_of = {COL_KA: ka_ref, COL_VA: va_ref, COL_KB: kb_ref, COL_VB: vb_ref}

    x = x_ref[...]
    h = _rms(x) * n1_ref[...]
    h = h * (1.0 + sc_ref[0]) + sh_ref[0]
    p = _dot(h.astype(BF16), w_ref[...])

    for c0, w, normed, roped, is_query in _PROJ_CHUNKS:
        pc = p[:, c0:c0 + w]
        if normed:
            seg = seg_ref[0:w, 0:w]
            sq = pc * pc
            hi = sq.astype(BF16)
            lo = (sq - hi.astype(F32)).astype(BF16)
            ss = _dot(hi, seg) + _dot(lo, seg)
            pc = pc * lax.rsqrt(ss * (1.0 / HEAD_DIM) + EPS) * gain_ref[:, c0:c0 + w]
        if states:
            for start, ref in state_of.items():
                if start <= c0 < start + ref.shape[2] * HEAD_DIM:
                    base = (c0 - start) // HEAD_DIM
                    for hh in range(w // HEAD_DIM):
                        ref[0, 0, base + hh] = pc[:, hh * HEAD_DIM:(hh + 1) * HEAD_DIM]
        if rope and roped:
            reps = w // LANES
            cos = jnp.concatenate([cos_ref[...]] * reps, axis=1) if reps > 1 else cos_ref[...]
            sin = jnp.concatenate([sin_ref[...]] * reps, axis=1) if reps > 1 else sin_ref[...]
            lane = lax.broadcasted_iota(jnp.int32, pc.shape, 1)
            first_half = (lane % (HEAD_DIM // 2)) < (HEAD_DIM // 4)
            partner = jnp.where(first_half,
                                pltpu.roll(pc, w - HEAD_DIM // 4, 1),
                                pltpu.roll(pc, HEAD_DIM // 4, 1))
            pc = pc * cos + partner * sin
        if is_query:
            pc = pc * (HEAD_DIM ** -0.5)
        out_ref[:, c0:c0 + w] = pc.astype(BF16)


def _project(x2d, shift, scale, norm1, w_in_bf, gain, seg, rope_tabs, *, tm, seq, states):
    t, d = x2d.shape
    nb = shift.shape[0]
    tiles_per_batch = seq // tm
    rope = rope_tabs is not None

    def mod_map(i):
        return ((i // tiles_per_batch) if nb > 1 else 0, 0, 0)

    in_specs = [pl.BlockSpec((tm, d), lambda i: (i, 0)),
                pl.BlockSpec((1, 1, d), mod_map),
                pl.BlockSpec((1, 1, d), mod_map),
                pl.BlockSpec((1, d), lambda i: (0, 0)),
                pl.BlockSpec((d, IN_COLS), lambda i: (0, 0)),
                pl.BlockSpec((1, IN_COLS), lambda i: (0, 0)),
                pl.BlockSpec((MXU_DIM, MXU_DIM), lambda i: (0, 0))]
    args = [x2d, shift, scale, norm1, w_in_bf, gain, seg]
    if rope:
        in_specs += [pl.BlockSpec((tm, LANES), lambda i: (i % tiles_per_batch, 0))] * 2
        args += list(rope_tabs)
    out_shape = [jax.ShapeDtypeStruct((t, IN_COLS), BF16)]
    out_specs = [pl.BlockSpec((tm, IN_COLS), lambda i: (i, 0))]
    if states:
        assert tm == seq
        b = t // seq
        for nh in (N_KV_A, N_KV_A, N_HEADS_B, N_HEADS_B):
            out_shape.append(jax.ShapeDtypeStruct((b, 1, nh, seq, HEAD_DIM), F32))
            out_specs.append(pl.BlockSpec((1, 1, nh, seq, HEAD_DIM), lambda i: (i, 0, 0, 0, 0)))
    return pl.pallas_call(
        functools.partial(_proj_kernel, rope=rope, states=states),
        out_shape=out_shape,
        grid=(t // tm,),
        in_specs=in_specs,
        out_specs=out_specs,
        compiler_params=_cparams(("arbitrary",)),
        name="proj_states" if states else "proj_rope",
    )(*args)


def _lane_half(shape):
    return lax.broadcasted_iota(jnp.int32, shape, 1) // HEAD_DIM


def _keep_half(x, half):
    return jnp.where(_lane_half(x.shape) == half, x, jnp.zeros_like(x))


def _transpose_bf16(x):
    return x.astype(F32).T.astype(BF16)


def _scores(q, keys, biases):
    scores = []
    for k, b in zip(keys, biases):
        s = _dot_nt(k, q)
        if b is not None:
            s = s + b
        scores.append(s)
    return scores


def _softmax_av(scores, values_t):
    m = functools.reduce(jnp.maximum, [jnp.max(s, axis=0, keepdims=True) for s in scores])
    denom = None
    out = None
    for s, vt in zip(scores, values_t):
        p = jnp.exp(s - m)
        ps = jnp.sum(p, axis=0, keepdims=True)
        po = _dot(vt, p.astype(BF16))
        denom = ps if denom is None else denom + ps
        out = po if out is None else out + po
    return out / denom


def _swap_halves(q_bf16):
    return pltpu.roll(q_bf16.astype(F32), HEAD_DIM, 1).astype(BF16)


def _ctx_attn_kernel(p_ref, oa_ref, ob_ref):
    ka = p_ref[:, COL_KA:COL_KA + LANES]
    va_t = [_transpose_bf16(p_ref[:, COL_VA:COL_VA + LANES])]
    seq = p_ref.shape[0]
    outs = []
    for g in range(N_KV_A):
        qs = []
        for j in range(GROUP_A):
            h = g * GROUP_A + j
            q = p_ref[:, COL_QA + (h // 2) * LANES:COL_QA + (h // 2 + 1) * LANES]
            qs.append(q if h % 2 == g else _swap_halves(q))
        o = _softmax_av(_scores(jnp.concatenate(qs, axis=0), [_keep_half(ka, g)], [None]), va_t)
        o = o[g * HEAD_DIM:(g + 1) * HEAD_DIM]
        outs += [o[:, j * seq:(j + 1) * seq] for j in range(GROUP_A)]
    oa_ref[...] = jnp.concatenate(outs, axis=0).T

    outs = []
    for i in range(N_HEADS_B // 2):
        q = p_ref[:, COL_QB + i * LANES:COL_QB + (i + 1) * LANES]
        k = p_ref[:, COL_KB + i * LANES:COL_KB + (i + 1) * LANES]
        vt = [_transpose_bf16(p_ref[:, COL_VB + i * LANES:COL_VB + (i + 1) * LANES])]
        scores = [_scores(q, [_keep_half(k, half)], [None]) for half in range(2)]
        for half in range(2):
            outs.append(_softmax_av(scores[half], vt)[half * HEAD_DIM:(half + 1) * HEAD_DIM])
    ob_ref[...] = jnp.concatenate(outs, axis=0).T


def _context_attention(proj, *, seq):
    t = proj.shape[0]
    return pl.pallas_call(
        _ctx_attn_kernel,
        out_shape=[jax.ShapeDtypeStruct((t, WIDTH_A), F32), jax.ShapeDtypeStruct((t, WIDTH_B), F32)],
        grid=(t // seq,),
        in_specs=[pl.BlockSpec((seq, IN_COLS), lambda i: (i, 0))],
        out_specs=[pl.BlockSpec((seq, WIDTH_A), lambda i: (i, 0)),
                   pl.BlockSpec((seq, WIDTH_B), lambda i: (i, 0))],
        compiler_params=_cparams(("arbitrary",)),
        name="context_attention",
    )(proj)


ATTN_SAFE_SHIFT = 40.0
ONES_ROWS = 16


def _round_up_bf16(x):
    return (x * (1.0 + 2.0 ** -6)).astype(BF16).astype(F32)


def _query_norm_bound(gain):
    return jnp.max(jnp.abs(gain)).reshape(1, 1).astype(F32)


def _ones_lane(g):
    return (1 - g) * HEAD_DIM


def _gqa_latent_kernel(q_ref, k_ref, v_ref, ck_ref, cv_ref, qmax_ref, o_ref,
                       kg_ref, ckg_ref, vt_ref, cvt_ref, shift_ref):
    lane_k = lax.broadcasted_iota(jnp.int32, (1, LANES), 1)

    @pl.when(pl.program_id(1) == 0)
    def _():
        k = k_ref[...]
        ck = ck_ref[0]
        vt = v_ref[...].astype(F32).T
        cvt = cv_ref[0].astype(F32).T
        for g in range(N_KV_A):
            kf = _keep_half(k, g).astype(F32)
            ckf = _keep_half(ck, g).astype(F32)
            ksq = jnp.maximum(jnp.max(jnp.sum(kf * kf, axis=1, keepdims=True), axis=0, keepdims=True),
                              jnp.max(jnp.sum(ckf * ckf, axis=1, keepdims=True), axis=0, keepdims=True))
            shift_ref[g] = jnp.broadcast_to(_round_up_bf16(qmax_ref[...] * jnp.sqrt(ksq)), shift_ref.shape[1:])
            kg_ref[g] = jnp.where(lane_k == _ones_lane(g), 1.0, kf).astype(BF16)
            ckg_ref[g] = jnp.where(lane_k == _ones_lane(g), 1.0, ckf).astype(BF16)
            rows = slice(g * HEAD_DIM, (g + 1) * HEAD_DIM)
            vt_ref[g] = jnp.concatenate([vt[rows], jnp.ones((ONES_ROWS, vt.shape[1]), F32)], axis=0).astype(BF16)
            cvt_ref[g] = jnp.concatenate([cvt[rows], jnp.ones((ONES_ROWS, cvt.shape[1]), F32)], axis=0).astype(BF16)

    tq = q_ref.shape[0]
    lane_q = lax.broadcasted_iota(jnp.int32, (GROUP_A * tq, LANES), 1)
    queries, shifts = [], []
    for g in range(N_KV_A):
        qs = []
        for j in range(GROUP_A):
            h = g * GROUP_A + j
            q = q_ref[:, (h // 2) * LANES:(h // 2 + 1) * LANES].astype(F32)
            qs.append(q if h % 2 == g else pltpu.roll(q, HEAD_DIM, 1))
        queries.append(jnp.where(lane_q // HEAD_DIM == g, jnp.concatenate(qs, axis=0), 0.0))
        shifts.append(shift_ref[g][0:1, 0:1])
    safe = jnp.max(jnp.maximum(shift_ref[0], shift_ref[1])) <= ATTN_SAFE_SHIFT

    def attend(g, p_lat, p_ctx):
        half = p_lat.shape[1] // 2
        o = jnp.concatenate([_dot(vt_ref[g], p_lat[:, :half]) + _dot(cvt_ref[g], p_ctx[:, :half]),
                             _dot(vt_ref[g], p_lat[:, half:]) + _dot(cvt_ref[g], p_ctx[:, half:])],
                            axis=1)
        o = o[:HEAD_DIM] / o[HEAD_DIM:HEAD_DIM + 1]
        heads = jnp.concatenate([o[:, j * tq:(j + 1) * tq] for j in range(GROUP_A)], axis=0)
        o_ref[:, g * GROUP_A * HEAD_DIM:(g + 1) * GROUP_A * HEAD_DIM] = heads.T

    def with_bound():
        for g in range(N_KV_A):
            qa = jnp.where(lane_q == _ones_lane(g), -shifts[g], queries[g]).astype(BF16)
            attend(g, jnp.exp(_dot_nt(kg_ref[g], qa)).astype(BF16), jnp.exp(_dot_nt(ckg_ref[g], qa)).astype(BF16))

    def with_row_max():
        for g in range(N_KV_A):
            qa = queries[g].astype(BF16)
            s_lat = _dot_nt(kg_ref[g], qa)
            s_ctx = _dot_nt(ckg_ref[g], qa)
            m = jnp.maximum(jnp.max(s_lat, axis=0, keepdims=True), jnp.max(s_ctx, axis=0, keepdims=True))
            attend(g, jnp.exp(s_lat - m).astype(BF16), jnp.exp(s_ctx - m).astype(BF16))

    pl.when(safe)(with_bound)
    pl.when(jnp.logical_not(safe))(with_row_max)


def _latent_gqa(proj, ctx_k, ctx_v, qmax, *, seq, tq):
    t = proj.shape[0]
    b = t // seq
    nq = seq // tq
    past = ctx_k.shape[1]
    return pl.pallas_call(
        _gqa_latent_kernel,
        out_shape=jax.ShapeDtypeStruct((t, WIDTH_A), F32),
        grid=(b, nq),
        in_specs=[pl.BlockSpec((tq, WIDTH_A), lambda bi, qi: (bi * nq + qi, 0)),
                  pl.BlockSpec((seq, LANES), lambda bi, qi: (bi, COL_KA // LANES)),
                  pl.BlockSpec((seq, LANES), lambda bi, qi: (bi, COL_VA // LANES)),
                  pl.BlockSpec((1, past, LANES), lambda bi, qi: (bi, 0, 0)),
                  pl.BlockSpec((1, past, LANES), lambda bi, qi: (bi, 0, 0)),
                  pl.BlockSpec((1, 1), lambda bi, qi: (0, 0))],
        out_specs=pl.BlockSpec((tq, WIDTH_A), lambda bi, qi: (bi * nq + qi, 0)),
        scratch_shapes=[pltpu.VMEM((N_KV_A, seq, LANES), BF16),
                        pltpu.VMEM((N_KV_A, past, LANES), BF16),
                        pltpu.VMEM((N_KV_A, HEAD_DIM + ONES_ROWS, seq), BF16),
                        pltpu.VMEM((N_KV_A, HEAD_DIM + ONES_ROWS, past), BF16),
                        pltpu.VMEM((N_KV_A, 8, LANES), F32)],
        compiler_params=_cparams(("arbitrary", "arbitrary")),
        name="latent_gqa",
    )(proj, proj, proj, ctx_k, ctx_v, qmax)


def _na_kernel(q_ref, k_ref, v_ref, ck_ref, cv_ref, bias_ref, bound_ref, o_ref, keys_ref, vt_ref, shift_ref,
               *, rows):
    i = pl.program_id(2)
    n_kblk = k_ref.shape[0] // NA_KBLK
    lane_k = lax.broadcasted_iota(jnp.int32, (1, LANES), 1)
    one = jnp.ones((), BF16)

    @pl.when(i == 0)
    def _():
        k = k_ref[...]
        ck = ck_ref[0]
        for half in range(2):
            kh = jnp.where(lane_k == _ones_lane(half), one, _keep_half(k, half))
            keys_ref[half, 0:n_kblk] = kh.reshape(n_kblk, NA_KBLK, LANES)
            ckh = _keep_half(ck, half)
            keys_ref[half, n_kblk] = jnp.where(lane_k == _ones_lane(half), one, ckh)
            ckf = ckh.astype(F32)
            ctx_norm = jnp.sqrt(jnp.max(jnp.sum(ckf * ckf, axis=1, keepdims=True), axis=0, keepdims=True))
            consts = bound_ref[0, half:half + 1, :]
            kmax = jnp.maximum(ctx_norm, consts[:, 2:3])
            shift_ref[half] = jnp.broadcast_to(_round_up_bf16(consts[:, 0:1] * kmax + consts[:, 1:2]),
                                               shift_ref.shape[1:])
        ones_rows = jnp.ones((ONES_ROWS, NA_KBLK), F32)
        vt = v_ref[...].astype(F32).T
        for j in range(n_kblk):
            vt_ref[j] = jnp.concatenate([vt[:, j * NA_KBLK:(j + 1) * NA_KBLK], ones_rows], axis=0).astype(BF16)
        vt_ref[n_kblk] = jnp.concatenate([cv_ref[0].astype(F32).T, ones_rows], axis=0).astype(BF16)

    q = q_ref[...]
    lane_q = lax.broadcasted_iota(jnp.int32, q.shape, 1)
    first = _na_first_key_block(i, rows)
    n_qblk = rows // NA_QROWS
    variant = jnp.where(i == 0, 0, jnp.where(i == n_qblk - 1, 2, 1))
    blocks = [first + j for j in range(NA_TK // NA_KBLK)] + [n_kblk]
    values_t = [vt_ref[blk] for blk in blocks]
    heads = []
    for half in range(2):
        keys = [keys_ref[half, blk] for blk in blocks]
        biases = [bias_ref[variant, half, j * NA_KBLK:(j + 1) * NA_KBLK, :] for j in range(NA_TK // NA_KBLK)] + [None]
        heads.append((shift_ref[half][0:1, 0:1], keys, biases))
    safe = jnp.max(jnp.maximum(shift_ref[0], shift_ref[1])) <= ATTN_SAFE_SHIFT

    def attend(probabilities):
        outs = []
        for half, ps in enumerate(probabilities):
            o = functools.reduce(lambda a, b: a + b, [_dot(vt, p) for vt, p in zip(values_t, ps)])
            outs.append(o[half * HEAD_DIM:(half + 1) * HEAD_DIM] / o[2 * HEAD_DIM:2 * HEAD_DIM + 1])
        o_ref[...] = jnp.concatenate(outs, axis=0).T

    def with_bound():
        probabilities = []
        for half, (shift, keys, biases) in enumerate(heads):
            qa = jnp.where(lane_q == _ones_lane(half), (-shift).astype(BF16), _keep_half(q, half))
            probabilities.append([jnp.exp(s).astype(BF16) for s in _scores(qa, keys, biases)])
        attend(probabilities)

    def with_row_max():
        all_scores = [_scores(_keep_half(q, half), keys, biases) for half, (_, keys, biases) in enumerate(heads)]
        probabilities = []
        for scores in all_scores:
            m = functools.reduce(jnp.maximum, [jnp.max(s, axis=0, keepdims=True) for s in scores])
            probabilities.append([jnp.exp(s - m).astype(BF16) for s in scores])
        attend(probabilities)

    pl.when(safe)(with_bound)
    pl.when(jnp.logical_not(safe))(with_row_max)


def _na_first_key_block(i, rows):
    per_qblock = NA_QROWS * GRID_W // NA_KBLK
    lead = (NA_KH // 2) * GRID_W // NA_KBLK
    return jnp.clip(per_qblock * i - lead, 0, (rows - NA_KROWS) * GRID_W // NA_KBLK)


def _latent_neighbourhood(proj, ctx_k, ctx_v, bias_t, bounds, *, seq):
    t = proj.shape[0]
    b = t // seq
    rows = seq // GRID_W
    nblk = rows // NA_QROWS
    n_kblk = seq // NA_KBLK
    past = ctx_k.shape[1]
    assert past == NA_KBLK
    grid = (N_HEADS_B // 2, b, nblk)
    in_specs = [pl.BlockSpec((NA_TQ, LANES), lambda hp, bi, i: (bi * nblk + i, COL_QB // LANES + hp)),
                pl.BlockSpec((seq, LANES), lambda hp, bi, i: (bi, COL_KB // LANES + hp)),
                pl.BlockSpec((seq, LANES), lambda hp, bi, i: (bi, COL_VB // LANES + hp)),
                pl.BlockSpec((1, past, LANES), lambda hp, bi, i: (bi, 0, hp)),
                pl.BlockSpec((1, past, LANES), lambda hp, bi, i: (bi, 0, hp)),
                pl.BlockSpec((3, 2, NA_TK, NA_TQ), lambda hp, bi, i: (0, hp, 0, 0)),
                pl.BlockSpec((1, 2, LANES), lambda hp, bi, i: (hp, 0, 0))]
    return pl.pallas_call(
        functools.partial(_na_kernel, rows=rows),
        out_shape=jax.ShapeDtypeStruct((t, WIDTH_B), F32),
        grid=grid,
        in_specs=in_specs,
        out_specs=pl.BlockSpec((NA_TQ, LANES), lambda hp, bi, i: (bi * nblk + i, hp)),
        scratch_shapes=[pltpu.VMEM((2, n_kblk + 1, NA_KBLK, LANES), BF16),
                        pltpu.VMEM((n_kblk + 1, 2 * HEAD_DIM + ONES_ROWS, NA_KBLK), BF16),
                        pltpu.VMEM((2, 8, LANES), F32)],
        compiler_params=_cparams(("arbitrary", "arbitrary", "arbitrary")),
        name="latent_neighbourhood",
    )(proj, proj, proj, ctx_k, ctx_v, bias_t, bounds)


def _neighbourhood_bounds(qn_b, kn_b, rpb):
    n_heads = rpb.shape[0]
    qmax = jnp.broadcast_to(_query_norm_bound(qn_b), (n_heads, 1))
    kmax = jnp.broadcast_to(_query_norm_bound(kn_b) * (HEAD_DIM ** 0.5), (n_heads, 1))
    bmax = jnp.maximum(jnp.max(rpb.reshape(n_heads, -1), axis=1, keepdims=True), 0.0).astype(F32)
    table = jnp.concatenate([qmax, bmax, kmax, jnp.zeros((n_heads, LANES - 3), F32)], axis=1)
    return table.reshape(n_heads // 2, 2, LANES)


def _neighbourhood_bias(rpb, rows):
    nblk = rows // NA_QROWS
    n_dr = 2 * NA_KH - 1
    n_dc = 2 * NA_KW - 1
    kc = np.arange(GRID_W)[:, None]
    qc = np.arange(GRID_W)[None, :]
    ws = np.clip(qc - NA_KW // 2, 0, GRID_W - NA_KW)
    col_ok = (kc >= ws) & (kc < ws + NA_KW)
    dc = np.clip(kc - qc + NA_KW - 1, 0, n_dc - 1)
    dc_onehot = (dc[None] == np.arange(n_dc)[:, None, None]).astype(np.float32)
    tiles = jnp.einsum('hab,bkq->hakq', rpb.astype(F32), jnp.asarray(dc_onehot),
                       precision=lax.Precision.HIGHEST)
    tiles = jnp.where(jnp.asarray(col_ok)[None, None], tiles, MASKED)
    masked_tile = jnp.full((rpb.shape[0], 1, GRID_W, GRID_W), MASKED, F32)
    tiles = jnp.concatenate([tiles, masked_tile], axis=1)
    tile_of = np.zeros((3, NA_KROWS, NA_QROWS), np.int32)
    for v, i in enumerate((0, 1, nblk - 1)):
        r0 = i * NA_QROWS
        ks = int(np.clip(r0 - NA_KH // 2, 0, rows - NA_KROWS))
        for kl in range(NA_KROWS):
            for ql in range(NA_QROWS):
                kr, qr = ks + kl, r0 + ql
                rs = int(np.clip(qr - NA_KH // 2, 0, rows - NA_KH))
                ok = rs <= kr < rs + NA_KH
                tile_of[v, kl, ql] = (kr - qr + NA_KH - 1) if ok else n_dr
    n_heads, n_tiles = tiles.shape[:2]
    return pl.pallas_call(
        _bias_table_kernel,
        out_shape=jax.ShapeDtypeStruct((3, n_heads, NA_TK, NA_TQ), F32),
        grid_spec=pltpu.PrefetchScalarGridSpec(
            num_scalar_prefetch=1, grid=(3, n_heads),
            in_specs=[pl.BlockSpec((1, n_tiles, GRID_W, GRID_W), lambda v, h, tile_of_ref: (h, 0, 0, 0))],
            out_specs=pl.BlockSpec((1, 1, NA_TK, NA_TQ), lambda v, h, tile_of_ref: (v, h, 0, 0))),
        compiler_params=_cparams(("arbitrary", "arbitrary")),
        name="neighbourhood_bias_table",
    )(jnp.asarray(tile_of.reshape(-1)), tiles)


def _bias_table_kernel(tile_of_ref, tiles_ref, o_ref):
    v = pl.program_id(0)
    for kl in range(NA_KROWS):
        row = [tiles_ref[0, tile_of_ref[(v * NA_KROWS + kl) * NA_QROWS + ql]] for ql in range(NA_QROWS)]
        o_ref[0, 0, kl * GRID_W:(kl + 1) * GRID_W, :] = jnp.concatenate(row, axis=1)


def _merge_kernel(xp_ref, oap_ref, obp_ref, xs_ref, oas_ref, obs_ref, ona_ref, onb_ref, wo_ref,
                  g1_ref, sh2_ref, sc2_ref, n2_ref, wrh_ref, wrl_ref, rb_ref, tri_ref,
                  y_ref, hp_ref, gates_ref, rank_ref, count_ref, *, ctx_tiles):
    i = pl.program_id(0)

    @pl.when(i == 0)
    def _():
        count_ref[...] = jnp.zeros_like(count_ref)

    def one_stream(x_ref, oa_ref, ob_ref):
        na = (_rms(oa_ref[...]) * ona_ref[...]).astype(BF16)
        nb = (_rms(ob_ref[...]) * onb_ref[...]).astype(BF16)
        mix = _dot(na, wo_ref[0:WIDTH_A, :]) + _dot(nb, wo_ref[WIDTH_A:WIDTH_A + WIDTH_B, :])
        y = x_ref[...] + g1_ref[0] * mix
        y_ref[...] = y
        h = _rms(y) * n2_ref[...]
        h = h * (1.0 + sc2_ref[0]) + sh2_ref[0]
        hp_ref[...] = _pack_rows(h)
        gates, chosen = _router_gates(h, wrh_ref[...], wrl_ref[...], rb_ref[...])
        gates_ref[...] = gates
        before = _dot(chosen.astype(BF16), tri_ref[...])
        seen = count_ref[...]
        rank_ref[...] = jnp.where(chosen > 0.0, before + seen[:, 0:1], -1.0)
        count_ref[...] = seen + jnp.sum(chosen, axis=1, keepdims=True)

    pl.when(i < ctx_tiles)(lambda: one_stream(xp_ref, oap_ref, obp_ref))
    pl.when(i >= ctx_tiles)(lambda: one_stream(xs_ref, oas_ref, obs_ref))


def _merge(ctx, lat, on_a, on_b, w_out_bf, gate1, shift2, scale2, norm2, wr_hi, wr_lo, rbias, *, tm, lat_seq):
    t_c, d = ctx[0].shape
    t_l = lat[0].shape[0]
    t = t_c + t_l
    tri = jnp.asarray(np.triu(np.ones((tm, tm), np.float32), k=1), BF16)
    ctx_tiles = t_c // tm
    lat_tiles_per_batch = lat_seq // tm

    def ctx_map(i):
        return (jnp.minimum(i, ctx_tiles - 1), 0)

    def lat_map(i):
        return (jnp.maximum(i - ctx_tiles, 0), 0)

    def mod_map(i):
        return (jnp.where(i < ctx_tiles, 0, 1 + (i - ctx_tiles) // lat_tiles_per_batch), 0, 0)

    def stream_specs(index_map):
        return [pl.BlockSpec((tm, d), index_map),
                pl.BlockSpec((tm, WIDTH_A), index_map),
                pl.BlockSpec((tm, WIDTH_B), index_map)]

    return pl.pallas_call(
        functools.partial(_merge_kernel, ctx_tiles=ctx_tiles),
        out_shape=[jax.ShapeDtypeStruct((t, d), F32),
                   jax.ShapeDtypeStruct((t, d // 2), jnp.int32),
                   jax.ShapeDtypeStruct((N_EXPERTS, t), F32),
                   jax.ShapeDtypeStruct((N_EXPERTS, t), F32),
                   jax.ShapeDtypeStruct((N_EXPERTS, LANES), F32)],
        grid=(t // tm,),
        in_specs=stream_specs(ctx_map) + stream_specs(lat_map) + [
            pl.BlockSpec((1, WIDTH_A), lambda i: (0, 0)),
            pl.BlockSpec((1, WIDTH_B), lambda i: (0, 0)),
            pl.BlockSpec((WIDTH_A + WIDTH_B, d), lambda i: (0, 0)),
            pl.BlockSpec((1, 1, d), mod_map),
            pl.BlockSpec((1, 1, d), mod_map),
            pl.BlockSpec((1, 1, d), mod_map),
            pl.BlockSpec((1, d), lambda i: (0, 0)),
            pl.BlockSpec((N_EXPERTS, d), lambda i: (0, 0)),
            pl.BlockSpec((N_EXPERTS, d), lambda i: (0, 0)),
            pl.BlockSpec((N_EXPERTS, 1), lambda i: (0, 0)),
            pl.BlockSpec((tm, tm), lambda i: (0, 0))],
        out_specs=[pl.BlockSpec((tm, d), lambda i: (i, 0)),
                   pl.BlockSpec((tm, d // 2), lambda i: (i, 0)),
                   pl.BlockSpec((N_EXPERTS, tm), lambda i: (0, i)),
                   pl.BlockSpec((N_EXPERTS, tm), lambda i: (0, i)),
                   pl.BlockSpec((N_EXPERTS, LANES), lambda i: (0, 0))],
        compiler_params=_cparams(("arbitrary",)),
        name="merge_route",
    )(*ctx, *lat, on_a, on_b, w_out_bf, gate1, shift2, scale2, norm2, wr_hi, wr_lo, rbias, tri)


def _first_index_of_max(x, iota):
    mx = jnp.max(x, axis=0, keepdims=True)
    idx = jnp.min(jnp.where(x == mx, iota, float(x.shape[0])), axis=0, keepdims=True)
    return mx, iota == idx


def _router_gates(h, wr_hi, wr_lo, rbias):
    h_hi = h.astype(BF16)
    h_lo = (h - h_hi.astype(F32)).astype(BF16)
    logits = _dot_nt(wr_hi, h_hi) + (_dot_nt(wr_lo, h_hi) + _dot_nt(wr_hi, h_lo))
    scores = _sigmoid(logits)
    sel = scores + rbias
    tm = sel.shape[1]
    iota_g = lax.broadcasted_iota(jnp.int32, (GROUP_SIZE, tm), 0).astype(F32)
    group_scores = []
    for g in range(N_GROUPS):
        grp = sel[g * GROUP_SIZE:(g + 1) * GROUP_SIZE]
        m1, first = _first_index_of_max(grp, iota_g)
        m2 = jnp.max(jnp.where(first, -jnp.inf, grp), axis=0, keepdims=True)
        group_scores.append(m1 + m2)
    gs = jnp.concatenate(group_scores, axis=0)
    iota_n = lax.broadcasted_iota(jnp.int32, (N_GROUPS, tm), 0).astype(F32)
    group_on = jnp.zeros((N_GROUPS, tm), F32)
    for _ in range(TOPK_GROUPS):
        _, pick = _first_index_of_max(gs, iota_n)
        group_on = jnp.where(pick, 1.0, group_on)
        gs = jnp.where(pick, -jnp.inf, gs)
    expert_on = jnp.concatenate(
        [jnp.broadcast_to(group_on[g:g + 1], (GROUP_SIZE, tm)) for g in range(N_GROUPS)], axis=0)
    cand = jnp.where(expert_on > 0.0, sel, -jnp.inf)
    iota_e = lax.broadcasted_iota(jnp.int32, (N_EXPERTS, tm), 0).astype(F32)
    w = jnp.zeros((N_EXPERTS, tm), F32)
    chosen = jnp.zeros((N_EXPERTS, tm), F32)
    for _ in range(TOP_K):
        _, pick = _first_index_of_max(cand, iota_e)
        w = jnp.where(pick, scores, w)
        chosen = jnp.where(pick, 1.0, chosen)
        cand = jnp.where(pick, -jnp.inf, cand)
    return w / jnp.sum(w, axis=0, keepdims=True) * ROUTED_SCALE, chosen


MOE_TS = 1024
MOE_ROUTE_TM = 1024
MOE_ROW_TM = 512


def _slots_kernel(gates_ref, rank_ref, off_ref, pos_ref, gtok_ref):
    gates = gates_ref[...]
    rank = rank_ref[...]
    tm = gates.shape[1]
    slot = off_ref[...] + rank
    left = jnp.where(rank >= 0.0, 1.0, 0.0)
    iota_e = lax.broadcasted_iota(jnp.int32, (N_EXPERTS, tm), 0).astype(F32)
    pos_rows, gate_rows = [], []
    for _ in range(TOP_K):
        _, pick = _first_index_of_max(left, iota_e)
        pos_rows.append(jnp.sum(jnp.where(pick, slot, 0.0), axis=0, keepdims=True))
        gate_rows.append(jnp.sum(jnp.where(pick, gates, 0.0), axis=0, keepdims=True))
        left = jnp.where(pick, 0.0, left)
    pos_ref[...] = jnp.concatenate(pos_rows, axis=0).astype(jnp.int32)
    pad = jnp.zeros((LANES - TOP_K, tm), F32)
    gtok_ref[...] = jnp.concatenate(gate_rows + [pad], axis=0).T


def _slots(gates_t, rank_t, off):
    t = gates_t.shape[1]
    tm = MOE_ROUTE_TM
    return pl.pallas_call(
        _slots_kernel,
        out_shape=[jax.ShapeDtypeStruct((TOP_K, t), jnp.int32), jax.ShapeDtypeStruct((t, LANES), F32)],
        grid=(t // tm,),
        in_specs=[pl.BlockSpec((N_EXPERTS, tm), lambda i: (0, i)),
                  pl.BlockSpec((N_EXPERTS, tm), lambda i: (0, i)),
                  pl.BlockSpec((N_EXPERTS, 1), lambda i: (0, 0))],
        out_specs=[pl.BlockSpec((TOP_K, tm), lambda i: (0, i)),
                   pl.BlockSpec((tm, LANES), lambda i: (i, 0))],
        compiler_params=_cparams(("arbitrary",)),
        name="moe_slots",
    )(gates_t, rank_t, off)


SC_CORES = 2
SC_SUBCORES = 16
SC_ROWS = 64


def _dispatch(hp_all, slot_of):
    t, width = hp_all.shape
    n_slots = slot_of.shape[0]
    n_pad = n_slots - TOP_K * t
    workers = SC_CORES * SC_SUBCORES
    per_worker = t // workers
    pad_per_worker = n_pad // workers
    assert per_worker * workers == t and per_worker % SC_ROWS == 0
    assert pad_per_worker * workers == n_pad and pad_per_worker % SC_ROWS == 0
    mesh = plsc.VectorSubcoreMesh(core_axis_name="core", subcore_axis_name="subcore")

    assert pad_per_worker % (TOP_K * SC_ROWS) == 0 and (per_worker // SC_ROWS) % 2 == 0

    @functools.partial(
        pl.kernel, mesh=mesh,
        out_type=jax.ShapeDtypeStruct((n_slots, width), jnp.int32),
        scratch_types=[pltpu.VMEM((SC_ROWS,), jnp.int32) for _ in range(TOP_K)]
        + [pltpu.VMEM((SC_ROWS, width), jnp.int32), pltpu.VMEM((SC_ROWS, width), jnp.int32),
           pltpu.SemaphoreType.DMA, pltpu.SemaphoreType.DMA, pltpu.SemaphoreType.DMA],
    )
    def scatter_rows(h_hbm, slot_hbm, out_hbm, *scratch):
        idx = scratch[:TOP_K]
        rows_a, rows_b, sem_idx, sem_rows, sem_load = scratch[TOP_K:]
        worker = lax.axis_index("subcore") * SC_CORES + lax.axis_index("core")

        def scatter_group(rows_v, index_starts):
            for k, start in enumerate(index_starts):
                pltpu.async_copy(slot_hbm.at[pl.ds(start, SC_ROWS)], idx[k], sem_idx)
            for k, start in enumerate(index_starts):
                pltpu.make_async_copy(slot_hbm.at[pl.ds(start, SC_ROWS)], idx[k], sem_idx).wait()
            for k in range(TOP_K):
                pltpu.async_copy(rows_v, out_hbm.at[idx[k]], sem_rows)
            for k in range(TOP_K):
                pltpu.make_async_copy(rows_v, out_hbm.at[idx[k]], sem_rows).wait()

        pltpu.sync_copy(h_hbm.at[pl.ds(0, SC_ROWS)], rows_a)
        pad_base = TOP_K * t + worker * pad_per_worker

        @pl.loop(0, pad_per_worker // (TOP_K * SC_ROWS))
        def _(j):
            first = pad_base + j * (TOP_K * SC_ROWS)
            scatter_group(rows_a, [first + k * SC_ROWS for k in range(TOP_K)])

        base = worker * per_worker
        n_chunks = per_worker // SC_ROWS

        def load(j, rows_v):
            return pltpu.make_async_copy(h_hbm.at[pl.ds(base + j * SC_ROWS, SC_ROWS)], rows_v, sem_load)

        def scatter_chunk(j, rows_v):
            scatter_group(rows_v, [k * t + base + j * SC_ROWS for k in range(TOP_K)])

        load(0, rows_a).start()
        load(0, rows_a).wait()

        @pl.loop(0, n_chunks, step=2)
        def _(j):
            load(j + 1, rows_b).start()
            scatter_chunk(j, rows_a)
            load(j + 1, rows_b).wait()

            @pl.when(j + 2 < n_chunks)
            def _():
                load(j + 2, rows_a).start()

            scatter_chunk(j + 1, rows_b)

            @pl.when(j + 2 < n_chunks)
            def _():
                load(j + 2, rows_a).wait()

    return scatter_rows(hp_all, slot_of)


def _experts_kernel(te_ref, used_ref, xs_ref, wg_ref, wu_ref, wd_ref, ys_ref, wgu_bf, wd_bf):
    i = pl.program_id(0)

    @pl.when(i >= used_ref[0])
    def _():
        ys_ref[...] = jnp.zeros_like(ys_ref)

    @pl.when(i < used_ref[0])
    def _():
        @pl.when((i == 0) | (te_ref[i] != te_ref[jnp.maximum(i, 1) - 1]))
        def _():
            wgu_bf[:, 0:D_EXPERT] = wg_ref[0].astype(BF16)
            wgu_bf[:, D_EXPERT:2 * D_EXPERT] = wu_ref[0].astype(BF16)
            wd_bf[...] = wd_ref[0].astype(BF16)

        left, right = _unpack_rows(xs_ref[...])
        x = jnp.concatenate([left, right], axis=1).astype(BF16)
        gu = _dot(x, wgu_bf[...])
        g = gu[:, 0:D_EXPERT]
        u = gu[:, D_EXPERT:2 * D_EXPERT]
        act = (g * _sigmoid(g)) * u
        ys_ref[...] = _pack_rows(_dot(act.astype(BF16), wd_bf[...]))


def _experts(xs, tile_expert, tiles_used, w_gate, w_up, w_down):
    n_slots, width = xs.shape
    d = 2 * width
    ts = MOE_TS

    def x_map(i, te, used):
        return (jnp.minimum(i, used[0] - 1), 0)

    return pl.pallas_call(
        _experts_kernel,
        out_shape=jax.ShapeDtypeStruct((n_slots, width), jnp.int32),
        grid_spec=pltpu.PrefetchScalarGridSpec(
            num_scalar_prefetch=2,
            grid=(n_slots // ts,),
            in_specs=[pl.BlockSpec((ts, width), x_map),
                      pl.BlockSpec((1, d, D_EXPERT), lambda i, te, used: (te[i], 0, 0)),
                      pl.BlockSpec((1, d, D_EXPERT), lambda i, te, used: (te[i], 0, 0)),
                      pl.BlockSpec((1, D_EXPERT, d), lambda i, te, used: (te[i], 0, 0))],
            out_specs=pl.BlockSpec((ts, width), lambda i, te, used: (i, 0)),
            scratch_shapes=[pltpu.VMEM((d, 2 * D_EXPERT), BF16), pltpu.VMEM((D_EXPERT, d), BF16)]),
        compiler_params=_cparams(("arbitrary",)),
        name="moe_experts",
    )(tile_expert, tiles_used, xs, w_gate, w_up, w_down)


def _gather_slots(y_slots, slot_of, t):
    width = y_slots.shape[1]
    workers = SC_CORES * SC_SUBCORES
    per_worker = t // workers
    n_blocks = (per_worker // SC_ROWS) * TOP_K
    assert per_worker * workers == t and per_worker % SC_ROWS == 0 and n_blocks % 2 == 0
    mesh = plsc.VectorSubcoreMesh(core_axis_name="core", subcore_axis_name="subcore")

    @functools.partial(
        pl.kernel, mesh=mesh,
        out_type=jax.ShapeDtypeStruct((TOP_K * t, width), jnp.int32),
        scratch_types=[pltpu.VMEM((SC_ROWS,), jnp.int32), pltpu.VMEM((SC_ROWS,), jnp.int32),
                       pltpu.VMEM((SC_ROWS, width), jnp.int32), pltpu.VMEM((SC_ROWS, width), jnp.int32),
                       pltpu.SemaphoreType.DMA, pltpu.SemaphoreType.DMA],
    )
    def gather_rows(ys_hbm, slot_hbm, out_hbm, idx0, idx1, rows0, rows1, sem0, sem1):
        worker = lax.axis_index("subcore") * SC_CORES + lax.axis_index("core")
        base = worker * per_worker

        def first_row(n):
            return (n % TOP_K) * t + base + (n // TOP_K) * SC_ROWS

        def start(n, idx_v, rows_v, sem):
            pltpu.sync_copy(slot_hbm.at[pl.ds(first_row(n), SC_ROWS)], idx_v)
            pltpu.async_copy(ys_hbm.at[idx_v], rows_v, sem)

        def finish(n, idx_v, rows_v, sem):
            pltpu.make_async_copy(ys_hbm.at[idx_v], rows_v, sem).wait()
            pltpu.sync_copy(rows_v, out_hbm.at[pl.ds(first_row(n), SC_ROWS)])

        start(0, idx0, rows0, sem0)

        @pl.loop(0, n_blocks, step=2)
        def _(n):
            start(n + 1, idx1, rows1, sem1)
            finish(n, idx0, rows0, sem0)

            @pl.when(n + 2 < n_blocks)
            def _():
                start(n + 2, idx0, rows0, sem0)

            finish(n + 1, idx1, rows1, sem1)

    return gather_rows(y_slots, slot_of)


def _combine_kernel(y_ref, h_ref, g2_ref, gtok_ref, rows_ref, wgs_ref, wus_ref, wds_ref, o_ref):
    h_left, h_right = _unpack_rows(h_ref[...])
    h = jnp.concatenate([h_left, h_right], axis=1).astype(BF16)
    gs = _dot(h, wgs_ref[...])
    us = _dot(h, wus_ref[...])
    shared = _dot(((gs * _sigmoid(gs)) * us).astype(BF16), wds_ref[...])

    gtok = gtok_ref[...]
    acc_left = acc_right = None
    for k in range(TOP_K):
        left, right = _unpack_rows(rows_ref[k])
        gate = gtok[:, k:k + 1]
        acc_left = gate * left if acc_left is None else acc_left + gate * left
        acc_right = gate * right if acc_right is None else acc_right + gate * right
    routed = jnp.concatenate([acc_left, acc_right], axis=1)
    o_ref[...] = y_ref[...] + g2_ref[0] * (routed + shared)


def _combine(y_all, hp_all, gate2, gtok, rows, wgs, wus, wds, *, first_token, tokens, seq):
    d = y_all.shape[1]
    width = hp_all.shape[1]
    tm = MOE_ROW_TM
    tile0 = first_token // tm
    nb = gate2.shape[0]
    tiles_per_batch = seq // tm

    def mod_map(i):
        return ((i // tiles_per_batch) if nb > 1 else 0, 0, 0)

    return pl.pallas_call(
        _combine_kernel,
        out_shape=jax.ShapeDtypeStruct((tokens, d), F32),
        grid=(tokens // tm,),
        in_specs=[pl.BlockSpec((tm, d), lambda i: (tile0 + i, 0)),
                  pl.BlockSpec((tm, width), lambda i: (tile0 + i, 0)),
                  pl.BlockSpec((1, 1, d), mod_map),
                  pl.BlockSpec((tm, LANES), lambda i: (tile0 + i, 0)),
                  pl.BlockSpec((TOP_K, tm, width), lambda i: (0, tile0 + i, 0)),
                  pl.BlockSpec((d, D_SHARED), lambda i: (0, 0)),
                  pl.BlockSpec((d, D_SHARED), lambda i: (0, 0)),
                  pl.BlockSpec((D_SHARED, d), lambda i: (0, 0))],
        out_specs=pl.BlockSpec((tm, d), lambda i: (i, 0)),
        compiler_params=_cparams(("arbitrary",)),
        name="moe_combine",
    )(y_all, hp_all, gate2, gtok, rows, wgs, wus, wds)


def _expert_layout(counts, n_tiles):
    cnt = counts.astype(jnp.int32)
    tiles = (cnt + (MOE_TS - 1)) // MOE_TS
    last_tile = jnp.cumsum(tiles)
    off = (last_tile - tiles) * MOE_TS
    pad_lo = off + cnt
    pad_hi = (off + tiles * MOE_TS).at[N_EXPERTS - 1].set(n_tiles * MOE_TS)
    pad_cnt = pad_hi - pad_lo
    pad_last = jnp.cumsum(pad_cnt)
    shift = pad_lo - (pad_last - pad_cnt)
    j = jnp.arange(N_EXPERTS * MOE_TS, dtype=jnp.int32)
    past = (pad_last[None, :-1] <= j[:, None]).astype(jnp.int32)
    pad_slots = j + shift[0] + jnp.sum(past * (shift[1:] - shift[:-1])[None, :], axis=1)
    tile_ids = jnp.arange(n_tiles, dtype=jnp.int32)
    tile_expert = jnp.minimum(
        jnp.sum((last_tile[None, :] <= tile_ids[:, None]).astype(jnp.int32), axis=1), N_EXPERTS - 1)
    return off, pad_slots, tile_expert, last_tile[-1:].astype(jnp.int32)


def _rope_tables(n_tokens):
    t = jnp.arange(n_tokens)
    row = (t // GRID_W).astype(F32)
    col = (t % GRID_W).astype(F32)
    nf = HEAD_DIM // 4
    freqs = ROPE_THETA ** (-jnp.arange(nf, dtype=F32) / nf)
    ang_r = row[:, None] * freqs
    ang_c = col[:, None] * freqs
    cos = jnp.concatenate([jnp.cos(ang_r)] * 2 + [jnp.cos(ang_c)] * 2, axis=1)
    sin = jnp.concatenate([-jnp.sin(ang_r), jnp.sin(ang_r), -jnp.sin(ang_c), jnp.sin(ang_c)], axis=1)
    reps = LANES // HEAD_DIM
    return jnp.tile(cos, (1, reps)), jnp.tile(sin, (1, reps))


def _head_gains(qn_a, kn_a, qn_b, kn_b):
    ones = jnp.ones((HEAD_DIM,), F32)
    parts = ([qn_a] * N_HEADS_A + [kn_a] * N_KV_A + [ones] * N_KV_A
             + [qn_b] * N_HEADS_B + [kn_b] * N_HEADS_B + [ones] * N_HEADS_B)
    return jnp.concatenate(parts).reshape(1, IN_COLS).astype(F32)


def _same_head_indicator():
    i = np.arange(MXU_DIM)
    return jnp.asarray((i[:, None] // HEAD_DIM) == (i[None, :] // HEAD_DIM), BF16)


def _token_major(cache):
    b, h, s, hd = cache.shape
    return cache.transpose(0, 2, 1, 3).reshape(b, s, h * hd).astype(BF16)


def kernel(x_prompt, x_sample, cache_k_a, cache_v_a, cache_k_b, cache_v_b, c, c_ctx, w_mod, b_mod, norm1, norm2, w_in, qn_a, kn_a, qn_b, kn_b, rpb, on_a, on_b, w_out, w_router, router_bias, w_gate_e, w_up_e, w_down_e, w_gate_s, w_up_s, w_down_s):
    depth = w_mod.shape[0]
    assert depth == 1
    l = 0
    bp, sp, d = x_prompt.shape
    bs, ss, _ = x_sample.shape

    cvec = jnp.concatenate([c_ctx[None, :], c], axis=0)
    rows = -(-cvec.shape[0] // 8) * 8
    cvec = jnp.pad(cvec, ((0, rows - cvec.shape[0]), (0, 0)))
    mod = _adaln(cvec, w_mod[l], b_mod[l])
    mod_p = [m.reshape(1, 1, d) for m in jnp.split(mod[0:1], 6, axis=-1)]
    mod_s = [m.reshape(bs, 1, d) for m in jnp.split(mod[1:1 + bs], 6, axis=-1)]
    mod_all = [m.reshape(1 + bs, 1, d) for m in jnp.split(mod[0:1 + bs], 6, axis=-1)]

    w_in_bf = w_in[l].astype(BF16)
    w_out_bf = w_out[l].astype(BF16)
    gain = _head_gains(qn_a[l], kn_a[l], qn_b[l], kn_b[l])
    seg = _same_head_indicator()
    n1 = norm1[l].reshape(1, d)
    n2 = norm2[l].reshape(1, d)
    ona = on_a[l].reshape(1, WIDTH_A)
    onb = on_b[l].reshape(1, WIDTH_B)
    wr_t = w_router[l].T
    wr_hi = wr_t.astype(BF16)
    wr_lo = (wr_t - wr_hi.astype(F32)).astype(BF16)
    rbias = router_bias[l].reshape(N_EXPERTS, 1).astype(F32)
    wgs = w_gate_s[l].astype(BF16)
    wus = w_up_s[l].astype(BF16)
    wds = w_down_s[l].astype(BF16)
    t_p = bp * sp
    t_s = bs * ss
    t_all = t_p + t_s

    xp = x_prompt.reshape(t_p, d)
    proj_p, st_ka, st_va, st_kb, st_vb = _project(
        xp, mod_p[0], mod_p[1], n1, w_in_bf, gain, seg, None, tm=sp, seq=sp, states=True)
    oa_p, ob_p = _context_attention(proj_p, seq=sp)

    xs = x_sample.reshape(t_s, d)
    proj_s, = _project(xs, mod_s[0], mod_s[1], n1, w_in_bf, gain, seg, _rope_tables(ss),
                       tm=1024, seq=ss, states=False)
    oa_s = _latent_gqa(proj_s, _token_major(cache_k_a[:, l]), _token_major(cache_v_a[:, l]),
                       _query_norm_bound(qn_a[l]), seq=ss, tq=256)
    bias_t = _neighbourhood_bias(rpb[l], ss // GRID_W)
    ob_s = _latent_neighbourhood(proj_s, _token_major(cache_k_b[:, l]), _token_major(cache_v_b[:, l]),
                                 bias_t, _neighbourhood_bounds(qn_b[l], kn_b[l], rpb[l]), seq=ss)

    y1_all, hp_all, gates_t, rank_t, counts = _merge(
        (xp, oa_p, ob_p), (xs, oa_s, ob_s), ona, onb, w_out_bf, mod_all[2], mod_all[3], mod_all[4], n2,
        wr_hi, wr_lo, rbias, tm=512, lat_seq=ss)
    n_tiles = t_all * TOP_K // MOE_TS + N_EXPERTS
    off, pad_slots, tile_expert, tiles_used = _expert_layout(counts[:, 0], n_tiles)
    pos, gtok = _slots(gates_t, rank_t, off.astype(F32).reshape(N_EXPERTS, 1))
    slot_of = pos.reshape(TOP_K * t_all)
    x_slots = _dispatch(hp_all, jnp.concatenate([slot_of, pad_slots]))
    y_slots = _experts(x_slots, tile_expert, tiles_used, w_gate_e[l], w_up_e[l], w_down_e[l])
    rows = _gather_slots(y_slots, slot_of, t_all).reshape(TOP_K, t_all, d // 2)
    y_p = _combine(y1_all, hp_all, mod_p[5], gtok, rows, wgs, wus, wds,
                   first_token=0, tokens=t_p, seq=sp)
    y_s = _combine(y1_all, hp_all, mod_s[5], gtok, rows, wgs, wus, wds,
                   first_token=t_p, tokens=t_s, seq=ss)

    return (y_p.reshape(bp, sp, d), y_s.reshape(bs, ss, d), st_ka, st_va, st_kb, st_vb)
```
